```python
import jax, jax.numpy as jnp
from jax import lax
import numpy as np

D_MODEL = 1024
BATCH = 8
SEQ = 8192
DEPTH = 2

GRID_W = 64
CTX_LEN = 256
HEAD_DIM = 128
N_Q_HEADS = 4
N_KV_HEADS = 2
ATTN_WIDTH = N_Q_HEADS * HEAD_DIM
KV_WIDTH = N_KV_HEADS * HEAD_DIM
CONV_WIDTH = D_MODEL - ATTN_WIDTH
SHORT_CONV_K = 3
IN_PROJ_WIDTH = ATTN_WIDTH + 2 * KV_WIDTH + 3 * CONV_WIDTH
PROJ_SPLITS = (ATTN_WIDTH,
               ATTN_WIDTH + KV_WIDTH,
               ATTN_WIDTH + 2 * KV_WIDTH,
               ATTN_WIDTH + 2 * KV_WIDTH + CONV_WIDTH,
               ATTN_WIDTH + 2 * KV_WIDTH + 2 * CONV_WIDTH)
Q_BLOCK = 128
ROPE_THETA = 10000.0
POOL_WINDOWS = (2, 4, 8, 16)
N_POOL_GROUPS = len(POOL_WINDOWS)
POOL_GROUP = D_MODEL // N_POOL_GROUPS
D_FF = 2816
FFN_CONV_K = 3
N_MOD = 6
EPS = 1e-6
N_EVEN = (DEPTH + 1) // 2
N_ODD = DEPTH // 2

kernel_name = "hybrid_gqa_shortconv_pool_convglu_dit"


def rms_norm(x, gain):
    xf = x.astype(jnp.float32)
    y = xf * lax.rsqrt(jnp.mean(xf * xf, axis=-1, keepdims=True) + EPS)
    return (y * gain.astype(jnp.float32)).astype(x.dtype)


def modulate(h, shift, scale):
    return h * (1.0 + scale) + shift


def dwconv_centred(h, w):
    k = w.shape[0]
    p = k // 2
    n = h.shape[1]
    hp = jnp.pad(h, ((0, 0), (p, p), (0, 0)))
    return sum(hp[:, j:j + n] * w[j] for j in range(k))


def axial_rope_tables(rows):
    row_ids = jnp.repeat(jnp.arange(rows), GRID_W).astype(jnp.float32)
    col_ids = jnp.tile(jnp.arange(GRID_W), rows).astype(jnp.float32)
    axis_dim = HEAD_DIM // 2
    inv_freq = jnp.power(ROPE_THETA, -jnp.arange(0, axis_dim, 2, dtype=jnp.float32) / axis_dim)
    ang = jnp.stack([row_ids[:, None] * inv_freq, col_ids[:, None] * inv_freq], axis=1)
    return jnp.cos(ang), jnp.sin(ang)


def apply_axial_rope(x, cos, sin):
    b, n, h, d = x.shape
    xr = x.reshape(b, n, h, 2, 2, d // 4)
    x1, x2 = xr[..., 0, :], xr[..., 1, :]
    c = cos[None, :, None].astype(x.dtype)
    s = sin[None, :, None].astype(x.dtype)
    out = jnp.stack([x1 * c - x2 * s, x2 * c + x1 * s], axis=-2)
    return out.reshape(b, n, h, d)


def gqa_softmax(qg, k, v):
    scores = jnp.einsum('bqkgd,btkd->bkgqt', qg, k).astype(jnp.float32) * (HEAD_DIM ** -0.5)
    p = jax.nn.softmax(scores, axis=-1).astype(v.dtype)
    return jnp.einsum('bkgqt,btkd->bqkgd', p, v)


def latent_attention(q, k_lat, v_lat, k_ctx, v_ctx):
    b, n, hq, hd = q.shape
    g = hq // N_KV_HEADS
    k_all = jnp.concatenate([k_ctx, k_lat], axis=1)
    v_all = jnp.concatenate([v_ctx, v_lat], axis=1)
    nblk = n // Q_BLOCK
    qb = q.reshape(b, nblk, Q_BLOCK, N_KV_HEADS, g, hd).transpose(1, 0, 2, 3, 4, 5)
    out = lax.map(lambda qblk: gqa_softmax(qblk, k_all, v_all), qb)
    return out.transpose(1, 0, 2, 3, 4, 5).reshape(b, n, hq * hd)


def context_attention(q, k, v):
    b, n, hq, hd = q.shape
    qg = q.reshape(b, n, N_KV_HEADS, hq // N_KV_HEADS, hd)
    return gqa_softmax(qg, k, v).reshape(b, n, hq * hd)


def split_projection(p, q_gain, k_gain):
    b, n, _ = p.shape
    q, k, v, gate_b, gate_c, x_in = jnp.split(p, PROJ_SPLITS, axis=-1)
    q = rms_norm(q.reshape(b, n, N_Q_HEADS, HEAD_DIM), q_gain)
    k = rms_norm(k.reshape(b, n, N_KV_HEADS, HEAD_DIM), k_gain)
    v = v.reshape(b, n, N_KV_HEADS, HEAD_DIM)
    return q, k, v, gate_b, gate_c, x_in


def even_mixer(a_lat, a_ctx, cos, sin, w_in, q_gain, k_gain, conv_w, w_out, with_ctx_out):
    b, lc, _ = a_ctx.shape
    if with_ctx_out:
        cq, ck, cv, cgb, cgc, cxin = split_projection(a_ctx @ w_in, q_gain, k_gain)
    else:
        ckv = a_ctx @ w_in[:, ATTN_WIDTH:ATTN_WIDTH + 2 * KV_WIDTH]
        ck, cv = jnp.split(ckv, 2, axis=-1)
        ck = rms_norm(ck.reshape(b, lc, N_KV_HEADS, HEAD_DIM), k_gain)
        cv = cv.reshape(b, lc, N_KV_HEADS, HEAD_DIM)
    q, k, v, gate_b, gate_c, x_in = split_projection(a_lat @ w_in, q_gain, k_gain)
    q = apply_axial_rope(q, cos, sin)
    k = apply_axial_rope(k, cos, sin)
    attn = latent_attention(q, k, v, ck, cv)
    conv = gate_b * dwconv_centred(gate_c * x_in, conv_w)
    y_lat = jnp.concatenate([attn, conv], axis=-1) @ w_out
    y_ctx = None
    if with_ctx_out:
        c_attn = context_attention(cq, ck, cv)
        c_conv = cgb * dwconv_centred(cgc * cxin, conv_w)
        y_ctx = jnp.concatenate([c_attn, c_conv], axis=-1) @ w_out
    return y_lat, y_ctx


def pool_mixer(h, pool_w, pool_scale):
    b, n, d = h.shape
    hf = h.astype(jnp.float32)
    cs = jnp.pad(jnp.cumsum(hf, axis=1), ((0, 0), (1, 0), (0, 0)))
    t = jnp.arange(n)
    groups = []
    for gi, w in enumerate(POOL_WINDOWS):
        lo = jnp.clip(t - w // 2, 0, n)
        hi = jnp.clip(t + w - w // 2, 0, n)
        sl = slice(gi * POOL_GROUP, (gi + 1) * POOL_GROUP)
        csg = cs[:, :, sl]
        mean = (csg[:, hi] - csg[:, lo]) / (hi - lo).astype(jnp.float32)[None, :, None]
        groups.append(mean - hf[:, :, sl])
    pooled = jnp.stack(groups, axis=2).astype(h.dtype)
    mixed = jnp.einsum('bngc,gce->bnge', pooled, pool_w).reshape(b, n, d)
    return mixed * pool_scale


def conv_glu(h, w_up, conv_w, conv_b, w_down):
    gate, val = jnp.split(h @ w_up, 2, axis=-1)
    gate = dwconv_centred(gate, conv_w) + conv_b
    return (jax.nn.silu(gate) * val) @ w_down


def _fwd_setup_inputs(seed: int = 0) -> dict:
    key = jax.random.key(seed)
    ks = jax.random.split(key, 24)
    f32 = jnp.float32
    nrm = lambda k, shape, s: jax.random.normal(k, shape, f32) * s
    d = D_MODEL
    return {
        "x": nrm(ks[0], (BATCH, SEQ, d), 1.0),
        "c": nrm(ks[1], (BATCH, d), 1.0),
        "ctx": nrm(ks[2], (BATCH, CTX_LEN, d), 1.0),
        "c_ctx": nrm(ks[3], (d,), 1.0),
        "ada_w": nrm(ks[4], (DEPTH, d, N_MOD * d), 0.5 * d ** -0.5),
        "ada_b": nrm(ks[5], (DEPTH, N_MOD * d), 0.02),
        "mix_norm": 1.0 + nrm(ks[6], (DEPTH, d), 0.02),
        "ffn_norm": 1.0 + nrm(ks[7], (DEPTH, d), 0.02),
        "even_w_in": nrm(ks[8], (N_EVEN, d, IN_PROJ_WIDTH), d ** -0.5),
        "even_q_gain": 1.0 + nrm(ks[9], (N_EVEN, HEAD_DIM), 0.02),
        "even_k_gain": 1.0 + nrm(ks[10], (N_EVEN, HEAD_DIM), 0.02),
        "even_conv_w": nrm(ks[11], (N_EVEN, SHORT_CONV_K, CONV_WIDTH), SHORT_CONV_K ** -0.5),
        "even_w_out": nrm(ks[12], (N_EVEN, ATTN_WIDTH + CONV_WIDTH, d), (ATTN_WIDTH + CONV_WIDTH) ** -0.5),
        "odd_pool_w": nrm(ks[13], (N_ODD, N_POOL_GROUPS, POOL_GROUP, POOL_GROUP), POOL_GROUP ** -0.5),
        "odd_pool_scale": 1.0 + nrm(ks[14], (N_ODD, d), 0.02),
        "ffn_w_up": nrm(ks[15], (DEPTH, d, 2 * D_FF), d ** -0.5),
        "ffn_conv_w": nrm(ks[16], (DEPTH, FFN_CONV_K, D_FF), FFN_CONV_K ** -0.5),
        "ffn_conv_b": nrm(ks[17], (DEPTH, D_FF), 0.02),
        "ffn_w_down": nrm(ks[18], (DEPTH, D_FF, d), D_FF ** -0.5),
    }


def _fwd_reference(x, c, ctx, c_ctx, ada_w, ada_b, mix_norm, ffn_norm, even_w_in, even_q_gain, even_k_gain,
              even_conv_w, even_w_out, odd_pool_w, odd_pool_scale, ffn_w_up, ffn_conv_w, ffn_conv_b, ffn_w_down):
    b, n, d = x.shape
    rows = n // GRID_W
    cos, sin = axial_rope_tables(rows)
    silu_c = jax.nn.silu(c)
    silu_cc = jax.nn.silu(c_ctx)
    h_ctx = ctx
    for l in range(DEPTH):
        ctx_later = any(j % 2 == 0 for j in range(l + 1, DEPTH))
        need_ctx = (l % 2 == 0) or ctx_later
        mod_lat = (silu_c @ ada_w[l] + ada_b[l])[:, None, :]
        sh1, sc1, g1, sh2, sc2, g2 = jnp.split(mod_lat, N_MOD, axis=-1)
        a_lat = modulate(rms_norm(x, mix_norm[l]), sh1, sc1)
        if need_ctx:
            mod_ctx = (silu_cc @ ada_w[l] + ada_b[l])[None, None, :]
            csh1, csc1, cg1, csh2, csc2, cg2 = jnp.split(mod_ctx, N_MOD, axis=-1)
            a_ctx = modulate(rms_norm(h_ctx, mix_norm[l]), csh1, csc1)
        if l % 2 == 0:
            e = l // 2
            y_lat, y_ctx = even_mixer(a_lat, a_ctx, cos, sin, even_w_in[e], even_q_gain[e], even_k_gain[e],
                                      even_conv_w[e], even_w_out[e], ctx_later)
        else:
            o = l // 2
            y_lat = pool_mixer(a_lat, odd_pool_w[o], odd_pool_scale[o])
            y_ctx = pool_mixer(a_ctx, odd_pool_w[o], odd_pool_scale[o]) if ctx_later else None
        x = x + g1 * y_lat
        f_lat = modulate(rms_norm(x, ffn_norm[l]), sh2, sc2)
        x = x + g2 * conv_glu(f_lat, ffn_w_up[l], ffn_conv_w[l], ffn_conv_b[l], ffn_w_down[l])
        if ctx_later:
            h_ctx = h_ctx + cg1 * y_ctx
            f_ctx = modulate(rms_norm(h_ctx, ffn_norm[l]), csh2, csc2)
            h_ctx = h_ctx + cg2 * conv_glu(f_ctx, ffn_w_up[l], ffn_conv_w[l], ffn_conv_b[l], ffn_w_down[l])
    return x


import jax as _jax
import jax.numpy as _jnp

TWIN_FORMAT = 'train_step'
FWD_PARAMS = ['x', 'c', 'ctx', 'c_ctx', 'ada_w', 'ada_b', 'mix_norm', 'ffn_norm', 'even_w_in', 'even_q_gain', 'even_k_gain', 'even_conv_w', 'even_w_out', 'odd_pool_w', 'odd_pool_scale', 'ffn_w_up', 'ffn_conv_w', 'ffn_conv_b', 'ffn_w_down']
TWIN_WEIGHTS = ['c_ctx', 'ada_w', 'ada_b', 'mix_norm', 'ffn_norm', 'even_w_in', 'even_q_gain', 'even_k_gain', 'even_conv_w', 'even_w_out', 'odd_pool_w', 'odd_pool_scale', 'ffn_w_up', 'ffn_conv_w', 'ffn_conv_b', 'ffn_w_down']
TWIN_DIFF_INPUT = 'x'
TWIN_INPUTS = ['x', 'c', 'ctx', 'c_ctx', 'ada_w', 'ada_b', 'mix_norm', 'ffn_norm', 'even_w_in', 'even_q_gain', 'even_k_gain', 'even_conv_w', 'even_w_out', 'odd_pool_w', 'odd_pool_scale', 'ffn_w_up', 'ffn_conv_w', 'ffn_conv_b', 'ffn_w_down', 'loss_target', 'm_c_ctx', 'm_ada_w', 'm_ada_b', 'm_mix_norm', 'm_ffn_norm', 'm_even_w_in', 'm_even_q_gain', 'm_even_k_gain', 'm_even_conv_w', 'm_even_w_out', 'm_odd_pool_w', 'm_odd_pool_scale', 'm_ffn_w_up', 'm_ffn_conv_w', 'm_ffn_conv_b', 'm_ffn_w_down', 'v_c_ctx', 'v_ada_w', 'v_ada_b', 'v_mix_norm', 'v_ffn_norm', 'v_even_w_in', 'v_even_q_gain', 'v_even_k_gain', 'v_even_conv_w', 'v_even_w_out', 'v_odd_pool_w', 'v_odd_pool_scale', 'v_ffn_w_up', 'v_ffn_conv_w', 'v_ffn_conv_b', 'v_ffn_w_down']
TWIN_OUTPUTS = ['loss', 'grad_x', 'grad_c_ctx', 'grad_ada_w', 'grad_ada_b', 'grad_mix_norm', 'grad_ffn_norm', 'grad_even_w_in', 'grad_even_q_gain', 'grad_even_k_gain', 'grad_even_conv_w', 'grad_even_w_out', 'grad_odd_pool_w', 'grad_odd_pool_scale', 'grad_ffn_w_up', 'grad_ffn_conv_w', 'grad_ffn_conv_b', 'grad_ffn_w_down', 'delta_c_ctx', 'delta_ada_w', 'delta_ada_b', 'delta_mix_norm', 'delta_ffn_norm', 'delta_even_w_in', 'delta_even_q_gain', 'delta_even_k_gain', 'delta_even_conv_w', 'delta_even_w_out', 'delta_odd_pool_w', 'delta_odd_pool_scale', 'delta_ffn_w_up', 'delta_ffn_conv_w', 'delta_ffn_conv_b', 'delta_ffn_w_down', 'new_m_c_ctx', 'new_m_ada_w', 'new_m_ada_b', 'new_m_mix_norm', 'new_m_ffn_norm', 'new_m_even_w_in', 'new_m_even_q_gain', 'new_m_even_k_gain', 'new_m_even_conv_w', 'new_m_even_w_out', 'new_m_odd_pool_w', 'new_m_odd_pool_scale', 'new_m_ffn_w_up', 'new_m_ffn_conv_w', 'new_m_ffn_conv_b', 'new_m_ffn_w_down', 'new_v_c_ctx', 'new_v_ada_w', 'new_v_ada_b', 'new_v_mix_norm', 'new_v_ffn_norm', 'new_v_even_w_in', 'new_v_even_q_gain', 'new_v_even_k_gain', 'new_v_even_conv_w', 'new_v_even_w_out', 'new_v_odd_pool_w', 'new_v_odd_pool_scale', 'new_v_ffn_w_up', 'new_v_ffn_conv_w', 'new_v_ffn_conv_b', 'new_v_ffn_w_down']
TWIN_LEAF_KINDS = {'loss': 'loss', 'grad_x': 'grad_x', 'grad_c_ctx': 'grad_w', 'grad_ada_w': 'grad_w', 'grad_ada_b': 'grad_w', 'grad_mix_norm': 'grad_w', 'grad_ffn_norm': 'grad_w', 'grad_even_w_in': 'grad_w', 'grad_even_q_gain': 'grad_w', 'grad_even_k_gain': 'grad_w', 'grad_even_conv_w': 'grad_w', 'grad_even_w_out': 'grad_w', 'grad_odd_pool_w': 'grad_w', 'grad_odd_pool_scale': 'grad_w', 'grad_ffn_w_up': 'grad_w', 'grad_ffn_conv_w': 'grad_w', 'grad_ffn_conv_b': 'grad_w', 'grad_ffn_w_down': 'grad_w', 'delta_c_ctx': 'delta_w', 'delta_ada_w': 'delta_w', 'delta_ada_b': 'delta_w', 'delta_mix_norm': 'delta_w', 'delta_ffn_norm': 'delta_w', 'delta_even_w_in': 'delta_w', 'delta_even_q_gain': 'delta_w', 'delta_even_k_gain': 'delta_w', 'delta_even_conv_w': 'delta_w', 'delta_even_w_out': 'delta_w', 'delta_odd_pool_w': 'delta_w', 'delta_odd_pool_scale': 'delta_w', 'delta_ffn_w_up': 'delta_w', 'delta_ffn_conv_w': 'delta_w', 'delta_ffn_conv_b': 'delta_w', 'delta_ffn_w_down': 'delta_w', 'new_m_c_ctx': 'new_m', 'new_m_ada_w': 'new_m', 'new_m_ada_b': 'new_m', 'new_m_mix_norm': 'new_m', 'new_m_ffn_norm': 'new_m', 'new_m_even_w_in': 'new_m', 'new_m_even_q_gain': 'new_m', 'new_m_even_k_gain': 'new_m', 'new_m_even_conv_w': 'new_m', 'new_m_even_w_out': 'new_m', 'new_m_odd_pool_w': 'new_m', 'new_m_odd_pool_scale': 'new_m', 'new_m_ffn_w_up': 'new_m', 'new_m_ffn_conv_w': 'new_m', 'new_m_ffn_conv_b': 'new_m', 'new_m_ffn_w_down': 'new_m', 'new_v_c_ctx': 'new_v', 'new_v_ada_w': 'new_v', 'new_v_ada_b': 'new_v', 'new_v_mix_norm': 'new_v', 'new_v_ffn_norm': 'new_v', 'new_v_even_w_in': 'new_v', 'new_v_even_q_gain': 'new_v', 'new_v_even_k_gain': 'new_v', 'new_v_even_conv_w': 'new_v', 'new_v_even_w_out': 'new_v', 'new_v_odd_pool_w': 'new_v', 'new_v_odd_pool_scale': 'new_v', 'new_v_ffn_w_up': 'new_v', 'new_v_ffn_conv_w': 'new_v', 'new_v_ffn_conv_b': 'new_v', 'new_v_ffn_w_down': 'new_v'}


def _forward(args):
    return _fwd_reference(*[args[k] for k in FWD_PARAMS])


def _output_shape():
    def fwd():
        inp = _fwd_setup_inputs(0)
        return _fwd_reference(*[inp[k] for k in FWD_PARAMS])
    out = _jax.eval_shape(fwd)
    return out.shape, out.dtype

N_MICROBATCH = 1
ADAM_LR = 0.001
ADAM_B1 = 0.9
ADAM_B2 = 0.999
ADAM_EPS = 1e-08
ADAM_WD = 0.01
ADAM_STEP = 10
PER_EXAMPLE_BATCH_AXIS = {'x': 0, 'c': 0, 'ctx': 0, 'loss_target': 0}
SHARED_INPUTS = []
_WEIGHT_DTYPES = {'c_ctx': _jnp.float32, 'ada_w': _jnp.float32, 'ada_b': _jnp.float32, 'mix_norm': _jnp.float32, 'ffn_norm': _jnp.float32, 'even_w_in': _jnp.float32, 'even_q_gain': _jnp.float32, 'even_k_gain': _jnp.float32, 'even_conv_w': _jnp.float32, 'even_w_out': _jnp.float32, 'odd_pool_w': _jnp.float32, 'odd_pool_scale': _jnp.float32, 'ffn_w_up': _jnp.float32, 'ffn_conv_w': _jnp.float32, 'ffn_conv_b': _jnp.float32, 'ffn_w_down': _jnp.float32}
MOMENT_SCALE = {'c_ctx': 4.241620e-02, 'ada_w': 2.196349e+00, 'ada_b': 5.965226e+00, 'mix_norm': 1.002058e+01, 'ffn_norm': 6.580184e+00, 'even_w_in': 3.059202e-01, 'even_q_gain': 3.422844e-02, 'even_k_gain': 3.489876e-02, 'even_conv_w': 5.371993e+00, 'even_w_out': 2.479609e-01, 'odd_pool_w': 4.739599e-01, 'odd_pool_scale': 5.490359e+00, 'ffn_w_up': 1.178453e-01, 'ffn_conv_w': 7.661019e-01, 'ffn_conv_b': 8.562995e-01, 'ffn_w_down': 1.241642e-01}


def _to_microbatches(a, axis):
    t = _jnp.moveaxis(a, axis, 0)
    t = t.reshape((N_MICROBATCH, t.shape[0] // N_MICROBATCH) + t.shape[1:])
    return _jnp.moveaxis(t, 1, axis + 1)


def setup_inputs(seed: int = 0) -> dict:
    inp = _fwd_setup_inputs(seed)
    key = _jax.random.fold_in(_jax.random.key(seed), 7919)
    shape, _ = _output_shape()
    out = dict(inp)
    out["loss_target"] = _jax.random.normal(_jax.random.fold_in(key, 0), shape, _jnp.float32)
    for i, name in enumerate(TWIN_WEIGHTS):
        w = inp[name].astype(_jnp.float32)
        if MOMENT_SCALE is None:
            s = _jnp.sqrt(_jnp.mean(_jnp.square(w)) + 1e-30)
        else:
            s = MOMENT_SCALE[name]
        km, kv = _jax.random.split(_jax.random.fold_in(key, i + 1))
        out[name] = w
        out["m_" + name] = s * _jax.random.normal(km, w.shape, _jnp.float32)
        out["v_" + name] = (s * s) * _jax.random.uniform(kv, w.shape, _jnp.float32, 0.5, 1.5)
    if N_MICROBATCH > 1:
        for name, axis in PER_EXAMPLE_BATCH_AXIS.items():
            out[name] = _to_microbatches(out[name], axis)
    return {'x': out['x'], 'c': out['c'], 'ctx': out['ctx'], 'c_ctx': out['c_ctx'], 'ada_w': out['ada_w'], 'ada_b': out['ada_b'], 'mix_norm': out['mix_norm'], 'ffn_norm': out['ffn_norm'], 'even_w_in': out['even_w_in'], 'even_q_gain': out['even_q_gain'], 'even_k_gain': out['even_k_gain'], 'even_conv_w': out['even_conv_w'], 'even_w_out': out['even_w_out'], 'odd_pool_w': out['odd_pool_w'], 'odd_pool_scale': out['odd_pool_scale'], 'ffn_w_up': out['ffn_w_up'], 'ffn_conv_w': out['ffn_conv_w'], 'ffn_conv_b': out['ffn_conv_b'], 'ffn_w_down': out['ffn_w_down'], 'loss_target': out['loss_target'], 'm_c_ctx': out['m_c_ctx'], 'm_ada_w': out['m_ada_w'], 'm_ada_b': out['m_ada_b'], 'm_mix_norm': out['m_mix_norm'], 'm_ffn_norm': out['m_ffn_norm'], 'm_even_w_in': out['m_even_w_in'], 'm_even_q_gain': out['m_even_q_gain'], 'm_even_k_gain': out['m_even_k_gain'], 'm_even_conv_w': out['m_even_conv_w'], 'm_even_w_out': out['m_even_w_out'], 'm_odd_pool_w': out['m_odd_pool_w'], 'm_odd_pool_scale': out['m_odd_pool_scale'], 'm_ffn_w_up': out['m_ffn_w_up'], 'm_ffn_conv_w': out['m_ffn_conv_w'], 'm_ffn_conv_b': out['m_ffn_conv_b'], 'm_ffn_w_down': out['m_ffn_w_down'], 'v_c_ctx': out['v_c_ctx'], 'v_ada_w': out['v_ada_w'], 'v_ada_b': out['v_ada_b'], 'v_mix_norm': out['v_mix_norm'], 'v_ffn_norm': out['v_ffn_norm'], 'v_even_w_in': out['v_even_w_in'], 'v_even_q_gain': out['v_even_q_gain'], 'v_even_k_gain': out['v_even_k_gain'], 'v_even_conv_w': out['v_even_conv_w'], 'v_even_w_out': out['v_even_w_out'], 'v_odd_pool_w': out['v_odd_pool_w'], 'v_odd_pool_scale': out['v_odd_pool_scale'], 'v_ffn_w_up': out['v_ffn_w_up'], 'v_ffn_conv_w': out['v_ffn_conv_w'], 'v_ffn_conv_b': out['v_ffn_conv_b'], 'v_ffn_w_down': out['v_ffn_w_down']}


def _loss(weights, diff, rest, loss_target):
    with _jax.named_scope("forward"):
        args = {**rest, TWIN_DIFF_INPUT: diff, **{k: w.astype(_WEIGHT_DTYPES[k]) for k, w in weights.items()}}
        y = _forward(args)
    with _jax.named_scope("loss_head"):
        err = _jnp.square(y.astype(_jnp.float32) - loss_target)
        return 0.5 * _jnp.sum(_jnp.mean(err, axis=-1)) if err.ndim else 0.5 * err


def _adamw(w, g, m, v):
    m = ADAM_B1 * m + (1.0 - ADAM_B1) * g
    v = ADAM_B2 * v + (1.0 - ADAM_B2) * _jnp.square(g)
    m_hat = m / (1.0 - ADAM_B1 ** ADAM_STEP)
    v_hat = v / (1.0 - ADAM_B2 ** ADAM_STEP)
    delta = -ADAM_LR * (m_hat / (_jnp.sqrt(v_hat) + ADAM_EPS) + ADAM_WD * w)
    return delta, m, v


def reference(x, c, ctx, c_ctx, ada_w, ada_b, mix_norm, ffn_norm, even_w_in, even_q_gain, even_k_gain, even_conv_w, even_w_out, odd_pool_w, odd_pool_scale, ffn_w_up, ffn_conv_w, ffn_conv_b, ffn_w_down, loss_target, m_c_ctx, m_ada_w, m_ada_b, m_mix_norm, m_ffn_norm, m_even_w_in, m_even_q_gain, m_even_k_gain, m_even_conv_w, m_even_w_out, m_odd_pool_w, m_odd_pool_scale, m_ffn_w_up, m_ffn_conv_w, m_ffn_conv_b, m_ffn_w_down, v_c_ctx, v_ada_w, v_ada_b, v_mix_norm, v_ffn_norm, v_even_w_in, v_even_q_gain, v_even_k_gain, v_even_conv_w, v_even_w_out, v_odd_pool_w, v_odd_pool_scale, v_ffn_w_up, v_ffn_conv_w, v_ffn_conv_b, v_ffn_w_down):
    given = dict(x=x, c=c, ctx=ctx, c_ctx=c_ctx, ada_w=ada_w, ada_b=ada_b, mix_norm=mix_norm, ffn_norm=ffn_norm, even_w_in=even_w_in, even_q_gain=even_q_gain, even_k_gain=even_k_gain, even_conv_w=even_conv_w, even_w_out=even_w_out, odd_pool_w=odd_pool_w, odd_pool_scale=odd_pool_scale, ffn_w_up=ffn_w_up, ffn_conv_w=ffn_conv_w, ffn_conv_b=ffn_conv_b, ffn_w_down=ffn_w_down, loss_target=loss_target, m_c_ctx=m_c_ctx, m_ada_w=m_ada_w, m_ada_b=m_ada_b, m_mix_norm=m_mix_norm, m_ffn_norm=m_ffn_norm, m_even_w_in=m_even_w_in, m_even_q_gain=m_even_q_gain, m_even_k_gain=m_even_k_gain, m_even_conv_w=m_even_conv_w, m_even_w_out=m_even_w_out, m_odd_pool_w=m_odd_pool_w, m_odd_pool_scale=m_odd_pool_scale, m_ffn_w_up=m_ffn_w_up, m_ffn_conv_w=m_ffn_conv_w, m_ffn_conv_b=m_ffn_conv_b, m_ffn_w_down=m_ffn_w_down, v_c_ctx=v_c_ctx, v_ada_w=v_ada_w, v_ada_b=v_ada_b, v_mix_norm=v_mix_norm, v_ffn_norm=v_ffn_norm, v_even_w_in=v_even_w_in, v_even_q_gain=v_even_q_gain, v_even_k_gain=v_even_k_gain, v_even_conv_w=v_even_conv_w, v_even_w_out=v_even_w_out, v_odd_pool_w=v_odd_pool_w, v_odd_pool_scale=v_odd_pool_scale, v_ffn_w_up=v_ffn_w_up, v_ffn_conv_w=v_ffn_conv_w, v_ffn_conv_b=v_ffn_conv_b, v_ffn_w_down=v_ffn_w_down)
    weights = {n: given[n] for n in TWIN_WEIGHTS}
    shared = {n: given[n] for n in SHARED_INPUTS}
    per_example = {n: given[n] for n in ['x', 'c', 'ctx']}
    grad_fn = _jax.value_and_grad(_loss, argnums=(0, 1))

    def one_microbatch(ex, loss_target):
        ex = dict(ex)
        diff = ex.pop(TWIN_DIFF_INPUT)
        return grad_fn(weights, diff, {**shared, **ex}, loss_target)

    if N_MICROBATCH == 1:
        loss, (grad_w, grad_x) = one_microbatch(per_example, given["loss_target"])
    else:
        def body(carry, xs):
            loss_sum, grad_sum = carry
            l_k, (gw_k, gx_k) = one_microbatch(xs[0], xs[1])
            with _jax.named_scope("update"):
                return (loss_sum + l_k, _jax.tree.map(_jnp.add, grad_sum, gw_k)), gx_k

        init = (_jnp.zeros((), _jnp.float32), _jax.tree.map(_jnp.zeros_like, weights))
        (loss, grad_w), grad_x = _jax.lax.scan(body, init, (per_example, given["loss_target"]))
    with _jax.named_scope("update"):
        delta_w, new_m, new_v = {}, {}, {}
        for n in TWIN_WEIGHTS:
            delta_w[n], new_m[n], new_v[n] = _adamw(weights[n], grad_w[n], given["m_" + n], given["v_" + n])
    return (loss, grad_x, *[grad_w[n] for n in TWIN_WEIGHTS], *[delta_w[n] for n in TWIN_WEIGHTS],
            *[new_m[n] for n in TWIN_WEIGHTS], *[new_v[n] for n in TWIN_WEIGHTS])
```

```python
import functools
import math

import jax
import jax.numpy as jnp
from jax import lax
from jax.experimental import pallas as pl
from jax.experimental.pallas import tpu as pltpu

F32 = jnp.float32
BF16 = jnp.bfloat16

D = 1024
HD = 128
NQ = 4
NKV = 2
AW = NQ * HD
CW = D - AW
DFF = 2816
GRID_W = 64
ROPE_THETA = 10000.0
POOL_WINDOWS = (2, 4, 8, 16)
PG = D // 4
EPS = 1e-6
NDEV = 8
HALO = 8
MESH = pl.DeviceIdType.MESH

ADAM_LR = 0.001
ADAM_B1 = 0.9
ADAM_B2 = 0.999
ADAM_EPS = 1e-08
ADAM_WD = 0.01
ADAM_STEP = 10


def _pick(dim, prefs):
    for p in prefs:
        if dim % p == 0:
            return p
    return dim


def _params(*sem):
    return pltpu.CompilerParams(dimension_semantics=sem)


def _mm(a_list, b, *, name, ta=False, tb=False, out_dtype=F32, silu_a=False, bias=None, tm=None, tn=None, tk=None):
    if not isinstance(a_list, (list, tuple)):
        a_list = [a_list]
    na = len(a_list)
    assert not (ta and na > 1)
    if ta:
        kdim, m = a_list[0].shape
        ks = [kdim]
    else:
        m = a_list[0].shape[0]
        ks = [a.shape[1] for a in a_list]
        kdim = sum(ks)
    n = b.shape[0] if tb else b.shape[1]
    assert (b.shape[1] if tb else b.shape[0]) == kdim
    kunit = math.gcd(*ks) if na > 1 else kdim
    tm = min(tm, m) if tm else _pick(m, (512, 256, 128, 64, 32, 16, 8))
    tn = min(tn, n) if tn else _pick(n, (512, 256, 128))
    tk = min(tk, kunit) if tk else _pick(kunit, (1024, 768, 512, 256, 128))
    assert m % tm == 0 and n % tn == 0 and all(k % tk == 0 for k in ks)
    nks = [k // tk for k in ks]
    starts = [sum(nks[:i]) for i in range(na)]
    nk = sum(nks)
    has_bias = bias is not None

    def body(*refs):
        a_refs = refs[:na]
        b_ref = refs[na]
        bias_ref = refs[na + 1] if has_bias else None
        o_ref = refs[na + 1 + has_bias]
        acc = refs[-1]
        k = pl.program_id(2)

        @pl.when(k == 0)
        def _():
            acc[...] = jnp.zeros_like(acc)

        bv = b_ref[...].astype(BF16)
        dn = (((0 if ta else 1,), (1 if tb else 0,)), ((), ()))
        for idx in range(na):
            def step(idx=idx):
                av = a_refs[idx][...]
                if silu_a:
                    av = av * jax.nn.sigmoid(av)
                acc[...] += lax.dot_general(av.astype(BF16), bv, dn, preferred_element_type=F32)
            if na == 1:
                step()
            else:
                pl.when((k >= starts[idx]) & (k < starts[idx] + nks[idx]))(step)

        @pl.when(k == nk - 1)
        def _():
            r = acc[...]
            if has_bias:
                r = r + bias_ref[...]
            o_ref[...] = r.astype(o_ref.dtype)

    in_specs = []
    for idx in range(na):
        if ta:
            in_specs.append(pl.BlockSpec((tk, tm), lambda i, j, k: (k, i)))
        else:
            lo, cnt = starts[idx], nks[idx]
            in_specs.append(pl.BlockSpec((tm, tk), lambda i, j, k, lo=lo, cnt=cnt: (i, jnp.clip(k - lo, 0, cnt - 1))))
    if tb:
        in_specs.append(pl.BlockSpec((tn, tk), lambda i, j, k: (j, k)))
    else:
        in_specs.append(pl.BlockSpec((tk, tn), lambda i, j, k: (k, j)))
    args = list(a_list) + [b]
    if has_bias:
        in_specs.append(pl.BlockSpec((1, tn), lambda i, j, k: (0, j)))
        args.append(bias)
    return pl.pallas_call(
        body, grid=(m // tm, n // tn, nk), in_specs=in_specs,
        out_specs=pl.BlockSpec((tm, tn), lambda i, j, k: (i, j)),
        out_shape=jax.ShapeDtypeStruct((m, n), out_dtype),
        scratch_shapes=[pltpu.VMEM((tm, tn), F32)], name=name,
        compiler_params=_params("parallel", "parallel", "arbitrary"))(*args)


def _vec(d, col=None):
    if col is None:
        return pl.BlockSpec((1, d), lambda i, *_: (0, 0))
    return pl.BlockSpec((1, d), col)


def _halo_specs(tm, width, nrows, colblk=0, row_off=0):
    r = tm // HALO
    off = row_off // HALO
    last = nrows // HALO - 1
    prev = pl.BlockSpec((HALO, width), lambda i, *_: (off + jnp.maximum(i * r - 1, 0), colblk))
    nxt = pl.BlockSpec((HALO, width), lambda i, *_: (off + jnp.minimum((i + 1) * r, last), colblk))
    return prev, nxt


def _ext(prev_ref, main_ref, next_ref, i, ni):
    p = jnp.where(i > 0, prev_ref[...], 0.0)
    n = jnp.where(i < ni - 1, next_ref[...], 0.0)
    return jnp.concatenate([p, main_ref[...], n], axis=0)


def _sh(ext, k, tm):
    if k == 0:
        return ext[HALO:HALO + tm]
    rows = ext.shape[0]
    return pltpu.roll(ext, (-k) % rows, axis=0)[HALO:HALO + tm]


def _roll_rows(v, k):
    rows = v.shape[0]
    return pltpu.roll(v, (-k) % rows, axis=0) if k % rows else v


def _conv3(ext, w_ref, tm):
    return _sh(ext, -1, tm) * w_ref[0:1, :] + _sh(ext, 0, tm) * w_ref[1:2, :] + _sh(ext, 1, tm) * w_ref[2:3, :]


def _colsum(v):
    return jnp.sum(v, axis=0, keepdims=True)


def _acc_out(ref, i, val):
    @pl.when(i == 0)
    def _():
        ref[...] = val

    @pl.when(i > 0)
    def _():
        ref[...] += val


def _sigmoid(v):
    return jax.nn.sigmoid(v)


def _norm_mod(x, gain, sc, sh, *, name, y=None, g=None, ymul=None, tm=256):
    n, d = x.shape
    has_res = y is not None
    has_mul = ymul is not None

    def body(*refs):
        it = iter(refs)
        x_ref = next(it)
        y_ref = next(it) if has_res else None
        g_ref = next(it) if has_res else None
        m_ref = next(it) if has_mul else None
        gain_ref, sc_ref, sh_ref = next(it), next(it), next(it)
        xo_ref = next(it) if has_res else None
        a_ref = next(it)
        xv = x_ref[...]
        if has_res:
            yv = y_ref[...]
            if has_mul:
                yv = yv * m_ref[...]
            xv = xv + g_ref[...] * yv
            xo_ref[...] = xv
        r = lax.rsqrt(jnp.mean(xv * xv, axis=-1, keepdims=True) + EPS)
        nrm = (xv * r) * gain_ref[...]
        a_ref[...] = (nrm * (1.0 + sc_ref[...]) + sh_ref[...]).astype(BF16)

    row = pl.BlockSpec((tm, d), lambda i: (i, 0))
    in_specs, args = [row], [x]
    if has_res:
        in_specs += [row, _vec(d)]
        args += [y, g]
    if has_mul:
        in_specs.append(_vec(d))
        args.append(ymul)
    in_specs += [_vec(d)] * 3
    args += [gain, sc, sh]
    out_specs, out_shape = [], []
    if has_res:
        out_specs.append(row)
        out_shape.append(jax.ShapeDtypeStruct((n, d), F32))
    out_specs.append(row)
    out_shape.append(jax.ShapeDtypeStruct((n, d), BF16))
    res = pl.pallas_call(body, grid=(n // tm,), in_specs=in_specs, out_specs=out_specs, out_shape=out_shape,
                         name=name, compiler_params=_params("parallel"))(*args)
    return res if has_res else res[0]


def _norm_mod_bwd(da, x, gain, sc, *, name, dres=None, tm=256):
    n, d = x.shape
    has_res = dres is not None

    def body(*refs):
        it = iter(refs)
        da_ref, x_ref = next(it), next(it)
        r_ref = next(it) if has_res else None
        gain_ref, sc_ref = next(it), next(it)
        dx_ref, dsh_ref, dsc_ref, dgn_ref = next(it), next(it), next(it), next(it)
        i = pl.program_id(0)
        xv = x_ref[...]
        dav = da_ref[...]
        r = lax.rsqrt(jnp.mean(xv * xv, axis=-1, keepdims=True) + EPS)
        xh = xv * r
        nrm = xh * gain_ref[...]
        dn = dav * (1.0 + sc_ref[...])
        dxh = dn * gain_ref[...]
        dx = r * (dxh - xh * jnp.mean(dxh * xh, axis=-1, keepdims=True))
        if has_res:
            dx = dx + r_ref[...]
        dx_ref[...] = dx
        _acc_out(dsh_ref, i, _colsum(dav))
        _acc_out(dsc_ref, i, _colsum(dav * nrm))
        _acc_out(dgn_ref, i, _colsum(dn * xh))

    row = pl.BlockSpec((tm, d), lambda i: (i, 0))
    in_specs, args = [row, row], [da, x]
    if has_res:
        in_specs.append(row)
        args.append(dres)
    in_specs += [_vec(d)] * 2
    args += [gain, sc]
    vec_shape = jax.ShapeDtypeStruct((1, d), F32)
    return pl.pallas_call(
        body, grid=(n // tm,), in_specs=in_specs, out_specs=[row, _vec(d), _vec(d), _vec(d)],
        out_shape=[jax.ShapeDtypeStruct((n, d), F32), vec_shape, vec_shape, vec_shape],
        name=name, compiler_params=_params("arbitrary"))(*args)


def _gate_bwd(dxo, y, g, *, name, tm=256):
    n, d = dxo.shape

    def body(dx_ref, y_ref, g_ref, dy_ref, dg_ref):
        i = pl.program_id(0)
        dxv = dx_ref[...]
        dy_ref[...] = (dxv * g_ref[...]).astype(BF16)
        _acc_out(dg_ref, i, _colsum(dxv * y_ref[...]))

    row = pl.BlockSpec((tm, d), lambda i: (i, 0))
    return pl.pallas_call(
        body, grid=(n // tm,), in_specs=[row, row, _vec(d)], out_specs=[row, _vec(d)],
        out_shape=[jax.ShapeDtypeStruct((n, d), BF16), jax.ShapeDtypeStruct((1, d), F32)],
        name=name, compiler_params=_params("arbitrary"))(dxo, y, g)


def _loss_head(x, z, g, tgt, *, name, tm=256):
    n, d = x.shape

    def body(x_ref, z_ref, g_ref, t_ref, dx_ref, loss_ref):
        i = pl.program_id(0)
        diff = (x_ref[...] + g_ref[...] * z_ref[...]) - t_ref[...]
        dx_ref[...] = diff * (1.0 / d)
        part = 0.5 * jnp.sum(jnp.mean(diff * diff, axis=-1, keepdims=True), axis=0, keepdims=True)
        _acc_out(loss_ref, i, jnp.broadcast_to(part, (1, 128)))

    row = pl.BlockSpec((tm, d), lambda i: (i, 0))
    return pl.pallas_call(
        body, grid=(n // tm,), in_specs=[row, row, _vec(d), row], out_specs=[row, _vec(128)],
        out_shape=[jax.ShapeDtypeStruct((n, d), F32), jax.ShapeDtypeStruct((1, 128), F32)],
        name=name, compiler_params=_params("arbitrary"))(x, z, g, tgt)


def _glu_fwd(u, cw, cb, *, name, tm=256, tc=256):
    n = u.shape[0]
    nc = DFF // tc
    ni = n // tm

    def body(g_ref, gp_ref, gn_ref, v_ref, cw_ref, cb_ref, h_ref):
        i = pl.program_id(0)
        gext = _ext(gp_ref, g_ref, gn_ref, i, ni)
        gc = _conv3(gext, cw_ref, tm) + cb_ref[...]
        h_ref[...] = (gc * _sigmoid(gc) * v_ref[...]).astype(BF16)

    prev = pl.BlockSpec((HALO, tc), lambda i, j: (jnp.maximum(i * (tm // HALO) - 1, 0), j))
    nxt = pl.BlockSpec((HALO, tc), lambda i, j: (jnp.minimum((i + 1) * (tm // HALO), n // HALO - 1), j))
    return pl.pallas_call(
        body, grid=(ni, nc),
        in_specs=[pl.BlockSpec((tm, tc), lambda i, j: (i, j)), prev, nxt,
                  pl.BlockSpec((tm, tc), lambda i, j: (i, nc + j)),
                  pl.BlockSpec((3, tc), lambda i, j: (0, j)), pl.BlockSpec((1, tc), lambda i, j: (0, j))],
        out_specs=pl.BlockSpec((tm, tc), lambda i, j: (i, j)),
        out_shape=jax.ShapeDtypeStruct((n, DFF), BF16), name=name,
        compiler_params=_params("parallel", "parallel"))(u, u, u, u, cw, cb)


def _glu_bwd(dh, u, cw, cb, *, name, tm=256, tc=256):
    n = u.shape[0]
    nc = DFF // tc
    ni = n // tm
    rows = tm + 2 * HALO

    def body(dh_ref, dhp_ref, dhn_ref, g_ref, gp_ref, gn_ref, v_ref, vp_ref, vn_ref, cw_ref, cb_ref,
             dg_ref, dv_ref, dcw_ref, dcb_ref):
        i = pl.program_id(1)
        gext = _ext(gp_ref, g_ref, gn_ref, i, ni)
        dhext = _ext(dhp_ref, dh_ref, dhn_ref, i, ni)
        vext = _ext(vp_ref, v_ref, vn_ref, i, ni)
        gc = (_roll_rows(gext, -1) * cw_ref[0:1, :] + gext * cw_ref[1:2, :] + _roll_rows(gext, 1) * cw_ref[2:3, :]
              + cb_ref[...])
        sg = _sigmoid(gc)
        dgc = dhext * vext * (sg * (1.0 + gc * (1.0 - sg)))
        dv_ref[...] = (dh_ref[...] * (gc[HALO:HALO + tm] * sg[HALO:HALO + tm])).astype(BF16)
        dgate = (_sh(dgc, 1, tm) * cw_ref[0:1, :] + _sh(dgc, 0, tm) * cw_ref[1:2, :] + _sh(dgc, -1, tm) * cw_ref[2:3, :])
        dg_ref[...] = dgate.astype(BF16)
        dgc_t = dgc[HALO:HALO + tm]
        dcw = jnp.concatenate([_colsum(dgc_t * _sh(gext, -1, tm)), _colsum(dgc_t * _sh(gext, 0, tm)),
                               _colsum(dgc_t * _sh(gext, 1, tm))], axis=0)
        _acc_out(dcw_ref, i, dcw)
        _acc_out(dcb_ref, i, _colsum(dgc_t))

    r = tm // HALO
    last = n // HALO - 1

    def trio(off):
        return [pl.BlockSpec((tm, tc), lambda j, i: (i, off + j)),
                pl.BlockSpec((HALO, tc), lambda j, i: (jnp.maximum(i * r - 1, 0), off + j)),
                pl.BlockSpec((HALO, tc), lambda j, i: (jnp.minimum((i + 1) * r, last), off + j))]

    del rows
    return pl.pallas_call(
        body, grid=(nc, ni),
        in_specs=trio(0) + trio(0) + trio(nc) + [pl.BlockSpec((3, tc), lambda j, i: (0, j)),
                                                 pl.BlockSpec((1, tc), lambda j, i: (0, j))],
        out_specs=[pl.BlockSpec((tm, tc), lambda j, i: (i, j)), pl.BlockSpec((tm, tc), lambda j, i: (i, j)),
                   pl.BlockSpec((3, tc), lambda j, i: (0, j)), pl.BlockSpec((1, tc), lambda j, i: (0, j))],
        out_shape=[jax.ShapeDtypeStruct((n, DFF), BF16), jax.ShapeDtypeStruct((n, DFF), BF16),
                   jax.ShapeDtypeStruct((3, DFF), F32), jax.ShapeDtypeStruct((1, DFF), F32)],
        name=name, compiler_params=_params("parallel", "arbitrary"))(dh, dh, dh, u, u, u, u, u, u, cw, cb)


def _rope_tables(n):
    rows = n // GRID_W
    row_ids = jnp.repeat(jnp.arange(rows), GRID_W).astype(F32)
    col_ids = jnp.tile(jnp.arange(GRID_W), rows).astype(F32)
    axis_dim = HD // 2
    inv_freq = jnp.power(ROPE_THETA, -jnp.arange(0, axis_dim, 2, dtype=F32) / axis_dim)
    ar = row_ids[:, None] * inv_freq
    ac = col_ids[:, None] * inv_freq
    cs = jnp.concatenate([jnp.cos(ar), jnp.cos(ar), jnp.cos(ac), jnp.cos(ac)], axis=1)
    sn = jnp.concatenate([-jnp.sin(ar), jnp.sin(ar), -jnp.sin(ac), jnp.sin(ac)], axis=1)
    return cs, sn


def _partner(v):
    lane = lax.broadcasted_iota(jnp.int32, v.shape, 1)
    return jnp.where((lane % 64) < 32, pltpu.roll(v, HD - 32, axis=1), pltpu.roll(v, 32, axis=1))


def _qkv_prep(p, q_gain, k_gain, cs, sn, *, name, has_q, kv_col, tm=256):
    n = p.shape[0]
    rope = cs is not None

    def body(*refs):
        it = iter(refs)
        q_ref = next(it) if has_q else None
        kv_ref = next(it)
        qg_ref, kg_ref = next(it), next(it)
        cs_ref = next(it) if rope else None
        sn_ref = next(it) if rope else None
        qo_ref = next(it) if has_q else None
        ko_ref, vo_ref = next(it), next(it)

        def norm_rope(xh, gain):
            r = lax.rsqrt(jnp.mean(xh * xh, axis=-1, keepdims=True) + EPS)
            xn = (xh * r) * gain
            if rope:
                xn = xn * cs_ref[...] + _partner(xn) * sn_ref[...]
            return xn.astype(BF16)

        if has_q:
            for h in range(NQ):
                qo_ref[h] = norm_rope(q_ref[:, h * HD:(h + 1) * HD], qg_ref[...])
        for h in range(NKV):
            ko_ref[h] = norm_rope(kv_ref[:, h * HD:(h + 1) * HD], kg_ref[...])
            vo_ref[h] = kv_ref[:, (NKV + h) * HD:(NKV + h + 1) * HD].astype(BF16)

    in_specs, args = [], []
    if has_q:
        in_specs.append(pl.BlockSpec((tm, AW), lambda i: (i, 0)))
        args.append(p)
    in_specs += [pl.BlockSpec((tm, 2 * NKV * HD), lambda i: (i, kv_col)), _vec(HD), _vec(HD)]
    args += [p, q_gain, k_gain]
    if rope:
        in_specs += [pl.BlockSpec((tm, HD), lambda i: (i, 0))] * 2
        args += [cs, sn]
    out_specs, out_shape = [], []
    if has_q:
        out_specs.append(pl.BlockSpec((NQ, tm, HD), lambda i: (0, i, 0)))
        out_shape.append(jax.ShapeDtypeStruct((NQ, n, HD), BF16))
    out_specs += [pl.BlockSpec((NKV, tm, HD), lambda i: (0, i, 0))] * 2
    out_shape += [jax.ShapeDtypeStruct((NKV, n, HD), BF16)] * 2
    return pl.pallas_call(body, grid=(n // tm,), in_specs=in_specs, out_specs=out_specs, out_shape=out_shape,
                          name=name, compiler_params=_params("parallel"))(*args)


def _qkv_bwd(p, dq, dk, dv, q_gain, k_gain, cs, sn, *, name, has_q, kv_col, kv_row_off, tm=256):
    n = p.shape[0]
    rope = cs is not None
    rb = kv_row_off // tm

    def body(*refs):
        it = iter(refs)
        q_ref = next(it) if has_q else None
        kv_ref = next(it)
        dq_ref = next(it) if has_q else None
        dk_ref, dv_ref = next(it), next(it)
        qg_ref, kg_ref = next(it), next(it)
        cs_ref = next(it) if rope else None
        sn_ref = next(it) if rope else None
        dp_ref, dqg_ref, dkg_ref = next(it), next(it), next(it)
        i = pl.program_id(0)

        def back(xh, dout, gain):
            if rope:
                dout = dout * cs_ref[...] + _partner(dout * sn_ref[...])
            r = lax.rsqrt(jnp.mean(xh * xh, axis=-1, keepdims=True) + EPS)
            xhat = xh * r
            dxh = dout * gain
            dx = r * (dxh - xhat * jnp.mean(dxh * xhat, axis=-1, keepdims=True))
            return dx, _colsum(dout * xhat)

        dqg = jnp.zeros((1, HD), F32)
        dkg = jnp.zeros((1, HD), F32)
        if has_q:
            for h in range(NQ):
                dx, dg = back(q_ref[:, h * HD:(h + 1) * HD], dq_ref[h], qg_ref[...])
                dp_ref[:, h * HD:(h + 1) * HD] = dx.astype(BF16)
                dqg = dqg + dg
        else:
            dp_ref[:, 0:AW] = jnp.zeros((tm, AW), BF16)
        for h in range(NKV):
            dx, dg = back(kv_ref[:, h * HD:(h + 1) * HD], dk_ref[h], kg_ref[...])
            dp_ref[:, AW + h * HD:AW + (h + 1) * HD] = dx.astype(BF16)
            dkg = dkg + dg
            dp_ref[:, AW + (NKV + h) * HD:AW + (NKV + h + 1) * HD] = dv_ref[h].astype(BF16)
        _acc_out(dqg_ref, i, dqg)
        _acc_out(dkg_ref, i, dkg)

    in_specs, args = [], []
    if has_q:
        in_specs.append(pl.BlockSpec((tm, AW), lambda i: (i, 0)))
        args.append(p)
    in_specs.append(pl.BlockSpec((tm, 2 * NKV * HD), lambda i: (i, kv_col)))
    args.append(p)
    if has_q:
        in_specs.append(pl.BlockSpec((NQ, tm, HD), lambda i: (0, i, 0)))
        args.append(dq)
    in_specs += [pl.BlockSpec((NKV, tm, HD), lambda i: (0, rb + i, 0))] * 2 + [_vec(HD), _vec(HD)]
    args += [dk, dv, q_gain, k_gain]
    if rope:
        in_specs += [pl.BlockSpec((tm, HD), lambda i: (i, 0))] * 2
        args += [cs, sn]
    return pl.pallas_call(
        body, grid=(n // tm,), in_specs=in_specs,
        out_specs=[pl.BlockSpec((tm, D), lambda i: (i, 0)), _vec(HD), _vec(HD)],
        out_shape=[jax.ShapeDtypeStruct((n, D), BF16), jax.ShapeDtypeStruct((1, HD), F32),
                   jax.ShapeDtypeStruct((1, HD), F32)],
        name=name, compiler_params=_params("arbitrary"))(*args)


def _conv_gate_fwd(p, o, conv_w, *, name, tm=256):
    n = p.shape[0]
    ni = n // tm

    def body(gb_ref, gc_ref, gcp_ref, gcn_ref, xi_ref, xip_ref, xin_ref, o_ref, w_ref, cat_ref):
        i = pl.program_id(0)
        hext = _ext(gcp_ref, gc_ref, gcn_ref, i, ni) * _ext(xip_ref, xi_ref, xin_ref, i, ni)
        cat_ref[:, 0:AW] = o_ref[...].astype(BF16)
        cat_ref[:, AW:D] = (gb_ref[...] * _conv3(hext, w_ref, tm)).astype(BF16)

    gcp, gcn = _halo_specs(tm, CW, n, colblk=3)
    xip, xin = _halo_specs(tm, CW, n, colblk=4)
    return pl.pallas_call(
        body, grid=(ni,),
        in_specs=[pl.BlockSpec((tm, CW), lambda i: (i, 2)), pl.BlockSpec((tm, CW), lambda i: (i, 3)), gcp, gcn,
                  pl.BlockSpec((tm, CW), lambda i: (i, 4)), xip, xin, pl.BlockSpec((tm, AW), lambda i: (i, 0)),
                  pl.BlockSpec((3, CW), lambda i: (0, 0))],
        out_specs=pl.BlockSpec((tm, D), lambda i: (i, 0)), out_shape=jax.ShapeDtypeStruct((n, D), BF16),
        name=name, compiler_params=_params("parallel"))(p, p, p, p, p, p, p, o, conv_w)


def _conv_gate_bwd(dcat, p, conv_w, *, name, tm=256):
    n = p.shape[0]
    ni = n // tm

    def body(dc_ref, dcp_ref, dcn_ref, gb_ref, gbp_ref, gbn_ref, gc_ref, gcp_ref, gcn_ref, xi_ref, xip_ref, xin_ref,
             w_ref, dp_ref, dw_ref):
        i = pl.program_id(0)
        gcext = _ext(gcp_ref, gc_ref, gcn_ref, i, ni)
        xiext = _ext(xip_ref, xi_ref, xin_ref, i, ni)
        hext = gcext * xiext
        dcv = _ext(dcp_ref, dc_ref, dcn_ref, i, ni) * _ext(gbp_ref, gb_ref, gbn_ref, i, ni)
        dp_ref[:, 0:CW] = (dc_ref[...] * _conv3(hext, w_ref, tm)).astype(BF16)
        dh = _sh(dcv, 1, tm) * w_ref[0:1, :] + _sh(dcv, 0, tm) * w_ref[1:2, :] + _sh(dcv, -1, tm) * w_ref[2:3, :]
        dp_ref[:, CW:2 * CW] = (dh * xi_ref[...]).astype(BF16)
        dp_ref[:, 2 * CW:3 * CW] = (dh * gc_ref[...]).astype(BF16)
        dcv_t = dcv[HALO:HALO + tm]
        dw = jnp.concatenate([_colsum(dcv_t * _sh(hext, -1, tm)), _colsum(dcv_t * _sh(hext, 0, tm)),
                              _colsum(dcv_t * _sh(hext, 1, tm))], axis=0)
        _acc_out(dw_ref, i, dw)

    def trio(colblk):
        prev, nxt = _halo_specs(tm, CW, n, colblk=colblk)
        return [pl.BlockSpec((tm, CW), lambda i: (i, colblk)), prev, nxt]

    return pl.pallas_call(
        body, grid=(ni,), in_specs=trio(1) + trio(2) + trio(3) + trio(4) + [pl.BlockSpec((3, CW), lambda i: (0, 0))],
        out_specs=[pl.BlockSpec((tm, 3 * CW), lambda i: (i, 0)), pl.BlockSpec((3, CW), lambda i: (0, 0))],
        out_shape=[jax.ShapeDtypeStruct((n, 3 * CW), BF16), jax.ShapeDtypeStruct((3, CW), F32)],
        name=name, compiler_params=_params("arbitrary"))(dcat, dcat, dcat, p, p, p, p, p, p, p, p, p, conv_w)


_SCALE = HD ** -0.5
_NT = (((1,), (1,)), ((), ()))


def _attn_fwd(q, k, v, *, name, bq=512, bk=768):
    n = q.shape[1]
    t = k.shape[1]
    nk = t // bk

    def body(q_ref, k_ref, v_ref, o_ref, lse_ref, m_s, l_s, acc_s):
        ki = pl.program_id(2)

        @pl.when(ki == 0)
        def _():
            m_s[...] = jnp.full_like(m_s, -jnp.inf)
            l_s[...] = jnp.zeros_like(l_s)
            acc_s[...] = jnp.zeros_like(acc_s)

        q2 = q_ref[...].reshape(2 * bq, HD)
        s = lax.dot_general(q2, k_ref[0], _NT, preferred_element_type=F32) * _SCALE
        m_new = jnp.maximum(m_s[...], jnp.max(s, axis=-1, keepdims=True))
        alpha = jnp.exp(m_s[...] - m_new)
        pv = jnp.exp(s - m_new)
        l_s[...] = alpha * l_s[...] + jnp.sum(pv, axis=-1, keepdims=True)
        acc_s[...] = alpha * acc_s[...] + jnp.dot(pv.astype(BF16), v_ref[0], preferred_element_type=F32)
        m_s[...] = m_new

        @pl.when(ki == nk - 1)
        def _():
            out = acc_s[...] / l_s[...]
            o_ref[:, 0:HD] = out[0:bq]
            o_ref[:, HD:2 * HD] = out[bq:2 * bq]
            lse_ref[...] = (m_s[...] + jnp.log(l_s[...])).reshape(2, bq, 1)

    return pl.pallas_call(
        body, grid=(NKV, n // bq, nk),
        in_specs=[pl.BlockSpec((2, bq, HD), lambda h, i, j: (h, i, 0)),
                  pl.BlockSpec((1, bk, HD), lambda h, i, j: (h, j, 0)),
                  pl.BlockSpec((1, bk, HD), lambda h, i, j: (h, j, 0))],
        out_specs=[pl.BlockSpec((bq, 2 * HD), lambda h, i, j: (i, h)),
                   pl.BlockSpec((2, bq, 1), lambda h, i, j: (h, i, 0))],
        out_shape=[jax.ShapeDtypeStruct((n, AW), F32), jax.ShapeDtypeStruct((NQ, n, 1), F32)],
        scratch_shapes=[pltpu.VMEM((2 * bq, 1), F32), pltpu.VMEM((2 * bq, 1), F32), pltpu.VMEM((2 * bq, HD), F32)],
        name=name, compiler_params=_params("parallel", "parallel", "arbitrary"))(q, k, v)


def _attn_bwd_prep(dcat, o, *, name, tm=256):
    n = o.shape[0]

    def body(dc_ref, o_ref, do_ref, dl_ref):
        for h in range(NQ):
            dh = dc_ref[:, h * HD:(h + 1) * HD]
            do_ref[h] = dh.astype(BF16)
            dl_ref[h] = jnp.sum(dh * o_ref[:, h * HD:(h + 1) * HD], axis=-1, keepdims=True)

    return pl.pallas_call(
        body, grid=(n // tm,),
        in_specs=[pl.BlockSpec((tm, AW), lambda i: (i, 0)), pl.BlockSpec((tm, AW), lambda i: (i, 0))],
        out_specs=[pl.BlockSpec((NQ, tm, HD), lambda i: (0, i, 0)), pl.BlockSpec((NQ, tm, 1), lambda i: (0, i, 0))],
        out_shape=[jax.ShapeDtypeStruct((NQ, n, HD), BF16), jax.ShapeDtypeStruct((NQ, n, 1), F32)],
        name=name, compiler_params=_params("parallel"))(dcat, o)


def _attn_bwd_dq(q, k, v, do, lse, delta, *, name, bq=512, bk=768):
    n = q.shape[1]
    t = k.shape[1]
    nk = t // bk

    def body(q_ref, k_ref, v_ref, do_ref, lse_ref, dl_ref, dq_ref, acc_s):
        ki = pl.program_id(2)

        @pl.when(ki == 0)
        def _():
            acc_s[...] = jnp.zeros_like(acc_s)

        q2 = q_ref[...].reshape(2 * bq, HD)
        do2 = do_ref[...].reshape(2 * bq, HD)
        s = lax.dot_general(q2, k_ref[0], _NT, preferred_element_type=F32) * _SCALE
        pv = jnp.exp(s - lse_ref[...].reshape(2 * bq, 1))
        dp = lax.dot_general(do2, v_ref[0], _NT, preferred_element_type=F32)
        ds = pv * (dp - dl_ref[...].reshape(2 * bq, 1)) * _SCALE
        acc_s[...] += jnp.dot(ds.astype(BF16), k_ref[0], preferred_element_type=F32)

        @pl.when(ki == nk - 1)
        def _():
            dq_ref[...] = acc_s[...].reshape(2, bq, HD)

    qspec = pl.BlockSpec((2, bq, HD), lambda h, i, j: (h, i, 0))
    kspec = pl.BlockSpec((1, bk, HD), lambda h, i, j: (h, j, 0))
    sspec = pl.BlockSpec((2, bq, 1), lambda h, i, j: (h, i, 0))
    return pl.pallas_call(
        body, grid=(NKV, n // bq, nk), in_specs=[qspec, kspec, kspec, qspec, sspec, sspec], out_specs=qspec,
        out_shape=jax.ShapeDtypeStruct((NQ, n, HD), F32), scratch_shapes=[pltpu.VMEM((2 * bq, HD), F32)],
        name=name, compiler_params=_params("parallel", "parallel", "arbitrary"))(q, k, v, do, lse, delta)


def _attn_bwd_dkv(q, k, v, do, lse_t, delta_t, *, name, bq=512, bk=768):
    n = q.shape[1]
    t = k.shape[1]
    nq = n // bq

    def body(q_ref, k_ref, v_ref, do_ref, lse_ref, dl_ref, dk_ref, dv_ref, dk_s, dv_s):
        qi = pl.program_id(2)

        @pl.when(qi == 0)
        def _():
            dk_s[...] = jnp.zeros_like(dk_s)
            dv_s[...] = jnp.zeros_like(dv_s)

        q2 = q_ref[...].reshape(2 * bq, HD)
        do2 = do_ref[...].reshape(2 * bq, HD)
        lse_row = jnp.concatenate([lse_ref[0], lse_ref[1]], axis=1)
        dl_row = jnp.concatenate([dl_ref[0], dl_ref[1]], axis=1)
        st = lax.dot_general(k_ref[0], q2, _NT, preferred_element_type=F32) * _SCALE
        pt = jnp.exp(st - lse_row)
        dv_s[...] += jnp.dot(pt.astype(BF16), do2, preferred_element_type=F32)
        dpt = lax.dot_general(v_ref[0], do2, _NT, preferred_element_type=F32)
        dst = pt * (dpt - dl_row) * _SCALE
        dk_s[...] += jnp.dot(dst.astype(BF16), q2, preferred_element_type=F32)

        @pl.when(qi == nq - 1)
        def _():
            dk_ref[0] = dk_s[...]
            dv_ref[0] = dv_s[...]

    qspec = pl.BlockSpec((2, bq, HD), lambda h, j, i: (h, i, 0))
    kspec = pl.BlockSpec((1, bk, HD), lambda h, j, i: (h, j, 0))
    sspec = pl.BlockSpec((2, 1, bq), lambda h, j, i: (h, 0, i))
    return pl.pallas_call(
        body, grid=(NKV, t // bk, nq), in_specs=[qspec, kspec, kspec, qspec, sspec, sspec], out_specs=[kspec, kspec],
        out_shape=[jax.ShapeDtypeStruct((NKV, t, HD), F32)] * 2,
        scratch_shapes=[pltpu.VMEM((bk, HD), F32), pltpu.VMEM((bk, HD), F32)],
        name=name, compiler_params=_params("parallel", "parallel", "arbitrary"))(q, k, v, do, lse_t, delta_t)


def _window_sums(ext, w):
    s, step = ext, 1
    while step < w:
        s = s + _roll_rows(s, step)
        step *= 2
    return s


def _pool_counts(i, tm, n, w, rows, first):
    t = i * tm - HALO + first + lax.broadcasted_iota(jnp.int32, (rows, 1), 0)
    lo = jnp.clip(t - w // 2, 0, n)
    hi = jnp.clip(t + w - w // 2, 0, n)
    return jnp.maximum(hi - lo, 1).astype(F32)


def _norm_mod_ext(xext, gain_ref, sc_ref, sh_ref, i, tm, n):
    rows = xext.shape[0]
    t = i * tm - HALO + lax.broadcasted_iota(jnp.int32, (rows, 1), 0)
    inside = (t >= 0) & (t < n)
    r = lax.rsqrt(jnp.mean(xext * xext, axis=-1, keepdims=True) + EPS)
    xh = xext * r
    a = (xh * gain_ref[...]) * (1.0 + sc_ref[...]) + sh_ref[...]
    return jnp.where(inside, a, 0.0), r, xh


def _pool_fwd(x, gain, sc, sh, pool_w, *, name, tm=256):
    n, d = x.shape
    ni = n // tm

    def body(x_ref, xp_ref, xn_ref, gain_ref, sc_ref, sh_ref, w_ref, o_ref):
        i = pl.program_id(0)
        aext, _, _ = _norm_mod_ext(_ext(xp_ref, x_ref, xn_ref, i, ni), gain_ref, sc_ref, sh_ref, i, tm, n)
        for gi, w in enumerate(POOL_WINDOWS):
            ag = aext[:, gi * PG:(gi + 1) * PG]
            mean = _sh(_window_sums(ag, w), -(w // 2), tm) / _pool_counts(i, tm, n, w, tm, HALO)
            pooled = mean - ag[HALO:HALO + tm]
            o_ref[:, gi * PG:(gi + 1) * PG] = jnp.dot(pooled.astype(BF16), w_ref[gi], preferred_element_type=F32)

    prev, nxt = _halo_specs(tm, d, n)
    return pl.pallas_call(
        body, grid=(ni,),
        in_specs=[pl.BlockSpec((tm, d), lambda i: (i, 0)), prev, nxt, _vec(d), _vec(d), _vec(d),
                  pl.BlockSpec((4, PG, PG), lambda i: (0, 0, 0))],
        out_specs=pl.BlockSpec((tm, d), lambda i: (i, 0)), out_shape=jax.ShapeDtypeStruct((n, d), F32),
        name=name, compiler_params=_params("parallel"))(x, x, x, gain, sc, sh, pool_w)


def _pool_bwd(dxo, mixed, x, g, scale, gain, sc, sh, pool_w, *, name, tm=256):
    n, d = x.shape
    ni = n // tm
    _TN = (((0,), (0,)), ((), ()))

    def body(dx_ref, dxp_ref, dxn_ref, mx_ref, x_ref, xp_ref, xn_ref, g_ref, s_ref, gain_ref, sc_ref, sh_ref, w_ref,
             dxi_ref, dw_ref, dg_ref, dsl_ref, dsh_ref, dsc_ref, dgn_ref):
        i = pl.program_id(0)
        dxo_t = dx_ref[...]
        mixed_t = mx_ref[...]
        dy_t = dxo_t * g_ref[...]
        _acc_out(dg_ref, i, _colsum(dxo_t * (mixed_t * s_ref[...])))
        _acc_out(dsl_ref, i, _colsum(dy_t * mixed_t))
        dmixed = (_ext(dxp_ref, dx_ref, dxn_ref, i, ni) * g_ref[...]) * s_ref[...]
        xext = _ext(xp_ref, x_ref, xn_ref, i, ni)
        aext, rext, xhext = _norm_mod_ext(xext, gain_ref, sc_ref, sh_ref, i, tm, n)
        rows = tm + 2 * HALO
        da_parts = []
        for gi, w in enumerate(POOL_WINDOWS):
            sl = slice(gi * PG, (gi + 1) * PG)
            ag = aext[:, sl]
            mean = _sh(_window_sums(ag, w), -(w // 2), tm) / _pool_counts(i, tm, n, w, tm, HALO)
            pooled = (mean - ag[HALO:HALO + tm]).astype(BF16)
            dmg = dmixed[:, sl].astype(BF16)
            dwg = lax.dot_general(pooled, dmixed[HALO:HALO + tm, sl].astype(BF16), _TN, preferred_element_type=F32)

            @pl.when(i == 0)
            def _(dwg=dwg, gi=gi):
                dw_ref[gi] = dwg

            @pl.when(i > 0)
            def _(dwg=dwg, gi=gi):
                dw_ref[gi] += dwg

            dpl = lax.dot_general(dmg, w_ref[gi], _NT, preferred_element_type=F32)
            e = dpl / _pool_counts(i, tm, n, w, rows, 0)
            da_parts.append(_sh(_window_sums(e, w), 1 - w // 2, tm) - dpl[HALO:HALO + tm])
        da = jnp.concatenate(da_parts, axis=1)
        r = rext[HALO:HALO + tm]
        xh = xhext[HALO:HALO + tm]
        nrm = xh * gain_ref[...]
        dn = da * (1.0 + sc_ref[...])
        dxh = dn * gain_ref[...]
        dxi_ref[...] = dxo_t + r * (dxh - xh * jnp.mean(dxh * xh, axis=-1, keepdims=True))
        _acc_out(dsh_ref, i, _colsum(da))
        _acc_out(dsc_ref, i, _colsum(da * nrm))
        _acc_out(dgn_ref, i, _colsum(dn * xh))

    row = pl.BlockSpec((tm, d), lambda i: (i, 0))
    prev, nxt = _halo_specs(tm, d, n)
    wspec = pl.BlockSpec((4, PG, PG), lambda i: (0, 0, 0))
    vshape = jax.ShapeDtypeStruct((1, d), F32)
    return pl.pallas_call(
        body, grid=(ni,),
        in_specs=[row, prev, nxt, row, row, prev, nxt] + [_vec(d)] * 5 + [wspec],
        out_specs=[row, wspec] + [_vec(d)] * 5,
        out_shape=[jax.ShapeDtypeStruct((n, d), F32), jax.ShapeDtypeStruct((4, PG, PG), F32)] + [vshape] * 5,
        name=name, compiler_params=_params("arbitrary"))(dxo, dxo, dxo, mixed, x, x, x, g, scale, gain, sc, sh, pool_w)


def _adamw(gparts, w, m, v, *, name, silu_grad_of=None):
    nparts, r, c = gparts.shape
    tr = _pick(r, (256, 128, 64, 32, 16, 8))
    has_c = silu_grad_of is not None

    def body(*refs):
        it = iter(refs)
        gp_ref, w_ref, m_ref, v_ref = next(it), next(it), next(it), next(it)
        c_ref = next(it) if has_c else None
        g_ref, d_ref, mo_ref, vo_ref = next(it), next(it), next(it), next(it)
        g = gp_ref[0].astype(F32)
        for p in range(1, nparts):
            g = g + gp_ref[p].astype(F32)
        if has_c:
            cv = c_ref[...]
            sg = _sigmoid(cv)
            g = g * (sg * (1.0 + cv * (1.0 - sg)))
        g_ref[...] = g
        mn = ADAM_B1 * m_ref[...] + (1.0 - ADAM_B1) * g
        vn = ADAM_B2 * v_ref[...] + (1.0 - ADAM_B2) * (g * g)
        m_hat = mn / (1.0 - ADAM_B1 ** ADAM_STEP)
        v_hat = vn / (1.0 - ADAM_B2 ** ADAM_STEP)
        d_ref[...] = -ADAM_LR * (m_hat / (jnp.sqrt(v_hat) + ADAM_EPS) + ADAM_WD * w_ref[...])
        mo_ref[...] = mn
        vo_ref[...] = vn

    row = pl.BlockSpec((tr, c), lambda i: (i, 0))
    in_specs = [pl.BlockSpec((nparts, tr, c), lambda i: (0, i, 0)), row, row, row]
    args = [gparts, w, m, v]
    if has_c:
        in_specs.append(row)
        args.append(silu_grad_of)
    return pl.pallas_call(
        body, grid=(r // tr,), in_specs=in_specs, out_specs=[row] * 4,
        out_shape=[jax.ShapeDtypeStruct((r, c), F32)] * 4, name=name, compiler_params=_params("parallel"))(*args)


def _adamw_nd(gparts, w, m, v, *, name, silu_grad_of=None):
    shape = w.shape
    c = shape[-1]
    r = math.prod(shape[:-1]) if len(shape) > 1 else 1
    rs = lambda a: a.reshape(r, c)
    res = _adamw(gparts.reshape(gparts.shape[0], r, c), rs(w), rs(m), rs(v), name=name,
                 silu_grad_of=None if silu_grad_of is None else rs(silu_grad_of))
    return [a.reshape(shape) for a in res]


def _place():
    return lax.axis_index("x"), lax.axis_index("y"), lax.axis_index("c")


def _all_gather(arrs, *, name):
    k_arr = len(arrs)

    def body(*refs):
        ins = refs[:k_arr]
        outs = refs[k_arr:2 * k_arr]
        send_sems, recv_sems, local_sems = refs[2 * k_arr:]
        x, y, c = _place()
        me, sibling = (x, y, c), (x, y, 1 - c)
        chips = [(1 - x, y), (x, 1 - y), (1 - x, 1 - y)]

        def slot(a, px, py, pc):
            return outs[a].at[4 * px + 2 * py + pc]

        def copy(a, s, block, to, src=None):
            return pltpu.make_async_remote_copy(
                src_ref=slot(a, *block) if src is None else src, dst_ref=slot(a, *block),
                send_sem=send_sems.at[a, s], recv_sem=recv_sems.at[a, s], device_id=to, device_id_type=MESH)

        mine = [pltpu.make_async_copy(ins[a], slot(a, *me), local_sems.at[a]) for a in range(k_arr)]
        for cp in mine:
            cp.start()
        first = []
        for a in range(k_arr):
            first.append(copy(a, 0, me, sibling, src=ins[a]))
            first += [copy(a, 1 + j, me, (*chip, c), src=ins[a]) for j, chip in enumerate(chips)]
        for cp in first:
            cp.start()
        passed = []
        for j, chip in enumerate(chips):
            for a in range(k_arr):
                copy(a, 1 + j, (*chip, c), me).wait_recv()
                fw = copy(a, 4 + j, (*chip, c), sibling)
                fw.start()
                passed.append(fw)
        for a in range(k_arr):
            copy(a, 0, sibling, me).wait_recv()
            for j, chip in enumerate(chips):
                copy(a, 4 + j, (*chip, 1 - c), me).wait_recv()
        for cp in first + passed:
            cp.wait_send()
        for cp in mine:
            cp.wait()

    any_spec = pl.BlockSpec(memory_space=pl.ANY)
    return pl.pallas_call(
        body, in_specs=[any_spec] * k_arr, out_specs=[any_spec] * k_arr,
        out_shape=[jax.ShapeDtypeStruct((NDEV,) + a.shape, a.dtype) for a in arrs],
        scratch_shapes=[pltpu.SemaphoreType.DMA((k_arr, 7)), pltpu.SemaphoreType.DMA((k_arr, 7)),
                        pltpu.SemaphoreType.DMA((k_arr,))],
        name=name)(*arrs)


def _all_to_all(arrs, *, name):
    k_arr = len(arrs)

    def body(*refs):
        ins = refs[:k_arr]
        outs = refs[k_arr:2 * k_arr]
        send_sems, recv_sems, local_sems = refs[2 * k_arr:]
        x, y, c = _place()
        me = 4 * x + 2 * y + c
        copies = []
        for a in range(k_arr):
            cp = pltpu.make_async_copy(ins[a].at[me], outs[a].at[me], local_sems.at[a])
            cp.start()
            copies.append(cp)
        remote = []
        for rel in range(1, NDEV):
            px, py, pc = x ^ (rel >> 2), y ^ ((rel >> 1) & 1), c ^ (rel & 1)
            peer = 4 * px + 2 * py + pc
            for a in range(k_arr):
                cp = pltpu.make_async_remote_copy(
                    src_ref=ins[a].at[peer], dst_ref=outs[a].at[me], send_sem=send_sems.at[a, rel - 1],
                    recv_sem=recv_sems.at[a, rel - 1], device_id=(px, py, pc), device_id_type=MESH)
                cp.start()
                remote.append((cp, a, rel, peer))
        for cp, a, rel, peer in remote:
            pltpu.make_async_remote_copy(
                src_ref=ins[a].at[peer], dst_ref=outs[a].at[peer], send_sem=send_sems.at[a, rel - 1],
                recv_sem=recv_sems.at[a, rel - 1], device_id=(x, y, c), device_id_type=MESH).wait_recv()
        for cp, a, rel, peer in remote:
            cp.wait_send()
        for cp in copies:
            cp.wait()

    any_spec = pl.BlockSpec(memory_space=pl.ANY)
    return pl.pallas_call(
        body, in_specs=[any_spec] * k_arr, out_specs=[any_spec] * k_arr,
        out_shape=[jax.ShapeDtypeStruct(a.shape, a.dtype) for a in arrs],
        scratch_shapes=[pltpu.SemaphoreType.DMA((k_arr, 7)), pltpu.SemaphoreType.DMA((k_arr, 7)),
                        pltpu.SemaphoreType.DMA((k_arr,))],
        name=name)(*arrs)


def _ffn_fwd(x_in, y, g, ymul, gain, sc, sh, w_up, cw, cb, w_down, tag):
    xr, f = _norm_mod(x_in, gain, sc, sh, y=y, g=g, ymul=ymul, name=f"ffn_norm_{tag}")
    u = _mm(f, w_up, name=f"ffn_up_{tag}", tm=1024, tn=512, tk=1024)
    hmid = _glu_fwd(u, cw, cb, name=f"ffn_glu_{tag}")
    z = _mm(hmid, w_down, name=f"ffn_down_{tag}", tm=512, tn=512, tk=DFF)
    return xr, f, u, hmid, z


def _ffn_bwd(dxo, xr, f, u, hmid, z, g2, gain, sc, w_up, cw, cb, w_down, tag):
    dz, dg2 = _gate_bwd(dxo, z, g2, name=f"ffn_gate_bwd_{tag}")
    dh = _mm(dz, w_down, tb=True, name=f"ffn_down_dx_{tag}", tm=512, tn=256, tk=1024)
    d_wdown = _mm(hmid, dz, ta=True, out_dtype=BF16, name=f"ffn_down_dw_{tag}", tm=256, tn=512, tk=1024)
    dug, duv, dcw, dcb = _glu_bwd(dh, u, cw, cb, name=f"ffn_glu_bwd_{tag}")
    df = _mm([dug, duv], w_up, tb=True, name=f"ffn_up_dx_{tag}", tm=512, tn=512, tk=DFF)
    d_wup_g = _mm(f, dug, ta=True, out_dtype=BF16, name=f"ffn_up_dwg_{tag}", tm=512, tn=256, tk=1024)
    d_wup_v = _mm(f, duv, ta=True, out_dtype=BF16, name=f"ffn_up_dwv_{tag}", tm=512, tn=256, tk=1024)
    dxr, dsh2, dsc2, dgain = _norm_mod_bwd(df, xr, gain, sc, dres=dxo, name=f"ffn_norm_bwd_{tag}")
    return dxr, (d_wup_g, d_wup_v, d_wdown, dcw, dcb), (dsh2, dsc2, dg2, dgain)


def _split6(mod):
    return [mod[j * D:(j + 1) * D][None, :] for j in range(6)]


def _row(v):
    return v.reshape(1, -1)


def kernel(x, c, ctx, c_ctx, ada_w, ada_b, mix_norm, ffn_norm, even_w_in, even_q_gain, even_k_gain, even_conv_w, even_w_out, odd_pool_w, odd_pool_scale, ffn_w_up, ffn_conv_w, ffn_conv_b, ffn_w_down, loss_target, m_c_ctx, m_ada_w, m_ada_b, m_mix_norm, m_ffn_norm, m_even_w_in, m_even_q_gain, m_even_k_gain, m_even_conv_w, m_even_w_out, m_odd_pool_w, m_odd_pool_scale, m_ffn_w_up, m_ffn_conv_w, m_ffn_conv_b, m_ffn_w_down, v_c_ctx, v_ada_w, v_ada_b, v_mix_norm, v_ffn_norm, v_even_w_in, v_even_q_gain, v_even_k_gain, v_even_conv_w, v_even_w_out, v_odd_pool_w, v_odd_pool_scale, v_ffn_w_up, v_ffn_conv_w, v_ffn_conv_b, v_ffn_w_down):
    n = x.shape[1]
    lc = ctx.shape[1]
    me = 4 * lax.axis_index("x") + 2 * lax.axis_index("y") + lax.axis_index("c")
    xs, ctxs, tgt = x[0], ctx[0], loss_target[0]
    acols = ada_w.shape[2]

    small = jnp.concatenate([even_conv_w.reshape(-1), ffn_conv_w.reshape(-1), odd_pool_scale.reshape(-1)])
    nsmall = small.shape[0]
    small = jnp.pad(small, (0, (-nsmall) % 1024)).reshape(-1, 128)
    c_rows = jnp.pad(c, ((0, 7), (0, 0)))
    g_c, g_win, g_wout, g_pool, g_up, g_down, g_small = _all_gather(
        [c_rows, even_w_in[0].astype(BF16), even_w_out[0].astype(BF16), odd_pool_w[0].astype(BF16),
         ffn_w_up.astype(BF16), ffn_w_down.astype(BF16), small], name="gather_weights")
    w_in = g_win.transpose(1, 0, 2).reshape(D, -1)
    w_out = g_wout.reshape(D, D)
    pool_w = g_pool.transpose(1, 0, 2, 3).reshape(4, PG, PG)
    w_up = [g_up[:, l].transpose(1, 0, 2).reshape(D, 2 * DFF) for l in range(2)]
    w_down = [g_down[:, l].reshape(DFF, D) for l in range(2)]
    g_small = g_small.reshape(NDEV, -1)
    ecw = even_conv_w.shape[2]
    fcw = ffn_conv_w.shape[2]
    conv_w = g_small[:, :3 * ecw].reshape(NDEV, 3, ecw).transpose(1, 0, 2).reshape(3, CW)
    o1 = 3 * ecw
    fconv_w = g_small[:, o1:o1 + 6 * fcw].reshape(NDEV, 2, 3, fcw).transpose(1, 2, 0, 3).reshape(2, 3, DFF)
    o2 = o1 + 6 * fcw
    pool_scale = g_small[:, o2:o2 + D // NDEV].reshape(1, D)

    mraw = jnp.concatenate([g_c[:, 0, :], c_ctx[None, :], jnp.zeros((7, D), F32)], axis=0)
    my_bias = lax.dynamic_slice_in_dim(ada_b, me * acols, acols, axis=1)
    modp = jnp.stack([_mm(mraw, ada_w[l], silu_a=True, bias=my_bias[l:l + 1], name=f"ada_proj_{l}", tm=16, tn=256)
                      for l in range(2)])
    (g_mod,) = _all_gather([modp], name="gather_mod")
    mod_rows = g_mod.transpose(1, 2, 0, 3).reshape(2, 16, 6 * D)
    mod = lax.dynamic_index_in_dim(mod_rows, me, axis=1, keepdims=False)
    sh1, sc1, g1, sh2, sc2, g2 = _split6(mod[0])
    sh1b, sc1b, g1b, sh2b, sc2b, g2b = _split6(mod[1])
    csh1, csc1 = _split6(mod_rows[0, 8])[:2]
    mixn = [_row(mix_norm[l]) for l in range(2)]
    ffnn = [_row(ffn_norm[l]) for l in range(2)]
    qg, kg = _row(even_q_gain[0]), _row(even_k_gain[0])
    fcb = [_row(ffn_conv_b[l]) for l in range(2)]

    cs_t, sn_t = _rope_tables(n)
    a_lat = _norm_mod(xs, mixn[0], sc1, sh1, name="mix0_norm")
    a_ctx = _norm_mod(ctxs, mixn[0], csc1, csh1, name="mix0_norm_ctx")
    p_lat = _mm(a_lat, w_in, name="in_proj", tm=1024, tn=512, tk=1024)
    p_ctx = _mm(a_ctx, w_in[:, AW:AW + 4 * HD], name="in_proj_ctx", tm=256, tn=512, tk=1024)
    q_r, k_lat, v_lat = _qkv_prep(p_lat, qg, kg, cs_t, sn_t, has_q=True, kv_col=1, name="qkv_prep")
    k_ctx, v_ctx = _qkv_prep(p_ctx, qg, kg, None, None, has_q=False, kv_col=0, name="qkv_prep_ctx")
    k_all = jnp.concatenate([k_ctx, k_lat], axis=1)
    v_all = jnp.concatenate([v_ctx, v_lat], axis=1)
    o_attn, lse = _attn_fwd(q_r, k_all, v_all, name="attn_fwd")
    cat = _conv_gate_fwd(p_lat, o_attn, conv_w, name="conv_gate")
    y0 = _mm(cat, w_out, name="out_proj", tm=512, tn=512, tk=1024)
    x1, f0, u0, h0, z0 = _ffn_fwd(xs, y0, g1, None, ffnn[0], sc2, sh2, w_up[0], fconv_w[0], fcb[0], w_down[0], "l0")

    x2, _unused_a = _norm_mod(x1, mixn[1], sc1b, sh1b, y=z0, g=g2, name="mix1_resid")
    mixed = _pool_fwd(x2, mixn[1], sc1b, sh1b, pool_w, name="pool_fwd")
    x3, f1, u1, h1, z1 = _ffn_fwd(x2, mixed, g1b, pool_scale, ffnn[1], sc2b, sh2b, w_up[1], fconv_w[1], fcb[1],
                                  w_down[1], "l1")
    dx4, loss_part = _loss_head(x3, z1, g2b, tgt, name="loss_head")
    loss = lax.psum(loss_part[0, 0], ("x", "y", "c"))

    dx3, (dup1g, dup1v, ddown1, dfcw1, dfcb1), (dsh2b, dsc2b, dg2b, dffn1) = _ffn_bwd(
        dx4, x3, f1, u1, h1, z1, g2b, ffnn[1], sc2b, w_up[1], fconv_w[1], fcb[1], w_down[1], "l1")
    dx2, dpool_w, dg1b, dpscale, dsh1b, dsc1b, dmix1 = _pool_bwd(
        dx3, mixed, x2, g1b, pool_scale, mixn[1], sc1b, sh1b, pool_w, name="pool_bwd")

    dx1, (dup0g, dup0v, ddown0, dfcw0, dfcb0), (dsh2, dsc2, dg2, dffn0) = _ffn_bwd(
        dx2, x1, f0, u0, h0, z0, g2, ffnn[0], sc2, w_up[0], fconv_w[0], fcb[0], w_down[0], "l0")
    dy0, dg1 = _gate_bwd(dx1, y0, g1, name="mix0_gate_bwd")
    dcat = _mm(dy0, w_out, tb=True, name="out_proj_dx", tm=512, tn=512, tk=1024)
    d_wout = _mm(cat, dy0, ta=True, out_dtype=BF16, name="out_proj_dw", tm=512, tn=512, tk=1024)
    dp_conv, dconv_w = _conv_gate_bwd(dcat, p_lat, conv_w, name="conv_gate_bwd")
    do_h, delta = _attn_bwd_prep(dcat, o_attn, name="attn_bwd_prep")
    dq_r = _attn_bwd_dq(q_r, k_all, v_all, do_h, lse, delta, name="attn_bwd_dq")
    dk_all, dv_all = _attn_bwd_dkv(q_r, k_all, v_all, do_h, lse.reshape(NQ, 1, n), delta.reshape(NQ, 1, n),
                                   name="attn_bwd_dkv")
    dp_qkv, dqg_l, dkg_l = _qkv_bwd(p_lat, dq_r, dk_all, dv_all, qg, kg, cs_t, sn_t, has_q=True, kv_col=1,
                                    kv_row_off=lc, name="qkv_bwd")
    dp_ctx, _zero_qg, dkg_c = _qkv_bwd(p_ctx, None, dk_all, dv_all, qg, kg, None, None, has_q=False, kv_col=0,
                                       kv_row_off=0, name="qkv_bwd_ctx")
    da_lat = _mm([dp_qkv, dp_conv], w_in, tb=True, name="in_proj_dx", tm=512, tn=512, tk=512)
    da_ctx = _mm(dp_ctx, w_in[:, :D], tb=True, name="in_proj_dx_ctx", tm=256, tn=512, tk=1024)
    a_all = jnp.concatenate([a_lat, a_ctx], axis=0)
    dp_all = jnp.concatenate([dp_qkv, dp_ctx], axis=0)
    d_win_qkv = _mm(a_all, dp_all, ta=True, out_dtype=BF16, name="in_proj_dw_qkv", tm=512, tn=512, tk=768)
    d_win_conv = _mm(a_lat, dp_conv, ta=True, out_dtype=BF16, name="in_proj_dw_conv", tm=512, tn=512, tk=1024)
    grad_x, dsh1, dsc1, dmix0 = _norm_mod_bwd(da_lat, xs, mixn[0], sc1, dres=dx1, name="mix0_norm_bwd")
    _dctx, dcsh1, dcsc1, dmix0c = _norm_mod_bwd(da_ctx, ctxs, mixn[0], csc1, name="mix0_norm_bwd_ctx")

    z1k = jnp.zeros((1, D), F32)
    pack = jnp.concatenate(
        [v.reshape(-1) for v in (dsh1, dsc1, dg1, dsh2, dsc2, dg2, dsh1b, dsc1b, dg1b, dsh2b, dsc2b, dg2b,
                                 dcsh1, dcsc1, z1k, z1k, z1k, z1k,
                                 dmix0, dmix1, dmix0c, z1k, dffn0, dffn1, dqg_l, dkg_l + dkg_c,
                                 dfcb0, dfcb1, dconv_w, dfcw0, dfcw1, dpscale)])
    npack = pack.shape[0]
    pack = jnp.pad(pack, (0, (-npack) % 1024)).reshape(-1, 128)
    (g_pack,) = _all_gather([pack], name="gather_small_grads")
    gp = g_pack.reshape(NDEV, -1)
    off = [0]

    def take(size):
        seg = gp[:, off[0]:off[0] + size]
        off[0] += size
        return seg

    dmod_all = take(12 * D).reshape(NDEV, 2, 6 * D)
    dmodc_all = take(6 * D).reshape(NDEV, 1, 6 * D)
    dmix_all = take(4 * D).reshape(NDEV, 2, 2, D)
    dffn_all = take(2 * D).reshape(NDEV, 2, D)
    dqg_all = take(HD).reshape(NDEV, 1, HD)
    dkg_all = take(HD).reshape(NDEV, 1, HD)
    dfcb_all = take(2 * DFF).reshape(NDEV, 2, DFF)
    dconvw_all = take(3 * CW).reshape(NDEV, 3, CW)
    dfcw_all = take(6 * DFF).reshape(NDEV, 2, 3, DFF)
    dpscale_all = take(D).reshape(NDEV, D)

    outs = {}

    def put(nm, res):
        outs["grad_" + nm], outs["delta_" + nm], outs["new_m_" + nm], outs["new_v_" + nm] = res

    dmodc_pad = jnp.concatenate([dmodc_all, jnp.zeros_like(dmodc_all)], axis=1)
    put("ada_b", _adamw_nd(jnp.concatenate([dmod_all, dmodc_pad], axis=0), ada_b, m_ada_b, v_ada_b, name="adam_ada_b"))
    put("mix_norm", _adamw_nd(jnp.concatenate([dmix_all[:, 0], dmix_all[:, 1]], axis=0), mix_norm, m_mix_norm,
                              v_mix_norm, name="adam_mix_norm"))
    put("ffn_norm", _adamw_nd(dffn_all, ffn_norm, m_ffn_norm, v_ffn_norm, name="adam_ffn_norm"))
    put("even_q_gain", _adamw_nd(dqg_all, even_q_gain, m_even_q_gain, v_even_q_gain, name="adam_q_gain"))
    put("even_k_gain", _adamw_nd(dkg_all, even_k_gain, m_even_k_gain, v_even_k_gain, name="adam_k_gain"))
    put("ffn_conv_b", _adamw_nd(dfcb_all, ffn_conv_b, m_ffn_conv_b, v_ffn_conv_b, name="adam_ffn_conv_b"))
    my_convw = lax.dynamic_slice_in_dim(dconvw_all, me * ecw, ecw, axis=2)[:, None]
    put("even_conv_w", _adamw_nd(my_convw, even_conv_w, m_even_conv_w, v_even_conv_w, name="adam_even_conv_w"))
    my_fcw = lax.dynamic_slice_in_dim(dfcw_all, me * fcw, fcw, axis=3)
    put("ffn_conv_w", _adamw_nd(my_fcw, ffn_conv_w, m_ffn_conv_w, v_ffn_conv_w, name="adam_ffn_conv_w"))
    my_ps = lax.dynamic_slice_in_dim(dpscale_all, me * (D // NDEV), D // NDEV, axis=1)[:, None]
    put("odd_pool_scale", _adamw_nd(my_ps, odd_pool_scale, m_odd_pool_scale, v_odd_pool_scale, name="adam_pool_scale"))

    dmodc_sum = dmodc_all[0]
    for dev in range(1, NDEV):
        dmodc_sum = dmodc_sum + dmodc_all[dev]
    my_cols = lambda a: lax.dynamic_slice_in_dim(a, me * acols, acols, axis=a.ndim - 1)
    rows0 = jnp.concatenate([my_cols(dmod_all[:, 0]), my_cols(dmodc_sum), jnp.zeros((7, acols), F32)], axis=0)
    rows1 = jnp.concatenate([my_cols(dmod_all[:, 1]), jnp.zeros((8, acols), F32)], axis=0)
    d_ada = jnp.stack([_mm(mraw, rows, ta=True, silu_a=True, name=f"ada_dw_{l}", tm=512, tn=256, tk=16)
                       for l, rows in enumerate((rows0, rows1))])
    put("ada_w", _adamw_nd(d_ada[None], ada_w, m_ada_w, v_ada_w, name="adam_ada_w"))
    dscc_part = _mm(rows0, ada_w[0], tb=True, name="ada_dcctx", tm=16, tn=512, tk=256)
    (g_dscc,) = _all_gather([dscc_part[8:16]], name="gather_dcctx")
    put("c_ctx", _adamw_nd(g_dscc[:, 0:1, :].reshape(NDEV, D), c_ctx, m_c_ctx, v_c_ctx, name="adam_c_ctx",
                           silu_grad_of=c_ctx))

    d_win = jnp.concatenate([d_win_qkv, d_win_conv], axis=1)
    s_win = d_win.reshape(D, NDEV, -1).transpose(1, 0, 2)
    s_wout = d_wout.reshape(NDEV, D // NDEV, D)
    s_pool = dpool_w.astype(BF16).reshape(4, NDEV, PG // NDEV, PG).transpose(1, 0, 2, 3)
    s_up = jnp.stack([jnp.concatenate([dup0g, dup0v], axis=1), jnp.concatenate([dup1g, dup1v], axis=1)])
    s_up = s_up.reshape(2, D, NDEV, -1).transpose(2, 0, 1, 3)
    s_down = jnp.stack([ddown0, ddown1]).reshape(2, NDEV, DFF // NDEV, D).transpose(1, 0, 2, 3)
    r_win, r_wout, r_pool, r_up, r_down = _all_to_all([s_win, s_wout, s_pool, s_up, s_down], name="scatter_grads")
    put("even_w_in", _adamw_nd(r_win[:, None], even_w_in, m_even_w_in, v_even_w_in, name="adam_w_in"))
    put("even_w_out", _adamw_nd(r_wout[:, None], even_w_out, m_even_w_out, v_even_w_out, name="adam_w_out"))
    put("odd_pool_w", _adamw_nd(r_pool[:, None], odd_pool_w, m_odd_pool_w, v_odd_pool_w, name="adam_pool_w"))
    put("ffn_w_up", _adamw_nd(r_up, ffn_w_up, m_ffn_w_up, v_ffn_w_up, name="adam_w_up"))
    put("ffn_w_down", _adamw_nd(r_down, ffn_w_down, m_ffn_w_down, v_ffn_w_down, name="adam_w_down"))

    names = ["c_ctx", "ada_w", "ada_b", "mix_norm", "ffn_norm", "even_w_in", "even_q_gain", "even_k_gain",
             "even_conv_w", "even_w_out", "odd_pool_w", "odd_pool_scale", "ffn_w_up", "ffn_conv_w", "ffn_conv_b",
             "ffn_w_down"]
    result = [loss, grad_x[None]]
    for kind in ("grad_", "delta_", "new_m_", "new_v_"):
        result += [outs[kind + nm] for nm in names]
    return tuple(result)
```

```python
import functools
import math

import jax
import jax.numpy as jnp
from jax import lax
from jax.experimental import pallas as pl
from jax.experimental.pallas import tpu as pltpu

F32 = jnp.float32
BF16 = jnp.bfloat16

D = 1024
HD = 128
NQ = 4
NKV = 2
AW = NQ * HD
CW = D - AW
DFF = 2816
GRID_W = 64
ROPE_THETA = 10000.0
POOL_WINDOWS = (2, 4, 8, 16)
PG = D // 4
EPS = 1e-6
NDEV = 8
HALO = 8
MESH = pl.DeviceIdType.MESH

ADAM_LR = 0.001
ADAM_B1 = 0.9
ADAM_B2 = 0.999
ADAM_EPS = 1e-08
ADAM_WD = 0.01
ADAM_STEP = 10


def _pick(dim, prefs):
    for p in prefs:
        if dim % p == 0:
            return p
    return dim


def _params(*sem):
    return pltpu.CompilerParams(dimension_semantics=sem)


_NT = (((1,), (1,)), ((), ()))
_TN = (((0,), (0,)), ((), ()))
_SCALE = HD ** -0.5
_QSCALE = _SCALE * math.log2(math.e)
_LN2 = math.log(2.0)


def _mm(a_list, b, *, name, ta=False, tb=False, out_dtype=F32, silu_a=False, bias=None, tm=None, tn=None, tk=None):
    if not isinstance(a_list, (list, tuple)):
        a_list = [a_list]
    na = len(a_list)
    assert not (ta and na > 1)
    if ta:
        kdim, m = a_list[0].shape
        ks = [kdim]
    else:
        m = a_list[0].shape[0]
        ks = [a.shape[1] for a in a_list]
        kdim = sum(ks)
    n = b.shape[0] if tb else b.shape[1]
    assert (b.shape[1] if tb else b.shape[0]) == kdim
    kunit = math.gcd(*ks) if na > 1 else kdim
    tm = min(tm, m) if tm else _pick(m, (512, 256, 128, 64, 32, 16, 8))
    tn = min(tn, n) if tn else _pick(n, (512, 256, 128))
    tk = min(tk, kunit) if tk else _pick(kunit, (1024, 768, 512, 256, 128))
    assert m % tm == 0 and n % tn == 0 and all(k % tk == 0 for k in ks)
    nks = [k // tk for k in ks]
    starts = [sum(nks[:i]) for i in range(na)]
    nk = sum(nks)
    has_bias = bias is not None

    def body(*refs):
        a_refs = refs[:na]
        b_ref = refs[na]
        bias_ref = refs[na + 1] if has_bias else None
        o_ref = refs[na + 1 + has_bias]
        acc = refs[-1]
        k = pl.program_id(2)

        @pl.when(k == 0)
        def _():
            acc[...] = jnp.zeros_like(acc)

        bv = b_ref[...].astype(BF16)
        dn = (((0 if ta else 1,), (1 if tb else 0,)), ((), ()))
        for idx in range(na):
            def step(idx=idx):
                av = a_refs[idx][...]
                if silu_a:
                    av = av * jax.nn.sigmoid(av)
                acc[...] += lax.dot_general(av.astype(BF16), bv, dn, preferred_element_type=F32)
            if na == 1:
                step()
            else:
                pl.when((k >= starts[idx]) & (k < starts[idx] + nks[idx]))(step)

        @pl.when(k == nk - 1)
        def _():
            r = acc[...]
            if has_bias:
                r = r + bias_ref[...]
            o_ref[...] = r.astype(o_ref.dtype)

    in_specs = []
    for idx in range(na):
        if ta:
            in_specs.append(pl.BlockSpec((tk, tm), lambda i, j, k: (k, i)))
        else:
            lo, cnt = starts[idx], nks[idx]
            in_specs.append(pl.BlockSpec((tm, tk), lambda i, j, k, lo=lo, cnt=cnt: (i, jnp.clip(k - lo, 0, cnt - 1))))
    if tb:
        in_specs.append(pl.BlockSpec((tn, tk), lambda i, j, k: (j, k)))
    else:
        in_specs.append(pl.BlockSpec((tk, tn), lambda i, j, k: (k, j)))
    args = list(a_list) + [b]
    if has_bias:
        in_specs.append(pl.BlockSpec((1, tn), lambda i, j, k: (0, j)))
        args.append(bias)
    return pl.pallas_call(
        body, grid=(m // tm, n // tn, nk), in_specs=in_specs,
        out_specs=pl.BlockSpec((tm, tn), lambda i, j, k: (i, j)),
        out_shape=jax.ShapeDtypeStruct((m, n), out_dtype),
        scratch_shapes=[pltpu.VMEM((tm, tn), F32)], name=name,
        compiler_params=_params("parallel", "parallel", "arbitrary"))(*args)


def _mm_w(a_list, w, *, name, tb=False, tm=256, out_dtype=F32):
    if not isinstance(a_list, (list, tuple)):
        a_list = [a_list]
    na = len(a_list)
    m = a_list[0].shape[0]
    ks = [a.shape[1] for a in a_list]
    offs = [sum(ks[:i]) for i in range(na)]
    n = w.shape[0] if tb else w.shape[1]
    assert (w.shape[1] if tb else w.shape[0]) == sum(ks)
    tm = min(tm, m)
    assert m % tm == 0

    def body(*refs):
        a_refs, w_ref, o_ref = refs[:na], refs[na], refs[na + 1]
        acc = None
        for idx in range(na):
            av = a_refs[idx][...].astype(BF16)
            if tb:
                part = lax.dot_general(av, w_ref[:, offs[idx]:offs[idx] + ks[idx]], _NT, preferred_element_type=F32)
            else:
                part = jnp.dot(av, w_ref[offs[idx]:offs[idx] + ks[idx], :], preferred_element_type=F32)
            acc = part if acc is None else acc + part
        o_ref[...] = acc.astype(o_ref.dtype)

    in_specs = [pl.BlockSpec((tm, k), lambda i: (i, 0)) for k in ks] + [pl.BlockSpec(w.shape, lambda i: (0, 0))]
    return pl.pallas_call(
        body, grid=(m // tm,), in_specs=in_specs, out_specs=pl.BlockSpec((tm, n), lambda i: (i, 0)),
        out_shape=jax.ShapeDtypeStruct((m, n), out_dtype), name=name, compiler_params=_params("parallel"))(*a_list, w)


def _mm_tn(a, b, *, name, tk=1024, out_dtype=BF16):
    kdim, m = a.shape
    n = b.shape[1]
    assert b.shape[0] == kdim
    tk = min(tk, kdim)
    assert kdim % tk == 0
    nk = kdim // tk

    def body(a_ref, b_ref, o_ref, acc):
        k = pl.program_id(0)
        part = lax.dot_general(a_ref[...], b_ref[...], _TN, preferred_element_type=F32)

        @pl.when(k == 0)
        def _():
            acc[...] = part

        @pl.when(k > 0)
        def _():
            acc[...] += part

        @pl.when(k == nk - 1)
        def _():
            o_ref[...] = acc[...].astype(o_ref.dtype)

    return pl.pallas_call(
        body, grid=(nk,), in_specs=[pl.BlockSpec((tk, m), lambda k: (k, 0)), pl.BlockSpec((tk, n), lambda k: (k, 0))],
        out_specs=pl.BlockSpec((m, n), lambda k: (0, 0)), out_shape=jax.ShapeDtypeStruct((m, n), out_dtype),
        scratch_shapes=[pltpu.VMEM((m, n), F32)], name=name, compiler_params=_params("arbitrary"))(a, b)


def _vec(d, col=None):
    if col is None:
        return pl.BlockSpec((1, d), lambda i, *_: (0, 0))
    return pl.BlockSpec((1, d), col)


def _halo_specs(tm, width, nrows, colblk=0, row_off=0):
    r = tm // HALO
    off = row_off // HALO
    last = nrows // HALO - 1
    prev = pl.BlockSpec((HALO, width), lambda i, *_: (off + jnp.maximum(i * r - 1, 0), colblk))
    nxt = pl.BlockSpec((HALO, width), lambda i, *_: (off + jnp.minimum((i + 1) * r, last), colblk))
    return prev, nxt


def _ext(prev_ref, main_ref, next_ref, i, ni):
    p = jnp.where(i > 0, prev_ref[...], 0.0)
    n = jnp.where(i < ni - 1, next_ref[...], 0.0)
    return jnp.concatenate([p, main_ref[...], n], axis=0)


def _sh(ext, k, tm):
    if k == 0:
        return ext[HALO:HALO + tm]
    rows = ext.shape[0]
    return pltpu.roll(ext, (-k) % rows, axis=0)[HALO:HALO + tm]


def _roll_rows(v, k):
    rows = v.shape[0]
    return pltpu.roll(v, (-k) % rows, axis=0) if k % rows else v


def _conv3(ext, w_ref, tm):
    return _sh(ext, -1, tm) * w_ref[0:1, :] + _sh(ext, 0, tm) * w_ref[1:2, :] + _sh(ext, 1, tm) * w_ref[2:3, :]


def _colsum(v):
    return jnp.sum(v, axis=0, keepdims=True)


def _acc_out(ref, i, val):
    @pl.when(i == 0)
    def _():
        ref[...] = val

    @pl.when(i > 0)
    def _():
        ref[...] += val


def _sigmoid(v):
    return jax.nn.sigmoid(v)


def _norm_mod(x, gain, sc, sh, *, name, y=None, g=None, ymul=None, tm=256):
    n, d = x.shape
    has_res = y is not None
    has_mul = ymul is not None

    def body(*refs):
        it = iter(refs)
        x_ref = next(it)
        y_ref = next(it) if has_res else None
        g_ref = next(it) if has_res else None
        m_ref = next(it) if has_mul else None
        gain_ref, sc_ref, sh_ref = next(it), next(it), next(it)
        xo_ref = next(it) if has_res else None
        a_ref = next(it)
        xv = x_ref[...]
        if has_res:
            yv = y_ref[...]
            if has_mul:
                yv = yv * m_ref[...]
            xv = xv + g_ref[...] * yv
            xo_ref[...] = xv
        r = lax.rsqrt(jnp.mean(xv * xv, axis=-1, keepdims=True) + EPS)
        nrm = (xv * r) * gain_ref[...]
        a_ref[...] = (nrm * (1.0 + sc_ref[...]) + sh_ref[...]).astype(BF16)

    row = pl.BlockSpec((tm, d), lambda i: (i, 0))
    in_specs, args = [row], [x]
    if has_res:
        in_specs += [row, _vec(d)]
        args += [y, g]
    if has_mul:
        in_specs.append(_vec(d))
        args.append(ymul)
    in_specs += [_vec(d)] * 3
    args += [gain, sc, sh]
    out_specs, out_shape = [], []
    if has_res:
        out_specs.append(row)
        out_shape.append(jax.ShapeDtypeStruct((n, d), F32))
    out_specs.append(row)
    out_shape.append(jax.ShapeDtypeStruct((n, d), BF16))
    res = pl.pallas_call(body, grid=(n // tm,), in_specs=in_specs, out_specs=out_specs, out_shape=out_shape,
                         name=name, compiler_params=_params("parallel"))(*args)
    return res if has_res else res[0]


def _norm_mod_bwd(da, x, gain, sc, *, name, dres=None, tm=256):
    n, d = x.shape
    has_res = dres is not None

    def body(*refs):
        it = iter(refs)
        da_ref, x_ref = next(it), next(it)
        r_ref = next(it) if has_res else None
        gain_ref, sc_ref = next(it), next(it)
        dx_ref, dsh_ref, dsc_ref, dgn_ref = next(it), next(it), next(it), next(it)
        i = pl.program_id(0)
        xv = x_ref[...]
        dav = da_ref[...]
        r = lax.rsqrt(jnp.mean(xv * xv, axis=-1, keepdims=True) + EPS)
        xh = xv * r
        nrm = xh * gain_ref[...]
        dn = dav * (1.0 + sc_ref[...])
        dxh = dn * gain_ref[...]
        dx = r * (dxh - xh * jnp.mean(dxh * xh, axis=-1, keepdims=True))
        if has_res:
            dx = dx + r_ref[...]
        dx_ref[...] = dx
        _acc_out(dsh_ref, i, _colsum(dav))
        _acc_out(dsc_ref, i, _colsum(dav * nrm))
        _acc_out(dgn_ref, i, _colsum(dn * xh))

    row = pl.BlockSpec((tm, d), lambda i: (i, 0))
    in_specs, args = [row, row], [da, x]
    if has_res:
        in_specs.append(row)
        args.append(dres)
    in_specs += [_vec(d)] * 2
    args += [gain, sc]
    vec_shape = jax.ShapeDtypeStruct((1, d), F32)
    return pl.pallas_call(
        body, grid=(n // tm,), in_specs=in_specs, out_specs=[row, _vec(d), _vec(d), _vec(d)],
        out_shape=[jax.ShapeDtypeStruct((n, d), F32), vec_shape, vec_shape, vec_shape],
        name=name, compiler_params=_params("arbitrary"))(*args)


def _gate_bwd(dxo, y, g, *, name, tm=256):
    n, d = dxo.shape

    def body(dx_ref, y_ref, g_ref, dy_ref, dg_ref):
        i = pl.program_id(0)
        dxv = dx_ref[...]
        dy_ref[...] = (dxv * g_ref[...]).astype(BF16)
        _acc_out(dg_ref, i, _colsum(dxv * y_ref[...]))

    row = pl.BlockSpec((tm, d), lambda i: (i, 0))
    return pl.pallas_call(
        body, grid=(n // tm,), in_specs=[row, row, _vec(d)], out_specs=[row, _vec(d)],
        out_shape=[jax.ShapeDtypeStruct((n, d), BF16), jax.ShapeDtypeStruct((1, d), F32)],
        name=name, compiler_params=_params("arbitrary"))(dxo, y, g)


def _loss_head(x, z, g, tgt, *, name, tm=256):
    n, d = x.shape

    def body(x_ref, z_ref, g_ref, t_ref, dx_ref, loss_ref):
        i = pl.program_id(0)
        diff = (x_ref[...] + g_ref[...] * z_ref[...]) - t_ref[...]
        dx_ref[...] = diff * (1.0 / d)
        part = 0.5 * jnp.sum(jnp.mean(diff * diff, axis=-1, keepdims=True), axis=0, keepdims=True)
        _acc_out(loss_ref, i, jnp.broadcast_to(part, (1, 128)))

    row = pl.BlockSpec((tm, d), lambda i: (i, 0))
    return pl.pallas_call(
        body, grid=(n // tm,), in_specs=[row, row, _vec(d), row], out_specs=[row, _vec(128)],
        out_shape=[jax.ShapeDtypeStruct((n, d), F32), jax.ShapeDtypeStruct((1, 128), F32)],
        name=name, compiler_params=_params("arbitrary"))(x, z, g, tgt)


def _glu_fwd(u, cw, cb, *, name, tm=256, tc=256):
    n = u.shape[0]
    tm = min(tm, n)
    nc = DFF // tc
    ni = n // tm

    def body(g_ref, gp_ref, gn_ref, v_ref, cw_ref, cb_ref, h_ref):
        i = pl.program_id(0)
        gext = _ext(gp_ref, g_ref, gn_ref, i, ni)
        gc = _conv3(gext, cw_ref, tm) + cb_ref[...]
        h_ref[...] = (gc * _sigmoid(gc) * v_ref[...]).astype(BF16)

    prev = pl.BlockSpec((HALO, tc), lambda i, j: (jnp.maximum(i * (tm // HALO) - 1, 0), j))
    nxt = pl.BlockSpec((HALO, tc), lambda i, j: (jnp.minimum((i + 1) * (tm // HALO), n // HALO - 1), j))
    return pl.pallas_call(
        body, grid=(ni, nc),
        in_specs=[pl.BlockSpec((tm, tc), lambda i, j: (i, j)), prev, nxt,
                  pl.BlockSpec((tm, tc), lambda i, j: (i, nc + j)),
                  pl.BlockSpec((3, tc), lambda i, j: (0, j)), pl.BlockSpec((1, tc), lambda i, j: (0, j))],
        out_specs=pl.BlockSpec((tm, tc), lambda i, j: (i, j)),
        out_shape=jax.ShapeDtypeStruct((n, DFF), BF16), name=name,
        compiler_params=_params("parallel", "parallel"))(u, u, u, u, cw, cb)


def _glu_bwd(dh, u, cw, cb, *, name, tm=256, tc=256):
    n = u.shape[0]
    tm = min(tm, n)
    nc = DFF // tc
    ni = n // tm
    rows = tm + 2 * HALO

    def body(dh_ref, dhp_ref, dhn_ref, g_ref, gp_ref, gn_ref, v_ref, vp_ref, vn_ref, cw_ref, cb_ref,
             dg_ref, dv_ref, dcw_ref, dcb_ref):
        i = pl.program_id(1)
        gext = _ext(gp_ref, g_ref, gn_ref, i, ni)
        dhext = _ext(dhp_ref, dh_ref, dhn_ref, i, ni)
        vext = _ext(vp_ref, v_ref, vn_ref, i, ni)
        gc = (_roll_rows(gext, -1) * cw_ref[0:1, :] + gext * cw_ref[1:2, :] + _roll_rows(gext, 1) * cw_ref[2:3, :]
              + cb_ref[...])
        sg = _sigmoid(gc)
        dgc = dhext * vext * (sg * (1.0 + gc * (1.0 - sg)))
        dv_ref[...] = (dh_ref[...] * (gc[HALO:HALO + tm] * sg[HALO:HALO + tm])).astype(BF16)
        dgate = (_sh(dgc, 1, tm) * cw_ref[0:1, :] + _sh(dgc, 0, tm) * cw_ref[1:2, :] + _sh(dgc, -1, tm) * cw_ref[2:3, :])
        dg_ref[...] = dgate.astype(BF16)
        dgc_t = dgc[HALO:HALO + tm]
        dcw = jnp.concatenate([_colsum(dgc_t * _sh(gext, -1, tm)), _colsum(dgc_t * _sh(gext, 0, tm)),
                               _colsum(dgc_t * _sh(gext, 1, tm))], axis=0)
        _acc_out(dcw_ref, i, dcw)
        _acc_out(dcb_ref, i, _colsum(dgc_t))

    r = tm // HALO
    last = n // HALO - 1

    def trio(off):
        return [pl.BlockSpec((tm, tc), lambda j, i: (i, off + j)),
                pl.BlockSpec((HALO, tc), lambda j, i: (jnp.maximum(i * r - 1, 0), off + j)),
                pl.BlockSpec((HALO, tc), lambda j, i: (jnp.minimum((i + 1) * r, last), off + j))]

    del rows
    return pl.pallas_call(
        body, grid=(nc, ni),
        in_specs=trio(0) + trio(0) + trio(nc) + [pl.BlockSpec((3, tc), lambda j, i: (0, j)),
                                                 pl.BlockSpec((1, tc), lambda j, i: (0, j))],
        out_specs=[pl.BlockSpec((tm, tc), lambda j, i: (i, j)), pl.BlockSpec((tm, tc), lambda j, i: (i, j)),
                   pl.BlockSpec((3, tc), lambda j, i: (0, j)), pl.BlockSpec((1, tc), lambda j, i: (0, j))],
        out_shape=[jax.ShapeDtypeStruct((n, DFF), BF16), jax.ShapeDtypeStruct((n, DFF), BF16),
                   jax.ShapeDtypeStruct((3, DFF), F32), jax.ShapeDtypeStruct((1, DFF), F32)],
        name=name, compiler_params=_params("parallel", "arbitrary"))(dh, dh, dh, u, u, u, u, u, u, cw, cb)


def _rope_tables(n):
    rows = n // GRID_W
    row_ids = jnp.repeat(jnp.arange(rows), GRID_W).astype(F32)
    col_ids = jnp.tile(jnp.arange(GRID_W), rows).astype(F32)
    axis_dim = HD // 2
    inv_freq = jnp.power(ROPE_THETA, -jnp.arange(0, axis_dim, 2, dtype=F32) / axis_dim)
    ar = row_ids[:, None] * inv_freq
    ac = col_ids[:, None] * inv_freq
    cs = jnp.concatenate([jnp.cos(ar), jnp.cos(ar), jnp.cos(ac), jnp.cos(ac)], axis=1)
    sn = jnp.concatenate([-jnp.sin(ar), jnp.sin(ar), -jnp.sin(ac), jnp.sin(ac)], axis=1)
    return cs, sn


def _partner(v):
    lane = lax.broadcasted_iota(jnp.int32, v.shape, 1)
    return jnp.where((lane % 64) < 32, pltpu.roll(v, HD - 32, axis=1), pltpu.roll(v, 32, axis=1))


def _qkv_prep(p, q_gain, k_gain, cs, sn, *, name, has_q, kv_col, tm=256):
    n = p.shape[0]
    rope = cs is not None

    def body(*refs):
        it = iter(refs)
        q_ref = next(it) if has_q else None
        kv_ref = next(it)
        qg_ref, kg_ref = next(it), next(it)
        cs_ref = next(it) if rope else None
        sn_ref = next(it) if rope else None
        qo_ref = next(it) if has_q else None
        ko_ref, vo_ref = next(it), next(it)

        def norm_rope(xh, gain, mul=None):
            r = lax.rsqrt(jnp.mean(xh * xh, axis=-1, keepdims=True) + EPS)
            xn = (xh * r) * gain
            if rope:
                xn = xn * cs_ref[...] + _partner(xn) * sn_ref[...]
            if mul is not None:
                xn = xn * mul
            return xn.astype(BF16)

        if has_q:
            for h in range(NQ):
                qo_ref[h] = norm_rope(q_ref[:, h * HD:(h + 1) * HD], qg_ref[...], _QSCALE)
        for h in range(NKV):
            ko_ref[h] = norm_rope(kv_ref[:, h * HD:(h + 1) * HD], kg_ref[...])
            vo_ref[h] = kv_ref[:, (NKV + h) * HD:(NKV + h + 1) * HD].astype(BF16)

    in_specs, args = [], []
    if has_q:
        in_specs.append(pl.BlockSpec((tm, AW), lambda i: (i, 0)))
        args.append(p)
    in_specs += [pl.BlockSpec((tm, 2 * NKV * HD), lambda i: (i, kv_col)), _vec(HD), _vec(HD)]
    args += [p, q_gain, k_gain]
    if rope:
        in_specs += [pl.BlockSpec((tm, HD), lambda i: (i, 0))] * 2
        args += [cs, sn]
    out_specs, out_shape = [], []
    if has_q:
        out_specs.append(pl.BlockSpec((NQ, tm, HD), lambda i: (0, i, 0)))
        out_shape.append(jax.ShapeDtypeStruct((NQ, n, HD), BF16))
    out_specs += [pl.BlockSpec((NKV, tm, HD), lambda i: (0, i, 0))] * 2
    out_shape += [jax.ShapeDtypeStruct((NKV, n, HD), BF16)] * 2
    return pl.pallas_call(body, grid=(n // tm,), in_specs=in_specs, out_specs=out_specs, out_shape=out_shape,
                          name=name, compiler_params=_params("parallel"))(*args)


def _qkv_bwd(p, dq, dk, dv, q_gain, k_gain, cs, sn, *, name, has_q, kv_col, kv_row_off, tm=256):
    n = p.shape[0]
    rope = cs is not None
    rb = kv_row_off // tm

    def body(*refs):
        it = iter(refs)
        q_ref = next(it) if has_q else None
        kv_ref = next(it)
        dq_ref = next(it) if has_q else None
        dk_ref, dv_ref = next(it), next(it)
        qg_ref, kg_ref = next(it), next(it)
        cs_ref = next(it) if rope else None
        sn_ref = next(it) if rope else None
        dp_ref, dqg_ref, dkg_ref = next(it), next(it), next(it)
        i = pl.program_id(0)

        def back(xh, dout, gain):
            if rope:
                dout = dout * cs_ref[...] + _partner(dout * sn_ref[...])
            r = lax.rsqrt(jnp.mean(xh * xh, axis=-1, keepdims=True) + EPS)
            xhat = xh * r
            dxh = dout * gain
            dx = r * (dxh - xhat * jnp.mean(dxh * xhat, axis=-1, keepdims=True))
            return dx, _colsum(dout * xhat)

        dqg = jnp.zeros((1, HD), F32)
        dkg = jnp.zeros((1, HD), F32)
        if has_q:
            for h in range(NQ):
                dx, dg = back(q_ref[:, h * HD:(h + 1) * HD], dq_ref[h], qg_ref[...])
                dp_ref[:, h * HD:(h + 1) * HD] = dx.astype(BF16)
                dqg = dqg + dg
        else:
            dp_ref[:, 0:AW] = jnp.zeros((tm, AW), BF16)
        for h in range(NKV):
            dx, dg = back(kv_ref[:, h * HD:(h + 1) * HD], dk_ref[h], kg_ref[...])
            dp_ref[:, AW + h * HD:AW + (h + 1) * HD] = dx.astype(BF16)
            dkg = dkg + dg
            dp_ref[:, AW + (NKV + h) * HD:AW + (NKV + h + 1) * HD] = dv_ref[h].astype(BF16)
        _acc_out(dqg_ref, i, dqg)
        _acc_out(dkg_ref, i, dkg)

    in_specs, args = [], []
    if has_q:
        in_specs.append(pl.BlockSpec((tm, AW), lambda i: (i, 0)))
        args.append(p)
    in_specs.append(pl.BlockSpec((tm, 2 * NKV * HD), lambda i: (i, kv_col)))
    args.append(p)
    if has_q:
        in_specs.append(pl.BlockSpec((NQ, tm, HD), lambda i: (0, i, 0)))
        args.append(dq)
    in_specs += [pl.BlockSpec((NKV, tm, HD), lambda i: (0, rb + i, 0))] * 2 + [_vec(HD), _vec(HD)]
    args += [dk, dv, q_gain, k_gain]
    if rope:
        in_specs += [pl.BlockSpec((tm, HD), lambda i: (i, 0))] * 2
        args += [cs, sn]
    return pl.pallas_call(
        body, grid=(n // tm,), in_specs=in_specs,
        out_specs=[pl.BlockSpec((tm, D), lambda i: (i, 0)), _vec(HD), _vec(HD)],
        out_shape=[jax.ShapeDtypeStruct((n, D), BF16), jax.ShapeDtypeStruct((1, HD), F32),
                   jax.ShapeDtypeStruct((1, HD), F32)],
        name=name, compiler_params=_params("arbitrary"))(*args)


def _conv_gate_fwd(p, o, conv_w, *, name, tm=256):
    n = p.shape[0]
    ni = n // tm

    def body(gb_ref, gc_ref, gcp_ref, gcn_ref, xi_ref, xip_ref, xin_ref, o_ref, w_ref, cat_ref):
        i = pl.program_id(0)
        hext = _ext(gcp_ref, gc_ref, gcn_ref, i, ni) * _ext(xip_ref, xi_ref, xin_ref, i, ni)
        cat_ref[:, 0:AW] = o_ref[...].astype(BF16)
        cat_ref[:, AW:D] = (gb_ref[...] * _conv3(hext, w_ref, tm)).astype(BF16)

    gcp, gcn = _halo_specs(tm, CW, n, colblk=3)
    xip, xin = _halo_specs(tm, CW, n, colblk=4)
    return pl.pallas_call(
        body, grid=(ni,),
        in_specs=[pl.BlockSpec((tm, CW), lambda i: (i, 2)), pl.BlockSpec((tm, CW), lambda i: (i, 3)), gcp, gcn,
                  pl.BlockSpec((tm, CW), lambda i: (i, 4)), xip, xin, pl.BlockSpec((tm, AW), lambda i: (i, 0)),
                  pl.BlockSpec((3, CW), lambda i: (0, 0))],
        out_specs=pl.BlockSpec((tm, D), lambda i: (i, 0)), out_shape=jax.ShapeDtypeStruct((n, D), BF16),
        name=name, compiler_params=_params("parallel"))(p, p, p, p, p, p, p, o, conv_w)


def _conv_gate_bwd(dcat, p, conv_w, *, name, tm=256):
    n = p.shape[0]
    ni = n // tm

    def body(dc_ref, dcp_ref, dcn_ref, gb_ref, gbp_ref, gbn_ref, gc_ref, gcp_ref, gcn_ref, xi_ref, xip_ref, xin_ref,
             w_ref, dp_ref, dw_ref):
        i = pl.program_id(0)
        gcext = _ext(gcp_ref, gc_ref, gcn_ref, i, ni)
        xiext = _ext(xip_ref, xi_ref, xin_ref, i, ni)
        hext = gcext * xiext
        dcv = _ext(dcp_ref, dc_ref, dcn_ref, i, ni) * _ext(gbp_ref, gb_ref, gbn_ref, i, ni)
        dp_ref[:, 0:CW] = (dc_ref[...] * _conv3(hext, w_ref, tm)).astype(BF16)
        dh = _sh(dcv, 1, tm) * w_ref[0:1, :] + _sh(dcv, 0, tm) * w_ref[1:2, :] + _sh(dcv, -1, tm) * w_ref[2:3, :]
        dp_ref[:, CW:2 * CW] = (dh * xi_ref[...]).astype(BF16)
        dp_ref[:, 2 * CW:3 * CW] = (dh * gc_ref[...]).astype(BF16)
        dcv_t = dcv[HALO:HALO + tm]
        dw = jnp.concatenate([_colsum(dcv_t * _sh(hext, -1, tm)), _colsum(dcv_t * _sh(hext, 0, tm)),
                              _colsum(dcv_t * _sh(hext, 1, tm))], axis=0)
        _acc_out(dw_ref, i, dw)

    def trio(colblk):
        prev, nxt = _halo_specs(tm, CW, n, colblk=colblk)
        return [pl.BlockSpec((tm, CW), lambda i: (i, colblk)), prev, nxt]

    return pl.pallas_call(
        body, grid=(ni,), in_specs=trio(1) + trio(2) + trio(3) + trio(4) + [pl.BlockSpec((3, CW), lambda i: (0, 0))],
        out_specs=[pl.BlockSpec((tm, 3 * CW), lambda i: (i, 0)), pl.BlockSpec((3, CW), lambda i: (0, 0))],
        out_shape=[jax.ShapeDtypeStruct((n, 3 * CW), BF16), jax.ShapeDtypeStruct((3, CW), F32)],
        name=name, compiler_params=_params("arbitrary"))(dcat, dcat, dcat, p, p, p, p, p, p, p, p, p, conv_w)


def _attn_fwd(q, k, v, *, name, bq=128):
    n = q.shape[1]
    t = k.shape[1]
    bq = min(bq, n)

    def body(q_ref, k_ref, v_ref, o_ref, lse_ref):
        q2 = q_ref[...].reshape(2 * bq, HD)
        s = lax.dot_general(q2, k_ref[0], _NT, preferred_element_type=F32)
        m = jnp.max(s, axis=-1, keepdims=True)
        pv = jnp.exp2(s - m)
        l = jnp.sum(pv, axis=-1, keepdims=True)
        out = jnp.dot(pv.astype(BF16), v_ref[0], preferred_element_type=F32) / l
        o_ref[:, 0:HD] = out[0:bq]
        o_ref[:, HD:2 * HD] = out[bq:2 * bq]
        lse_ref[...] = (m + jnp.log2(l)).reshape(2, bq, 1)

    kspec = pl.BlockSpec((1, t, HD), lambda h, i: (h, 0, 0))
    return pl.pallas_call(
        body, grid=(NKV, n // bq),
        in_specs=[pl.BlockSpec((2, bq, HD), lambda h, i: (h, i, 0)), kspec, kspec],
        out_specs=[pl.BlockSpec((bq, 2 * HD), lambda h, i: (i, h)), pl.BlockSpec((2, bq, 1), lambda h, i: (h, i, 0))],
        out_shape=[jax.ShapeDtypeStruct((n, AW), F32), jax.ShapeDtypeStruct((NQ, n, 1), F32)],
        name=name, compiler_params=_params("parallel", "parallel"))(q, k, v)


def _attn_bwd_prep(dcat, o, *, name, tm=256):
    n = o.shape[0]

    def body(dc_ref, o_ref, do_ref, dl_ref):
        for h in range(NQ):
            dh = dc_ref[:, h * HD:(h + 1) * HD]
            do_ref[h] = dh.astype(BF16)
            dl_ref[h] = jnp.sum(dh * o_ref[:, h * HD:(h + 1) * HD], axis=-1, keepdims=True)

    return pl.pallas_call(
        body, grid=(n // tm,),
        in_specs=[pl.BlockSpec((tm, AW), lambda i: (i, 0)), pl.BlockSpec((tm, AW), lambda i: (i, 0))],
        out_specs=[pl.BlockSpec((NQ, tm, HD), lambda i: (0, i, 0)), pl.BlockSpec((NQ, tm, 1), lambda i: (0, i, 0))],
        out_shape=[jax.ShapeDtypeStruct((NQ, n, HD), BF16), jax.ShapeDtypeStruct((NQ, n, 1), F32)],
        name=name, compiler_params=_params("parallel"))(dcat, o)


def _attn_bwd(q, k, v, do, lse, delta, *, name, bq=256):
    n = q.shape[1]
    t = k.shape[1]
    bq = min(bq, n)

    def body(q_ref, k_ref, v_ref, do_ref, lse_ref, dl_ref, dq_ref, dk_ref, dv_ref):
        qi = pl.program_id(1)
        q2 = q_ref[...].reshape(2 * bq, HD)
        do2 = do_ref[...].reshape(2 * bq, HD)
        s = lax.dot_general(q2, k_ref[0], _NT, preferred_element_type=F32)
        pv = jnp.exp2(s - lse_ref[...].reshape(2 * bq, 1))
        dp = lax.dot_general(do2, v_ref[0], _NT, preferred_element_type=F32)
        ds = (pv * (dp - dl_ref[...].reshape(2 * bq, 1))).astype(BF16)
        dq_ref[...] = (jnp.dot(ds, k_ref[0], preferred_element_type=F32) * _SCALE).reshape(2, bq, HD)
        dk_part = lax.dot_general(ds, q2, _TN, preferred_element_type=F32) * _LN2
        dv_part = lax.dot_general(pv.astype(BF16), do2, _TN, preferred_element_type=F32)

        @pl.when(qi == 0)
        def _():
            dk_ref[0] = dk_part
            dv_ref[0] = dv_part

        @pl.when(qi > 0)
        def _():
            dk_ref[0] += dk_part
            dv_ref[0] += dv_part

    qspec = pl.BlockSpec((2, bq, HD), lambda h, i: (h, i, 0))
    kspec = pl.BlockSpec((1, t, HD), lambda h, i: (h, 0, 0))
    sspec = pl.BlockSpec((2, bq, 1), lambda h, i: (h, i, 0))
    return pl.pallas_call(
        body, grid=(NKV, n // bq), in_specs=[qspec, kspec, kspec, qspec, sspec, sspec], out_specs=[qspec, kspec, kspec],
        out_shape=[jax.ShapeDtypeStruct((NQ, n, HD), F32), jax.ShapeDtypeStruct((NKV, t, HD), F32),
                   jax.ShapeDtypeStruct((NKV, t, HD), F32)],
        name=name, compiler_params=_params("parallel", "arbitrary"))(q, k, v, do, lse, delta)


def _window_sums(ext, w):
    s, step = ext, 1
    while step < w:
        s = s + _roll_rows(s, step)
        step *= 2
    return s


def _pool_counts(i, tm, n, w, rows, first):
    t = i * tm - HALO + first + lax.broadcasted_iota(jnp.int32, (rows, 1), 0)
    lo = jnp.clip(t - w // 2, 0, n)
    hi = jnp.clip(t + w - w // 2, 0, n)
    return jnp.maximum(hi - lo, 1).astype(F32)


def _norm_mod_ext(xext, gain_ref, sc_ref, sh_ref, i, tm, n):
    rows = xext.shape[0]
    t = i * tm - HALO + lax.broadcasted_iota(jnp.int32, (rows, 1), 0)
    inside = (t >= 0) & (t < n)
    r = lax.rsqrt(jnp.mean(xext * xext, axis=-1, keepdims=True) + EPS)
    xh = xext * r
    a = (xh * gain_ref[...]) * (1.0 + sc_ref[...]) + sh_ref[...]
    return jnp.where(inside, a, 0.0), r, xh


def _pool_fwd(x, y, g, gain, sc, sh, pool_w, *, name, tm=256):
    n, d = x.shape
    ni = n // tm

    def body(x_ref, xp_ref, xn_ref, y_ref, yp_ref, yn_ref, g_ref, gain_ref, sc_ref, sh_ref, w_ref, xo_ref, o_ref):
        i = pl.program_id(0)
        xext = _ext(xp_ref, x_ref, xn_ref, i, ni) + g_ref[...] * _ext(yp_ref, y_ref, yn_ref, i, ni)
        xo_ref[...] = xext[HALO:HALO + tm]
        aext, _, _ = _norm_mod_ext(xext, gain_ref, sc_ref, sh_ref, i, tm, n)
        for gi, w in enumerate(POOL_WINDOWS):
            ag = aext[:, gi * PG:(gi + 1) * PG]
            mean = _sh(_window_sums(ag, w), -(w // 2), tm) / _pool_counts(i, tm, n, w, tm, HALO)
            pooled = mean - ag[HALO:HALO + tm]
            o_ref[:, gi * PG:(gi + 1) * PG] = jnp.dot(pooled.astype(BF16), w_ref[gi], preferred_element_type=F32)

    row = pl.BlockSpec((tm, d), lambda i: (i, 0))
    prev, nxt = _halo_specs(tm, d, n)
    return pl.pallas_call(
        body, grid=(ni,),
        in_specs=[row, prev, nxt, row, prev, nxt, _vec(d), _vec(d), _vec(d), _vec(d),
                  pl.BlockSpec((4, PG, PG), lambda i: (0, 0, 0))],
        out_specs=[row, row], out_shape=[jax.ShapeDtypeStruct((n, d), F32)] * 2,
        name=name, compiler_params=_params("parallel"))(x, x, x, y, y, y, g, gain, sc, sh, pool_w)


def _pool_bwd(dxo, mixed, x, g, scale, gain, sc, sh, pool_w, *, name, tm=256):
    n, d = x.shape
    ni = n // tm
    def body(dx_ref, dxp_ref, dxn_ref, mx_ref, x_ref, xp_ref, xn_ref, g_ref, s_ref, gain_ref, sc_ref, sh_ref, w_ref,
             dxi_ref, dw_ref, dg_ref, dsl_ref, dsh_ref, dsc_ref, dgn_ref):
        i = pl.program_id(0)
        dxo_t = dx_ref[...]
        mixed_t = mx_ref[...]
        dy_t = dxo_t * g_ref[...]
        _acc_out(dg_ref, i, _colsum(dxo_t * (mixed_t * s_ref[...])))
        _acc_out(dsl_ref, i, _colsum(dy_t * mixed_t))
        dmixed = (_ext(dxp_ref, dx_ref, dxn_ref, i, ni) * g_ref[...]) * s_ref[...]
        xext = _ext(xp_ref, x_ref, xn_ref, i, ni)
        aext, rext, xhext = _norm_mod_ext(xext, gain_ref, sc_ref, sh_ref, i, tm, n)
        rows = tm + 2 * HALO
        da_parts = []
        for gi, w in enumerate(POOL_WINDOWS):
            sl = slice(gi * PG, (gi + 1) * PG)
            ag = aext[:, sl]
            mean = _sh(_window_sums(ag, w), -(w // 2), tm) / _pool_counts(i, tm, n, w, tm, HALO)
            pooled = (mean - ag[HALO:HALO + tm]).astype(BF16)
            dmg = dmixed[:, sl].astype(BF16)
            dwg = lax.dot_general(pooled, dmixed[HALO:HALO + tm, sl].astype(BF16), _TN, preferred_element_type=F32)

            @pl.when(i == 0)
            def _(dwg=dwg, gi=gi):
                dw_ref[gi] = dwg

            @pl.when(i > 0)
            def _(dwg=dwg, gi=gi):
                dw_ref[gi] += dwg

            dpl = lax.dot_general(dmg, w_ref[gi], _NT, preferred_element_type=F32)
            e = dpl / _pool_counts(i, tm, n, w, rows, 0)
            da_parts.append(_sh(_window_sums(e, w), 1 - w // 2, tm) - dpl[HALO:HALO + tm])
        da = jnp.concatenate(da_parts, axis=1)
        r = rext[HALO:HALO + tm]
        xh = xhext[HALO:HALO + tm]
        nrm = xh * gain_ref[...]
        dn = da * (1.0 + sc_ref[...])
        dxh = dn * gain_ref[...]
        dxi_ref[...] = dxo_t + r * (dxh - xh * jnp.mean(dxh * xh, axis=-1, keepdims=True))
        _acc_out(dsh_ref, i, _colsum(da))
        _acc_out(dsc_ref, i, _colsum(da * nrm))
        _acc_out(dgn_ref, i, _colsum(dn * xh))

    row = pl.BlockSpec((tm, d), lambda i: (i, 0))
    prev, nxt = _halo_specs(tm, d, n)
    wspec = pl.BlockSpec((4, PG, PG), lambda i: (0, 0, 0))
    vshape = jax.ShapeDtypeStruct((1, d), F32)
    return pl.pallas_call(
        body, grid=(ni,),
        in_specs=[row, prev, nxt, row, row, prev, nxt] + [_vec(d)] * 5 + [wspec],
        out_specs=[row, wspec] + [_vec(d)] * 5,
        out_shape=[jax.ShapeDtypeStruct((n, d), F32), jax.ShapeDtypeStruct((4, PG, PG), F32)] + [vshape] * 5,
        name=name, compiler_params=_params("arbitrary"))(dxo, dxo, dxo, mixed, x, x, x, g, scale, gain, sc, sh, pool_w)


def _adamw(gparts, w, m, v, *, name, silu_grad_of=None):
    nparts, r, c = gparts.shape
    tr = _pick(r, (256, 128, 64, 32, 16, 8))
    has_c = silu_grad_of is not None

    def body(*refs):
        it = iter(refs)
        gp_ref, w_ref, m_ref, v_ref = next(it), next(it), next(it), next(it)
        c_ref = next(it) if has_c else None
        g_ref, d_ref, mo_ref, vo_ref = next(it), next(it), next(it), next(it)
        g = gp_ref[0].astype(F32)
        for p in range(1, nparts):
            g = g + gp_ref[p].astype(F32)
        if has_c:
            cv = c_ref[...]
            sg = _sigmoid(cv)
            g = g * (sg * (1.0 + cv * (1.0 - sg)))
        g_ref[...] = g
        mn = ADAM_B1 * m_ref[...] + (1.0 - ADAM_B1) * g
        vn = ADAM_B2 * v_ref[...] + (1.0 - ADAM_B2) * (g * g)
        m_hat = mn / (1.0 - ADAM_B1 ** ADAM_STEP)
        v_hat = vn / (1.0 - ADAM_B2 ** ADAM_STEP)
        d_ref[...] = -ADAM_LR * (m_hat / (jnp.sqrt(v_hat) + ADAM_EPS) + ADAM_WD * w_ref[...])
        mo_ref[...] = mn
        vo_ref[...] = vn

    row = pl.BlockSpec((tr, c), lambda i: (i, 0))
    in_specs = [pl.BlockSpec((nparts, tr, c), lambda i: (0, i, 0)), row, row, row]
    args = [gparts, w, m, v]
    if has_c:
        in_specs.append(row)
        args.append(silu_grad_of)
    return pl.pallas_call(
        body, grid=(r // tr,), in_specs=in_specs, out_specs=[row] * 4,
        out_shape=[jax.ShapeDtypeStruct((r, c), F32)] * 4, name=name, compiler_params=_params("parallel"))(*args)


def _adamw_nd(gparts, w, m, v, *, name, silu_grad_of=None):
    shape = w.shape
    c = shape[-1]
    r = math.prod(shape[:-1]) if len(shape) > 1 else 1
    rs = lambda a: a.reshape(r, c)
    res = _adamw(gparts.reshape(gparts.shape[0], r, c), rs(w), rs(m), rs(v), name=name,
                 silu_grad_of=None if silu_grad_of is None else rs(silu_grad_of))
    return [a.reshape(shape) for a in res]


def _place():
    return lax.axis_index("x"), lax.axis_index("y"), lax.axis_index("c")


def _all_gather(arrs, *, name):
    k_arr = len(arrs)

    def body(*refs):
        ins = refs[:k_arr]
        outs = refs[k_arr:2 * k_arr]
        send_sems, recv_sems, local_sems = refs[2 * k_arr:]
        x, y, c = _place()
        me, sibling = (x, y, c), (x, y, 1 - c)
        chips = [(1 - x, y), (x, 1 - y), (1 - x, 1 - y)]

        def slot(a, px, py, pc):
            return outs[a].at[4 * px + 2 * py + pc]

        def copy(a, s, block, to, src=None):
            return pltpu.make_async_remote_copy(
                src_ref=slot(a, *block) if src is None else src, dst_ref=slot(a, *block),
                send_sem=send_sems.at[a, s], recv_sem=recv_sems.at[a, s], device_id=to, device_id_type=MESH)

        mine = [pltpu.make_async_copy(ins[a], slot(a, *me), local_sems.at[a]) for a in range(k_arr)]
        for cp in mine:
            cp.start()
        first = []
        for a in range(k_arr):
            first.append(copy(a, 0, me, sibling, src=ins[a]))
            first += [copy(a, 1 + j, me, (*chip, c), src=ins[a]) for j, chip in enumerate(chips)]
        for cp in first:
            cp.start()
        passed = []
        for j, chip in enumerate(chips):
            for a in range(k_arr):
                copy(a, 1 + j, (*chip, c), me).wait_recv()
                fw = copy(a, 4 + j, (*chip, c), sibling)
                fw.start()
                passed.append(fw)
        for a in range(k_arr):
            copy(a, 0, sibling, me).wait_recv()
            for j, chip in enumerate(chips):
                copy(a, 4 + j, (*chip, 1 - c), me).wait_recv()
        for cp in first + passed:
            cp.wait_send()
        for cp in mine:
            cp.wait()

    any_spec = pl.BlockSpec(memory_space=pl.ANY)
    return pl.pallas_call(
        body, in_specs=[any_spec] * k_arr, out_specs=[any_spec] * k_arr,
        out_shape=[jax.ShapeDtypeStruct((NDEV,) + a.shape, a.dtype) for a in arrs],
        scratch_shapes=[pltpu.SemaphoreType.DMA((k_arr, 7)), pltpu.SemaphoreType.DMA((k_arr, 7)),
                        pltpu.SemaphoreType.DMA((k_arr,))],
        name=name)(*arrs)


def _all_to_all(arrs, *, name):
    k_arr = len(arrs)

    def body(*refs):
        ins = refs[:k_arr]
        outs = refs[k_arr:2 * k_arr]
        send_sems, recv_sems, local_sems = refs[2 * k_arr:]
        x, y, c = _place()
        me = 4 * x + 2 * y + c
        copies = []
        for a in range(k_arr):
            cp = pltpu.make_async_copy(ins[a].at[me], outs[a].at[me], local_sems.at[a])
            cp.start()
            copies.append(cp)
        remote = []
        for rel in range(1, NDEV):
            px, py, pc = x ^ (rel >> 2), y ^ ((rel >> 1) & 1), c ^ (rel & 1)
            peer = 4 * px + 2 * py + pc
            for a in range(k_arr):
                cp = pltpu.make_async_remote_copy(
                    src_ref=ins[a].at[peer], dst_ref=outs[a].at[me], send_sem=send_sems.at[a, rel - 1],
                    recv_sem=recv_sems.at[a, rel - 1], device_id=(px, py, pc), device_id_type=MESH)
                cp.start()
                remote.append((cp, a, rel, peer))
        for cp, a, rel, peer in remote:
            pltpu.make_async_remote_copy(
                src_ref=ins[a].at[peer], dst_ref=outs[a].at[peer], send_sem=send_sems.at[a, rel - 1],
                recv_sem=recv_sems.at[a, rel - 1], device_id=(x, y, c), device_id_type=MESH).wait_recv()
        for cp, a, rel, peer in remote:
            cp.wait_send()
        for cp in copies:
            cp.wait()

    any_spec = pl.BlockSpec(memory_space=pl.ANY)
    return pl.pallas_call(
        body, in_specs=[any_spec] * k_arr, out_specs=[any_spec] * k_arr,
        out_shape=[jax.ShapeDtypeStruct(a.shape, a.dtype) for a in arrs],
        scratch_shapes=[pltpu.SemaphoreType.DMA((k_arr, 7)), pltpu.SemaphoreType.DMA((k_arr, 7)),
                        pltpu.SemaphoreType.DMA((k_arr,))],
        name=name)(*arrs)


def _ffn_fwd(x_in, y, g, ymul, gain, sc, sh, w_up, cw, cb, w_down, tag):
    xr, f = _norm_mod(x_in, gain, sc, sh, y=y, g=g, ymul=ymul, name=f"ffn_norm_{tag}")
    u = _mm_w(f, w_up, name=f"ffn_up_{tag}")
    hmid = _glu_fwd(u, cw, cb, name=f"ffn_glu_{tag}", tm=1024)
    z = _mm_w(hmid, w_down, name=f"ffn_down_{tag}")
    return xr, f, u, hmid, z


def _ffn_bwd(dxo, xr, f, u, hmid, z, g2, gain, sc, w_up, cw, cb, w_down, tag):
    dz, dg2 = _gate_bwd(dxo, z, g2, name=f"ffn_gate_bwd_{tag}")
    dh = _mm_w(dz, w_down, tb=True, name=f"ffn_down_dx_{tag}")
    d_wdown = _mm_tn(hmid, dz, name=f"ffn_down_dw_{tag}")
    dug, duv, dcw, dcb = _glu_bwd(dh, u, cw, cb, name=f"ffn_glu_bwd_{tag}", tm=1024)
    df = _mm_w([dug, duv], w_up, tb=True, name=f"ffn_up_dx_{tag}")
    d_wup_g = _mm_tn(f, dug, name=f"ffn_up_dwg_{tag}")
    d_wup_v = _mm_tn(f, duv, name=f"ffn_up_dwv_{tag}")
    dxr, dsh2, dsc2, dgain = _norm_mod_bwd(df, xr, gain, sc, dres=dxo, name=f"ffn_norm_bwd_{tag}")
    return dxr, (d_wup_g, d_wup_v, d_wdown, dcw, dcb), (dsh2, dsc2, dg2, dgain)


def _split6(mod):
    return [mod[j * D:(j + 1) * D][None, :] for j in range(6)]


def _row(v):
    return v.reshape(1, -1)


def kernel(x, c, ctx, c_ctx, ada_w, ada_b, mix_norm, ffn_norm, even_w_in, even_q_gain, even_k_gain, even_conv_w, even_w_out, odd_pool_w, odd_pool_scale, ffn_w_up, ffn_conv_w, ffn_conv_b, ffn_w_down, loss_target, m_c_ctx, m_ada_w, m_ada_b, m_mix_norm, m_ffn_norm, m_even_w_in, m_even_q_gain, m_even_k_gain, m_even_conv_w, m_even_w_out, m_odd_pool_w, m_odd_pool_scale, m_ffn_w_up, m_ffn_conv_w, m_ffn_conv_b, m_ffn_w_down, v_c_ctx, v_ada_w, v_ada_b, v_mix_norm, v_ffn_norm, v_even_w_in, v_even_q_gain, v_even_k_gain, v_even_conv_w, v_even_w_out, v_odd_pool_w, v_odd_pool_scale, v_ffn_w_up, v_ffn_conv_w, v_ffn_conv_b, v_ffn_w_down):
    n = x.shape[1]
    lc = ctx.shape[1]
    me = 4 * lax.axis_index("x") + 2 * lax.axis_index("y") + lax.axis_index("c")
    xs, ctxs, tgt = x[0], ctx[0], loss_target[0]
    acols = ada_w.shape[2]

    small = jnp.concatenate([even_conv_w.reshape(-1), ffn_conv_w.reshape(-1), odd_pool_scale.reshape(-1)])
    nsmall = small.shape[0]
    small = jnp.pad(small, (0, (-nsmall) % 1024)).reshape(-1, 128)
    c_rows = jnp.pad(c, ((0, 7), (0, 0)))
    g_c, g_win, g_wout, g_pool, g_up, g_down, g_small = _all_gather(
        [c_rows, even_w_in[0].astype(BF16), even_w_out[0].astype(BF16), odd_pool_w[0].astype(BF16),
         ffn_w_up.astype(BF16), ffn_w_down.astype(BF16), small], name="gather_weights")
    w_in = g_win.transpose(1, 0, 2).reshape(D, -1)
    w_out = g_wout.reshape(D, D)
    pool_w = g_pool.transpose(1, 0, 2, 3).reshape(4, PG, PG)
    w_up = [g_up[:, l].transpose(1, 0, 2).reshape(D, 2 * DFF) for l in range(2)]
    w_down = [g_down[:, l].reshape(DFF, D) for l in range(2)]
    g_small = g_small.reshape(NDEV, -1)
    ecw = even_conv_w.shape[2]
    fcw = ffn_conv_w.shape[2]
    conv_w = g_small[:, :3 * ecw].reshape(NDEV, 3, ecw).transpose(1, 0, 2).reshape(3, CW)
    o1 = 3 * ecw
    fconv_w = g_small[:, o1:o1 + 6 * fcw].reshape(NDEV, 2, 3, fcw).transpose(1, 2, 0, 3).reshape(2, 3, DFF)
    o2 = o1 + 6 * fcw
    pool_scale = g_small[:, o2:o2 + D // NDEV].reshape(1, D)

    mraw = jnp.concatenate([g_c[:, 0, :], c_ctx[None, :], jnp.zeros((7, D), F32)], axis=0)
    my_bias = lax.dynamic_slice_in_dim(ada_b, me * acols, acols, axis=1)
    modp = jnp.stack([_mm(mraw, ada_w[l], silu_a=True, bias=my_bias[l:l + 1], name=f"ada_proj_{l}", tm=16, tn=256)
                      for l in range(2)])
    (g_mod,) = _all_gather([modp], name="gather_mod")
    mod_rows = g_mod.transpose(1, 2, 0, 3).reshape(2, 16, 6 * D)
    mod = lax.dynamic_index_in_dim(mod_rows, me, axis=1, keepdims=False)
    sh1, sc1, g1, sh2, sc2, g2 = _split6(mod[0])
    sh1b, sc1b, g1b, sh2b, sc2b, g2b = _split6(mod[1])
    csh1, csc1 = _split6(mod_rows[0, 8])[:2]
    mixn = [_row(mix_norm[l]) for l in range(2)]
    ffnn = [_row(ffn_norm[l]) for l in range(2)]
    qg, kg = _row(even_q_gain[0]), _row(even_k_gain[0])
    fcb = [_row(ffn_conv_b[l]) for l in range(2)]

    cs_t, sn_t = _rope_tables(n)
    a_lat = _norm_mod(xs, mixn[0], sc1, sh1, name="mix0_norm")
    a_ctx = _norm_mod(ctxs, mixn[0], csc1, csh1, name="mix0_norm_ctx")
    p_lat = _mm_w(a_lat, w_in, name="in_proj")
    p_ctx = _mm(a_ctx, w_in[:, AW:AW + 4 * HD], name="in_proj_ctx", tm=256, tn=512, tk=1024)
    q_r, k_lat, v_lat = _qkv_prep(p_lat, qg, kg, cs_t, sn_t, has_q=True, kv_col=1, name="qkv_prep")
    k_ctx, v_ctx = _qkv_prep(p_ctx, qg, kg, None, None, has_q=False, kv_col=0, name="qkv_prep_ctx")
    k_all = jnp.concatenate([k_ctx, k_lat], axis=1)
    v_all = jnp.concatenate([v_ctx, v_lat], axis=1)
    o_attn, lse = _attn_fwd(q_r, k_all, v_all, name="attn_fwd")
    cat = _conv_gate_fwd(p_lat, o_attn, conv_w, name="conv_gate")
    y0 = _mm_w(cat, w_out, name="out_proj", tm=512)
    x1, f0, u0, h0, z0 = _ffn_fwd(xs, y0, g1, None, ffnn[0], sc2, sh2, w_up[0], fconv_w[0], fcb[0], w_down[0], "l0")

    x2, mixed = _pool_fwd(x1, z0, g2, mixn[1], sc1b, sh1b, pool_w, name="pool_fwd")
    x3, f1, u1, h1, z1 = _ffn_fwd(x2, mixed, g1b, pool_scale, ffnn[1], sc2b, sh2b, w_up[1], fconv_w[1], fcb[1],
                                  w_down[1], "l1")
    dx4, loss_part = _loss_head(x3, z1, g2b, tgt, name="loss_head")
    loss = lax.psum(loss_part[0, 0], ("x", "y", "c"))

    dx3, (dup1g, dup1v, ddown1, dfcw1, dfcb1), (dsh2b, dsc2b, dg2b, dffn1) = _ffn_bwd(
        dx4, x3, f1, u1, h1, z1, g2b, ffnn[1], sc2b, w_up[1], fconv_w[1], fcb[1], w_down[1], "l1")
    dx2, dpool_w, dg1b, dpscale, dsh1b, dsc1b, dmix1 = _pool_bwd(
        dx3, mixed, x2, g1b, pool_scale, mixn[1], sc1b, sh1b, pool_w, name="pool_bwd")

    dx1, (dup0g, dup0v, ddown0, dfcw0, dfcb0), (dsh2, dsc2, dg2, dffn0) = _ffn_bwd(
        dx2, x1, f0, u0, h0, z0, g2, ffnn[0], sc2, w_up[0], fconv_w[0], fcb[0], w_down[0], "l0")
    dy0, dg1 = _gate_bwd(dx1, y0, g1, name="mix0_gate_bwd")
    dcat = _mm_w(dy0, w_out, tb=True, name="out_proj_dx", tm=512)
    d_wout = _mm_tn(cat, dy0, name="out_proj_dw")
    dp_conv, dconv_w = _conv_gate_bwd(dcat, p_lat, conv_w, name="conv_gate_bwd")
    do_h, delta = _attn_bwd_prep(dcat, o_attn, name="attn_bwd_prep")
    dq_r, dk_all, dv_all = _attn_bwd(q_r, k_all, v_all, do_h, lse, delta, name="attn_bwd")
    dp_qkv, dqg_l, dkg_l = _qkv_bwd(p_lat, dq_r, dk_all, dv_all, qg, kg, cs_t, sn_t, has_q=True, kv_col=1,
                                    kv_row_off=lc, name="qkv_bwd")
    dp_ctx, _zero_qg, dkg_c = _qkv_bwd(p_ctx, None, dk_all, dv_all, qg, kg, None, None, has_q=False, kv_col=0,
                                       kv_row_off=0, name="qkv_bwd_ctx")
    da_lat = _mm_w([dp_qkv, dp_conv], w_in, tb=True, name="in_proj_dx", tm=512)
    da_ctx = _mm(dp_ctx, w_in[:, :D], tb=True, name="in_proj_dx_ctx", tm=256, tn=512, tk=1024)
    a_all = jnp.concatenate([a_lat, a_ctx], axis=0)
    dp_all = jnp.concatenate([dp_qkv, dp_ctx], axis=0)
    d_win_qkv = _mm_tn(a_all, dp_all, name="in_proj_dw_qkv", tk=768)
    d_win_conv = _mm_tn(a_lat, dp_conv, name="in_proj_dw_conv")
    grad_x, dsh1, dsc1, dmix0 = _norm_mod_bwd(da_lat, xs, mixn[0], sc1, dres=dx1, name="mix0_norm_bwd")
    _dctx, dcsh1, dcsc1, dmix0c = _norm_mod_bwd(da_ctx, ctxs, mixn[0], csc1, name="mix0_norm_bwd_ctx")

    z1k = jnp.zeros((1, D), F32)
    pack = jnp.concatenate(
        [v.reshape(-1) for v in (dsh1, dsc1, dg1, dsh2, dsc2, dg2, dsh1b, dsc1b, dg1b, dsh2b, dsc2b, dg2b,
                                 dcsh1, dcsc1, z1k, z1k, z1k, z1k,
                                 dmix0, dmix1, dmix0c, z1k, dffn0, dffn1, dqg_l, dkg_l + dkg_c,
                                 dfcb0, dfcb1, dconv_w, dfcw0, dfcw1, dpscale)])
    npack = pack.shape[0]
    pack = jnp.pad(pack, (0, (-npack) % 1024)).reshape(-1, 128)
    (g_pack,) = _all_gather([pack], name="gather_small_grads")
    gp = g_pack.reshape(NDEV, -1)
    off = [0]

    def take(size):
        seg = gp[:, off[0]:off[0] + size]
        off[0] += size
        return seg

    dmod_all = take(12 * D).reshape(NDEV, 2, 6 * D)
    dmodc_all = take(6 * D).reshape(NDEV, 1, 6 * D)
    dmix_all = take(4 * D).reshape(NDEV, 2, 2, D)
    dffn_all = take(2 * D).reshape(NDEV, 2, D)
    dqg_all = take(HD).reshape(NDEV, 1, HD)
    dkg_all = take(HD).reshape(NDEV, 1, HD)
    dfcb_all = take(2 * DFF).reshape(NDEV, 2, DFF)
    dconvw_all = take(3 * CW).reshape(NDEV, 3, CW)
    dfcw_all = take(6 * DFF).reshape(NDEV, 2, 3, DFF)
    dpscale_all = take(D).reshape(NDEV, D)

    outs = {}

    def put(nm, res):
        outs["grad_" + nm], outs["delta_" + nm], outs["new_m_" + nm], outs["new_v_" + nm] = res

    dmodc_pad = jnp.concatenate([dmodc_all, jnp.zeros_like(dmodc_all)], axis=1)
    put("ada_b", _adamw_nd(jnp.concatenate([dmod_all, dmodc_pad], axis=0), ada_b, m_ada_b, v_ada_b, name="adam_ada_b"))
    put("mix_norm", _adamw_nd(jnp.concatenate([dmix_all[:, 0], dmix_all[:, 1]], axis=0), mix_norm, m_mix_norm,
                              v_mix_norm, name="adam_mix_norm"))
    put("ffn_norm", _adamw_nd(dffn_all, ffn_norm, m_ffn_norm, v_ffn_norm, name="adam_ffn_norm"))
    put("even_q_gain", _adamw_nd(dqg_all, even_q_gain, m_even_q_gain, v_even_q_gain, name="adam_q_gain"))
    put("even_k_gain", _adamw_nd(dkg_all, even_k_gain, m_even_k_gain, v_even_k_gain, name="adam_k_gain"))
    put("ffn_conv_b", _adamw_nd(dfcb_all, ffn_conv_b, m_ffn_conv_b, v_ffn_conv_b, name="adam_ffn_conv_b"))
    my_convw = lax.dynamic_slice_in_dim(dconvw_all, me * ecw, ecw, axis=2)[:, None]
    put("even_conv_w", _adamw_nd(my_convw, even_conv_w, m_even_conv_w, v_even_conv_w, name="adam_even_conv_w"))
    my_fcw = lax.dynamic_slice_in_dim(dfcw_all, me * fcw, fcw, axis=3)
    put("ffn_conv_w", _adamw_nd(my_fcw, ffn_conv_w, m_ffn_conv_w, v_ffn_conv_w, name="adam_ffn_conv_w"))
    my_ps = lax.dynamic_slice_in_dim(dpscale_all, me * (D // NDEV), D // NDEV, axis=1)[:, None]
    put("odd_pool_scale", _adamw_nd(my_ps, odd_pool_scale, m_odd_pool_scale, v_odd_pool_scale, name="adam_pool_scale"))

    dmodc_sum = dmodc_all[0]
    for dev in range(1, NDEV):
        dmodc_sum = dmodc_sum + dmodc_all[dev]
    my_cols = lambda a: lax.dynamic_slice_in_dim(a, me * acols, acols, axis=a.ndim - 1)
    rows0 = jnp.concatenate([my_cols(dmod_all[:, 0]), my_cols(dmodc_sum), jnp.zeros((7, acols), F32)], axis=0)
    rows1 = jnp.concatenate([my_cols(dmod_all[:, 1]), jnp.zeros((8, acols), F32)], axis=0)
    d_ada = jnp.stack([_mm(mraw, rows, ta=True, silu_a=True, name=f"ada_dw_{l}", tm=512, tn=256, tk=16)
                       for l, rows in enumerate((rows0, rows1))])
    put("ada_w", _adamw_nd(d_ada[None], ada_w, m_ada_w, v_ada_w, name="adam_ada_w"))
    dscc_part = _mm(rows0, ada_w[0], tb=True, name="ada_dcctx", tm=16, tn=512, tk=256)
    (g_dscc,) = _all_gather([dscc_part[8:16]], name="gather_dcctx")
    put("c_ctx", _adamw_nd(g_dscc[:, 0:1, :].reshape(NDEV, D), c_ctx, m_c_ctx, v_c_ctx, name="adam_c_ctx",
                           silu_grad_of=c_ctx))

    d_win = jnp.concatenate([d_win_qkv, d_win_conv], axis=1)
    s_win = d_win.reshape(D, NDEV, -1).transpose(1, 0, 2)
    s_wout = d_wout.reshape(NDEV, D // NDEV, D)
    s_pool = dpool_w.astype(BF16).reshape(4, NDEV, PG // NDEV, PG).transpose(1, 0, 2, 3)
    s_up = jnp.stack([jnp.concatenate([dup0g, dup0v], axis=1), jnp.concatenate([dup1g, dup1v], axis=1)])
    s_up = s_up.reshape(2, D, NDEV, -1).transpose(2, 0, 1, 3)
    s_down = jnp.stack([ddown0, ddown1]).reshape(2, NDEV, DFF // NDEV, D).transpose(1, 0, 2, 3)
    r_win, r_wout, r_pool, r_up, r_down = _all_to_all([s_win, s_wout, s_pool, s_up, s_down], name="scatter_grads")
    put("even_w_in", _adamw_nd(r_win[:, None], even_w_in, m_even_w_in, v_even_w_in, name="adam_w_in"))
    put("even_w_out", _adamw_nd(r_wout[:, None], even_w_out, m_even_w_out, v_even_w_out, name="adam_w_out"))
    put("odd_pool_w", _adamw_nd(r_pool[:, None], odd_pool_w, m_odd_pool_w, v_odd_pool_w, name="adam_pool_w"))
    put("ffn_w_up", _adamw_nd(r_up, ffn_w_up, m_ffn_w_up, v_ffn_w_up, name="adam_w_up"))
    put("ffn_w_down", _adamw_nd(r_down, ffn_w_down, m_ffn_w_down, v_ffn_w_down, name="adam_w_down"))

    names = ["c_ctx", "ada_w", "ada_b", "mix_norm", "ffn_norm", "even_w_in", "even_q_gain", "even_k_gain",
             "even_conv_w", "even_w_out", "odd_pool_w", "odd_pool_scale", "ffn_w_up", "ffn_conv_w", "ffn_conv_b",
             "ffn_w_down"]
    result = [loss, grad_x[None]]
    for kind in ("grad_", "delta_", "new_m_", "new_v_"):
        result += [outs[kind + nm] for nm in names]
    return tuple(result)
```

```python
import functools
import math

import jax
import jax.numpy as jnp
from jax import lax
from jax.experimental import pallas as pl
from jax.experimental.pallas import tpu as pltpu

F32 = jnp.float32
BF16 = jnp.bfloat16

D = 1024
HD = 128
NQ = 4
NKV = 2
AW = NQ * HD
CW = D - AW
DFF = 2816
GRID_W = 64
ROPE_THETA = 10000.0
POOL_WINDOWS = (2, 4, 8, 16)
PG = D // 4
EPS = 1e-6
NDEV = 8
HALO = 8
MESH = pl.DeviceIdType.MESH

ADAM_LR = 0.001
ADAM_B1 = 0.9
ADAM_B2 = 0.999
ADAM_EPS = 1e-08
ADAM_WD = 0.01
ADAM_STEP = 10


def _pick(dim, prefs):
    for p in prefs:
        if dim % p == 0:
            return p
    return dim


def _params(*sem):
    return pltpu.CompilerParams(dimension_semantics=sem)


_NT = (((1,), (1,)), ((), ()))
_TN = (((0,), (0,)), ((), ()))
_SCALE = HD ** -0.5
_QSCALE = _SCALE * math.log2(math.e)
_LN2 = math.log(2.0)


def _mm(a_list, b, *, name, ta=False, tb=False, out_dtype=F32, silu_a=False, bias=None, tm=None, tn=None, tk=None):
    if not isinstance(a_list, (list, tuple)):
        a_list = [a_list]
    na = len(a_list)
    assert not (ta and na > 1)
    if ta:
        kdim, m = a_list[0].shape
        ks = [kdim]
    else:
        m = a_list[0].shape[0]
        ks = [a.shape[1] for a in a_list]
        kdim = sum(ks)
    n = b.shape[0] if tb else b.shape[1]
    assert (b.shape[1] if tb else b.shape[0]) == kdim
    kunit = math.gcd(*ks) if na > 1 else kdim
    tm = min(tm, m) if tm else _pick(m, (512, 256, 128, 64, 32, 16, 8))
    tn = min(tn, n) if tn else _pick(n, (512, 256, 128))
    tk = min(tk, kunit) if tk else _pick(kunit, (1024, 768, 512, 256, 128))
    assert m % tm == 0 and n % tn == 0 and all(k % tk == 0 for k in ks)
    nks = [k // tk for k in ks]
    starts = [sum(nks[:i]) for i in range(na)]
    nk = sum(nks)
    has_bias = bias is not None

    def body(*refs):
        a_refs = refs[:na]
        b_ref = refs[na]
        bias_ref = refs[na + 1] if has_bias else None
        o_ref = refs[na + 1 + has_bias]
        acc = refs[-1]
        k = pl.program_id(2)

        @pl.when(k == 0)
        def _():
            acc[...] = jnp.zeros_like(acc)

        bv = b_ref[...].astype(BF16)
        dn = (((0 if ta else 1,), (1 if tb else 0,)), ((), ()))
        for idx in range(na):
            def step(idx=idx):
                av = a_refs[idx][...]
                if silu_a:
                    av = av * jax.nn.sigmoid(av)
                acc[...] += lax.dot_general(av.astype(BF16), bv, dn, preferred_element_type=F32)
            if na == 1:
                step()
            else:
                pl.when((k >= starts[idx]) & (k < starts[idx] + nks[idx]))(step)

        @pl.when(k == nk - 1)
        def _():
            r = acc[...]
            if has_bias:
                r = r + bias_ref[...]
            o_ref[...] = r.astype(o_ref.dtype)

    in_specs = []
    for idx in range(na):
        if ta:
            in_specs.append(pl.BlockSpec((tk, tm), lambda i, j, k: (k, i)))
        else:
            lo, cnt = starts[idx], nks[idx]
            in_specs.append(pl.BlockSpec((tm, tk), lambda i, j, k, lo=lo, cnt=cnt: (i, jnp.clip(k - lo, 0, cnt - 1))))
    if tb:
        in_specs.append(pl.BlockSpec((tn, tk), lambda i, j, k: (j, k)))
    else:
        in_specs.append(pl.BlockSpec((tk, tn), lambda i, j, k: (k, j)))
    args = list(a_list) + [b]
    if has_bias:
        in_specs.append(pl.BlockSpec((1, tn), lambda i, j, k: (0, j)))
        args.append(bias)
    return pl.pallas_call(
        body, grid=(m // tm, n // tn, nk), in_specs=in_specs,
        out_specs=pl.BlockSpec((tm, tn), lambda i, j, k: (i, j)),
        out_shape=jax.ShapeDtypeStruct((m, n), out_dtype),
        scratch_shapes=[pltpu.VMEM((tm, tn), F32)], name=name,
        compiler_params=_params("parallel", "parallel", "arbitrary"))(*args)


def _mm_w(a_list, w, *, name, tb=False, tm=256, out_dtype=F32):
    if not isinstance(a_list, (list, tuple)):
        a_list = [a_list]
    na = len(a_list)
    m = a_list[0].shape[0]
    ks = [a.shape[1] for a in a_list]
    offs = [sum(ks[:i]) for i in range(na)]
    n = w.shape[0] if tb else w.shape[1]
    assert (w.shape[1] if tb else w.shape[0]) == sum(ks)
    tm = min(tm, m)
    assert m % tm == 0

    def body(*refs):
        a_refs, w_ref, o_ref = refs[:na], refs[na], refs[na + 1]
        acc = None
        for idx in range(na):
            av = a_refs[idx][...].astype(BF16)
            if tb:
                part = lax.dot_general(av, w_ref[:, offs[idx]:offs[idx] + ks[idx]], _NT, preferred_element_type=F32)
            else:
                part = jnp.dot(av, w_ref[offs[idx]:offs[idx] + ks[idx], :], preferred_element_type=F32)
            acc = part if acc is None else acc + part
        o_ref[...] = acc.astype(o_ref.dtype)

    in_specs = [pl.BlockSpec((tm, k), lambda i: (i, 0)) for k in ks] + [pl.BlockSpec(w.shape, lambda i: (0, 0))]
    return pl.pallas_call(
        body, grid=(m // tm,), in_specs=in_specs, out_specs=pl.BlockSpec((tm, n), lambda i: (i, 0)),
        out_shape=jax.ShapeDtypeStruct((m, n), out_dtype), name=name, compiler_params=_params("parallel"))(*a_list, w)


def _mm_tn(a, b, *, name, tk=1024, out_dtype=BF16):
    kdim, m = a.shape
    n = b.shape[1]
    assert b.shape[0] == kdim
    tk = min(tk, kdim)
    assert kdim % tk == 0
    nk = kdim // tk

    def body(a_ref, b_ref, o_ref, acc):
        k = pl.program_id(0)
        part = lax.dot_general(a_ref[...], b_ref[...], _TN, preferred_element_type=F32)

        @pl.when(k == 0)
        def _():
            acc[...] = part

        @pl.when(k > 0)
        def _():
            acc[...] += part

        @pl.when(k == nk - 1)
        def _():
            o_ref[...] = acc[...].astype(o_ref.dtype)

    return pl.pallas_call(
        body, grid=(nk,), in_specs=[pl.BlockSpec((tk, m), lambda k: (k, 0)), pl.BlockSpec((tk, n), lambda k: (k, 0))],
        out_specs=pl.BlockSpec((m, n), lambda k: (0, 0)), out_shape=jax.ShapeDtypeStruct((m, n), out_dtype),
        scratch_shapes=[pltpu.VMEM((m, n), F32)], name=name, compiler_params=_params("arbitrary"))(a, b)


def _vec(d, col=None):
    if col is None:
        return pl.BlockSpec((1, d), lambda i, *_: (0, 0))
    return pl.BlockSpec((1, d), col)


def _halo_specs(tm, width, nrows, colblk=0, row_off=0):
    r = tm // HALO
    off = row_off // HALO
    last = nrows // HALO - 1
    prev = pl.BlockSpec((HALO, width), lambda i, *_: (off + jnp.maximum(i * r - 1, 0), colblk))
    nxt = pl.BlockSpec((HALO, width), lambda i, *_: (off + jnp.minimum((i + 1) * r, last), colblk))
    return prev, nxt


def _ext(prev_ref, main_ref, next_ref, i, ni):
    p = jnp.where(i > 0, prev_ref[...], 0.0)
    n = jnp.where(i < ni - 1, next_ref[...], 0.0)
    return jnp.concatenate([p, main_ref[...], n], axis=0)


def _sh(ext, k, tm):
    if k == 0:
        return ext[HALO:HALO + tm]
    rows = ext.shape[0]
    return pltpu.roll(ext, (-k) % rows, axis=0)[HALO:HALO + tm]


def _roll_rows(v, k):
    rows = v.shape[0]
    return pltpu.roll(v, (-k) % rows, axis=0) if k % rows else v


def _conv3(ext, w_ref, tm):
    return _sh(ext, -1, tm) * w_ref[0:1, :] + _sh(ext, 0, tm) * w_ref[1:2, :] + _sh(ext, 1, tm) * w_ref[2:3, :]


def _colsum(v):
    return jnp.sum(v, axis=0, keepdims=True)


def _acc_out(ref, i, val):
    @pl.when(i == 0)
    def _():
        ref[...] = val

    @pl.when(i > 0)
    def _():
        ref[...] += val


def _sigmoid(v):
    return jax.nn.sigmoid(v)


def _norm_mod(x, gain, sc, sh, *, name, y=None, g=None, ymul=None, tm=256):
    n, d = x.shape
    has_res = y is not None
    has_mul = ymul is not None

    def body(*refs):
        it = iter(refs)
        x_ref = next(it)
        y_ref = next(it) if has_res else None
        g_ref = next(it) if has_res else None
        m_ref = next(it) if has_mul else None
        gain_ref, sc_ref, sh_ref = next(it), next(it), next(it)
        xo_ref = next(it) if has_res else None
        a_ref = next(it)
        xv = x_ref[...]
        if has_res:
            yv = y_ref[...]
            if has_mul:
                yv = yv * m_ref[...]
            xv = xv + g_ref[...] * yv
            xo_ref[...] = xv
        r = lax.rsqrt(jnp.mean(xv * xv, axis=-1, keepdims=True) + EPS)
        nrm = (xv * r) * gain_ref[...]
        a_ref[...] = (nrm * (1.0 + sc_ref[...]) + sh_ref[...]).astype(BF16)

    row = pl.BlockSpec((tm, d), lambda i: (i, 0))
    in_specs, args = [row], [x]
    if has_res:
        in_specs += [row, _vec(d)]
        args += [y, g]
    if has_mul:
        in_specs.append(_vec(d))
        args.append(ymul)
    in_specs += [_vec(d)] * 3
    args += [gain, sc, sh]
    out_specs, out_shape = [], []
    if has_res:
        out_specs.append(row)
        out_shape.append(jax.ShapeDtypeStruct((n, d), F32))
    out_specs.append(row)
    out_shape.append(jax.ShapeDtypeStruct((n, d), BF16))
    res = pl.pallas_call(body, grid=(n // tm,), in_specs=in_specs, out_specs=out_specs, out_shape=out_shape,
                         name=name, compiler_params=_params("parallel"))(*args)
    return res if has_res else res[0]


def _norm_mod_bwd(da, x, gain, sc, *, name, dres=None, tm=256):
    n, d = x.shape
    has_res = dres is not None

    def body(*refs):
        it = iter(refs)
        da_ref, x_ref = next(it), next(it)
        r_ref = next(it) if has_res else None
        gain_ref, sc_ref = next(it), next(it)
        dx_ref, dsh_ref, dsc_ref, dgn_ref = next(it), next(it), next(it), next(it)
        i = pl.program_id(0)
        xv = x_ref[...]
        dav = da_ref[...]
        r = lax.rsqrt(jnp.mean(xv * xv, axis=-1, keepdims=True) + EPS)
        xh = xv * r
        nrm = xh * gain_ref[...]
        dn = dav * (1.0 + sc_ref[...])
        dxh = dn * gain_ref[...]
        dx = r * (dxh - xh * jnp.mean(dxh * xh, axis=-1, keepdims=True))
        if has_res:
            dx = dx + r_ref[...]
        dx_ref[...] = dx
        _acc_out(dsh_ref, i, _colsum(dav))
        _acc_out(dsc_ref, i, _colsum(dav * nrm))
        _acc_out(dgn_ref, i, _colsum(dn * xh))

    row = pl.BlockSpec((tm, d), lambda i: (i, 0))
    in_specs, args = [row, row], [da, x]
    if has_res:
        in_specs.append(row)
        args.append(dres)
    in_specs += [_vec(d)] * 2
    args += [gain, sc]
    vec_shape = jax.ShapeDtypeStruct((1, d), F32)
    return pl.pallas_call(
        body, grid=(n // tm,), in_specs=in_specs, out_specs=[row, _vec(d), _vec(d), _vec(d)],
        out_shape=[jax.ShapeDtypeStruct((n, d), F32), vec_shape, vec_shape, vec_shape],
        name=name, compiler_params=_params("arbitrary"))(*args)


def _gate_bwd(dxo, y, g, *, name, tm=256):
    n, d = dxo.shape

    def body(dx_ref, y_ref, g_ref, dy_ref, dg_ref):
        i = pl.program_id(0)
        dxv = dx_ref[...]
        dy_ref[...] = (dxv * g_ref[...]).astype(BF16)
        _acc_out(dg_ref, i, _colsum(dxv * y_ref[...]))

    row = pl.BlockSpec((tm, d), lambda i: (i, 0))
    return pl.pallas_call(
        body, grid=(n // tm,), in_specs=[row, row, _vec(d)], out_specs=[row, _vec(d)],
        out_shape=[jax.ShapeDtypeStruct((n, d), BF16), jax.ShapeDtypeStruct((1, d), F32)],
        name=name, compiler_params=_params("arbitrary"))(dxo, y, g)


def _loss_head(x, z, g, tgt, *, name, tm=256):
    n, d = x.shape

    def body(x_ref, z_ref, g_ref, t_ref, dx_ref, loss_ref):
        i = pl.program_id(0)
        diff = (x_ref[...] + g_ref[...] * z_ref[...]) - t_ref[...]
        dx_ref[...] = diff * (1.0 / d)
        part = 0.5 * jnp.sum(jnp.mean(diff * diff, axis=-1, keepdims=True), axis=0, keepdims=True)
        _acc_out(loss_ref, i, jnp.broadcast_to(part, (1, 128)))

    row = pl.BlockSpec((tm, d), lambda i: (i, 0))
    return pl.pallas_call(
        body, grid=(n // tm,), in_specs=[row, row, _vec(d), row], out_specs=[row, _vec(128)],
        out_shape=[jax.ShapeDtypeStruct((n, d), F32), jax.ShapeDtypeStruct((1, 128), F32)],
        name=name, compiler_params=_params("arbitrary"))(x, z, g, tgt)


def _glu_fwd(u, cw, cb, *, name, tm=256, tc=256):
    n = u.shape[0]
    tm = min(tm, n)
    nc = DFF // tc
    ni = n // tm

    def body(g_ref, gp_ref, gn_ref, v_ref, cw_ref, cb_ref, h_ref):
        i = pl.program_id(0)
        gext = _ext(gp_ref, g_ref, gn_ref, i, ni)
        gc = _conv3(gext, cw_ref, tm) + cb_ref[...]
        h_ref[...] = (gc * _sigmoid(gc) * v_ref[...]).astype(BF16)

    prev = pl.BlockSpec((HALO, tc), lambda i, j: (jnp.maximum(i * (tm // HALO) - 1, 0), j))
    nxt = pl.BlockSpec((HALO, tc), lambda i, j: (jnp.minimum((i + 1) * (tm // HALO), n // HALO - 1), j))
    return pl.pallas_call(
        body, grid=(ni, nc),
        in_specs=[pl.BlockSpec((tm, tc), lambda i, j: (i, j)), prev, nxt,
                  pl.BlockSpec((tm, tc), lambda i, j: (i, nc + j)),
                  pl.BlockSpec((3, tc), lambda i, j: (0, j)), pl.BlockSpec((1, tc), lambda i, j: (0, j))],
        out_specs=pl.BlockSpec((tm, tc), lambda i, j: (i, j)),
        out_shape=jax.ShapeDtypeStruct((n, DFF), BF16), name=name,
        compiler_params=_params("parallel", "parallel"))(u, u, u, u, cw, cb)


def _glu_bwd(dh, u, cw, cb, *, name, tm=256, tc=256):
    n = u.shape[0]
    tm = min(tm, n)
    nc = DFF // tc
    ni = n // tm
    rows = tm + 2 * HALO

    def body(dh_ref, dhp_ref, dhn_ref, g_ref, gp_ref, gn_ref, v_ref, vp_ref, vn_ref, cw_ref, cb_ref,
             dg_ref, dv_ref, dcw_ref, dcb_ref):
        i = pl.program_id(1)
        gext = _ext(gp_ref, g_ref, gn_ref, i, ni)
        dhext = _ext(dhp_ref, dh_ref, dhn_ref, i, ni)
        vext = _ext(vp_ref, v_ref, vn_ref, i, ni)
        gc = (_roll_rows(gext, -1) * cw_ref[0:1, :] + gext * cw_ref[1:2, :] + _roll_rows(gext, 1) * cw_ref[2:3, :]
              + cb_ref[...])
        sg = _sigmoid(gc)
        dgc = dhext * vext * (sg * (1.0 + gc * (1.0 - sg)))
        dv_ref[...] = (dh_ref[...] * (gc[HALO:HALO + tm] * sg[HALO:HALO + tm])).astype(BF16)
        dgate = (_sh(dgc, 1, tm) * cw_ref[0:1, :] + _sh(dgc, 0, tm) * cw_ref[1:2, :] + _sh(dgc, -1, tm) * cw_ref[2:3, :])
        dg_ref[...] = dgate.astype(BF16)
        dgc_t = dgc[HALO:HALO + tm]
        dcw = jnp.concatenate([_colsum(dgc_t * _sh(gext, -1, tm)), _colsum(dgc_t * _sh(gext, 0, tm)),
                               _colsum(dgc_t * _sh(gext, 1, tm))], axis=0)
        _acc_out(dcw_ref, i, dcw)
        _acc_out(dcb_ref, i, _colsum(dgc_t))

    r = tm // HALO
    last = n // HALO - 1

    def trio(off):
        return [pl.BlockSpec((tm, tc), lambda j, i: (i, off + j)),
                pl.BlockSpec((HALO, tc), lambda j, i: (jnp.maximum(i * r - 1, 0), off + j)),
                pl.BlockSpec((HALO, tc), lambda j, i: (jnp.minimum((i + 1) * r, last), off + j))]

    del rows
    return pl.pallas_call(
        body, grid=(nc, ni),
        in_specs=trio(0) + trio(0) + trio(nc) + [pl.BlockSpec((3, tc), lambda j, i: (0, j)),
                                                 pl.BlockSpec((1, tc), lambda j, i: (0, j))],
        out_specs=[pl.BlockSpec((tm, tc), lambda j, i: (i, j)), pl.BlockSpec((tm, tc), lambda j, i: (i, j)),
                   pl.BlockSpec((3, tc), lambda j, i: (0, j)), pl.BlockSpec((1, tc), lambda j, i: (0, j))],
        out_shape=[jax.ShapeDtypeStruct((n, DFF), BF16), jax.ShapeDtypeStruct((n, DFF), BF16),
                   jax.ShapeDtypeStruct((3, DFF), F32), jax.ShapeDtypeStruct((1, DFF), F32)],
        name=name, compiler_params=_params("parallel", "arbitrary"))(dh, dh, dh, u, u, u, u, u, u, cw, cb)


def _rope_tables(n):
    rows = n // GRID_W
    row_ids = jnp.repeat(jnp.arange(rows), GRID_W).astype(F32)
    col_ids = jnp.tile(jnp.arange(GRID_W), rows).astype(F32)
    axis_dim = HD // 2
    inv_freq = jnp.power(ROPE_THETA, -jnp.arange(0, axis_dim, 2, dtype=F32) / axis_dim)
    ar = row_ids[:, None] * inv_freq
    ac = col_ids[:, None] * inv_freq
    cs = jnp.concatenate([jnp.cos(ar), jnp.cos(ar), jnp.cos(ac), jnp.cos(ac)], axis=1)
    sn = jnp.concatenate([-jnp.sin(ar), jnp.sin(ar), -jnp.sin(ac), jnp.sin(ac)], axis=1)
    return cs, sn


def _partner(v):
    lane = lax.broadcasted_iota(jnp.int32, v.shape, 1)
    return jnp.where((lane % 64) < 32, pltpu.roll(v, HD - 32, axis=1), pltpu.roll(v, 32, axis=1))


def _qkv_prep(p, q_gain, k_gain, cs, sn, *, name, has_q, kv_col, tm=256):
    n = p.shape[0]
    rope = cs is not None

    def body(*refs):
        it = iter(refs)
        q_ref = next(it) if has_q else None
        kv_ref = next(it)
        qg_ref, kg_ref = next(it), next(it)
        cs_ref = next(it) if rope else None
        sn_ref = next(it) if rope else None
        qo_ref = next(it) if has_q else None
        ko_ref, vo_ref = next(it), next(it)

        def norm_rope(xh, gain, mul=None):
            r = lax.rsqrt(jnp.mean(xh * xh, axis=-1, keepdims=True) + EPS)
            xn = (xh * r) * gain
            if rope:
                xn = xn * cs_ref[...] + _partner(xn) * sn_ref[...]
            if mul is not None:
                xn = xn * mul
            return xn.astype(BF16)

        if has_q:
            for h in range(NQ):
                qo_ref[h] = norm_rope(q_ref[:, h * HD:(h + 1) * HD], qg_ref[...], _QSCALE)
        for h in range(NKV):
            ko_ref[h] = norm_rope(kv_ref[:, h * HD:(h + 1) * HD], kg_ref[...])
            vo_ref[h] = kv_ref[:, (NKV + h) * HD:(NKV + h + 1) * HD].astype(BF16)

    in_specs, args = [], []
    if has_q:
        in_specs.append(pl.BlockSpec((tm, AW), lambda i: (i, 0)))
        args.append(p)
    in_specs += [pl.BlockSpec((tm, 2 * NKV * HD), lambda i: (i, kv_col)), _vec(HD), _vec(HD)]
    args += [p, q_gain, k_gain]
    if rope:
        in_specs += [pl.BlockSpec((tm, HD), lambda i: (i, 0))] * 2
        args += [cs, sn]
    out_specs, out_shape = [], []
    if has_q:
        out_specs.append(pl.BlockSpec((NQ, tm, HD), lambda i: (0, i, 0)))
        out_shape.append(jax.ShapeDtypeStruct((NQ, n, HD), BF16))
    out_specs += [pl.BlockSpec((NKV, tm, HD), lambda i: (0, i, 0))] * 2
    out_shape += [jax.ShapeDtypeStruct((NKV, n, HD), BF16)] * 2
    return pl.pallas_call(body, grid=(n // tm,), in_specs=in_specs, out_specs=out_specs, out_shape=out_shape,
                          name=name, compiler_params=_params("parallel"))(*args)


def _qkv_bwd(p, dq, dk, dv, q_gain, k_gain, cs, sn, *, name, has_q, kv_col, kv_row_off, tm=256):
    n = p.shape[0]
    rope = cs is not None
    rb = kv_row_off // tm

    def body(*refs):
        it = iter(refs)
        q_ref = next(it) if has_q else None
        kv_ref = next(it)
        dq_ref = next(it) if has_q else None
        dk_ref, dv_ref = next(it), next(it)
        qg_ref, kg_ref = next(it), next(it)
        cs_ref = next(it) if rope else None
        sn_ref = next(it) if rope else None
        dp_ref, dqg_ref, dkg_ref = next(it), next(it), next(it)
        i = pl.program_id(0)

        def back(xh, dout, gain):
            if rope:
                dout = dout * cs_ref[...] + _partner(dout * sn_ref[...])
            r = lax.rsqrt(jnp.mean(xh * xh, axis=-1, keepdims=True) + EPS)
            xhat = xh * r
            dxh = dout * gain
            dx = r * (dxh - xhat * jnp.mean(dxh * xhat, axis=-1, keepdims=True))
            return dx, _colsum(dout * xhat)

        dqg = jnp.zeros((1, HD), F32)
        dkg = jnp.zeros((1, HD), F32)
        if has_q:
            for h in range(NQ):
                dx, dg = back(q_ref[:, h * HD:(h + 1) * HD], dq_ref[h], qg_ref[...])
                dp_ref[:, h * HD:(h + 1) * HD] = dx.astype(BF16)
                dqg = dqg + dg
        else:
            dp_ref[:, 0:AW] = jnp.zeros((tm, AW), BF16)
        for h in range(NKV):
            dx, dg = back(kv_ref[:, h * HD:(h + 1) * HD], dk_ref[h], kg_ref[...])
            dp_ref[:, AW + h * HD:AW + (h + 1) * HD] = dx.astype(BF16)
            dkg = dkg + dg
            dp_ref[:, AW + (NKV + h) * HD:AW + (NKV + h + 1) * HD] = dv_ref[h].astype(BF16)
        _acc_out(dqg_ref, i, dqg)
        _acc_out(dkg_ref, i, dkg)

    in_specs, args = [], []
    if has_q:
        in_specs.append(pl.BlockSpec((tm, AW), lambda i: (i, 0)))
        args.append(p)
    in_specs.append(pl.BlockSpec((tm, 2 * NKV * HD), lambda i: (i, kv_col)))
    args.append(p)
    if has_q:
        in_specs.append(pl.BlockSpec((NQ, tm, HD), lambda i: (0, i, 0)))
        args.append(dq)
    in_specs += [pl.BlockSpec((NKV, tm, HD), lambda i: (0, rb + i, 0))] * 2 + [_vec(HD), _vec(HD)]
    args += [dk, dv, q_gain, k_gain]
    if rope:
        in_specs += [pl.BlockSpec((tm, HD), lambda i: (i, 0))] * 2
        args += [cs, sn]
    return pl.pallas_call(
        body, grid=(n // tm,), in_specs=in_specs,
        out_specs=[pl.BlockSpec((tm, D), lambda i: (i, 0)), _vec(HD), _vec(HD)],
        out_shape=[jax.ShapeDtypeStruct((n, D), BF16), jax.ShapeDtypeStruct((1, HD), F32),
                   jax.ShapeDtypeStruct((1, HD), F32)],
        name=name, compiler_params=_params("arbitrary"))(*args)


def _conv_gate_fwd(p, o, conv_w, *, name, tm=256):
    n = p.shape[0]
    ni = n // tm

    def body(gb_ref, gc_ref, gcp_ref, gcn_ref, xi_ref, xip_ref, xin_ref, o_ref, w_ref, cat_ref):
        i = pl.program_id(0)
        hext = _ext(gcp_ref, gc_ref, gcn_ref, i, ni) * _ext(xip_ref, xi_ref, xin_ref, i, ni)
        cat_ref[:, 0:AW] = o_ref[...].astype(BF16)
        cat_ref[:, AW:D] = (gb_ref[...] * _conv3(hext, w_ref, tm)).astype(BF16)

    gcp, gcn = _halo_specs(tm, CW, n, colblk=3)
    xip, xin = _halo_specs(tm, CW, n, colblk=4)
    return pl.pallas_call(
        body, grid=(ni,),
        in_specs=[pl.BlockSpec((tm, CW), lambda i: (i, 2)), pl.BlockSpec((tm, CW), lambda i: (i, 3)), gcp, gcn,
                  pl.BlockSpec((tm, CW), lambda i: (i, 4)), xip, xin, pl.BlockSpec((tm, AW), lambda i: (i, 0)),
                  pl.BlockSpec((3, CW), lambda i: (0, 0))],
        out_specs=pl.BlockSpec((tm, D), lambda i: (i, 0)), out_shape=jax.ShapeDtypeStruct((n, D), BF16),
        name=name, compiler_params=_params("parallel"))(p, p, p, p, p, p, p, o, conv_w)


def _conv_gate_bwd(dcat, p, conv_w, *, name, tm=256):
    n = p.shape[0]
    ni = n // tm

    def body(dc_ref, dcp_ref, dcn_ref, gb_ref, gbp_ref, gbn_ref, gc_ref, gcp_ref, gcn_ref, xi_ref, xip_ref, xin_ref,
             w_ref, dp_ref, dw_ref):
        i = pl.program_id(0)
        gcext = _ext(gcp_ref, gc_ref, gcn_ref, i, ni)
        xiext = _ext(xip_ref, xi_ref, xin_ref, i, ni)
        hext = gcext * xiext
        dcv = _ext(dcp_ref, dc_ref, dcn_ref, i, ni) * _ext(gbp_ref, gb_ref, gbn_ref, i, ni)
        dp_ref[:, 0:CW] = (dc_ref[...] * _conv3(hext, w_ref, tm)).astype(BF16)
        dh = _sh(dcv, 1, tm) * w_ref[0:1, :] + _sh(dcv, 0, tm) * w_ref[1:2, :] + _sh(dcv, -1, tm) * w_ref[2:3, :]
        dp_ref[:, CW:2 * CW] = (dh * xi_ref[...]).astype(BF16)
        dp_ref[:, 2 * CW:3 * CW] = (dh * gc_ref[...]).astype(BF16)
        dcv_t = dcv[HALO:HALO + tm]
        dw = jnp.concatenate([_colsum(dcv_t * _sh(hext, -1, tm)), _colsum(dcv_t * _sh(hext, 0, tm)),
                              _colsum(dcv_t * _sh(hext, 1, tm))], axis=0)
        _acc_out(dw_ref, i, dw)

    def trio(colblk):
        prev, nxt = _halo_specs(tm, CW, n, colblk=colblk)
        return [pl.BlockSpec((tm, CW), lambda i: (i, colblk)), prev, nxt]

    return pl.pallas_call(
        body, grid=(ni,), in_specs=trio(1) + trio(2) + trio(3) + trio(4) + [pl.BlockSpec((3, CW), lambda i: (0, 0))],
        out_specs=[pl.BlockSpec((tm, 3 * CW), lambda i: (i, 0)), pl.BlockSpec((3, CW), lambda i: (0, 0))],
        out_shape=[jax.ShapeDtypeStruct((n, 3 * CW), BF16), jax.ShapeDtypeStruct((3, CW), F32)],
        name=name, compiler_params=_params("arbitrary"))(dcat, dcat, dcat, p, p, p, p, p, p, p, p, p, conv_w)


def _attn_fwd(q, k, v, *, name, bq=128):
    n = q.shape[1]
    t = k.shape[1]
    bq = min(bq, n)

    def body(q_ref, k_ref, v_ref, o_ref, lse_ref):
        q2 = q_ref[...].reshape(2 * bq, HD)
        s = lax.dot_general(q2, k_ref[0], _NT, preferred_element_type=F32)
        m = jnp.max(s, axis=-1, keepdims=True)
        pv = jnp.exp2(s - m)
        l = jnp.sum(pv, axis=-1, keepdims=True)
        out = jnp.dot(pv.astype(BF16), v_ref[0], preferred_element_type=F32) / l
        o_ref[:, 0:HD] = out[0:bq]
        o_ref[:, HD:2 * HD] = out[bq:2 * bq]
        lse_ref[...] = (m + jnp.log2(l)).reshape(2, bq, 1)

    kspec = pl.BlockSpec((1, t, HD), lambda h, i: (h, 0, 0))
    return pl.pallas_call(
        body, grid=(NKV, n // bq),
        in_specs=[pl.BlockSpec((2, bq, HD), lambda h, i: (h, i, 0)), kspec, kspec],
        out_specs=[pl.BlockSpec((bq, 2 * HD), lambda h, i: (i, h)), pl.BlockSpec((2, bq, 1), lambda h, i: (h, i, 0))],
        out_shape=[jax.ShapeDtypeStruct((n, AW), F32), jax.ShapeDtypeStruct((NQ, n, 1), F32)],
        name=name, compiler_params=_params("parallel", "parallel"))(q, k, v)


def _attn_bwd_prep(dcat, o, *, name, tm=256):
    n = o.shape[0]

    def body(dc_ref, o_ref, do_ref, dl_ref):
        for h in range(NQ):
            dh = dc_ref[:, h * HD:(h + 1) * HD]
            do_ref[h] = dh.astype(BF16)
            dl_ref[h] = jnp.sum(dh * o_ref[:, h * HD:(h + 1) * HD], axis=-1, keepdims=True)

    return pl.pallas_call(
        body, grid=(n // tm,),
        in_specs=[pl.BlockSpec((tm, AW), lambda i: (i, 0)), pl.BlockSpec((tm, AW), lambda i: (i, 0))],
        out_specs=[pl.BlockSpec((NQ, tm, HD), lambda i: (0, i, 0)), pl.BlockSpec((NQ, tm, 1), lambda i: (0, i, 0))],
        out_shape=[jax.ShapeDtypeStruct((NQ, n, HD), BF16), jax.ShapeDtypeStruct((NQ, n, 1), F32)],
        name=name, compiler_params=_params("parallel"))(dcat, o)


def _attn_bwd(q, k, v, do, lse, delta, *, name, bq=256):
    n = q.shape[1]
    t = k.shape[1]
    bq = min(bq, n)

    def body(q_ref, k_ref, v_ref, do_ref, lse_ref, dl_ref, dq_ref, dk_ref, dv_ref):
        qi = pl.program_id(1)
        q2 = q_ref[...].reshape(2 * bq, HD)
        do2 = do_ref[...].reshape(2 * bq, HD)
        s = lax.dot_general(q2, k_ref[0], _NT, preferred_element_type=F32)
        pv = jnp.exp2(s - lse_ref[...].reshape(2 * bq, 1))
        dp = lax.dot_general(do2, v_ref[0], _NT, preferred_element_type=F32)
        ds = (pv * (dp - dl_ref[...].reshape(2 * bq, 1))).astype(BF16)
        dq_ref[...] = (jnp.dot(ds, k_ref[0], preferred_element_type=F32) * _SCALE).reshape(2, bq, HD)
        dk_part = lax.dot_general(ds, q2, _TN, preferred_element_type=F32) * _LN2
        dv_part = lax.dot_general(pv.astype(BF16), do2, _TN, preferred_element_type=F32)

        @pl.when(qi == 0)
        def _():
            dk_ref[0] = dk_part
            dv_ref[0] = dv_part

        @pl.when(qi > 0)
        def _():
            dk_ref[0] += dk_part
            dv_ref[0] += dv_part

    qspec = pl.BlockSpec((2, bq, HD), lambda h, i: (h, i, 0))
    kspec = pl.BlockSpec((1, t, HD), lambda h, i: (h, 0, 0))
    sspec = pl.BlockSpec((2, bq, 1), lambda h, i: (h, i, 0))
    return pl.pallas_call(
        body, grid=(NKV, n // bq), in_specs=[qspec, kspec, kspec, qspec, sspec, sspec], out_specs=[qspec, kspec, kspec],
        out_shape=[jax.ShapeDtypeStruct((NQ, n, HD), F32), jax.ShapeDtypeStruct((NKV, t, HD), F32),
                   jax.ShapeDtypeStruct((NKV, t, HD), F32)],
        name=name, compiler_params=_params("parallel", "arbitrary"))(q, k, v, do, lse, delta)


def _window_sums(ext, w):
    s, step = ext, 1
    while step < w:
        s = s + _roll_rows(s, step)
        step *= 2
    return s


def _pool_counts(i, tm, n, w, rows, first):
    t = i * tm - HALO + first + lax.broadcasted_iota(jnp.int32, (rows, 1), 0)
    lo = jnp.clip(t - w // 2, 0, n)
    hi = jnp.clip(t + w - w // 2, 0, n)
    return jnp.maximum(hi - lo, 1).astype(F32)


def _norm_mod_ext(xext, gain_ref, sc_ref, sh_ref, i, tm, n):
    rows = xext.shape[0]
    t = i * tm - HALO + lax.broadcasted_iota(jnp.int32, (rows, 1), 0)
    inside = (t >= 0) & (t < n)
    r = lax.rsqrt(jnp.mean(xext * xext, axis=-1, keepdims=True) + EPS)
    xh = xext * r
    a = (xh * gain_ref[...]) * (1.0 + sc_ref[...]) + sh_ref[...]
    return jnp.where(inside, a, 0.0), r, xh


def _pool_fwd(x, y, g, gain, sc, sh, pool_w, *, name, tm=256):
    n, d = x.shape
    ni = n // tm

    def body(x_ref, xp_ref, xn_ref, y_ref, yp_ref, yn_ref, g_ref, gain_ref, sc_ref, sh_ref, w_ref, xo_ref, o_ref):
        i = pl.program_id(0)
        xext = _ext(xp_ref, x_ref, xn_ref, i, ni) + g_ref[...] * _ext(yp_ref, y_ref, yn_ref, i, ni)
        xo_ref[...] = xext[HALO:HALO + tm]
        aext, _, _ = _norm_mod_ext(xext, gain_ref, sc_ref, sh_ref, i, tm, n)
        for gi, w in enumerate(POOL_WINDOWS):
            ag = aext[:, gi * PG:(gi + 1) * PG]
            mean = _sh(_window_sums(ag, w), -(w // 2), tm) / _pool_counts(i, tm, n, w, tm, HALO)
            pooled = mean - ag[HALO:HALO + tm]
            o_ref[:, gi * PG:(gi + 1) * PG] = jnp.dot(pooled.astype(BF16), w_ref[gi], preferred_element_type=F32)

    row = pl.BlockSpec((tm, d), lambda i: (i, 0))
    prev, nxt = _halo_specs(tm, d, n)
    return pl.pallas_call(
        body, grid=(ni,),
        in_specs=[row, prev, nxt, row, prev, nxt, _vec(d), _vec(d), _vec(d), _vec(d),
                  pl.BlockSpec((4, PG, PG), lambda i: (0, 0, 0))],
        out_specs=[row, row], out_shape=[jax.ShapeDtypeStruct((n, d), F32)] * 2,
        name=name, compiler_params=_params("parallel"))(x, x, x, y, y, y, g, gain, sc, sh, pool_w)


def _pool_bwd(dxo, mixed, x, g, scale, gain, sc, sh, pool_w, *, name, tm=256):
    n, d = x.shape
    ni = n // tm
    def body(dx_ref, dxp_ref, dxn_ref, mx_ref, x_ref, xp_ref, xn_ref, g_ref, s_ref, gain_ref, sc_ref, sh_ref, w_ref,
             dxi_ref, dw_ref, dg_ref, dsl_ref, dsh_ref, dsc_ref, dgn_ref):
        i = pl.program_id(0)
        dxo_t = dx_ref[...]
        mixed_t = mx_ref[...]
        dy_t = dxo_t * g_ref[...]
        _acc_out(dg_ref, i, _colsum(dxo_t * (mixed_t * s_ref[...])))
        _acc_out(dsl_ref, i, _colsum(dy_t * mixed_t))
        dmixed = (_ext(dxp_ref, dx_ref, dxn_ref, i, ni) * g_ref[...]) * s_ref[...]
        xext = _ext(xp_ref, x_ref, xn_ref, i, ni)
        aext, rext, xhext = _norm_mod_ext(xext, gain_ref, sc_ref, sh_ref, i, tm, n)
        rows = tm + 2 * HALO
        da_parts = []
        for gi, w in enumerate(POOL_WINDOWS):
            sl = slice(gi * PG, (gi + 1) * PG)
            ag = aext[:, sl]
            mean = _sh(_window_sums(ag, w), -(w // 2), tm) / _pool_counts(i, tm, n, w, tm, HALO)
            pooled = (mean - ag[HALO:HALO + tm]).astype(BF16)
            dmg = dmixed[:, sl].astype(BF16)
            dwg = lax.dot_general(pooled, dmixed[HALO:HALO + tm, sl].astype(BF16), _TN, preferred_element_type=F32)

            @pl.when(i == 0)
            def _(dwg=dwg, gi=gi):
                dw_ref[gi] = dwg

            @pl.when(i > 0)
            def _(dwg=dwg, gi=gi):
                dw_ref[gi] += dwg

            dpl = lax.dot_general(dmg, w_ref[gi], _NT, preferred_element_type=F32)
            e = dpl / _pool_counts(i, tm, n, w, rows, 0)
            da_parts.append(_sh(_window_sums(e, w), 1 - w // 2, tm) - dpl[HALO:HALO + tm])
        da = jnp.concatenate(da_parts, axis=1)
        r = rext[HALO:HALO + tm]
        xh = xhext[HALO:HALO + tm]
        nrm = xh * gain_ref[...]
        dn = da * (1.0 + sc_ref[...])
        dxh = dn * gain_ref[...]
        dxi_ref[...] = dxo_t + r * (dxh - xh * jnp.mean(dxh * xh, axis=-1, keepdims=True))
        _acc_out(dsh_ref, i, _colsum(da))
        _acc_out(dsc_ref, i, _colsum(da * nrm))
        _acc_out(dgn_ref, i, _colsum(dn * xh))

    row = pl.BlockSpec((tm, d), lambda i: (i, 0))
    prev, nxt = _halo_specs(tm, d, n)
    wspec = pl.BlockSpec((4, PG, PG), lambda i: (0, 0, 0))
    vshape = jax.ShapeDtypeStruct((1, d), F32)
    return pl.pallas_call(
        body, grid=(ni,),
        in_specs=[row, prev, nxt, row, row, prev, nxt] + [_vec(d)] * 5 + [wspec],
        out_specs=[row, wspec] + [_vec(d)] * 5,
        out_shape=[jax.ShapeDtypeStruct((n, d), F32), jax.ShapeDtypeStruct((4, PG, PG), F32)] + [vshape] * 5,
        name=name, compiler_params=_params("arbitrary"))(dxo, dxo, dxo, mixed, x, x, x, g, scale, gain, sc, sh, pool_w)


def _adamw(gparts_list, w, m, v, *, name, silu_grad_of=None):
    nl = len(gparts_list)
    nparts, r, c = gparts_list[0].shape
    tr = _pick(r, (256, 128, 64, 32, 16, 8))
    has_c = silu_grad_of is not None

    def body(*refs):
        gp_refs = refs[:nl]
        it = iter(refs[nl:])
        w_ref, m_ref, v_ref = next(it), next(it), next(it)
        c_ref = next(it) if has_c else None
        g_ref, d_ref, mo_ref, vo_ref = next(it), next(it), next(it), next(it)
        layer = pl.program_id(0)

        def update(gp_ref):
            g = gp_ref[0].astype(F32)
            for p in range(1, nparts):
                g = g + gp_ref[p].astype(F32)
            if has_c:
                cv = c_ref[0]
                sg = _sigmoid(cv)
                g = g * (sg * (1.0 + cv * (1.0 - sg)))
            g_ref[0] = g
            mn = ADAM_B1 * m_ref[0] + (1.0 - ADAM_B1) * g
            vn = ADAM_B2 * v_ref[0] + (1.0 - ADAM_B2) * (g * g)
            m_hat = mn / (1.0 - ADAM_B1 ** ADAM_STEP)
            v_hat = vn / (1.0 - ADAM_B2 ** ADAM_STEP)
            d_ref[0] = -ADAM_LR * (m_hat / (jnp.sqrt(v_hat) + ADAM_EPS) + ADAM_WD * w_ref[0])
            mo_ref[0] = mn
            vo_ref[0] = vn

        if nl == 1:
            update(gp_refs[0])
        else:
            for li in range(nl):
                pl.when(layer == li)(functools.partial(update, gp_refs[li]))

    row = pl.BlockSpec((1, tr, c), lambda l, i: (l, i, 0))
    in_specs = [pl.BlockSpec((nparts, tr, c), lambda l, i, li=li: (0, jnp.where(l == li, i, 0), 0)) for li in range(nl)]
    in_specs += [row, row, row]
    args = list(gparts_list) + [w, m, v]
    if has_c:
        in_specs.append(row)
        args.append(silu_grad_of)
    return pl.pallas_call(
        body, grid=(nl, r // tr), in_specs=in_specs, out_specs=[row] * 4,
        out_shape=[jax.ShapeDtypeStruct((nl, r, c), F32)] * 4, name=name,
        compiler_params=_params("arbitrary", "arbitrary"))(*args)


def _adamw_nd(gparts, w, m, v, *, name, silu_grad_of=None):
    shape = w.shape
    c = shape[-1]
    if isinstance(gparts, (list, tuple)):
        nl = len(gparts)
        r = math.prod(shape[1:-1])
    else:
        nl = 1
        r = math.prod(shape[:-1]) if len(shape) > 1 else 1
        gparts = [gparts]
    rs = lambda a: a.reshape(nl, r, c)
    res = _adamw([gp.reshape(gp.shape[0], r, c) for gp in gparts], rs(w), rs(m), rs(v), name=name,
                 silu_grad_of=None if silu_grad_of is None else rs(silu_grad_of))
    return [a.reshape(shape) for a in res]


def _place():
    return lax.axis_index("x"), lax.axis_index("y"), lax.axis_index("c")


def _all_gather(arrs, *, name):
    k_arr = len(arrs)

    def body(*refs):
        ins = refs[:k_arr]
        outs = refs[k_arr:2 * k_arr]
        send_sems, recv_sems, local_sems = refs[2 * k_arr:]
        x, y, c = _place()
        me, sibling = (x, y, c), (x, y, 1 - c)
        chips = [(1 - x, y), (x, 1 - y), (1 - x, 1 - y)]

        def slot(a, px, py, pc):
            return outs[a].at[4 * px + 2 * py + pc]

        def copy(a, s, block, to, src=None):
            return pltpu.make_async_remote_copy(
                src_ref=slot(a, *block) if src is None else src, dst_ref=slot(a, *block),
                send_sem=send_sems.at[a, s], recv_sem=recv_sems.at[a, s], device_id=to, device_id_type=MESH)

        mine = [pltpu.make_async_copy(ins[a], slot(a, *me), local_sems.at[a]) for a in range(k_arr)]
        for cp in mine:
            cp.start()
        first = []
        for a in range(k_arr):
            first.append(copy(a, 0, me, sibling, src=ins[a]))
            first += [copy(a, 1 + j, me, (*chip, c), src=ins[a]) for j, chip in enumerate(chips)]
        for cp in first:
            cp.start()
        passed = []
        for j, chip in enumerate(chips):
            for a in range(k_arr):
                copy(a, 1 + j, (*chip, c), me).wait_recv()
                fw = copy(a, 4 + j, (*chip, c), sibling)
                fw.start()
                passed.append(fw)
        for a in range(k_arr):
            copy(a, 0, sibling, me).wait_recv()
            for j, chip in enumerate(chips):
                copy(a, 4 + j, (*chip, 1 - c), me).wait_recv()
        for cp in first + passed:
            cp.wait_send()
        for cp in mine:
            cp.wait()

    any_spec = pl.BlockSpec(memory_space=pl.ANY)
    return pl.pallas_call(
        body, in_specs=[any_spec] * k_arr, out_specs=[any_spec] * k_arr,
        out_shape=[jax.ShapeDtypeStruct((NDEV,) + a.shape, a.dtype) for a in arrs],
        scratch_shapes=[pltpu.SemaphoreType.DMA((k_arr, 7)), pltpu.SemaphoreType.DMA((k_arr, 7)),
                        pltpu.SemaphoreType.DMA((k_arr,))],
        name=name)(*arrs)


_HBM = pl.BlockSpec(memory_space=pltpu.HBM)
_SEM = pl.BlockSpec(memory_space=pltpu.SEMAPHORE)
_EFFECT = pltpu.SideEffectType.DATAFLOW_SIDE_EFFECTING


def _peers(x, y, c):
    return [(x ^ (rel >> 2), y ^ ((rel >> 1) & 1), c ^ (rel & 1)) for rel in range(1, NDEV)]


def _exchange_copies(srcs, lands, send_sems, recv_sems, scatter):
    x, y, c = _place()
    me = 4 * x + 2 * y + c
    copies = []
    for r, (px, py, pc) in enumerate(_peers(x, y, c)):
        peer = 4 * px + 2 * py + pc
        for a in range(len(srcs)):
            copies.append(pltpu.make_async_remote_copy(
                src_ref=srcs[a].at[peer] if scatter else srcs[a], dst_ref=lands[a].at[me],
                send_sem=send_sems.at[7 * a + r], recv_sem=recv_sems.at[7 * a + r], device_id=(px, py, pc),
                device_id_type=MESH))
    return copies


def _exchange_start(arrs, *, scatter, name):
    k_arr = len(arrs)
    land_shapes = [a.shape if scatter else (NDEV,) + a.shape for a in arrs]
    lands = [pltpu.with_memory_space_constraint(lax.empty(s, a.dtype), pltpu.HBM) for s, a in zip(land_shapes, arrs)]
    srcs = [pltpu.with_memory_space_constraint(a, pltpu.HBM) for a in arrs]

    def body(*refs):
        src_refs, land_refs = refs[:k_arr], refs[k_arr:2 * k_arr]
        send_sems, recv_sems = refs[2 * k_arr], refs[2 * k_arr + 1]
        token = refs[-1]
        for cp in _exchange_copies(src_refs, land_refs, send_sems, recv_sems, scatter):
            cp.start()
        token[...] = jnp.zeros_like(token)

    out_shape = ([pltpu.SemaphoreType.DMA((7 * k_arr,)), pltpu.SemaphoreType.DMA((7 * k_arr,))]
                 + [pltpu.HBM(a.shape, a.dtype) for a in arrs] + [pltpu.HBM(s, a.dtype) for s, a in zip(land_shapes, arrs)]
                 + [jax.ShapeDtypeStruct((8, 128), F32)])
    res = pl.pallas_call(
        body, name=name, out_shape=out_shape, in_specs=[_HBM] * (2 * k_arr),
        out_specs=[_SEM, _SEM] + [_HBM] * (2 * k_arr) + [pl.BlockSpec(memory_space=pltpu.VMEM)],
        input_output_aliases={i: 2 + i for i in range(2 * k_arr)},
        compiler_params=pltpu.CompilerParams(has_side_effects=_EFFECT))(*srcs, *lands)
    return dict(send=res[0], recv=res[1], srcs=list(res[2:2 + k_arr]), lands=list(res[2 + k_arr:2 + 2 * k_arr]),
                token=res[-1], scatter=scatter)


def _exchange_wait(handle, after, *, name):
    k_arr = len(handle["srcs"])
    scatter = handle["scatter"]

    def body(*refs):
        src_refs, land_refs = refs[:k_arr], refs[k_arr:2 * k_arr]
        send_sems, recv_sems = refs[2 * k_arr], refs[2 * k_arr + 1]
        x, y, c = _place()
        me = 4 * x + 2 * y + c
        for r, (px, py, pc) in enumerate(_peers(x, y, c)):
            peer = 4 * px + 2 * py + pc
            for a in range(k_arr):
                cp = pltpu.make_async_remote_copy(
                    src_ref=src_refs[a].at[peer] if scatter else src_refs[a], dst_ref=land_refs[a].at[peer],
                    send_sem=send_sems.at[7 * a + r], recv_sem=recv_sems.at[7 * a + r], device_id=(x, y, c),
                    device_id_type=MESH)
                cp.wait_send()
                cp.wait_recv()

    arrs = handle["srcs"] + handle["lands"]
    res = pl.pallas_call(
        body, name=name, out_shape=[pltpu.HBM(a.shape, a.dtype) for a in arrs],
        in_specs=[_HBM] * (2 * k_arr) + [_SEM, _SEM, pl.BlockSpec(memory_space=pl.ANY)],
        out_specs=[_HBM] * (2 * k_arr), input_output_aliases={i: i for i in range(2 * k_arr)},
        compiler_params=pltpu.CompilerParams(has_side_effects=_EFFECT))(*arrs, handle["send"], handle["recv"], after)
    me = 4 * lax.axis_index("x") + 2 * lax.axis_index("y") + lax.axis_index("c")
    out = []
    for src, land in zip(res[:k_arr], res[k_arr:]):
        own = lax.dynamic_index_in_dim(src, me, 0, keepdims=False) if scatter else src
        out.append(lax.dynamic_update_index_in_dim(land, own, me, 0))
    return out


def _ffn_fwd(x_in, y, g, ymul, gain, sc, sh, w_up, cw, cb, w_down, tag):
    xr, f = _norm_mod(x_in, gain, sc, sh, y=y, g=g, ymul=ymul, name=f"ffn_norm_{tag}")
    u = _mm_w(f, w_up, name=f"ffn_up_{tag}")
    hmid = _glu_fwd(u, cw, cb, name=f"ffn_glu_{tag}", tm=1024)
    z = _mm_w(hmid, w_down, name=f"ffn_down_{tag}")
    return xr, f, u, hmid, z


def _ffn_bwd(dxo, xr, f, u, hmid, z, g2, gain, sc, w_up, cw, cb, w_down, tag):
    dz, dg2 = _gate_bwd(dxo, z, g2, name=f"ffn_gate_bwd_{tag}")
    dh = _mm_w(dz, w_down, tb=True, name=f"ffn_down_dx_{tag}")
    d_wdown = _mm_tn(hmid, dz, name=f"ffn_down_dw_{tag}")
    dug, duv, dcw, dcb = _glu_bwd(dh, u, cw, cb, name=f"ffn_glu_bwd_{tag}", tm=1024)
    df = _mm_w([dug, duv], w_up, tb=True, name=f"ffn_up_dx_{tag}")
    d_wup_g = _mm_tn(f, dug, name=f"ffn_up_dwg_{tag}")
    d_wup_v = _mm_tn(f, duv, name=f"ffn_up_dwv_{tag}")
    dxr, dsh2, dsc2, dgain = _norm_mod_bwd(df, xr, gain, sc, dres=dxo, name=f"ffn_norm_bwd_{tag}")
    return dxr, (d_wup_g, d_wup_v, d_wdown, dcw, dcb), (dsh2, dsc2, dg2, dgain)


def _split6(mod):
    return [mod[j * D:(j + 1) * D][None, :] for j in range(6)]


def _row(v):
    return v.reshape(1, -1)


def kernel(x, c, ctx, c_ctx, ada_w, ada_b, mix_norm, ffn_norm, even_w_in, even_q_gain, even_k_gain, even_conv_w, even_w_out, odd_pool_w, odd_pool_scale, ffn_w_up, ffn_conv_w, ffn_conv_b, ffn_w_down, loss_target, m_c_ctx, m_ada_w, m_ada_b, m_mix_norm, m_ffn_norm, m_even_w_in, m_even_q_gain, m_even_k_gain, m_even_conv_w, m_even_w_out, m_odd_pool_w, m_odd_pool_scale, m_ffn_w_up, m_ffn_conv_w, m_ffn_conv_b, m_ffn_w_down, v_c_ctx, v_ada_w, v_ada_b, v_mix_norm, v_ffn_norm, v_even_w_in, v_even_q_gain, v_even_k_gain, v_even_conv_w, v_even_w_out, v_odd_pool_w, v_odd_pool_scale, v_ffn_w_up, v_ffn_conv_w, v_ffn_conv_b, v_ffn_w_down):
    n = x.shape[1]
    lc = ctx.shape[1]
    me = 4 * lax.axis_index("x") + 2 * lax.axis_index("y") + lax.axis_index("c")
    xs, ctxs, tgt = x[0], ctx[0], loss_target[0]
    acols = ada_w.shape[2]

    small = jnp.concatenate([even_conv_w.reshape(-1), ffn_conv_w.reshape(-1), odd_pool_scale.reshape(-1)])
    nsmall = small.shape[0]
    small = jnp.pad(small, (0, (-nsmall) % 1024)).reshape(-1, 128)
    late_shards = [even_w_out[0].astype(BF16), odd_pool_w[0].astype(BF16), ffn_w_up.astype(BF16),
                   ffn_w_down.astype(BF16)]
    h_weights = _exchange_start(late_shards, scatter=False, name="weights_start")
    c_rows = jnp.pad(c, ((0, 7), (0, 0))) + h_weights["token"][0, 0]
    g_c, g_win, g_small = _all_gather([c_rows, even_w_in[0].astype(BF16), small], name="gather_first")
    w_in = g_win.transpose(1, 0, 2).reshape(D, -1)
    g_small = g_small.reshape(NDEV, -1)
    ecw = even_conv_w.shape[2]
    fcw = ffn_conv_w.shape[2]
    conv_w = g_small[:, :3 * ecw].reshape(NDEV, 3, ecw).transpose(1, 0, 2).reshape(3, CW)
    o1 = 3 * ecw
    fconv_w = g_small[:, o1:o1 + 6 * fcw].reshape(NDEV, 2, 3, fcw).transpose(1, 2, 0, 3).reshape(2, 3, DFF)
    o2 = o1 + 6 * fcw
    pool_scale = g_small[:, o2:o2 + D // NDEV].reshape(1, D)

    mraw = jnp.concatenate([g_c[:, 0, :], c_ctx[None, :], jnp.zeros((7, D), F32)], axis=0)
    my_bias = lax.dynamic_slice_in_dim(ada_b, me * acols, acols, axis=1)
    modp = jnp.stack([_mm(mraw, ada_w[l], silu_a=True, bias=my_bias[l:l + 1], name=f"ada_proj_{l}", tm=16, tn=256)
                      for l in range(2)])
    (g_mod,) = _all_gather([modp], name="gather_mod")
    mod_rows = g_mod.transpose(1, 2, 0, 3).reshape(2, 16, 6 * D)
    mod = lax.dynamic_index_in_dim(mod_rows, me, axis=1, keepdims=False)
    sh1, sc1, g1, sh2, sc2, g2 = _split6(mod[0])
    sh1b, sc1b, g1b, sh2b, sc2b, g2b = _split6(mod[1])
    csh1, csc1 = _split6(mod_rows[0, 8])[:2]
    mixn = [_row(mix_norm[l]) for l in range(2)]
    ffnn = [_row(ffn_norm[l]) for l in range(2)]
    qg, kg = _row(even_q_gain[0]), _row(even_k_gain[0])
    fcb = [_row(ffn_conv_b[l]) for l in range(2)]

    cs_t, sn_t = _rope_tables(n)
    a_lat = _norm_mod(xs, mixn[0], sc1, sh1, name="mix0_norm")
    a_ctx = _norm_mod(ctxs, mixn[0], csc1, csh1, name="mix0_norm_ctx")
    p_lat = _mm_w(a_lat, w_in, name="in_proj")
    p_ctx = _mm(a_ctx, w_in[:, AW:AW + 4 * HD], name="in_proj_ctx", tm=256, tn=512, tk=1024)
    q_r, k_lat, v_lat = _qkv_prep(p_lat, qg, kg, cs_t, sn_t, has_q=True, kv_col=1, name="qkv_prep")
    k_ctx, v_ctx = _qkv_prep(p_ctx, qg, kg, None, None, has_q=False, kv_col=0, name="qkv_prep_ctx")
    k_all = jnp.concatenate([k_ctx, k_lat], axis=1)
    v_all = jnp.concatenate([v_ctx, v_lat], axis=1)
    o_attn, lse = _attn_fwd(q_r, k_all, v_all, name="attn_fwd")
    cat = _conv_gate_fwd(p_lat, o_attn, conv_w, name="conv_gate")
    g_wout, g_pool, g_up, g_down = _exchange_wait(h_weights, cat, name="weights_wait")
    w_out = g_wout.reshape(D, D)
    pool_w = g_pool.transpose(1, 0, 2, 3).reshape(4, PG, PG)
    w_up = [g_up[:, l].transpose(1, 0, 2).reshape(D, 2 * DFF) for l in range(2)]
    w_down = [g_down[:, l].reshape(DFF, D) for l in range(2)]
    y0 = _mm_w(cat, w_out, name="out_proj", tm=512)
    x1, f0, u0, h0, z0 = _ffn_fwd(xs, y0, g1, None, ffnn[0], sc2, sh2, w_up[0], fconv_w[0], fcb[0], w_down[0], "l0")

    x2, mixed = _pool_fwd(x1, z0, g2, mixn[1], sc1b, sh1b, pool_w, name="pool_fwd")
    x3, f1, u1, h1, z1 = _ffn_fwd(x2, mixed, g1b, pool_scale, ffnn[1], sc2b, sh2b, w_up[1], fconv_w[1], fcb[1],
                                  w_down[1], "l1")
    dx4, loss_part = _loss_head(x3, z1, g2b, tgt, name="loss_head")
    loss = lax.psum(loss_part[0, 0], ("x", "y", "c"))

    dx3, (dup1g, dup1v, ddown1, dfcw1, dfcb1), (dsh2b, dsc2b, dg2b, dffn1) = _ffn_bwd(
        dx4, x3, f1, u1, h1, z1, g2b, ffnn[1], sc2b, w_up[1], fconv_w[1], fcb[1], w_down[1], "l1")
    dx2, dpool_w, dg1b, dpscale, dsh1b, dsc1b, dmix1 = _pool_bwd(
        dx3, mixed, x2, g1b, pool_scale, mixn[1], sc1b, sh1b, pool_w, name="pool_bwd")

    def up_shards(dg, dv):
        return jnp.concatenate([dg, dv], axis=1).reshape(D, NDEV, -1).transpose(1, 0, 2)

    s_pool = dpool_w.astype(BF16).reshape(4, NDEV, PG // NDEV, PG).transpose(1, 0, 2, 3)
    h_g1 = _exchange_start([s_pool, up_shards(dup1g, dup1v), ddown1.reshape(NDEV, DFF // NDEV, D)], scatter=True,
                           name="grads1_start")

    dx1, (dup0g, dup0v, ddown0, dfcw0, dfcb0), (dsh2, dsc2, dg2, dffn0) = _ffn_bwd(
        dx2, x1, f0, u0, h0, z0, g2 + h_g1["token"][0, 0], ffnn[0], sc2, w_up[0], fconv_w[0], fcb[0], w_down[0], "l0")
    h_g0 = _exchange_start([up_shards(dup0g, dup0v), ddown0.reshape(NDEV, DFF // NDEV, D)], scatter=True,
                           name="grads0_start")
    dy0, dg1 = _gate_bwd(dx1, y0, g1 + h_g0["token"][0, 0], name="mix0_gate_bwd")
    dcat = _mm_w(dy0, w_out, tb=True, name="out_proj_dx", tm=512)
    d_wout = _mm_tn(cat, dy0, name="out_proj_dw")
    dp_conv, dconv_w = _conv_gate_bwd(dcat, p_lat, conv_w, name="conv_gate_bwd")
    do_h, delta = _attn_bwd_prep(dcat, o_attn, name="attn_bwd_prep")
    dq_r, dk_all, dv_all = _attn_bwd(q_r, k_all, v_all, do_h, lse, delta, name="attn_bwd")
    dp_qkv, dqg_l, dkg_l = _qkv_bwd(p_lat, dq_r, dk_all, dv_all, qg, kg, cs_t, sn_t, has_q=True, kv_col=1,
                                    kv_row_off=lc, name="qkv_bwd")
    dp_ctx, _zero_qg, dkg_c = _qkv_bwd(p_ctx, None, dk_all, dv_all, qg, kg, None, None, has_q=False, kv_col=0,
                                       kv_row_off=0, name="qkv_bwd_ctx")
    da_lat = _mm_w([dp_qkv, dp_conv], w_in, tb=True, name="in_proj_dx", tm=512)
    da_ctx = _mm(dp_ctx, w_in[:, :D], tb=True, name="in_proj_dx_ctx", tm=256, tn=512, tk=1024)
    a_all = jnp.concatenate([a_lat, a_ctx], axis=0)
    dp_all = jnp.concatenate([dp_qkv, dp_ctx], axis=0)
    d_win_qkv = _mm_tn(a_all, dp_all, name="in_proj_dw_qkv", tk=768)
    d_win_conv = _mm_tn(a_lat, dp_conv, name="in_proj_dw_conv")
    d_win = jnp.concatenate([d_win_qkv, d_win_conv], axis=1)
    h_ga = _exchange_start([d_win.reshape(D, NDEV, -1).transpose(1, 0, 2), d_wout.reshape(NDEV, D // NDEV, D)],
                           scatter=True, name="grads_attn_start")
    mixn0_late = mixn[0] + h_ga["token"][0, 0]
    grad_x, dsh1, dsc1, dmix0 = _norm_mod_bwd(da_lat, xs, mixn0_late, sc1, dres=dx1, name="mix0_norm_bwd")
    _dctx, dcsh1, dcsc1, dmix0c = _norm_mod_bwd(da_ctx, ctxs, mixn0_late, csc1, name="mix0_norm_bwd_ctx")

    z1k = jnp.zeros((1, D), F32)
    pack = jnp.concatenate(
        [v.reshape(-1) for v in (dsh1, dsc1, dg1, dsh2, dsc2, dg2, dsh1b, dsc1b, dg1b, dsh2b, dsc2b, dg2b,
                                 dcsh1, dcsc1, z1k, z1k, z1k, z1k,
                                 dmix0, dmix1, dmix0c, z1k, dffn0, dffn1, dqg_l, dkg_l + dkg_c,
                                 dfcb0, dfcb1, dconv_w, dfcw0, dfcw1, dpscale)])
    npack = pack.shape[0]
    pack = jnp.pad(pack, (0, (-npack) % 1024)).reshape(-1, 128)
    (g_pack,) = _all_gather([pack], name="gather_small_grads")
    gp = g_pack.reshape(NDEV, -1)
    off = [0]

    def take(size):
        seg = gp[:, off[0]:off[0] + size]
        off[0] += size
        return seg

    dmod_all = take(12 * D).reshape(NDEV, 2, 6 * D)
    dmodc_all = take(6 * D).reshape(NDEV, 1, 6 * D)
    dmix_all = take(4 * D).reshape(NDEV, 2, 2, D)
    dffn_all = take(2 * D).reshape(NDEV, 2, D)
    dqg_all = take(HD).reshape(NDEV, 1, HD)
    dkg_all = take(HD).reshape(NDEV, 1, HD)
    dfcb_all = take(2 * DFF).reshape(NDEV, 2, DFF)
    dconvw_all = take(3 * CW).reshape(NDEV, 3, CW)
    dfcw_all = take(6 * DFF).reshape(NDEV, 2, 3, DFF)
    dpscale_all = take(D).reshape(NDEV, D)

    outs = {}

    def put(nm, res):
        outs["grad_" + nm], outs["delta_" + nm], outs["new_m_" + nm], outs["new_v_" + nm] = res

    dmodc_pad = jnp.concatenate([dmodc_all, jnp.zeros_like(dmodc_all)], axis=1)
    put("ada_b", _adamw_nd(jnp.concatenate([dmod_all, dmodc_pad], axis=0), ada_b, m_ada_b, v_ada_b, name="adam_ada_b"))
    put("mix_norm", _adamw_nd(jnp.concatenate([dmix_all[:, 0], dmix_all[:, 1]], axis=0), mix_norm, m_mix_norm,
                              v_mix_norm, name="adam_mix_norm"))
    put("ffn_norm", _adamw_nd(dffn_all, ffn_norm, m_ffn_norm, v_ffn_norm, name="adam_ffn_norm"))
    put("even_q_gain", _adamw_nd(dqg_all, even_q_gain, m_even_q_gain, v_even_q_gain, name="adam_q_gain"))
    put("even_k_gain", _adamw_nd(dkg_all, even_k_gain, m_even_k_gain, v_even_k_gain, name="adam_k_gain"))
    put("ffn_conv_b", _adamw_nd(dfcb_all, ffn_conv_b, m_ffn_conv_b, v_ffn_conv_b, name="adam_ffn_conv_b"))
    my_convw = lax.dynamic_slice_in_dim(dconvw_all, me * ecw, ecw, axis=2)[:, None]
    put("even_conv_w", _adamw_nd(my_convw, even_conv_w, m_even_conv_w, v_even_conv_w, name="adam_even_conv_w"))
    my_fcw = lax.dynamic_slice_in_dim(dfcw_all, me * fcw, fcw, axis=3)
    put("ffn_conv_w", _adamw_nd(my_fcw, ffn_conv_w, m_ffn_conv_w, v_ffn_conv_w, name="adam_ffn_conv_w"))
    my_ps = lax.dynamic_slice_in_dim(dpscale_all, me * (D // NDEV), D // NDEV, axis=1)[:, None]
    put("odd_pool_scale", _adamw_nd(my_ps, odd_pool_scale, m_odd_pool_scale, v_odd_pool_scale, name="adam_pool_scale"))

    dmodc_sum = dmodc_all[0]
    for dev in range(1, NDEV):
        dmodc_sum = dmodc_sum + dmodc_all[dev]
    my_cols = lambda a: lax.dynamic_slice_in_dim(a, me * acols, acols, axis=a.ndim - 1)
    rows0 = jnp.concatenate([my_cols(dmod_all[:, 0]), my_cols(dmodc_sum), jnp.zeros((7, acols), F32)], axis=0)
    rows1 = jnp.concatenate([my_cols(dmod_all[:, 1]), jnp.zeros((8, acols), F32)], axis=0)
    d_ada = jnp.stack([_mm(mraw, rows, ta=True, silu_a=True, name=f"ada_dw_{l}", tm=512, tn=256, tk=16)
                       for l, rows in enumerate((rows0, rows1))])
    put("ada_w", _adamw_nd(d_ada[None], ada_w, m_ada_w, v_ada_w, name="adam_ada_w"))
    dscc_part = _mm(rows0, ada_w[0], tb=True, name="ada_dcctx", tm=16, tn=512, tk=256)
    (g_dscc,) = _all_gather([dscc_part[8:16]], name="gather_dcctx")
    put("c_ctx", _adamw_nd(g_dscc[:, 0:1, :].reshape(NDEV, D), c_ctx, m_c_ctx, v_c_ctx, name="adam_c_ctx",
                           silu_grad_of=c_ctx))

    r_pool, r_up1, r_down1 = _exchange_wait(h_g1, outs["grad_ada_b"], name="grads1_wait")
    r_up0, r_down0 = _exchange_wait(h_g0, outs["grad_mix_norm"], name="grads0_wait")
    r_win, r_wout = _exchange_wait(h_ga, outs["grad_c_ctx"], name="grads_attn_wait")
    put("even_w_in", _adamw_nd(r_win[:, None], even_w_in, m_even_w_in, v_even_w_in, name="adam_w_in"))
    put("even_w_out", _adamw_nd(r_wout[:, None], even_w_out, m_even_w_out, v_even_w_out, name="adam_w_out"))
    put("odd_pool_w", _adamw_nd(r_pool[:, None], odd_pool_w, m_odd_pool_w, v_odd_pool_w, name="adam_pool_w"))
    put("ffn_w_up", _adamw_nd([r_up0, r_up1], ffn_w_up, m_ffn_w_up, v_ffn_w_up, name="adam_w_up"))
    put("ffn_w_down", _adamw_nd([r_down0, r_down1], ffn_w_down, m_ffn_w_down, v_ffn_w_down, name="adam_w_down"))

    names = ["c_ctx", "ada_w", "ada_b", "mix_norm", "ffn_norm", "even_w_in", "even_q_gain", "even_k_gain",
             "even_conv_w", "even_w_out", "odd_pool_w", "odd_pool_scale", "ffn_w_up", "ffn_conv_w", "ffn_conv_b",
             "ffn_w_down"]
    result = [loss, grad_x[None]]
    for kind in ("grad_", "delta_", "new_m_", "new_v_"):
        result += [outs[kind + nm] for nm in names]
    return tuple(result)
```

```python
import functools
import math

import jax
import jax.numpy as jnp
from jax import lax
from jax.experimental import pallas as pl
from jax.experimental.pallas import tpu as pltpu

F32 = jnp.float32
BF16 = jnp.bfloat16

D = 1024
HD = 128
NQ = 4
NKV = 2
AW = NQ * HD
CW = D - AW
DFF = 2816
GRID_W = 64
ROPE_THETA = 10000.0
POOL_WINDOWS = (2, 4, 8, 16)
PG = D // 4
EPS = 1e-6
NDEV = 8
HALO = 8
MESH = pl.DeviceIdType.MESH

ADAM_LR = 0.001
ADAM_B1 = 0.9
ADAM_B2 = 0.999
ADAM_EPS = 1e-08
ADAM_WD = 0.01
ADAM_STEP = 10


def _pick(dim, prefs):
    for p in prefs:
        if dim % p == 0:
            return p
    return dim


def _params(*sem):
    return pltpu.CompilerParams(dimension_semantics=sem)


_NT = (((1,), (1,)), ((), ()))
_TN = (((0,), (0,)), ((), ()))
_SCALE = HD ** -0.5
_QSCALE = _SCALE * math.log2(math.e)
_LN2 = math.log(2.0)


def _mm(a_list, b, *, name, ta=False, tb=False, out_dtype=F32, silu_a=False, bias=None, tm=None, tn=None, tk=None):
    if not isinstance(a_list, (list, tuple)):
        a_list = [a_list]
    na = len(a_list)
    assert not (ta and na > 1)
    if ta:
        kdim, m = a_list[0].shape
        ks = [kdim]
    else:
        m = a_list[0].shape[0]
        ks = [a.shape[1] for a in a_list]
        kdim = sum(ks)
    n = b.shape[0] if tb else b.shape[1]
    assert (b.shape[1] if tb else b.shape[0]) == kdim
    kunit = math.gcd(*ks) if na > 1 else kdim
    tm = min(tm, m) if tm else _pick(m, (512, 256, 128, 64, 32, 16, 8))
    tn = min(tn, n) if tn else _pick(n, (512, 256, 128))
    tk = min(tk, kunit) if tk else _pick(kunit, (1024, 768, 512, 256, 128))
    assert m % tm == 0 and n % tn == 0 and all(k % tk == 0 for k in ks)
    nks = [k // tk for k in ks]
    starts = [sum(nks[:i]) for i in range(na)]
    nk = sum(nks)
    has_bias = bias is not None

    def body(*refs):
        a_refs = refs[:na]
        b_ref = refs[na]
        bias_ref = refs[na + 1] if has_bias else None
        o_ref = refs[na + 1 + has_bias]
        acc = refs[-1]
        k = pl.program_id(2)

        @pl.when(k == 0)
        def _():
            acc[...] = jnp.zeros_like(acc)

        bv = b_ref[...].astype(BF16)
        dn = (((0 if ta else 1,), (1 if tb else 0,)), ((), ()))
        for idx in range(na):
            def step(idx=idx):
                av = a_refs[idx][...]
                if silu_a:
                    av = av * jax.nn.sigmoid(av)
                acc[...] += lax.dot_general(av.astype(BF16), bv, dn, preferred_element_type=F32)
            if na == 1:
                step()
            else:
                pl.when((k >= starts[idx]) & (k < starts[idx] + nks[idx]))(step)

        @pl.when(k == nk - 1)
        def _():
            r = acc[...]
            if has_bias:
                r = r + bias_ref[...]
            o_ref[...] = r.astype(o_ref.dtype)

    in_specs = []
    for idx in range(na):
        if ta:
            in_specs.append(pl.BlockSpec((tk, tm), lambda i, j, k: (k, i)))
        else:
            lo, cnt = starts[idx], nks[idx]
            in_specs.append(pl.BlockSpec((tm, tk), lambda i, j, k, lo=lo, cnt=cnt: (i, jnp.clip(k - lo, 0, cnt - 1))))
    if tb:
        in_specs.append(pl.BlockSpec((tn, tk), lambda i, j, k: (j, k)))
    else:
        in_specs.append(pl.BlockSpec((tk, tn), lambda i, j, k: (k, j)))
    args = list(a_list) + [b]
    if has_bias:
        in_specs.append(pl.BlockSpec((1, tn), lambda i, j, k: (0, j)))
        args.append(bias)
    return pl.pallas_call(
        body, grid=(m // tm, n // tn, nk), in_specs=in_specs,
        out_specs=pl.BlockSpec((tm, tn), lambda i, j, k: (i, j)),
        out_shape=jax.ShapeDtypeStruct((m, n), out_dtype),
        scratch_shapes=[pltpu.VMEM((tm, tn), F32)], name=name,
        compiler_params=_params("parallel", "parallel", "arbitrary"))(*args)


def _mm_w(a_list, w, *, name, tb=False, tm=256, out_dtype=F32):
    if not isinstance(a_list, (list, tuple)):
        a_list = [a_list]
    na = len(a_list)
    m = a_list[0].shape[0]
    ks = [a.shape[1] for a in a_list]
    offs = [sum(ks[:i]) for i in range(na)]
    n = w.shape[0] if tb else w.shape[1]
    assert (w.shape[1] if tb else w.shape[0]) == sum(ks)
    tm = min(tm, m)
    assert m % tm == 0

    def body(*refs):
        a_refs, w_ref, o_ref = refs[:na], refs[na], refs[na + 1]
        acc = None
        for idx in range(na):
            av = a_refs[idx][...].astype(BF16)
            if tb:
                part = lax.dot_general(av, w_ref[:, offs[idx]:offs[idx] + ks[idx]], _NT, preferred_element_type=F32)
            else:
                part = jnp.dot(av, w_ref[offs[idx]:offs[idx] + ks[idx], :], preferred_element_type=F32)
            acc = part if acc is None else acc + part
        o_ref[...] = acc.astype(o_ref.dtype)

    in_specs = [pl.BlockSpec((tm, k), lambda i: (i, 0)) for k in ks] + [pl.BlockSpec(w.shape, lambda i: (0, 0))]
    return pl.pallas_call(
        body, grid=(m // tm,), in_specs=in_specs, out_specs=pl.BlockSpec((tm, n), lambda i: (i, 0)),
        out_shape=jax.ShapeDtypeStruct((m, n), out_dtype), name=name, compiler_params=_params("parallel"))(*a_list, w)


def _mm_tn(pairs, *, name, tk=1024, out_dtype=BF16):
    if not isinstance(pairs, list):
        pairs = [pairs]
    m, n = pairs[0][0].shape[1], pairs[0][1].shape[1]
    tks = [min(tk, a.shape[0]) for a, _ in pairs]
    nks = [a.shape[0] // t for (a, _), t in zip(pairs, tks)]
    assert all(a.shape[0] == b.shape[0] and a.shape[0] % t == 0 for (a, b), t in zip(pairs, tks))
    starts = [sum(nks[:i]) for i in range(len(pairs))]
    nk = sum(nks)

    def body(*refs):
        o_ref, acc = refs[-2], refs[-1]
        k = pl.program_id(0)
        for idx in range(len(pairs)):
            a_ref, b_ref = refs[2 * idx], refs[2 * idx + 1]

            def step(a_ref=a_ref, b_ref=b_ref, first=(idx == 0)):
                part = lax.dot_general(a_ref[...], b_ref[...], _TN, preferred_element_type=F32)
                if first:
                    @pl.when(k == 0)
                    def _():
                        acc[...] = part

                    @pl.when(k > 0)
                    def _():
                        acc[...] += part
                else:
                    acc[...] += part

            if len(pairs) == 1:
                step()
            else:
                pl.when((k >= starts[idx]) & (k < starts[idx] + nks[idx]))(step)

        @pl.when(k == nk - 1)
        def _():
            o_ref[...] = acc[...].astype(o_ref.dtype)

    in_specs, args = [], []
    for (a, b), t, lo, cnt in zip(pairs, tks, starts, nks):
        idx_map = lambda k, lo=lo, cnt=cnt: (jnp.clip(k - lo, 0, cnt - 1), 0)
        in_specs += [pl.BlockSpec((t, m), idx_map), pl.BlockSpec((t, n), idx_map)]
        args += [a, b]
    return pl.pallas_call(
        body, grid=(nk,), in_specs=in_specs, out_specs=pl.BlockSpec((m, n), lambda k: (0, 0)),
        out_shape=jax.ShapeDtypeStruct((m, n), out_dtype), scratch_shapes=[pltpu.VMEM((m, n), F32)], name=name,
        compiler_params=_params("arbitrary"))(*args)


def _vec(d, col=None):
    if col is None:
        return pl.BlockSpec((1, d), lambda i, *_: (0, 0))
    return pl.BlockSpec((1, d), col)


def _halo_specs(tm, width, nrows, colblk=0, row_off=0):
    r = tm // HALO
    off = row_off // HALO
    last = nrows // HALO - 1
    prev = pl.BlockSpec((HALO, width), lambda i, *_: (off + jnp.maximum(i * r - 1, 0), colblk))
    nxt = pl.BlockSpec((HALO, width), lambda i, *_: (off + jnp.minimum((i + 1) * r, last), colblk))
    return prev, nxt


def _ext(prev_ref, main_ref, next_ref, i, ni):
    p = jnp.where(i > 0, prev_ref[...], 0.0)
    n = jnp.where(i < ni - 1, next_ref[...], 0.0)
    return jnp.concatenate([p, main_ref[...], n], axis=0)


def _sh(ext, k, tm):
    if k == 0:
        return ext[HALO:HALO + tm]
    rows = ext.shape[0]
    return pltpu.roll(ext, (-k) % rows, axis=0)[HALO:HALO + tm]


def _roll_rows(v, k):
    rows = v.shape[0]
    return pltpu.roll(v, (-k) % rows, axis=0) if k % rows else v


def _conv3(ext, w_ref, tm):
    return _sh(ext, -1, tm) * w_ref[0:1, :] + _sh(ext, 0, tm) * w_ref[1:2, :] + _sh(ext, 1, tm) * w_ref[2:3, :]


def _colsum(v):
    return jnp.sum(v, axis=0, keepdims=True)


def _acc_out(ref, i, val):
    @pl.when(i == 0)
    def _():
        ref[...] = val

    @pl.when(i > 0)
    def _():
        ref[...] += val


def _sigmoid(v):
    return jax.nn.sigmoid(v)


def _norm_mod(x, gain, sc, sh, *, name, y=None, g=None, ymul=None, tm=512):
    n, d = x.shape
    tm = min(tm, n)
    has_res = y is not None
    has_mul = ymul is not None

    def body(*refs):
        it = iter(refs)
        x_ref = next(it)
        y_ref = next(it) if has_res else None
        g_ref = next(it) if has_res else None
        m_ref = next(it) if has_mul else None
        gain_ref, sc_ref, sh_ref = next(it), next(it), next(it)
        xo_ref = next(it) if has_res else None
        a_ref = next(it)
        xv = x_ref[...]
        if has_res:
            yv = y_ref[...]
            if has_mul:
                yv = yv * m_ref[...]
            xv = xv + g_ref[...] * yv
            xo_ref[...] = xv
        r = lax.rsqrt(jnp.mean(xv * xv, axis=-1, keepdims=True) + EPS)
        nrm = (xv * r) * gain_ref[...]
        a_ref[...] = (nrm * (1.0 + sc_ref[...]) + sh_ref[...]).astype(BF16)

    row = pl.BlockSpec((tm, d), lambda i: (i, 0))
    in_specs, args = [row], [x]
    if has_res:
        in_specs += [row, _vec(d)]
        args += [y, g]
    if has_mul:
        in_specs.append(_vec(d))
        args.append(ymul)
    in_specs += [_vec(d)] * 3
    args += [gain, sc, sh]
    out_specs, out_shape = [], []
    if has_res:
        out_specs.append(row)
        out_shape.append(jax.ShapeDtypeStruct((n, d), F32))
    out_specs.append(row)
    out_shape.append(jax.ShapeDtypeStruct((n, d), BF16))
    res = pl.pallas_call(body, grid=(n // tm,), in_specs=in_specs, out_specs=out_specs, out_shape=out_shape,
                         name=name, compiler_params=_params("parallel"))(*args)
    return res if has_res else res[0]


def _norm_mod_bwd(da, x, gain, sc, *, name, dres=None, gate_y=None, gate_g=None, tm=512):
    n, d = x.shape
    tm = min(tm, n)
    has_res = dres is not None
    has_gate = gate_y is not None

    def body(*refs):
        it = iter(refs)
        da_ref, x_ref = next(it), next(it)
        r_ref = next(it) if has_res else None
        y_ref = next(it) if has_gate else None
        g_ref = next(it) if has_gate else None
        gain_ref, sc_ref = next(it), next(it)
        dx_ref, dsh_ref, dsc_ref, dgn_ref = next(it), next(it), next(it), next(it)
        dy_ref = next(it) if has_gate else None
        dg_ref = next(it) if has_gate else None
        i = pl.program_id(0)
        xv = x_ref[...]
        dav = da_ref[...]
        r = lax.rsqrt(jnp.mean(xv * xv, axis=-1, keepdims=True) + EPS)
        xh = xv * r
        nrm = xh * gain_ref[...]
        dn = dav * (1.0 + sc_ref[...])
        dxh = dn * gain_ref[...]
        dx = r * (dxh - xh * jnp.mean(dxh * xh, axis=-1, keepdims=True))
        if has_res:
            dx = dx + r_ref[...]
        dx_ref[...] = dx
        _acc_out(dsh_ref, i, _colsum(dav))
        _acc_out(dsc_ref, i, _colsum(dav * nrm))
        _acc_out(dgn_ref, i, _colsum(dn * xh))
        if has_gate:
            dy_ref[...] = (dx * g_ref[...]).astype(BF16)
            _acc_out(dg_ref, i, _colsum(dx * y_ref[...]))

    row = pl.BlockSpec((tm, d), lambda i: (i, 0))
    in_specs, args = [row, row], [da, x]
    if has_res:
        in_specs.append(row)
        args.append(dres)
    if has_gate:
        in_specs += [row, _vec(d)]
        args += [gate_y, gate_g]
    in_specs += [_vec(d)] * 2
    args += [gain, sc]
    vec_shape = jax.ShapeDtypeStruct((1, d), F32)
    out_specs = [row, _vec(d), _vec(d), _vec(d)]
    out_shape = [jax.ShapeDtypeStruct((n, d), F32), vec_shape, vec_shape, vec_shape]
    if has_gate:
        out_specs += [row, _vec(d)]
        out_shape += [jax.ShapeDtypeStruct((n, d), BF16), vec_shape]
    return pl.pallas_call(
        body, grid=(n // tm,), in_specs=in_specs, out_specs=out_specs, out_shape=out_shape,
        name=name, compiler_params=_params("arbitrary"))(*args)


def _loss_head(x, z, g, tgt, *, name, tm=512):
    n, d = x.shape
    tm = min(tm, n)

    def body(x_ref, z_ref, g_ref, t_ref, dx_ref, loss_ref, dz_ref, dg_ref):
        i = pl.program_id(0)
        zv = z_ref[...]
        diff = (x_ref[...] + g_ref[...] * zv) - t_ref[...]
        dx = diff * (1.0 / d)
        dx_ref[...] = dx
        part = 0.5 * jnp.sum(jnp.mean(diff * diff, axis=-1, keepdims=True), axis=0, keepdims=True)
        _acc_out(loss_ref, i, jnp.broadcast_to(part, (1, 128)))
        dz_ref[...] = (dx * g_ref[...]).astype(BF16)
        _acc_out(dg_ref, i, _colsum(dx * zv))

    row = pl.BlockSpec((tm, d), lambda i: (i, 0))
    return pl.pallas_call(
        body, grid=(n // tm,), in_specs=[row, row, _vec(d), row], out_specs=[row, _vec(128), row, _vec(d)],
        out_shape=[jax.ShapeDtypeStruct((n, d), F32), jax.ShapeDtypeStruct((1, 128), F32),
                   jax.ShapeDtypeStruct((n, d), BF16), jax.ShapeDtypeStruct((1, d), F32)],
        name=name, compiler_params=_params("arbitrary"))(x, z, g, tgt)


def _glu_fwd(u, cw, cb, *, name, tm=256, tc=256):
    n = u.shape[0]
    tm = min(tm, n)
    nc = DFF // tc
    ni = n // tm

    def body(g_ref, gp_ref, gn_ref, v_ref, cw_ref, cb_ref, h_ref):
        i = pl.program_id(0)
        gext = _ext(gp_ref, g_ref, gn_ref, i, ni)
        gc = _conv3(gext, cw_ref, tm) + cb_ref[...]
        h_ref[...] = (gc * _sigmoid(gc) * v_ref[...]).astype(BF16)

    prev = pl.BlockSpec((HALO, tc), lambda i, j: (jnp.maximum(i * (tm // HALO) - 1, 0), j))
    nxt = pl.BlockSpec((HALO, tc), lambda i, j: (jnp.minimum((i + 1) * (tm // HALO), n // HALO - 1), j))
    return pl.pallas_call(
        body, grid=(ni, nc),
        in_specs=[pl.BlockSpec((tm, tc), lambda i, j: (i, j)), prev, nxt,
                  pl.BlockSpec((tm, tc), lambda i, j: (i, nc + j)),
                  pl.BlockSpec((3, tc), lambda i, j: (0, j)), pl.BlockSpec((1, tc), lambda i, j: (0, j))],
        out_specs=pl.BlockSpec((tm, tc), lambda i, j: (i, j)),
        out_shape=jax.ShapeDtypeStruct((n, DFF), BF16), name=name,
        compiler_params=_params("parallel", "parallel"))(u, u, u, u, cw, cb)


def _glu_bwd(dh, u, cw, cb, *, name, tm=256, tc=256):
    n = u.shape[0]
    tm = min(tm, n)
    nc = DFF // tc
    ni = n // tm
    rows = tm + 2 * HALO

    def body(dh_ref, dhp_ref, dhn_ref, g_ref, gp_ref, gn_ref, v_ref, vp_ref, vn_ref, cw_ref, cb_ref,
             dg_ref, dv_ref, dcw_ref, dcb_ref):
        i = pl.program_id(1)
        gext = _ext(gp_ref, g_ref, gn_ref, i, ni)
        dhext = _ext(dhp_ref, dh_ref, dhn_ref, i, ni)
        vext = _ext(vp_ref, v_ref, vn_ref, i, ni)
        gc = (_roll_rows(gext, -1) * cw_ref[0:1, :] + gext * cw_ref[1:2, :] + _roll_rows(gext, 1) * cw_ref[2:3, :]
              + cb_ref[...])
        sg = _sigmoid(gc)
        dgc = dhext * vext * (sg * (1.0 + gc * (1.0 - sg)))
        dv_ref[...] = (dh_ref[...] * (gc[HALO:HALO + tm] * sg[HALO:HALO + tm])).astype(BF16)
        dgate = (_sh(dgc, 1, tm) * cw_ref[0:1, :] + _sh(dgc, 0, tm) * cw_ref[1:2, :] + _sh(dgc, -1, tm) * cw_ref[2:3, :])
        dg_ref[...] = dgate.astype(BF16)
        dgc_t = dgc[HALO:HALO + tm]
        dcw = jnp.concatenate([_colsum(dgc_t * _sh(gext, -1, tm)), _colsum(dgc_t * _sh(gext, 0, tm)),
                               _colsum(dgc_t * _sh(gext, 1, tm))], axis=0)
        _acc_out(dcw_ref, i, dcw)
        _acc_out(dcb_ref, i, _colsum(dgc_t))

    r = tm // HALO
    last = n // HALO - 1

    def trio(off):
        return [pl.BlockSpec((tm, tc), lambda j, i: (i, off + j)),
                pl.BlockSpec((HALO, tc), lambda j, i: (jnp.maximum(i * r - 1, 0), off + j)),
                pl.BlockSpec((HALO, tc), lambda j, i: (jnp.minimum((i + 1) * r, last), off + j))]

    del rows
    return pl.pallas_call(
        body, grid=(nc, ni),
        in_specs=trio(0) + trio(0) + trio(nc) + [pl.BlockSpec((3, tc), lambda j, i: (0, j)),
                                                 pl.BlockSpec((1, tc), lambda j, i: (0, j))],
        out_specs=[pl.BlockSpec((tm, tc), lambda j, i: (i, j)), pl.BlockSpec((tm, tc), lambda j, i: (i, j)),
                   pl.BlockSpec((3, tc), lambda j, i: (0, j)), pl.BlockSpec((1, tc), lambda j, i: (0, j))],
        out_shape=[jax.ShapeDtypeStruct((n, DFF), BF16), jax.ShapeDtypeStruct((n, DFF), BF16),
                   jax.ShapeDtypeStruct((3, DFF), F32), jax.ShapeDtypeStruct((1, DFF), F32)],
        name=name, compiler_params=_params("parallel", "arbitrary"))(dh, dh, dh, u, u, u, u, u, u, cw, cb)


def _rope_tables(n):
    rows = n // GRID_W
    row_ids = jnp.repeat(jnp.arange(rows), GRID_W).astype(F32)
    col_ids = jnp.tile(jnp.arange(GRID_W), rows).astype(F32)
    axis_dim = HD // 2
    inv_freq = jnp.power(ROPE_THETA, -jnp.arange(0, axis_dim, 2, dtype=F32) / axis_dim)
    ar = row_ids[:, None] * inv_freq
    ac = col_ids[:, None] * inv_freq
    cs = jnp.concatenate([jnp.cos(ar), jnp.cos(ar), jnp.cos(ac), jnp.cos(ac)], axis=1)
    sn = jnp.concatenate([-jnp.sin(ar), jnp.sin(ar), -jnp.sin(ac), jnp.sin(ac)], axis=1)
    return cs, sn


def _partner(v):
    lane = lax.broadcasted_iota(jnp.int32, v.shape, 1)
    return jnp.where((lane % 64) < 32, pltpu.roll(v, HD - 32, axis=1), pltpu.roll(v, 32, axis=1))


def _qkv_prep(p, q_gain, k_gain, cs, sn, *, name, has_q, kv_col, tm=256):
    n = p.shape[0]
    rope = cs is not None

    def body(*refs):
        it = iter(refs)
        q_ref = next(it) if has_q else None
        kv_ref = next(it)
        qg_ref, kg_ref = next(it), next(it)
        cs_ref = next(it) if rope else None
        sn_ref = next(it) if rope else None
        qo_ref = next(it) if has_q else None
        ko_ref, vo_ref = next(it), next(it)

        def norm_rope(xh, gain, mul=None):
            r = lax.rsqrt(jnp.mean(xh * xh, axis=-1, keepdims=True) + EPS)
            xn = (xh * r) * gain
            if rope:
                xn = xn * cs_ref[...] + _partner(xn) * sn_ref[...]
            if mul is not None:
                xn = xn * mul
            return xn.astype(BF16)

        if has_q:
            for h in range(NQ):
                qo_ref[h] = norm_rope(q_ref[:, h * HD:(h + 1) * HD], qg_ref[...], _QSCALE)
        for h in range(NKV):
            ko_ref[h] = norm_rope(kv_ref[:, h * HD:(h + 1) * HD], kg_ref[...])
            vo_ref[h] = kv_ref[:, (NKV + h) * HD:(NKV + h + 1) * HD].astype(BF16)

    in_specs, args = [], []
    if has_q:
        in_specs.append(pl.BlockSpec((tm, AW), lambda i: (i, 0)))
        args.append(p)
    in_specs += [pl.BlockSpec((tm, 2 * NKV * HD), lambda i: (i, kv_col)), _vec(HD), _vec(HD)]
    args += [p, q_gain, k_gain]
    if rope:
        in_specs += [pl.BlockSpec((tm, HD), lambda i: (i, 0))] * 2
        args += [cs, sn]
    out_specs, out_shape = [], []
    if has_q:
        out_specs.append(pl.BlockSpec((NQ, tm, HD), lambda i: (0, i, 0)))
        out_shape.append(jax.ShapeDtypeStruct((NQ, n, HD), BF16))
    out_specs += [pl.BlockSpec((NKV, tm, HD), lambda i: (0, i, 0))] * 2
    out_shape += [jax.ShapeDtypeStruct((NKV, n, HD), BF16)] * 2
    return pl.pallas_call(body, grid=(n // tm,), in_specs=in_specs, out_specs=out_specs, out_shape=out_shape,
                          name=name, compiler_params=_params("parallel"))(*args)


def _qkv_bwd(p, dq, dk, dv, q_gain, k_gain, cs, sn, *, name, has_q, kv_col, kv_row_off, tm=256):
    n = p.shape[0]
    rope = cs is not None
    rb = kv_row_off // tm

    def body(*refs):
        it = iter(refs)
        q_ref = next(it) if has_q else None
        kv_ref = next(it)
        dq_ref = next(it) if has_q else None
        dk_ref, dv_ref = next(it), next(it)
        qg_ref, kg_ref = next(it), next(it)
        cs_ref = next(it) if rope else None
        sn_ref = next(it) if rope else None
        dp_ref, dqg_ref, dkg_ref = next(it), next(it), next(it)
        i = pl.program_id(0)

        def back(xh, dout, gain):
            if rope:
                dout = dout * cs_ref[...] + _partner(dout * sn_ref[...])
            r = lax.rsqrt(jnp.mean(xh * xh, axis=-1, keepdims=True) + EPS)
            xhat = xh * r
            dxh = dout * gain
            dx = r * (dxh - xhat * jnp.mean(dxh * xhat, axis=-1, keepdims=True))
            return dx, _colsum(dout * xhat)

        dqg = jnp.zeros((1, HD), F32)
        dkg = jnp.zeros((1, HD), F32)
        if has_q:
            for h in range(NQ):
                dx, dg = back(q_ref[:, h * HD:(h + 1) * HD], dq_ref[h], qg_ref[...])
                dp_ref[:, h * HD:(h + 1) * HD] = dx.astype(BF16)
                dqg = dqg + dg
        else:
            dp_ref[:, 0:AW] = jnp.zeros((tm, AW), BF16)
        for h in range(NKV):
            dx, dg = back(kv_ref[:, h * HD:(h + 1) * HD], dk_ref[h], kg_ref[...])
            dp_ref[:, AW + h * HD:AW + (h + 1) * HD] = dx.astype(BF16)
            dkg = dkg + dg
            dp_ref[:, AW + (NKV + h) * HD:AW + (NKV + h + 1) * HD] = dv_ref[h].astype(BF16)
        _acc_out(dqg_ref, i, dqg)
        _acc_out(dkg_ref, i, dkg)

    in_specs, args = [], []
    if has_q:
        in_specs.append(pl.BlockSpec((tm, AW), lambda i: (i, 0)))
        args.append(p)
    in_specs.append(pl.BlockSpec((tm, 2 * NKV * HD), lambda i: (i, kv_col)))
    args.append(p)
    if has_q:
        in_specs.append(pl.BlockSpec((NQ, tm, HD), lambda i: (0, i, 0)))
        args.append(dq)
    in_specs += [pl.BlockSpec((NKV, tm, HD), lambda i: (0, rb + i, 0))] * 2 + [_vec(HD), _vec(HD)]
    args += [dk, dv, q_gain, k_gain]
    if rope:
        in_specs += [pl.BlockSpec((tm, HD), lambda i: (i, 0))] * 2
        args += [cs, sn]
    return pl.pallas_call(
        body, grid=(n // tm,), in_specs=in_specs,
        out_specs=[pl.BlockSpec((tm, D), lambda i: (i, 0)), _vec(HD), _vec(HD)],
        out_shape=[jax.ShapeDtypeStruct((n, D), BF16), jax.ShapeDtypeStruct((1, HD), F32),
                   jax.ShapeDtypeStruct((1, HD), F32)],
        name=name, compiler_params=_params("arbitrary"))(*args)


def _conv_gate_fwd(p, o, conv_w, *, name, tm=256):
    n = p.shape[0]
    ni = n // tm

    def body(gb_ref, gc_ref, gcp_ref, gcn_ref, xi_ref, xip_ref, xin_ref, o_ref, w_ref, cat_ref):
        i = pl.program_id(0)
        hext = _ext(gcp_ref, gc_ref, gcn_ref, i, ni) * _ext(xip_ref, xi_ref, xin_ref, i, ni)
        cat_ref[:, 0:AW] = o_ref[...].astype(BF16)
        cat_ref[:, AW:D] = (gb_ref[...] * _conv3(hext, w_ref, tm)).astype(BF16)

    gcp, gcn = _halo_specs(tm, CW, n, colblk=3)
    xip, xin = _halo_specs(tm, CW, n, colblk=4)
    return pl.pallas_call(
        body, grid=(ni,),
        in_specs=[pl.BlockSpec((tm, CW), lambda i: (i, 2)), pl.BlockSpec((tm, CW), lambda i: (i, 3)), gcp, gcn,
                  pl.BlockSpec((tm, CW), lambda i: (i, 4)), xip, xin, pl.BlockSpec((tm, AW), lambda i: (i, 0)),
                  pl.BlockSpec((3, CW), lambda i: (0, 0))],
        out_specs=pl.BlockSpec((tm, D), lambda i: (i, 0)), out_shape=jax.ShapeDtypeStruct((n, D), BF16),
        name=name, compiler_params=_params("parallel"))(p, p, p, p, p, p, p, o, conv_w)


def _conv_gate_bwd(dcat, p, conv_w, *, name, tm=256):
    n = p.shape[0]
    ni = n // tm

    def body(dc_ref, dcp_ref, dcn_ref, gb_ref, gbp_ref, gbn_ref, gc_ref, gcp_ref, gcn_ref, xi_ref, xip_ref, xin_ref,
             w_ref, dp_ref, dw_ref):
        i = pl.program_id(0)
        gcext = _ext(gcp_ref, gc_ref, gcn_ref, i, ni)
        xiext = _ext(xip_ref, xi_ref, xin_ref, i, ni)
        hext = gcext * xiext
        dcv = _ext(dcp_ref, dc_ref, dcn_ref, i, ni) * _ext(gbp_ref, gb_ref, gbn_ref, i, ni)
        dp_ref[:, 0:CW] = (dc_ref[...] * _conv3(hext, w_ref, tm)).astype(BF16)
        dh = _sh(dcv, 1, tm) * w_ref[0:1, :] + _sh(dcv, 0, tm) * w_ref[1:2, :] + _sh(dcv, -1, tm) * w_ref[2:3, :]
        dp_ref[:, CW:2 * CW] = (dh * xi_ref[...]).astype(BF16)
        dp_ref[:, 2 * CW:3 * CW] = (dh * gc_ref[...]).astype(BF16)
        dcv_t = dcv[HALO:HALO + tm]
        dw = jnp.concatenate([_colsum(dcv_t * _sh(hext, -1, tm)), _colsum(dcv_t * _sh(hext, 0, tm)),
                              _colsum(dcv_t * _sh(hext, 1, tm))], axis=0)
        _acc_out(dw_ref, i, dw)

    def trio(colblk):
        prev, nxt = _halo_specs(tm, CW, n, colblk=colblk)
        return [pl.BlockSpec((tm, CW), lambda i: (i, colblk)), prev, nxt]

    return pl.pallas_call(
        body, grid=(ni,), in_specs=trio(1) + trio(2) + trio(3) + trio(4) + [pl.BlockSpec((3, CW), lambda i: (0, 0))],
        out_specs=[pl.BlockSpec((tm, 3 * CW), lambda i: (i, 0)), pl.BlockSpec((3, CW), lambda i: (0, 0))],
        out_shape=[jax.ShapeDtypeStruct((n, 3 * CW), BF16), jax.ShapeDtypeStruct((3, CW), F32)],
        name=name, compiler_params=_params("arbitrary"))(dcat, dcat, dcat, p, p, p, p, p, p, p, p, p, conv_w)


def _attn_fwd(q, k, v, *, name, bq=128):
    n = q.shape[1]
    t = k.shape[1]
    bq = min(bq, n)

    def body(q_ref, k_ref, v_ref, o_ref, lse_ref):
        q2 = q_ref[...].reshape(2 * bq, HD)
        s = lax.dot_general(q2, k_ref[0], _NT, preferred_element_type=F32)
        m = jnp.max(s, axis=-1, keepdims=True)
        pv = jnp.exp2(s - m)
        l = jnp.sum(pv, axis=-1, keepdims=True)
        out = jnp.dot(pv.astype(BF16), v_ref[0], preferred_element_type=F32) / l
        o_ref[:, 0:HD] = out[0:bq]
        o_ref[:, HD:2 * HD] = out[bq:2 * bq]
        lse_ref[...] = (m + jnp.log2(l)).reshape(2, bq, 1)

    kspec = pl.BlockSpec((1, t, HD), lambda h, i: (h, 0, 0))
    return pl.pallas_call(
        body, grid=(NKV, n // bq),
        in_specs=[pl.BlockSpec((2, bq, HD), lambda h, i: (h, i, 0)), kspec, kspec],
        out_specs=[pl.BlockSpec((bq, 2 * HD), lambda h, i: (i, h)), pl.BlockSpec((2, bq, 1), lambda h, i: (h, i, 0))],
        out_shape=[jax.ShapeDtypeStruct((n, AW), F32), jax.ShapeDtypeStruct((NQ, n, 1), F32)],
        name=name, compiler_params=_params("parallel", "parallel"))(q, k, v)


def _attn_bwd(q, k, v, dcat, o, lse, *, name, bq=256):
    n = q.shape[1]
    t = k.shape[1]
    bq = min(bq, n)

    def body(q_ref, k_ref, v_ref, dc_ref, o_ref, lse_ref, dq_ref, dk_ref, dv_ref):
        qi = pl.program_id(1)
        q2 = q_ref[...].reshape(2 * bq, HD)
        do_f = jnp.concatenate([dc_ref[:, 0:HD], dc_ref[:, HD:2 * HD]], axis=0)
        o_f = jnp.concatenate([o_ref[:, 0:HD], o_ref[:, HD:2 * HD]], axis=0)
        delta = jnp.sum(do_f * o_f, axis=-1, keepdims=True)
        do2 = do_f.astype(BF16)
        s = lax.dot_general(q2, k_ref[0], _NT, preferred_element_type=F32)
        pv = jnp.exp2(s - lse_ref[...].reshape(2 * bq, 1))
        dp = lax.dot_general(do2, v_ref[0], _NT, preferred_element_type=F32)
        ds = (pv * (dp - delta)).astype(BF16)
        dq_ref[...] = (jnp.dot(ds, k_ref[0], preferred_element_type=F32) * _SCALE).reshape(2, bq, HD)
        dk_part = lax.dot_general(ds, q2, _TN, preferred_element_type=F32) * _LN2
        dv_part = lax.dot_general(pv.astype(BF16), do2, _TN, preferred_element_type=F32)

        @pl.when(qi == 0)
        def _():
            dk_ref[0] = dk_part
            dv_ref[0] = dv_part

        @pl.when(qi > 0)
        def _():
            dk_ref[0] += dk_part
            dv_ref[0] += dv_part

    qspec = pl.BlockSpec((2, bq, HD), lambda h, i: (h, i, 0))
    kspec = pl.BlockSpec((1, t, HD), lambda h, i: (h, 0, 0))
    sspec = pl.BlockSpec((2, bq, 1), lambda h, i: (h, i, 0))
    cspec = pl.BlockSpec((bq, 2 * HD), lambda h, i: (i, h))
    return pl.pallas_call(
        body, grid=(NKV, n // bq), in_specs=[qspec, kspec, kspec, cspec, cspec, sspec], out_specs=[qspec, kspec, kspec],
        out_shape=[jax.ShapeDtypeStruct((NQ, n, HD), F32), jax.ShapeDtypeStruct((NKV, t, HD), F32),
                   jax.ShapeDtypeStruct((NKV, t, HD), F32)],
        name=name, compiler_params=_params("parallel", "arbitrary"))(q, k, v, dcat, o, lse)


def _window_sums(ext, w):
    s, step = ext, 1
    while step < w:
        s = s + _roll_rows(s, step)
        step *= 2
    return s


def _pool_counts(i, tm, n, w, rows, first):
    t = i * tm - HALO + first + lax.broadcasted_iota(jnp.int32, (rows, 1), 0)
    lo = jnp.clip(t - w // 2, 0, n)
    hi = jnp.clip(t + w - w // 2, 0, n)
    return jnp.maximum(hi - lo, 1).astype(F32)


def _norm_mod_ext(xext, gain_ref, sc_ref, sh_ref, i, tm, n):
    rows = xext.shape[0]
    t = i * tm - HALO + lax.broadcasted_iota(jnp.int32, (rows, 1), 0)
    inside = (t >= 0) & (t < n)
    r = lax.rsqrt(jnp.mean(xext * xext, axis=-1, keepdims=True) + EPS)
    xh = xext * r
    a = (xh * gain_ref[...]) * (1.0 + sc_ref[...]) + sh_ref[...]
    return jnp.where(inside, a, 0.0), r, xh


def _pool_fwd(x, y, g, gain, sc, sh, pool_w, *, name, tm=256):
    n, d = x.shape
    ni = n // tm

    def body(x_ref, xp_ref, xn_ref, y_ref, yp_ref, yn_ref, g_ref, gain_ref, sc_ref, sh_ref, w_ref, xo_ref, o_ref):
        i = pl.program_id(0)
        xext = _ext(xp_ref, x_ref, xn_ref, i, ni) + g_ref[...] * _ext(yp_ref, y_ref, yn_ref, i, ni)
        xo_ref[...] = xext[HALO:HALO + tm]
        aext, _, _ = _norm_mod_ext(xext, gain_ref, sc_ref, sh_ref, i, tm, n)
        for gi, w in enumerate(POOL_WINDOWS):
            ag = aext[:, gi * PG:(gi + 1) * PG]
            mean = _sh(_window_sums(ag, w), -(w // 2), tm) / _pool_counts(i, tm, n, w, tm, HALO)
            pooled = mean - ag[HALO:HALO + tm]
            o_ref[:, gi * PG:(gi + 1) * PG] = jnp.dot(pooled.astype(BF16), w_ref[gi], preferred_element_type=F32)

    row = pl.BlockSpec((tm, d), lambda i: (i, 0))
    prev, nxt = _halo_specs(tm, d, n)
    return pl.pallas_call(
        body, grid=(ni,),
        in_specs=[row, prev, nxt, row, prev, nxt, _vec(d), _vec(d), _vec(d), _vec(d),
                  pl.BlockSpec((4, PG, PG), lambda i: (0, 0, 0))],
        out_specs=[row, row], out_shape=[jax.ShapeDtypeStruct((n, d), F32)] * 2,
        name=name, compiler_params=_params("parallel"))(x, x, x, y, y, y, g, gain, sc, sh, pool_w)


def _pool_bwd(dxo, mixed, x, g, scale, gain, sc, sh, pool_w, zprev, gprev, *, name, tm=256):
    n, d = x.shape
    ni = n // tm

    def body(dx_ref, dxp_ref, dxn_ref, mx_ref, x_ref, xp_ref, xn_ref, g_ref, s_ref, gain_ref, sc_ref, sh_ref, w_ref,
             zp_ref, gp_ref, dxi_ref, dw_ref, dg_ref, dsl_ref, dsh_ref, dsc_ref, dgn_ref, dzp_ref, dgp_ref):
        i = pl.program_id(0)
        dxo_t = dx_ref[...]
        mixed_t = mx_ref[...]
        dy_t = dxo_t * g_ref[...]
        _acc_out(dg_ref, i, _colsum(dxo_t * (mixed_t * s_ref[...])))
        _acc_out(dsl_ref, i, _colsum(dy_t * mixed_t))
        dmixed = (_ext(dxp_ref, dx_ref, dxn_ref, i, ni) * g_ref[...]) * s_ref[...]
        xext = _ext(xp_ref, x_ref, xn_ref, i, ni)
        aext, rext, xhext = _norm_mod_ext(xext, gain_ref, sc_ref, sh_ref, i, tm, n)
        rows = tm + 2 * HALO
        da_parts = []
        for gi, w in enumerate(POOL_WINDOWS):
            sl = slice(gi * PG, (gi + 1) * PG)
            ag = aext[:, sl]
            mean = _sh(_window_sums(ag, w), -(w // 2), tm) / _pool_counts(i, tm, n, w, tm, HALO)
            pooled = (mean - ag[HALO:HALO + tm]).astype(BF16)
            dmg = dmixed[:, sl].astype(BF16)
            dwg = lax.dot_general(pooled, dmixed[HALO:HALO + tm, sl].astype(BF16), _TN, preferred_element_type=F32)

            @pl.when(i == 0)
            def _(dwg=dwg, gi=gi):
                dw_ref[gi] = dwg

            @pl.when(i > 0)
            def _(dwg=dwg, gi=gi):
                dw_ref[gi] += dwg

            dpl = lax.dot_general(dmg, w_ref[gi], _NT, preferred_element_type=F32)
            e = dpl / _pool_counts(i, tm, n, w, rows, 0)
            da_parts.append(_sh(_window_sums(e, w), 1 - w // 2, tm) - dpl[HALO:HALO + tm])
        da = jnp.concatenate(da_parts, axis=1)
        r = rext[HALO:HALO + tm]
        xh = xhext[HALO:HALO + tm]
        nrm = xh * gain_ref[...]
        dn = da * (1.0 + sc_ref[...])
        dxh = dn * gain_ref[...]
        dxi = dxo_t + r * (dxh - xh * jnp.mean(dxh * xh, axis=-1, keepdims=True))
        dxi_ref[...] = dxi
        _acc_out(dsh_ref, i, _colsum(da))
        _acc_out(dsc_ref, i, _colsum(da * nrm))
        _acc_out(dgn_ref, i, _colsum(dn * xh))
        dzp_ref[...] = (dxi * gp_ref[...]).astype(BF16)
        _acc_out(dgp_ref, i, _colsum(dxi * zp_ref[...]))

    row = pl.BlockSpec((tm, d), lambda i: (i, 0))
    prev, nxt = _halo_specs(tm, d, n)
    wspec = pl.BlockSpec((4, PG, PG), lambda i: (0, 0, 0))
    vshape = jax.ShapeDtypeStruct((1, d), F32)
    return pl.pallas_call(
        body, grid=(ni,),
        in_specs=[row, prev, nxt, row, row, prev, nxt] + [_vec(d)] * 5 + [wspec, row, _vec(d)],
        out_specs=[row, wspec] + [_vec(d)] * 5 + [row, _vec(d)],
        out_shape=[jax.ShapeDtypeStruct((n, d), F32), jax.ShapeDtypeStruct((4, PG, PG), F32)] + [vshape] * 5
        + [jax.ShapeDtypeStruct((n, d), BF16), vshape],
        name=name, compiler_params=_params("arbitrary"))(dxo, dxo, dxo, mixed, x, x, x, g, scale, gain, sc, sh, pool_w,
                                                         zprev, gprev)


def _adamw(gparts_list, w, m, v, *, name, silu_grad_of=None):
    nl = len(gparts_list)
    nparts, r, c = gparts_list[0].shape
    tr = _pick(r, (256, 128, 64, 32, 16, 8))
    has_c = silu_grad_of is not None

    def body(*refs):
        gp_refs = refs[:nl]
        it = iter(refs[nl:])
        w_ref, m_ref, v_ref = next(it), next(it), next(it)
        c_ref = next(it) if has_c else None
        g_ref, d_ref, mo_ref, vo_ref = next(it), next(it), next(it), next(it)
        layer = pl.program_id(0)

        def update(gp_ref):
            g = gp_ref[0].astype(F32)
            for p in range(1, nparts):
                g = g + gp_ref[p].astype(F32)
            if has_c:
                cv = c_ref[0]
                sg = _sigmoid(cv)
                g = g * (sg * (1.0 + cv * (1.0 - sg)))
            g_ref[0] = g
            mn = ADAM_B1 * m_ref[0] + (1.0 - ADAM_B1) * g
            vn = ADAM_B2 * v_ref[0] + (1.0 - ADAM_B2) * (g * g)
            m_hat = mn / (1.0 - ADAM_B1 ** ADAM_STEP)
            v_hat = vn / (1.0 - ADAM_B2 ** ADAM_STEP)
            d_ref[0] = -ADAM_LR * (m_hat / (jnp.sqrt(v_hat) + ADAM_EPS) + ADAM_WD * w_ref[0])
            mo_ref[0] = mn
            vo_ref[0] = vn

        if nl == 1:
            update(gp_refs[0])
        else:
            for li in range(nl):
                pl.when(layer == li)(functools.partial(update, gp_refs[li]))

    row = pl.BlockSpec((1, tr, c), lambda l, i: (l, i, 0))
    in_specs = [pl.BlockSpec((nparts, tr, c), lambda l, i, li=li: (0, jnp.where(l == li, i, 0), 0)) for li in range(nl)]
    in_specs += [row, row, row]
    args = list(gparts_list) + [w, m, v]
    if has_c:
        in_specs.append(row)
        args.append(silu_grad_of)
    return pl.pallas_call(
        body, grid=(nl, r // tr), in_specs=in_specs, out_specs=[row] * 4,
        out_shape=[jax.ShapeDtypeStruct((nl, r, c), F32)] * 4, name=name,
        compiler_params=_params("arbitrary", "arbitrary"))(*args)


def _adamw_nd(gparts, w, m, v, *, name, silu_grad_of=None):
    shape = w.shape
    c = shape[-1]
    if isinstance(gparts, (list, tuple)):
        nl = len(gparts)
        r = math.prod(shape[1:-1])
    else:
        nl = 1
        r = math.prod(shape[:-1]) if len(shape) > 1 else 1
        gparts = [gparts]
    rs = lambda a: a.reshape(nl, r, c)
    res = _adamw([gp.reshape(gp.shape[0], r, c) for gp in gparts], rs(w), rs(m), rs(v), name=name,
                 silu_grad_of=None if silu_grad_of is None else rs(silu_grad_of))
    return [a.reshape(shape) for a in res]


def _place():
    return lax.axis_index("x"), lax.axis_index("y"), lax.axis_index("c")


def _all_gather(arrs, *, name):
    k_arr = len(arrs)

    def body(*refs):
        ins = refs[:k_arr]
        outs = refs[k_arr:2 * k_arr]
        send_sems, recv_sems, local_sems = refs[2 * k_arr:]
        x, y, c = _place()
        me, sibling = (x, y, c), (x, y, 1 - c)
        chips = [(1 - x, y), (x, 1 - y), (1 - x, 1 - y)]

        def slot(a, px, py, pc):
            return outs[a].at[4 * px + 2 * py + pc]

        def copy(a, s, block, to, src=None):
            return pltpu.make_async_remote_copy(
                src_ref=slot(a, *block) if src is None else src, dst_ref=slot(a, *block),
                send_sem=send_sems.at[a, s], recv_sem=recv_sems.at[a, s], device_id=to, device_id_type=MESH)

        mine = [pltpu.make_async_copy(ins[a], slot(a, *me), local_sems.at[a]) for a in range(k_arr)]
        for cp in mine:
            cp.start()
        first = []
        for a in range(k_arr):
            first.append(copy(a, 0, me, sibling, src=ins[a]))
            first += [copy(a, 1 + j, me, (*chip, c), src=ins[a]) for j, chip in enumerate(chips)]
        for cp in first:
            cp.start()
        passed = []
        for j, chip in enumerate(chips):
            for a in range(k_arr):
                copy(a, 1 + j, (*chip, c), me).wait_recv()
                fw = copy(a, 4 + j, (*chip, c), sibling)
                fw.start()
                passed.append(fw)
        for a in range(k_arr):
            copy(a, 0, sibling, me).wait_recv()
            for j, chip in enumerate(chips):
                copy(a, 4 + j, (*chip, 1 - c), me).wait_recv()
        for cp in first + passed:
            cp.wait_send()
        for cp in mine:
            cp.wait()

    any_spec = pl.BlockSpec(memory_space=pl.ANY)
    return pl.pallas_call(
        body, in_specs=[any_spec] * k_arr, out_specs=[any_spec] * k_arr,
        out_shape=[jax.ShapeDtypeStruct((NDEV,) + a.shape, a.dtype) for a in arrs],
        scratch_shapes=[pltpu.SemaphoreType.DMA((k_arr, 7)), pltpu.SemaphoreType.DMA((k_arr, 7)),
                        pltpu.SemaphoreType.DMA((k_arr,))],
        name=name)(*arrs)


_HBM = pl.BlockSpec(memory_space=pltpu.HBM)
_SEM = pl.BlockSpec(memory_space=pltpu.SEMAPHORE)
_EFFECT = pltpu.SideEffectType.DATAFLOW_SIDE_EFFECTING


def _peers(x, y, c):
    return [(x ^ (rel >> 2), y ^ ((rel >> 1) & 1), c ^ (rel & 1)) for rel in range(1, NDEV)]


def _exchange_copies(srcs, lands, send_sems, recv_sems, scatter):
    x, y, c = _place()
    me = 4 * x + 2 * y + c
    copies = []
    for r, (px, py, pc) in enumerate(_peers(x, y, c)):
        peer = 4 * px + 2 * py + pc
        for a in range(len(srcs)):
            copies.append(pltpu.make_async_remote_copy(
                src_ref=srcs[a].at[peer] if scatter else srcs[a], dst_ref=lands[a].at[me],
                send_sem=send_sems.at[7 * a + r], recv_sem=recv_sems.at[7 * a + r], device_id=(px, py, pc),
                device_id_type=MESH))
    return copies


def _exchange_start(arrs, *, scatter, name):
    k_arr = len(arrs)
    land_shapes = [a.shape if scatter else (NDEV,) + a.shape for a in arrs]
    lands = [pltpu.with_memory_space_constraint(lax.empty(s, a.dtype), pltpu.HBM) for s, a in zip(land_shapes, arrs)]
    srcs = [pltpu.with_memory_space_constraint(a, pltpu.HBM) for a in arrs]

    def body(*refs):
        src_refs, land_refs = refs[:k_arr], refs[k_arr:2 * k_arr]
        send_sems, recv_sems = refs[2 * k_arr], refs[2 * k_arr + 1]
        token = refs[-1]
        for cp in _exchange_copies(src_refs, land_refs, send_sems, recv_sems, scatter):
            cp.start()
        token[...] = jnp.zeros_like(token)

    out_shape = ([pltpu.SemaphoreType.DMA((7 * k_arr,)), pltpu.SemaphoreType.DMA((7 * k_arr,))]
                 + [pltpu.HBM(a.shape, a.dtype) for a in arrs] + [pltpu.HBM(s, a.dtype) for s, a in zip(land_shapes, arrs)]
                 + [jax.ShapeDtypeStruct((8, 128), F32)])
    res = pl.pallas_call(
        body, name=name, out_shape=out_shape, in_specs=[_HBM] * (2 * k_arr),
        out_specs=[_SEM, _SEM] + [_HBM] * (2 * k_arr) + [pl.BlockSpec(memory_space=pltpu.VMEM)],
        input_output_aliases={i: 2 + i for i in range(2 * k_arr)},
        compiler_params=pltpu.CompilerParams(has_side_effects=_EFFECT))(*srcs, *lands)
    return dict(send=res[0], recv=res[1], srcs=list(res[2:2 + k_arr]), lands=list(res[2 + k_arr:2 + 2 * k_arr]),
                token=res[-1], scatter=scatter)


def _exchange_wait(handle, after, *, name):
    k_arr = len(handle["srcs"])
    scatter = handle["scatter"]

    def body(*refs):
        src_refs, land_refs = refs[:k_arr], refs[k_arr:2 * k_arr]
        send_sems, recv_sems = refs[2 * k_arr], refs[2 * k_arr + 1]
        x, y, c = _place()
        me = 4 * x + 2 * y + c
        for r, (px, py, pc) in enumerate(_peers(x, y, c)):
            peer = 4 * px + 2 * py + pc
            for a in range(k_arr):
                cp = pltpu.make_async_remote_copy(
                    src_ref=src_refs[a].at[peer] if scatter else src_refs[a], dst_ref=land_refs[a].at[peer],
                    send_sem=send_sems.at[7 * a + r], recv_sem=recv_sems.at[7 * a + r], device_id=(x, y, c),
                    device_id_type=MESH)
                cp.wait_send()
                cp.wait_recv()

    arrs = handle["srcs"] + handle["lands"]
    res = pl.pallas_call(
        body, name=name, out_shape=[pltpu.HBM(a.shape, a.dtype) for a in arrs],
        in_specs=[_HBM] * (2 * k_arr) + [_SEM, _SEM, pl.BlockSpec(memory_space=pl.ANY)],
        out_specs=[_HBM] * (2 * k_arr), input_output_aliases={i: i for i in range(2 * k_arr)},
        compiler_params=pltpu.CompilerParams(has_side_effects=_EFFECT))(*arrs, handle["send"], handle["recv"], after)
    me = 4 * lax.axis_index("x") + 2 * lax.axis_index("y") + lax.axis_index("c")
    out = []
    for src, land in zip(res[:k_arr], res[k_arr:]):
        own = lax.dynamic_index_in_dim(src, me, 0, keepdims=False) if scatter else src
        out.append(lax.dynamic_update_index_in_dim(land, own, me, 0))
    return out


def _ffn_fwd(x_in, y, g, ymul, gain, sc, sh, w_up, cw, cb, w_down, tag):
    xr, f = _norm_mod(x_in, gain, sc, sh, y=y, g=g, ymul=ymul, name=f"ffn_norm_{tag}")
    u = _mm_w(f, w_up, name=f"ffn_up_{tag}")
    hmid = _glu_fwd(u, cw, cb, name=f"ffn_glu_{tag}", tm=1024)
    z = _mm_w(hmid, w_down, name=f"ffn_down_{tag}")
    return xr, f, u, hmid, z


def _ffn_bwd(dxo, dz, xr, f, u, hmid, gain, sc, w_up, cw, cb, w_down, tag, gate_y=None, gate_g=None):
    dh = _mm_w(dz, w_down, tb=True, name=f"ffn_down_dx_{tag}")
    d_wdown = _mm_tn((hmid, dz), name=f"ffn_down_dw_{tag}")
    dug, duv, dcw, dcb = _glu_bwd(dh, u, cw, cb, name=f"ffn_glu_bwd_{tag}", tm=1024)
    df = _mm_w([dug, duv], w_up, tb=True, name=f"ffn_up_dx_{tag}")
    d_wup_g = _mm_tn((f, dug), name=f"ffn_up_dwg_{tag}")
    d_wup_v = _mm_tn((f, duv), name=f"ffn_up_dwv_{tag}")
    norm_res = _norm_mod_bwd(df, xr, gain, sc, dres=dxo, gate_y=gate_y, gate_g=gate_g, name=f"ffn_norm_bwd_{tag}")
    return norm_res, (d_wup_g, d_wup_v, d_wdown, dcw, dcb)


def _split6(mod):
    return [mod[j * D:(j + 1) * D][None, :] for j in range(6)]


def _row(v):
    return v.reshape(1, -1)


def kernel(x, c, ctx, c_ctx, ada_w, ada_b, mix_norm, ffn_norm, even_w_in, even_q_gain, even_k_gain, even_conv_w, even_w_out, odd_pool_w, odd_pool_scale, ffn_w_up, ffn_conv_w, ffn_conv_b, ffn_w_down, loss_target, m_c_ctx, m_ada_w, m_ada_b, m_mix_norm, m_ffn_norm, m_even_w_in, m_even_q_gain, m_even_k_gain, m_even_conv_w, m_even_w_out, m_odd_pool_w, m_odd_pool_scale, m_ffn_w_up, m_ffn_conv_w, m_ffn_conv_b, m_ffn_w_down, v_c_ctx, v_ada_w, v_ada_b, v_mix_norm, v_ffn_norm, v_even_w_in, v_even_q_gain, v_even_k_gain, v_even_conv_w, v_even_w_out, v_odd_pool_w, v_odd_pool_scale, v_ffn_w_up, v_ffn_conv_w, v_ffn_conv_b, v_ffn_w_down):
    n = x.shape[1]
    lc = ctx.shape[1]
    me = 4 * lax.axis_index("x") + 2 * lax.axis_index("y") + lax.axis_index("c")
    xs, ctxs, tgt = x[0], ctx[0], loss_target[0]
    acols = ada_w.shape[2]

    small = jnp.concatenate([even_conv_w.reshape(-1), ffn_conv_w.reshape(-1), odd_pool_scale.reshape(-1)])
    nsmall = small.shape[0]
    small = jnp.pad(small, (0, (-nsmall) % 1024)).reshape(-1, 128)
    c_rows = jnp.pad(c, ((0, 7), (0, 0)))
    g_c, g_win, g_small = _all_gather([c_rows, even_w_in[0].astype(BF16), small], name="gather_first")
    w_in = g_win.transpose(1, 0, 2).reshape(D, -1)
    g_small = g_small.reshape(NDEV, -1)
    ecw = even_conv_w.shape[2]
    fcw = ffn_conv_w.shape[2]
    conv_w = g_small[:, :3 * ecw].reshape(NDEV, 3, ecw).transpose(1, 0, 2).reshape(3, CW)
    o1 = 3 * ecw
    fconv_w = g_small[:, o1:o1 + 6 * fcw].reshape(NDEV, 2, 3, fcw).transpose(1, 2, 0, 3).reshape(2, 3, DFF)
    o2 = o1 + 6 * fcw
    pool_scale = g_small[:, o2:o2 + D // NDEV].reshape(1, D)

    mraw = jnp.concatenate([g_c[:, 0, :], c_ctx[None, :], jnp.zeros((7, D), F32)], axis=0)
    my_bias = lax.dynamic_slice_in_dim(ada_b, me * acols, acols, axis=1)
    modp = jnp.stack([_mm(mraw, ada_w[l], silu_a=True, bias=my_bias[l:l + 1], name=f"ada_proj_{l}", tm=16, tn=256)
                      for l in range(2)])
    (g_mod,) = _all_gather([modp], name="gather_mod")
    mod_rows = g_mod.transpose(1, 2, 0, 3).reshape(2, 16, 6 * D)
    late_shards = [even_w_out[0].astype(BF16), odd_pool_w[0].astype(BF16), ffn_w_up.astype(BF16),
                   ffn_w_down.astype(BF16)]
    late_shards, mod_rows = lax.optimization_barrier((late_shards, mod_rows))
    h_weights = _exchange_start(late_shards, scatter=False, name="weights_start")
    mod_rows = mod_rows + h_weights["token"][0, 0]
    mod = lax.dynamic_index_in_dim(mod_rows, me, axis=1, keepdims=False)
    sh1, sc1, g1, sh2, sc2, g2 = _split6(mod[0])
    sh1b, sc1b, g1b, sh2b, sc2b, g2b = _split6(mod[1])
    csh1, csc1 = _split6(mod_rows[0, 8])[:2]
    mixn = [_row(mix_norm[l]) for l in range(2)]
    ffnn = [_row(ffn_norm[l]) for l in range(2)]
    qg, kg = _row(even_q_gain[0]), _row(even_k_gain[0])
    fcb = [_row(ffn_conv_b[l]) for l in range(2)]

    cs_t, sn_t = _rope_tables(n)
    a_lat = _norm_mod(xs, mixn[0], sc1, sh1, name="mix0_norm")
    a_ctx = _norm_mod(ctxs, mixn[0], csc1, csh1, name="mix0_norm_ctx")
    p_lat = _mm_w(a_lat, w_in, name="in_proj")
    p_ctx = _mm(a_ctx, w_in[:, AW:AW + 4 * HD], name="in_proj_ctx", tm=256, tn=512, tk=1024)
    q_r, k_lat, v_lat = _qkv_prep(p_lat, qg, kg, cs_t, sn_t, has_q=True, kv_col=1, name="qkv_prep")
    k_ctx, v_ctx = _qkv_prep(p_ctx, qg, kg, None, None, has_q=False, kv_col=0, name="qkv_prep_ctx")
    k_all = jnp.concatenate([k_ctx, k_lat], axis=1)
    v_all = jnp.concatenate([v_ctx, v_lat], axis=1)
    o_attn, lse = _attn_fwd(q_r, k_all, v_all, name="attn_fwd")
    cat = _conv_gate_fwd(p_lat, o_attn, conv_w, name="conv_gate")
    g_wout, g_pool, g_up, g_down = _exchange_wait(h_weights, cat, name="weights_wait")
    w_out = g_wout.reshape(D, D)
    pool_w = g_pool.transpose(1, 0, 2, 3).reshape(4, PG, PG)
    w_up = [g_up[:, l].transpose(1, 0, 2).reshape(D, 2 * DFF) for l in range(2)]
    w_down = [g_down[:, l].reshape(DFF, D) for l in range(2)]
    y0 = _mm_w(cat, w_out, name="out_proj", tm=512)
    x1, f0, u0, h0, z0 = _ffn_fwd(xs, y0, g1, None, ffnn[0], sc2, sh2, w_up[0], fconv_w[0], fcb[0], w_down[0], "l0")

    x2, mixed = _pool_fwd(x1, z0, g2, mixn[1], sc1b, sh1b, pool_w, name="pool_fwd")
    x3, f1, u1, h1, z1 = _ffn_fwd(x2, mixed, g1b, pool_scale, ffnn[1], sc2b, sh2b, w_up[1], fconv_w[1], fcb[1],
                                  w_down[1], "l1")
    dx4, loss_part, dz1, dg2b = _loss_head(x3, z1, g2b, tgt, name="loss_head")
    loss = lax.psum(loss_part[0, 0], ("x", "y", "c"))

    (dx3, dsh2b, dsc2b, dffn1), (dup1g, dup1v, ddown1, dfcw1, dfcb1) = _ffn_bwd(
        dx4, dz1, x3, f1, u1, h1, ffnn[1], sc2b, w_up[1], fconv_w[1], fcb[1], w_down[1], "l1")
    dx2, dpool_w, dg1b, dpscale, dsh1b, dsc1b, dmix1, dz0, dg2 = _pool_bwd(
        dx3, mixed, x2, g1b, pool_scale, mixn[1], sc1b, sh1b, pool_w, z0, g2, name="pool_bwd")

    def up_shards(dg, dv):
        return jnp.concatenate([dg, dv], axis=1).reshape(D, NDEV, -1).transpose(1, 0, 2)

    s_pool = dpool_w.astype(BF16).reshape(4, NDEV, PG // NDEV, PG).transpose(1, 0, 2, 3)
    h_g1 = _exchange_start([s_pool, up_shards(dup1g, dup1v), ddown1.reshape(NDEV, DFF // NDEV, D)], scatter=True,
                           name="grads1_start")

    (dx1, dsh2, dsc2, dffn0, dy0, dg1), (dup0g, dup0v, ddown0, dfcw0, dfcb0) = _ffn_bwd(
        dx2, dz0, x1, f0, u0, h0, ffnn[0], sc2, w_up[0], fconv_w[0] + h_g1["token"][0, 0], fcb[0], w_down[0], "l0",
        gate_y=y0, gate_g=g1)
    h_g0 = _exchange_start([up_shards(dup0g, dup0v), ddown0.reshape(NDEV, DFF // NDEV, D)], scatter=True,
                           name="grads0_start")
    dcat = _mm_w(dy0, w_out, tb=True, name="out_proj_dx", tm=512)
    d_wout = _mm_tn((cat, dy0), name="out_proj_dw")
    dp_conv, dconv_w = _conv_gate_bwd(dcat, p_lat, conv_w + h_g0["token"][0, 0], name="conv_gate_bwd")
    dq_r, dk_all, dv_all = _attn_bwd(q_r, k_all, v_all, dcat, o_attn, lse, name="attn_bwd")
    dp_qkv, dqg_l, dkg_l = _qkv_bwd(p_lat, dq_r, dk_all, dv_all, qg, kg, cs_t, sn_t, has_q=True, kv_col=1,
                                    kv_row_off=lc, name="qkv_bwd")
    dp_ctx, _zero_qg, dkg_c = _qkv_bwd(p_ctx, None, dk_all, dv_all, qg, kg, None, None, has_q=False, kv_col=0,
                                       kv_row_off=0, name="qkv_bwd_ctx")
    da_lat = _mm_w([dp_qkv, dp_conv], w_in, tb=True, name="in_proj_dx", tm=512)
    da_ctx = _mm(dp_ctx, w_in[:, :D], tb=True, name="in_proj_dx_ctx", tm=256, tn=512, tk=1024)
    d_win_qkv = _mm_tn([(a_lat, dp_qkv), (a_ctx, dp_ctx)], name="in_proj_dw_qkv")
    d_win_conv = _mm_tn((a_lat, dp_conv), name="in_proj_dw_conv")
    d_win = jnp.concatenate([d_win_qkv, d_win_conv], axis=1)
    h_ga = _exchange_start([d_win.reshape(D, NDEV, -1).transpose(1, 0, 2), d_wout.reshape(NDEV, D // NDEV, D)],
                           scatter=True, name="grads_attn_start")
    mixn0_late = mixn[0] + h_ga["token"][0, 0]
    grad_x, dsh1, dsc1, dmix0 = _norm_mod_bwd(da_lat, xs, mixn0_late, sc1, dres=dx1, name="mix0_norm_bwd")
    _dctx, dcsh1, dcsc1, dmix0c = _norm_mod_bwd(da_ctx, ctxs, mixn0_late, csc1, name="mix0_norm_bwd_ctx")

    z1k = jnp.zeros((1, D), F32)
    pack = jnp.concatenate(
        [v.reshape(-1) for v in (dsh1, dsc1, dg1, dsh2, dsc2, dg2, dsh1b, dsc1b, dg1b, dsh2b, dsc2b, dg2b,
                                 dcsh1, dcsc1, z1k, z1k, z1k, z1k,
                                 dmix0, dmix1, dmix0c, z1k, dffn0, dffn1, dqg_l, dkg_l + dkg_c,
                                 dfcb0, dfcb1, dconv_w, dfcw0, dfcw1, dpscale)])
    npack = pack.shape[0]
    pack = jnp.pad(pack, (0, (-npack) % 1024)).reshape(-1, 128)
    (g_pack,) = _all_gather([pack], name="gather_small_grads")
    gp = g_pack.reshape(NDEV, -1)
    off = [0]

    def take(size):
        seg = gp[:, off[0]:off[0] + size]
        off[0] += size
        return seg

    dmod_all = take(12 * D).reshape(NDEV, 2, 6 * D)
    dmodc_all = take(6 * D).reshape(NDEV, 1, 6 * D)
    dmix_all = take(4 * D).reshape(NDEV, 2, 2, D)
    dffn_all = take(2 * D).reshape(NDEV, 2, D)
    dqg_all = take(HD).reshape(NDEV, 1, HD)
    dkg_all = take(HD).reshape(NDEV, 1, HD)
    dfcb_all = take(2 * DFF).reshape(NDEV, 2, DFF)
    dconvw_all = take(3 * CW).reshape(NDEV, 3, CW)
    dfcw_all = take(6 * DFF).reshape(NDEV, 2, 3, DFF)
    dpscale_all = take(D).reshape(NDEV, D)

    outs = {}

    def put(nm, res):
        outs["grad_" + nm], outs["delta_" + nm], outs["new_m_" + nm], outs["new_v_" + nm] = res

    dmodc_pad = jnp.concatenate([dmodc_all, jnp.zeros_like(dmodc_all)], axis=1)
    put("ada_b", _adamw_nd(jnp.concatenate([dmod_all, dmodc_pad], axis=0), ada_b, m_ada_b, v_ada_b, name="adam_ada_b"))
    put("mix_norm", _adamw_nd(jnp.concatenate([dmix_all[:, 0], dmix_all[:, 1]], axis=0), mix_norm, m_mix_norm,
                              v_mix_norm, name="adam_mix_norm"))
    put("ffn_norm", _adamw_nd(dffn_all, ffn_norm, m_ffn_norm, v_ffn_norm, name="adam_ffn_norm"))
    put("even_q_gain", _adamw_nd(dqg_all, even_q_gain, m_even_q_gain, v_even_q_gain, name="adam_q_gain"))
    put("even_k_gain", _adamw_nd(dkg_all, even_k_gain, m_even_k_gain, v_even_k_gain, name="adam_k_gain"))
    put("ffn_conv_b", _adamw_nd(dfcb_all, ffn_conv_b, m_ffn_conv_b, v_ffn_conv_b, name="adam_ffn_conv_b"))
    my_convw = lax.dynamic_slice_in_dim(dconvw_all, me * ecw, ecw, axis=2)[:, None]
    put("even_conv_w", _adamw_nd(my_convw, even_conv_w, m_even_conv_w, v_even_conv_w, name="adam_even_conv_w"))
    my_fcw = lax.dynamic_slice_in_dim(dfcw_all, me * fcw, fcw, axis=3)
    put("ffn_conv_w", _adamw_nd(my_fcw, ffn_conv_w, m_ffn_conv_w, v_ffn_conv_w, name="adam_ffn_conv_w"))
    my_ps = lax.dynamic_slice_in_dim(dpscale_all, me * (D // NDEV), D // NDEV, axis=1)[:, None]
    put("odd_pool_scale", _adamw_nd(my_ps, odd_pool_scale, m_odd_pool_scale, v_odd_pool_scale, name="adam_pool_scale"))

    dmodc_sum = dmodc_all[0]
    for dev in range(1, NDEV):
        dmodc_sum = dmodc_sum + dmodc_all[dev]
    my_cols = lambda a: lax.dynamic_slice_in_dim(a, me * acols, acols, axis=a.ndim - 1)
    rows0 = jnp.concatenate([my_cols(dmod_all[:, 0]), my_cols(dmodc_sum), jnp.zeros((7, acols), F32)], axis=0)
    rows1 = jnp.concatenate([my_cols(dmod_all[:, 1]), jnp.zeros((8, acols), F32)], axis=0)
    d_ada = jnp.stack([_mm(mraw, rows, ta=True, silu_a=True, name=f"ada_dw_{l}", tm=512, tn=256, tk=16)
                       for l, rows in enumerate((rows0, rows1))])
    put("ada_w", _adamw_nd(d_ada[None], ada_w, m_ada_w, v_ada_w, name="adam_ada_w"))
    dscc_part = _mm(rows0, ada_w[0], tb=True, name="ada_dcctx", tm=16, tn=512, tk=256)
    (g_dscc,) = _all_gather([dscc_part[8:16]], name="gather_dcctx")
    put("c_ctx", _adamw_nd(g_dscc[:, 0:1, :].reshape(NDEV, D), c_ctx, m_c_ctx, v_c_ctx, name="adam_c_ctx",
                           silu_grad_of=c_ctx))

    r_pool, r_up1, r_down1 = _exchange_wait(h_g1, outs["grad_ada_b"], name="grads1_wait")
    r_up0, r_down0 = _exchange_wait(h_g0, outs["grad_mix_norm"], name="grads0_wait")
    r_win, r_wout = _exchange_wait(h_ga, outs["grad_c_ctx"], name="grads_attn_wait")
    put("even_w_in", _adamw_nd(r_win[:, None], even_w_in, m_even_w_in, v_even_w_in, name="adam_w_in"))
    put("even_w_out", _adamw_nd(r_wout[:, None], even_w_out, m_even_w_out, v_even_w_out, name="adam_w_out"))
    put("odd_pool_w", _adamw_nd(r_pool[:, None], odd_pool_w, m_odd_pool_w, v_odd_pool_w, name="adam_pool_w"))
    put("ffn_w_up", _adamw_nd([r_up0, r_up1], ffn_w_up, m_ffn_w_up, v_ffn_w_up, name="adam_w_up"))
    put("ffn_w_down", _adamw_nd([r_down0, r_down1], ffn_w_down, m_ffn_w_down, v_ffn_w_down, name="adam_w_down"))

    names = ["c_ctx", "ada_w", "ada_b", "mix_norm", "ffn_norm", "even_w_in", "even_q_gain", "even_k_gain",
             "even_conv_w", "even_w_out", "odd_pool_w", "odd_pool_scale", "ffn_w_up", "ffn_conv_w", "ffn_conv_b",
             "ffn_w_down"]
    result = [loss, grad_x[None]]
    for kind in ("grad_", "delta_", "new_m_", "new_v_"):
        result += [outs[kind + nm] for nm in names]
    return tuple(result)
```

```python
import functools
import math

import jax
import jax.numpy as jnp
from jax import lax
from jax.experimental import pallas as pl
from jax.experimental.pallas import tpu as pltpu

F32 = jnp.float32
BF16 = jnp.bfloat16

D = 1024
HD = 128
NQ = 4
NKV = 2
AW = NQ * HD
CW = D - AW
DFF = 2816
GRID_W = 64
ROPE_THETA = 10000.0
POOL_WINDOWS = (2, 4, 8, 16)
PG = D // 4
EPS = 1e-6
NDEV = 8
HALO = 8
MESH = pl.DeviceIdType.MESH

ADAM_LR = 0.001
ADAM_B1 = 0.9
ADAM_B2 = 0.999
ADAM_EPS = 1e-08
ADAM_WD = 0.01
ADAM_STEP = 10


def _pick(dim, prefs):
    for p in prefs:
        if dim % p == 0:
            return p
    return dim


def _params(*sem):
    return pltpu.CompilerParams(dimension_semantics=sem)


_NT = (((1,), (1,)), ((), ()))
_TN = (((0,), (0,)), ((), ()))
_SCALE = HD ** -0.5
_QSCALE = _SCALE * math.log2(math.e)
_LN2 = math.log(2.0)


def _mm(a_list, b, *, name, ta=False, tb=False, out_dtype=F32, silu_a=False, bias=None, tm=None, tn=None, tk=None):
    if not isinstance(a_list, (list, tuple)):
        a_list = [a_list]
    na = len(a_list)
    assert not (ta and na > 1)
    if ta:
        kdim, m = a_list[0].shape
        ks = [kdim]
    else:
        m = a_list[0].shape[0]
        ks = [a.shape[1] for a in a_list]
        kdim = sum(ks)
    n = b.shape[0] if tb else b.shape[1]
    assert (b.shape[1] if tb else b.shape[0]) == kdim
    kunit = math.gcd(*ks) if na > 1 else kdim
    tm = min(tm, m) if tm else _pick(m, (512, 256, 128, 64, 32, 16, 8))
    tn = min(tn, n) if tn else _pick(n, (512, 256, 128))
    tk = min(tk, kunit) if tk else _pick(kunit, (1024, 768, 512, 256, 128))
    assert m % tm == 0 and n % tn == 0 and all(k % tk == 0 for k in ks)
    nks = [k // tk for k in ks]
    starts = [sum(nks[:i]) for i in range(na)]
    nk = sum(nks)
    has_bias = bias is not None

    def body(*refs):
        a_refs = refs[:na]
        b_ref = refs[na]
        bias_ref = refs[na + 1] if has_bias else None
        o_ref = refs[na + 1 + has_bias]
        acc = refs[-1]
        k = pl.program_id(2)

        @pl.when(k == 0)
        def _():
            acc[...] = jnp.zeros_like(acc)

        bv = b_ref[...].astype(BF16)
        dn = (((0 if ta else 1,), (1 if tb else 0,)), ((), ()))
        for idx in range(na):
            def step(idx=idx):
                av = a_refs[idx][...]
                if silu_a:
                    av = av * jax.nn.sigmoid(av)
                acc[...] += lax.dot_general(av.astype(BF16), bv, dn, preferred_element_type=F32)
            if na == 1:
                step()
            else:
                pl.when((k >= starts[idx]) & (k < starts[idx] + nks[idx]))(step)

        @pl.when(k == nk - 1)
        def _():
            r = acc[...]
            if has_bias:
                r = r + bias_ref[...]
            o_ref[...] = r.astype(o_ref.dtype)

    in_specs = []
    for idx in range(na):
        if ta:
            in_specs.append(pl.BlockSpec((tk, tm), lambda i, j, k: (k, i)))
        else:
            lo, cnt = starts[idx], nks[idx]
            in_specs.append(pl.BlockSpec((tm, tk), lambda i, j, k, lo=lo, cnt=cnt: (i, jnp.clip(k - lo, 0, cnt - 1))))
    if tb:
        in_specs.append(pl.BlockSpec((tn, tk), lambda i, j, k: (j, k)))
    else:
        in_specs.append(pl.BlockSpec((tk, tn), lambda i, j, k: (k, j)))
    args = list(a_list) + [b]
    if has_bias:
        in_specs.append(pl.BlockSpec((1, tn), lambda i, j, k: (0, j)))
        args.append(bias)
    return pl.pallas_call(
        body, grid=(m // tm, n // tn, nk), in_specs=in_specs,
        out_specs=pl.BlockSpec((tm, tn), lambda i, j, k: (i, j)),
        out_shape=jax.ShapeDtypeStruct((m, n), out_dtype),
        scratch_shapes=[pltpu.VMEM((tm, tn), F32)], name=name,
        compiler_params=_params("parallel", "parallel", "arbitrary"))(*args)


def _mm_w(a_list, w, *, name, tb=False, tm=256, out_dtype=F32):
    if not isinstance(a_list, (list, tuple)):
        a_list = [a_list]
    na = len(a_list)
    m = a_list[0].shape[0]
    ks = [a.shape[1] for a in a_list]
    offs = [sum(ks[:i]) for i in range(na)]
    n = w.shape[0] if tb else w.shape[1]
    assert (w.shape[1] if tb else w.shape[0]) == sum(ks)
    tm = min(tm, m)
    assert m % tm == 0

    def body(*refs):
        a_refs, w_ref, o_ref = refs[:na], refs[na], refs[na + 1]
        acc = None
        for idx in range(na):
            av = a_refs[idx][...].astype(BF16)
            if tb:
                part = lax.dot_general(av, w_ref[:, offs[idx]:offs[idx] + ks[idx]], _NT, preferred_element_type=F32)
            else:
                part = jnp.dot(av, w_ref[offs[idx]:offs[idx] + ks[idx], :], preferred_element_type=F32)
            acc = part if acc is None else acc + part
        o_ref[...] = acc.astype(o_ref.dtype)

    in_specs = [pl.BlockSpec((tm, k), lambda i: (i, 0)) for k in ks] + [pl.BlockSpec(w.shape, lambda i: (0, 0))]
    return pl.pallas_call(
        body, grid=(m // tm,), in_specs=in_specs, out_specs=pl.BlockSpec((tm, n), lambda i: (i, 0)),
        out_shape=jax.ShapeDtypeStruct((m, n), out_dtype), name=name, compiler_params=_params("parallel"))(*a_list, w)


def _mm_tn(pairs, *, name, tk=1024, out_dtype=BF16):
    if not isinstance(pairs, list):
        pairs = [pairs]
    m, n = pairs[0][0].shape[1], pairs[0][1].shape[1]
    tks = [min(tk, a.shape[0]) for a, _ in pairs]
    nks = [a.shape[0] // t for (a, _), t in zip(pairs, tks)]
    assert all(a.shape[0] == b.shape[0] and a.shape[0] % t == 0 for (a, b), t in zip(pairs, tks))
    starts = [sum(nks[:i]) for i in range(len(pairs))]
    nk = sum(nks)

    def body(*refs):
        o_ref, acc = refs[-2], refs[-1]
        k = pl.program_id(0)
        for idx in range(len(pairs)):
            a_ref, b_ref = refs[2 * idx], refs[2 * idx + 1]

            def step(a_ref=a_ref, b_ref=b_ref, first=(idx == 0)):
                part = lax.dot_general(a_ref[...], b_ref[...], _TN, preferred_element_type=F32)
                if first:
                    @pl.when(k == 0)
                    def _():
                        acc[...] = part

                    @pl.when(k > 0)
                    def _():
                        acc[...] += part
                else:
                    acc[...] += part

            if len(pairs) == 1:
                step()
            else:
                pl.when((k >= starts[idx]) & (k < starts[idx] + nks[idx]))(step)

        @pl.when(k == nk - 1)
        def _():
            o_ref[...] = acc[...].astype(o_ref.dtype)

    in_specs, args = [], []
    for (a, b), t, lo, cnt in zip(pairs, tks, starts, nks):
        idx_map = lambda k, lo=lo, cnt=cnt: (jnp.clip(k - lo, 0, cnt - 1), 0)
        in_specs += [pl.BlockSpec((t, m), idx_map), pl.BlockSpec((t, n), idx_map)]
        args += [a, b]
    return pl.pallas_call(
        body, grid=(nk,), in_specs=in_specs, out_specs=pl.BlockSpec((m, n), lambda k: (0, 0)),
        out_shape=jax.ShapeDtypeStruct((m, n), out_dtype), scratch_shapes=[pltpu.VMEM((m, n), F32)], name=name,
        compiler_params=_params("arbitrary"))(*args)


def _vec(d, col=None):
    if col is None:
        return pl.BlockSpec((1, d), lambda i, *_: (0, 0))
    return pl.BlockSpec((1, d), col)


def _halo_specs(tm, width, nrows, colblk=0, row_off=0):
    r = tm // HALO
    off = row_off // HALO
    last = nrows // HALO - 1
    prev = pl.BlockSpec((HALO, width), lambda i, *_: (off + jnp.maximum(i * r - 1, 0), colblk))
    nxt = pl.BlockSpec((HALO, width), lambda i, *_: (off + jnp.minimum((i + 1) * r, last), colblk))
    return prev, nxt


def _ext(prev_ref, main_ref, next_ref, i, ni):
    p = jnp.where(i > 0, prev_ref[...], 0.0)
    n = jnp.where(i < ni - 1, next_ref[...], 0.0)
    return jnp.concatenate([p, main_ref[...], n], axis=0)


def _sh(ext, k, tm):
    if k == 0:
        return ext[HALO:HALO + tm]
    rows = ext.shape[0]
    return pltpu.roll(ext, (-k) % rows, axis=0)[HALO:HALO + tm]


def _roll_rows(v, k):
    rows = v.shape[0]
    return pltpu.roll(v, (-k) % rows, axis=0) if k % rows else v


def _conv3(ext, w_ref, tm):
    return _sh(ext, -1, tm) * w_ref[0:1, :] + _sh(ext, 0, tm) * w_ref[1:2, :] + _sh(ext, 1, tm) * w_ref[2:3, :]


def _colsum(v):
    return jnp.sum(v, axis=0, keepdims=True)


def _acc_out(ref, i, val):
    @pl.when(i == 0)
    def _():
        ref[...] = val

    @pl.when(i > 0)
    def _():
        ref[...] += val


def _sigmoid(v):
    return jax.nn.sigmoid(v)


def _norm_mod(x, gain, sc, sh, *, name, y=None, g=None, ymul=None, tm=512):
    n, d = x.shape
    tm = min(tm, n)
    has_res = y is not None
    has_mul = ymul is not None

    def body(*refs):
        it = iter(refs)
        x_ref = next(it)
        y_ref = next(it) if has_res else None
        g_ref = next(it) if has_res else None
        m_ref = next(it) if has_mul else None
        gain_ref, sc_ref, sh_ref = next(it), next(it), next(it)
        xo_ref = next(it) if has_res else None
        a_ref = next(it)
        xv = x_ref[...]
        if has_res:
            yv = y_ref[...]
            if has_mul:
                yv = yv * m_ref[...]
            xv = xv + g_ref[...] * yv
            xo_ref[...] = xv
        r = lax.rsqrt(jnp.mean(xv * xv, axis=-1, keepdims=True) + EPS)
        nrm = (xv * r) * gain_ref[...]
        a_ref[...] = (nrm * (1.0 + sc_ref[...]) + sh_ref[...]).astype(BF16)

    row = pl.BlockSpec((tm, d), lambda i: (i, 0))
    in_specs, args = [row], [x]
    if has_res:
        in_specs += [row, _vec(d)]
        args += [y, g]
    if has_mul:
        in_specs.append(_vec(d))
        args.append(ymul)
    in_specs += [_vec(d)] * 3
    args += [gain, sc, sh]
    out_specs, out_shape = [], []
    if has_res:
        out_specs.append(row)
        out_shape.append(jax.ShapeDtypeStruct((n, d), F32))
    out_specs.append(row)
    out_shape.append(jax.ShapeDtypeStruct((n, d), BF16))
    res = pl.pallas_call(body, grid=(n // tm,), in_specs=in_specs, out_specs=out_specs, out_shape=out_shape,
                         name=name, compiler_params=_params("parallel"))(*args)
    return res if has_res else res[0]


def _norm_mod_bwd(da, x, gain, sc, *, name, dres=None, gate_y=None, gate_g=None, tm=512):
    n, d = x.shape
    tm = min(tm, n)
    has_res = dres is not None
    has_gate = gate_y is not None

    def body(*refs):
        it = iter(refs)
        da_ref, x_ref = next(it), next(it)
        r_ref = next(it) if has_res else None
        y_ref = next(it) if has_gate else None
        g_ref = next(it) if has_gate else None
        gain_ref, sc_ref = next(it), next(it)
        dx_ref, dsh_ref, dsc_ref, dgn_ref = next(it), next(it), next(it), next(it)
        dy_ref = next(it) if has_gate else None
        dg_ref = next(it) if has_gate else None
        i = pl.program_id(0)
        xv = x_ref[...]
        dav = da_ref[...]
        r = lax.rsqrt(jnp.mean(xv * xv, axis=-1, keepdims=True) + EPS)
        xh = xv * r
        nrm = xh * gain_ref[...]
        dn = dav * (1.0 + sc_ref[...])
        dxh = dn * gain_ref[...]
        dx = r * (dxh - xh * jnp.mean(dxh * xh, axis=-1, keepdims=True))
        if has_res:
            dx = dx + r_ref[...]
        dx_ref[...] = dx
        _acc_out(dsh_ref, i, _colsum(dav))
        _acc_out(dsc_ref, i, _colsum(dav * nrm))
        _acc_out(dgn_ref, i, _colsum(dn * xh))
        if has_gate:
            dy_ref[...] = (dx * g_ref[...]).astype(BF16)
            _acc_out(dg_ref, i, _colsum(dx * y_ref[...]))

    row = pl.BlockSpec((tm, d), lambda i: (i, 0))
    in_specs, args = [row, row], [da, x]
    if has_res:
        in_specs.append(row)
        args.append(dres)
    if has_gate:
        in_specs += [row, _vec(d)]
        args += [gate_y, gate_g]
    in_specs += [_vec(d)] * 2
    args += [gain, sc]
    vec_shape = jax.ShapeDtypeStruct((1, d), F32)
    out_specs = [row, _vec(d), _vec(d), _vec(d)]
    out_shape = [jax.ShapeDtypeStruct((n, d), F32), vec_shape, vec_shape, vec_shape]
    if has_gate:
        out_specs += [row, _vec(d)]
        out_shape += [jax.ShapeDtypeStruct((n, d), BF16), vec_shape]
    return pl.pallas_call(
        body, grid=(n // tm,), in_specs=in_specs, out_specs=out_specs, out_shape=out_shape,
        name=name, compiler_params=_params("arbitrary"))(*args)


def _loss_head(x, z, g, tgt, *, name, tm=512):
    n, d = x.shape
    tm = min(tm, n)

    def body(x_ref, z_ref, g_ref, t_ref, dx_ref, loss_ref, dz_ref, dg_ref):
        i = pl.program_id(0)
        zv = z_ref[...]
        diff = (x_ref[...] + g_ref[...] * zv) - t_ref[...]
        dx = diff * (1.0 / d)
        dx_ref[...] = dx
        part = 0.5 * jnp.sum(jnp.mean(diff * diff, axis=-1, keepdims=True), axis=0, keepdims=True)
        _acc_out(loss_ref, i, jnp.broadcast_to(part, (1, 128)))
        dz_ref[...] = (dx * g_ref[...]).astype(BF16)
        _acc_out(dg_ref, i, _colsum(dx * zv))

    row = pl.BlockSpec((tm, d), lambda i: (i, 0))
    return pl.pallas_call(
        body, grid=(n // tm,), in_specs=[row, row, _vec(d), row], out_specs=[row, _vec(128), row, _vec(d)],
        out_shape=[jax.ShapeDtypeStruct((n, d), F32), jax.ShapeDtypeStruct((1, 128), F32),
                   jax.ShapeDtypeStruct((n, d), BF16), jax.ShapeDtypeStruct((1, d), F32)],
        name=name, compiler_params=_params("arbitrary"))(x, z, g, tgt)


def _ffn_up_glu(f, w_up, cw, cb, *, name, tm=256, tc=256):
    n, d = f.shape
    tm = min(tm, n)
    ni = n // tm
    nc = DFF // tc
    halo = 16
    rows = tm + 2 * halo
    r = tm // halo
    last = n // halo - 1

    def body(f_ref, fp_ref, fn_ref, w_ref, cw_ref, cb_ref, u_ref, h_ref):
        i = pl.program_id(0)
        a = f_ref[...]
        aext = jnp.concatenate([jnp.where(i > 0, fp_ref[...], jnp.zeros_like(fp_ref[...])), a,
                                jnp.where(i < ni - 1, fn_ref[...], jnp.zeros_like(fn_ref[...]))], axis=0)
        for j in range(nc):
            cols = slice(j * tc, (j + 1) * tc)
            vcols = slice(DFF + j * tc, DFF + (j + 1) * tc)
            gext = jnp.dot(aext, w_ref[:, cols], preferred_element_type=F32)
            val = jnp.dot(a, w_ref[:, vcols], preferred_element_type=F32)
            gate = gext[halo:halo + tm]
            gc = (pltpu.roll(gext, 1, axis=0)[halo:halo + tm] * cw_ref[0:1, cols] + gate * cw_ref[1:2, cols]
                  + pltpu.roll(gext, rows - 1, axis=0)[halo:halo + tm] * cw_ref[2:3, cols]) + cb_ref[:, cols]
            u_ref[:, cols] = gate
            u_ref[:, vcols] = val
            h_ref[:, cols] = (gc * _sigmoid(gc) * val).astype(BF16)

    return pl.pallas_call(
        body, grid=(ni,),
        in_specs=[pl.BlockSpec((tm, d), lambda i: (i, 0)),
                  pl.BlockSpec((halo, d), lambda i: (jnp.maximum(i * r - 1, 0), 0)),
                  pl.BlockSpec((halo, d), lambda i: (jnp.minimum((i + 1) * r, last), 0)),
                  pl.BlockSpec(w_up.shape, lambda i: (0, 0)), pl.BlockSpec((3, DFF), lambda i: (0, 0)),
                  pl.BlockSpec((1, DFF), lambda i: (0, 0))],
        out_specs=[pl.BlockSpec((tm, 2 * DFF), lambda i: (i, 0)), pl.BlockSpec((tm, DFF), lambda i: (i, 0))],
        out_shape=[jax.ShapeDtypeStruct((n, 2 * DFF), F32), jax.ShapeDtypeStruct((n, DFF), BF16)], name=name,
        compiler_params=_params("parallel"))(f, f, f, w_up, cw, cb)


def _glu_bwd(dh, u, cw, cb, *, name, tm=256, tc=256):
    n = u.shape[0]
    tm = min(tm, n)
    nc = DFF // tc
    ni = n // tm
    rows = tm + 2 * HALO

    def body(dh_ref, dhp_ref, dhn_ref, g_ref, gp_ref, gn_ref, v_ref, vp_ref, vn_ref, cw_ref, cb_ref,
             dg_ref, dv_ref, dcw_ref, dcb_ref):
        i = pl.program_id(1)
        gext = _ext(gp_ref, g_ref, gn_ref, i, ni)
        dhext = _ext(dhp_ref, dh_ref, dhn_ref, i, ni)
        vext = _ext(vp_ref, v_ref, vn_ref, i, ni)
        gc = (_roll_rows(gext, -1) * cw_ref[0:1, :] + gext * cw_ref[1:2, :] + _roll_rows(gext, 1) * cw_ref[2:3, :]
              + cb_ref[...])
        sg = _sigmoid(gc)
        dgc = dhext * vext * (sg * (1.0 + gc * (1.0 - sg)))
        dv_ref[...] = (dh_ref[...] * (gc[HALO:HALO + tm] * sg[HALO:HALO + tm])).astype(BF16)
        dgate = (_sh(dgc, 1, tm) * cw_ref[0:1, :] + _sh(dgc, 0, tm) * cw_ref[1:2, :] + _sh(dgc, -1, tm) * cw_ref[2:3, :])
        dg_ref[...] = dgate.astype(BF16)
        dgc_t = dgc[HALO:HALO + tm]
        dcw = jnp.concatenate([_colsum(dgc_t * _sh(gext, -1, tm)), _colsum(dgc_t * _sh(gext, 0, tm)),
                               _colsum(dgc_t * _sh(gext, 1, tm))], axis=0)
        _acc_out(dcw_ref, i, dcw)
        _acc_out(dcb_ref, i, _colsum(dgc_t))

    r = tm // HALO
    last = n // HALO - 1

    def trio(off):
        return [pl.BlockSpec((tm, tc), lambda j, i: (i, off + j)),
                pl.BlockSpec((HALO, tc), lambda j, i: (jnp.maximum(i * r - 1, 0), off + j)),
                pl.BlockSpec((HALO, tc), lambda j, i: (jnp.minimum((i + 1) * r, last), off + j))]

    del rows
    return pl.pallas_call(
        body, grid=(nc, ni),
        in_specs=trio(0) + trio(0) + trio(nc) + [pl.BlockSpec((3, tc), lambda j, i: (0, j)),
                                                 pl.BlockSpec((1, tc), lambda j, i: (0, j))],
        out_specs=[pl.BlockSpec((tm, tc), lambda j, i: (i, j)), pl.BlockSpec((tm, tc), lambda j, i: (i, j)),
                   pl.BlockSpec((3, tc), lambda j, i: (0, j)), pl.BlockSpec((1, tc), lambda j, i: (0, j))],
        out_shape=[jax.ShapeDtypeStruct((n, DFF), BF16), jax.ShapeDtypeStruct((n, DFF), BF16),
                   jax.ShapeDtypeStruct((3, DFF), F32), jax.ShapeDtypeStruct((1, DFF), F32)],
        name=name, compiler_params=_params("parallel", "arbitrary"))(dh, dh, dh, u, u, u, u, u, u, cw, cb)


def _rope_tables(n):
    rows = n // GRID_W
    axis_dim = HD // 2
    inv_freq = jnp.power(ROPE_THETA, -jnp.arange(0, axis_dim, 2, dtype=F32) / axis_dim)
    ar = jnp.arange(rows, dtype=F32)[:, None] * inv_freq
    ac = jnp.arange(GRID_W, dtype=F32)[:, None] * inv_freq
    by_row = lambda a: jnp.repeat(a, GRID_W, axis=0)
    by_col = lambda a: jnp.tile(a, (rows, 1))
    cr, sr, cc, sc = by_row(jnp.cos(ar)), by_row(jnp.sin(ar)), by_col(jnp.cos(ac)), by_col(jnp.sin(ac))
    return jnp.concatenate([cr, cr, cc, cc], axis=1), jnp.concatenate([-sr, sr, -sc, sc], axis=1)


def _partner(v):
    lane = lax.broadcasted_iota(jnp.int32, v.shape, 1)
    return jnp.where((lane % 64) < 32, pltpu.roll(v, HD - 32, axis=1), pltpu.roll(v, 32, axis=1))


def _qkv_prep(p, q_gain, k_gain, cs, sn, *, name, has_q, kv_col, kv_rows=None, kv_row_off=0, kv_into=None, tm=256):
    n = p.shape[0]
    rope = cs is not None
    kv_rows = kv_rows or n
    rb = kv_row_off // tm

    def body(*refs):
        it = iter(refs)
        q_ref = next(it) if has_q else None
        kv_ref = next(it)
        qg_ref, kg_ref = next(it), next(it)
        cs_ref = next(it) if rope else None
        sn_ref = next(it) if rope else None
        if kv_into is not None:
            next(it), next(it)
        qo_ref = next(it) if has_q else None
        ko_ref, vo_ref = next(it), next(it)

        def norm_rope(xh, gain, mul=None):
            r = lax.rsqrt(jnp.mean(xh * xh, axis=-1, keepdims=True) + EPS)
            xn = (xh * r) * gain
            if rope:
                xn = xn * cs_ref[...] + _partner(xn) * sn_ref[...]
            if mul is not None:
                xn = xn * mul
            return xn.astype(BF16)

        if has_q:
            for h in range(NQ):
                qo_ref[h] = norm_rope(q_ref[:, h * HD:(h + 1) * HD], qg_ref[...], _QSCALE)
        for h in range(NKV):
            ko_ref[h] = norm_rope(kv_ref[:, h * HD:(h + 1) * HD], kg_ref[...])
            vo_ref[h] = kv_ref[:, (NKV + h) * HD:(NKV + h + 1) * HD].astype(BF16)

    in_specs, args = [], []
    if has_q:
        in_specs.append(pl.BlockSpec((tm, AW), lambda i: (i, 0)))
        args.append(p)
    in_specs += [pl.BlockSpec((tm, 2 * NKV * HD), lambda i: (i, kv_col)), _vec(HD), _vec(HD)]
    args += [p, q_gain, k_gain]
    if rope:
        in_specs += [pl.BlockSpec((tm, HD), lambda i: (i, 0))] * 2
        args += [cs, sn]
    out_specs, out_shape = [], []
    if has_q:
        out_specs.append(pl.BlockSpec((NQ, tm, HD), lambda i: (0, i, 0)))
        out_shape.append(jax.ShapeDtypeStruct((NQ, n, HD), BF16))
    out_specs += [pl.BlockSpec((NKV, tm, HD), lambda i: (0, rb + i, 0))] * 2
    out_shape += [jax.ShapeDtypeStruct((NKV, kv_rows, HD), BF16)] * 2
    aliases = {}
    if kv_into is not None:
        aliases = {len(args): int(has_q), len(args) + 1: int(has_q) + 1}
        in_specs += [pl.BlockSpec(memory_space=pl.ANY)] * 2
        args += list(kv_into)
    return pl.pallas_call(body, grid=(n // tm,), in_specs=in_specs, out_specs=out_specs, out_shape=out_shape,
                          input_output_aliases=aliases, name=name, compiler_params=_params("parallel"))(*args)


def _qkv_bwd(p, dq, dk, dv, q_gain, k_gain, cs, sn, *, name, has_q, kv_col, kv_row_off, tm=256):
    n = p.shape[0]
    rope = cs is not None
    rb = kv_row_off // tm

    def body(*refs):
        it = iter(refs)
        q_ref = next(it) if has_q else None
        kv_ref = next(it)
        dq_ref = next(it) if has_q else None
        dk_ref, dv_ref = next(it), next(it)
        qg_ref, kg_ref = next(it), next(it)
        cs_ref = next(it) if rope else None
        sn_ref = next(it) if rope else None
        dp_ref, dqg_ref, dkg_ref = next(it), next(it), next(it)
        i = pl.program_id(0)

        def back(xh, dout, gain):
            if rope:
                dout = dout * cs_ref[...] + _partner(dout * sn_ref[...])
            r = lax.rsqrt(jnp.mean(xh * xh, axis=-1, keepdims=True) + EPS)
            xhat = xh * r
            dxh = dout * gain
            dx = r * (dxh - xhat * jnp.mean(dxh * xhat, axis=-1, keepdims=True))
            return dx, _colsum(dout * xhat)

        dqg = jnp.zeros((1, HD), F32)
        dkg = jnp.zeros((1, HD), F32)
        if has_q:
            for h in range(NQ):
                dx, dg = back(q_ref[:, h * HD:(h + 1) * HD], dq_ref[h], qg_ref[...])
                dp_ref[:, h * HD:(h + 1) * HD] = dx.astype(BF16)
                dqg = dqg + dg
        else:
            dp_ref[:, 0:AW] = jnp.zeros((tm, AW), BF16)
        for h in range(NKV):
            dx, dg = back(kv_ref[:, h * HD:(h + 1) * HD], dk_ref[h], kg_ref[...])
            dp_ref[:, AW + h * HD:AW + (h + 1) * HD] = dx.astype(BF16)
            dkg = dkg + dg
            dp_ref[:, AW + (NKV + h) * HD:AW + (NKV + h + 1) * HD] = dv_ref[h].astype(BF16)
        _acc_out(dqg_ref, i, dqg)
        _acc_out(dkg_ref, i, dkg)

    in_specs, args = [], []
    if has_q:
        in_specs.append(pl.BlockSpec((tm, AW), lambda i: (i, 0)))
        args.append(p)
    in_specs.append(pl.BlockSpec((tm, 2 * NKV * HD), lambda i: (i, kv_col)))
    args.append(p)
    if has_q:
        in_specs.append(pl.BlockSpec((NQ, tm, HD), lambda i: (0, i, 0)))
        args.append(dq)
    in_specs += [pl.BlockSpec((NKV, tm, HD), lambda i: (0, rb + i, 0))] * 2 + [_vec(HD), _vec(HD)]
    args += [dk, dv, q_gain, k_gain]
    if rope:
        in_specs += [pl.BlockSpec((tm, HD), lambda i: (i, 0))] * 2
        args += [cs, sn]
    return pl.pallas_call(
        body, grid=(n // tm,), in_specs=in_specs,
        out_specs=[pl.BlockSpec((tm, D), lambda i: (i, 0)), _vec(HD), _vec(HD)],
        out_shape=[jax.ShapeDtypeStruct((n, D), BF16), jax.ShapeDtypeStruct((1, HD), F32),
                   jax.ShapeDtypeStruct((1, HD), F32)],
        name=name, compiler_params=_params("arbitrary"))(*args)


def _conv_gate_fwd(p, o, conv_w, *, name, tm=256):
    n = p.shape[0]
    ni = n // tm

    def body(gb_ref, gc_ref, gcp_ref, gcn_ref, xi_ref, xip_ref, xin_ref, o_ref, w_ref, cat_ref):
        i = pl.program_id(0)
        hext = _ext(gcp_ref, gc_ref, gcn_ref, i, ni) * _ext(xip_ref, xi_ref, xin_ref, i, ni)
        cat_ref[:, 0:AW] = o_ref[...].astype(BF16)
        cat_ref[:, AW:D] = (gb_ref[...] * _conv3(hext, w_ref, tm)).astype(BF16)

    gcp, gcn = _halo_specs(tm, CW, n, colblk=3)
    xip, xin = _halo_specs(tm, CW, n, colblk=4)
    return pl.pallas_call(
        body, grid=(ni,),
        in_specs=[pl.BlockSpec((tm, CW), lambda i: (i, 2)), pl.BlockSpec((tm, CW), lambda i: (i, 3)), gcp, gcn,
                  pl.BlockSpec((tm, CW), lambda i: (i, 4)), xip, xin, pl.BlockSpec((tm, AW), lambda i: (i, 0)),
                  pl.BlockSpec((3, CW), lambda i: (0, 0))],
        out_specs=pl.BlockSpec((tm, D), lambda i: (i, 0)), out_shape=jax.ShapeDtypeStruct((n, D), BF16),
        name=name, compiler_params=_params("parallel"))(p, p, p, p, p, p, p, o, conv_w)


def _conv_gate_bwd(dcat, p, conv_w, *, name, tm=256):
    n = p.shape[0]
    ni = n // tm

    def body(dc_ref, dcp_ref, dcn_ref, gb_ref, gbp_ref, gbn_ref, gc_ref, gcp_ref, gcn_ref, xi_ref, xip_ref, xin_ref,
             w_ref, dp_ref, dw_ref):
        i = pl.program_id(0)
        gcext = _ext(gcp_ref, gc_ref, gcn_ref, i, ni)
        xiext = _ext(xip_ref, xi_ref, xin_ref, i, ni)
        hext = gcext * xiext
        dcv = _ext(dcp_ref, dc_ref, dcn_ref, i, ni) * _ext(gbp_ref, gb_ref, gbn_ref, i, ni)
        dp_ref[:, 0:CW] = (dc_ref[...] * _conv3(hext, w_ref, tm)).astype(BF16)
        dh = _sh(dcv, 1, tm) * w_ref[0:1, :] + _sh(dcv, 0, tm) * w_ref[1:2, :] + _sh(dcv, -1, tm) * w_ref[2:3, :]
        dp_ref[:, CW:2 * CW] = (dh * xi_ref[...]).astype(BF16)
        dp_ref[:, 2 * CW:3 * CW] = (dh * gc_ref[...]).astype(BF16)
        dcv_t = dcv[HALO:HALO + tm]
        dw = jnp.concatenate([_colsum(dcv_t * _sh(hext, -1, tm)), _colsum(dcv_t * _sh(hext, 0, tm)),
                              _colsum(dcv_t * _sh(hext, 1, tm))], axis=0)
        _acc_out(dw_ref, i, dw)

    def trio(colblk):
        prev, nxt = _halo_specs(tm, CW, n, colblk=colblk)
        return [pl.BlockSpec((tm, CW), lambda i: (i, colblk)), prev, nxt]

    return pl.pallas_call(
        body, grid=(ni,), in_specs=trio(1) + trio(2) + trio(3) + trio(4) + [pl.BlockSpec((3, CW), lambda i: (0, 0))],
        out_specs=[pl.BlockSpec((tm, 3 * CW), lambda i: (i, 0)), pl.BlockSpec((3, CW), lambda i: (0, 0))],
        out_shape=[jax.ShapeDtypeStruct((n, 3 * CW), BF16), jax.ShapeDtypeStruct((3, CW), F32)],
        name=name, compiler_params=_params("arbitrary"))(dcat, dcat, dcat, p, p, p, p, p, p, p, p, p, conv_w)


def _attn_fwd(q, k, v, *, name, bq=128):
    n = q.shape[1]
    t = k.shape[1]
    bq = min(bq, n)

    def body(q_ref, k_ref, v_ref, o_ref, lse_ref):
        q2 = q_ref[...].reshape(2 * bq, HD)
        s = lax.dot_general(q2, k_ref[0], _NT, preferred_element_type=F32)
        m = jnp.max(s, axis=-1, keepdims=True)
        pv = jnp.exp2(s - m)
        l = jnp.sum(pv, axis=-1, keepdims=True)
        out = jnp.dot(pv.astype(BF16), v_ref[0], preferred_element_type=F32) / l
        o_ref[:, 0:HD] = out[0:bq]
        o_ref[:, HD:2 * HD] = out[bq:2 * bq]
        lse_ref[...] = (m + jnp.log2(l)).reshape(2, bq, 1)

    kspec = pl.BlockSpec((1, t, HD), lambda h, i: (h, 0, 0))
    return pl.pallas_call(
        body, grid=(NKV, n // bq),
        in_specs=[pl.BlockSpec((2, bq, HD), lambda h, i: (h, i, 0)), kspec, kspec],
        out_specs=[pl.BlockSpec((bq, 2 * HD), lambda h, i: (i, h)), pl.BlockSpec((2, bq, 1), lambda h, i: (h, i, 0))],
        out_shape=[jax.ShapeDtypeStruct((n, AW), F32), jax.ShapeDtypeStruct((NQ, n, 1), F32)],
        name=name, compiler_params=_params("parallel", "parallel"))(q, k, v)


def _attn_bwd(q, k, v, dcat, o, lse, *, name, bq=256):
    n = q.shape[1]
    t = k.shape[1]
    bq = min(bq, n)

    def body(q_ref, k_ref, v_ref, dc_ref, o_ref, lse_ref, dq_ref, dk_ref, dv_ref):
        qi = pl.program_id(1)
        q2 = q_ref[...].reshape(2 * bq, HD)
        do_f = jnp.concatenate([dc_ref[:, 0:HD], dc_ref[:, HD:2 * HD]], axis=0)
        o_f = jnp.concatenate([o_ref[:, 0:HD], o_ref[:, HD:2 * HD]], axis=0)
        delta = jnp.sum(do_f * o_f, axis=-1, keepdims=True)
        do2 = do_f.astype(BF16)
        s = lax.dot_general(q2, k_ref[0], _NT, preferred_element_type=F32)
        pv = jnp.exp2(s - lse_ref[...].reshape(2 * bq, 1))
        dp = lax.dot_general(do2, v_ref[0], _NT, preferred_element_type=F32)
        ds = (pv * (dp - delta)).astype(BF16)
        dq_ref[...] = (jnp.dot(ds, k_ref[0], preferred_element_type=F32) * _SCALE).reshape(2, bq, HD)
        dk_part = lax.dot_general(ds, q2, _TN, preferred_element_type=F32) * _LN2
        dv_part = lax.dot_general(pv.astype(BF16), do2, _TN, preferred_element_type=F32)

        @pl.when(qi == 0)
        def _():
            dk_ref[0] = dk_part
            dv_ref[0] = dv_part

        @pl.when(qi > 0)
        def _():
            dk_ref[0] += dk_part
            dv_ref[0] += dv_part

    qspec = pl.BlockSpec((2, bq, HD), lambda h, i: (h, i, 0))
    kspec = pl.BlockSpec((1, t, HD), lambda h, i: (h, 0, 0))
    sspec = pl.BlockSpec((2, bq, 1), lambda h, i: (h, i, 0))
    cspec = pl.BlockSpec((bq, 2 * HD), lambda h, i: (i, h))
    return pl.pallas_call(
        body, grid=(NKV, n // bq), in_specs=[qspec, kspec, kspec, cspec, cspec, sspec], out_specs=[qspec, kspec, kspec],
        out_shape=[jax.ShapeDtypeStruct((NQ, n, HD), F32), jax.ShapeDtypeStruct((NKV, t, HD), F32),
                   jax.ShapeDtypeStruct((NKV, t, HD), F32)],
        name=name, compiler_params=_params("parallel", "arbitrary"))(q, k, v, dcat, o, lse)


def _window_sums(ext, w):
    s, step = ext, 1
    while step < w:
        s = s + _roll_rows(s, step)
        step *= 2
    return s


def _pool_counts(i, tm, n, w, rows, first):
    t = i * tm - HALO + first + lax.broadcasted_iota(jnp.int32, (rows, 1), 0)
    lo = jnp.clip(t - w // 2, 0, n)
    hi = jnp.clip(t + w - w // 2, 0, n)
    return jnp.maximum(hi - lo, 1).astype(F32)


def _norm_mod_ext(xext, gain_ref, sc_ref, sh_ref, i, tm, n):
    rows = xext.shape[0]
    t = i * tm - HALO + lax.broadcasted_iota(jnp.int32, (rows, 1), 0)
    inside = (t >= 0) & (t < n)
    r = lax.rsqrt(jnp.mean(xext * xext, axis=-1, keepdims=True) + EPS)
    xh = xext * r
    a = (xh * gain_ref[...]) * (1.0 + sc_ref[...]) + sh_ref[...]
    return jnp.where(inside, a, 0.0), r, xh


def _pool_fwd(x, y, g, gain, sc, sh, pool_w, *, name, tm=256):
    n, d = x.shape
    ni = n // tm

    def body(x_ref, xp_ref, xn_ref, y_ref, yp_ref, yn_ref, g_ref, gain_ref, sc_ref, sh_ref, w_ref, xo_ref, o_ref):
        i = pl.program_id(0)
        xext = _ext(xp_ref, x_ref, xn_ref, i, ni) + g_ref[...] * _ext(yp_ref, y_ref, yn_ref, i, ni)
        xo_ref[...] = xext[HALO:HALO + tm]
        aext, _, _ = _norm_mod_ext(xext, gain_ref, sc_ref, sh_ref, i, tm, n)
        for gi, w in enumerate(POOL_WINDOWS):
            ag = aext[:, gi * PG:(gi + 1) * PG]
            mean = _sh(_window_sums(ag, w), -(w // 2), tm) / _pool_counts(i, tm, n, w, tm, HALO)
            pooled = mean - ag[HALO:HALO + tm]
            o_ref[:, gi * PG:(gi + 1) * PG] = jnp.dot(pooled.astype(BF16), w_ref[gi], preferred_element_type=F32)

    row = pl.BlockSpec((tm, d), lambda i: (i, 0))
    prev, nxt = _halo_specs(tm, d, n)
    return pl.pallas_call(
        body, grid=(ni,),
        in_specs=[row, prev, nxt, row, prev, nxt, _vec(d), _vec(d), _vec(d), _vec(d),
                  pl.BlockSpec((4, PG, PG), lambda i: (0, 0, 0))],
        out_specs=[row, row], out_shape=[jax.ShapeDtypeStruct((n, d), F32)] * 2,
        name=name, compiler_params=_params("parallel"))(x, x, x, y, y, y, g, gain, sc, sh, pool_w)


def _pool_bwd(dxo, mixed, x, g, scale, gain, sc, sh, pool_w, zprev, gprev, *, name, tm=256):
    n, d = x.shape
    ni = n // tm

    def body(dx_ref, dxp_ref, dxn_ref, mx_ref, x_ref, xp_ref, xn_ref, g_ref, s_ref, gain_ref, sc_ref, sh_ref, w_ref,
             zp_ref, gp_ref, dxi_ref, dw_ref, dg_ref, dsl_ref, dsh_ref, dsc_ref, dgn_ref, dzp_ref, dgp_ref):
        i = pl.program_id(0)
        dxo_t = dx_ref[...]
        mixed_t = mx_ref[...]
        dy_t = dxo_t * g_ref[...]
        _acc_out(dg_ref, i, _colsum(dxo_t * (mixed_t * s_ref[...])))
        _acc_out(dsl_ref, i, _colsum(dy_t * mixed_t))
        dmixed = (_ext(dxp_ref, dx_ref, dxn_ref, i, ni) * g_ref[...]) * s_ref[...]
        xext = _ext(xp_ref, x_ref, xn_ref, i, ni)
        aext, rext, xhext = _norm_mod_ext(xext, gain_ref, sc_ref, sh_ref, i, tm, n)
        rows = tm + 2 * HALO
        da_parts = []
        for gi, w in enumerate(POOL_WINDOWS):
            sl = slice(gi * PG, (gi + 1) * PG)
            ag = aext[:, sl]
            mean = _sh(_window_sums(ag, w), -(w // 2), tm) / _pool_counts(i, tm, n, w, tm, HALO)
            pooled = (mean - ag[HALO:HALO + tm]).astype(BF16)
            dmg = dmixed[:, sl].astype(BF16)
            dwg = lax.dot_general(pooled, dmixed[HALO:HALO + tm, sl].astype(BF16), _TN, preferred_element_type=F32)

            @pl.when(i == 0)
            def _(dwg=dwg, gi=gi):
                dw_ref[gi] = dwg

            @pl.when(i > 0)
            def _(dwg=dwg, gi=gi):
                dw_ref[gi] += dwg

            dpl = lax.dot_general(dmg, w_ref[gi], _NT, preferred_element_type=F32)
            e = dpl / _pool_counts(i, tm, n, w, rows, 0)
            da_parts.append(_sh(_window_sums(e, w), 1 - w // 2, tm) - dpl[HALO:HALO + tm])
        da = jnp.concatenate(da_parts, axis=1)
        r = rext[HALO:HALO + tm]
        xh = xhext[HALO:HALO + tm]
        nrm = xh * gain_ref[...]
        dn = da * (1.0 + sc_ref[...])
        dxh = dn * gain_ref[...]
        dxi = dxo_t + r * (dxh - xh * jnp.mean(dxh * xh, axis=-1, keepdims=True))
        dxi_ref[...] = dxi
        _acc_out(dsh_ref, i, _colsum(da))
        _acc_out(dsc_ref, i, _colsum(da * nrm))
        _acc_out(dgn_ref, i, _colsum(dn * xh))
        dzp_ref[...] = (dxi * gp_ref[...]).astype(BF16)
        _acc_out(dgp_ref, i, _colsum(dxi * zp_ref[...]))

    row = pl.BlockSpec((tm, d), lambda i: (i, 0))
    prev, nxt = _halo_specs(tm, d, n)
    wspec = pl.BlockSpec((4, PG, PG), lambda i: (0, 0, 0))
    vshape = jax.ShapeDtypeStruct((1, d), F32)
    return pl.pallas_call(
        body, grid=(ni,),
        in_specs=[row, prev, nxt, row, row, prev, nxt] + [_vec(d)] * 5 + [wspec, row, _vec(d)],
        out_specs=[row, wspec] + [_vec(d)] * 5 + [row, _vec(d)],
        out_shape=[jax.ShapeDtypeStruct((n, d), F32), jax.ShapeDtypeStruct((4, PG, PG), F32)] + [vshape] * 5
        + [jax.ShapeDtypeStruct((n, d), BF16), vshape],
        name=name, compiler_params=_params("arbitrary"))(dxo, dxo, dxo, mixed, x, x, x, g, scale, gain, sc, sh, pool_w,
                                                         zprev, gprev)


def _adamw(gparts_list, w, m, v, *, name, silu_grad_of=None):
    nl = len(gparts_list)
    nparts, r, c = gparts_list[0].shape
    tr = _pick(r, (256, 128, 64, 32, 16, 8))
    has_c = silu_grad_of is not None

    def body(*refs):
        gp_refs = refs[:nl]
        it = iter(refs[nl:])
        w_ref, m_ref, v_ref = next(it), next(it), next(it)
        c_ref = next(it) if has_c else None
        g_ref, d_ref, mo_ref, vo_ref = next(it), next(it), next(it), next(it)
        layer = pl.program_id(0)

        def update(gp_ref):
            g = gp_ref[0].astype(F32)
            for p in range(1, nparts):
                g = g + gp_ref[p].astype(F32)
            if has_c:
                cv = c_ref[0]
                sg = _sigmoid(cv)
                g = g * (sg * (1.0 + cv * (1.0 - sg)))
            g_ref[0] = g
            mn = ADAM_B1 * m_ref[0] + (1.0 - ADAM_B1) * g
            vn = ADAM_B2 * v_ref[0] + (1.0 - ADAM_B2) * (g * g)
            m_hat = mn / (1.0 - ADAM_B1 ** ADAM_STEP)
            v_hat = vn / (1.0 - ADAM_B2 ** ADAM_STEP)
            d_ref[0] = -ADAM_LR * (m_hat / (jnp.sqrt(v_hat) + ADAM_EPS) + ADAM_WD * w_ref[0])
            mo_ref[0] = mn
            vo_ref[0] = vn

        if nl == 1:
            update(gp_refs[0])
        else:
            for li in range(nl):
                pl.when(layer == li)(functools.partial(update, gp_refs[li]))

    row = pl.BlockSpec((1, tr, c), lambda l, i: (l, i, 0))
    in_specs = [pl.BlockSpec((nparts, tr, c), lambda l, i, li=li: (0, jnp.where(l == li, i, 0), 0)) for li in range(nl)]
    in_specs += [row, row, row]
    args = list(gparts_list) + [w, m, v]
    if has_c:
        in_specs.append(row)
        args.append(silu_grad_of)
    return pl.pallas_call(
        body, grid=(nl, r // tr), in_specs=in_specs, out_specs=[row] * 4,
        out_shape=[jax.ShapeDtypeStruct((nl, r, c), F32)] * 4, name=name,
        compiler_params=_params("arbitrary", "arbitrary"))(*args)


def _adamw_nd(gparts, w, m, v, *, name, silu_grad_of=None):
    shape = w.shape
    c = shape[-1]
    if isinstance(gparts, (list, tuple)):
        nl = len(gparts)
        r = math.prod(shape[1:-1])
    else:
        nl = 1
        r = math.prod(shape[:-1]) if len(shape) > 1 else 1
        gparts = [gparts]
    rs = lambda a: a.reshape(nl, r, c)
    res = _adamw([gp.reshape(gp.shape[0], r, c) for gp in gparts], rs(w), rs(m), rs(v), name=name,
                 silu_grad_of=None if silu_grad_of is None else rs(silu_grad_of))
    return [a.reshape(shape) for a in res]


def _place():
    return lax.axis_index("x"), lax.axis_index("y"), lax.axis_index("c")


def _all_gather(arrs, *, name):
    k_arr = len(arrs)

    def body(*refs):
        ins = refs[:k_arr]
        outs = refs[k_arr:2 * k_arr]
        send_sems, recv_sems, local_sems = refs[2 * k_arr:]
        x, y, c = _place()
        me, sibling = (x, y, c), (x, y, 1 - c)
        chips = [(1 - x, y), (x, 1 - y), (1 - x, 1 - y)]

        def slot(a, px, py, pc):
            return outs[a].at[4 * px + 2 * py + pc]

        def copy(a, s, block, to, src=None):
            return pltpu.make_async_remote_copy(
                src_ref=slot(a, *block) if src is None else src, dst_ref=slot(a, *block),
                send_sem=send_sems.at[a, s], recv_sem=recv_sems.at[a, s], device_id=to, device_id_type=MESH)

        mine = [pltpu.make_async_copy(ins[a], slot(a, *me), local_sems.at[a]) for a in range(k_arr)]
        for cp in mine:
            cp.start()
        first = []
        for a in range(k_arr):
            first.append(copy(a, 0, me, sibling, src=ins[a]))
            first += [copy(a, 1 + j, me, (*chip, c), src=ins[a]) for j, chip in enumerate(chips)]
        for cp in first:
            cp.start()
        passed = []
        for j, chip in enumerate(chips):
            for a in range(k_arr):
                copy(a, 1 + j, (*chip, c), me).wait_recv()
                fw = copy(a, 4 + j, (*chip, c), sibling)
                fw.start()
                passed.append(fw)
        for a in range(k_arr):
            copy(a, 0, sibling, me).wait_recv()
            for j, chip in enumerate(chips):
                copy(a, 4 + j, (*chip, 1 - c), me).wait_recv()
        for cp in first + passed:
            cp.wait_send()
        for cp in mine:
            cp.wait()

    any_spec = pl.BlockSpec(memory_space=pl.ANY)
    return pl.pallas_call(
        body, in_specs=[any_spec] * k_arr, out_specs=[any_spec] * k_arr,
        out_shape=[jax.ShapeDtypeStruct((NDEV,) + a.shape, a.dtype) for a in arrs],
        scratch_shapes=[pltpu.SemaphoreType.DMA((k_arr, 7)), pltpu.SemaphoreType.DMA((k_arr, 7)),
                        pltpu.SemaphoreType.DMA((k_arr,))],
        name=name)(*arrs)


_HBM = pl.BlockSpec(memory_space=pltpu.HBM)
_SEM = pl.BlockSpec(memory_space=pltpu.SEMAPHORE)
_EFFECT = pltpu.SideEffectType.DATAFLOW_SIDE_EFFECTING


def _peers(x, y, c):
    return [(x ^ (rel >> 2), y ^ ((rel >> 1) & 1), c ^ (rel & 1)) for rel in range(1, NDEV)]


def _exchange_copies(srcs, lands, send_sems, recv_sems, scatter):
    x, y, c = _place()
    me = 4 * x + 2 * y + c
    copies = []
    for r, (px, py, pc) in enumerate(_peers(x, y, c)):
        peer = 4 * px + 2 * py + pc
        for a in range(len(srcs)):
            copies.append(pltpu.make_async_remote_copy(
                src_ref=srcs[a].at[peer] if scatter else srcs[a], dst_ref=lands[a].at[me],
                send_sem=send_sems.at[7 * a + r], recv_sem=recv_sems.at[7 * a + r], device_id=(px, py, pc),
                device_id_type=MESH))
    return copies


def _exchange_start(arrs, *, scatter, name):
    k_arr = len(arrs)
    land_shapes = [a.shape if scatter else (NDEV,) + a.shape for a in arrs]
    lands = [pltpu.with_memory_space_constraint(lax.empty(s, a.dtype), pltpu.HBM) for s, a in zip(land_shapes, arrs)]
    srcs = [pltpu.with_memory_space_constraint(a, pltpu.HBM) for a in arrs]

    def body(*refs):
        src_refs, land_refs = refs[:k_arr], refs[k_arr:2 * k_arr]
        send_sems, recv_sems = refs[2 * k_arr], refs[2 * k_arr + 1]
        token = refs[-1]
        for cp in _exchange_copies(src_refs, land_refs, send_sems, recv_sems, scatter):
            cp.start()
        token[...] = jnp.zeros_like(token)

    out_shape = ([pltpu.SemaphoreType.DMA((7 * k_arr,)), pltpu.SemaphoreType.DMA((7 * k_arr,))]
                 + [pltpu.HBM(a.shape, a.dtype) for a in arrs] + [pltpu.HBM(s, a.dtype) for s, a in zip(land_shapes, arrs)]
                 + [jax.ShapeDtypeStruct((8, 128), F32)])
    res = pl.pallas_call(
        body, name=name, out_shape=out_shape, in_specs=[_HBM] * (2 * k_arr),
        out_specs=[_SEM, _SEM] + [_HBM] * (2 * k_arr) + [pl.BlockSpec(memory_space=pltpu.VMEM)],
        input_output_aliases={i: 2 + i for i in range(2 * k_arr)},
        compiler_params=pltpu.CompilerParams(has_side_effects=_EFFECT))(*srcs, *lands)
    return dict(send=res[0], recv=res[1], srcs=list(res[2:2 + k_arr]), lands=list(res[2 + k_arr:2 + 2 * k_arr]),
                token=res[-1], scatter=scatter)


def _exchange_wait(handle, after, *, name):
    k_arr = len(handle["srcs"])
    scatter = handle["scatter"]

    def body(*refs):
        src_refs, land_refs = refs[:k_arr], refs[k_arr:2 * k_arr]
        send_sems, recv_sems = refs[2 * k_arr], refs[2 * k_arr + 1]
        x, y, c = _place()
        me = 4 * x + 2 * y + c
        for r, (px, py, pc) in enumerate(_peers(x, y, c)):
            peer = 4 * px + 2 * py + pc
            for a in range(k_arr):
                cp = pltpu.make_async_remote_copy(
                    src_ref=src_refs[a].at[peer] if scatter else src_refs[a], dst_ref=land_refs[a].at[peer],
                    send_sem=send_sems.at[7 * a + r], recv_sem=recv_sems.at[7 * a + r], device_id=(x, y, c),
                    device_id_type=MESH)
                cp.wait_send()
                cp.wait_recv()

    arrs = handle["srcs"] + handle["lands"]
    res = pl.pallas_call(
        body, name=name, out_shape=[pltpu.HBM(a.shape, a.dtype) for a in arrs],
        in_specs=[_HBM] * (2 * k_arr) + [_SEM, _SEM, pl.BlockSpec(memory_space=pl.ANY)],
        out_specs=[_HBM] * (2 * k_arr), input_output_aliases={i: i for i in range(2 * k_arr)},
        compiler_params=pltpu.CompilerParams(has_side_effects=_EFFECT))(*arrs, handle["send"], handle["recv"], after)
    me = 4 * lax.axis_index("x") + 2 * lax.axis_index("y") + lax.axis_index("c")
    out = []
    for src, land in zip(res[:k_arr], res[k_arr:]):
        own = lax.dynamic_index_in_dim(src, me, 0, keepdims=False) if scatter else src
        out.append(lax.dynamic_update_index_in_dim(land, own, me, 0))
    return out


def _ffn_fwd(x_in, y, g, ymul, gain, sc, sh, w_up, cw, cb, w_down, tag):
    xr, f = _norm_mod(x_in, gain, sc, sh, y=y, g=g, ymul=ymul, name=f"ffn_norm_{tag}")
    u, hmid = _ffn_up_glu(f, w_up, cw, cb, name=f"ffn_up_glu_{tag}")
    z = _mm_w(hmid, w_down, name=f"ffn_down_{tag}")
    return xr, f, u, hmid, z


def _ffn_bwd(dxo, dz, xr, f, u, hmid, gain, sc, w_up, cw, cb, w_down, tag, gate_y=None, gate_g=None):
    dh = _mm_w(dz, w_down, tb=True, name=f"ffn_down_dx_{tag}")
    d_wdown = _mm_tn((hmid, dz), name=f"ffn_down_dw_{tag}")
    dug, duv, dcw, dcb = _glu_bwd(dh, u, cw, cb, name=f"ffn_glu_bwd_{tag}", tm=1024)
    df = _mm_w([dug, duv], w_up, tb=True, name=f"ffn_up_dx_{tag}")
    d_wup_g = _mm_tn((f, dug), name=f"ffn_up_dwg_{tag}")
    d_wup_v = _mm_tn((f, duv), name=f"ffn_up_dwv_{tag}")
    norm_res = _norm_mod_bwd(df, xr, gain, sc, dres=dxo, gate_y=gate_y, gate_g=gate_g, name=f"ffn_norm_bwd_{tag}")
    return norm_res, (d_wup_g, d_wup_v, d_wdown, dcw, dcb)


def _split6(mod):
    return [mod[j * D:(j + 1) * D][None, :] for j in range(6)]


def _row(v):
    return v.reshape(1, -1)


def kernel(x, c, ctx, c_ctx, ada_w, ada_b, mix_norm, ffn_norm, even_w_in, even_q_gain, even_k_gain, even_conv_w, even_w_out, odd_pool_w, odd_pool_scale, ffn_w_up, ffn_conv_w, ffn_conv_b, ffn_w_down, loss_target, m_c_ctx, m_ada_w, m_ada_b, m_mix_norm, m_ffn_norm, m_even_w_in, m_even_q_gain, m_even_k_gain, m_even_conv_w, m_even_w_out, m_odd_pool_w, m_odd_pool_scale, m_ffn_w_up, m_ffn_conv_w, m_ffn_conv_b, m_ffn_w_down, v_c_ctx, v_ada_w, v_ada_b, v_mix_norm, v_ffn_norm, v_even_w_in, v_even_q_gain, v_even_k_gain, v_even_conv_w, v_even_w_out, v_odd_pool_w, v_odd_pool_scale, v_ffn_w_up, v_ffn_conv_w, v_ffn_conv_b, v_ffn_w_down):
    n = x.shape[1]
    lc = ctx.shape[1]
    me = 4 * lax.axis_index("x") + 2 * lax.axis_index("y") + lax.axis_index("c")
    xs, ctxs, tgt = x[0], ctx[0], loss_target[0]
    acols = ada_w.shape[2]

    small = jnp.concatenate([even_conv_w.reshape(-1), ffn_conv_w.reshape(-1), odd_pool_scale.reshape(-1)])
    nsmall = small.shape[0]
    small = jnp.pad(small, (0, (-nsmall) % 1024)).reshape(-1, 128)
    c_rows = jnp.pad(c, ((0, 7), (0, 0)))
    g_c, g_win, g_small = _all_gather([c_rows, even_w_in[0].astype(BF16), small], name="gather_first")
    w_in = g_win.transpose(1, 0, 2).reshape(D, -1)
    g_small = g_small.reshape(NDEV, -1)
    ecw = even_conv_w.shape[2]
    fcw = ffn_conv_w.shape[2]
    conv_w = g_small[:, :3 * ecw].reshape(NDEV, 3, ecw).transpose(1, 0, 2).reshape(3, CW)
    o1 = 3 * ecw
    fconv_w = g_small[:, o1:o1 + 6 * fcw].reshape(NDEV, 2, 3, fcw).transpose(1, 2, 0, 3).reshape(2, 3, DFF)
    o2 = o1 + 6 * fcw
    pool_scale = g_small[:, o2:o2 + D // NDEV].reshape(1, D)

    mraw = jnp.concatenate([g_c[:, 0, :], c_ctx[None, :], jnp.zeros((7, D), F32)], axis=0)
    my_bias = lax.dynamic_slice_in_dim(ada_b, me * acols, acols, axis=1)
    modp = jnp.stack([_mm(mraw, ada_w[l], silu_a=True, bias=my_bias[l:l + 1], name=f"ada_proj_{l}", tm=16, tn=256)
                      for l in range(2)])
    (g_mod,) = _all_gather([modp], name="gather_mod")
    mod_rows = g_mod.transpose(1, 2, 0, 3).reshape(2, 16, 6 * D)
    late_shards = [even_w_out[0].astype(BF16), odd_pool_w[0].astype(BF16), ffn_w_up.astype(BF16),
                   ffn_w_down.astype(BF16)]
    late_shards, mod_rows = lax.optimization_barrier((late_shards, mod_rows))
    h_weights = _exchange_start(late_shards, scatter=False, name="weights_start")
    mod_rows = mod_rows + h_weights["token"][0, 0]
    mod = lax.dynamic_index_in_dim(mod_rows, me, axis=1, keepdims=False)
    sh1, sc1, g1, sh2, sc2, g2 = _split6(mod[0])
    sh1b, sc1b, g1b, sh2b, sc2b, g2b = _split6(mod[1])
    csh1, csc1 = _split6(mod_rows[0, 8])[:2]
    mixn = [_row(mix_norm[l]) for l in range(2)]
    ffnn = [_row(ffn_norm[l]) for l in range(2)]
    qg, kg = _row(even_q_gain[0]), _row(even_k_gain[0])
    fcb = [_row(ffn_conv_b[l]) for l in range(2)]

    cs_t, sn_t = _rope_tables(n)
    a_lat = _norm_mod(xs, mixn[0], sc1, sh1, name="mix0_norm")
    a_ctx = _norm_mod(ctxs, mixn[0], csc1, csh1, name="mix0_norm_ctx")
    p_lat = _mm_w(a_lat, w_in, name="in_proj")
    p_ctx = _mm(a_ctx, w_in[:, AW:AW + 4 * HD], name="in_proj_ctx", tm=256, tn=512, tk=1024)
    kv_ctx = _qkv_prep(p_ctx, qg, kg, None, None, has_q=False, kv_col=0, kv_rows=lc + n, name="qkv_prep_ctx")
    q_r, k_all, v_all = _qkv_prep(p_lat, qg, kg, cs_t, sn_t, has_q=True, kv_col=1, kv_rows=lc + n, kv_row_off=lc,
                                  kv_into=kv_ctx, name="qkv_prep")
    o_attn, lse = _attn_fwd(q_r, k_all, v_all, name="attn_fwd")
    cat = _conv_gate_fwd(p_lat, o_attn, conv_w, name="conv_gate")
    g_wout, g_pool, g_up, g_down = _exchange_wait(h_weights, cat, name="weights_wait")
    w_out = g_wout.reshape(D, D)
    pool_w = g_pool.transpose(1, 0, 2, 3).reshape(4, PG, PG)
    w_up = [g_up[:, l].transpose(1, 0, 2).reshape(D, 2 * DFF) for l in range(2)]
    w_down = [g_down[:, l].reshape(DFF, D) for l in range(2)]
    y0 = _mm_w(cat, w_out, name="out_proj", tm=512)
    x1, f0, u0, h0, z0 = _ffn_fwd(xs, y0, g1, None, ffnn[0], sc2, sh2, w_up[0], fconv_w[0], fcb[0], w_down[0], "l0")

    x2, mixed = _pool_fwd(x1, z0, g2, mixn[1], sc1b, sh1b, pool_w, name="pool_fwd")
    x3, f1, u1, h1, z1 = _ffn_fwd(x2, mixed, g1b, pool_scale, ffnn[1], sc2b, sh2b, w_up[1], fconv_w[1], fcb[1],
                                  w_down[1], "l1")
    dx4, loss_part, dz1, dg2b = _loss_head(x3, z1, g2b, tgt, name="loss_head")
    loss = lax.psum(loss_part[0, 0], ("x", "y", "c"))

    (dx3, dsh2b, dsc2b, dffn1), (dup1g, dup1v, ddown1, dfcw1, dfcb1) = _ffn_bwd(
        dx4, dz1, x3, f1, u1, h1, ffnn[1], sc2b, w_up[1], fconv_w[1], fcb[1], w_down[1], "l1")
    dx2, dpool_w, dg1b, dpscale, dsh1b, dsc1b, dmix1, dz0, dg2 = _pool_bwd(
        dx3, mixed, x2, g1b, pool_scale, mixn[1], sc1b, sh1b, pool_w, z0, g2, name="pool_bwd")

    def up_shards(dg, dv):
        return jnp.concatenate([dg, dv], axis=1).reshape(D, NDEV, -1).transpose(1, 0, 2)

    s_pool = dpool_w.astype(BF16).reshape(4, NDEV, PG // NDEV, PG).transpose(1, 0, 2, 3)
    h_g1 = _exchange_start([s_pool, up_shards(dup1g, dup1v), ddown1.reshape(NDEV, DFF // NDEV, D)], scatter=True,
                           name="grads1_start")

    (dx1, dsh2, dsc2, dffn0, dy0, dg1), (dup0g, dup0v, ddown0, dfcw0, dfcb0) = _ffn_bwd(
        dx2, dz0, x1, f0, u0, h0, ffnn[0], sc2, w_up[0], fconv_w[0] + h_g1["token"][0, 0], fcb[0], w_down[0], "l0",
        gate_y=y0, gate_g=g1)
    h_g0 = _exchange_start([up_shards(dup0g, dup0v), ddown0.reshape(NDEV, DFF // NDEV, D)], scatter=True,
                           name="grads0_start")
    dcat = _mm_w(dy0, w_out, tb=True, name="out_proj_dx", tm=512)
    d_wout = _mm_tn((cat, dy0), name="out_proj_dw")
    dp_conv, dconv_w = _conv_gate_bwd(dcat, p_lat, conv_w + h_g0["token"][0, 0], name="conv_gate_bwd")
    dq_r, dk_all, dv_all = _attn_bwd(q_r, k_all, v_all, dcat, o_attn, lse, name="attn_bwd")
    dp_qkv, dqg_l, dkg_l = _qkv_bwd(p_lat, dq_r, dk_all, dv_all, qg, kg, cs_t, sn_t, has_q=True, kv_col=1,
                                    kv_row_off=lc, name="qkv_bwd")
    dp_ctx, _zero_qg, dkg_c = _qkv_bwd(p_ctx, None, dk_all, dv_all, qg, kg, None, None, has_q=False, kv_col=0,
                                       kv_row_off=0, name="qkv_bwd_ctx")
    da_lat = _mm_w([dp_qkv, dp_conv], w_in, tb=True, name="in_proj_dx", tm=512)
    da_ctx = _mm(dp_ctx, w_in[:, :D], tb=True, name="in_proj_dx_ctx", tm=256, tn=512, tk=1024)
    d_win_qkv = _mm_tn([(a_lat, dp_qkv), (a_ctx, dp_ctx)], name="in_proj_dw_qkv")
    d_win_conv = _mm_tn((a_lat, dp_conv), name="in_proj_dw_conv")
    d_win = jnp.concatenate([d_win_qkv, d_win_conv], axis=1)
    h_ga = _exchange_start([d_win.reshape(D, NDEV, -1).transpose(1, 0, 2), d_wout.reshape(NDEV, D // NDEV, D)],
                           scatter=True, name="grads_attn_start")
    mixn0_late = mixn[0] + h_ga["token"][0, 0]
    grad_x, dsh1, dsc1, dmix0 = _norm_mod_bwd(da_lat, xs, mixn0_late, sc1, dres=dx1, name="mix0_norm_bwd")
    _dctx, dcsh1, dcsc1, dmix0c = _norm_mod_bwd(da_ctx, ctxs, mixn0_late, csc1, name="mix0_norm_bwd_ctx")

    z1k = jnp.zeros((1, D), F32)
    pack = jnp.concatenate(
        [v.reshape(-1) for v in (dsh1, dsc1, dg1, dsh2, dsc2, dg2, dsh1b, dsc1b, dg1b, dsh2b, dsc2b, dg2b,
                                 dcsh1, dcsc1, z1k, z1k, z1k, z1k,
                                 dmix0, dmix1, dmix0c, z1k, dffn0, dffn1, dqg_l, dkg_l + dkg_c,
                                 dfcb0, dfcb1, dconv_w, dfcw0, dfcw1, dpscale)])
    npack = pack.shape[0]
    pack = jnp.pad(pack, (0, (-npack) % 1024)).reshape(-1, 128)
    (g_pack,) = _all_gather([pack], name="gather_small_grads")
    gp = g_pack.reshape(NDEV, -1)
    off = [0]

    def take(size):
        seg = gp[:, off[0]:off[0] + size]
        off[0] += size
        return seg

    dmod_all = take(12 * D).reshape(NDEV, 2, 6 * D)
    dmodc_all = take(6 * D).reshape(NDEV, 1, 6 * D)
    dmix_all = take(4 * D).reshape(NDEV, 2, 2, D)
    dffn_all = take(2 * D).reshape(NDEV, 2, D)
    dqg_all = take(HD).reshape(NDEV, 1, HD)
    dkg_all = take(HD).reshape(NDEV, 1, HD)
    dfcb_all = take(2 * DFF).reshape(NDEV, 2, DFF)
    dconvw_all = take(3 * CW).reshape(NDEV, 3, CW)
    dfcw_all = take(6 * DFF).reshape(NDEV, 2, 3, DFF)
    dpscale_all = take(D).reshape(NDEV, D)

    outs = {}

    def put(nm, res):
        outs["grad_" + nm], outs["delta_" + nm], outs["new_m_" + nm], outs["new_v_" + nm] = res

    dmodc_pad = jnp.concatenate([dmodc_all, jnp.zeros_like(dmodc_all)], axis=1)
    put("ada_b", _adamw_nd(jnp.concatenate([dmod_all, dmodc_pad], axis=0), ada_b, m_ada_b, v_ada_b, name="adam_ada_b"))
    put("mix_norm", _adamw_nd(jnp.concatenate([dmix_all[:, 0], dmix_all[:, 1]], axis=0), mix_norm, m_mix_norm,
                              v_mix_norm, name="adam_mix_norm"))
    put("ffn_norm", _adamw_nd(dffn_all, ffn_norm, m_ffn_norm, v_ffn_norm, name="adam_ffn_norm"))
    put("even_q_gain", _adamw_nd(dqg_all, even_q_gain, m_even_q_gain, v_even_q_gain, name="adam_q_gain"))
    put("even_k_gain", _adamw_nd(dkg_all, even_k_gain, m_even_k_gain, v_even_k_gain, name="adam_k_gain"))
    put("ffn_conv_b", _adamw_nd(dfcb_all, ffn_conv_b, m_ffn_conv_b, v_ffn_conv_b, name="adam_ffn_conv_b"))
    my_convw = lax.dynamic_slice_in_dim(dconvw_all, me * ecw, ecw, axis=2)[:, None]
    put("even_conv_w", _adamw_nd(my_convw, even_conv_w, m_even_conv_w, v_even_conv_w, name="adam_even_conv_w"))
    my_fcw = lax.dynamic_slice_in_dim(dfcw_all, me * fcw, fcw, axis=3)
    put("ffn_conv_w", _adamw_nd(my_fcw, ffn_conv_w, m_ffn_conv_w, v_ffn_conv_w, name="adam_ffn_conv_w"))
    my_ps = lax.dynamic_slice_in_dim(dpscale_all, me * (D // NDEV), D // NDEV, axis=1)[:, None]
    put("odd_pool_scale", _adamw_nd(my_ps, odd_pool_scale, m_odd_pool_scale, v_odd_pool_scale, name="adam_pool_scale"))

    dmodc_sum = dmodc_all[0]
    for dev in range(1, NDEV):
        dmodc_sum = dmodc_sum + dmodc_all[dev]
    my_cols = lambda a: lax.dynamic_slice_in_dim(a, me * acols, acols, axis=a.ndim - 1)
    rows0 = jnp.concatenate([my_cols(dmod_all[:, 0]), my_cols(dmodc_sum), jnp.zeros((7, acols), F32)], axis=0)
    rows1 = jnp.concatenate([my_cols(dmod_all[:, 1]), jnp.zeros((8, acols), F32)], axis=0)
    d_ada = jnp.stack([_mm(mraw, rows, ta=True, silu_a=True, name=f"ada_dw_{l}", tm=512, tn=256, tk=16)
                       for l, rows in enumerate((rows0, rows1))])
    put("ada_w", _adamw_nd(d_ada[None], ada_w, m_ada_w, v_ada_w, name="adam_ada_w"))
    dscc_part = _mm(rows0, ada_w[0], tb=True, name="ada_dcctx", tm=16, tn=512, tk=256)
    (g_dscc,) = _all_gather([dscc_part[8:16]], name="gather_dcctx")
    put("c_ctx", _adamw_nd(g_dscc[:, 0:1, :].reshape(NDEV, D), c_ctx, m_c_ctx, v_c_ctx, name="adam_c_ctx",
                           silu_grad_of=c_ctx))

    r_pool, r_up1, r_down1 = _exchange_wait(h_g1, outs["grad_ada_b"], name="grads1_wait")
    r_up0, r_down0 = _exchange_wait(h_g0, outs["grad_mix_norm"], name="grads0_wait")
    r_win, r_wout = _exchange_wait(h_ga, outs["grad_c_ctx"], name="grads_attn_wait")
    put("even_w_in", _adamw_nd(r_win[:, None], even_w_in, m_even_w_in, v_even_w_in, name="adam_w_in"))
    put("even_w_out", _adamw_nd(r_wout[:, None], even_w_out, m_even_w_out, v_even_w_out, name="adam_w_out"))
    put("odd_pool_w", _adamw_nd(r_pool[:, None], odd_pool_w, m_odd_pool_w, v_odd_pool_w, name="adam_pool_w"))
    put("ffn_w_up", _adamw_nd([r_up0, r_up1], ffn_w_up, m_ffn_w_up, v_ffn_w_up, name="adam_w_up"))
    put("ffn_w_down", _adamw_nd([r_down0, r_down1], ffn_w_down, m_ffn_w_down, v_ffn_w_down, name="adam_w_down"))

    names = ["c_ctx", "ada_w", "ada_b", "mix_norm", "ffn_norm", "even_w_in", "even_q_gain", "even_k_gain",
             "even_conv_w", "even_w_out", "odd_pool_w", "odd_pool_scale", "ffn_w_up", "ffn_conv_w", "ffn_conv_b",
             "ffn_w_down"]
    result = [loss, grad_x[None]]
    for kind in ("grad_", "delta_", "new_m_", "new_v_"):
        result += [outs[kind + nm] for nm in names]
    return tuple(result)
```

```python
import functools
import math

import jax
import jax.numpy as jnp
from jax import lax
from jax.experimental import pallas as pl
from jax.experimental.pallas import tpu as pltpu

F32 = jnp.float32
BF16 = jnp.bfloat16

D = 1024
HD = 128
NQ = 4
NKV = 2
AW = NQ * HD
CW = D - AW
DFF = 2816
GRID_W = 64
ROPE_THETA = 10000.0
POOL_WINDOWS = (2, 4, 8, 16)
PG = D // 4
EPS = 1e-6
NDEV = 8
HALO = 8
MESH = pl.DeviceIdType.MESH

ADAM_LR = 0.001
ADAM_B1 = 0.9
ADAM_B2 = 0.999
ADAM_EPS = 1e-08
ADAM_WD = 0.01
ADAM_STEP = 10


def _pick(dim, prefs):
    for p in prefs:
        if dim % p == 0:
            return p
    return dim


def _params(*sem):
    return pltpu.CompilerParams(dimension_semantics=sem)


_NT = (((1,), (1,)), ((), ()))
_TN = (((0,), (0,)), ((), ()))
_SCALE = HD ** -0.5
_QSCALE = _SCALE * math.log2(math.e)
_LN2 = math.log(2.0)


def _mm(a_list, b, *, name, ta=False, tb=False, out_dtype=F32, silu_a=False, bias=None, tm=None, tn=None, tk=None):
    if not isinstance(a_list, (list, tuple)):
        a_list = [a_list]
    na = len(a_list)
    assert not (ta and na > 1)
    if ta:
        kdim, m = a_list[0].shape
        ks = [kdim]
    else:
        m = a_list[0].shape[0]
        ks = [a.shape[1] for a in a_list]
        kdim = sum(ks)
    n = b.shape[0] if tb else b.shape[1]
    assert (b.shape[1] if tb else b.shape[0]) == kdim
    kunit = math.gcd(*ks) if na > 1 else kdim
    tm = min(tm, m) if tm else _pick(m, (512, 256, 128, 64, 32, 16, 8))
    tn = min(tn, n) if tn else _pick(n, (512, 256, 128))
    tk = min(tk, kunit) if tk else _pick(kunit, (1024, 768, 512, 256, 128))
    assert m % tm == 0 and n % tn == 0 and all(k % tk == 0 for k in ks)
    nks = [k // tk for k in ks]
    starts = [sum(nks[:i]) for i in range(na)]
    nk = sum(nks)
    has_bias = bias is not None

    def body(*refs):
        a_refs = refs[:na]
        b_ref = refs[na]
        bias_ref = refs[na + 1] if has_bias else None
        o_ref = refs[na + 1 + has_bias]
        acc = refs[-1]
        k = pl.program_id(2)

        @pl.when(k == 0)
        def _():
            acc[...] = jnp.zeros_like(acc)

        bv = b_ref[...].astype(BF16)
        dn = (((0 if ta else 1,), (1 if tb else 0,)), ((), ()))
        for idx in range(na):
            def step(idx=idx):
                av = a_refs[idx][...]
                if silu_a:
                    av = av * jax.nn.sigmoid(av)
                acc[...] += lax.dot_general(av.astype(BF16), bv, dn, preferred_element_type=F32)
            if na == 1:
                step()
            else:
                pl.when((k >= starts[idx]) & (k < starts[idx] + nks[idx]))(step)

        @pl.when(k == nk - 1)
        def _():
            r = acc[...]
            if has_bias:
                r = r + bias_ref[...]
            o_ref[...] = r.astype(o_ref.dtype)

    in_specs = []
    for idx in range(na):
        if ta:
            in_specs.append(pl.BlockSpec((tk, tm), lambda i, j, k: (k, i)))
        else:
            lo, cnt = starts[idx], nks[idx]
            in_specs.append(pl.BlockSpec((tm, tk), lambda i, j, k, lo=lo, cnt=cnt: (i, jnp.clip(k - lo, 0, cnt - 1))))
    if tb:
        in_specs.append(pl.BlockSpec((tn, tk), lambda i, j, k: (j, k)))
    else:
        in_specs.append(pl.BlockSpec((tk, tn), lambda i, j, k: (k, j)))
    args = list(a_list) + [b]
    if has_bias:
        in_specs.append(pl.BlockSpec((1, tn), lambda i, j, k: (0, j)))
        args.append(bias)
    return pl.pallas_call(
        body, grid=(m // tm, n // tn, nk), in_specs=in_specs,
        out_specs=pl.BlockSpec((tm, tn), lambda i, j, k: (i, j)),
        out_shape=jax.ShapeDtypeStruct((m, n), out_dtype),
        scratch_shapes=[pltpu.VMEM((tm, tn), F32)], name=name,
        compiler_params=_params("parallel", "parallel", "arbitrary"))(*args)


def _mm_w(a_list, w, *, name, tb=False, tm=256, out_dtype=F32):
    if not isinstance(a_list, (list, tuple)):
        a_list = [a_list]
    na = len(a_list)
    m = a_list[0].shape[0]
    ks = [a.shape[1] for a in a_list]
    offs = [sum(ks[:i]) for i in range(na)]
    n = w.shape[0] if tb else w.shape[1]
    assert (w.shape[1] if tb else w.shape[0]) == sum(ks)
    tm = min(tm, m)
    assert m % tm == 0

    def body(*refs):
        a_refs, w_ref, o_ref = refs[:na], refs[na], refs[na + 1]
        acc = None
        for idx in range(na):
            av = a_refs[idx][...].astype(BF16)
            if tb:
                part = lax.dot_general(av, w_ref[:, offs[idx]:offs[idx] + ks[idx]], _NT, preferred_element_type=F32)
            else:
                part = jnp.dot(av, w_ref[offs[idx]:offs[idx] + ks[idx], :], preferred_element_type=F32)
            acc = part if acc is None else acc + part
        o_ref[...] = acc.astype(o_ref.dtype)

    in_specs = [pl.BlockSpec((tm, k), lambda i: (i, 0)) for k in ks] + [pl.BlockSpec(w.shape, lambda i: (0, 0))]
    return pl.pallas_call(
        body, grid=(m // tm,), in_specs=in_specs, out_specs=pl.BlockSpec((tm, n), lambda i: (i, 0)),
        out_shape=jax.ShapeDtypeStruct((m, n), out_dtype), name=name, compiler_params=_params("parallel"))(*a_list, w)


def _mm_tn(pairs, *, name, tk=1024, out_dtype=BF16):
    if not isinstance(pairs, list):
        pairs = [pairs]
    m, n = pairs[0][0].shape[1], pairs[0][1].shape[1]
    tks = [min(tk, a.shape[0]) for a, _ in pairs]
    nks = [a.shape[0] // t for (a, _), t in zip(pairs, tks)]
    assert all(a.shape[0] == b.shape[0] and a.shape[0] % t == 0 for (a, b), t in zip(pairs, tks))
    starts = [sum(nks[:i]) for i in range(len(pairs))]
    nk = sum(nks)

    def body(*refs):
        o_ref, acc = refs[-2], refs[-1]
        k = pl.program_id(0)

        @pl.when(k == 0)
        def _():
            acc[...] = jnp.zeros_like(acc)

        for idx in range(len(pairs)):
            a_ref, b_ref = refs[2 * idx], refs[2 * idx + 1]

            def step(a_ref=a_ref, b_ref=b_ref):
                acc[...] += lax.dot_general(a_ref[...], b_ref[...], _TN, preferred_element_type=F32)

            if len(pairs) == 1:
                step()
            else:
                pl.when((k >= starts[idx]) & (k < starts[idx] + nks[idx]))(step)

        @pl.when(k == nk - 1)
        def _():
            o_ref[...] = acc[...].astype(o_ref.dtype)

    in_specs, args = [], []
    for (a, b), t, lo, cnt in zip(pairs, tks, starts, nks):
        idx_map = lambda k, lo=lo, cnt=cnt: (jnp.clip(k - lo, 0, cnt - 1), 0)
        in_specs += [pl.BlockSpec((t, m), idx_map), pl.BlockSpec((t, n), idx_map)]
        args += [a, b]
    return pl.pallas_call(
        body, grid=(nk,), in_specs=in_specs, out_specs=pl.BlockSpec((m, n), lambda k: (0, 0)),
        out_shape=jax.ShapeDtypeStruct((m, n), out_dtype), scratch_shapes=[pltpu.VMEM((m, n), F32)], name=name,
        compiler_params=_params("arbitrary"))(*args)


def _vec(d, col=None):
    if col is None:
        return pl.BlockSpec((1, d), lambda i, *_: (0, 0))
    return pl.BlockSpec((1, d), col)


def _halo_specs(tm, width, nrows, colblk=0, row_off=0):
    r = tm // HALO
    off = row_off // HALO
    last = nrows // HALO - 1
    prev = pl.BlockSpec((HALO, width), lambda i, *_: (off + jnp.maximum(i * r - 1, 0), colblk))
    nxt = pl.BlockSpec((HALO, width), lambda i, *_: (off + jnp.minimum((i + 1) * r, last), colblk))
    return prev, nxt


def _ext(prev_ref, main_ref, next_ref, i, ni):
    p = jnp.where(i > 0, prev_ref[...], 0.0)
    n = jnp.where(i < ni - 1, next_ref[...], 0.0)
    return jnp.concatenate([p, main_ref[...], n], axis=0)


def _sh(ext, k, tm):
    if k == 0:
        return ext[HALO:HALO + tm]
    rows = ext.shape[0]
    return pltpu.roll(ext, (-k) % rows, axis=0)[HALO:HALO + tm]


def _roll_rows(v, k):
    rows = v.shape[0]
    return pltpu.roll(v, (-k) % rows, axis=0) if k % rows else v


def _conv3(ext, w_ref, tm):
    return _sh(ext, -1, tm) * w_ref[0:1, :] + _sh(ext, 0, tm) * w_ref[1:2, :] + _sh(ext, 1, tm) * w_ref[2:3, :]


def _colsum(v):
    return jnp.sum(v, axis=0, keepdims=True)


def _acc_out(ref, i, val):
    @pl.when(i == 0)
    def _():
        ref[...] = jnp.zeros_like(ref)

    ref[...] += val


def _sigmoid(v):
    return jax.nn.sigmoid(v)


def _norm_mod(x, gain, sc, sh, *, name, y=None, g=None, ymul=None, tm=512):
    n, d = x.shape
    tm = min(tm, n)
    has_res = y is not None
    has_mul = ymul is not None

    def body(*refs):
        it = iter(refs)
        x_ref = next(it)
        y_ref = next(it) if has_res else None
        g_ref = next(it) if has_res else None
        m_ref = next(it) if has_mul else None
        gain_ref, sc_ref, sh_ref = next(it), next(it), next(it)
        xo_ref = next(it) if has_res else None
        a_ref = next(it)
        xv = x_ref[...]
        if has_res:
            yv = y_ref[...]
            if has_mul:
                yv = yv * m_ref[...]
            xv = xv + g_ref[...] * yv
            xo_ref[...] = xv
        r = lax.rsqrt(jnp.mean(xv * xv, axis=-1, keepdims=True) + EPS)
        nrm = (xv * r) * gain_ref[...]
        a_ref[...] = (nrm * (1.0 + sc_ref[...]) + sh_ref[...]).astype(BF16)

    row = pl.BlockSpec((tm, d), lambda i: (i, 0))
    in_specs, args = [row], [x]
    if has_res:
        in_specs += [row, _vec(d)]
        args += [y, g]
    if has_mul:
        in_specs.append(_vec(d))
        args.append(ymul)
    in_specs += [_vec(d)] * 3
    args += [gain, sc, sh]
    out_specs, out_shape = [], []
    if has_res:
        out_specs.append(row)
        out_shape.append(jax.ShapeDtypeStruct((n, d), F32))
    out_specs.append(row)
    out_shape.append(jax.ShapeDtypeStruct((n, d), BF16))
    res = pl.pallas_call(body, grid=(n // tm,), in_specs=in_specs, out_specs=out_specs, out_shape=out_shape,
                         name=name, compiler_params=_params("parallel"))(*args)
    return res if has_res else res[0]


def _norm_mod_bwd(da, x, gain, sc, *, name, dres=None, gate_y=None, gate_g=None, tm=512):
    n, d = x.shape
    tm = min(tm, n)
    has_res = dres is not None
    has_gate = gate_y is not None

    def body(*refs):
        it = iter(refs)
        da_ref, x_ref = next(it), next(it)
        r_ref = next(it) if has_res else None
        y_ref = next(it) if has_gate else None
        g_ref = next(it) if has_gate else None
        gain_ref, sc_ref = next(it), next(it)
        dx_ref, dsh_ref, dsc_ref, dgn_ref = next(it), next(it), next(it), next(it)
        dy_ref = next(it) if has_gate else None
        dg_ref = next(it) if has_gate else None
        i = pl.program_id(0)
        xv = x_ref[...]
        dav = da_ref[...]
        r = lax.rsqrt(jnp.mean(xv * xv, axis=-1, keepdims=True) + EPS)
        xh = xv * r
        nrm = xh * gain_ref[...]
        dn = dav * (1.0 + sc_ref[...])
        dxh = dn * gain_ref[...]
        dx = r * (dxh - xh * jnp.mean(dxh * xh, axis=-1, keepdims=True))
        if has_res:
            dx = dx + r_ref[...]
        dx_ref[...] = dx
        _acc_out(dsh_ref, i, _colsum(dav))
        _acc_out(dsc_ref, i, _colsum(dav * nrm))
        _acc_out(dgn_ref, i, _colsum(dn * xh))
        if has_gate:
            dy_ref[...] = (dx * g_ref[...]).astype(BF16)
            _acc_out(dg_ref, i, _colsum(dx * y_ref[...]))

    row = pl.BlockSpec((tm, d), lambda i: (i, 0))
    in_specs, args = [row, row], [da, x]
    if has_res:
        in_specs.append(row)
        args.append(dres)
    if has_gate:
        in_specs += [row, _vec(d)]
        args += [gate_y, gate_g]
    in_specs += [_vec(d)] * 2
    args += [gain, sc]
    vec_shape = jax.ShapeDtypeStruct((1, d), F32)
    out_specs = [row, _vec(d), _vec(d), _vec(d)]
    out_shape = [jax.ShapeDtypeStruct((n, d), F32), vec_shape, vec_shape, vec_shape]
    if has_gate:
        out_specs += [row, _vec(d)]
        out_shape += [jax.ShapeDtypeStruct((n, d), BF16), vec_shape]
    return pl.pallas_call(
        body, grid=(n // tm,), in_specs=in_specs, out_specs=out_specs, out_shape=out_shape,
        name=name, compiler_params=_params("arbitrary"))(*args)


def _loss_head(x, z, g, tgt, *, name, tm=512):
    n, d = x.shape
    tm = min(tm, n)

    def body(x_ref, z_ref, g_ref, t_ref, dx_ref, loss_ref, dz_ref, dg_ref):
        i = pl.program_id(0)
        zv = z_ref[...]
        diff = (x_ref[...] + g_ref[...] * zv) - t_ref[...]
        dx = diff * (1.0 / d)
        dx_ref[...] = dx
        part = 0.5 * jnp.sum(jnp.mean(diff * diff, axis=-1, keepdims=True), axis=0, keepdims=True)
        _acc_out(loss_ref, i, jnp.broadcast_to(part, (1, 128)))
        dz_ref[...] = (dx * g_ref[...]).astype(BF16)
        _acc_out(dg_ref, i, _colsum(dx * zv))

    row = pl.BlockSpec((tm, d), lambda i: (i, 0))
    return pl.pallas_call(
        body, grid=(n // tm,), in_specs=[row, row, _vec(d), row], out_specs=[row, _vec(128), row, _vec(d)],
        out_shape=[jax.ShapeDtypeStruct((n, d), F32), jax.ShapeDtypeStruct((1, 128), F32),
                   jax.ShapeDtypeStruct((n, d), BF16), jax.ShapeDtypeStruct((1, d), F32)],
        name=name, compiler_params=_params("arbitrary"))(x, z, g, tgt)


def _ffn_up_glu(f, w_up, cw, cb, *, name, tm=256, tc=256):
    n, d = f.shape
    tm = min(tm, n)
    ni = n // tm
    nc = DFF // tc
    halo = 16
    rows = tm + 2 * halo
    r = tm // halo
    last = n // halo - 1

    def body(f_ref, fp_ref, fn_ref, w_ref, cw_ref, cb_ref, u_ref, gc_ref, h_ref):
        i = pl.program_id(0)
        a = f_ref[...]
        aext = jnp.concatenate([jnp.where(i > 0, fp_ref[...], jnp.zeros_like(fp_ref[...])), a,
                                jnp.where(i < ni - 1, fn_ref[...], jnp.zeros_like(fn_ref[...]))], axis=0)
        for j in range(nc):
            cols = slice(j * tc, (j + 1) * tc)
            vcols = slice(DFF + j * tc, DFF + (j + 1) * tc)
            gext = jnp.dot(aext, w_ref[:, cols], preferred_element_type=F32)
            val = jnp.dot(a, w_ref[:, vcols], preferred_element_type=F32)
            gate = gext[halo:halo + tm]
            gc = (pltpu.roll(gext, 1, axis=0)[halo:halo + tm] * cw_ref[0:1, cols] + gate * cw_ref[1:2, cols]
                  + pltpu.roll(gext, rows - 1, axis=0)[halo:halo + tm] * cw_ref[2:3, cols]) + cb_ref[:, cols]
            u_ref[:, cols] = gate
            u_ref[:, vcols] = val
            gc_ref[:, cols] = gc
            h_ref[:, cols] = (gc * _sigmoid(gc) * val).astype(BF16)

    return pl.pallas_call(
        body, grid=(ni,),
        in_specs=[pl.BlockSpec((tm, d), lambda i: (i, 0)),
                  pl.BlockSpec((halo, d), lambda i: (jnp.maximum(i * r - 1, 0), 0)),
                  pl.BlockSpec((halo, d), lambda i: (jnp.minimum((i + 1) * r, last), 0)),
                  pl.BlockSpec(w_up.shape, lambda i: (0, 0)), pl.BlockSpec((3, DFF), lambda i: (0, 0)),
                  pl.BlockSpec((1, DFF), lambda i: (0, 0))],
        out_specs=[pl.BlockSpec((tm, 2 * DFF), lambda i: (i, 0)), pl.BlockSpec((tm, DFF), lambda i: (i, 0)),
                   pl.BlockSpec((tm, DFF), lambda i: (i, 0))],
        out_shape=[jax.ShapeDtypeStruct((n, 2 * DFF), F32), jax.ShapeDtypeStruct((n, DFF), F32),
                   jax.ShapeDtypeStruct((n, DFF), BF16)], name=name,
        compiler_params=_params("parallel"))(f, f, f, w_up, cw, cb)


def _ffn_down_glu_bwd(dz, w_down, u, gc, cw, *, name, tm=256, tc=256):
    n, d = dz.shape
    tm = min(tm, n)
    ni = n // tm
    nc = DFF // tc
    halo = 16
    rows = tm + 2 * halo
    r = tm // halo
    last = n // halo - 1

    def body(z_ref, zp_ref, zn_ref, w_ref, u_ref, up_ref, un_ref, c_ref, cp_ref, cn_ref, cw_ref,
             dg_ref, dv_ref, dcw_ref, dcb_ref):
        i = pl.program_id(0)

        @pl.when(i == 0)
        def _():
            dcw_ref[...] = jnp.zeros_like(dcw_ref)
            dcb_ref[...] = jnp.zeros_like(dcb_ref)

        zext = jnp.concatenate([jnp.where(i > 0, zp_ref[...], jnp.zeros_like(zp_ref[...])), z_ref[...],
                                jnp.where(i < ni - 1, zn_ref[...], jnp.zeros_like(zn_ref[...]))], axis=0)
        for j in range(nc):
            cols = slice(j * tc, (j + 1) * tc)
            vcols = slice(DFF + j * tc, DFF + (j + 1) * tc)
            dh = lax.dot_general(zext, w_ref[cols, :], _NT, preferred_element_type=F32)
            gcx = jnp.concatenate([cp_ref[:, cols], c_ref[:, cols], cn_ref[:, cols]], axis=0)
            vext = jnp.concatenate([up_ref[:, vcols], u_ref[:, vcols], un_ref[:, vcols]], axis=0)
            sg = _sigmoid(gcx)
            dgc = dh * vext * (sg * (1.0 + gcx * (1.0 - sg)))
            dv_ref[:, cols] = (dh[halo:halo + tm] * (gcx[halo:halo + tm] * sg[halo:halo + tm])).astype(BF16)
            d_next = pltpu.roll(dgc, rows - 1, axis=0)[halo:halo + tm]
            d_prev = pltpu.roll(dgc, 1, axis=0)[halo:halo + tm]
            d_here = dgc[halo:halo + tm]
            dg_ref[:, cols] = (d_next * cw_ref[0:1, cols] + d_here * cw_ref[1:2, cols]
                               + d_prev * cw_ref[2:3, cols]).astype(BF16)
            gate = u_ref[:, cols]
            dcw_ref[:, cols] += jnp.concatenate([_colsum(d_next * gate), _colsum(d_here * gate),
                                                 _colsum(d_prev * gate)], axis=0)
            dcb_ref[:, cols] += _colsum(d_here)

    def trio(width):
        return [pl.BlockSpec((tm, width), lambda i: (i, 0)),
                pl.BlockSpec((halo, width), lambda i: (jnp.maximum(i * r - 1, 0), 0)),
                pl.BlockSpec((halo, width), lambda i: (jnp.minimum((i + 1) * r, last), 0))]

    whole = lambda shape: pl.BlockSpec(shape, lambda i: (0, 0))
    return pl.pallas_call(
        body, grid=(ni,),
        in_specs=trio(d) + [whole(w_down.shape)] + trio(2 * DFF) + trio(DFF) + [whole((3, DFF))],
        out_specs=[pl.BlockSpec((tm, DFF), lambda i: (i, 0)), pl.BlockSpec((tm, DFF), lambda i: (i, 0)),
                   whole((3, DFF)), whole((1, DFF))],
        out_shape=[jax.ShapeDtypeStruct((n, DFF), BF16), jax.ShapeDtypeStruct((n, DFF), BF16),
                   jax.ShapeDtypeStruct((3, DFF), F32), jax.ShapeDtypeStruct((1, DFF), F32)],
        name=name, compiler_params=_params("arbitrary"))(dz, dz, dz, w_down, u, u, u, gc, gc, gc, cw)


def _rope_tables(n):
    rows = n // GRID_W
    axis_dim = HD // 2
    inv_freq = jnp.power(ROPE_THETA, -jnp.arange(0, axis_dim, 2, dtype=F32) / axis_dim)
    ar = jnp.arange(rows, dtype=F32)[:, None] * inv_freq
    ac = jnp.arange(GRID_W, dtype=F32)[:, None] * inv_freq
    by_row = lambda a: jnp.repeat(a, GRID_W, axis=0)
    by_col = lambda a: jnp.tile(a, (rows, 1))
    cr, sr, cc, sc = by_row(jnp.cos(ar)), by_row(jnp.sin(ar)), by_col(jnp.cos(ac)), by_col(jnp.sin(ac))
    return jnp.concatenate([cr, cr, cc, cc], axis=1), jnp.concatenate([-sr, sr, -sc, sc], axis=1)


def _partner(v):
    lane = lax.broadcasted_iota(jnp.int32, v.shape, 1)
    return jnp.where((lane % 64) < 32, pltpu.roll(v, HD - 32, axis=1), pltpu.roll(v, 32, axis=1))


def _qkv_prep(p, q_gain, k_gain, cs, sn, *, name, has_q, kv_col, kv_rows=None, kv_row_off=0, kv_into=None, tm=256):
    n = p.shape[0]
    rope = cs is not None
    kv_rows = kv_rows or n
    rb = kv_row_off // tm

    def body(*refs):
        it = iter(refs)
        q_ref = next(it) if has_q else None
        kv_ref = next(it)
        qg_ref, kg_ref = next(it), next(it)
        cs_ref = next(it) if rope else None
        sn_ref = next(it) if rope else None
        if kv_into is not None:
            next(it), next(it)
        qo_ref = next(it) if has_q else None
        ko_ref, vo_ref = next(it), next(it)

        def norm_rope(xh, gain, mul=None):
            r = lax.rsqrt(jnp.mean(xh * xh, axis=-1, keepdims=True) + EPS)
            xn = (xh * r) * gain
            if rope:
                xn = xn * cs_ref[...] + _partner(xn) * sn_ref[...]
            if mul is not None:
                xn = xn * mul
            return xn.astype(BF16)

        if has_q:
            for h in range(NQ):
                qo_ref[h] = norm_rope(q_ref[:, h * HD:(h + 1) * HD], qg_ref[...], _QSCALE)
        for h in range(NKV):
            ko_ref[h] = norm_rope(kv_ref[:, h * HD:(h + 1) * HD], kg_ref[...])
            vo_ref[h] = kv_ref[:, (NKV + h) * HD:(NKV + h + 1) * HD].astype(BF16)

    in_specs, args = [], []
    if has_q:
        in_specs.append(pl.BlockSpec((tm, AW), lambda i: (i, 0)))
        args.append(p)
    in_specs += [pl.BlockSpec((tm, 2 * NKV * HD), lambda i: (i, kv_col)), _vec(HD), _vec(HD)]
    args += [p, q_gain, k_gain]
    if rope:
        in_specs += [pl.BlockSpec((tm, HD), lambda i: (i, 0))] * 2
        args += [cs, sn]
    out_specs, out_shape = [], []
    if has_q:
        out_specs.append(pl.BlockSpec((NQ, tm, HD), lambda i: (0, i, 0)))
        out_shape.append(jax.ShapeDtypeStruct((NQ, n, HD), BF16))
    out_specs += [pl.BlockSpec((NKV, tm, HD), lambda i: (0, rb + i, 0))] * 2
    out_shape += [jax.ShapeDtypeStruct((NKV, kv_rows, HD), BF16)] * 2
    aliases = {}
    if kv_into is not None:
        aliases = {len(args): int(has_q), len(args) + 1: int(has_q) + 1}
        in_specs += [pl.BlockSpec(memory_space=pl.ANY)] * 2
        args += list(kv_into)
    return pl.pallas_call(body, grid=(n // tm,), in_specs=in_specs, out_specs=out_specs, out_shape=out_shape,
                          input_output_aliases=aliases, name=name, compiler_params=_params("parallel"))(*args)


def _qkv_bwd(p, dq, dk, dv, q_gain, k_gain, cs, sn, *, name, has_q, kv_col, kv_row_off, tm=256):
    n = p.shape[0]
    rope = cs is not None
    rb = kv_row_off // tm

    def body(*refs):
        it = iter(refs)
        q_ref = next(it) if has_q else None
        kv_ref = next(it)
        dq_ref = next(it) if has_q else None
        dk_ref, dv_ref = next(it), next(it)
        qg_ref, kg_ref = next(it), next(it)
        cs_ref = next(it) if rope else None
        sn_ref = next(it) if rope else None
        dp_ref, dqg_ref, dkg_ref = next(it), next(it), next(it)
        i = pl.program_id(0)

        def back(xh, dout, gain):
            if rope:
                dout = dout * cs_ref[...] + _partner(dout * sn_ref[...])
            r = lax.rsqrt(jnp.mean(xh * xh, axis=-1, keepdims=True) + EPS)
            xhat = xh * r
            dxh = dout * gain
            dx = r * (dxh - xhat * jnp.mean(dxh * xhat, axis=-1, keepdims=True))
            return dx, _colsum(dout * xhat)

        dqg = jnp.zeros((1, HD), F32)
        dkg = jnp.zeros((1, HD), F32)
        if has_q:
            for h in range(NQ):
                dx, dg = back(q_ref[:, h * HD:(h + 1) * HD], dq_ref[h], qg_ref[...])
                dp_ref[:, h * HD:(h + 1) * HD] = dx.astype(BF16)
                dqg = dqg + dg
        else:
            dp_ref[:, 0:AW] = jnp.zeros((tm, AW), BF16)
        for h in range(NKV):
            dx, dg = back(kv_ref[:, h * HD:(h + 1) * HD], dk_ref[h], kg_ref[...])
            dp_ref[:, AW + h * HD:AW + (h + 1) * HD] = dx.astype(BF16)
            dkg = dkg + dg
            dp_ref[:, AW + (NKV + h) * HD:AW + (NKV + h + 1) * HD] = dv_ref[h].astype(BF16)
        _acc_out(dqg_ref, i, dqg)
        _acc_out(dkg_ref, i, dkg)

    in_specs, args = [], []
    if has_q:
        in_specs.append(pl.BlockSpec((tm, AW), lambda i: (i, 0)))
        args.append(p)
    in_specs.append(pl.BlockSpec((tm, 2 * NKV * HD), lambda i: (i, kv_col)))
    args.append(p)
    if has_q:
        in_specs.append(pl.BlockSpec((NQ, tm, HD), lambda i: (0, i, 0)))
        args.append(dq)
    in_specs += [pl.BlockSpec((NKV, tm, HD), lambda i: (0, rb + i, 0))] * 2 + [_vec(HD), _vec(HD)]
    args += [dk, dv, q_gain, k_gain]
    if rope:
        in_specs += [pl.BlockSpec((tm, HD), lambda i: (i, 0))] * 2
        args += [cs, sn]
    return pl.pallas_call(
        body, grid=(n // tm,), in_specs=in_specs,
        out_specs=[pl.BlockSpec((tm, D), lambda i: (i, 0)), _vec(HD), _vec(HD)],
        out_shape=[jax.ShapeDtypeStruct((n, D), BF16), jax.ShapeDtypeStruct((1, HD), F32),
                   jax.ShapeDtypeStruct((1, HD), F32)],
        name=name, compiler_params=_params("arbitrary"))(*args)


def _conv_gate_fwd(p, o, conv_w, *, name, tm=256):
    n = p.shape[0]
    ni = n // tm

    def body(gb_ref, gc_ref, gcp_ref, gcn_ref, xi_ref, xip_ref, xin_ref, o_ref, w_ref, cat_ref):
        i = pl.program_id(0)
        hext = _ext(gcp_ref, gc_ref, gcn_ref, i, ni) * _ext(xip_ref, xi_ref, xin_ref, i, ni)
        cat_ref[:, 0:AW] = o_ref[...].astype(BF16)
        cat_ref[:, AW:D] = (gb_ref[...] * _conv3(hext, w_ref, tm)).astype(BF16)

    gcp, gcn = _halo_specs(tm, CW, n, colblk=3)
    xip, xin = _halo_specs(tm, CW, n, colblk=4)
    return pl.pallas_call(
        body, grid=(ni,),
        in_specs=[pl.BlockSpec((tm, CW), lambda i: (i, 2)), pl.BlockSpec((tm, CW), lambda i: (i, 3)), gcp, gcn,
                  pl.BlockSpec((tm, CW), lambda i: (i, 4)), xip, xin, pl.BlockSpec((tm, AW), lambda i: (i, 0)),
                  pl.BlockSpec((3, CW), lambda i: (0, 0))],
        out_specs=pl.BlockSpec((tm, D), lambda i: (i, 0)), out_shape=jax.ShapeDtypeStruct((n, D), BF16),
        name=name, compiler_params=_params("parallel"))(p, p, p, p, p, p, p, o, conv_w)


def _conv_gate_bwd(dcat, p, conv_w, *, name, tm=256):
    n = p.shape[0]
    ni = n // tm

    def body(dc_ref, dcp_ref, dcn_ref, gb_ref, gbp_ref, gbn_ref, gc_ref, gcp_ref, gcn_ref, xi_ref, xip_ref, xin_ref,
             w_ref, dp_ref, dw_ref):
        i = pl.program_id(0)
        gcext = _ext(gcp_ref, gc_ref, gcn_ref, i, ni)
        xiext = _ext(xip_ref, xi_ref, xin_ref, i, ni)
        hext = gcext * xiext
        dcv = _ext(dcp_ref, dc_ref, dcn_ref, i, ni) * _ext(gbp_ref, gb_ref, gbn_ref, i, ni)
        dp_ref[:, 0:CW] = (dc_ref[...] * _conv3(hext, w_ref, tm)).astype(BF16)
        dh = _sh(dcv, 1, tm) * w_ref[0:1, :] + _sh(dcv, 0, tm) * w_ref[1:2, :] + _sh(dcv, -1, tm) * w_ref[2:3, :]
        dp_ref[:, CW:2 * CW] = (dh * xi_ref[...]).astype(BF16)
        dp_ref[:, 2 * CW:3 * CW] = (dh * gc_ref[...]).astype(BF16)
        dcv_t = dcv[HALO:HALO + tm]
        dw = jnp.concatenate([_colsum(dcv_t * _sh(hext, -1, tm)), _colsum(dcv_t * _sh(hext, 0, tm)),
                              _colsum(dcv_t * _sh(hext, 1, tm))], axis=0)
        _acc_out(dw_ref, i, dw)

    def trio(colblk):
        prev, nxt = _halo_specs(tm, CW, n, colblk=colblk)
        return [pl.BlockSpec((tm, CW), lambda i: (i, colblk)), prev, nxt]

    return pl.pallas_call(
        body, grid=(ni,), in_specs=trio(1) + trio(2) + trio(3) + trio(4) + [pl.BlockSpec((3, CW), lambda i: (0, 0))],
        out_specs=[pl.BlockSpec((tm, 3 * CW), lambda i: (i, 0)), pl.BlockSpec((3, CW), lambda i: (0, 0))],
        out_shape=[jax.ShapeDtypeStruct((n, 3 * CW), BF16), jax.ShapeDtypeStruct((3, CW), F32)],
        name=name, compiler_params=_params("arbitrary"))(dcat, dcat, dcat, p, p, p, p, p, p, p, p, p, conv_w)


def _attn_fwd(q, k, v, *, name, bq=128):
    n = q.shape[1]
    t = k.shape[1]
    bq = min(bq, n)

    def body(q_ref, k_ref, v_ref, o_ref, lse_ref):
        q2 = q_ref[...].reshape(2 * bq, HD)
        s = lax.dot_general(q2, k_ref[0], _NT, preferred_element_type=F32)
        m = jnp.max(s, axis=-1, keepdims=True)
        pv = jnp.exp2(s - m)
        l = jnp.sum(pv, axis=-1, keepdims=True)
        out = jnp.dot(pv.astype(BF16), v_ref[0], preferred_element_type=F32) / l
        o_ref[:, 0:HD] = out[0:bq]
        o_ref[:, HD:2 * HD] = out[bq:2 * bq]
        lse_ref[...] = (m + jnp.log2(l)).reshape(2, bq, 1)

    kspec = pl.BlockSpec((1, t, HD), lambda h, i: (h, 0, 0))
    return pl.pallas_call(
        body, grid=(NKV, n // bq),
        in_specs=[pl.BlockSpec((2, bq, HD), lambda h, i: (h, i, 0)), kspec, kspec],
        out_specs=[pl.BlockSpec((bq, 2 * HD), lambda h, i: (i, h)), pl.BlockSpec((2, bq, 1), lambda h, i: (h, i, 0))],
        out_shape=[jax.ShapeDtypeStruct((n, AW), F32), jax.ShapeDtypeStruct((NQ, n, 1), F32)],
        name=name, compiler_params=_params("parallel", "parallel"))(q, k, v)


def _attn_bwd(q, k, v, dcat, o, lse, *, name, bq=256):
    n = q.shape[1]
    t = k.shape[1]
    bq = min(bq, n)

    def body(q_ref, k_ref, v_ref, dc_ref, o_ref, lse_ref, dq_ref, dk_ref, dv_ref):
        @pl.when(pl.program_id(1) == 0)
        def _():
            dk_ref[...] = jnp.zeros_like(dk_ref)
            dv_ref[...] = jnp.zeros_like(dv_ref)

        q2 = q_ref[...].reshape(2 * bq, HD)
        do_f = jnp.concatenate([dc_ref[:, 0:HD], dc_ref[:, HD:2 * HD]], axis=0)
        o_f = jnp.concatenate([o_ref[:, 0:HD], o_ref[:, HD:2 * HD]], axis=0)
        delta = jnp.sum(do_f * o_f, axis=-1, keepdims=True)
        do2 = do_f.astype(BF16)
        s = lax.dot_general(q2, k_ref[0], _NT, preferred_element_type=F32)
        pv = jnp.exp2(s - lse_ref[...].reshape(2 * bq, 1))
        dp = lax.dot_general(do2, v_ref[0], _NT, preferred_element_type=F32)
        ds = (pv * (dp - delta)).astype(BF16)
        dq_ref[...] = (jnp.dot(ds, k_ref[0], preferred_element_type=F32) * _SCALE).reshape(2, bq, HD)
        dk_ref[0] += lax.dot_general(ds, q2, _TN, preferred_element_type=F32) * _LN2
        dv_ref[0] += lax.dot_general(pv.astype(BF16), do2, _TN, preferred_element_type=F32)

    qspec = pl.BlockSpec((2, bq, HD), lambda h, i: (h, i, 0))
    kspec = pl.BlockSpec((1, t, HD), lambda h, i: (h, 0, 0))
    sspec = pl.BlockSpec((2, bq, 1), lambda h, i: (h, i, 0))
    cspec = pl.BlockSpec((bq, 2 * HD), lambda h, i: (i, h))
    return pl.pallas_call(
        body, grid=(NKV, n // bq), in_specs=[qspec, kspec, kspec, cspec, cspec, sspec], out_specs=[qspec, kspec, kspec],
        out_shape=[jax.ShapeDtypeStruct((NQ, n, HD), F32), jax.ShapeDtypeStruct((NKV, t, HD), F32),
                   jax.ShapeDtypeStruct((NKV, t, HD), F32)],
        name=name, compiler_params=_params("parallel", "arbitrary"))(q, k, v, dcat, o, lse)


def _window_sums(ext, w):
    s, step = ext, 1
    while step < w:
        s = s + _roll_rows(s, step)
        step *= 2
    return s


def _pool_counts(i, tm, n, w, rows, first):
    t = i * tm - HALO + first + lax.broadcasted_iota(jnp.int32, (rows, 1), 0)
    lo = jnp.clip(t - w // 2, 0, n)
    hi = jnp.clip(t + w - w // 2, 0, n)
    return jnp.maximum(hi - lo, 1).astype(F32)


def _norm_mod_ext(xext, gain_ref, sc_ref, sh_ref, i, tm, n):
    rows = xext.shape[0]
    t = i * tm - HALO + lax.broadcasted_iota(jnp.int32, (rows, 1), 0)
    inside = (t >= 0) & (t < n)
    r = lax.rsqrt(jnp.mean(xext * xext, axis=-1, keepdims=True) + EPS)
    xh = xext * r
    a = (xh * gain_ref[...]) * (1.0 + sc_ref[...]) + sh_ref[...]
    return jnp.where(inside, a, 0.0), r, xh


def _pool_fwd(x, y, g, gain, sc, sh, pool_w, *, name, tm=256):
    n, d = x.shape
    ni = n // tm

    def body(x_ref, xp_ref, xn_ref, y_ref, yp_ref, yn_ref, g_ref, gain_ref, sc_ref, sh_ref, w_ref, xo_ref, o_ref):
        i = pl.program_id(0)
        xext = _ext(xp_ref, x_ref, xn_ref, i, ni) + g_ref[...] * _ext(yp_ref, y_ref, yn_ref, i, ni)
        xo_ref[...] = xext[HALO:HALO + tm]
        aext, _, _ = _norm_mod_ext(xext, gain_ref, sc_ref, sh_ref, i, tm, n)
        for gi, w in enumerate(POOL_WINDOWS):
            ag = aext[:, gi * PG:(gi + 1) * PG]
            mean = _sh(_window_sums(ag, w), -(w // 2), tm) / _pool_counts(i, tm, n, w, tm, HALO)
            pooled = mean - ag[HALO:HALO + tm]
            o_ref[:, gi * PG:(gi + 1) * PG] = jnp.dot(pooled.astype(BF16), w_ref[gi], preferred_element_type=F32)

    row = pl.BlockSpec((tm, d), lambda i: (i, 0))
    prev, nxt = _halo_specs(tm, d, n)
    return pl.pallas_call(
        body, grid=(ni,),
        in_specs=[row, prev, nxt, row, prev, nxt, _vec(d), _vec(d), _vec(d), _vec(d),
                  pl.BlockSpec((4, PG, PG), lambda i: (0, 0, 0))],
        out_specs=[row, row], out_shape=[jax.ShapeDtypeStruct((n, d), F32)] * 2,
        name=name, compiler_params=_params("parallel"))(x, x, x, y, y, y, g, gain, sc, sh, pool_w)


def _pool_bwd(dxo, mixed, x, g, scale, gain, sc, sh, pool_w, zprev, gprev, *, name, tm=256):
    n, d = x.shape
    ni = n // tm

    def body(dx_ref, dxp_ref, dxn_ref, mx_ref, x_ref, xp_ref, xn_ref, g_ref, s_ref, gain_ref, sc_ref, sh_ref, w_ref,
             zp_ref, gp_ref, dxi_ref, dw_ref, dg_ref, dsl_ref, dsh_ref, dsc_ref, dgn_ref, dzp_ref, dgp_ref):
        i = pl.program_id(0)

        @pl.when(i == 0)
        def _():
            dw_ref[...] = jnp.zeros_like(dw_ref)

        dxo_t = dx_ref[...]
        mixed_t = mx_ref[...]
        dy_t = dxo_t * g_ref[...]
        _acc_out(dg_ref, i, _colsum(dxo_t * (mixed_t * s_ref[...])))
        _acc_out(dsl_ref, i, _colsum(dy_t * mixed_t))
        dmixed = (_ext(dxp_ref, dx_ref, dxn_ref, i, ni) * g_ref[...]) * s_ref[...]
        xext = _ext(xp_ref, x_ref, xn_ref, i, ni)
        aext, rext, xhext = _norm_mod_ext(xext, gain_ref, sc_ref, sh_ref, i, tm, n)
        rows = tm + 2 * HALO
        da_parts = []
        for gi, w in enumerate(POOL_WINDOWS):
            sl = slice(gi * PG, (gi + 1) * PG)
            ag = aext[:, sl]
            mean = _sh(_window_sums(ag, w), -(w // 2), tm) / _pool_counts(i, tm, n, w, tm, HALO)
            pooled = (mean - ag[HALO:HALO + tm]).astype(BF16)
            dmg = dmixed[:, sl].astype(BF16)
            dw_ref[gi] += lax.dot_general(pooled, dmixed[HALO:HALO + tm, sl].astype(BF16), _TN,
                                          preferred_element_type=F32)
            dpl = lax.dot_general(dmg, w_ref[gi], _NT, preferred_element_type=F32)
            e = dpl / _pool_counts(i, tm, n, w, rows, 0)
            da_parts.append(_sh(_window_sums(e, w), 1 - w // 2, tm) - dpl[HALO:HALO + tm])
        da = jnp.concatenate(da_parts, axis=1)
        r = rext[HALO:HALO + tm]
        xh = xhext[HALO:HALO + tm]
        nrm = xh * gain_ref[...]
        dn = da * (1.0 + sc_ref[...])
        dxh = dn * gain_ref[...]
        dxi = dxo_t + r * (dxh - xh * jnp.mean(dxh * xh, axis=-1, keepdims=True))
        dxi_ref[...] = dxi
        _acc_out(dsh_ref, i, _colsum(da))
        _acc_out(dsc_ref, i, _colsum(da * nrm))
        _acc_out(dgn_ref, i, _colsum(dn * xh))
        dzp_ref[...] = (dxi * gp_ref[...]).astype(BF16)
        _acc_out(dgp_ref, i, _colsum(dxi * zp_ref[...]))

    row = pl.BlockSpec((tm, d), lambda i: (i, 0))
    prev, nxt = _halo_specs(tm, d, n)
    wspec = pl.BlockSpec((4, PG, PG), lambda i: (0, 0, 0))
    vshape = jax.ShapeDtypeStruct((1, d), F32)
    return pl.pallas_call(
        body, grid=(ni,),
        in_specs=[row, prev, nxt, row, row, prev, nxt] + [_vec(d)] * 5 + [wspec, row, _vec(d)],
        out_specs=[row, wspec] + [_vec(d)] * 5 + [row, _vec(d)],
        out_shape=[jax.ShapeDtypeStruct((n, d), F32), jax.ShapeDtypeStruct((4, PG, PG), F32)] + [vshape] * 5
        + [jax.ShapeDtypeStruct((n, d), BF16), vshape],
        name=name, compiler_params=_params("arbitrary"))(dxo, dxo, dxo, mixed, x, x, x, g, scale, gain, sc, sh, pool_w,
                                                         zprev, gprev)


def _adamw(gparts_list, w, m, v, *, name, silu_grad_of=None):
    nl = len(gparts_list)
    nparts, r, c = gparts_list[0].shape
    tr = _pick(r, (256, 128, 64, 32, 16, 8))
    has_c = silu_grad_of is not None

    def body(*refs):
        gp_refs = refs[:nl]
        it = iter(refs[nl:])
        w_ref, m_ref, v_ref = next(it), next(it), next(it)
        c_ref = next(it) if has_c else None
        g_ref, d_ref, mo_ref, vo_ref = next(it), next(it), next(it), next(it)
        layer = pl.program_id(0)

        def update(gp_ref):
            g = gp_ref[0].astype(F32)
            for p in range(1, nparts):
                g = g + gp_ref[p].astype(F32)
            if has_c:
                cv = c_ref[0]
                sg = _sigmoid(cv)
                g = g * (sg * (1.0 + cv * (1.0 - sg)))
            g_ref[0] = g
            mn = ADAM_B1 * m_ref[0] + (1.0 - ADAM_B1) * g
            vn = ADAM_B2 * v_ref[0] + (1.0 - ADAM_B2) * (g * g)
            m_hat = mn / (1.0 - ADAM_B1 ** ADAM_STEP)
            v_hat = vn / (1.0 - ADAM_B2 ** ADAM_STEP)
            d_ref[0] = -ADAM_LR * (m_hat / (jnp.sqrt(v_hat) + ADAM_EPS) + ADAM_WD * w_ref[0])
            mo_ref[0] = mn
            vo_ref[0] = vn

        if nl == 1:
            update(gp_refs[0])
        else:
            for li in range(nl):
                pl.when(layer == li)(functools.partial(update, gp_refs[li]))

    row = pl.BlockSpec((1, tr, c), lambda l, i: (l, i, 0))
    in_specs = [pl.BlockSpec((nparts, tr, c), lambda l, i, li=li: (0, jnp.where(l == li, i, 0), 0)) for li in range(nl)]
    in_specs += [row, row, row]
    args = list(gparts_list) + [w, m, v]
    if has_c:
        in_specs.append(row)
        args.append(silu_grad_of)
    return pl.pallas_call(
        body, grid=(nl, r // tr), in_specs=in_specs, out_specs=[row] * 4,
        out_shape=[jax.ShapeDtypeStruct((nl, r, c), F32)] * 4, name=name,
        compiler_params=_params("arbitrary", "arbitrary"))(*args)


def _adamw_nd(gparts, w, m, v, *, name, silu_grad_of=None):
    shape = w.shape
    c = shape[-1]
    if isinstance(gparts, (list, tuple)):
        nl = len(gparts)
        r = math.prod(shape[1:-1])
    else:
        nl = 1
        r = math.prod(shape[:-1]) if len(shape) > 1 else 1
        gparts = [gparts]
    rs = lambda a: a.reshape(nl, r, c)
    res = _adamw([gp.reshape(gp.shape[0], r, c) for gp in gparts], rs(w), rs(m), rs(v), name=name,
                 silu_grad_of=None if silu_grad_of is None else rs(silu_grad_of))
    return [a.reshape(shape) for a in res]


def _place():
    return lax.axis_index("x"), lax.axis_index("y"), lax.axis_index("c")


def _all_gather(arrs, *, name):
    k_arr = len(arrs)

    def body(*refs):
        ins = refs[:k_arr]
        outs = refs[k_arr:2 * k_arr]
        send_sems, recv_sems, local_sems = refs[2 * k_arr:]
        x, y, c = _place()
        me, sibling = (x, y, c), (x, y, 1 - c)
        chips = [(1 - x, y), (x, 1 - y), (1 - x, 1 - y)]

        def slot(a, px, py, pc):
            return outs[a].at[4 * px + 2 * py + pc]

        def copy(a, s, block, to, src=None):
            return pltpu.make_async_remote_copy(
                src_ref=slot(a, *block) if src is None else src, dst_ref=slot(a, *block),
                send_sem=send_sems.at[a, s], recv_sem=recv_sems.at[a, s], device_id=to, device_id_type=MESH)

        mine = [pltpu.make_async_copy(ins[a], slot(a, *me), local_sems.at[a]) for a in range(k_arr)]
        for cp in mine:
            cp.start()
        first = []
        for a in range(k_arr):
            first.append(copy(a, 0, me, sibling, src=ins[a]))
            first += [copy(a, 1 + j, me, (*chip, c), src=ins[a]) for j, chip in enumerate(chips)]
        for cp in first:
            cp.start()
        passed = []
        for j, chip in enumerate(chips):
            for a in range(k_arr):
                copy(a, 1 + j, (*chip, c), me).wait_recv()
                fw = copy(a, 4 + j, (*chip, c), sibling)
                fw.start()
                passed.append(fw)
        for a in range(k_arr):
            copy(a, 0, sibling, me).wait_recv()
            for j, chip in enumerate(chips):
                copy(a, 4 + j, (*chip, 1 - c), me).wait_recv()
        for cp in first + passed:
            cp.wait_send()
        for cp in mine:
            cp.wait()

    any_spec = pl.BlockSpec(memory_space=pl.ANY)
    return pl.pallas_call(
        body, in_specs=[any_spec] * k_arr, out_specs=[any_spec] * k_arr,
        out_shape=[jax.ShapeDtypeStruct((NDEV,) + a.shape, a.dtype) for a in arrs],
        scratch_shapes=[pltpu.SemaphoreType.DMA((k_arr, 7)), pltpu.SemaphoreType.DMA((k_arr, 7)),
                        pltpu.SemaphoreType.DMA((k_arr,))],
        name=name)(*arrs)


_HBM = pl.BlockSpec(memory_space=pltpu.HBM)
_SEM = pl.BlockSpec(memory_space=pltpu.SEMAPHORE)
_EFFECT = pltpu.SideEffectType.DATAFLOW_SIDE_EFFECTING


def _peers(x, y, c):
    return [(x ^ (rel >> 2), y ^ ((rel >> 1) & 1), c ^ (rel & 1)) for rel in range(1, NDEV)]


def _exchange_copies(srcs, lands, send_sems, recv_sems, scatter):
    x, y, c = _place()
    me = 4 * x + 2 * y + c
    copies = []
    for r, (px, py, pc) in enumerate(_peers(x, y, c)):
        peer = 4 * px + 2 * py + pc
        for a in range(len(srcs)):
            copies.append(pltpu.make_async_remote_copy(
                src_ref=srcs[a].at[peer] if scatter else srcs[a], dst_ref=lands[a].at[me],
                send_sem=send_sems.at[7 * a + r], recv_sem=recv_sems.at[7 * a + r], device_id=(px, py, pc),
                device_id_type=MESH))
    return copies


def _exchange_start(arrs, *, scatter, name):
    k_arr = len(arrs)
    land_shapes = [a.shape if scatter else (NDEV,) + a.shape for a in arrs]
    lands = [pltpu.with_memory_space_constraint(lax.empty(s, a.dtype), pltpu.HBM) for s, a in zip(land_shapes, arrs)]
    srcs = [pltpu.with_memory_space_constraint(a, pltpu.HBM) for a in arrs]

    def body(*refs):
        src_refs, land_refs = refs[:k_arr], refs[k_arr:2 * k_arr]
        send_sems, recv_sems = refs[2 * k_arr], refs[2 * k_arr + 1]
        token = refs[-1]
        for cp in _exchange_copies(src_refs, land_refs, send_sems, recv_sems, scatter):
            cp.start()
        token[...] = jnp.zeros_like(token)

    out_shape = ([pltpu.SemaphoreType.DMA((7 * k_arr,)), pltpu.SemaphoreType.DMA((7 * k_arr,))]
                 + [pltpu.HBM(a.shape, a.dtype) for a in arrs] + [pltpu.HBM(s, a.dtype) for s, a in zip(land_shapes, arrs)]
                 + [jax.ShapeDtypeStruct((8, 128), F32)])
    res = pl.pallas_call(
        body, name=name, out_shape=out_shape, in_specs=[_HBM] * (2 * k_arr),
        out_specs=[_SEM, _SEM] + [_HBM] * (2 * k_arr) + [pl.BlockSpec(memory_space=pltpu.VMEM)],
        input_output_aliases={i: 2 + i for i in range(2 * k_arr)},
        compiler_params=pltpu.CompilerParams(has_side_effects=_EFFECT))(*srcs, *lands)
    return dict(send=res[0], recv=res[1], srcs=list(res[2:2 + k_arr]), lands=list(res[2 + k_arr:2 + 2 * k_arr]),
                token=res[-1], scatter=scatter)


def _exchange_wait(handle, after, *, name):
    k_arr = len(handle["srcs"])
    scatter = handle["scatter"]

    def body(*refs):
        src_refs, land_refs = refs[:k_arr], refs[k_arr:2 * k_arr]
        send_sems, recv_sems = refs[2 * k_arr], refs[2 * k_arr + 1]
        x, y, c = _place()
        me = 4 * x + 2 * y + c
        for r, (px, py, pc) in enumerate(_peers(x, y, c)):
            peer = 4 * px + 2 * py + pc
            for a in range(k_arr):
                cp = pltpu.make_async_remote_copy(
                    src_ref=src_refs[a].at[peer] if scatter else src_refs[a], dst_ref=land_refs[a].at[peer],
                    send_sem=send_sems.at[7 * a + r], recv_sem=recv_sems.at[7 * a + r], device_id=(x, y, c),
                    device_id_type=MESH)
                cp.wait_send()
                cp.wait_recv()

    arrs = handle["srcs"] + handle["lands"]
    res = pl.pallas_call(
        body, name=name, out_shape=[pltpu.HBM(a.shape, a.dtype) for a in arrs],
        in_specs=[_HBM] * (2 * k_arr) + [_SEM, _SEM, pl.BlockSpec(memory_space=pl.ANY)],
        out_specs=[_HBM] * (2 * k_arr), input_output_aliases={i: i for i in range(2 * k_arr)},
        compiler_params=pltpu.CompilerParams(has_side_effects=_EFFECT))(*arrs, handle["send"], handle["recv"], after)
    me = 4 * lax.axis_index("x") + 2 * lax.axis_index("y") + lax.axis_index("c")
    out = []
    for src, land in zip(res[:k_arr], res[k_arr:]):
        own = lax.dynamic_index_in_dim(src, me, 0, keepdims=False) if scatter else src
        out.append(lax.dynamic_update_index_in_dim(land, own, me, 0))
    return out


def _ffn_fwd(x_in, y, g, ymul, gain, sc, sh, w_up, cw, cb, w_down, tag):
    xr, f = _norm_mod(x_in, gain, sc, sh, y=y, g=g, ymul=ymul, name=f"ffn_norm_{tag}")
    u, gc, hmid = _ffn_up_glu(f, w_up, cw, cb, name=f"ffn_up_glu_{tag}")
    z = _mm_w(hmid, w_down, name=f"ffn_down_{tag}")
    return xr, f, (u, gc), hmid, z


def _ffn_bwd(dxo, dz, xr, f, u_gc, hmid, gain, sc, w_up, cw, w_down, tag, gate_y=None, gate_g=None):
    d_wdown = _mm_tn((hmid, dz), name=f"ffn_down_dw_{tag}")
    dug, duv, dcw, dcb = _ffn_down_glu_bwd(dz, w_down, u_gc[0], u_gc[1], cw, name=f"ffn_down_glu_bwd_{tag}")
    df = _mm_w([dug, duv], w_up, tb=True, name=f"ffn_up_dx_{tag}")
    d_wup_g = _mm_tn((f, dug), name=f"ffn_up_dwg_{tag}")
    d_wup_v = _mm_tn((f, duv), name=f"ffn_up_dwv_{tag}")
    norm_res = _norm_mod_bwd(df, xr, gain, sc, dres=dxo, gate_y=gate_y, gate_g=gate_g, name=f"ffn_norm_bwd_{tag}")
    return norm_res, (d_wup_g, d_wup_v, d_wdown, dcw, dcb)


def _split6(mod):
    return [mod[j * D:(j + 1) * D][None, :] for j in range(6)]


def _row(v):
    return v.reshape(1, -1)


def kernel(x, c, ctx, c_ctx, ada_w, ada_b, mix_norm, ffn_norm, even_w_in, even_q_gain, even_k_gain, even_conv_w, even_w_out, odd_pool_w, odd_pool_scale, ffn_w_up, ffn_conv_w, ffn_conv_b, ffn_w_down, loss_target, m_c_ctx, m_ada_w, m_ada_b, m_mix_norm, m_ffn_norm, m_even_w_in, m_even_q_gain, m_even_k_gain, m_even_conv_w, m_even_w_out, m_odd_pool_w, m_odd_pool_scale, m_ffn_w_up, m_ffn_conv_w, m_ffn_conv_b, m_ffn_w_down, v_c_ctx, v_ada_w, v_ada_b, v_mix_norm, v_ffn_norm, v_even_w_in, v_even_q_gain, v_even_k_gain, v_even_conv_w, v_even_w_out, v_odd_pool_w, v_odd_pool_scale, v_ffn_w_up, v_ffn_conv_w, v_ffn_conv_b, v_ffn_w_down):
    n = x.shape[1]
    lc = ctx.shape[1]
    me = 4 * lax.axis_index("x") + 2 * lax.axis_index("y") + lax.axis_index("c")
    xs, ctxs, tgt = x[0], ctx[0], loss_target[0]
    acols = ada_w.shape[2]

    small = jnp.concatenate([even_conv_w.reshape(-1), ffn_conv_w.reshape(-1), odd_pool_scale.reshape(-1)])
    nsmall = small.shape[0]
    small = jnp.pad(small, (0, (-nsmall) % 1024)).reshape(-1, 128)
    c_rows = jnp.pad(c, ((0, 7), (0, 0)))
    g_c, g_win, g_small = _all_gather([c_rows, even_w_in[0].astype(BF16), small], name="gather_first")
    w_in = g_win.transpose(1, 0, 2).reshape(D, -1)
    g_small = g_small.reshape(NDEV, -1)
    ecw = even_conv_w.shape[2]
    fcw = ffn_conv_w.shape[2]
    conv_w = g_small[:, :3 * ecw].reshape(NDEV, 3, ecw).transpose(1, 0, 2).reshape(3, CW)
    o1 = 3 * ecw
    fconv_w = g_small[:, o1:o1 + 6 * fcw].reshape(NDEV, 2, 3, fcw).transpose(1, 2, 0, 3).reshape(2, 3, DFF)
    o2 = o1 + 6 * fcw
    pool_scale = g_small[:, o2:o2 + D // NDEV].reshape(1, D)

    mraw = jnp.concatenate([g_c[:, 0, :], c_ctx[None, :], jnp.zeros((7, D), F32)], axis=0)
    my_bias = lax.dynamic_slice_in_dim(ada_b, me * acols, acols, axis=1)
    modp = jnp.stack([_mm(mraw, ada_w[l], silu_a=True, bias=my_bias[l:l + 1], name=f"ada_proj_{l}", tm=16, tn=256)
                      for l in range(2)])
    (g_mod,) = _all_gather([modp], name="gather_mod")
    mod_rows = g_mod.transpose(1, 2, 0, 3).reshape(2, 16, 6 * D)
    late_shards = [even_w_out[0].astype(BF16), odd_pool_w[0].astype(BF16), ffn_w_up.astype(BF16),
                   ffn_w_down.astype(BF16)]
    late_shards, mod_rows = lax.optimization_barrier((late_shards, mod_rows))
    h_weights = _exchange_start(late_shards, scatter=False, name="weights_start")
    mod_rows = mod_rows + h_weights["token"][0, 0]
    mod = lax.dynamic_index_in_dim(mod_rows, me, axis=1, keepdims=False)
    sh1, sc1, g1, sh2, sc2, g2 = _split6(mod[0])
    sh1b, sc1b, g1b, sh2b, sc2b, g2b = _split6(mod[1])
    csh1, csc1 = _split6(mod_rows[0, 8])[:2]
    mixn = [_row(mix_norm[l]) for l in range(2)]
    ffnn = [_row(ffn_norm[l]) for l in range(2)]
    qg, kg = _row(even_q_gain[0]), _row(even_k_gain[0])
    fcb = [_row(ffn_conv_b[l]) for l in range(2)]

    cs_t, sn_t = _rope_tables(n)
    a_lat = _norm_mod(xs, mixn[0], sc1, sh1, name="mix0_norm")
    a_ctx = _norm_mod(ctxs, mixn[0], csc1, csh1, name="mix0_norm_ctx")
    p_lat = _mm_w(a_lat, w_in, name="in_proj")
    p_ctx = _mm(a_ctx, w_in[:, AW:AW + 4 * HD], name="in_proj_ctx", tm=256, tn=512, tk=1024)
    kv_ctx = _qkv_prep(p_ctx, qg, kg, None, None, has_q=False, kv_col=0, kv_rows=lc + n, name="qkv_prep_ctx")
    q_r, k_all, v_all = _qkv_prep(p_lat, qg, kg, cs_t, sn_t, has_q=True, kv_col=1, kv_rows=lc + n, kv_row_off=lc,
                                  kv_into=kv_ctx, name="qkv_prep")
    o_attn, lse = _attn_fwd(q_r, k_all, v_all, name="attn_fwd")
    cat = _conv_gate_fwd(p_lat, o_attn, conv_w, name="conv_gate")
    g_wout, g_pool, g_up, g_down = _exchange_wait(h_weights, cat, name="weights_wait")
    w_out = g_wout.reshape(D, D)
    pool_w = g_pool.transpose(1, 0, 2, 3).reshape(4, PG, PG)
    w_up = [g_up[:, l].transpose(1, 0, 2).reshape(D, 2 * DFF) for l in range(2)]
    w_down = [g_down[:, l].reshape(DFF, D) for l in range(2)]
    y0 = _mm_w(cat, w_out, name="out_proj", tm=512)
    x1, f0, u0, h0, z0 = _ffn_fwd(xs, y0, g1, None, ffnn[0], sc2, sh2, w_up[0], fconv_w[0], fcb[0], w_down[0], "l0")

    x2, mixed = _pool_fwd(x1, z0, g2, mixn[1], sc1b, sh1b, pool_w, name="pool_fwd")
    x3, f1, u1, h1, z1 = _ffn_fwd(x2, mixed, g1b, pool_scale, ffnn[1], sc2b, sh2b, w_up[1], fconv_w[1], fcb[1],
                                  w_down[1], "l1")
    dx4, loss_part, dz1, dg2b = _loss_head(x3, z1, g2b, tgt, name="loss_head")
    loss = lax.psum(loss_part[0, 0], ("x", "y", "c"))

    (dx3, dsh2b, dsc2b, dffn1), (dup1g, dup1v, ddown1, dfcw1, dfcb1) = _ffn_bwd(
        dx4, dz1, x3, f1, u1, h1, ffnn[1], sc2b, w_up[1], fconv_w[1], w_down[1], "l1")
    dx2, dpool_w, dg1b, dpscale, dsh1b, dsc1b, dmix1, dz0, dg2 = _pool_bwd(
        dx3, mixed, x2, g1b, pool_scale, mixn[1], sc1b, sh1b, pool_w, z0, g2, name="pool_bwd")

    def up_shards(dg, dv):
        return jnp.concatenate([dg, dv], axis=1).reshape(D, NDEV, -1).transpose(1, 0, 2)

    s_pool = dpool_w.astype(BF16).reshape(4, NDEV, PG // NDEV, PG).transpose(1, 0, 2, 3)
    h_g1 = _exchange_start([s_pool, up_shards(dup1g, dup1v), ddown1.reshape(NDEV, DFF // NDEV, D)], scatter=True,
                           name="grads1_start")

    (dx1, dsh2, dsc2, dffn0, dy0, dg1), (dup0g, dup0v, ddown0, dfcw0, dfcb0) = _ffn_bwd(
        dx2, dz0, x1, f0, u0, h0, ffnn[0], sc2, w_up[0], fconv_w[0] + h_g1["token"][0, 0], w_down[0], "l0",
        gate_y=y0, gate_g=g1)
    h_g0 = _exchange_start([up_shards(dup0g, dup0v), ddown0.reshape(NDEV, DFF // NDEV, D)], scatter=True,
                           name="grads0_start")
    dcat = _mm_w(dy0, w_out, tb=True, name="out_proj_dx", tm=512)
    d_wout = _mm_tn((cat, dy0), name="out_proj_dw")
    dp_conv, dconv_w = _conv_gate_bwd(dcat, p_lat, conv_w + h_g0["token"][0, 0], name="conv_gate_bwd")
    dq_r, dk_all, dv_all = _attn_bwd(q_r, k_all, v_all, dcat, o_attn, lse, name="attn_bwd")
    dp_qkv, dqg_l, dkg_l = _qkv_bwd(p_lat, dq_r, dk_all, dv_all, qg, kg, cs_t, sn_t, has_q=True, kv_col=1,
                                    kv_row_off=lc, name="qkv_bwd")
    dp_ctx, _zero_qg, dkg_c = _qkv_bwd(p_ctx, None, dk_all, dv_all, qg, kg, None, None, has_q=False, kv_col=0,
                                       kv_row_off=0, name="qkv_bwd_ctx")
    da_lat = _mm_w([dp_qkv, dp_conv], w_in, tb=True, name="in_proj_dx", tm=512)
    da_ctx = _mm(dp_ctx, w_in[:, :D], tb=True, name="in_proj_dx_ctx", tm=256, tn=512, tk=1024)
    d_win_qkv = _mm_tn([(a_lat, dp_qkv), (a_ctx, dp_ctx)], name="in_proj_dw_qkv")
    d_win_conv = _mm_tn((a_lat, dp_conv), name="in_proj_dw_conv")
    d_win = jnp.concatenate([d_win_qkv, d_win_conv], axis=1)
    grad_x, dsh1, dsc1, dmix0 = _norm_mod_bwd(da_lat, xs, mixn[0], sc1, dres=dx1, name="mix0_norm_bwd")
    _dctx, dcsh1, dcsc1, dmix0c = _norm_mod_bwd(da_ctx, ctxs, mixn[0], csc1, name="mix0_norm_bwd_ctx")

    z1k = jnp.zeros((1, D), F32)
    pack = jnp.concatenate(
        [v.reshape(-1) for v in (dsh1, dsc1, dg1, dsh2, dsc2, dg2, dsh1b, dsc1b, dg1b, dsh2b, dsc2b, dg2b,
                                 dcsh1, dcsc1, z1k, z1k, z1k, z1k,
                                 dmix0, dmix1, dmix0c, z1k, dffn0, dffn1, dqg_l, dkg_l + dkg_c,
                                 dfcb0, dfcb1, dconv_w, dfcw0, dfcw1, dpscale)])
    npack = pack.shape[0]
    pack = jnp.pad(pack, (0, (-npack) % 1024)).reshape(-1, 128)
    (g_pack,) = _all_gather([pack], name="gather_small_grads")
    gp = g_pack.reshape(NDEV, -1)
    off = [0]

    def take(size):
        seg = gp[:, off[0]:off[0] + size]
        off[0] += size
        return seg

    dmod_all = take(12 * D).reshape(NDEV, 2, 6 * D)
    dmodc_all = take(6 * D).reshape(NDEV, 1, 6 * D)
    dmix_all = take(4 * D).reshape(NDEV, 2, 2, D)
    dffn_all = take(2 * D).reshape(NDEV, 2, D)
    dqg_all = take(HD).reshape(NDEV, 1, HD)
    dkg_all = take(HD).reshape(NDEV, 1, HD)
    dfcb_all = take(2 * DFF).reshape(NDEV, 2, DFF)
    dconvw_all = take(3 * CW).reshape(NDEV, 3, CW)
    dfcw_all = take(6 * DFF).reshape(NDEV, 2, 3, DFF)
    dpscale_all = take(D).reshape(NDEV, D)

    dmodc_sum = dmodc_all[0]
    for dev in range(1, NDEV):
        dmodc_sum = dmodc_sum + dmodc_all[dev]
    my_cols = lambda a: lax.dynamic_slice_in_dim(a, me * acols, acols, axis=a.ndim - 1)
    rows0 = jnp.concatenate([my_cols(dmod_all[:, 0]), my_cols(dmodc_sum), jnp.zeros((7, acols), F32)], axis=0)
    rows1 = jnp.concatenate([my_cols(dmod_all[:, 1]), jnp.zeros((8, acols), F32)], axis=0)
    d_ada = jnp.stack([_mm(mraw, rows, ta=True, silu_a=True, name=f"ada_dw_{l}", tm=512, tn=256, tk=16)
                       for l, rows in enumerate((rows0, rows1))])
    dscc_part = _mm(rows0, ada_w[0], tb=True, name="ada_dcctx", tm=16, tn=512, tk=256)
    (g_dscc,) = _all_gather([dscc_part[8:16]], name="gather_dcctx")

    attn_shards = [d_win.reshape(D, NDEV, -1).transpose(1, 0, 2), d_wout.reshape(NDEV, D // NDEV, D)]
    attn_shards, g_dscc = lax.optimization_barrier((attn_shards, g_dscc))
    h_ga = _exchange_start(attn_shards, scatter=True, name="grads_attn_start")
    dmod_all = dmod_all + h_ga["token"][0, 0]

    outs = {}

    def put(nm, res):
        outs["grad_" + nm], outs["delta_" + nm], outs["new_m_" + nm], outs["new_v_" + nm] = res

    dmodc_pad = jnp.concatenate([dmodc_all, jnp.zeros_like(dmodc_all)], axis=1)
    put("ada_b", _adamw_nd(jnp.concatenate([dmod_all, dmodc_pad], axis=0), ada_b, m_ada_b, v_ada_b, name="adam_ada_b"))
    put("mix_norm", _adamw_nd(jnp.concatenate([dmix_all[:, 0], dmix_all[:, 1]], axis=0), mix_norm, m_mix_norm,
                              v_mix_norm, name="adam_mix_norm"))
    put("ffn_norm", _adamw_nd(dffn_all, ffn_norm, m_ffn_norm, v_ffn_norm, name="adam_ffn_norm"))
    put("even_q_gain", _adamw_nd(dqg_all, even_q_gain, m_even_q_gain, v_even_q_gain, name="adam_q_gain"))
    put("even_k_gain", _adamw_nd(dkg_all, even_k_gain, m_even_k_gain, v_even_k_gain, name="adam_k_gain"))
    put("ffn_conv_b", _adamw_nd(dfcb_all, ffn_conv_b, m_ffn_conv_b, v_ffn_conv_b, name="adam_ffn_conv_b"))
    my_convw = lax.dynamic_slice_in_dim(dconvw_all, me * ecw, ecw, axis=2)[:, None]
    put("even_conv_w", _adamw_nd(my_convw, even_conv_w, m_even_conv_w, v_even_conv_w, name="adam_even_conv_w"))
    my_fcw = lax.dynamic_slice_in_dim(dfcw_all, me * fcw, fcw, axis=3)
    put("ffn_conv_w", _adamw_nd(my_fcw, ffn_conv_w, m_ffn_conv_w, v_ffn_conv_w, name="adam_ffn_conv_w"))
    my_ps = lax.dynamic_slice_in_dim(dpscale_all, me * (D // NDEV), D // NDEV, axis=1)[:, None]
    put("odd_pool_scale", _adamw_nd(my_ps, odd_pool_scale, m_odd_pool_scale, v_odd_pool_scale, name="adam_pool_scale"))

    put("ada_w", _adamw_nd(d_ada[None], ada_w, m_ada_w, v_ada_w, name="adam_ada_w"))
    put("c_ctx", _adamw_nd(g_dscc[:, 0:1, :].reshape(NDEV, D), c_ctx, m_c_ctx, v_c_ctx, name="adam_c_ctx",
                           silu_grad_of=c_ctx))

    r_pool, r_up1, r_down1 = _exchange_wait(h_g1, outs["grad_ada_b"], name="grads1_wait")
    r_up0, r_down0 = _exchange_wait(h_g0, outs["grad_mix_norm"], name="grads0_wait")
    r_win, r_wout = _exchange_wait(h_ga, outs["grad_c_ctx"], name="grads_attn_wait")
    put("even_w_in", _adamw_nd(r_win[:, None], even_w_in, m_even_w_in, v_even_w_in, name="adam_w_in"))
    put("even_w_out", _adamw_nd(r_wout[:, None], even_w_out, m_even_w_out, v_even_w_out, name="adam_w_out"))
    put("odd_pool_w", _adamw_nd(r_pool[:, None], odd_pool_w, m_odd_pool_w, v_odd_pool_w, name="adam_pool_w"))
    put("ffn_w_up", _adamw_nd([r_up0, r_up1], ffn_w_up, m_ffn_w_up, v_ffn_w_up, name="adam_w_up"))
    put("ffn_w_down", _adamw_nd([r_down0, r_down1], ffn_w_down, m_ffn_w_down, v_ffn_w_down, name="adam_w_down"))

    names = ["c_ctx", "ada_w", "ada_b", "mix_norm", "ffn_norm", "even_w_in", "even_q_gain", "even_k_gain",
             "even_conv_w", "even_w_out", "odd_pool_w", "odd_pool_scale", "ffn_w_up", "ffn_conv_w", "ffn_conv_b",
             "ffn_w_down"]
    result = [loss, grad_x[None]]
    for kind in ("grad_", "delta_", "new_m_", "new_v_"):
        result += [outs[kind + nm] for nm in names]
    return tuple(result)
```

```python
import functools
import math

import jax
import jax.numpy as jnp
from jax import lax
from jax.experimental import pallas as pl
from jax.experimental.pallas import tpu as pltpu

F32 = jnp.float32
BF16 = jnp.bfloat16

D = 1024
HD = 128
NQ = 4
NKV = 2
AW = NQ * HD
CW = D - AW
DFF = 2816
GRID_W = 64
ROPE_THETA = 10000.0
POOL_WINDOWS = (2, 4, 8, 16)
PG = D // 4
EPS = 1e-6
NDEV = 8
HALO = 8
MESH = pl.DeviceIdType.MESH

ADAM_LR = 0.001
ADAM_B1 = 0.9
ADAM_B2 = 0.999
ADAM_EPS = 1e-08
ADAM_WD = 0.01
ADAM_STEP = 10


def _pick(dim, prefs):
    for p in prefs:
        if dim % p == 0:
            return p
    return dim


def _params(*sem):
    return pltpu.CompilerParams(dimension_semantics=sem)


_NT = (((1,), (1,)), ((), ()))
_TN = (((0,), (0,)), ((), ()))
_SCALE = HD ** -0.5
_QSCALE = _SCALE * math.log2(math.e)
_LN2 = math.log(2.0)


def _mm(a_list, b, *, name, ta=False, tb=False, out_dtype=F32, silu_a=False, bias=None, tm=None, tn=None, tk=None):
    if not isinstance(a_list, (list, tuple)):
        a_list = [a_list]
    na = len(a_list)
    assert not (ta and na > 1)
    if ta:
        kdim, m = a_list[0].shape
        ks = [kdim]
    else:
        m = a_list[0].shape[0]
        ks = [a.shape[1] for a in a_list]
        kdim = sum(ks)
    n = b.shape[0] if tb else b.shape[1]
    assert (b.shape[1] if tb else b.shape[0]) == kdim
    kunit = math.gcd(*ks) if na > 1 else kdim
    tm = min(tm, m) if tm else _pick(m, (512, 256, 128, 64, 32, 16, 8))
    tn = min(tn, n) if tn else _pick(n, (512, 256, 128))
    tk = min(tk, kunit) if tk else _pick(kunit, (1024, 768, 512, 256, 128))
    assert m % tm == 0 and n % tn == 0 and all(k % tk == 0 for k in ks)
    nks = [k // tk for k in ks]
    starts = [sum(nks[:i]) for i in range(na)]
    nk = sum(nks)
    has_bias = bias is not None

    def body(*refs):
        a_refs = refs[:na]
        b_ref = refs[na]
        bias_ref = refs[na + 1] if has_bias else None
        o_ref = refs[na + 1 + has_bias]
        acc = refs[-1]
        k = pl.program_id(2)

        @pl.when(k == 0)
        def _():
            acc[...] = jnp.zeros_like(acc)

        bv = b_ref[...].astype(BF16)
        dn = (((0 if ta else 1,), (1 if tb else 0,)), ((), ()))
        for idx in range(na):
            def step(idx=idx):
                av = a_refs[idx][...]
                if silu_a:
                    av = av * jax.nn.sigmoid(av)
                acc[...] += lax.dot_general(av.astype(BF16), bv, dn, preferred_element_type=F32)
            if na == 1:
                step()
            else:
                pl.when((k >= starts[idx]) & (k < starts[idx] + nks[idx]))(step)

        @pl.when(k == nk - 1)
        def _():
            r = acc[...]
            if has_bias:
                r = r + bias_ref[...]
            o_ref[...] = r.astype(o_ref.dtype)

    in_specs = []
    for idx in range(na):
        if ta:
            in_specs.append(pl.BlockSpec((tk, tm), lambda i, j, k: (k, i)))
        else:
            lo, cnt = starts[idx], nks[idx]
            in_specs.append(pl.BlockSpec((tm, tk), lambda i, j, k, lo=lo, cnt=cnt: (i, jnp.clip(k - lo, 0, cnt - 1))))
    if tb:
        in_specs.append(pl.BlockSpec((tn, tk), lambda i, j, k: (j, k)))
    else:
        in_specs.append(pl.BlockSpec((tk, tn), lambda i, j, k: (k, j)))
    args = list(a_list) + [b]
    if has_bias:
        in_specs.append(pl.BlockSpec((1, tn), lambda i, j, k: (0, j)))
        args.append(bias)
    return pl.pallas_call(
        body, grid=(m // tm, n // tn, nk), in_specs=in_specs,
        out_specs=pl.BlockSpec((tm, tn), lambda i, j, k: (i, j)),
        out_shape=jax.ShapeDtypeStruct((m, n), out_dtype),
        scratch_shapes=[pltpu.VMEM((tm, tn), F32)], name=name,
        compiler_params=_params("parallel", "parallel", "arbitrary"))(*args)


def _mm_w(a_list, w, *, name, tb=False, tm=256, out_dtype=F32):
    if not isinstance(a_list, (list, tuple)):
        a_list = [a_list]
    na = len(a_list)
    m = a_list[0].shape[0]
    ks = [a.shape[1] for a in a_list]
    offs = [sum(ks[:i]) for i in range(na)]
    n = w.shape[0] if tb else w.shape[1]
    assert (w.shape[1] if tb else w.shape[0]) == sum(ks)
    tm = min(tm, m)
    assert m % tm == 0

    def body(*refs):
        a_refs, w_ref, o_ref = refs[:na], refs[na], refs[na + 1]
        acc = None
        for idx in range(na):
            av = a_refs[idx][...].astype(BF16)
            if tb:
                part = lax.dot_general(av, w_ref[:, offs[idx]:offs[idx] + ks[idx]], _NT, preferred_element_type=F32)
            else:
                part = jnp.dot(av, w_ref[offs[idx]:offs[idx] + ks[idx], :], preferred_element_type=F32)
            acc = part if acc is None else acc + part
        o_ref[...] = acc.astype(o_ref.dtype)

    in_specs = [pl.BlockSpec((tm, k), lambda i: (i, 0)) for k in ks] + [pl.BlockSpec(w.shape, lambda i: (0, 0))]
    return pl.pallas_call(
        body, grid=(m // tm,), in_specs=in_specs, out_specs=pl.BlockSpec((tm, n), lambda i: (i, 0)),
        out_shape=jax.ShapeDtypeStruct((m, n), out_dtype), name=name, compiler_params=_params("parallel"))(*a_list, w)


def _mm_tn(pairs, *, name, tk=1024, out_dtype=BF16):
    if not isinstance(pairs, list):
        pairs = [pairs]
    m, n = pairs[0][0].shape[1], pairs[0][1].shape[1]
    tks = [min(tk, a.shape[0]) for a, _ in pairs]
    nks = [a.shape[0] // t for (a, _), t in zip(pairs, tks)]
    assert all(a.shape[0] == b.shape[0] and a.shape[0] % t == 0 for (a, b), t in zip(pairs, tks))
    starts = [sum(nks[:i]) for i in range(len(pairs))]
    nk = sum(nks)

    def body(*refs):
        o_ref, acc = refs[-2], refs[-1]
        k = pl.program_id(0)

        @pl.when(k == 0)
        def _():
            acc[...] = jnp.zeros_like(acc)

        for idx in range(len(pairs)):
            a_ref, b_ref = refs[2 * idx], refs[2 * idx + 1]

            def step(a_ref=a_ref, b_ref=b_ref):
                acc[...] += lax.dot_general(a_ref[...], b_ref[...], _TN, preferred_element_type=F32)

            if len(pairs) == 1:
                step()
            else:
                pl.when((k >= starts[idx]) & (k < starts[idx] + nks[idx]))(step)

        @pl.when(k == nk - 1)
        def _():
            o_ref[...] = acc[...].astype(o_ref.dtype)

    in_specs, args = [], []
    for (a, b), t, lo, cnt in zip(pairs, tks, starts, nks):
        idx_map = lambda k, lo=lo, cnt=cnt: (jnp.clip(k - lo, 0, cnt - 1), 0)
        in_specs += [pl.BlockSpec((t, m), idx_map), pl.BlockSpec((t, n), idx_map)]
        args += [a, b]
    return pl.pallas_call(
        body, grid=(nk,), in_specs=in_specs, out_specs=pl.BlockSpec((m, n), lambda k: (0, 0)),
        out_shape=jax.ShapeDtypeStruct((m, n), out_dtype), scratch_shapes=[pltpu.VMEM((m, n), F32)], name=name,
        compiler_params=_params("arbitrary"))(*args)


def _vec(d, col=None):
    if col is None:
        return pl.BlockSpec((1, d), lambda i, *_: (0, 0))
    return pl.BlockSpec((1, d), col)


def _halo_specs(tm, width, nrows, colblk=0, row_off=0):
    r = tm // HALO
    off = row_off // HALO
    last = nrows // HALO - 1
    prev = pl.BlockSpec((HALO, width), lambda i, *_: (off + jnp.maximum(i * r - 1, 0), colblk))
    nxt = pl.BlockSpec((HALO, width), lambda i, *_: (off + jnp.minimum((i + 1) * r, last), colblk))
    return prev, nxt


def _ext(prev_ref, main_ref, next_ref, i, ni):
    p = jnp.where(i > 0, prev_ref[...], 0.0)
    n = jnp.where(i < ni - 1, next_ref[...], 0.0)
    return jnp.concatenate([p, main_ref[...], n], axis=0)


def _sh(ext, k, tm):
    if k == 0:
        return ext[HALO:HALO + tm]
    rows = ext.shape[0]
    return pltpu.roll(ext, (-k) % rows, axis=0)[HALO:HALO + tm]


def _roll_rows(v, k):
    rows = v.shape[0]
    return pltpu.roll(v, (-k) % rows, axis=0) if k % rows else v


def _conv3(ext, w_ref, tm):
    return _sh(ext, -1, tm) * w_ref[0:1, :] + _sh(ext, 0, tm) * w_ref[1:2, :] + _sh(ext, 1, tm) * w_ref[2:3, :]


def _colsum(v):
    return jnp.sum(v, axis=0, keepdims=True)


def _acc_out(ref, i, val):
    @pl.when(i == 0)
    def _():
        ref[...] = jnp.zeros_like(ref)

    ref[...] += val


def _sigmoid(v):
    return jax.nn.sigmoid(v)


def _norm_mod(x, gain, sc, sh, *, name, y=None, g=None, ymul=None, tm=512):
    n, d = x.shape
    tm = min(tm, n)
    has_res = y is not None
    has_mul = ymul is not None

    def body(*refs):
        it = iter(refs)
        x_ref = next(it)
        y_ref = next(it) if has_res else None
        g_ref = next(it) if has_res else None
        m_ref = next(it) if has_mul else None
        gain_ref, sc_ref, sh_ref = next(it), next(it), next(it)
        xo_ref = next(it) if has_res else None
        a_ref = next(it)
        xv = x_ref[...]
        if has_res:
            yv = y_ref[...]
            if has_mul:
                yv = yv * m_ref[...]
            xv = xv + g_ref[...] * yv
            xo_ref[...] = xv
        r = lax.rsqrt(jnp.mean(xv * xv, axis=-1, keepdims=True) + EPS)
        nrm = (xv * r) * gain_ref[...]
        a_ref[...] = (nrm * (1.0 + sc_ref[...]) + sh_ref[...]).astype(BF16)

    row = pl.BlockSpec((tm, d), lambda i: (i, 0))
    in_specs, args = [row], [x]
    if has_res:
        in_specs += [row, _vec(d)]
        args += [y, g]
    if has_mul:
        in_specs.append(_vec(d))
        args.append(ymul)
    in_specs += [_vec(d)] * 3
    args += [gain, sc, sh]
    out_specs, out_shape = [], []
    if has_res:
        out_specs.append(row)
        out_shape.append(jax.ShapeDtypeStruct((n, d), F32))
    out_specs.append(row)
    out_shape.append(jax.ShapeDtypeStruct((n, d), BF16))
    res = pl.pallas_call(body, grid=(n // tm,), in_specs=in_specs, out_specs=out_specs, out_shape=out_shape,
                         name=name, compiler_params=_params("parallel"))(*args)
    return res if has_res else res[0]


def _norm_mod_bwd(da, x, gain, sc, *, name, dres=None, gate_y=None, gate_g=None, tm=512):
    n, d = x.shape
    tm = min(tm, n)
    has_res = dres is not None
    has_gate = gate_y is not None

    def body(*refs):
        it = iter(refs)
        da_ref, x_ref = next(it), next(it)
        r_ref = next(it) if has_res else None
        y_ref = next(it) if has_gate else None
        g_ref = next(it) if has_gate else None
        gain_ref, sc_ref = next(it), next(it)
        dx_ref, dsh_ref, dsc_ref, dgn_ref = next(it), next(it), next(it), next(it)
        dy_ref = next(it) if has_gate else None
        dg_ref = next(it) if has_gate else None
        i = pl.program_id(0)
        xv = x_ref[...]
        dav = da_ref[...]
        r = lax.rsqrt(jnp.mean(xv * xv, axis=-1, keepdims=True) + EPS)
        xh = xv * r
        nrm = xh * gain_ref[...]
        dn = dav * (1.0 + sc_ref[...])
        dxh = dn * gain_ref[...]
        dx = r * (dxh - xh * jnp.mean(dxh * xh, axis=-1, keepdims=True))
        if has_res:
            dx = dx + r_ref[...]
        dx_ref[...] = dx
        _acc_out(dsh_ref, i, _colsum(dav))
        _acc_out(dsc_ref, i, _colsum(dav * nrm))
        _acc_out(dgn_ref, i, _colsum(dn * xh))
        if has_gate:
            dy_ref[...] = (dx * g_ref[...]).astype(BF16)
            _acc_out(dg_ref, i, _colsum(dx * y_ref[...]))

    row = pl.BlockSpec((tm, d), lambda i: (i, 0))
    in_specs, args = [row, row], [da, x]
    if has_res:
        in_specs.append(row)
        args.append(dres)
    if has_gate:
        in_specs += [row, _vec(d)]
        args += [gate_y, gate_g]
    in_specs += [_vec(d)] * 2
    args += [gain, sc]
    vec_shape = jax.ShapeDtypeStruct((1, d), F32)
    out_specs = [row, _vec(d), _vec(d), _vec(d)]
    out_shape = [jax.ShapeDtypeStruct((n, d), F32), vec_shape, vec_shape, vec_shape]
    if has_gate:
        out_specs += [row, _vec(d)]
        out_shape += [jax.ShapeDtypeStruct((n, d), BF16), vec_shape]
    return pl.pallas_call(
        body, grid=(n // tm,), in_specs=in_specs, out_specs=out_specs, out_shape=out_shape,
        name=name, compiler_params=_params("arbitrary"))(*args)


def _loss_head(x, z, g, tgt, *, name, tm=512):
    n, d = x.shape
    tm = min(tm, n)

    def body(x_ref, z_ref, g_ref, t_ref, dx_ref, loss_ref, dz_ref, dg_ref):
        i = pl.program_id(0)
        zv = z_ref[...]
        diff = (x_ref[...] + g_ref[...] * zv) - t_ref[...]
        dx = diff * (1.0 / d)
        dx_ref[...] = dx
        part = 0.5 * jnp.sum(jnp.mean(diff * diff, axis=-1, keepdims=True), axis=0, keepdims=True)
        _acc_out(loss_ref, i, jnp.broadcast_to(part, (1, 128)))
        dz_ref[...] = (dx * g_ref[...]).astype(BF16)
        _acc_out(dg_ref, i, _colsum(dx * zv))

    row = pl.BlockSpec((tm, d), lambda i: (i, 0))
    return pl.pallas_call(
        body, grid=(n // tm,), in_specs=[row, row, _vec(d), row], out_specs=[row, _vec(128), row, _vec(d)],
        out_shape=[jax.ShapeDtypeStruct((n, d), F32), jax.ShapeDtypeStruct((1, 128), F32),
                   jax.ShapeDtypeStruct((n, d), BF16), jax.ShapeDtypeStruct((1, d), F32)],
        name=name, compiler_params=_params("arbitrary"))(x, z, g, tgt)


def _ffn_up_glu(f, w_up, cw, cb, *, name, tm=256, tc=256):
    n, d = f.shape
    tm = min(tm, n)
    ni = n // tm
    nc = DFF // tc
    halo = 16
    rows = tm + 2 * halo
    r = tm // halo
    last = n // halo - 1

    def body(f_ref, fp_ref, fn_ref, w_ref, cw_ref, cb_ref, u_ref, gc_ref, h_ref):
        i = pl.program_id(0)
        a = f_ref[...]
        aext = jnp.concatenate([jnp.where(i > 0, fp_ref[...], jnp.zeros_like(fp_ref[...])), a,
                                jnp.where(i < ni - 1, fn_ref[...], jnp.zeros_like(fn_ref[...]))], axis=0)
        for j in range(nc):
            cols = slice(j * tc, (j + 1) * tc)
            vcols = slice(DFF + j * tc, DFF + (j + 1) * tc)
            gext = jnp.dot(aext, w_ref[:, cols], preferred_element_type=F32)
            val = jnp.dot(a, w_ref[:, vcols], preferred_element_type=F32)
            gate = gext[halo:halo + tm]
            gc = (pltpu.roll(gext, 1, axis=0)[halo:halo + tm] * cw_ref[0:1, cols] + gate * cw_ref[1:2, cols]
                  + pltpu.roll(gext, rows - 1, axis=0)[halo:halo + tm] * cw_ref[2:3, cols]) + cb_ref[:, cols]
            u_ref[:, cols] = gate
            u_ref[:, vcols] = val
            gc_ref[:, cols] = gc
            h_ref[:, cols] = (gc * _sigmoid(gc) * val).astype(BF16)

    return pl.pallas_call(
        body, grid=(ni,),
        in_specs=[pl.BlockSpec((tm, d), lambda i: (i, 0)),
                  pl.BlockSpec((halo, d), lambda i: (jnp.maximum(i * r - 1, 0), 0)),
                  pl.BlockSpec((halo, d), lambda i: (jnp.minimum((i + 1) * r, last), 0)),
                  pl.BlockSpec(w_up.shape, lambda i: (0, 0)), pl.BlockSpec((3, DFF), lambda i: (0, 0)),
                  pl.BlockSpec((1, DFF), lambda i: (0, 0))],
        out_specs=[pl.BlockSpec((tm, 2 * DFF), lambda i: (i, 0)), pl.BlockSpec((tm, DFF), lambda i: (i, 0)),
                   pl.BlockSpec((tm, DFF), lambda i: (i, 0))],
        out_shape=[jax.ShapeDtypeStruct((n, 2 * DFF), F32), jax.ShapeDtypeStruct((n, DFF), F32),
                   jax.ShapeDtypeStruct((n, DFF), BF16)], name=name,
        compiler_params=_params("parallel"))(f, f, f, w_up, cw, cb)


def _ffn_down_glu_bwd(dz, w_down, u, gc, cw, *, name, tm=256, tc=256):
    n, d = dz.shape
    tm = min(tm, n)
    ni = n // tm
    nc = DFF // tc
    rows = tm + 2 * HALO

    def body(z_ref, zp_ref, zn_ref, w_ref, u_ref, vp_ref, vn_ref, c_ref, cp_ref, cn_ref, cw_ref,
             dg_ref, dv_ref, dcw_ref, dcb_ref):
        i = pl.program_id(0)

        @pl.when(i == 0)
        def _():
            dcw_ref[...] = jnp.zeros_like(dcw_ref)
            dcb_ref[...] = jnp.zeros_like(dcb_ref)

        zext = jnp.concatenate([jnp.where(i > 0, zp_ref[...], jnp.zeros_like(zp_ref[...])), z_ref[...],
                                jnp.where(i < ni - 1, zn_ref[...], jnp.zeros_like(zn_ref[...]))], axis=0)
        for j in range(nc):
            cols = slice(j * tc, (j + 1) * tc)
            vcols = slice(DFF + j * tc, DFF + (j + 1) * tc)
            dh = lax.dot_general(zext, w_ref[cols, :], _NT, preferred_element_type=F32)[HALO:HALO + rows]
            gcx = jnp.concatenate([cp_ref[:, cols], c_ref[:, cols], cn_ref[:, cols]], axis=0)
            vext = jnp.concatenate([vp_ref[:, cols], u_ref[:, vcols], vn_ref[:, cols]], axis=0)
            sg = _sigmoid(gcx)
            dgc = dh * vext * (sg * (1.0 + gcx * (1.0 - sg)))
            dv_ref[:, cols] = (dh[HALO:HALO + tm] * (gcx[HALO:HALO + tm] * sg[HALO:HALO + tm])).astype(BF16)
            d_next = pltpu.roll(dgc, rows - 1, axis=0)[HALO:HALO + tm]
            d_prev = pltpu.roll(dgc, 1, axis=0)[HALO:HALO + tm]
            d_here = dgc[HALO:HALO + tm]
            dg_ref[:, cols] = (d_next * cw_ref[0:1, cols] + d_here * cw_ref[1:2, cols]
                               + d_prev * cw_ref[2:3, cols]).astype(BF16)
            gate = u_ref[:, cols]
            dcw_ref[:, cols] += jnp.concatenate([_colsum(d_next * gate), _colsum(d_here * gate),
                                                 _colsum(d_prev * gate)], axis=0)
            dcb_ref[:, cols] += _colsum(d_here)

    def trio(width, halo, tile_width=None, colblk=0):
        r, last = tm // halo, n // halo - 1
        return [pl.BlockSpec((tm, tile_width or width), lambda i: (i, 0)),
                pl.BlockSpec((halo, width), lambda i: (jnp.maximum(i * r - 1, 0), colblk)),
                pl.BlockSpec((halo, width), lambda i: (jnp.minimum((i + 1) * r, last), colblk))]

    whole = lambda shape: pl.BlockSpec(shape, lambda i: (0, 0))
    return pl.pallas_call(
        body, grid=(ni,),
        in_specs=(trio(d, 16) + [whole(w_down.shape)] + trio(DFF, HALO, tile_width=2 * DFF, colblk=1)
                  + trio(DFF, HALO) + [whole((3, DFF))]),
        out_specs=[pl.BlockSpec((tm, DFF), lambda i: (i, 0)), pl.BlockSpec((tm, DFF), lambda i: (i, 0)),
                   whole((3, DFF)), whole((1, DFF))],
        out_shape=[jax.ShapeDtypeStruct((n, DFF), BF16), jax.ShapeDtypeStruct((n, DFF), BF16),
                   jax.ShapeDtypeStruct((3, DFF), F32), jax.ShapeDtypeStruct((1, DFF), F32)],
        name=name, compiler_params=_params("arbitrary"))(dz, dz, dz, w_down, u, u, u, gc, gc, gc, cw)


def _rope_tables(n):
    rows = n // GRID_W
    axis_dim = HD // 2
    inv_freq = jnp.power(ROPE_THETA, -jnp.arange(0, axis_dim, 2, dtype=F32) / axis_dim)
    ar = jnp.arange(rows, dtype=F32)[:, None] * inv_freq
    ac = jnp.arange(GRID_W, dtype=F32)[:, None] * inv_freq
    by_row = lambda a: jnp.repeat(a, GRID_W, axis=0)
    by_col = lambda a: jnp.tile(a, (rows, 1))
    cr, sr, cc, sc = by_row(jnp.cos(ar)), by_row(jnp.sin(ar)), by_col(jnp.cos(ac)), by_col(jnp.sin(ac))
    return jnp.concatenate([cr, cr, cc, cc], axis=1), jnp.concatenate([-sr, sr, -sc, sc], axis=1)


def _partner(v):
    lane = lax.broadcasted_iota(jnp.int32, v.shape, 1)
    return jnp.where((lane % 64) < 32, pltpu.roll(v, HD - 32, axis=1), pltpu.roll(v, 32, axis=1))


def _qkv_prep(p, q_gain, k_gain, cs, sn, *, name, has_q, kv_col, kv_rows=None, kv_row_off=0, kv_into=None, tm=256):
    n = p.shape[0]
    rope = cs is not None
    kv_rows = kv_rows or n
    rb = kv_row_off // tm

    def body(*refs):
        it = iter(refs)
        q_ref = next(it) if has_q else None
        kv_ref = next(it)
        qg_ref, kg_ref = next(it), next(it)
        cs_ref = next(it) if rope else None
        sn_ref = next(it) if rope else None
        if kv_into is not None:
            next(it), next(it)
        qo_ref = next(it) if has_q else None
        ko_ref, vo_ref = next(it), next(it)

        def norm_rope(xh, gain, mul=None):
            r = lax.rsqrt(jnp.mean(xh * xh, axis=-1, keepdims=True) + EPS)
            xn = (xh * r) * gain
            if rope:
                xn = xn * cs_ref[...] + _partner(xn) * sn_ref[...]
            if mul is not None:
                xn = xn * mul
            return xn.astype(BF16)

        if has_q:
            for h in range(NQ):
                qo_ref[h] = norm_rope(q_ref[:, h * HD:(h + 1) * HD], qg_ref[...], _QSCALE)
        for h in range(NKV):
            ko_ref[h] = norm_rope(kv_ref[:, h * HD:(h + 1) * HD], kg_ref[...])
            vo_ref[h] = kv_ref[:, (NKV + h) * HD:(NKV + h + 1) * HD].astype(BF16)

    in_specs, args = [], []
    if has_q:
        in_specs.append(pl.BlockSpec((tm, AW), lambda i: (i, 0)))
        args.append(p)
    in_specs += [pl.BlockSpec((tm, 2 * NKV * HD), lambda i: (i, kv_col)), _vec(HD), _vec(HD)]
    args += [p, q_gain, k_gain]
    if rope:
        in_specs += [pl.BlockSpec((tm, HD), lambda i: (i, 0))] * 2
        args += [cs, sn]
    out_specs, out_shape = [], []
    if has_q:
        out_specs.append(pl.BlockSpec((NQ, tm, HD), lambda i: (0, i, 0)))
        out_shape.append(jax.ShapeDtypeStruct((NQ, n, HD), BF16))
    out_specs += [pl.BlockSpec((NKV, tm, HD), lambda i: (0, rb + i, 0))] * 2
    out_shape += [jax.ShapeDtypeStruct((NKV, kv_rows, HD), BF16)] * 2
    aliases = {}
    if kv_into is not None:
        aliases = {len(args): int(has_q), len(args) + 1: int(has_q) + 1}
        in_specs += [pl.BlockSpec(memory_space=pl.ANY)] * 2
        args += list(kv_into)
    return pl.pallas_call(body, grid=(n // tm,), in_specs=in_specs, out_specs=out_specs, out_shape=out_shape,
                          input_output_aliases=aliases, name=name, compiler_params=_params("parallel"))(*args)


def _qkv_bwd(p, dq, dk, dv, q_gain, k_gain, cs, sn, *, name, has_q, kv_col, kv_row_off, tm=256):
    n = p.shape[0]
    rope = cs is not None
    rb = kv_row_off // tm

    def body(*refs):
        it = iter(refs)
        q_ref = next(it) if has_q else None
        kv_ref = next(it)
        dq_ref = next(it) if has_q else None
        dk_ref, dv_ref = next(it), next(it)
        qg_ref, kg_ref = next(it), next(it)
        cs_ref = next(it) if rope else None
        sn_ref = next(it) if rope else None
        dp_ref, dqg_ref, dkg_ref = next(it), next(it), next(it)
        i = pl.program_id(0)

        def back(xh, dout, gain):
            if rope:
                dout = dout * cs_ref[...] + _partner(dout * sn_ref[...])
            r = lax.rsqrt(jnp.mean(xh * xh, axis=-1, keepdims=True) + EPS)
            xhat = xh * r
            dxh = dout * gain
            dx = r * (dxh - xhat * jnp.mean(dxh * xhat, axis=-1, keepdims=True))
            return dx, _colsum(dout * xhat)

        dqg = jnp.zeros((1, HD), F32)
        dkg = jnp.zeros((1, HD), F32)
        if has_q:
            for h in range(NQ):
                dx, dg = back(q_ref[:, h * HD:(h + 1) * HD], dq_ref[h], qg_ref[...])
                dp_ref[:, h * HD:(h + 1) * HD] = dx.astype(BF16)
                dqg = dqg + dg
        else:
            dp_ref[:, 0:AW] = jnp.zeros((tm, AW), BF16)
        for h in range(NKV):
            dx, dg = back(kv_ref[:, h * HD:(h + 1) * HD], dk_ref[h], kg_ref[...])
            dp_ref[:, AW + h * HD:AW + (h + 1) * HD] = dx.astype(BF16)
            dkg = dkg + dg
            dp_ref[:, AW + (NKV + h) * HD:AW + (NKV + h + 1) * HD] = dv_ref[h].astype(BF16)
        _acc_out(dqg_ref, i, dqg)
        _acc_out(dkg_ref, i, dkg)

    in_specs, args = [], []
    if has_q:
        in_specs.append(pl.BlockSpec((tm, AW), lambda i: (i, 0)))
        args.append(p)
    in_specs.append(pl.BlockSpec((tm, 2 * NKV * HD), lambda i: (i, kv_col)))
    args.append(p)
    if has_q:
        in_specs.append(pl.BlockSpec((NQ, tm, HD), lambda i: (0, i, 0)))
        args.append(dq)
    in_specs += [pl.BlockSpec((NKV, tm, HD), lambda i: (0, rb + i, 0))] * 2 + [_vec(HD), _vec(HD)]
    args += [dk, dv, q_gain, k_gain]
    if rope:
        in_specs += [pl.BlockSpec((tm, HD), lambda i: (i, 0))] * 2
        args += [cs, sn]
    return pl.pallas_call(
        body, grid=(n // tm,), in_specs=in_specs,
        out_specs=[pl.BlockSpec((tm, D), lambda i: (i, 0)), _vec(HD), _vec(HD)],
        out_shape=[jax.ShapeDtypeStruct((n, D), BF16), jax.ShapeDtypeStruct((1, HD), F32),
                   jax.ShapeDtypeStruct((1, HD), F32)],
        name=name, compiler_params=_params("arbitrary"))(*args)


def _conv_gate_fwd(p, o, conv_w, *, name, tm=256):
    n = p.shape[0]
    ni = n // tm

    def body(gb_ref, gc_ref, gcp_ref, gcn_ref, xi_ref, xip_ref, xin_ref, o_ref, w_ref, cat_ref):
        i = pl.program_id(0)
        hext = _ext(gcp_ref, gc_ref, gcn_ref, i, ni) * _ext(xip_ref, xi_ref, xin_ref, i, ni)
        cat_ref[:, 0:AW] = o_ref[...].astype(BF16)
        cat_ref[:, AW:D] = (gb_ref[...] * _conv3(hext, w_ref, tm)).astype(BF16)

    gcp, gcn = _halo_specs(tm, CW, n, colblk=3)
    xip, xin = _halo_specs(tm, CW, n, colblk=4)
    return pl.pallas_call(
        body, grid=(ni,),
        in_specs=[pl.BlockSpec((tm, CW), lambda i: (i, 2)), pl.BlockSpec((tm, CW), lambda i: (i, 3)), gcp, gcn,
                  pl.BlockSpec((tm, CW), lambda i: (i, 4)), xip, xin, pl.BlockSpec((tm, AW), lambda i: (i, 0)),
                  pl.BlockSpec((3, CW), lambda i: (0, 0))],
        out_specs=pl.BlockSpec((tm, D), lambda i: (i, 0)), out_shape=jax.ShapeDtypeStruct((n, D), BF16),
        name=name, compiler_params=_params("parallel"))(p, p, p, p, p, p, p, o, conv_w)


def _conv_gate_bwd(dcat, p, conv_w, *, name, tm=256):
    n = p.shape[0]
    ni = n // tm

    def body(dc_ref, dcp_ref, dcn_ref, gb_ref, gbp_ref, gbn_ref, gc_ref, gcp_ref, gcn_ref, xi_ref, xip_ref, xin_ref,
             w_ref, dp_ref, dw_ref):
        i = pl.program_id(0)
        gcext = _ext(gcp_ref, gc_ref, gcn_ref, i, ni)
        xiext = _ext(xip_ref, xi_ref, xin_ref, i, ni)
        hext = gcext * xiext
        dcv = _ext(dcp_ref, dc_ref, dcn_ref, i, ni) * _ext(gbp_ref, gb_ref, gbn_ref, i, ni)
        dp_ref[:, 0:CW] = (dc_ref[...] * _conv3(hext, w_ref, tm)).astype(BF16)
        dh = _sh(dcv, 1, tm) * w_ref[0:1, :] + _sh(dcv, 0, tm) * w_ref[1:2, :] + _sh(dcv, -1, tm) * w_ref[2:3, :]
        dp_ref[:, CW:2 * CW] = (dh * xi_ref[...]).astype(BF16)
        dp_ref[:, 2 * CW:3 * CW] = (dh * gc_ref[...]).astype(BF16)
        dcv_t = dcv[HALO:HALO + tm]
        dw = jnp.concatenate([_colsum(dcv_t * _sh(hext, -1, tm)), _colsum(dcv_t * _sh(hext, 0, tm)),
                              _colsum(dcv_t * _sh(hext, 1, tm))], axis=0)
        _acc_out(dw_ref, i, dw)

    def trio(colblk):
        prev, nxt = _halo_specs(tm, CW, n, colblk=colblk)
        return [pl.BlockSpec((tm, CW), lambda i: (i, colblk)), prev, nxt]

    return pl.pallas_call(
        body, grid=(ni,), in_specs=trio(1) + trio(2) + trio(3) + trio(4) + [pl.BlockSpec((3, CW), lambda i: (0, 0))],
        out_specs=[pl.BlockSpec((tm, 3 * CW), lambda i: (i, 0)), pl.BlockSpec((3, CW), lambda i: (0, 0))],
        out_shape=[jax.ShapeDtypeStruct((n, 3 * CW), BF16), jax.ShapeDtypeStruct((3, CW), F32)],
        name=name, compiler_params=_params("arbitrary"))(dcat, dcat, dcat, p, p, p, p, p, p, p, p, p, conv_w)


def _attn_fwd(q, k, v, *, name, bq=512, sub=256):
    n = q.shape[1]
    t = k.shape[1]
    bq = min(bq, n)
    sub = min(sub, 2 * bq)

    def body(q_ref, k_ref, v_ref, o_ref, lse_ref):
        q2 = q_ref[...].reshape(2 * bq, HD)
        outs, lses = [], []
        for r0 in range(0, 2 * bq, sub):
            s = lax.dot_general(q2[r0:r0 + sub], k_ref[0], _NT, preferred_element_type=F32)
            m = jnp.max(s, axis=-1, keepdims=True)
            pv = jnp.exp2(s - m)
            l = jnp.sum(pv, axis=-1, keepdims=True)
            outs.append(jnp.dot(pv.astype(BF16), v_ref[0], preferred_element_type=F32) / l)
            lses.append(m + jnp.log2(l))
        out = jnp.concatenate(outs, axis=0)
        o_ref[:, 0:HD] = out[0:bq]
        o_ref[:, HD:2 * HD] = out[bq:2 * bq]
        lse_ref[...] = jnp.concatenate(lses, axis=0).reshape(2, bq, 1)

    kspec = pl.BlockSpec((1, t, HD), lambda h, i: (h, 0, 0))
    return pl.pallas_call(
        body, grid=(NKV, n // bq),
        in_specs=[pl.BlockSpec((2, bq, HD), lambda h, i: (h, i, 0)), kspec, kspec],
        out_specs=[pl.BlockSpec((bq, 2 * HD), lambda h, i: (i, h)), pl.BlockSpec((2, bq, 1), lambda h, i: (h, i, 0))],
        out_shape=[jax.ShapeDtypeStruct((n, AW), F32), jax.ShapeDtypeStruct((NQ, n, 1), F32)],
        name=name, compiler_params=_params("parallel", "parallel"))(q, k, v)


def _attn_bwd(q, k, v, dcat, o, lse, *, name, bq=256):
    n = q.shape[1]
    t = k.shape[1]
    bq = min(bq, n)

    def body(q_ref, k_ref, v_ref, dc_ref, o_ref, lse_ref, dq_ref, dk_ref, dv_ref):
        @pl.when(pl.program_id(1) == 0)
        def _():
            dk_ref[...] = jnp.zeros_like(dk_ref)
            dv_ref[...] = jnp.zeros_like(dv_ref)

        q2 = q_ref[...].reshape(2 * bq, HD)
        do_f = jnp.concatenate([dc_ref[:, 0:HD], dc_ref[:, HD:2 * HD]], axis=0)
        o_f = jnp.concatenate([o_ref[:, 0:HD], o_ref[:, HD:2 * HD]], axis=0)
        delta = jnp.sum(do_f * o_f, axis=-1, keepdims=True)
        do2 = do_f.astype(BF16)
        s = lax.dot_general(q2, k_ref[0], _NT, preferred_element_type=F32)
        pv = jnp.exp2(s - lse_ref[...].reshape(2 * bq, 1))
        dp = lax.dot_general(do2, v_ref[0], _NT, preferred_element_type=F32)
        ds = (pv * (dp - delta)).astype(BF16)
        dq_ref[...] = (jnp.dot(ds, k_ref[0], preferred_element_type=F32) * _SCALE).reshape(2, bq, HD)
        dk_ref[0] += lax.dot_general(ds, q2, _TN, preferred_element_type=F32) * _LN2
        dv_ref[0] += lax.dot_general(pv.astype(BF16), do2, _TN, preferred_element_type=F32)

    qspec = pl.BlockSpec((2, bq, HD), lambda h, i: (h, i, 0))
    kspec = pl.BlockSpec((1, t, HD), lambda h, i: (h, 0, 0))
    sspec = pl.BlockSpec((2, bq, 1), lambda h, i: (h, i, 0))
    cspec = pl.BlockSpec((bq, 2 * HD), lambda h, i: (i, h))
    return pl.pallas_call(
        body, grid=(NKV, n // bq), in_specs=[qspec, kspec, kspec, cspec, cspec, sspec], out_specs=[qspec, kspec, kspec],
        out_shape=[jax.ShapeDtypeStruct((NQ, n, HD), F32), jax.ShapeDtypeStruct((NKV, t, HD), F32),
                   jax.ShapeDtypeStruct((NKV, t, HD), F32)],
        name=name, compiler_params=_params("parallel", "arbitrary"))(q, k, v, dcat, o, lse)


def _window_sums(ext, w):
    s, step = ext, 1
    while step < w:
        s = s + _roll_rows(s, step)
        step *= 2
    return s


def _pool_counts(i, tm, n, w, rows, first):
    t = i * tm - HALO + first + lax.broadcasted_iota(jnp.int32, (rows, 1), 0)
    lo = jnp.clip(t - w // 2, 0, n)
    hi = jnp.clip(t + w - w // 2, 0, n)
    return jnp.maximum(hi - lo, 1).astype(F32)


def _norm_mod_ext(xext, gain_ref, sc_ref, sh_ref, i, tm, n):
    rows = xext.shape[0]
    t = i * tm - HALO + lax.broadcasted_iota(jnp.int32, (rows, 1), 0)
    inside = (t >= 0) & (t < n)
    r = lax.rsqrt(jnp.mean(xext * xext, axis=-1, keepdims=True) + EPS)
    xh = xext * r
    a = (xh * gain_ref[...]) * (1.0 + sc_ref[...]) + sh_ref[...]
    return jnp.where(inside, a, 0.0), r, xh


def _pool_fwd(x, y, g, gain, sc, sh, pool_w, *, name, tm=256):
    n, d = x.shape
    ni = n // tm

    def body(x_ref, xp_ref, xn_ref, y_ref, yp_ref, yn_ref, g_ref, gain_ref, sc_ref, sh_ref, w_ref, xo_ref, o_ref):
        i = pl.program_id(0)
        xext = _ext(xp_ref, x_ref, xn_ref, i, ni) + g_ref[...] * _ext(yp_ref, y_ref, yn_ref, i, ni)
        xo_ref[...] = xext[HALO:HALO + tm]
        aext, _, _ = _norm_mod_ext(xext, gain_ref, sc_ref, sh_ref, i, tm, n)
        for gi, w in enumerate(POOL_WINDOWS):
            ag = aext[:, gi * PG:(gi + 1) * PG]
            mean = _sh(_window_sums(ag, w), -(w // 2), tm) / _pool_counts(i, tm, n, w, tm, HALO)
            pooled = mean - ag[HALO:HALO + tm]
            o_ref[:, gi * PG:(gi + 1) * PG] = jnp.dot(pooled.astype(BF16), w_ref[gi], preferred_element_type=F32)

    row = pl.BlockSpec((tm, d), lambda i: (i, 0))
    prev, nxt = _halo_specs(tm, d, n)
    return pl.pallas_call(
        body, grid=(ni,),
        in_specs=[row, prev, nxt, row, prev, nxt, _vec(d), _vec(d), _vec(d), _vec(d),
                  pl.BlockSpec((4, PG, PG), lambda i: (0, 0, 0))],
        out_specs=[row, row], out_shape=[jax.ShapeDtypeStruct((n, d), F32)] * 2,
        name=name, compiler_params=_params("parallel"))(x, x, x, y, y, y, g, gain, sc, sh, pool_w)


def _pool_bwd(dxo, mixed, x, g, scale, gain, sc, sh, pool_w, zprev, gprev, *, name, tm=256):
    n, d = x.shape
    ni = n // tm

    def body(dx_ref, dxp_ref, dxn_ref, mx_ref, x_ref, xp_ref, xn_ref, g_ref, s_ref, gain_ref, sc_ref, sh_ref, w_ref,
             zp_ref, gp_ref, dxi_ref, dw_ref, dg_ref, dsl_ref, dsh_ref, dsc_ref, dgn_ref, dzp_ref, dgp_ref):
        i = pl.program_id(0)

        @pl.when(i == 0)
        def _():
            dw_ref[...] = jnp.zeros_like(dw_ref)

        dxo_t = dx_ref[...]
        mixed_t = mx_ref[...]
        dy_t = dxo_t * g_ref[...]
        _acc_out(dg_ref, i, _colsum(dxo_t * (mixed_t * s_ref[...])))
        _acc_out(dsl_ref, i, _colsum(dy_t * mixed_t))
        dmixed = (_ext(dxp_ref, dx_ref, dxn_ref, i, ni) * g_ref[...]) * s_ref[...]
        xext = _ext(xp_ref, x_ref, xn_ref, i, ni)
        aext, rext, xhext = _norm_mod_ext(xext, gain_ref, sc_ref, sh_ref, i, tm, n)
        rows = tm + 2 * HALO
        da_parts = []
        for gi, w in enumerate(POOL_WINDOWS):
            sl = slice(gi * PG, (gi + 1) * PG)
            ag = aext[:, sl]
            mean = _sh(_window_sums(ag, w), -(w // 2), tm) / _pool_counts(i, tm, n, w, tm, HALO)
            pooled = (mean - ag[HALO:HALO + tm]).astype(BF16)
            dmg = dmixed[:, sl].astype(BF16)
            dw_ref[gi] += lax.dot_general(pooled, dmixed[HALO:HALO + tm, sl].astype(BF16), _TN,
                                          preferred_element_type=F32)
            dpl = lax.dot_general(dmg, w_ref[gi], _NT, preferred_element_type=F32)
            e = dpl / _pool_counts(i, tm, n, w, rows, 0)
            da_parts.append(_sh(_window_sums(e, w), 1 - w // 2, tm) - dpl[HALO:HALO + tm])
        da = jnp.concatenate(da_parts, axis=1)
        r = rext[HALO:HALO + tm]
        xh = xhext[HALO:HALO + tm]
        nrm = xh * gain_ref[...]
        dn = da * (1.0 + sc_ref[...])
        dxh = dn * gain_ref[...]
        dxi = dxo_t + r * (dxh - xh * jnp.mean(dxh * xh, axis=-1, keepdims=True))
        dxi_ref[...] = dxi
        _acc_out(dsh_ref, i, _colsum(da))
        _acc_out(dsc_ref, i, _colsum(da * nrm))
        _acc_out(dgn_ref, i, _colsum(dn * xh))
        dzp_ref[...] = (dxi * gp_ref[...]).astype(BF16)
        _acc_out(dgp_ref, i, _colsum(dxi * zp_ref[...]))

    row = pl.BlockSpec((tm, d), lambda i: (i, 0))
    prev, nxt = _halo_specs(tm, d, n)
    wspec = pl.BlockSpec((4, PG, PG), lambda i: (0, 0, 0))
    vshape = jax.ShapeDtypeStruct((1, d), F32)
    return pl.pallas_call(
        body, grid=(ni,),
        in_specs=[row, prev, nxt, row, row, prev, nxt] + [_vec(d)] * 5 + [wspec, row, _vec(d)],
        out_specs=[row, wspec] + [_vec(d)] * 5 + [row, _vec(d)],
        out_shape=[jax.ShapeDtypeStruct((n, d), F32), jax.ShapeDtypeStruct((4, PG, PG), F32)] + [vshape] * 5
        + [jax.ShapeDtypeStruct((n, d), BF16), vshape],
        name=name, compiler_params=_params("arbitrary"))(dxo, dxo, dxo, mixed, x, x, x, g, scale, gain, sc, sh, pool_w,
                                                         zprev, gprev)


def _adamw(gparts_list, w, m, v, *, name, silu_grad_of=None):
    nl = len(gparts_list)
    nparts, r, c = gparts_list[0].shape
    tr = _pick(r, (256, 128, 64, 32, 16, 8))
    has_c = silu_grad_of is not None

    def body(*refs):
        gp_refs = refs[:nl]
        it = iter(refs[nl:])
        w_ref, m_ref, v_ref = next(it), next(it), next(it)
        c_ref = next(it) if has_c else None
        g_ref, d_ref, mo_ref, vo_ref = next(it), next(it), next(it), next(it)
        layer = pl.program_id(0)

        def update(gp_ref):
            g = gp_ref[0].astype(F32)
            for p in range(1, nparts):
                g = g + gp_ref[p].astype(F32)
            if has_c:
                cv = c_ref[0]
                sg = _sigmoid(cv)
                g = g * (sg * (1.0 + cv * (1.0 - sg)))
            g_ref[0] = g
            mn = ADAM_B1 * m_ref[0] + (1.0 - ADAM_B1) * g
            vn = ADAM_B2 * v_ref[0] + (1.0 - ADAM_B2) * (g * g)
            m_hat = mn / (1.0 - ADAM_B1 ** ADAM_STEP)
            v_hat = vn / (1.0 - ADAM_B2 ** ADAM_STEP)
            d_ref[0] = -ADAM_LR * (m_hat / (jnp.sqrt(v_hat) + ADAM_EPS) + ADAM_WD * w_ref[0])
            mo_ref[0] = mn
            vo_ref[0] = vn

        if nl == 1:
            update(gp_refs[0])
        else:
            for li in range(nl):
                pl.when(layer == li)(functools.partial(update, gp_refs[li]))

    row = pl.BlockSpec((1, tr, c), lambda l, i: (l, i, 0))
    in_specs = [pl.BlockSpec((nparts, tr, c), lambda l, i, li=li: (0, jnp.where(l == li, i, 0), 0)) for li in range(nl)]
    in_specs += [row, row, row]
    args = list(gparts_list) + [w, m, v]
    if has_c:
        in_specs.append(row)
        args.append(silu_grad_of)
    return pl.pallas_call(
        body, grid=(nl, r // tr), in_specs=in_specs, out_specs=[row] * 4,
        out_shape=[jax.ShapeDtypeStruct((nl, r, c), F32)] * 4, name=name,
        compiler_params=_params("arbitrary", "arbitrary"))(*args)


def _adamw_nd(gparts, w, m, v, *, name, silu_grad_of=None):
    shape = w.shape
    c = shape[-1]
    if isinstance(gparts, (list, tuple)):
        nl = len(gparts)
        r = math.prod(shape[1:-1])
    else:
        nl = 1
        r = math.prod(shape[:-1]) if len(shape) > 1 else 1
        gparts = [gparts]
    rs = lambda a: a.reshape(nl, r, c)
    res = _adamw([gp.reshape(gp.shape[0], r, c) for gp in gparts], rs(w), rs(m), rs(v), name=name,
                 silu_grad_of=None if silu_grad_of is None else rs(silu_grad_of))
    return [a.reshape(shape) for a in res]


def _place():
    return lax.axis_index("x"), lax.axis_index("y"), lax.axis_index("c")


def _all_gather(arrs, *, name):
    k_arr = len(arrs)

    def body(*refs):
        ins = refs[:k_arr]
        outs = refs[k_arr:2 * k_arr]
        send_sems, recv_sems, local_sems = refs[2 * k_arr:]
        x, y, c = _place()
        me, sibling = (x, y, c), (x, y, 1 - c)
        chips = [(1 - x, y), (x, 1 - y), (1 - x, 1 - y)]

        def slot(a, px, py, pc):
            return outs[a].at[4 * px + 2 * py + pc]

        def copy(a, s, block, to, src=None):
            return pltpu.make_async_remote_copy(
                src_ref=slot(a, *block) if src is None else src, dst_ref=slot(a, *block),
                send_sem=send_sems.at[a, s], recv_sem=recv_sems.at[a, s], device_id=to, device_id_type=MESH)

        mine = [pltpu.make_async_copy(ins[a], slot(a, *me), local_sems.at[a]) for a in range(k_arr)]
        for cp in mine:
            cp.start()
        first = []
        for a in range(k_arr):
            first.append(copy(a, 0, me, sibling, src=ins[a]))
            first += [copy(a, 1 + j, me, (*chip, c), src=ins[a]) for j, chip in enumerate(chips)]
        for cp in first:
            cp.start()
        passed = []
        for j, chip in enumerate(chips):
            for a in range(k_arr):
                copy(a, 1 + j, (*chip, c), me).wait_recv()
                fw = copy(a, 4 + j, (*chip, c), sibling)
                fw.start()
                passed.append(fw)
        for a in range(k_arr):
            copy(a, 0, sibling, me).wait_recv()
            for j, chip in enumerate(chips):
                copy(a, 4 + j, (*chip, 1 - c), me).wait_recv()
        for cp in first + passed:
            cp.wait_send()
        for cp in mine:
            cp.wait()

    any_spec = pl.BlockSpec(memory_space=pl.ANY)
    return pl.pallas_call(
        body, in_specs=[any_spec] * k_arr, out_specs=[any_spec] * k_arr,
        out_shape=[jax.ShapeDtypeStruct((NDEV,) + a.shape, a.dtype) for a in arrs],
        scratch_shapes=[pltpu.SemaphoreType.DMA((k_arr, 7)), pltpu.SemaphoreType.DMA((k_arr, 7)),
                        pltpu.SemaphoreType.DMA((k_arr,))],
        name=name)(*arrs)


_HBM = pl.BlockSpec(memory_space=pltpu.HBM)
_SEM = pl.BlockSpec(memory_space=pltpu.SEMAPHORE)
_EFFECT = pltpu.SideEffectType.DATAFLOW_SIDE_EFFECTING


def _peers(x, y, c):
    return [(x ^ (rel >> 2), y ^ ((rel >> 1) & 1), c ^ (rel & 1)) for rel in range(1, NDEV)]


def _exchange_copies(srcs, lands, send_sems, recv_sems, scatter):
    x, y, c = _place()
    me = 4 * x + 2 * y + c
    copies = []
    for r, (px, py, pc) in enumerate(_peers(x, y, c)):
        peer = 4 * px + 2 * py + pc
        for a in range(len(srcs)):
            copies.append(pltpu.make_async_remote_copy(
                src_ref=srcs[a].at[peer] if scatter else srcs[a], dst_ref=lands[a].at[me],
                send_sem=send_sems.at[7 * a + r], recv_sem=recv_sems.at[7 * a + r], device_id=(px, py, pc),
                device_id_type=MESH))
    return copies


def _exchange_start(arrs, *, scatter, name):
    k_arr = len(arrs)
    land_shapes = [a.shape if scatter else (NDEV,) + a.shape for a in arrs]
    lands = [pltpu.with_memory_space_constraint(lax.empty(s, a.dtype), pltpu.HBM) for s, a in zip(land_shapes, arrs)]
    srcs = [pltpu.with_memory_space_constraint(a, pltpu.HBM) for a in arrs]

    def body(*refs):
        src_refs, land_refs = refs[:k_arr], refs[k_arr:2 * k_arr]
        send_sems, recv_sems = refs[2 * k_arr], refs[2 * k_arr + 1]
        token = refs[-1]
        for cp in _exchange_copies(src_refs, land_refs, send_sems, recv_sems, scatter):
            cp.start()
        token[...] = jnp.zeros_like(token)

    out_shape = ([pltpu.SemaphoreType.DMA((7 * k_arr,)), pltpu.SemaphoreType.DMA((7 * k_arr,))]
                 + [pltpu.HBM(a.shape, a.dtype) for a in arrs] + [pltpu.HBM(s, a.dtype) for s, a in zip(land_shapes, arrs)]
                 + [jax.ShapeDtypeStruct((8, 128), F32)])
    res = pl.pallas_call(
        body, name=name, out_shape=out_shape, in_specs=[_HBM] * (2 * k_arr),
        out_specs=[_SEM, _SEM] + [_HBM] * (2 * k_arr) + [pl.BlockSpec(memory_space=pltpu.VMEM)],
        input_output_aliases={i: 2 + i for i in range(2 * k_arr)},
        compiler_params=pltpu.CompilerParams(has_side_effects=_EFFECT))(*srcs, *lands)
    return dict(send=res[0], recv=res[1], srcs=list(res[2:2 + k_arr]), lands=list(res[2 + k_arr:2 + 2 * k_arr]),
                token=res[-1], scatter=scatter)


def _exchange_wait(handle, after, *, name):
    k_arr = len(handle["srcs"])
    scatter = handle["scatter"]

    def body(*refs):
        src_refs, land_refs = refs[:k_arr], refs[k_arr:2 * k_arr]
        send_sems, recv_sems = refs[2 * k_arr], refs[2 * k_arr + 1]
        x, y, c = _place()
        me = 4 * x + 2 * y + c
        for r, (px, py, pc) in enumerate(_peers(x, y, c)):
            peer = 4 * px + 2 * py + pc
            for a in range(k_arr):
                cp = pltpu.make_async_remote_copy(
                    src_ref=src_refs[a].at[peer] if scatter else src_refs[a], dst_ref=land_refs[a].at[peer],
                    send_sem=send_sems.at[7 * a + r], recv_sem=recv_sems.at[7 * a + r], device_id=(x, y, c),
                    device_id_type=MESH)
                cp.wait_send()
                cp.wait_recv()

    arrs = handle["srcs"] + handle["lands"]
    res = pl.pallas_call(
        body, name=name, out_shape=[pltpu.HBM(a.shape, a.dtype) for a in arrs],
        in_specs=[_HBM] * (2 * k_arr) + [_SEM, _SEM, pl.BlockSpec(memory_space=pl.ANY)],
        out_specs=[_HBM] * (2 * k_arr), input_output_aliases={i: i for i in range(2 * k_arr)},
        compiler_params=pltpu.CompilerParams(has_side_effects=_EFFECT))(*arrs, handle["send"], handle["recv"], after)
    me = 4 * lax.axis_index("x") + 2 * lax.axis_index("y") + lax.axis_index("c")
    out = []
    for src, land in zip(res[:k_arr], res[k_arr:]):
        own = lax.dynamic_index_in_dim(src, me, 0, keepdims=False) if scatter else src
        out.append(lax.dynamic_update_index_in_dim(land, own, me, 0))
    return out


def _ffn_fwd(x_in, y, g, ymul, gain, sc, sh, w_up, cw, cb, w_down, tag):
    xr, f = _norm_mod(x_in, gain, sc, sh, y=y, g=g, ymul=ymul, name=f"ffn_norm_{tag}")
    u, gc, hmid = _ffn_up_glu(f, w_up, cw, cb, name=f"ffn_up_glu_{tag}")
    z = _mm_w(hmid, w_down, name=f"ffn_down_{tag}")
    return xr, f, (u, gc), hmid, z


def _ffn_bwd(dxo, dz, xr, f, u_gc, hmid, gain, sc, w_up, cw, w_down, tag, gate_y=None, gate_g=None):
    d_wdown = _mm_tn((hmid, dz), name=f"ffn_down_dw_{tag}")
    dug, duv, dcw, dcb = _ffn_down_glu_bwd(dz, w_down, u_gc[0], u_gc[1], cw, name=f"ffn_down_glu_bwd_{tag}")
    df = _mm_w([dug, duv], w_up, tb=True, name=f"ffn_up_dx_{tag}")
    d_wup_g = _mm_tn((f, dug), name=f"ffn_up_dwg_{tag}")
    d_wup_v = _mm_tn((f, duv), name=f"ffn_up_dwv_{tag}")
    norm_res = _norm_mod_bwd(df, xr, gain, sc, dres=dxo, gate_y=gate_y, gate_g=gate_g, name=f"ffn_norm_bwd_{tag}")
    return norm_res, (d_wup_g, d_wup_v, d_wdown, dcw, dcb)


def _split6(mod):
    return [mod[j * D:(j + 1) * D][None, :] for j in range(6)]


def _row(v):
    return v.reshape(1, -1)


def kernel(x, c, ctx, c_ctx, ada_w, ada_b, mix_norm, ffn_norm, even_w_in, even_q_gain, even_k_gain, even_conv_w, even_w_out, odd_pool_w, odd_pool_scale, ffn_w_up, ffn_conv_w, ffn_conv_b, ffn_w_down, loss_target, m_c_ctx, m_ada_w, m_ada_b, m_mix_norm, m_ffn_norm, m_even_w_in, m_even_q_gain, m_even_k_gain, m_even_conv_w, m_even_w_out, m_odd_pool_w, m_odd_pool_scale, m_ffn_w_up, m_ffn_conv_w, m_ffn_conv_b, m_ffn_w_down, v_c_ctx, v_ada_w, v_ada_b, v_mix_norm, v_ffn_norm, v_even_w_in, v_even_q_gain, v_even_k_gain, v_even_conv_w, v_even_w_out, v_odd_pool_w, v_odd_pool_scale, v_ffn_w_up, v_ffn_conv_w, v_ffn_conv_b, v_ffn_w_down):
    n = x.shape[1]
    lc = ctx.shape[1]
    me = 4 * lax.axis_index("x") + 2 * lax.axis_index("y") + lax.axis_index("c")
    xs, ctxs, tgt = x[0], ctx[0], loss_target[0]
    acols = ada_w.shape[2]

    small = jnp.concatenate([even_conv_w.reshape(-1), ffn_conv_w.reshape(-1), odd_pool_scale.reshape(-1)])
    nsmall = small.shape[0]
    small = jnp.pad(small, (0, (-nsmall) % 1024)).reshape(-1, 128)
    c_rows = jnp.pad(c, ((0, 7), (0, 0)))
    g_c, g_win, g_small = _all_gather([c_rows, even_w_in[0].astype(BF16), small], name="gather_first")
    w_in = g_win.transpose(1, 0, 2).reshape(D, -1)
    g_small = g_small.reshape(NDEV, -1)
    ecw = even_conv_w.shape[2]
    fcw = ffn_conv_w.shape[2]
    conv_w = g_small[:, :3 * ecw].reshape(NDEV, 3, ecw).transpose(1, 0, 2).reshape(3, CW)
    o1 = 3 * ecw
    fconv_w = g_small[:, o1:o1 + 6 * fcw].reshape(NDEV, 2, 3, fcw).transpose(1, 2, 0, 3).reshape(2, 3, DFF)
    o2 = o1 + 6 * fcw
    pool_scale = g_small[:, o2:o2 + D // NDEV].reshape(1, D)

    mraw = jnp.concatenate([g_c[:, 0, :], c_ctx[None, :], jnp.zeros((7, D), F32)], axis=0)
    my_bias = lax.dynamic_slice_in_dim(ada_b, me * acols, acols, axis=1)
    modp = jnp.stack([_mm(mraw, ada_w[l], silu_a=True, bias=my_bias[l:l + 1], name=f"ada_proj_{l}", tm=16, tn=256)
                      for l in range(2)])
    (g_mod,) = _all_gather([modp], name="gather_mod")
    mod_rows = g_mod.transpose(1, 2, 0, 3).reshape(2, 16, 6 * D)
    late_shards = [even_w_out[0].astype(BF16), odd_pool_w[0].astype(BF16), ffn_w_up.astype(BF16),
                   ffn_w_down.astype(BF16)]
    late_shards, mod_rows = lax.optimization_barrier((late_shards, mod_rows))
    h_weights = _exchange_start(late_shards, scatter=False, name="weights_start")
    mod_rows = mod_rows + h_weights["token"][0, 0]
    mod = lax.dynamic_index_in_dim(mod_rows, me, axis=1, keepdims=False)
    sh1, sc1, g1, sh2, sc2, g2 = _split6(mod[0])
    sh1b, sc1b, g1b, sh2b, sc2b, g2b = _split6(mod[1])
    csh1, csc1 = _split6(mod_rows[0, 8])[:2]
    mixn = [_row(mix_norm[l]) for l in range(2)]
    ffnn = [_row(ffn_norm[l]) for l in range(2)]
    qg, kg = _row(even_q_gain[0]), _row(even_k_gain[0])
    fcb = [_row(ffn_conv_b[l]) for l in range(2)]

    cs_t, sn_t = _rope_tables(n)
    a_lat = _norm_mod(xs, mixn[0], sc1, sh1, name="mix0_norm")
    a_ctx = _norm_mod(ctxs, mixn[0], csc1, csh1, name="mix0_norm_ctx")
    p_lat = _mm_w(a_lat, w_in, name="in_proj")
    p_ctx = _mm(a_ctx, w_in[:, AW:AW + 4 * HD], name="in_proj_ctx", tm=256, tn=512, tk=1024)
    kv_ctx = _qkv_prep(p_ctx, qg, kg, None, None, has_q=False, kv_col=0, kv_rows=lc + n, name="qkv_prep_ctx")
    q_r, k_all, v_all = _qkv_prep(p_lat, qg, kg, cs_t, sn_t, has_q=True, kv_col=1, kv_rows=lc + n, kv_row_off=lc,
                                  kv_into=kv_ctx, name="qkv_prep")
    o_attn, lse = _attn_fwd(q_r, k_all, v_all, name="attn_fwd")
    cat = _conv_gate_fwd(p_lat, o_attn, conv_w, name="conv_gate")
    g_wout, g_pool, g_up, g_down = _exchange_wait(h_weights, cat, name="weights_wait")
    w_out = g_wout.reshape(D, D)
    pool_w = g_pool.transpose(1, 0, 2, 3).reshape(4, PG, PG)
    w_up = [g_up[:, l].transpose(1, 0, 2).reshape(D, 2 * DFF) for l in range(2)]
    w_down = [g_down[:, l].reshape(DFF, D) for l in range(2)]
    y0 = _mm_w(cat, w_out, name="out_proj", tm=512)
    x1, f0, u0, h0, z0 = _ffn_fwd(xs, y0, g1, None, ffnn[0], sc2, sh2, w_up[0], fconv_w[0], fcb[0], w_down[0], "l0")

    x2, mixed = _pool_fwd(x1, z0, g2, mixn[1], sc1b, sh1b, pool_w, name="pool_fwd")
    x3, f1, u1, h1, z1 = _ffn_fwd(x2, mixed, g1b, pool_scale, ffnn[1], sc2b, sh2b, w_up[1], fconv_w[1], fcb[1],
                                  w_down[1], "l1")
    dx4, loss_part, dz1, dg2b = _loss_head(x3, z1, g2b, tgt, name="loss_head")
    loss = lax.psum(loss_part[0, 0], ("x", "y", "c"))

    (dx3, dsh2b, dsc2b, dffn1), (dup1g, dup1v, ddown1, dfcw1, dfcb1) = _ffn_bwd(
        dx4, dz1, x3, f1, u1, h1, ffnn[1], sc2b, w_up[1], fconv_w[1], w_down[1], "l1")
    dx2, dpool_w, dg1b, dpscale, dsh1b, dsc1b, dmix1, dz0, dg2 = _pool_bwd(
        dx3, mixed, x2, g1b, pool_scale, mixn[1], sc1b, sh1b, pool_w, z0, g2, name="pool_bwd")

    def up_shards(dg, dv):
        return jnp.concatenate([dg, dv], axis=1).reshape(D, NDEV, -1).transpose(1, 0, 2)

    s_pool = dpool_w.astype(BF16).reshape(4, NDEV, PG // NDEV, PG).transpose(1, 0, 2, 3)
    h_g1 = _exchange_start([s_pool, up_shards(dup1g, dup1v), ddown1.reshape(NDEV, DFF // NDEV, D)], scatter=True,
                           name="grads1_start")

    (dx1, dsh2, dsc2, dffn0, dy0, dg1), (dup0g, dup0v, ddown0, dfcw0, dfcb0) = _ffn_bwd(
        dx2, dz0, x1, f0, u0, h0, ffnn[0], sc2, w_up[0], fconv_w[0] + h_g1["token"][0, 0], w_down[0], "l0",
        gate_y=y0, gate_g=g1)
    h_g0 = _exchange_start([up_shards(dup0g, dup0v), ddown0.reshape(NDEV, DFF // NDEV, D)], scatter=True,
                           name="grads0_start")
    dcat = _mm_w(dy0, w_out, tb=True, name="out_proj_dx", tm=512)
    d_wout = _mm_tn((cat, dy0), name="out_proj_dw")
    dp_conv, dconv_w = _conv_gate_bwd(dcat, p_lat, conv_w + h_g0["token"][0, 0], name="conv_gate_bwd")
    dq_r, dk_all, dv_all = _attn_bwd(q_r, k_all, v_all, dcat, o_attn, lse, name="attn_bwd")
    dp_qkv, dqg_l, dkg_l = _qkv_bwd(p_lat, dq_r, dk_all, dv_all, qg, kg, cs_t, sn_t, has_q=True, kv_col=1,
                                    kv_row_off=lc, name="qkv_bwd")
    dp_ctx, _zero_qg, dkg_c = _qkv_bwd(p_ctx, None, dk_all, dv_all, qg, kg, None, None, has_q=False, kv_col=0,
                                       kv_row_off=0, name="qkv_bwd_ctx")
    da_lat = _mm_w([dp_qkv, dp_conv], w_in, tb=True, name="in_proj_dx", tm=512)
    da_ctx = _mm(dp_ctx, w_in[:, :D], tb=True, name="in_proj_dx_ctx", tm=256, tn=512, tk=1024)
    d_win_qkv = _mm_tn([(a_lat, dp_qkv), (a_ctx, dp_ctx)], name="in_proj_dw_qkv")
    d_win_conv = _mm_tn((a_lat, dp_conv), name="in_proj_dw_conv")
    d_win = jnp.concatenate([d_win_qkv, d_win_conv], axis=1)
    grad_x, dsh1, dsc1, dmix0 = _norm_mod_bwd(da_lat, xs, mixn[0], sc1, dres=dx1, name="mix0_norm_bwd")
    _dctx, dcsh1, dcsc1, dmix0c = _norm_mod_bwd(da_ctx, ctxs, mixn[0], csc1, name="mix0_norm_bwd_ctx")

    z1k = jnp.zeros((1, D), F32)
    pack = jnp.concatenate(
        [v.reshape(-1) for v in (dsh1, dsc1, dg1, dsh2, dsc2, dg2, dsh1b, dsc1b, dg1b, dsh2b, dsc2b, dg2b,
                                 dcsh1, dcsc1, z1k, z1k, z1k, z1k,
                                 dmix0, dmix1, dmix0c, z1k, dffn0, dffn1, dqg_l, dkg_l + dkg_c,
                                 dfcb0, dfcb1, dconv_w, dfcw0, dfcw1, dpscale)])
    npack = pack.shape[0]
    pack = jnp.pad(pack, (0, (-npack) % 1024)).reshape(-1, 128)
    (g_pack,) = _all_gather([pack], name="gather_small_grads")
    gp = g_pack.reshape(NDEV, -1)
    off = [0]

    def take(size):
        seg = gp[:, off[0]:off[0] + size]
        off[0] += size
        return seg

    dmod_all = take(12 * D).reshape(NDEV, 2, 6 * D)
    dmodc_all = take(6 * D).reshape(NDEV, 1, 6 * D)
    dmix_all = take(4 * D).reshape(NDEV, 2, 2, D)
    dffn_all = take(2 * D).reshape(NDEV, 2, D)
    dqg_all = take(HD).reshape(NDEV, 1, HD)
    dkg_all = take(HD).reshape(NDEV, 1, HD)
    dfcb_all = take(2 * DFF).reshape(NDEV, 2, DFF)
    dconvw_all = take(3 * CW).reshape(NDEV, 3, CW)
    dfcw_all = take(6 * DFF).reshape(NDEV, 2, 3, DFF)
    dpscale_all = take(D).reshape(NDEV, D)

    dmodc_sum = dmodc_all[0]
    for dev in range(1, NDEV):
        dmodc_sum = dmodc_sum + dmodc_all[dev]
    my_cols = lambda a: lax.dynamic_slice_in_dim(a, me * acols, acols, axis=a.ndim - 1)
    rows0 = jnp.concatenate([my_cols(dmod_all[:, 0]), my_cols(dmodc_sum), jnp.zeros((7, acols), F32)], axis=0)
    rows1 = jnp.concatenate([my_cols(dmod_all[:, 1]), jnp.zeros((8, acols), F32)], axis=0)
    d_ada = jnp.stack([_mm(mraw, rows, ta=True, silu_a=True, name=f"ada_dw_{l}", tm=512, tn=256, tk=16)
                       for l, rows in enumerate((rows0, rows1))])
    dscc_part = _mm(rows0, ada_w[0], tb=True, name="ada_dcctx", tm=16, tn=512, tk=256)
    (g_dscc,) = _all_gather([dscc_part[8:16]], name="gather_dcctx")

    attn_shards = [d_win.reshape(D, NDEV, -1).transpose(1, 0, 2), d_wout.reshape(NDEV, D // NDEV, D)]
    attn_shards, g_dscc = lax.optimization_barrier((attn_shards, g_dscc))
    h_ga = _exchange_start(attn_shards, scatter=True, name="grads_attn_start")
    dmod_all = dmod_all + h_ga["token"][0, 0]

    outs = {}

    def put(nm, res):
        outs["grad_" + nm], outs["delta_" + nm], outs["new_m_" + nm], outs["new_v_" + nm] = res

    dmodc_pad = jnp.concatenate([dmodc_all, jnp.zeros_like(dmodc_all)], axis=1)
    put("ada_b", _adamw_nd(jnp.concatenate([dmod_all, dmodc_pad], axis=0), ada_b, m_ada_b, v_ada_b, name="adam_ada_b"))
    put("mix_norm", _adamw_nd(jnp.concatenate([dmix_all[:, 0], dmix_all[:, 1]], axis=0), mix_norm, m_mix_norm,
                              v_mix_norm, name="adam_mix_norm"))
    put("ffn_norm", _adamw_nd(dffn_all, ffn_norm, m_ffn_norm, v_ffn_norm, name="adam_ffn_norm"))
    put("even_q_gain", _adamw_nd(dqg_all, even_q_gain, m_even_q_gain, v_even_q_gain, name="adam_q_gain"))
    put("even_k_gain", _adamw_nd(dkg_all, even_k_gain, m_even_k_gain, v_even_k_gain, name="adam_k_gain"))
    put("ffn_conv_b", _adamw_nd(dfcb_all, ffn_conv_b, m_ffn_conv_b, v_ffn_conv_b, name="adam_ffn_conv_b"))
    my_convw = lax.dynamic_slice_in_dim(dconvw_all, me * ecw, ecw, axis=2)[:, None]
    put("even_conv_w", _adamw_nd(my_convw, even_conv_w, m_even_conv_w, v_even_conv_w, name="adam_even_conv_w"))
    my_fcw = lax.dynamic_slice_in_dim(dfcw_all, me * fcw, fcw, axis=3)
    put("ffn_conv_w", _adamw_nd(my_fcw, ffn_conv_w, m_ffn_conv_w, v_ffn_conv_w, name="adam_ffn_conv_w"))
    my_ps = lax.dynamic_slice_in_dim(dpscale_all, me * (D // NDEV), D // NDEV, axis=1)[:, None]
    put("odd_pool_scale", _adamw_nd(my_ps, odd_pool_scale, m_odd_pool_scale, v_odd_pool_scale, name="adam_pool_scale"))

    put("ada_w", _adamw_nd(d_ada[None], ada_w, m_ada_w, v_ada_w, name="adam_ada_w"))
    put("c_ctx", _adamw_nd(g_dscc[:, 0:1, :].reshape(NDEV, D), c_ctx, m_c_ctx, v_c_ctx, name="adam_c_ctx",
                           silu_grad_of=c_ctx))

    r_pool, r_up1, r_down1 = _exchange_wait(h_g1, outs["grad_ada_b"], name="grads1_wait")
    r_up0, r_down0 = _exchange_wait(h_g0, outs["grad_mix_norm"], name="grads0_wait")
    r_win, r_wout = _exchange_wait(h_ga, outs["grad_c_ctx"], name="grads_attn_wait")
    put("even_w_in", _adamw_nd(r_win[:, None], even_w_in, m_even_w_in, v_even_w_in, name="adam_w_in"))
    put("even_w_out", _adamw_nd(r_wout[:, None], even_w_out, m_even_w_out, v_even_w_out, name="adam_w_out"))
    put("odd_pool_w", _adamw_nd(r_pool[:, None], odd_pool_w, m_odd_pool_w, v_odd_pool_w, name="adam_pool_w"))
    put("ffn_w_up", _adamw_nd([r_up0, r_up1], ffn_w_up, m_ffn_w_up, v_ffn_w_up, name="adam_w_up"))
    put("ffn_w_down", _adamw_nd([r_down0, r_down1], ffn_w_down, m_ffn_w_down, v_ffn_w_down, name="adam_w_down"))

    names = ["c_ctx", "ada_w", "ada_b", "mix_norm", "ffn_norm", "even_w_in", "even_q_gain", "even_k_gain",
             "even_conv_w", "even_w_out", "odd_pool_w", "odd_pool_scale", "ffn_w_up", "ffn_conv_w", "ffn_conv_b",
             "ffn_w_down"]
    result = [loss, grad_x[None]]
    for kind in ("grad_", "delta_", "new_m_", "new_v_"):
        result += [outs[kind + nm] for nm in names]
    return tuple(result)
```

```python
import functools
import math

import jax
import jax.numpy as jnp
from jax import lax
from jax.experimental import pallas as pl
from jax.experimental.pallas import tpu as pltpu

F32 = jnp.float32
BF16 = jnp.bfloat16

D = 1024
HD = 128
NQ = 4
NKV = 2
AW = NQ * HD
CW = D - AW
DFF = 2816
GRID_W = 64
ROPE_THETA = 10000.0
POOL_WINDOWS = (2, 4, 8, 16)
PG = D // 4
EPS = 1e-6
NDEV = 8
HALO = 8
MESH = pl.DeviceIdType.MESH

ADAM_LR = 0.001
ADAM_B1 = 0.9
ADAM_B2 = 0.999
ADAM_EPS = 1e-08
ADAM_WD = 0.01
ADAM_STEP = 10


def _pick(dim, prefs):
    for p in prefs:
        if dim % p == 0:
            return p
    return dim


def _params(*sem):
    return pltpu.CompilerParams(dimension_semantics=sem)


_NT = (((1,), (1,)), ((), ()))
_TN = (((0,), (0,)), ((), ()))
_SCALE = HD ** -0.5
_QSCALE = _SCALE * math.log2(math.e)
_LN2 = math.log(2.0)


def _mm(a_list, b, *, name, ta=False, tb=False, out_dtype=F32, silu_a=False, bias=None, tm=None, tn=None, tk=None):
    if not isinstance(a_list, (list, tuple)):
        a_list = [a_list]
    na = len(a_list)
    assert not (ta and na > 1)
    if ta:
        kdim, m = a_list[0].shape
        ks = [kdim]
    else:
        m = a_list[0].shape[0]
        ks = [a.shape[1] for a in a_list]
        kdim = sum(ks)
    n = b.shape[0] if tb else b.shape[1]
    assert (b.shape[1] if tb else b.shape[0]) == kdim
    kunit = math.gcd(*ks) if na > 1 else kdim
    tm = min(tm, m) if tm else _pick(m, (512, 256, 128, 64, 32, 16, 8))
    tn = min(tn, n) if tn else _pick(n, (512, 256, 128))
    tk = min(tk, kunit) if tk else _pick(kunit, (1024, 768, 512, 256, 128))
    assert m % tm == 0 and n % tn == 0 and all(k % tk == 0 for k in ks)
    nks = [k // tk for k in ks]
    starts = [sum(nks[:i]) for i in range(na)]
    nk = sum(nks)
    has_bias = bias is not None

    def body(*refs):
        a_refs = refs[:na]
        b_ref = refs[na]
        bias_ref = refs[na + 1] if has_bias else None
        o_ref = refs[na + 1 + has_bias]
        acc = refs[-1]
        k = pl.program_id(2)

        @pl.when(k == 0)
        def _():
            acc[...] = jnp.zeros_like(acc)

        bv = b_ref[...].astype(BF16)
        dn = (((0 if ta else 1,), (1 if tb else 0,)), ((), ()))
        for idx in range(na):
            def step(idx=idx):
                av = a_refs[idx][...]
                if silu_a:
                    av = av * jax.nn.sigmoid(av)
                acc[...] += lax.dot_general(av.astype(BF16), bv, dn, preferred_element_type=F32)
            if na == 1:
                step()
            else:
                pl.when((k >= starts[idx]) & (k < starts[idx] + nks[idx]))(step)

        @pl.when(k == nk - 1)
        def _():
            r = acc[...]
            if has_bias:
                r = r + bias_ref[...]
            o_ref[...] = r.astype(o_ref.dtype)

    in_specs = []
    for idx in range(na):
        if ta:
            in_specs.append(pl.BlockSpec((tk, tm), lambda i, j, k: (k, i)))
        else:
            lo, cnt = starts[idx], nks[idx]
            in_specs.append(pl.BlockSpec((tm, tk), lambda i, j, k, lo=lo, cnt=cnt: (i, jnp.clip(k - lo, 0, cnt - 1))))
    if tb:
        in_specs.append(pl.BlockSpec((tn, tk), lambda i, j, k: (j, k)))
    else:
        in_specs.append(pl.BlockSpec((tk, tn), lambda i, j, k: (k, j)))
    args = list(a_list) + [b]
    if has_bias:
        in_specs.append(pl.BlockSpec((1, tn), lambda i, j, k: (0, j)))
        args.append(bias)
    return pl.pallas_call(
        body, grid=(m // tm, n // tn, nk), in_specs=in_specs,
        out_specs=pl.BlockSpec((tm, tn), lambda i, j, k: (i, j)),
        out_shape=jax.ShapeDtypeStruct((m, n), out_dtype),
        scratch_shapes=[pltpu.VMEM((tm, tn), F32)], name=name,
        compiler_params=_params("parallel", "parallel", "arbitrary"))(*args)


def _mm_w(a_list, w, *, name, tb=False, tm=256, out_dtype=F32):
    if not isinstance(a_list, (list, tuple)):
        a_list = [a_list]
    na = len(a_list)
    m = a_list[0].shape[0]
    ks = [a.shape[1] for a in a_list]
    offs = [sum(ks[:i]) for i in range(na)]
    n = w.shape[0] if tb else w.shape[1]
    assert (w.shape[1] if tb else w.shape[0]) == sum(ks)
    tm = min(tm, m)
    assert m % tm == 0

    def body(*refs):
        a_refs, w_ref, o_ref = refs[:na], refs[na], refs[na + 1]
        acc = None
        for idx in range(na):
            av = a_refs[idx][...].astype(BF16)
            if tb:
                part = lax.dot_general(av, w_ref[:, offs[idx]:offs[idx] + ks[idx]], _NT, preferred_element_type=F32)
            else:
                part = jnp.dot(av, w_ref[offs[idx]:offs[idx] + ks[idx], :], preferred_element_type=F32)
            acc = part if acc is None else acc + part
        o_ref[...] = acc.astype(o_ref.dtype)

    in_specs = [pl.BlockSpec((tm, k), lambda i: (i, 0)) for k in ks] + [pl.BlockSpec(w.shape, lambda i: (0, 0))]
    return pl.pallas_call(
        body, grid=(m // tm,), in_specs=in_specs, out_specs=pl.BlockSpec((tm, n), lambda i: (i, 0)),
        out_shape=jax.ShapeDtypeStruct((m, n), out_dtype), name=name, compiler_params=_params("parallel"))(*a_list, w)


def _mm_w_ep(a_list, w, epilogue, row_in, vec_in, out_dtypes, sum_widths, *, name, tb=False, tm=256, sub=128):
    if not isinstance(a_list, (list, tuple)):
        a_list = [a_list]
    na, nr, nv, no, ns = len(a_list), len(row_in), len(vec_in), len(out_dtypes), len(sum_widths)
    m = a_list[0].shape[0]
    ks = [a.shape[1] for a in a_list]
    offs = [sum(ks[:i]) for i in range(na)]
    n = w.shape[0] if tb else w.shape[1]
    assert (w.shape[1] if tb else w.shape[0]) == sum(ks)
    tm = min(tm, m)
    sub = min(sub, tm)
    assert m % tm == 0 and tm % sub == 0

    def body(*refs):
        a_refs, w_ref = refs[:na], refs[na]
        row_refs = refs[na + 1:na + 1 + nr]
        vec_refs = refs[na + 1 + nr:na + 1 + nr + nv]
        out_refs = refs[na + 1 + nr + nv:na + 1 + nr + nv + no]
        sum_refs = refs[na + 1 + nr + nv + no:]

        @pl.when(pl.program_id(0) == 0)
        def _():
            for s_ref in sum_refs:
                s_ref[...] = jnp.zeros_like(s_ref)

        vecs = [v[...] for v in vec_refs]
        for r0 in range(0, tm, sub):
            acc = None
            for idx in range(na):
                av = a_refs[idx][r0:r0 + sub, :].astype(BF16)
                if tb:
                    part = lax.dot_general(av, w_ref[:, offs[idx]:offs[idx] + ks[idx]], _NT, preferred_element_type=F32)
                else:
                    part = jnp.dot(av, w_ref[offs[idx]:offs[idx] + ks[idx], :], preferred_element_type=F32)
                acc = part if acc is None else acc + part
            outs, sums = epilogue(acc, [r[r0:r0 + sub, :] for r in row_refs], vecs)
            for o_ref, o in zip(out_refs, outs):
                o_ref[r0:r0 + sub, :] = o.astype(o_ref.dtype)
            for s_ref, s in zip(sum_refs, sums):
                s_ref[...] += s

    row = pl.BlockSpec((tm, n), lambda i: (i, 0))
    in_specs = ([pl.BlockSpec((tm, k), lambda i: (i, 0)) for k in ks] + [pl.BlockSpec(w.shape, lambda i: (0, 0))]
                + [row] * nr + [_vec(n)] * nv)
    return pl.pallas_call(
        body, grid=(m // tm,), in_specs=in_specs, out_specs=[row] * no + [_vec(sw) for sw in sum_widths],
        out_shape=[jax.ShapeDtypeStruct((m, n), dt) for dt in out_dtypes]
        + [jax.ShapeDtypeStruct((1, sw), F32) for sw in sum_widths],
        name=name, compiler_params=_params("arbitrary" if ns else "parallel"))(*a_list, w, *row_in, *vec_in)


def _ep_norm_bwd(has_res, has_gate):
    def ep(dav, rows, vecs):
        xv = rows[0]
        gain, scv = vecs[0], vecs[1]
        r = lax.rsqrt(jnp.mean(xv * xv, axis=-1, keepdims=True) + EPS)
        xh = xv * r
        nrm = xh * gain
        dn = dav * (1.0 + scv)
        dxh = dn * gain
        dx = r * (dxh - xh * jnp.mean(dxh * xh, axis=-1, keepdims=True))
        if has_res:
            dx = dx + rows[1]
        outs, sums = [dx], [_colsum(dav), _colsum(dav * nrm), _colsum(dn * xh)]
        if has_gate:
            outs.append(dx * vecs[2])
            sums.append(_colsum(dx * rows[1 + has_res]))
        return outs, sums
    return ep


def _ep_loss(d):
    def ep(zv, rows, vecs):
        xv, tv = rows
        gv = vecs[0]
        diff = (xv + gv * zv) - tv
        dx = diff * (1.0 / d)
        part = 0.5 * jnp.sum(jnp.mean(diff * diff, axis=-1, keepdims=True), axis=0, keepdims=True)
        return [dx, dx * gv], [jnp.broadcast_to(part, (1, 128)), _colsum(dx * zv)]
    return ep


def _ep_resid_norm(yv, rows, vecs):
    g, gain, scv, shv = vecs
    xv = rows[0] + g * yv
    r = lax.rsqrt(jnp.mean(xv * xv, axis=-1, keepdims=True) + EPS)
    return [yv, xv, ((xv * r) * gain) * (1.0 + scv) + shv], []


def _mm_tn(pairs, *, name, tk=1024, out_dtype=BF16):
    if not isinstance(pairs, list):
        pairs = [pairs]
    m, n = pairs[0][0].shape[1], pairs[0][1].shape[1]
    tks = [min(tk, a.shape[0]) for a, _ in pairs]
    nks = [a.shape[0] // t for (a, _), t in zip(pairs, tks)]
    assert all(a.shape[0] == b.shape[0] and a.shape[0] % t == 0 for (a, b), t in zip(pairs, tks))
    starts = [sum(nks[:i]) for i in range(len(pairs))]
    nk = sum(nks)

    def body(*refs):
        o_ref, acc = refs[-2], refs[-1]
        k = pl.program_id(0)

        @pl.when(k == 0)
        def _():
            acc[...] = jnp.zeros_like(acc)

        for idx in range(len(pairs)):
            a_ref, b_ref = refs[2 * idx], refs[2 * idx + 1]

            def step(a_ref=a_ref, b_ref=b_ref):
                acc[...] += lax.dot_general(a_ref[...], b_ref[...], _TN, preferred_element_type=F32)

            if len(pairs) == 1:
                step()
            else:
                pl.when((k >= starts[idx]) & (k < starts[idx] + nks[idx]))(step)

        @pl.when(k == nk - 1)
        def _():
            o_ref[...] = acc[...].astype(o_ref.dtype)

    in_specs, args = [], []
    for (a, b), t, lo, cnt in zip(pairs, tks, starts, nks):
        idx_map = lambda k, lo=lo, cnt=cnt: (jnp.clip(k - lo, 0, cnt - 1), 0)
        in_specs += [pl.BlockSpec((t, m), idx_map), pl.BlockSpec((t, n), idx_map)]
        args += [a, b]
    return pl.pallas_call(
        body, grid=(nk,), in_specs=in_specs, out_specs=pl.BlockSpec((m, n), lambda k: (0, 0)),
        out_shape=jax.ShapeDtypeStruct((m, n), out_dtype), scratch_shapes=[pltpu.VMEM((m, n), F32)], name=name,
        compiler_params=_params("arbitrary"))(*args)


def _vec(d, col=None):
    if col is None:
        return pl.BlockSpec((1, d), lambda i, *_: (0, 0))
    return pl.BlockSpec((1, d), col)


def _halo_specs(tm, width, nrows, colblk=0, row_off=0):
    r = tm // HALO
    off = row_off // HALO
    last = nrows // HALO - 1
    prev = pl.BlockSpec((HALO, width), lambda i, *_: (off + jnp.maximum(i * r - 1, 0), colblk))
    nxt = pl.BlockSpec((HALO, width), lambda i, *_: (off + jnp.minimum((i + 1) * r, last), colblk))
    return prev, nxt


def _ext(prev_ref, main_ref, next_ref, i, ni):
    p = jnp.where(i > 0, prev_ref[...], 0.0)
    n = jnp.where(i < ni - 1, next_ref[...], 0.0)
    return jnp.concatenate([p, main_ref[...], n], axis=0)


def _sh(ext, k, tm):
    if k == 0:
        return ext[HALO:HALO + tm]
    rows = ext.shape[0]
    return pltpu.roll(ext, (-k) % rows, axis=0)[HALO:HALO + tm]


def _roll_rows(v, k):
    rows = v.shape[0]
    return pltpu.roll(v, (-k) % rows, axis=0) if k % rows else v


def _conv3(ext, w_ref, tm):
    return _sh(ext, -1, tm) * w_ref[0:1, :] + _sh(ext, 0, tm) * w_ref[1:2, :] + _sh(ext, 1, tm) * w_ref[2:3, :]


def _colsum(v):
    return jnp.sum(v, axis=0, keepdims=True)


def _acc_out(ref, i, val):
    @pl.when(i == 0)
    def _():
        ref[...] = jnp.zeros_like(ref)

    ref[...] += val


def _sigmoid(v):
    return jax.nn.sigmoid(v)


def _norm_mod(x, gain, sc, sh, *, name, y=None, g=None, ymul=None, tm=512):
    n, d = x.shape
    tm = min(tm, n)
    has_res = y is not None
    has_mul = ymul is not None

    def body(*refs):
        it = iter(refs)
        x_ref = next(it)
        y_ref = next(it) if has_res else None
        g_ref = next(it) if has_res else None
        m_ref = next(it) if has_mul else None
        gain_ref, sc_ref, sh_ref = next(it), next(it), next(it)
        xo_ref = next(it) if has_res else None
        a_ref = next(it)
        xv = x_ref[...]
        if has_res:
            yv = y_ref[...]
            if has_mul:
                yv = yv * m_ref[...]
            xv = xv + g_ref[...] * yv
            xo_ref[...] = xv
        r = lax.rsqrt(jnp.mean(xv * xv, axis=-1, keepdims=True) + EPS)
        nrm = (xv * r) * gain_ref[...]
        a_ref[...] = (nrm * (1.0 + sc_ref[...]) + sh_ref[...]).astype(BF16)

    row = pl.BlockSpec((tm, d), lambda i: (i, 0))
    in_specs, args = [row], [x]
    if has_res:
        in_specs += [row, _vec(d)]
        args += [y, g]
    if has_mul:
        in_specs.append(_vec(d))
        args.append(ymul)
    in_specs += [_vec(d)] * 3
    args += [gain, sc, sh]
    out_specs, out_shape = [], []
    if has_res:
        out_specs.append(row)
        out_shape.append(jax.ShapeDtypeStruct((n, d), F32))
    out_specs.append(row)
    out_shape.append(jax.ShapeDtypeStruct((n, d), BF16))
    res = pl.pallas_call(body, grid=(n // tm,), in_specs=in_specs, out_specs=out_specs, out_shape=out_shape,
                         name=name, compiler_params=_params("parallel"))(*args)
    return res if has_res else res[0]


def _norm_mod_bwd(da, x, gain, sc, *, name, dres=None, gate_y=None, gate_g=None, tm=512):
    n, d = x.shape
    tm = min(tm, n)
    has_res = dres is not None
    has_gate = gate_y is not None

    def body(*refs):
        it = iter(refs)
        da_ref, x_ref = next(it), next(it)
        r_ref = next(it) if has_res else None
        y_ref = next(it) if has_gate else None
        g_ref = next(it) if has_gate else None
        gain_ref, sc_ref = next(it), next(it)
        dx_ref, dsh_ref, dsc_ref, dgn_ref = next(it), next(it), next(it), next(it)
        dy_ref = next(it) if has_gate else None
        dg_ref = next(it) if has_gate else None
        i = pl.program_id(0)
        xv = x_ref[...]
        dav = da_ref[...]
        r = lax.rsqrt(jnp.mean(xv * xv, axis=-1, keepdims=True) + EPS)
        xh = xv * r
        nrm = xh * gain_ref[...]
        dn = dav * (1.0 + sc_ref[...])
        dxh = dn * gain_ref[...]
        dx = r * (dxh - xh * jnp.mean(dxh * xh, axis=-1, keepdims=True))
        if has_res:
            dx = dx + r_ref[...]
        dx_ref[...] = dx
        _acc_out(dsh_ref, i, _colsum(dav))
        _acc_out(dsc_ref, i, _colsum(dav * nrm))
        _acc_out(dgn_ref, i, _colsum(dn * xh))
        if has_gate:
            dy_ref[...] = (dx * g_ref[...]).astype(BF16)
            _acc_out(dg_ref, i, _colsum(dx * y_ref[...]))

    row = pl.BlockSpec((tm, d), lambda i: (i, 0))
    in_specs, args = [row, row], [da, x]
    if has_res:
        in_specs.append(row)
        args.append(dres)
    if has_gate:
        in_specs += [row, _vec(d)]
        args += [gate_y, gate_g]
    in_specs += [_vec(d)] * 2
    args += [gain, sc]
    vec_shape = jax.ShapeDtypeStruct((1, d), F32)
    out_specs = [row, _vec(d), _vec(d), _vec(d)]
    out_shape = [jax.ShapeDtypeStruct((n, d), F32), vec_shape, vec_shape, vec_shape]
    if has_gate:
        out_specs += [row, _vec(d)]
        out_shape += [jax.ShapeDtypeStruct((n, d), BF16), vec_shape]
    return pl.pallas_call(
        body, grid=(n // tm,), in_specs=in_specs, out_specs=out_specs, out_shape=out_shape,
        name=name, compiler_params=_params("arbitrary"))(*args)


def _ffn_up_glu(f, w_up, cw, cb, *, name, tm=256, tc=256):
    n, d = f.shape
    tm = min(tm, n)
    ni = n // tm
    nc = DFF // tc
    halo = 16
    rows = tm + 2 * halo
    r = tm // halo
    last = n // halo - 1

    def body(f_ref, fp_ref, fn_ref, w_ref, cw_ref, cb_ref, u_ref, gc_ref, h_ref):
        i = pl.program_id(0)
        a = f_ref[...]
        aext = jnp.concatenate([jnp.where(i > 0, fp_ref[...], jnp.zeros_like(fp_ref[...])), a,
                                jnp.where(i < ni - 1, fn_ref[...], jnp.zeros_like(fn_ref[...]))], axis=0)
        for j in range(nc):
            cols = slice(j * tc, (j + 1) * tc)
            vcols = slice(DFF + j * tc, DFF + (j + 1) * tc)
            gext = jnp.dot(aext, w_ref[:, cols], preferred_element_type=F32)
            val = jnp.dot(a, w_ref[:, vcols], preferred_element_type=F32)
            gate = gext[halo:halo + tm]
            gc = (pltpu.roll(gext, 1, axis=0)[halo:halo + tm] * cw_ref[0:1, cols] + gate * cw_ref[1:2, cols]
                  + pltpu.roll(gext, rows - 1, axis=0)[halo:halo + tm] * cw_ref[2:3, cols]) + cb_ref[:, cols]
            u_ref[:, cols] = gate
            u_ref[:, vcols] = val
            gc_ref[:, cols] = gc
            h_ref[:, cols] = (gc * _sigmoid(gc) * val).astype(BF16)

    return pl.pallas_call(
        body, grid=(ni,),
        in_specs=[pl.BlockSpec((tm, d), lambda i: (i, 0)),
                  pl.BlockSpec((halo, d), lambda i: (jnp.maximum(i * r - 1, 0), 0)),
                  pl.BlockSpec((halo, d), lambda i: (jnp.minimum((i + 1) * r, last), 0)),
                  pl.BlockSpec(w_up.shape, lambda i: (0, 0)), pl.BlockSpec((3, DFF), lambda i: (0, 0)),
                  pl.BlockSpec((1, DFF), lambda i: (0, 0))],
        out_specs=[pl.BlockSpec((tm, 2 * DFF), lambda i: (i, 0)), pl.BlockSpec((tm, DFF), lambda i: (i, 0)),
                   pl.BlockSpec((tm, DFF), lambda i: (i, 0))],
        out_shape=[jax.ShapeDtypeStruct((n, 2 * DFF), F32), jax.ShapeDtypeStruct((n, DFF), F32),
                   jax.ShapeDtypeStruct((n, DFF), BF16)], name=name,
        compiler_params=_params("parallel"))(f, f, f, w_up, cw, cb)


def _ffn_down_glu_bwd(dz, w_down, u, gc, cw, *, name, tm=256, tc=256):
    n, d = dz.shape
    tm = min(tm, n)
    ni = n // tm
    nc = DFF // tc
    rows = tm + 2 * HALO

    def body(z_ref, zp_ref, zn_ref, w_ref, u_ref, vp_ref, vn_ref, c_ref, cp_ref, cn_ref, cw_ref,
             dg_ref, dv_ref, dcw_ref, dcb_ref):
        i = pl.program_id(0)

        @pl.when(i == 0)
        def _():
            dcw_ref[...] = jnp.zeros_like(dcw_ref)
            dcb_ref[...] = jnp.zeros_like(dcb_ref)

        zext = jnp.concatenate([jnp.where(i > 0, zp_ref[...], jnp.zeros_like(zp_ref[...])), z_ref[...],
                                jnp.where(i < ni - 1, zn_ref[...], jnp.zeros_like(zn_ref[...]))], axis=0)
        for j in range(nc):
            cols = slice(j * tc, (j + 1) * tc)
            vcols = slice(DFF + j * tc, DFF + (j + 1) * tc)
            dh = lax.dot_general(zext, w_ref[cols, :], _NT, preferred_element_type=F32)[HALO:HALO + rows]
            gcx = jnp.concatenate([cp_ref[:, cols], c_ref[:, cols], cn_ref[:, cols]], axis=0)
            vext = jnp.concatenate([vp_ref[:, cols], u_ref[:, vcols], vn_ref[:, cols]], axis=0)
            sg = _sigmoid(gcx)
            dgc = dh * vext * (sg * (1.0 + gcx * (1.0 - sg)))
            dv_ref[:, cols] = (dh[HALO:HALO + tm] * (gcx[HALO:HALO + tm] * sg[HALO:HALO + tm])).astype(BF16)
            d_next = pltpu.roll(dgc, rows - 1, axis=0)[HALO:HALO + tm]
            d_prev = pltpu.roll(dgc, 1, axis=0)[HALO:HALO + tm]
            d_here = dgc[HALO:HALO + tm]
            dg_ref[:, cols] = (d_next * cw_ref[0:1, cols] + d_here * cw_ref[1:2, cols]
                               + d_prev * cw_ref[2:3, cols]).astype(BF16)
            gate = u_ref[:, cols]
            dcw_ref[:, cols] += jnp.concatenate([_colsum(d_next * gate), _colsum(d_here * gate),
                                                 _colsum(d_prev * gate)], axis=0)
            dcb_ref[:, cols] += _colsum(d_here)

    def trio(width, halo, tile_width=None, colblk=0):
        r, last = tm // halo, n // halo - 1
        return [pl.BlockSpec((tm, tile_width or width), lambda i: (i, 0)),
                pl.BlockSpec((halo, width), lambda i: (jnp.maximum(i * r - 1, 0), colblk)),
                pl.BlockSpec((halo, width), lambda i: (jnp.minimum((i + 1) * r, last), colblk))]

    whole = lambda shape: pl.BlockSpec(shape, lambda i: (0, 0))
    return pl.pallas_call(
        body, grid=(ni,),
        in_specs=(trio(d, 16) + [whole(w_down.shape)] + trio(DFF, HALO, tile_width=2 * DFF, colblk=1)
                  + trio(DFF, HALO) + [whole((3, DFF))]),
        out_specs=[pl.BlockSpec((tm, DFF), lambda i: (i, 0)), pl.BlockSpec((tm, DFF), lambda i: (i, 0)),
                   whole((3, DFF)), whole((1, DFF))],
        out_shape=[jax.ShapeDtypeStruct((n, DFF), BF16), jax.ShapeDtypeStruct((n, DFF), BF16),
                   jax.ShapeDtypeStruct((3, DFF), F32), jax.ShapeDtypeStruct((1, DFF), F32)],
        name=name, compiler_params=_params("arbitrary"))(dz, dz, dz, w_down, u, u, u, gc, gc, gc, cw)


def _rope_tables(n):
    rows = n // GRID_W
    axis_dim = HD // 2
    inv_freq = jnp.power(ROPE_THETA, -jnp.arange(0, axis_dim, 2, dtype=F32) / axis_dim)
    ar = jnp.arange(rows, dtype=F32)[:, None] * inv_freq
    ac = jnp.arange(GRID_W, dtype=F32)[:, None] * inv_freq
    by_row = lambda a: jnp.repeat(a, GRID_W, axis=0)
    by_col = lambda a: jnp.tile(a, (rows, 1))
    cr, sr, cc, sc = by_row(jnp.cos(ar)), by_row(jnp.sin(ar)), by_col(jnp.cos(ac)), by_col(jnp.sin(ac))
    return jnp.concatenate([cr, cr, cc, cc], axis=1), jnp.concatenate([-sr, sr, -sc, sc], axis=1)


def _partner(v):
    lane = lax.broadcasted_iota(jnp.int32, v.shape, 1)
    return jnp.where((lane % 64) < 32, pltpu.roll(v, HD - 32, axis=1), pltpu.roll(v, 32, axis=1))


def _qkv_prep(p, q_gain, k_gain, cs, sn, *, name, has_q, kv_col, kv_rows=None, kv_row_off=0, kv_into=None, tm=256):
    n = p.shape[0]
    rope = cs is not None
    kv_rows = kv_rows or n
    rb = kv_row_off // tm

    def body(*refs):
        it = iter(refs)
        q_ref = next(it) if has_q else None
        kv_ref = next(it)
        qg_ref, kg_ref = next(it), next(it)
        cs_ref = next(it) if rope else None
        sn_ref = next(it) if rope else None
        if kv_into is not None:
            next(it), next(it)
        qo_ref = next(it) if has_q else None
        ko_ref, vo_ref = next(it), next(it)

        def norm_rope(xh, gain, mul=None):
            r = lax.rsqrt(jnp.mean(xh * xh, axis=-1, keepdims=True) + EPS)
            xn = (xh * r) * gain
            if rope:
                xn = xn * cs_ref[...] + _partner(xn) * sn_ref[...]
            if mul is not None:
                xn = xn * mul
            return xn.astype(BF16)

        if has_q:
            for h in range(NQ):
                qo_ref[h] = norm_rope(q_ref[:, h * HD:(h + 1) * HD], qg_ref[...], _QSCALE)
        for h in range(NKV):
            ko_ref[h] = norm_rope(kv_ref[:, h * HD:(h + 1) * HD], kg_ref[...])
            vo_ref[h] = kv_ref[:, (NKV + h) * HD:(NKV + h + 1) * HD].astype(BF16)

    in_specs, args = [], []
    if has_q:
        in_specs.append(pl.BlockSpec((tm, AW), lambda i: (i, 0)))
        args.append(p)
    in_specs += [pl.BlockSpec((tm, 2 * NKV * HD), lambda i: (i, kv_col)), _vec(HD), _vec(HD)]
    args += [p, q_gain, k_gain]
    if rope:
        in_specs += [pl.BlockSpec((tm, HD), lambda i: (i, 0))] * 2
        args += [cs, sn]
    out_specs, out_shape = [], []
    if has_q:
        out_specs.append(pl.BlockSpec((NQ, tm, HD), lambda i: (0, i, 0)))
        out_shape.append(jax.ShapeDtypeStruct((NQ, n, HD), BF16))
    out_specs += [pl.BlockSpec((NKV, tm, HD), lambda i: (0, rb + i, 0))] * 2
    out_shape += [jax.ShapeDtypeStruct((NKV, kv_rows, HD), BF16)] * 2
    aliases = {}
    if kv_into is not None:
        aliases = {len(args): int(has_q), len(args) + 1: int(has_q) + 1}
        in_specs += [pl.BlockSpec(memory_space=pl.ANY)] * 2
        args += list(kv_into)
    return pl.pallas_call(body, grid=(n // tm,), in_specs=in_specs, out_specs=out_specs, out_shape=out_shape,
                          input_output_aliases=aliases, name=name, compiler_params=_params("parallel"))(*args)


def _qkv_bwd(p, dq, dk, dv, q_gain, k_gain, cs, sn, *, name, has_q, kv_col, kv_row_off, tm=256):
    n = p.shape[0]
    rope = cs is not None
    rb = kv_row_off // tm

    def body(*refs):
        it = iter(refs)
        q_ref = next(it) if has_q else None
        kv_ref = next(it)
        dq_ref = next(it) if has_q else None
        dk_ref, dv_ref = next(it), next(it)
        qg_ref, kg_ref = next(it), next(it)
        cs_ref = next(it) if rope else None
        sn_ref = next(it) if rope else None
        dp_ref, dqg_ref, dkg_ref = next(it), next(it), next(it)
        i = pl.program_id(0)

        def back(xh, dout, gain):
            if rope:
                dout = dout * cs_ref[...] + _partner(dout * sn_ref[...])
            r = lax.rsqrt(jnp.mean(xh * xh, axis=-1, keepdims=True) + EPS)
            xhat = xh * r
            dxh = dout * gain
            dx = r * (dxh - xhat * jnp.mean(dxh * xhat, axis=-1, keepdims=True))
            return dx, _colsum(dout * xhat)

        dqg = jnp.zeros((1, HD), F32)
        dkg = jnp.zeros((1, HD), F32)
        if has_q:
            for h in range(NQ):
                dx, dg = back(q_ref[:, h * HD:(h + 1) * HD], dq_ref[h], qg_ref[...])
                dp_ref[:, h * HD:(h + 1) * HD] = dx.astype(BF16)
                dqg = dqg + dg
        else:
            dp_ref[:, 0:AW] = jnp.zeros((tm, AW), BF16)
        for h in range(NKV):
            dx, dg = back(kv_ref[:, h * HD:(h + 1) * HD], dk_ref[h], kg_ref[...])
            dp_ref[:, AW + h * HD:AW + (h + 1) * HD] = dx.astype(BF16)
            dkg = dkg + dg
            dp_ref[:, AW + (NKV + h) * HD:AW + (NKV + h + 1) * HD] = dv_ref[h].astype(BF16)
        _acc_out(dqg_ref, i, dqg)
        _acc_out(dkg_ref, i, dkg)

    in_specs, args = [], []
    if has_q:
        in_specs.append(pl.BlockSpec((tm, AW), lambda i: (i, 0)))
        args.append(p)
    in_specs.append(pl.BlockSpec((tm, 2 * NKV * HD), lambda i: (i, kv_col)))
    args.append(p)
    if has_q:
        in_specs.append(pl.BlockSpec((NQ, tm, HD), lambda i: (0, i, 0)))
        args.append(dq)
    in_specs += [pl.BlockSpec((NKV, tm, HD), lambda i: (0, rb + i, 0))] * 2 + [_vec(HD), _vec(HD)]
    args += [dk, dv, q_gain, k_gain]
    if rope:
        in_specs += [pl.BlockSpec((tm, HD), lambda i: (i, 0))] * 2
        args += [cs, sn]
    return pl.pallas_call(
        body, grid=(n // tm,), in_specs=in_specs,
        out_specs=[pl.BlockSpec((tm, D), lambda i: (i, 0)), _vec(HD), _vec(HD)],
        out_shape=[jax.ShapeDtypeStruct((n, D), BF16), jax.ShapeDtypeStruct((1, HD), F32),
                   jax.ShapeDtypeStruct((1, HD), F32)],
        name=name, compiler_params=_params("arbitrary"))(*args)


def _conv_gate_fwd(p, o, conv_w, *, name, tm=256):
    n = p.shape[0]
    ni = n // tm

    def body(gb_ref, gc_ref, gcp_ref, gcn_ref, xi_ref, xip_ref, xin_ref, o_ref, w_ref, cat_ref):
        i = pl.program_id(0)
        hext = _ext(gcp_ref, gc_ref, gcn_ref, i, ni) * _ext(xip_ref, xi_ref, xin_ref, i, ni)
        cat_ref[:, 0:AW] = o_ref[...].astype(BF16)
        cat_ref[:, AW:D] = (gb_ref[...] * _conv3(hext, w_ref, tm)).astype(BF16)

    gcp, gcn = _halo_specs(tm, CW, n, colblk=3)
    xip, xin = _halo_specs(tm, CW, n, colblk=4)
    return pl.pallas_call(
        body, grid=(ni,),
        in_specs=[pl.BlockSpec((tm, CW), lambda i: (i, 2)), pl.BlockSpec((tm, CW), lambda i: (i, 3)), gcp, gcn,
                  pl.BlockSpec((tm, CW), lambda i: (i, 4)), xip, xin, pl.BlockSpec((tm, AW), lambda i: (i, 0)),
                  pl.BlockSpec((3, CW), lambda i: (0, 0))],
        out_specs=pl.BlockSpec((tm, D), lambda i: (i, 0)), out_shape=jax.ShapeDtypeStruct((n, D), BF16),
        name=name, compiler_params=_params("parallel"))(p, p, p, p, p, p, p, o, conv_w)


def _conv_gate_bwd(dcat, p, conv_w, *, name, tm=256):
    n = p.shape[0]
    ni = n // tm

    def body(dc_ref, dcp_ref, dcn_ref, gb_ref, gbp_ref, gbn_ref, gc_ref, gcp_ref, gcn_ref, xi_ref, xip_ref, xin_ref,
             w_ref, dp_ref, dw_ref):
        i = pl.program_id(0)
        gcext = _ext(gcp_ref, gc_ref, gcn_ref, i, ni)
        xiext = _ext(xip_ref, xi_ref, xin_ref, i, ni)
        hext = gcext * xiext
        dcv = _ext(dcp_ref, dc_ref, dcn_ref, i, ni) * _ext(gbp_ref, gb_ref, gbn_ref, i, ni)
        dp_ref[:, 0:CW] = (dc_ref[...] * _conv3(hext, w_ref, tm)).astype(BF16)
        dh = _sh(dcv, 1, tm) * w_ref[0:1, :] + _sh(dcv, 0, tm) * w_ref[1:2, :] + _sh(dcv, -1, tm) * w_ref[2:3, :]
        dp_ref[:, CW:2 * CW] = (dh * xi_ref[...]).astype(BF16)
        dp_ref[:, 2 * CW:3 * CW] = (dh * gc_ref[...]).astype(BF16)
        dcv_t = dcv[HALO:HALO + tm]
        dw = jnp.concatenate([_colsum(dcv_t * _sh(hext, -1, tm)), _colsum(dcv_t * _sh(hext, 0, tm)),
                              _colsum(dcv_t * _sh(hext, 1, tm))], axis=0)
        _acc_out(dw_ref, i, dw)

    def trio(colblk):
        prev, nxt = _halo_specs(tm, CW, n, colblk=colblk)
        return [pl.BlockSpec((tm, CW), lambda i: (i, colblk)), prev, nxt]

    return pl.pallas_call(
        body, grid=(ni,), in_specs=trio(1) + trio(2) + trio(3) + trio(4) + [pl.BlockSpec((3, CW), lambda i: (0, 0))],
        out_specs=[pl.BlockSpec((tm, 3 * CW), lambda i: (i, 0)), pl.BlockSpec((3, CW), lambda i: (0, 0))],
        out_shape=[jax.ShapeDtypeStruct((n, 3 * CW), BF16), jax.ShapeDtypeStruct((3, CW), F32)],
        name=name, compiler_params=_params("arbitrary"))(dcat, dcat, dcat, p, p, p, p, p, p, p, p, p, conv_w)


def _attn_fwd(q, k, v, *, name, bq=512, sub=256):
    n = q.shape[1]
    t = k.shape[1]
    bq = min(bq, n)
    sub = min(sub, 2 * bq)

    def body(q_ref, k_ref, v_ref, o_ref, lse_ref):
        q2 = q_ref[...].reshape(2 * bq, HD)
        outs, lses = [], []
        for r0 in range(0, 2 * bq, sub):
            s = lax.dot_general(q2[r0:r0 + sub], k_ref[0], _NT, preferred_element_type=F32)
            m = jnp.max(s, axis=-1, keepdims=True)
            pv = jnp.exp2(s - m)
            l = jnp.sum(pv, axis=-1, keepdims=True)
            outs.append(jnp.dot(pv.astype(BF16), v_ref[0], preferred_element_type=F32) / l)
            lses.append(m + jnp.log2(l))
        out = jnp.concatenate(outs, axis=0)
        o_ref[:, 0:HD] = out[0:bq]
        o_ref[:, HD:2 * HD] = out[bq:2 * bq]
        lse_ref[...] = jnp.concatenate(lses, axis=0).reshape(2, bq, 1)

    kspec = pl.BlockSpec((1, t, HD), lambda h, i: (h, 0, 0))
    return pl.pallas_call(
        body, grid=(NKV, n // bq),
        in_specs=[pl.BlockSpec((2, bq, HD), lambda h, i: (h, i, 0)), kspec, kspec],
        out_specs=[pl.BlockSpec((bq, 2 * HD), lambda h, i: (i, h)), pl.BlockSpec((2, bq, 1), lambda h, i: (h, i, 0))],
        out_shape=[jax.ShapeDtypeStruct((n, AW), F32), jax.ShapeDtypeStruct((NQ, n, 1), F32)],
        name=name, compiler_params=_params("parallel", "parallel"))(q, k, v)


def _attn_bwd(q, k, v, dcat, o, lse, *, name, bq=256):
    n = q.shape[1]
    t = k.shape[1]
    bq = min(bq, n)

    def body(q_ref, k_ref, v_ref, dc_ref, o_ref, lse_ref, dq_ref, dk_ref, dv_ref):
        @pl.when(pl.program_id(1) == 0)
        def _():
            dk_ref[...] = jnp.zeros_like(dk_ref)
            dv_ref[...] = jnp.zeros_like(dv_ref)

        q2 = q_ref[...].reshape(2 * bq, HD)
        do_f = jnp.concatenate([dc_ref[:, 0:HD], dc_ref[:, HD:2 * HD]], axis=0)
        o_f = jnp.concatenate([o_ref[:, 0:HD], o_ref[:, HD:2 * HD]], axis=0)
        delta = jnp.sum(do_f * o_f, axis=-1, keepdims=True)
        do2 = do_f.astype(BF16)
        s = lax.dot_general(q2, k_ref[0], _NT, preferred_element_type=F32)
        pv = jnp.exp2(s - lse_ref[...].reshape(2 * bq, 1))
        dp = lax.dot_general(do2, v_ref[0], _NT, preferred_element_type=F32)
        ds = (pv * (dp - delta)).astype(BF16)
        dq_ref[...] = (jnp.dot(ds, k_ref[0], preferred_element_type=F32) * _SCALE).reshape(2, bq, HD)
        dk_ref[0] += lax.dot_general(ds, q2, _TN, preferred_element_type=F32) * _LN2
        dv_ref[0] += lax.dot_general(pv.astype(BF16), do2, _TN, preferred_element_type=F32)

    qspec = pl.BlockSpec((2, bq, HD), lambda h, i: (h, i, 0))
    kspec = pl.BlockSpec((1, t, HD), lambda h, i: (h, 0, 0))
    sspec = pl.BlockSpec((2, bq, 1), lambda h, i: (h, i, 0))
    cspec = pl.BlockSpec((bq, 2 * HD), lambda h, i: (i, h))
    return pl.pallas_call(
        body, grid=(NKV, n // bq), in_specs=[qspec, kspec, kspec, cspec, cspec, sspec], out_specs=[qspec, kspec, kspec],
        out_shape=[jax.ShapeDtypeStruct((NQ, n, HD), F32), jax.ShapeDtypeStruct((NKV, t, HD), F32),
                   jax.ShapeDtypeStruct((NKV, t, HD), F32)],
        name=name, compiler_params=_params("parallel", "arbitrary"))(q, k, v, dcat, o, lse)


def _window_sums(ext, w):
    s, step = ext, 1
    while step < w:
        s = s + _roll_rows(s, step)
        step *= 2
    return s


def _pool_counts(i, tm, n, w, rows, first):
    t = i * tm - HALO + first + lax.broadcasted_iota(jnp.int32, (rows, 1), 0)
    lo = jnp.clip(t - w // 2, 0, n)
    hi = jnp.clip(t + w - w // 2, 0, n)
    return jnp.maximum(hi - lo, 1).astype(F32)


def _norm_mod_ext(xext, gain_ref, sc_ref, sh_ref, i, tm, n):
    rows = xext.shape[0]
    t = i * tm - HALO + lax.broadcasted_iota(jnp.int32, (rows, 1), 0)
    inside = (t >= 0) & (t < n)
    r = lax.rsqrt(jnp.mean(xext * xext, axis=-1, keepdims=True) + EPS)
    xh = xext * r
    a = (xh * gain_ref[...]) * (1.0 + sc_ref[...]) + sh_ref[...]
    return jnp.where(inside, a, 0.0), r, xh


def _pool_fwd(x, y, g, gain, sc, sh, pool_w, *, name, tm=256):
    n, d = x.shape
    ni = n // tm

    def body(x_ref, xp_ref, xn_ref, y_ref, yp_ref, yn_ref, g_ref, gain_ref, sc_ref, sh_ref, w_ref, xo_ref, o_ref):
        i = pl.program_id(0)
        xext = _ext(xp_ref, x_ref, xn_ref, i, ni) + g_ref[...] * _ext(yp_ref, y_ref, yn_ref, i, ni)
        xo_ref[...] = xext[HALO:HALO + tm]
        aext, _, _ = _norm_mod_ext(xext, gain_ref, sc_ref, sh_ref, i, tm, n)
        for gi, w in enumerate(POOL_WINDOWS):
            ag = aext[:, gi * PG:(gi + 1) * PG]
            mean = _sh(_window_sums(ag, w), -(w // 2), tm) / _pool_counts(i, tm, n, w, tm, HALO)
            pooled = mean - ag[HALO:HALO + tm]
            o_ref[:, gi * PG:(gi + 1) * PG] = jnp.dot(pooled.astype(BF16), w_ref[gi], preferred_element_type=F32)

    row = pl.BlockSpec((tm, d), lambda i: (i, 0))
    prev, nxt = _halo_specs(tm, d, n)
    return pl.pallas_call(
        body, grid=(ni,),
        in_specs=[row, prev, nxt, row, prev, nxt, _vec(d), _vec(d), _vec(d), _vec(d),
                  pl.BlockSpec((4, PG, PG), lambda i: (0, 0, 0))],
        out_specs=[row, row], out_shape=[jax.ShapeDtypeStruct((n, d), F32)] * 2,
        name=name, compiler_params=_params("parallel"))(x, x, x, y, y, y, g, gain, sc, sh, pool_w)


def _pool_bwd(dxo, mixed, x, g, scale, gain, sc, sh, pool_w, zprev, gprev, *, name, tm=256):
    n, d = x.shape
    ni = n // tm

    def body(dx_ref, dxp_ref, dxn_ref, mx_ref, x_ref, xp_ref, xn_ref, g_ref, s_ref, gain_ref, sc_ref, sh_ref, w_ref,
             zp_ref, gp_ref, dxi_ref, dw_ref, dg_ref, dsl_ref, dsh_ref, dsc_ref, dgn_ref, dzp_ref, dgp_ref):
        i = pl.program_id(0)

        @pl.when(i == 0)
        def _():
            dw_ref[...] = jnp.zeros_like(dw_ref)

        dxo_t = dx_ref[...]
        mixed_t = mx_ref[...]
        dy_t = dxo_t * g_ref[...]
        _acc_out(dg_ref, i, _colsum(dxo_t * (mixed_t * s_ref[...])))
        _acc_out(dsl_ref, i, _colsum(dy_t * mixed_t))
        dmixed = (_ext(dxp_ref, dx_ref, dxn_ref, i, ni) * g_ref[...]) * s_ref[...]
        xext = _ext(xp_ref, x_ref, xn_ref, i, ni)
        aext, rext, xhext = _norm_mod_ext(xext, gain_ref, sc_ref, sh_ref, i, tm, n)
        rows = tm + 2 * HALO
        da_parts = []
        for gi, w in enumerate(POOL_WINDOWS):
            sl = slice(gi * PG, (gi + 1) * PG)
            ag = aext[:, sl]
            mean = _sh(_window_sums(ag, w), -(w // 2), tm) / _pool_counts(i, tm, n, w, tm, HALO)
            pooled = (mean - ag[HALO:HALO + tm]).astype(BF16)
            dmg = dmixed[:, sl].astype(BF16)
            dw_ref[gi] += lax.dot_general(pooled, dmixed[HALO:HALO + tm, sl].astype(BF16), _TN,
                                          preferred_element_type=F32)
            dpl = lax.dot_general(dmg, w_ref[gi], _NT, preferred_element_type=F32)
            e = dpl / _pool_counts(i, tm, n, w, rows, 0)
            da_parts.append(_sh(_window_sums(e, w), 1 - w // 2, tm) - dpl[HALO:HALO + tm])
        da = jnp.concatenate(da_parts, axis=1)
        r = rext[HALO:HALO + tm]
        xh = xhext[HALO:HALO + tm]
        nrm = xh * gain_ref[...]
        dn = da * (1.0 + sc_ref[...])
        dxh = dn * gain_ref[...]
        dxi = dxo_t + r * (dxh - xh * jnp.mean(dxh * xh, axis=-1, keepdims=True))
        dxi_ref[...] = dxi
        _acc_out(dsh_ref, i, _colsum(da))
        _acc_out(dsc_ref, i, _colsum(da * nrm))
        _acc_out(dgn_ref, i, _colsum(dn * xh))
        dzp_ref[...] = (dxi * gp_ref[...]).astype(BF16)
        _acc_out(dgp_ref, i, _colsum(dxi * zp_ref[...]))

    row = pl.BlockSpec((tm, d), lambda i: (i, 0))
    prev, nxt = _halo_specs(tm, d, n)
    wspec = pl.BlockSpec((4, PG, PG), lambda i: (0, 0, 0))
    vshape = jax.ShapeDtypeStruct((1, d), F32)
    return pl.pallas_call(
        body, grid=(ni,),
        in_specs=[row, prev, nxt, row, row, prev, nxt] + [_vec(d)] * 5 + [wspec, row, _vec(d)],
        out_specs=[row, wspec] + [_vec(d)] * 5 + [row, _vec(d)],
        out_shape=[jax.ShapeDtypeStruct((n, d), F32), jax.ShapeDtypeStruct((4, PG, PG), F32)] + [vshape] * 5
        + [jax.ShapeDtypeStruct((n, d), BF16), vshape],
        name=name, compiler_params=_params("arbitrary"))(dxo, dxo, dxo, mixed, x, x, x, g, scale, gain, sc, sh, pool_w,
                                                         zprev, gprev)


def _adamw(gparts_list, w, m, v, *, name, silu_grad_of=None):
    nl = len(gparts_list)
    nparts, r, c = gparts_list[0].shape
    tr = _pick(r, (256, 128, 64, 32, 16, 8))
    has_c = silu_grad_of is not None

    def body(*refs):
        gp_refs = refs[:nl]
        it = iter(refs[nl:])
        w_ref, m_ref, v_ref = next(it), next(it), next(it)
        c_ref = next(it) if has_c else None
        g_ref, d_ref, mo_ref, vo_ref = next(it), next(it), next(it), next(it)
        layer = pl.program_id(0)

        def update(gp_ref):
            g = gp_ref[0].astype(F32)
            for p in range(1, nparts):
                g = g + gp_ref[p].astype(F32)
            if has_c:
                cv = c_ref[0]
                sg = _sigmoid(cv)
                g = g * (sg * (1.0 + cv * (1.0 - sg)))
            g_ref[0] = g
            mn = ADAM_B1 * m_ref[0] + (1.0 - ADAM_B1) * g
            vn = ADAM_B2 * v_ref[0] + (1.0 - ADAM_B2) * (g * g)
            m_hat = mn / (1.0 - ADAM_B1 ** ADAM_STEP)
            v_hat = vn / (1.0 - ADAM_B2 ** ADAM_STEP)
            d_ref[0] = -ADAM_LR * (m_hat / (jnp.sqrt(v_hat) + ADAM_EPS) + ADAM_WD * w_ref[0])
            mo_ref[0] = mn
            vo_ref[0] = vn

        if nl == 1:
            update(gp_refs[0])
        else:
            for li in range(nl):
                pl.when(layer == li)(functools.partial(update, gp_refs[li]))

    row = pl.BlockSpec((1, tr, c), lambda l, i: (l, i, 0))
    in_specs = [pl.BlockSpec((nparts, tr, c), lambda l, i, li=li: (0, jnp.where(l == li, i, 0), 0)) for li in range(nl)]
    in_specs += [row, row, row]
    args = list(gparts_list) + [w, m, v]
    if has_c:
        in_specs.append(row)
        args.append(silu_grad_of)
    return pl.pallas_call(
        body, grid=(nl, r // tr), in_specs=in_specs, out_specs=[row] * 4,
        out_shape=[jax.ShapeDtypeStruct((nl, r, c), F32)] * 4, name=name,
        compiler_params=_params("arbitrary", "arbitrary"))(*args)


def _adamw_nd(gparts, w, m, v, *, name, silu_grad_of=None):
    shape = w.shape
    c = shape[-1]
    if isinstance(gparts, (list, tuple)):
        nl = len(gparts)
        r = math.prod(shape[1:-1])
    else:
        nl = 1
        r = math.prod(shape[:-1]) if len(shape) > 1 else 1
        gparts = [gparts]
    rs = lambda a: a.reshape(nl, r, c)
    res = _adamw([gp.reshape(gp.shape[0], r, c) for gp in gparts], rs(w), rs(m), rs(v), name=name,
                 silu_grad_of=None if silu_grad_of is None else rs(silu_grad_of))
    return [a.reshape(shape) for a in res]


def _place():
    return lax.axis_index("x"), lax.axis_index("y"), lax.axis_index("c")


def _all_gather(arrs, *, name):
    k_arr = len(arrs)

    def body(*refs):
        ins = refs[:k_arr]
        outs = refs[k_arr:2 * k_arr]
        send_sems, recv_sems, local_sems = refs[2 * k_arr:]
        x, y, c = _place()
        me, sibling = (x, y, c), (x, y, 1 - c)
        chips = [(1 - x, y), (x, 1 - y), (1 - x, 1 - y)]

        def slot(a, px, py, pc):
            return outs[a].at[4 * px + 2 * py + pc]

        def copy(a, s, block, to, src=None):
            return pltpu.make_async_remote_copy(
                src_ref=slot(a, *block) if src is None else src, dst_ref=slot(a, *block),
                send_sem=send_sems.at[a, s], recv_sem=recv_sems.at[a, s], device_id=to, device_id_type=MESH)

        mine = [pltpu.make_async_copy(ins[a], slot(a, *me), local_sems.at[a]) for a in range(k_arr)]
        for cp in mine:
            cp.start()
        first = []
        for a in range(k_arr):
            first.append(copy(a, 0, me, sibling, src=ins[a]))
            first += [copy(a, 1 + j, me, (*chip, c), src=ins[a]) for j, chip in enumerate(chips)]
        for cp in first:
            cp.start()
        passed = []
        for j, chip in enumerate(chips):
            for a in range(k_arr):
                copy(a, 1 + j, (*chip, c), me).wait_recv()
                fw = copy(a, 4 + j, (*chip, c), sibling)
                fw.start()
                passed.append(fw)
        for a in range(k_arr):
            copy(a, 0, sibling, me).wait_recv()
            for j, chip in enumerate(chips):
                copy(a, 4 + j, (*chip, 1 - c), me).wait_recv()
        for cp in first + passed:
            cp.wait_send()
        for cp in mine:
            cp.wait()

    any_spec = pl.BlockSpec(memory_space=pl.ANY)
    return pl.pallas_call(
        body, in_specs=[any_spec] * k_arr, out_specs=[any_spec] * k_arr,
        out_shape=[jax.ShapeDtypeStruct((NDEV,) + a.shape, a.dtype) for a in arrs],
        scratch_shapes=[pltpu.SemaphoreType.DMA((k_arr, 7)), pltpu.SemaphoreType.DMA((k_arr, 7)),
                        pltpu.SemaphoreType.DMA((k_arr,))],
        name=name)(*arrs)


_HBM = pl.BlockSpec(memory_space=pltpu.HBM)
_SEM = pl.BlockSpec(memory_space=pltpu.SEMAPHORE)
_EFFECT = pltpu.SideEffectType.DATAFLOW_SIDE_EFFECTING


def _peers(x, y, c):
    return [(x ^ (rel >> 2), y ^ ((rel >> 1) & 1), c ^ (rel & 1)) for rel in range(1, NDEV)]


def _exchange_copies(srcs, lands, send_sems, recv_sems, scatter):
    x, y, c = _place()
    me = 4 * x + 2 * y + c
    copies = []
    for r, (px, py, pc) in enumerate(_peers(x, y, c)):
        peer = 4 * px + 2 * py + pc
        for a in range(len(srcs)):
            copies.append(pltpu.make_async_remote_copy(
                src_ref=srcs[a].at[peer] if scatter else srcs[a], dst_ref=lands[a].at[me],
                send_sem=send_sems.at[7 * a + r], recv_sem=recv_sems.at[7 * a + r], device_id=(px, py, pc),
                device_id_type=MESH))
    return copies


def _exchange_start(arrs, *, scatter, name):
    k_arr = len(arrs)
    land_shapes = [a.shape if scatter else (NDEV,) + a.shape for a in arrs]
    lands = [pltpu.with_memory_space_constraint(lax.empty(s, a.dtype), pltpu.HBM) for s, a in zip(land_shapes, arrs)]
    srcs = [pltpu.with_memory_space_constraint(a, pltpu.HBM) for a in arrs]

    def body(*refs):
        src_refs, land_refs = refs[:k_arr], refs[k_arr:2 * k_arr]
        send_sems, recv_sems = refs[2 * k_arr], refs[2 * k_arr + 1]
        token = refs[-1]
        for cp in _exchange_copies(src_refs, land_refs, send_sems, recv_sems, scatter):
            cp.start()
        token[...] = jnp.zeros_like(token)

    out_shape = ([pltpu.SemaphoreType.DMA((7 * k_arr,)), pltpu.SemaphoreType.DMA((7 * k_arr,))]
                 + [pltpu.HBM(a.shape, a.dtype) for a in arrs] + [pltpu.HBM(s, a.dtype) for s, a in zip(land_shapes, arrs)]
                 + [jax.ShapeDtypeStruct((8, 128), F32)])
    res = pl.pallas_call(
        body, name=name, out_shape=out_shape, in_specs=[_HBM] * (2 * k_arr),
        out_specs=[_SEM, _SEM] + [_HBM] * (2 * k_arr) + [pl.BlockSpec(memory_space=pltpu.VMEM)],
        input_output_aliases={i: 2 + i for i in range(2 * k_arr)},
        compiler_params=pltpu.CompilerParams(has_side_effects=_EFFECT))(*srcs, *lands)
    return dict(send=res[0], recv=res[1], srcs=list(res[2:2 + k_arr]), lands=list(res[2 + k_arr:2 + 2 * k_arr]),
                token=res[-1], scatter=scatter)


def _exchange_wait(handle, after, *, name):
    k_arr = len(handle["srcs"])
    scatter = handle["scatter"]

    def body(*refs):
        src_refs, land_refs = refs[:k_arr], refs[k_arr:2 * k_arr]
        send_sems, recv_sems = refs[2 * k_arr], refs[2 * k_arr + 1]
        x, y, c = _place()
        me = 4 * x + 2 * y + c
        for r, (px, py, pc) in enumerate(_peers(x, y, c)):
            peer = 4 * px + 2 * py + pc
            for a in range(k_arr):
                cp = pltpu.make_async_remote_copy(
                    src_ref=src_refs[a].at[peer] if scatter else src_refs[a], dst_ref=land_refs[a].at[peer],
                    send_sem=send_sems.at[7 * a + r], recv_sem=recv_sems.at[7 * a + r], device_id=(x, y, c),
                    device_id_type=MESH)
                cp.wait_send()
                cp.wait_recv()

    arrs = handle["srcs"] + handle["lands"]
    res = pl.pallas_call(
        body, name=name, out_shape=[pltpu.HBM(a.shape, a.dtype) for a in arrs],
        in_specs=[_HBM] * (2 * k_arr) + [_SEM, _SEM, pl.BlockSpec(memory_space=pl.ANY)],
        out_specs=[_HBM] * (2 * k_arr), input_output_aliases={i: i for i in range(2 * k_arr)},
        compiler_params=pltpu.CompilerParams(has_side_effects=_EFFECT))(*arrs, handle["send"], handle["recv"], after)
    me = 4 * lax.axis_index("x") + 2 * lax.axis_index("y") + lax.axis_index("c")
    out = []
    for src, land in zip(res[:k_arr], res[k_arr:]):
        own = lax.dynamic_index_in_dim(src, me, 0, keepdims=False) if scatter else src
        out.append(lax.dynamic_update_index_in_dim(land, own, me, 0))
    return out


def _ffn_bwd(dxo, dz, xr, f, u_gc, hmid, gain, sc, w_up, cw, w_down, tag, gate_y=None, gate_g=None):
    d_wdown = _mm_tn((hmid, dz), name=f"ffn_down_dw_{tag}")
    dug, duv, dcw, dcb = _ffn_down_glu_bwd(dz, w_down, u_gc[0], u_gc[1], cw, name=f"ffn_down_glu_bwd_{tag}")
    d_wup_g = _mm_tn((f, dug), name=f"ffn_up_dwg_{tag}")
    d_wup_v = _mm_tn((f, duv), name=f"ffn_up_dwv_{tag}")
    gated = gate_y is not None
    res = _mm_w_ep([dug, duv], w_up, _ep_norm_bwd(True, gated), [xr, dxo] + ([gate_y] if gated else []),
                   [gain, sc] + ([gate_g] if gated else []), [F32] + ([BF16] if gated else []),
                   [D] * (4 if gated else 3), tb=True, name=f"ffn_up_dx_norm_bwd_{tag}")
    n_out = 2 if gated else 1
    return res[:n_out], res[n_out:], (d_wup_g, d_wup_v, d_wdown, dcw, dcb)


def _split6(mod):
    return [mod[j * D:(j + 1) * D][None, :] for j in range(6)]


def _row(v):
    return v.reshape(1, -1)


def kernel(x, c, ctx, c_ctx, ada_w, ada_b, mix_norm, ffn_norm, even_w_in, even_q_gain, even_k_gain, even_conv_w, even_w_out, odd_pool_w, odd_pool_scale, ffn_w_up, ffn_conv_w, ffn_conv_b, ffn_w_down, loss_target, m_c_ctx, m_ada_w, m_ada_b, m_mix_norm, m_ffn_norm, m_even_w_in, m_even_q_gain, m_even_k_gain, m_even_conv_w, m_even_w_out, m_odd_pool_w, m_odd_pool_scale, m_ffn_w_up, m_ffn_conv_w, m_ffn_conv_b, m_ffn_w_down, v_c_ctx, v_ada_w, v_ada_b, v_mix_norm, v_ffn_norm, v_even_w_in, v_even_q_gain, v_even_k_gain, v_even_conv_w, v_even_w_out, v_odd_pool_w, v_odd_pool_scale, v_ffn_w_up, v_ffn_conv_w, v_ffn_conv_b, v_ffn_w_down):
    n = x.shape[1]
    lc = ctx.shape[1]
    me = 4 * lax.axis_index("x") + 2 * lax.axis_index("y") + lax.axis_index("c")
    xs, ctxs, tgt = x[0], ctx[0], loss_target[0]
    acols = ada_w.shape[2]

    small = jnp.concatenate([even_conv_w.reshape(-1), ffn_conv_w.reshape(-1), odd_pool_scale.reshape(-1)])
    nsmall = small.shape[0]
    small = jnp.pad(small, (0, (-nsmall) % 1024)).reshape(-1, 128)
    c_rows = jnp.pad(c, ((0, 7), (0, 0)))
    g_c, g_win, g_small = _all_gather([c_rows, even_w_in[0].astype(BF16), small], name="gather_first")
    w_in = g_win.transpose(1, 0, 2).reshape(D, -1)
    g_small = g_small.reshape(NDEV, -1)
    ecw = even_conv_w.shape[2]
    fcw = ffn_conv_w.shape[2]
    conv_w = g_small[:, :3 * ecw].reshape(NDEV, 3, ecw).transpose(1, 0, 2).reshape(3, CW)
    o1 = 3 * ecw
    fconv_w = g_small[:, o1:o1 + 6 * fcw].reshape(NDEV, 2, 3, fcw).transpose(1, 2, 0, 3).reshape(2, 3, DFF)
    o2 = o1 + 6 * fcw
    pool_scale = g_small[:, o2:o2 + D // NDEV].reshape(1, D)

    mraw = jnp.concatenate([g_c[:, 0, :], c_ctx[None, :], jnp.zeros((7, D), F32)], axis=0)
    my_bias = lax.dynamic_slice_in_dim(ada_b, me * acols, acols, axis=1)
    modp = jnp.stack([_mm(mraw, ada_w[l], silu_a=True, bias=my_bias[l:l + 1], name=f"ada_proj_{l}", tm=16, tn=256)
                      for l in range(2)])
    (g_mod,) = _all_gather([modp], name="gather_mod")
    mod_rows = g_mod.transpose(1, 2, 0, 3).reshape(2, 16, 6 * D)
    late_shards = [even_w_out[0].astype(BF16), odd_pool_w[0].astype(BF16), ffn_w_up.astype(BF16),
                   ffn_w_down.astype(BF16)]
    late_shards, mod_rows = lax.optimization_barrier((late_shards, mod_rows))
    h_weights = _exchange_start(late_shards, scatter=False, name="weights_start")
    mod_rows = mod_rows + h_weights["token"][0, 0]
    mod = lax.dynamic_index_in_dim(mod_rows, me, axis=1, keepdims=False)
    sh1, sc1, g1, sh2, sc2, g2 = _split6(mod[0])
    sh1b, sc1b, g1b, sh2b, sc2b, g2b = _split6(mod[1])
    csh1, csc1 = _split6(mod_rows[0, 8])[:2]
    mixn = [_row(mix_norm[l]) for l in range(2)]
    ffnn = [_row(ffn_norm[l]) for l in range(2)]
    qg, kg = _row(even_q_gain[0]), _row(even_k_gain[0])
    fcb = [_row(ffn_conv_b[l]) for l in range(2)]

    cs_t, sn_t = _rope_tables(n)
    a_lat = _norm_mod(xs, mixn[0], sc1, sh1, name="mix0_norm")
    a_ctx = _norm_mod(ctxs, mixn[0], csc1, csh1, name="mix0_norm_ctx")
    p_lat = _mm_w(a_lat, w_in, name="in_proj")
    p_ctx = _mm(a_ctx, w_in[:, AW:AW + 4 * HD], name="in_proj_ctx", tm=256, tn=512, tk=1024)
    kv_ctx = _qkv_prep(p_ctx, qg, kg, None, None, has_q=False, kv_col=0, kv_rows=lc + n, name="qkv_prep_ctx")
    q_r, k_all, v_all = _qkv_prep(p_lat, qg, kg, cs_t, sn_t, has_q=True, kv_col=1, kv_rows=lc + n, kv_row_off=lc,
                                  kv_into=kv_ctx, name="qkv_prep")
    o_attn, lse = _attn_fwd(q_r, k_all, v_all, name="attn_fwd")
    cat = _conv_gate_fwd(p_lat, o_attn, conv_w, name="conv_gate")
    g_wout, g_pool, g_up, g_down = _exchange_wait(h_weights, cat, name="weights_wait")
    w_out = g_wout.reshape(D, D)
    pool_w = g_pool.transpose(1, 0, 2, 3).reshape(4, PG, PG)
    w_up = [g_up[:, l].transpose(1, 0, 2).reshape(D, 2 * DFF) for l in range(2)]
    w_down = [g_down[:, l].reshape(DFF, D) for l in range(2)]
    y0, x1, f0 = _mm_w_ep(cat, w_out, _ep_resid_norm, [xs], [g1, ffnn[0], sc2, sh2], [F32, F32, BF16], [],
                          name="out_proj_norm")[:3]
    *u0, h0 = _ffn_up_glu(f0, w_up[0], fconv_w[0], fcb[0], name="ffn_up_glu_l0")
    z0 = _mm_w(h0, w_down[0], name="ffn_down_l0")

    x2, mixed = _pool_fwd(x1, z0, g2, mixn[1], sc1b, sh1b, pool_w, name="pool_fwd")
    x3, f1 = _norm_mod(x2, ffnn[1], sc2b, sh2b, y=mixed, g=g1b, ymul=pool_scale, name="ffn_norm_l1")
    *u1, h1 = _ffn_up_glu(f1, w_up[1], fconv_w[1], fcb[1], name="ffn_up_glu_l1")
    dx4, dz1, loss_part, dg2b = _mm_w_ep(h1, w_down[1], _ep_loss(D), [x3, tgt], [g2b], [F32, BF16], [128, D],
                                         name="ffn_down_loss")
    loss = lax.psum(loss_part[0, 0], ("x", "y", "c"))

    (dx3,), (dsh2b, dsc2b, dffn1), (dup1g, dup1v, ddown1, dfcw1, dfcb1) = _ffn_bwd(
        dx4, dz1, x3, f1, u1, h1, ffnn[1], sc2b, w_up[1], fconv_w[1], w_down[1], "l1")
    dx2, dpool_w, dg1b, dpscale, dsh1b, dsc1b, dmix1, dz0, dg2 = _pool_bwd(
        dx3, mixed, x2, g1b, pool_scale, mixn[1], sc1b, sh1b, pool_w, z0, g2, name="pool_bwd")

    def up_shards(dg, dv):
        return jnp.concatenate([dg, dv], axis=1).reshape(D, NDEV, -1).transpose(1, 0, 2)

    s_pool = dpool_w.astype(BF16).reshape(4, NDEV, PG // NDEV, PG).transpose(1, 0, 2, 3)
    h_g1 = _exchange_start([s_pool, up_shards(dup1g, dup1v), ddown1.reshape(NDEV, DFF // NDEV, D)], scatter=True,
                           name="grads1_start")

    (dx1, dy0), (dsh2, dsc2, dffn0, dg1), (dup0g, dup0v, ddown0, dfcw0, dfcb0) = _ffn_bwd(
        dx2, dz0, x1, f0, u0, h0, ffnn[0], sc2, w_up[0], fconv_w[0] + h_g1["token"][0, 0], w_down[0], "l0",
        gate_y=y0, gate_g=g1)
    h_g0 = _exchange_start([up_shards(dup0g, dup0v), ddown0.reshape(NDEV, DFF // NDEV, D)], scatter=True,
                           name="grads0_start")
    dcat = _mm_w(dy0, w_out, tb=True, name="out_proj_dx", tm=512)
    d_wout = _mm_tn((cat, dy0), name="out_proj_dw")
    dp_conv, dconv_w = _conv_gate_bwd(dcat, p_lat, conv_w + h_g0["token"][0, 0], name="conv_gate_bwd")
    dq_r, dk_all, dv_all = _attn_bwd(q_r, k_all, v_all, dcat, o_attn, lse, name="attn_bwd")
    dp_qkv, dqg_l, dkg_l = _qkv_bwd(p_lat, dq_r, dk_all, dv_all, qg, kg, cs_t, sn_t, has_q=True, kv_col=1,
                                    kv_row_off=lc, name="qkv_bwd")
    dp_ctx, _zero_qg, dkg_c = _qkv_bwd(p_ctx, None, dk_all, dv_all, qg, kg, None, None, has_q=False, kv_col=0,
                                       kv_row_off=0, name="qkv_bwd_ctx")
    da_ctx = _mm(dp_ctx, w_in[:, :D], tb=True, name="in_proj_dx_ctx", tm=256, tn=512, tk=1024)
    d_win_qkv = _mm_tn([(a_lat, dp_qkv), (a_ctx, dp_ctx)], name="in_proj_dw_qkv")
    d_win_conv = _mm_tn((a_lat, dp_conv), name="in_proj_dw_conv")
    d_win = jnp.concatenate([d_win_qkv, d_win_conv], axis=1)
    grad_x, dsh1, dsc1, dmix0 = _mm_w_ep([dp_qkv, dp_conv], w_in, _ep_norm_bwd(True, False), [xs, dx1],
                                         [mixn[0], sc1], [F32], [D] * 3, tb=True, name="in_proj_dx_norm_bwd")
    _dctx, dcsh1, dcsc1, dmix0c = _norm_mod_bwd(da_ctx, ctxs, mixn[0], csc1, name="mix0_norm_bwd_ctx")

    z1k = jnp.zeros((1, D), F32)
    pack = jnp.concatenate(
        [v.reshape(-1) for v in (dsh1, dsc1, dg1, dsh2, dsc2, dg2, dsh1b, dsc1b, dg1b, dsh2b, dsc2b, dg2b,
                                 dcsh1, dcsc1, z1k, z1k, z1k, z1k,
                                 dmix0, dmix1, dmix0c, z1k, dffn0, dffn1, dqg_l, dkg_l + dkg_c,
                                 dfcb0, dfcb1, dconv_w, dfcw0, dfcw1, dpscale)])
    npack = pack.shape[0]
    pack = jnp.pad(pack, (0, (-npack) % 1024)).reshape(-1, 128)
    (g_pack,) = _all_gather([pack], name="gather_small_grads")
    gp = g_pack.reshape(NDEV, -1)
    off = [0]

    def take(size):
        seg = gp[:, off[0]:off[0] + size]
        off[0] += size
        return seg

    dmod_all = take(12 * D).reshape(NDEV, 2, 6 * D)
    dmodc_all = take(6 * D).reshape(NDEV, 1, 6 * D)
    dmix_all = take(4 * D).reshape(NDEV, 2, 2, D)
    dffn_all = take(2 * D).reshape(NDEV, 2, D)
    dqg_all = take(HD).reshape(NDEV, 1, HD)
    dkg_all = take(HD).reshape(NDEV, 1, HD)
    dfcb_all = take(2 * DFF).reshape(NDEV, 2, DFF)
    dconvw_all = take(3 * CW).reshape(NDEV, 3, CW)
    dfcw_all = take(6 * DFF).reshape(NDEV, 2, 3, DFF)
    dpscale_all = take(D).reshape(NDEV, D)

    dmodc_sum = dmodc_all[0]
    for dev in range(1, NDEV):
        dmodc_sum = dmodc_sum + dmodc_all[dev]
    my_cols = lambda a: lax.dynamic_slice_in_dim(a, me * acols, acols, axis=a.ndim - 1)
    rows0 = jnp.concatenate([my_cols(dmod_all[:, 0]), my_cols(dmodc_sum), jnp.zeros((7, acols), F32)], axis=0)
    rows1 = jnp.concatenate([my_cols(dmod_all[:, 1]), jnp.zeros((8, acols), F32)], axis=0)
    d_ada = jnp.stack([_mm(mraw, rows, ta=True, silu_a=True, name=f"ada_dw_{l}", tm=512, tn=256, tk=16)
                       for l, rows in enumerate((rows0, rows1))])
    dscc_part = _mm(rows0, ada_w[0], tb=True, name="ada_dcctx", tm=16, tn=512, tk=256)
    (g_dscc,) = _all_gather([dscc_part[8:16]], name="gather_dcctx")

    attn_shards = [d_win.reshape(D, NDEV, -1).transpose(1, 0, 2), d_wout.reshape(NDEV, D // NDEV, D)]
    attn_shards, g_dscc = lax.optimization_barrier((attn_shards, g_dscc))
    h_ga = _exchange_start(attn_shards, scatter=True, name="grads_attn_start")
    dmod_all = dmod_all + h_ga["token"][0, 0]

    outs = {}

    def put(nm, res):
        outs["grad_" + nm], outs["delta_" + nm], outs["new_m_" + nm], outs["new_v_" + nm] = res

    dmodc_pad = jnp.concatenate([dmodc_all, jnp.zeros_like(dmodc_all)], axis=1)
    put("ada_b", _adamw_nd(jnp.concatenate([dmod_all, dmodc_pad], axis=0), ada_b, m_ada_b, v_ada_b, name="adam_ada_b"))
    put("mix_norm", _adamw_nd(jnp.concatenate([dmix_all[:, 0], dmix_all[:, 1]], axis=0), mix_norm, m_mix_norm,
                              v_mix_norm, name="adam_mix_norm"))
    put("ffn_norm", _adamw_nd(dffn_all, ffn_norm, m_ffn_norm, v_ffn_norm, name="adam_ffn_norm"))
    put("even_q_gain", _adamw_nd(dqg_all, even_q_gain, m_even_q_gain, v_even_q_gain, name="adam_q_gain"))
    put("even_k_gain", _adamw_nd(dkg_all, even_k_gain, m_even_k_gain, v_even_k_gain, name="adam_k_gain"))
    put("ffn_conv_b", _adamw_nd(dfcb_all, ffn_conv_b, m_ffn_conv_b, v_ffn_conv_b, name="adam_ffn_conv_b"))
    my_convw = lax.dynamic_slice_in_dim(dconvw_all, me * ecw, ecw, axis=2)[:, None]
    put("even_conv_w", _adamw_nd(my_convw, even_conv_w, m_even_conv_w, v_even_conv_w, name="adam_even_conv_w"))
    my_fcw = lax.dynamic_slice_in_dim(dfcw_all, me * fcw, fcw, axis=3)
    put("ffn_conv_w", _adamw_nd(my_fcw, ffn_conv_w, m_ffn_conv_w, v_ffn_conv_w, name="adam_ffn_conv_w"))
    my_ps = lax.dynamic_slice_in_dim(dpscale_all, me * (D // NDEV), D // NDEV, axis=1)[:, None]
    put("odd_pool_scale", _adamw_nd(my_ps, odd_pool_scale, m_odd_pool_scale, v_odd_pool_scale, name="adam_pool_scale"))

    put("ada_w", _adamw_nd(d_ada[None], ada_w, m_ada_w, v_ada_w, name="adam_ada_w"))
    put("c_ctx", _adamw_nd(g_dscc[:, 0:1, :].reshape(NDEV, D), c_ctx, m_c_ctx, v_c_ctx, name="adam_c_ctx",
                           silu_grad_of=c_ctx))

    r_pool, r_up1, r_down1 = _exchange_wait(h_g1, outs["grad_ada_b"], name="grads1_wait")
    r_up0, r_down0 = _exchange_wait(h_g0, outs["grad_mix_norm"], name="grads0_wait")
    r_win, r_wout = _exchange_wait(h_ga, outs["grad_c_ctx"], name="grads_attn_wait")
    put("even_w_in", _adamw_nd(r_win[:, None], even_w_in, m_even_w_in, v_even_w_in, name="adam_w_in"))
    put("even_w_out", _adamw_nd(r_wout[:, None], even_w_out, m_even_w_out, v_even_w_out, name="adam_w_out"))
    put("odd_pool_w", _adamw_nd(r_pool[:, None], odd_pool_w, m_odd_pool_w, v_odd_pool_w, name="adam_pool_w"))
    put("ffn_w_up", _adamw_nd([r_up0, r_up1], ffn_w_up, m_ffn_w_up, v_ffn_w_up, name="adam_w_up"))
    put("ffn_w_down", _adamw_nd([r_down0, r_down1], ffn_w_down, m_ffn_w_down, v_ffn_w_down, name="adam_w_down"))

    names = ["c_ctx", "ada_w", "ada_b", "mix_norm", "ffn_norm", "even_w_in", "even_q_gain", "even_k_gain",
             "even_conv_w", "even_w_out", "odd_pool_w", "odd_pool_scale", "ffn_w_up", "ffn_conv_w", "ffn_conv_b",
             "ffn_w_down"]
    result = [loss, grad_x[None]]
    for kind in ("grad_", "delta_", "new_m_", "new_v_"):
        result += [outs[kind + nm] for nm in names]
    return tuple(result)
```

```python
import functools
import math

import jax
import jax.numpy as jnp
from jax import lax
from jax.experimental import pallas as pl
from jax.experimental.pallas import tpu as pltpu

F32 = jnp.float32
BF16 = jnp.bfloat16

D = 1024
HD = 128
NQ = 4
NKV = 2
AW = NQ * HD
CW = D - AW
DFF = 2816
GRID_W = 64
ROPE_THETA = 10000.0
POOL_WINDOWS = (2, 4, 8, 16)
PG = D // 4
EPS = 1e-6
NDEV = 8
HALO = 8
MESH = pl.DeviceIdType.MESH

ADAM_LR = 0.001
ADAM_B1 = 0.9
ADAM_B2 = 0.999
ADAM_EPS = 1e-08
ADAM_WD = 0.01
ADAM_STEP = 10


def _pick(dim, prefs):
    for p in prefs:
        if dim % p == 0:
            return p
    return dim


def _params(*sem):
    return pltpu.CompilerParams(dimension_semantics=sem)


_NT = (((1,), (1,)), ((), ()))
_TN = (((0,), (0,)), ((), ()))
_SCALE = HD ** -0.5
_QSCALE = _SCALE * math.log2(math.e)
_LN2 = math.log(2.0)


def _mm(a_list, b, *, name, ta=False, tb=False, out_dtype=F32, silu_a=False, bias=None, tm=None, tn=None, tk=None):
    if not isinstance(a_list, (list, tuple)):
        a_list = [a_list]
    na = len(a_list)
    assert not (ta and na > 1)
    if ta:
        kdim, m = a_list[0].shape
        ks = [kdim]
    else:
        m = a_list[0].shape[0]
        ks = [a.shape[1] for a in a_list]
        kdim = sum(ks)
    n = b.shape[0] if tb else b.shape[1]
    assert (b.shape[1] if tb else b.shape[0]) == kdim
    kunit = math.gcd(*ks) if na > 1 else kdim
    tm = min(tm, m) if tm else _pick(m, (512, 256, 128, 64, 32, 16, 8))
    tn = min(tn, n) if tn else _pick(n, (512, 256, 128))
    tk = min(tk, kunit) if tk else _pick(kunit, (1024, 768, 512, 256, 128))
    assert m % tm == 0 and n % tn == 0 and all(k % tk == 0 for k in ks)
    nks = [k // tk for k in ks]
    starts = [sum(nks[:i]) for i in range(na)]
    nk = sum(nks)
    has_bias = bias is not None

    def body(*refs):
        a_refs = refs[:na]
        b_ref = refs[na]
        bias_ref = refs[na + 1] if has_bias else None
        o_ref = refs[na + 1 + has_bias]
        acc = refs[-1]
        k = pl.program_id(2)

        @pl.when(k == 0)
        def _():
            acc[...] = jnp.zeros_like(acc)

        bv = b_ref[...].astype(BF16)
        dn = (((0 if ta else 1,), (1 if tb else 0,)), ((), ()))
        for idx in range(na):
            def step(idx=idx):
                av = a_refs[idx][...]
                if silu_a:
                    av = av * jax.nn.sigmoid(av)
                acc[...] += lax.dot_general(av.astype(BF16), bv, dn, preferred_element_type=F32)
            if na == 1:
                step()
            else:
                pl.when((k >= starts[idx]) & (k < starts[idx] + nks[idx]))(step)

        @pl.when(k == nk - 1)
        def _():
            r = acc[...]
            if has_bias:
                r = r + bias_ref[...]
            o_ref[...] = r.astype(o_ref.dtype)

    in_specs = []
    for idx in range(na):
        if ta:
            in_specs.append(pl.BlockSpec((tk, tm), lambda i, j, k: (k, i)))
        else:
            lo, cnt = starts[idx], nks[idx]
            in_specs.append(pl.BlockSpec((tm, tk), lambda i, j, k, lo=lo, cnt=cnt: (i, jnp.clip(k - lo, 0, cnt - 1))))
    if tb:
        in_specs.append(pl.BlockSpec((tn, tk), lambda i, j, k: (j, k)))
    else:
        in_specs.append(pl.BlockSpec((tk, tn), lambda i, j, k: (k, j)))
    args = list(a_list) + [b]
    if has_bias:
        in_specs.append(pl.BlockSpec((1, tn), lambda i, j, k: (0, j)))
        args.append(bias)
    return pl.pallas_call(
        body, grid=(m // tm, n // tn, nk), in_specs=in_specs,
        out_specs=pl.BlockSpec((tm, tn), lambda i, j, k: (i, j)),
        out_shape=jax.ShapeDtypeStruct((m, n), out_dtype),
        scratch_shapes=[pltpu.VMEM((tm, tn), F32)], name=name,
        compiler_params=_params("parallel", "parallel", "arbitrary"))(*args)


def _mm_w(a_list, w, *, name, tb=False, tm=256, out_dtype=F32):
    if not isinstance(a_list, (list, tuple)):
        a_list = [a_list]
    na = len(a_list)
    m = a_list[0].shape[0]
    ks = [a.shape[1] for a in a_list]
    offs = [sum(ks[:i]) for i in range(na)]
    n = w.shape[0] if tb else w.shape[1]
    assert (w.shape[1] if tb else w.shape[0]) == sum(ks)
    tm = min(tm, m)
    assert m % tm == 0

    def body(*refs):
        a_refs, w_ref, o_ref = refs[:na], refs[na], refs[na + 1]
        acc = None
        for idx in range(na):
            av = a_refs[idx][...].astype(BF16)
            if tb:
                part = lax.dot_general(av, w_ref[:, offs[idx]:offs[idx] + ks[idx]], _NT, preferred_element_type=F32)
            else:
                part = jnp.dot(av, w_ref[offs[idx]:offs[idx] + ks[idx], :], preferred_element_type=F32)
            acc = part if acc is None else acc + part
        o_ref[...] = acc.astype(o_ref.dtype)

    in_specs = [pl.BlockSpec((tm, k), lambda i: (i, 0)) for k in ks] + [pl.BlockSpec(w.shape, lambda i: (0, 0))]
    return pl.pallas_call(
        body, grid=(m // tm,), in_specs=in_specs, out_specs=pl.BlockSpec((tm, n), lambda i: (i, 0)),
        out_shape=jax.ShapeDtypeStruct((m, n), out_dtype), name=name, compiler_params=_params("parallel"))(*a_list, w)


def _mm_w_ep(a_list, w, epilogue, row_in, vec_in, out_dtypes, sum_widths, *, name, tb=False, tm=256, sub=128):
    if not isinstance(a_list, (list, tuple)):
        a_list = [a_list]
    na, nr, nv, no, ns = len(a_list), len(row_in), len(vec_in), len(out_dtypes), len(sum_widths)
    m = a_list[0].shape[0]
    ks = [a.shape[1] for a in a_list]
    offs = [sum(ks[:i]) for i in range(na)]
    n = w.shape[0] if tb else w.shape[1]
    assert (w.shape[1] if tb else w.shape[0]) == sum(ks)
    tm = min(tm, m)
    sub = min(sub, tm)
    assert m % tm == 0 and tm % sub == 0

    def body(*refs):
        a_refs, w_ref = refs[:na], refs[na]
        row_refs = refs[na + 1:na + 1 + nr]
        vec_refs = refs[na + 1 + nr:na + 1 + nr + nv]
        out_refs = refs[na + 1 + nr + nv:na + 1 + nr + nv + no]
        sum_refs = refs[na + 1 + nr + nv + no:]

        @pl.when(pl.program_id(0) == 0)
        def _():
            for s_ref in sum_refs:
                s_ref[...] = jnp.zeros_like(s_ref)

        vecs = [v[...] for v in vec_refs]
        for r0 in range(0, tm, sub):
            acc = None
            for idx in range(na):
                av = a_refs[idx][r0:r0 + sub, :].astype(BF16)
                if tb:
                    part = lax.dot_general(av, w_ref[:, offs[idx]:offs[idx] + ks[idx]], _NT, preferred_element_type=F32)
                else:
                    part = jnp.dot(av, w_ref[offs[idx]:offs[idx] + ks[idx], :], preferred_element_type=F32)
                acc = part if acc is None else acc + part
            outs, sums = epilogue(acc, [r[r0:r0 + sub, :] for r in row_refs], vecs)
            for o_ref, o in zip(out_refs, outs):
                o_ref[r0:r0 + sub, :] = o.astype(o_ref.dtype)
            for s_ref, s in zip(sum_refs, sums):
                s_ref[...] += s

    row = pl.BlockSpec((tm, n), lambda i: (i, 0))
    in_specs = ([pl.BlockSpec((tm, k), lambda i: (i, 0)) for k in ks] + [pl.BlockSpec(w.shape, lambda i: (0, 0))]
                + [row] * nr + [_vec(n)] * nv)
    return pl.pallas_call(
        body, grid=(m // tm,), in_specs=in_specs, out_specs=[row] * no + [_vec(sw) for sw in sum_widths],
        out_shape=[jax.ShapeDtypeStruct((m, n), dt) for dt in out_dtypes]
        + [jax.ShapeDtypeStruct((1, sw), F32) for sw in sum_widths],
        name=name, compiler_params=_params("arbitrary" if ns else "parallel"))(*a_list, w, *row_in, *vec_in)


def _ep_loss(d):
    def ep(zv, rows, vecs):
        xv, tv = rows
        gv = vecs[0]
        diff = (xv + gv * zv) - tv
        dx = diff * (1.0 / d)
        part = 0.5 * jnp.sum(jnp.mean(diff * diff, axis=-1, keepdims=True), axis=0, keepdims=True)
        return [dx, dx * gv], [jnp.broadcast_to(part, (1, 128)), _colsum(dx * zv)]
    return ep


def _ep_resid_norm(yv, rows, vecs):
    g, gain, scv, shv = vecs
    xv = rows[0] + g * yv
    r = lax.rsqrt(jnp.mean(xv * xv, axis=-1, keepdims=True) + EPS)
    return [yv, xv, ((xv * r) * gain) * (1.0 + scv) + shv], []


def _mm_tn(pairs, *, name, tk=1024, out_dtype=BF16):
    if not isinstance(pairs, list):
        pairs = [pairs]
    m, n = pairs[0][0].shape[1], pairs[0][1].shape[1]
    tks = [min(tk, a.shape[0]) for a, _ in pairs]
    nks = [a.shape[0] // t for (a, _), t in zip(pairs, tks)]
    assert all(a.shape[0] == b.shape[0] and a.shape[0] % t == 0 for (a, b), t in zip(pairs, tks))
    starts = [sum(nks[:i]) for i in range(len(pairs))]
    nk = sum(nks)

    def body(*refs):
        o_ref, acc = refs[-2], refs[-1]
        k = pl.program_id(0)

        @pl.when(k == 0)
        def _():
            acc[...] = jnp.zeros_like(acc)

        for idx in range(len(pairs)):
            a_ref, b_ref = refs[2 * idx], refs[2 * idx + 1]

            def step(a_ref=a_ref, b_ref=b_ref):
                acc[...] += lax.dot_general(a_ref[...], b_ref[...], _TN, preferred_element_type=F32)

            if len(pairs) == 1:
                step()
            else:
                pl.when((k >= starts[idx]) & (k < starts[idx] + nks[idx]))(step)

        @pl.when(k == nk - 1)
        def _():
            o_ref[...] = acc[...].astype(o_ref.dtype)

    in_specs, args = [], []
    for (a, b), t, lo, cnt in zip(pairs, tks, starts, nks):
        idx_map = lambda k, lo=lo, cnt=cnt: (jnp.clip(k - lo, 0, cnt - 1), 0)
        in_specs += [pl.BlockSpec((t, m), idx_map), pl.BlockSpec((t, n), idx_map)]
        args += [a, b]
    return pl.pallas_call(
        body, grid=(nk,), in_specs=in_specs, out_specs=pl.BlockSpec((m, n), lambda k: (0, 0)),
        out_shape=jax.ShapeDtypeStruct((m, n), out_dtype), scratch_shapes=[pltpu.VMEM((m, n), F32)], name=name,
        compiler_params=_params("arbitrary"))(*args)


def _vec(d, col=None):
    if col is None:
        return pl.BlockSpec((1, d), lambda i, *_: (0, 0))
    return pl.BlockSpec((1, d), col)


def _halo_specs(tm, width, nrows, colblk=0, row_off=0):
    r = tm // HALO
    off = row_off // HALO
    last = nrows // HALO - 1
    prev = pl.BlockSpec((HALO, width), lambda i, *_: (off + jnp.maximum(i * r - 1, 0), colblk))
    nxt = pl.BlockSpec((HALO, width), lambda i, *_: (off + jnp.minimum((i + 1) * r, last), colblk))
    return prev, nxt


def _ext(prev_ref, main_ref, next_ref, i, ni):
    p = jnp.where(i > 0, prev_ref[...], 0.0)
    n = jnp.where(i < ni - 1, next_ref[...], 0.0)
    return jnp.concatenate([p, main_ref[...], n], axis=0)


def _sh(ext, k, tm):
    if k == 0:
        return ext[HALO:HALO + tm]
    rows = ext.shape[0]
    return pltpu.roll(ext, (-k) % rows, axis=0)[HALO:HALO + tm]


def _roll_rows(v, k):
    rows = v.shape[0]
    return pltpu.roll(v, (-k) % rows, axis=0) if k % rows else v


def _conv3(ext, w_ref, tm):
    return _sh(ext, -1, tm) * w_ref[0:1, :] + _sh(ext, 0, tm) * w_ref[1:2, :] + _sh(ext, 1, tm) * w_ref[2:3, :]


def _colsum(v):
    return jnp.sum(v, axis=0, keepdims=True)


def _acc_out(ref, i, val):
    @pl.when(i == 0)
    def _():
        ref[...] = jnp.zeros_like(ref)

    ref[...] += val


def _sigmoid(v):
    return jax.nn.sigmoid(v)


def _norm_mod(x, gain, sc, sh, *, name, y=None, g=None, ymul=None, tm=512):
    n, d = x.shape
    tm = min(tm, n)
    has_res = y is not None
    has_mul = ymul is not None

    def body(*refs):
        it = iter(refs)
        x_ref = next(it)
        y_ref = next(it) if has_res else None
        g_ref = next(it) if has_res else None
        m_ref = next(it) if has_mul else None
        gain_ref, sc_ref, sh_ref = next(it), next(it), next(it)
        xo_ref = next(it) if has_res else None
        a_ref = next(it)
        xv = x_ref[...]
        if has_res:
            yv = y_ref[...]
            if has_mul:
                yv = yv * m_ref[...]
            xv = xv + g_ref[...] * yv
            xo_ref[...] = xv
        r = lax.rsqrt(jnp.mean(xv * xv, axis=-1, keepdims=True) + EPS)
        nrm = (xv * r) * gain_ref[...]
        a_ref[...] = (nrm * (1.0 + sc_ref[...]) + sh_ref[...]).astype(BF16)

    row = pl.BlockSpec((tm, d), lambda i: (i, 0))
    in_specs, args = [row], [x]
    if has_res:
        in_specs += [row, _vec(d)]
        args += [y, g]
    if has_mul:
        in_specs.append(_vec(d))
        args.append(ymul)
    in_specs += [_vec(d)] * 3
    args += [gain, sc, sh]
    out_specs, out_shape = [], []
    if has_res:
        out_specs.append(row)
        out_shape.append(jax.ShapeDtypeStruct((n, d), F32))
    out_specs.append(row)
    out_shape.append(jax.ShapeDtypeStruct((n, d), BF16))
    res = pl.pallas_call(body, grid=(n // tm,), in_specs=in_specs, out_specs=out_specs, out_shape=out_shape,
                         name=name, compiler_params=_params("parallel"))(*args)
    return res if has_res else res[0]


def _norm_mod_bwd(da, x, gain, sc, *, name, dres=None, gate_y=None, gate_g=None, tm=512):
    n, d = x.shape
    tm = min(tm, n)
    has_res = dres is not None
    has_gate = gate_y is not None

    def body(*refs):
        it = iter(refs)
        da_ref, x_ref = next(it), next(it)
        r_ref = next(it) if has_res else None
        y_ref = next(it) if has_gate else None
        g_ref = next(it) if has_gate else None
        gain_ref, sc_ref = next(it), next(it)
        dx_ref, dsh_ref, dsc_ref, dgn_ref = next(it), next(it), next(it), next(it)
        dy_ref = next(it) if has_gate else None
        dg_ref = next(it) if has_gate else None
        i = pl.program_id(0)
        xv = x_ref[...]
        dav = da_ref[...]
        r = lax.rsqrt(jnp.mean(xv * xv, axis=-1, keepdims=True) + EPS)
        xh = xv * r
        nrm = xh * gain_ref[...]
        dn = dav * (1.0 + sc_ref[...])
        dxh = dn * gain_ref[...]
        dx = r * (dxh - xh * jnp.mean(dxh * xh, axis=-1, keepdims=True))
        if has_res:
            dx = dx + r_ref[...]
        dx_ref[...] = dx
        _acc_out(dsh_ref, i, _colsum(dav))
        _acc_out(dsc_ref, i, _colsum(dav * nrm))
        _acc_out(dgn_ref, i, _colsum(dn * xh))
        if has_gate:
            dy_ref[...] = (dx * g_ref[...]).astype(BF16)
            _acc_out(dg_ref, i, _colsum(dx * y_ref[...]))

    row = pl.BlockSpec((tm, d), lambda i: (i, 0))
    in_specs, args = [row, row], [da, x]
    if has_res:
        in_specs.append(row)
        args.append(dres)
    if has_gate:
        in_specs += [row, _vec(d)]
        args += [gate_y, gate_g]
    in_specs += [_vec(d)] * 2
    args += [gain, sc]
    vec_shape = jax.ShapeDtypeStruct((1, d), F32)
    out_specs = [row, _vec(d), _vec(d), _vec(d)]
    out_shape = [jax.ShapeDtypeStruct((n, d), F32), vec_shape, vec_shape, vec_shape]
    if has_gate:
        out_specs += [row, _vec(d)]
        out_shape += [jax.ShapeDtypeStruct((n, d), BF16), vec_shape]
    return pl.pallas_call(
        body, grid=(n // tm,), in_specs=in_specs, out_specs=out_specs, out_shape=out_shape,
        name=name, compiler_params=_params("arbitrary"))(*args)


def _ffn_up_glu(f, w_up, cw, cb, *, name, tm=256, tc=256):
    n, d = f.shape
    tm = min(tm, n)
    ni = n // tm
    nc = DFF // tc
    halo = 16
    rows = tm + 2 * halo
    r = tm // halo
    last = n // halo - 1

    def body(f_ref, fp_ref, fn_ref, w_ref, cw_ref, cb_ref, u_ref, gc_ref, h_ref):
        i = pl.program_id(0)
        a = f_ref[...]
        aext = jnp.concatenate([jnp.where(i > 0, fp_ref[...], jnp.zeros_like(fp_ref[...])), a,
                                jnp.where(i < ni - 1, fn_ref[...], jnp.zeros_like(fn_ref[...]))], axis=0)
        for j in range(nc):
            cols = slice(j * tc, (j + 1) * tc)
            vcols = slice(DFF + j * tc, DFF + (j + 1) * tc)
            gext = jnp.dot(aext, w_ref[:, cols], preferred_element_type=F32)
            val = jnp.dot(a, w_ref[:, vcols], preferred_element_type=F32)
            gate = gext[halo:halo + tm]
            gc = (pltpu.roll(gext, 1, axis=0)[halo:halo + tm] * cw_ref[0:1, cols] + gate * cw_ref[1:2, cols]
                  + pltpu.roll(gext, rows - 1, axis=0)[halo:halo + tm] * cw_ref[2:3, cols]) + cb_ref[:, cols]
            u_ref[:, cols] = gate
            u_ref[:, vcols] = val
            gc_ref[:, cols] = gc
            h_ref[:, cols] = (gc * _sigmoid(gc) * val).astype(BF16)

    return pl.pallas_call(
        body, grid=(ni,),
        in_specs=[pl.BlockSpec((tm, d), lambda i: (i, 0)),
                  pl.BlockSpec((halo, d), lambda i: (jnp.maximum(i * r - 1, 0), 0)),
                  pl.BlockSpec((halo, d), lambda i: (jnp.minimum((i + 1) * r, last), 0)),
                  pl.BlockSpec(w_up.shape, lambda i: (0, 0)), pl.BlockSpec((3, DFF), lambda i: (0, 0)),
                  pl.BlockSpec((1, DFF), lambda i: (0, 0))],
        out_specs=[pl.BlockSpec((tm, 2 * DFF), lambda i: (i, 0)), pl.BlockSpec((tm, DFF), lambda i: (i, 0)),
                   pl.BlockSpec((tm, DFF), lambda i: (i, 0))],
        out_shape=[jax.ShapeDtypeStruct((n, 2 * DFF), F32), jax.ShapeDtypeStruct((n, DFF), F32),
                   jax.ShapeDtypeStruct((n, DFF), BF16)], name=name,
        compiler_params=_params("parallel"))(f, f, f, w_up, cw, cb)


def _ffn_down_glu_bwd(dz, w_down, u, gc, cw, *, name, tm=256, tc=256):
    n, d = dz.shape
    tm = min(tm, n)
    ni = n // tm
    nc = DFF // tc
    rows = tm + 2 * HALO

    def body(z_ref, zp_ref, zn_ref, w_ref, u_ref, vp_ref, vn_ref, c_ref, cp_ref, cn_ref, cw_ref,
             dg_ref, dv_ref, dcw_ref, dcb_ref):
        i = pl.program_id(0)

        @pl.when(i == 0)
        def _():
            dcw_ref[...] = jnp.zeros_like(dcw_ref)
            dcb_ref[...] = jnp.zeros_like(dcb_ref)

        zext = jnp.concatenate([jnp.where(i > 0, zp_ref[...], jnp.zeros_like(zp_ref[...])), z_ref[...],
                                jnp.where(i < ni - 1, zn_ref[...], jnp.zeros_like(zn_ref[...]))], axis=0)
        for j in range(nc):
            cols = slice(j * tc, (j + 1) * tc)
            vcols = slice(DFF + j * tc, DFF + (j + 1) * tc)
            dh = lax.dot_general(zext, w_ref[cols, :], _NT, preferred_element_type=F32)[HALO:HALO + rows]
            gcx = jnp.concatenate([cp_ref[:, cols], c_ref[:, cols], cn_ref[:, cols]], axis=0)
            vext = jnp.concatenate([vp_ref[:, cols], u_ref[:, vcols], vn_ref[:, cols]], axis=0)
            sg = _sigmoid(gcx)
            dgc = dh * vext * (sg * (1.0 + gcx * (1.0 - sg)))
            dv_ref[:, cols] = (dh[HALO:HALO + tm] * (gcx[HALO:HALO + tm] * sg[HALO:HALO + tm])).astype(BF16)
            d_next = pltpu.roll(dgc, rows - 1, axis=0)[HALO:HALO + tm]
            d_prev = pltpu.roll(dgc, 1, axis=0)[HALO:HALO + tm]
            d_here = dgc[HALO:HALO + tm]
            dg_ref[:, cols] = (d_next * cw_ref[0:1, cols] + d_here * cw_ref[1:2, cols]
                               + d_prev * cw_ref[2:3, cols]).astype(BF16)
            gate = u_ref[:, cols]
            dcw_ref[:, cols] += jnp.concatenate([_colsum(d_next * gate), _colsum(d_here * gate),
                                                 _colsum(d_prev * gate)], axis=0)
            dcb_ref[:, cols] += _colsum(d_here)

    def trio(width, halo, tile_width=None, colblk=0):
        r, last = tm // halo, n // halo - 1
        return [pl.BlockSpec((tm, tile_width or width), lambda i: (i, 0)),
                pl.BlockSpec((halo, width), lambda i: (jnp.maximum(i * r - 1, 0), colblk)),
                pl.BlockSpec((halo, width), lambda i: (jnp.minimum((i + 1) * r, last), colblk))]

    whole = lambda shape: pl.BlockSpec(shape, lambda i: (0, 0))
    return pl.pallas_call(
        body, grid=(ni,),
        in_specs=(trio(d, 16) + [whole(w_down.shape)] + trio(DFF, HALO, tile_width=2 * DFF, colblk=1)
                  + trio(DFF, HALO) + [whole((3, DFF))]),
        out_specs=[pl.BlockSpec((tm, DFF), lambda i: (i, 0)), pl.BlockSpec((tm, DFF), lambda i: (i, 0)),
                   whole((3, DFF)), whole((1, DFF))],
        out_shape=[jax.ShapeDtypeStruct((n, DFF), BF16), jax.ShapeDtypeStruct((n, DFF), BF16),
                   jax.ShapeDtypeStruct((3, DFF), F32), jax.ShapeDtypeStruct((1, DFF), F32)],
        name=name, compiler_params=_params("arbitrary"))(dz, dz, dz, w_down, u, u, u, gc, gc, gc, cw)


def _rope_tables(n):
    rows = n // GRID_W
    axis_dim = HD // 2
    inv_freq = jnp.power(ROPE_THETA, -jnp.arange(0, axis_dim, 2, dtype=F32) / axis_dim)
    ar = jnp.arange(rows, dtype=F32)[:, None] * inv_freq
    ac = jnp.arange(GRID_W, dtype=F32)[:, None] * inv_freq
    by_row = lambda a: jnp.repeat(a, GRID_W, axis=0)
    by_col = lambda a: jnp.tile(a, (rows, 1))
    cr, sr, cc, sc = by_row(jnp.cos(ar)), by_row(jnp.sin(ar)), by_col(jnp.cos(ac)), by_col(jnp.sin(ac))
    return jnp.concatenate([cr, cr, cc, cc], axis=1), jnp.concatenate([-sr, sr, -sc, sc], axis=1)


def _partner(v):
    lane = lax.broadcasted_iota(jnp.int32, v.shape, 1)
    return jnp.where((lane % 64) < 32, pltpu.roll(v, HD - 32, axis=1), pltpu.roll(v, 32, axis=1))


def _qkv_prep(p, q_gain, k_gain, cs, sn, *, name, has_q, kv_col, kv_rows=None, kv_row_off=0, kv_into=None, tm=256):
    n = p.shape[0]
    rope = cs is not None
    kv_rows = kv_rows or n
    rb = kv_row_off // tm

    def body(*refs):
        it = iter(refs)
        q_ref = next(it) if has_q else None
        kv_ref = next(it)
        qg_ref, kg_ref = next(it), next(it)
        cs_ref = next(it) if rope else None
        sn_ref = next(it) if rope else None
        if kv_into is not None:
            next(it), next(it)
        qo_ref = next(it) if has_q else None
        ko_ref, vo_ref = next(it), next(it)

        def norm_rope(xh, gain, mul=None):
            r = lax.rsqrt(jnp.mean(xh * xh, axis=-1, keepdims=True) + EPS)
            xn = (xh * r) * gain
            if rope:
                xn = xn * cs_ref[...] + _partner(xn) * sn_ref[...]
            if mul is not None:
                xn = xn * mul
            return xn.astype(BF16)

        if has_q:
            for h in range(NQ):
                qo_ref[h] = norm_rope(q_ref[:, h * HD:(h + 1) * HD], qg_ref[...], _QSCALE)
        for h in range(NKV):
            ko_ref[h] = norm_rope(kv_ref[:, h * HD:(h + 1) * HD], kg_ref[...])
            vo_ref[h] = kv_ref[:, (NKV + h) * HD:(NKV + h + 1) * HD].astype(BF16)

    in_specs, args = [], []
    if has_q:
        in_specs.append(pl.BlockSpec((tm, AW), lambda i: (i, 0)))
        args.append(p)
    in_specs += [pl.BlockSpec((tm, 2 * NKV * HD), lambda i: (i, kv_col)), _vec(HD), _vec(HD)]
    args += [p, q_gain, k_gain]
    if rope:
        in_specs += [pl.BlockSpec((tm, HD), lambda i: (i, 0))] * 2
        args += [cs, sn]
    out_specs, out_shape = [], []
    if has_q:
        out_specs.append(pl.BlockSpec((NQ, tm, HD), lambda i: (0, i, 0)))
        out_shape.append(jax.ShapeDtypeStruct((NQ, n, HD), BF16))
    out_specs += [pl.BlockSpec((NKV, tm, HD), lambda i: (0, rb + i, 0))] * 2
    out_shape += [jax.ShapeDtypeStruct((NKV, kv_rows, HD), BF16)] * 2
    aliases = {}
    if kv_into is not None:
        aliases = {len(args): int(has_q), len(args) + 1: int(has_q) + 1}
        in_specs += [pl.BlockSpec(memory_space=pl.ANY)] * 2
        args += list(kv_into)
    return pl.pallas_call(body, grid=(n // tm,), in_specs=in_specs, out_specs=out_specs, out_shape=out_shape,
                          input_output_aliases=aliases, name=name, compiler_params=_params("parallel"))(*args)


def _qkv_bwd(p, dq, dk, dv, q_gain, k_gain, cs, sn, *, name, has_q, kv_col, kv_row_off, tm=256):
    n = p.shape[0]
    rope = cs is not None
    rb = kv_row_off // tm

    def body(*refs):
        it = iter(refs)
        q_ref = next(it) if has_q else None
        kv_ref = next(it)
        dq_ref = next(it) if has_q else None
        dk_ref, dv_ref = next(it), next(it)
        qg_ref, kg_ref = next(it), next(it)
        cs_ref = next(it) if rope else None
        sn_ref = next(it) if rope else None
        dp_ref, dqg_ref, dkg_ref = next(it), next(it), next(it)
        i = pl.program_id(0)

        def back(xh, dout, gain):
            if rope:
                dout = dout * cs_ref[...] + _partner(dout * sn_ref[...])
            r = lax.rsqrt(jnp.mean(xh * xh, axis=-1, keepdims=True) + EPS)
            xhat = xh * r
            dxh = dout * gain
            dx = r * (dxh - xhat * jnp.mean(dxh * xhat, axis=-1, keepdims=True))
            return dx, _colsum(dout * xhat)

        dqg = jnp.zeros((1, HD), F32)
        dkg = jnp.zeros((1, HD), F32)
        if has_q:
            for h in range(NQ):
                dx, dg = back(q_ref[:, h * HD:(h + 1) * HD], dq_ref[h], qg_ref[...])
                dp_ref[:, h * HD:(h + 1) * HD] = dx.astype(BF16)
                dqg = dqg + dg
        else:
            dp_ref[:, 0:AW] = jnp.zeros((tm, AW), BF16)
        for h in range(NKV):
            dx, dg = back(kv_ref[:, h * HD:(h + 1) * HD], dk_ref[h], kg_ref[...])
            dp_ref[:, AW + h * HD:AW + (h + 1) * HD] = dx.astype(BF16)
            dkg = dkg + dg
            dp_ref[:, AW + (NKV + h) * HD:AW + (NKV + h + 1) * HD] = dv_ref[h].astype(BF16)
        _acc_out(dqg_ref, i, dqg)
        _acc_out(dkg_ref, i, dkg)

    in_specs, args = [], []
    if has_q:
        in_specs.append(pl.BlockSpec((tm, AW), lambda i: (i, 0)))
        args.append(p)
    in_specs.append(pl.BlockSpec((tm, 2 * NKV * HD), lambda i: (i, kv_col)))
    args.append(p)
    if has_q:
        in_specs.append(pl.BlockSpec((NQ, tm, HD), lambda i: (0, i, 0)))
        args.append(dq)
    in_specs += [pl.BlockSpec((NKV, tm, HD), lambda i: (0, rb + i, 0))] * 2 + [_vec(HD), _vec(HD)]
    args += [dk, dv, q_gain, k_gain]
    if rope:
        in_specs += [pl.BlockSpec((tm, HD), lambda i: (i, 0))] * 2
        args += [cs, sn]
    return pl.pallas_call(
        body, grid=(n // tm,), in_specs=in_specs,
        out_specs=[pl.BlockSpec((tm, D), lambda i: (i, 0)), _vec(HD), _vec(HD)],
        out_shape=[jax.ShapeDtypeStruct((n, D), BF16), jax.ShapeDtypeStruct((1, HD), F32),
                   jax.ShapeDtypeStruct((1, HD), F32)],
        name=name, compiler_params=_params("arbitrary"))(*args)


def _conv_gate_fwd(p, o, conv_w, *, name, tm=256):
    n = p.shape[0]
    ni = n // tm

    def body(gb_ref, gc_ref, gcp_ref, gcn_ref, xi_ref, xip_ref, xin_ref, o_ref, w_ref, cat_ref):
        i = pl.program_id(0)
        hext = _ext(gcp_ref, gc_ref, gcn_ref, i, ni) * _ext(xip_ref, xi_ref, xin_ref, i, ni)
        cat_ref[:, 0:AW] = o_ref[...].astype(BF16)
        cat_ref[:, AW:D] = (gb_ref[...] * _conv3(hext, w_ref, tm)).astype(BF16)

    gcp, gcn = _halo_specs(tm, CW, n, colblk=3)
    xip, xin = _halo_specs(tm, CW, n, colblk=4)
    return pl.pallas_call(
        body, grid=(ni,),
        in_specs=[pl.BlockSpec((tm, CW), lambda i: (i, 2)), pl.BlockSpec((tm, CW), lambda i: (i, 3)), gcp, gcn,
                  pl.BlockSpec((tm, CW), lambda i: (i, 4)), xip, xin, pl.BlockSpec((tm, AW), lambda i: (i, 0)),
                  pl.BlockSpec((3, CW), lambda i: (0, 0))],
        out_specs=pl.BlockSpec((tm, D), lambda i: (i, 0)), out_shape=jax.ShapeDtypeStruct((n, D), BF16),
        name=name, compiler_params=_params("parallel"))(p, p, p, p, p, p, p, o, conv_w)


def _conv_gate_bwd(dcat, p, conv_w, *, name, tm=256):
    n = p.shape[0]
    ni = n // tm

    def body(dc_ref, dcp_ref, dcn_ref, gb_ref, gbp_ref, gbn_ref, gc_ref, gcp_ref, gcn_ref, xi_ref, xip_ref, xin_ref,
             w_ref, dp_ref, dw_ref):
        i = pl.program_id(0)
        gcext = _ext(gcp_ref, gc_ref, gcn_ref, i, ni)
        xiext = _ext(xip_ref, xi_ref, xin_ref, i, ni)
        hext = gcext * xiext
        dcv = _ext(dcp_ref, dc_ref, dcn_ref, i, ni) * _ext(gbp_ref, gb_ref, gbn_ref, i, ni)
        dp_ref[:, 0:CW] = (dc_ref[...] * _conv3(hext, w_ref, tm)).astype(BF16)
        dh = _sh(dcv, 1, tm) * w_ref[0:1, :] + _sh(dcv, 0, tm) * w_ref[1:2, :] + _sh(dcv, -1, tm) * w_ref[2:3, :]
        dp_ref[:, CW:2 * CW] = (dh * xi_ref[...]).astype(BF16)
        dp_ref[:, 2 * CW:3 * CW] = (dh * gc_ref[...]).astype(BF16)
        dcv_t = dcv[HALO:HALO + tm]
        dw = jnp.concatenate([_colsum(dcv_t * _sh(hext, -1, tm)), _colsum(dcv_t * _sh(hext, 0, tm)),
                              _colsum(dcv_t * _sh(hext, 1, tm))], axis=0)
        _acc_out(dw_ref, i, dw)

    def trio(colblk):
        prev, nxt = _halo_specs(tm, CW, n, colblk=colblk)
        return [pl.BlockSpec((tm, CW), lambda i: (i, colblk)), prev, nxt]

    return pl.pallas_call(
        body, grid=(ni,), in_specs=trio(1) + trio(2) + trio(3) + trio(4) + [pl.BlockSpec((3, CW), lambda i: (0, 0))],
        out_specs=[pl.BlockSpec((tm, 3 * CW), lambda i: (i, 0)), pl.BlockSpec((3, CW), lambda i: (0, 0))],
        out_shape=[jax.ShapeDtypeStruct((n, 3 * CW), BF16), jax.ShapeDtypeStruct((3, CW), F32)],
        name=name, compiler_params=_params("arbitrary"))(dcat, dcat, dcat, p, p, p, p, p, p, p, p, p, conv_w)


def _attn_fwd(q, k, v, *, name, bq=512, sub=256):
    n = q.shape[1]
    t = k.shape[1]
    bq = min(bq, n)
    sub = min(sub, 2 * bq)

    def body(q_ref, k_ref, v_ref, o_ref, lse_ref):
        q2 = q_ref[...].reshape(2 * bq, HD)
        outs, lses = [], []
        for r0 in range(0, 2 * bq, sub):
            s = lax.dot_general(q2[r0:r0 + sub], k_ref[0], _NT, preferred_element_type=F32)
            m = jnp.max(s, axis=-1, keepdims=True)
            pv = jnp.exp2(s - m)
            l = jnp.sum(pv, axis=-1, keepdims=True)
            outs.append(jnp.dot(pv.astype(BF16), v_ref[0], preferred_element_type=F32) / l)
            lses.append(m + jnp.log2(l))
        out = jnp.concatenate(outs, axis=0)
        o_ref[:, 0:HD] = out[0:bq]
        o_ref[:, HD:2 * HD] = out[bq:2 * bq]
        lse_ref[...] = jnp.concatenate(lses, axis=0).reshape(2, bq, 1)

    kspec = pl.BlockSpec((1, t, HD), lambda h, i: (h, 0, 0))
    return pl.pallas_call(
        body, grid=(NKV, n // bq),
        in_specs=[pl.BlockSpec((2, bq, HD), lambda h, i: (h, i, 0)), kspec, kspec],
        out_specs=[pl.BlockSpec((bq, 2 * HD), lambda h, i: (i, h)), pl.BlockSpec((2, bq, 1), lambda h, i: (h, i, 0))],
        out_shape=[jax.ShapeDtypeStruct((n, AW), F32), jax.ShapeDtypeStruct((NQ, n, 1), F32)],
        name=name, compiler_params=_params("parallel", "parallel"))(q, k, v)


def _attn_bwd(q, k, v, dcat, o, lse, *, name, bq=256):
    n = q.shape[1]
    t = k.shape[1]
    bq = min(bq, n)

    def body(q_ref, k_ref, v_ref, dc_ref, o_ref, lse_ref, dq_ref, dk_ref, dv_ref):
        @pl.when(pl.program_id(1) == 0)
        def _():
            dk_ref[...] = jnp.zeros_like(dk_ref)
            dv_ref[...] = jnp.zeros_like(dv_ref)

        q2 = q_ref[...].reshape(2 * bq, HD)
        do_f = jnp.concatenate([dc_ref[:, 0:HD], dc_ref[:, HD:2 * HD]], axis=0)
        o_f = jnp.concatenate([o_ref[:, 0:HD], o_ref[:, HD:2 * HD]], axis=0)
        delta = jnp.sum(do_f * o_f, axis=-1, keepdims=True)
        do2 = do_f.astype(BF16)
        s = lax.dot_general(q2, k_ref[0], _NT, preferred_element_type=F32)
        pv = jnp.exp2(s - lse_ref[...].reshape(2 * bq, 1))
        dp = lax.dot_general(do2, v_ref[0], _NT, preferred_element_type=F32)
        ds = (pv * (dp - delta)).astype(BF16)
        dq_ref[...] = (jnp.dot(ds, k_ref[0], preferred_element_type=F32) * _SCALE).reshape(2, bq, HD)
        dk_ref[0] += lax.dot_general(ds, q2, _TN, preferred_element_type=F32) * _LN2
        dv_ref[0] += lax.dot_general(pv.astype(BF16), do2, _TN, preferred_element_type=F32)

    qspec = pl.BlockSpec((2, bq, HD), lambda h, i: (h, i, 0))
    kspec = pl.BlockSpec((1, t, HD), lambda h, i: (h, 0, 0))
    sspec = pl.BlockSpec((2, bq, 1), lambda h, i: (h, i, 0))
    cspec = pl.BlockSpec((bq, 2 * HD), lambda h, i: (i, h))
    return pl.pallas_call(
        body, grid=(NKV, n // bq), in_specs=[qspec, kspec, kspec, cspec, cspec, sspec], out_specs=[qspec, kspec, kspec],
        out_shape=[jax.ShapeDtypeStruct((NQ, n, HD), F32), jax.ShapeDtypeStruct((NKV, t, HD), F32),
                   jax.ShapeDtypeStruct((NKV, t, HD), F32)],
        name=name, compiler_params=_params("parallel", "arbitrary"))(q, k, v, dcat, o, lse)


def _window_sums(ext, w):
    s, step = ext, 1
    while step < w:
        s = s + _roll_rows(s, step)
        step *= 2
    return s


def _pool_counts(i, tm, n, w, rows, first):
    t = i * tm - HALO + first + lax.broadcasted_iota(jnp.int32, (rows, 1), 0)
    lo = jnp.clip(t - w // 2, 0, n)
    hi = jnp.clip(t + w - w // 2, 0, n)
    return jnp.maximum(hi - lo, 1).astype(F32)


def _norm_mod_ext(xext, gain_ref, sc_ref, sh_ref, i, tm, n):
    rows = xext.shape[0]
    t = i * tm - HALO + lax.broadcasted_iota(jnp.int32, (rows, 1), 0)
    inside = (t >= 0) & (t < n)
    r = lax.rsqrt(jnp.mean(xext * xext, axis=-1, keepdims=True) + EPS)
    xh = xext * r
    a = (xh * gain_ref[...]) * (1.0 + sc_ref[...]) + sh_ref[...]
    return jnp.where(inside, a, 0.0), r, xh


def _pool_fwd(x, y, g, gain, sc, sh, pool_w, *, name, tm=256):
    n, d = x.shape
    ni = n // tm

    def body(x_ref, xp_ref, xn_ref, y_ref, yp_ref, yn_ref, g_ref, gain_ref, sc_ref, sh_ref, w_ref, xo_ref, o_ref):
        i = pl.program_id(0)
        xext = _ext(xp_ref, x_ref, xn_ref, i, ni) + g_ref[...] * _ext(yp_ref, y_ref, yn_ref, i, ni)
        xo_ref[...] = xext[HALO:HALO + tm]
        aext, _, _ = _norm_mod_ext(xext, gain_ref, sc_ref, sh_ref, i, tm, n)
        for gi, w in enumerate(POOL_WINDOWS):
            ag = aext[:, gi * PG:(gi + 1) * PG]
            mean = _sh(_window_sums(ag, w), -(w // 2), tm) / _pool_counts(i, tm, n, w, tm, HALO)
            pooled = mean - ag[HALO:HALO + tm]
            o_ref[:, gi * PG:(gi + 1) * PG] = jnp.dot(pooled.astype(BF16), w_ref[gi], preferred_element_type=F32)

    row = pl.BlockSpec((tm, d), lambda i: (i, 0))
    prev, nxt = _halo_specs(tm, d, n)
    return pl.pallas_call(
        body, grid=(ni,),
        in_specs=[row, prev, nxt, row, prev, nxt, _vec(d), _vec(d), _vec(d), _vec(d),
                  pl.BlockSpec((4, PG, PG), lambda i: (0, 0, 0))],
        out_specs=[row, row], out_shape=[jax.ShapeDtypeStruct((n, d), F32)] * 2,
        name=name, compiler_params=_params("parallel"))(x, x, x, y, y, y, g, gain, sc, sh, pool_w)


def _pool_bwd(dxo, mixed, x, g, scale, gain, sc, sh, pool_w, zprev, gprev, *, name, tm=256):
    n, d = x.shape
    ni = n // tm

    def body(dx_ref, dxp_ref, dxn_ref, mx_ref, x_ref, xp_ref, xn_ref, g_ref, s_ref, gain_ref, sc_ref, sh_ref, w_ref,
             zp_ref, gp_ref, dxi_ref, dw_ref, dg_ref, dsl_ref, dsh_ref, dsc_ref, dgn_ref, dzp_ref, dgp_ref):
        i = pl.program_id(0)

        @pl.when(i == 0)
        def _():
            dw_ref[...] = jnp.zeros_like(dw_ref)

        dxo_t = dx_ref[...]
        mixed_t = mx_ref[...]
        dy_t = dxo_t * g_ref[...]
        _acc_out(dg_ref, i, _colsum(dxo_t * (mixed_t * s_ref[...])))
        _acc_out(dsl_ref, i, _colsum(dy_t * mixed_t))
        dmixed = (_ext(dxp_ref, dx_ref, dxn_ref, i, ni) * g_ref[...]) * s_ref[...]
        xext = _ext(xp_ref, x_ref, xn_ref, i, ni)
        aext, rext, xhext = _norm_mod_ext(xext, gain_ref, sc_ref, sh_ref, i, tm, n)
        rows = tm + 2 * HALO
        da_parts = []
        for gi, w in enumerate(POOL_WINDOWS):
            sl = slice(gi * PG, (gi + 1) * PG)
            ag = aext[:, sl]
            mean = _sh(_window_sums(ag, w), -(w // 2), tm) / _pool_counts(i, tm, n, w, tm, HALO)
            pooled = (mean - ag[HALO:HALO + tm]).astype(BF16)
            dmg = dmixed[:, sl].astype(BF16)
            dw_ref[gi] += lax.dot_general(pooled, dmixed[HALO:HALO + tm, sl].astype(BF16), _TN,
                                          preferred_element_type=F32)
            dpl = lax.dot_general(dmg, w_ref[gi], _NT, preferred_element_type=F32)
            e = dpl / _pool_counts(i, tm, n, w, rows, 0)
            da_parts.append(_sh(_window_sums(e, w), 1 - w // 2, tm) - dpl[HALO:HALO + tm])
        da = jnp.concatenate(da_parts, axis=1)
        r = rext[HALO:HALO + tm]
        xh = xhext[HALO:HALO + tm]
        nrm = xh * gain_ref[...]
        dn = da * (1.0 + sc_ref[...])
        dxh = dn * gain_ref[...]
        dxi = dxo_t + r * (dxh - xh * jnp.mean(dxh * xh, axis=-1, keepdims=True))
        dxi_ref[...] = dxi
        _acc_out(dsh_ref, i, _colsum(da))
        _acc_out(dsc_ref, i, _colsum(da * nrm))
        _acc_out(dgn_ref, i, _colsum(dn * xh))
        dzp_ref[...] = (dxi * gp_ref[...]).astype(BF16)
        _acc_out(dgp_ref, i, _colsum(dxi * zp_ref[...]))

    row = pl.BlockSpec((tm, d), lambda i: (i, 0))
    prev, nxt = _halo_specs(tm, d, n)
    wspec = pl.BlockSpec((4, PG, PG), lambda i: (0, 0, 0))
    vshape = jax.ShapeDtypeStruct((1, d), F32)
    return pl.pallas_call(
        body, grid=(ni,),
        in_specs=[row, prev, nxt, row, row, prev, nxt] + [_vec(d)] * 5 + [wspec, row, _vec(d)],
        out_specs=[row, wspec] + [_vec(d)] * 5 + [row, _vec(d)],
        out_shape=[jax.ShapeDtypeStruct((n, d), F32), jax.ShapeDtypeStruct((4, PG, PG), F32)] + [vshape] * 5
        + [jax.ShapeDtypeStruct((n, d), BF16), vshape],
        name=name, compiler_params=_params("arbitrary"))(dxo, dxo, dxo, mixed, x, x, x, g, scale, gain, sc, sh, pool_w,
                                                         zprev, gprev)


def _adamw(gparts_list, w, m, v, *, name, silu_grad_of=None):
    nl = len(gparts_list)
    nparts, r, c = gparts_list[0].shape
    tr = _pick(r, (256, 128, 64, 32, 16, 8))
    has_c = silu_grad_of is not None

    def body(*refs):
        gp_refs = refs[:nl]
        it = iter(refs[nl:])
        w_ref, m_ref, v_ref = next(it), next(it), next(it)
        c_ref = next(it) if has_c else None
        g_ref, d_ref, mo_ref, vo_ref = next(it), next(it), next(it), next(it)
        layer = pl.program_id(0)

        def update(gp_ref):
            g = gp_ref[0].astype(F32)
            for p in range(1, nparts):
                g = g + gp_ref[p].astype(F32)
            if has_c:
                cv = c_ref[0]
                sg = _sigmoid(cv)
                g = g * (sg * (1.0 + cv * (1.0 - sg)))
            g_ref[0] = g
            mn = ADAM_B1 * m_ref[0] + (1.0 - ADAM_B1) * g
            vn = ADAM_B2 * v_ref[0] + (1.0 - ADAM_B2) * (g * g)
            m_hat = mn / (1.0 - ADAM_B1 ** ADAM_STEP)
            v_hat = vn / (1.0 - ADAM_B2 ** ADAM_STEP)
            d_ref[0] = -ADAM_LR * (m_hat / (jnp.sqrt(v_hat) + ADAM_EPS) + ADAM_WD * w_ref[0])
            mo_ref[0] = mn
            vo_ref[0] = vn

        if nl == 1:
            update(gp_refs[0])
        else:
            for li in range(nl):
                pl.when(layer == li)(functools.partial(update, gp_refs[li]))

    row = pl.BlockSpec((1, tr, c), lambda l, i: (l, i, 0))
    in_specs = [pl.BlockSpec((nparts, tr, c), lambda l, i, li=li: (0, jnp.where(l == li, i, 0), 0)) for li in range(nl)]
    in_specs += [row, row, row]
    args = list(gparts_list) + [w, m, v]
    if has_c:
        in_specs.append(row)
        args.append(silu_grad_of)
    return pl.pallas_call(
        body, grid=(nl, r // tr), in_specs=in_specs, out_specs=[row] * 4,
        out_shape=[jax.ShapeDtypeStruct((nl, r, c), F32)] * 4, name=name,
        compiler_params=_params("arbitrary", "arbitrary"))(*args)


def _adamw_nd(gparts, w, m, v, *, name, silu_grad_of=None):
    shape = w.shape
    c = shape[-1]
    if isinstance(gparts, (list, tuple)):
        nl = len(gparts)
        r = math.prod(shape[1:-1])
    else:
        nl = 1
        r = math.prod(shape[:-1]) if len(shape) > 1 else 1
        gparts = [gparts]
    rs = lambda a: a.reshape(nl, r, c)
    res = _adamw([gp.reshape(gp.shape[0], r, c) for gp in gparts], rs(w), rs(m), rs(v), name=name,
                 silu_grad_of=None if silu_grad_of is None else rs(silu_grad_of))
    return [a.reshape(shape) for a in res]


def _place():
    return lax.axis_index("x"), lax.axis_index("y"), lax.axis_index("c")


def _all_gather(arrs, *, name):
    k_arr = len(arrs)

    def body(*refs):
        ins = refs[:k_arr]
        outs = refs[k_arr:2 * k_arr]
        send_sems, recv_sems, local_sems = refs[2 * k_arr:]
        x, y, c = _place()
        me, sibling = (x, y, c), (x, y, 1 - c)
        chips = [(1 - x, y), (x, 1 - y), (1 - x, 1 - y)]

        def slot(a, px, py, pc):
            return outs[a].at[4 * px + 2 * py + pc]

        def copy(a, s, block, to, src=None):
            return pltpu.make_async_remote_copy(
                src_ref=slot(a, *block) if src is None else src, dst_ref=slot(a, *block),
                send_sem=send_sems.at[a, s], recv_sem=recv_sems.at[a, s], device_id=to, device_id_type=MESH)

        mine = [pltpu.make_async_copy(ins[a], slot(a, *me), local_sems.at[a]) for a in range(k_arr)]
        for cp in mine:
            cp.start()
        first = []
        for a in range(k_arr):
            first.append(copy(a, 0, me, sibling, src=ins[a]))
            first += [copy(a, 1 + j, me, (*chip, c), src=ins[a]) for j, chip in enumerate(chips)]
        for cp in first:
            cp.start()
        passed = []
        for j, chip in enumerate(chips):
            for a in range(k_arr):
                copy(a, 1 + j, (*chip, c), me).wait_recv()
                fw = copy(a, 4 + j, (*chip, c), sibling)
                fw.start()
                passed.append(fw)
        for a in range(k_arr):
            copy(a, 0, sibling, me).wait_recv()
            for j, chip in enumerate(chips):
                copy(a, 4 + j, (*chip, 1 - c), me).wait_recv()
        for cp in first + passed:
            cp.wait_send()
        for cp in mine:
            cp.wait()

    any_spec = pl.BlockSpec(memory_space=pl.ANY)
    return pl.pallas_call(
        body, in_specs=[any_spec] * k_arr, out_specs=[any_spec] * k_arr,
        out_shape=[jax.ShapeDtypeStruct((NDEV,) + a.shape, a.dtype) for a in arrs],
        scratch_shapes=[pltpu.SemaphoreType.DMA((k_arr, 7)), pltpu.SemaphoreType.DMA((k_arr, 7)),
                        pltpu.SemaphoreType.DMA((k_arr,))],
        name=name)(*arrs)


_HBM = pl.BlockSpec(memory_space=pltpu.HBM)
_SEM = pl.BlockSpec(memory_space=pltpu.SEMAPHORE)
_EFFECT = pltpu.SideEffectType.DATAFLOW_SIDE_EFFECTING


def _peers(x, y, c):
    return [(x ^ (rel >> 2), y ^ ((rel >> 1) & 1), c ^ (rel & 1)) for rel in range(1, NDEV)]


def _exchange_copies(srcs, lands, send_sems, recv_sems, scatter):
    x, y, c = _place()
    me = 4 * x + 2 * y + c
    copies = []
    for r, (px, py, pc) in enumerate(_peers(x, y, c)):
        peer = 4 * px + 2 * py + pc
        for a in range(len(srcs)):
            copies.append(pltpu.make_async_remote_copy(
                src_ref=srcs[a].at[peer] if scatter else srcs[a], dst_ref=lands[a].at[me],
                send_sem=send_sems.at[7 * a + r], recv_sem=recv_sems.at[7 * a + r], device_id=(px, py, pc),
                device_id_type=MESH))
    return copies


def _exchange_start(arrs, *, scatter, name):
    k_arr = len(arrs)
    land_shapes = [a.shape if scatter else (NDEV,) + a.shape for a in arrs]
    lands = [pltpu.with_memory_space_constraint(lax.empty(s, a.dtype), pltpu.HBM) for s, a in zip(land_shapes, arrs)]
    srcs = [pltpu.with_memory_space_constraint(a, pltpu.HBM) for a in arrs]

    def body(*refs):
        src_refs, land_refs = refs[:k_arr], refs[k_arr:2 * k_arr]
        send_sems, recv_sems = refs[2 * k_arr], refs[2 * k_arr + 1]
        token = refs[-1]
        for cp in _exchange_copies(src_refs, land_refs, send_sems, recv_sems, scatter):
            cp.start()
        token[...] = jnp.zeros_like(token)

    out_shape = ([pltpu.SemaphoreType.DMA((7 * k_arr,)), pltpu.SemaphoreType.DMA((7 * k_arr,))]
                 + [pltpu.HBM(a.shape, a.dtype) for a in arrs] + [pltpu.HBM(s, a.dtype) for s, a in zip(land_shapes, arrs)]
                 + [jax.ShapeDtypeStruct((8, 128), F32)])
    res = pl.pallas_call(
        body, name=name, out_shape=out_shape, in_specs=[_HBM] * (2 * k_arr),
        out_specs=[_SEM, _SEM] + [_HBM] * (2 * k_arr) + [pl.BlockSpec(memory_space=pltpu.VMEM)],
        input_output_aliases={i: 2 + i for i in range(2 * k_arr)},
        compiler_params=pltpu.CompilerParams(has_side_effects=_EFFECT))(*srcs, *lands)
    return dict(send=res[0], recv=res[1], srcs=list(res[2:2 + k_arr]), lands=list(res[2 + k_arr:2 + 2 * k_arr]),
                token=res[-1], scatter=scatter)


def _exchange_wait(handle, after, *, name):
    k_arr = len(handle["srcs"])
    scatter = handle["scatter"]

    def body(*refs):
        src_refs, land_refs = refs[:k_arr], refs[k_arr:2 * k_arr]
        send_sems, recv_sems = refs[2 * k_arr], refs[2 * k_arr + 1]
        x, y, c = _place()
        me = 4 * x + 2 * y + c
        for r, (px, py, pc) in enumerate(_peers(x, y, c)):
            peer = 4 * px + 2 * py + pc
            for a in range(k_arr):
                cp = pltpu.make_async_remote_copy(
                    src_ref=src_refs[a].at[peer] if scatter else src_refs[a], dst_ref=land_refs[a].at[peer],
                    send_sem=send_sems.at[7 * a + r], recv_sem=recv_sems.at[7 * a + r], device_id=(x, y, c),
                    device_id_type=MESH)
                cp.wait_send()
                cp.wait_recv()

    arrs = handle["srcs"] + handle["lands"]
    res = pl.pallas_call(
        body, name=name, out_shape=[pltpu.HBM(a.shape, a.dtype) for a in arrs],
        in_specs=[_HBM] * (2 * k_arr) + [_SEM, _SEM, pl.BlockSpec(memory_space=pl.ANY)],
        out_specs=[_HBM] * (2 * k_arr), input_output_aliases={i: i for i in range(2 * k_arr)},
        compiler_params=pltpu.CompilerParams(has_side_effects=_EFFECT))(*arrs, handle["send"], handle["recv"], after)
    me = 4 * lax.axis_index("x") + 2 * lax.axis_index("y") + lax.axis_index("c")
    out = []
    for src, land in zip(res[:k_arr], res[k_arr:]):
        own = lax.dynamic_index_in_dim(src, me, 0, keepdims=False) if scatter else src
        out.append(lax.dynamic_update_index_in_dim(land, own, me, 0))
    return out


def _ffn_bwd(dxo, dz, xr, f, u_gc, hmid, gain, sc, w_up, cw, w_down, tag, gate_y=None, gate_g=None):
    d_wdown = _mm_tn((hmid, dz), name=f"ffn_down_dw_{tag}")
    dug, duv, dcw, dcb = _ffn_down_glu_bwd(dz, w_down, u_gc[0], u_gc[1], cw, name=f"ffn_down_glu_bwd_{tag}")
    d_wup_g = _mm_tn((f, dug), name=f"ffn_up_dwg_{tag}")
    d_wup_v = _mm_tn((f, duv), name=f"ffn_up_dwv_{tag}")
    df = _mm_w([dug, duv], w_up, tb=True, name=f"ffn_up_dx_{tag}")
    res = _norm_mod_bwd(df, xr, gain, sc, dres=dxo, gate_y=gate_y, gate_g=gate_g, name=f"ffn_norm_bwd_{tag}")
    outs = [res[0]] + ([res[4]] if gate_y is not None else [])
    sums = list(res[1:4]) + ([res[5]] if gate_y is not None else [])
    return outs, sums, (d_wup_g, d_wup_v, d_wdown, dcw, dcb)


def _split6(mod):
    return [mod[j * D:(j + 1) * D][None, :] for j in range(6)]


def _row(v):
    return v.reshape(1, -1)


def kernel(x, c, ctx, c_ctx, ada_w, ada_b, mix_norm, ffn_norm, even_w_in, even_q_gain, even_k_gain, even_conv_w, even_w_out, odd_pool_w, odd_pool_scale, ffn_w_up, ffn_conv_w, ffn_conv_b, ffn_w_down, loss_target, m_c_ctx, m_ada_w, m_ada_b, m_mix_norm, m_ffn_norm, m_even_w_in, m_even_q_gain, m_even_k_gain, m_even_conv_w, m_even_w_out, m_odd_pool_w, m_odd_pool_scale, m_ffn_w_up, m_ffn_conv_w, m_ffn_conv_b, m_ffn_w_down, v_c_ctx, v_ada_w, v_ada_b, v_mix_norm, v_ffn_norm, v_even_w_in, v_even_q_gain, v_even_k_gain, v_even_conv_w, v_even_w_out, v_odd_pool_w, v_odd_pool_scale, v_ffn_w_up, v_ffn_conv_w, v_ffn_conv_b, v_ffn_w_down):
    n = x.shape[1]
    lc = ctx.shape[1]
    me = 4 * lax.axis_index("x") + 2 * lax.axis_index("y") + lax.axis_index("c")
    xs, ctxs, tgt = x[0], ctx[0], loss_target[0]
    acols = ada_w.shape[2]

    small = jnp.concatenate([even_conv_w.reshape(-1), ffn_conv_w.reshape(-1), odd_pool_scale.reshape(-1)])
    nsmall = small.shape[0]
    small = jnp.pad(small, (0, (-nsmall) % 1024)).reshape(-1, 128)
    c_rows = jnp.pad(c, ((0, 7), (0, 0)))
    g_c, g_win, g_small = _all_gather([c_rows, even_w_in[0].astype(BF16), small], name="gather_first")
    w_in = g_win.transpose(1, 0, 2).reshape(D, -1)
    g_small = g_small.reshape(NDEV, -1)
    ecw = even_conv_w.shape[2]
    fcw = ffn_conv_w.shape[2]
    conv_w = g_small[:, :3 * ecw].reshape(NDEV, 3, ecw).transpose(1, 0, 2).reshape(3, CW)
    o1 = 3 * ecw
    fconv_w = g_small[:, o1:o1 + 6 * fcw].reshape(NDEV, 2, 3, fcw).transpose(1, 2, 0, 3).reshape(2, 3, DFF)
    o2 = o1 + 6 * fcw
    pool_scale = g_small[:, o2:o2 + D // NDEV].reshape(1, D)

    mraw = jnp.concatenate([g_c[:, 0, :], c_ctx[None, :], jnp.zeros((7, D), F32)], axis=0)
    my_bias = lax.dynamic_slice_in_dim(ada_b, me * acols, acols, axis=1)
    modp = jnp.stack([_mm(mraw, ada_w[l], silu_a=True, bias=my_bias[l:l + 1], name=f"ada_proj_{l}", tm=16, tn=256)
                      for l in range(2)])
    (g_mod,) = _all_gather([modp], name="gather_mod")
    mod_rows = g_mod.transpose(1, 2, 0, 3).reshape(2, 16, 6 * D)
    late_shards = [even_w_out[0].astype(BF16), odd_pool_w[0].astype(BF16), ffn_w_up.astype(BF16),
                   ffn_w_down.astype(BF16)]
    late_shards, mod_rows = lax.optimization_barrier((late_shards, mod_rows))
    h_weights = _exchange_start(late_shards, scatter=False, name="weights_start")
    mod_rows = mod_rows + h_weights["token"][0, 0]
    mod = lax.dynamic_index_in_dim(mod_rows, me, axis=1, keepdims=False)
    sh1, sc1, g1, sh2, sc2, g2 = _split6(mod[0])
    sh1b, sc1b, g1b, sh2b, sc2b, g2b = _split6(mod[1])
    csh1, csc1 = _split6(mod_rows[0, 8])[:2]
    mixn = [_row(mix_norm[l]) for l in range(2)]
    ffnn = [_row(ffn_norm[l]) for l in range(2)]
    qg, kg = _row(even_q_gain[0]), _row(even_k_gain[0])
    fcb = [_row(ffn_conv_b[l]) for l in range(2)]

    cs_t, sn_t = _rope_tables(n)
    a_lat = _norm_mod(xs, mixn[0], sc1, sh1, name="mix0_norm")
    a_ctx = _norm_mod(ctxs, mixn[0], csc1, csh1, name="mix0_norm_ctx")
    p_lat = _mm_w(a_lat, w_in, name="in_proj")
    p_ctx = _mm(a_ctx, w_in[:, AW:AW + 4 * HD], name="in_proj_ctx", tm=256, tn=512, tk=1024)
    kv_ctx = _qkv_prep(p_ctx, qg, kg, None, None, has_q=False, kv_col=0, kv_rows=lc + n, name="qkv_prep_ctx")
    q_r, k_all, v_all = _qkv_prep(p_lat, qg, kg, cs_t, sn_t, has_q=True, kv_col=1, kv_rows=lc + n, kv_row_off=lc,
                                  kv_into=kv_ctx, name="qkv_prep")
    o_attn, lse = _attn_fwd(q_r, k_all, v_all, name="attn_fwd")
    cat = _conv_gate_fwd(p_lat, o_attn, conv_w, name="conv_gate")
    g_wout, g_pool, g_up, g_down = _exchange_wait(h_weights, cat, name="weights_wait")
    w_out = g_wout.reshape(D, D)
    pool_w = g_pool.transpose(1, 0, 2, 3).reshape(4, PG, PG)
    w_up = [g_up[:, l].transpose(1, 0, 2).reshape(D, 2 * DFF) for l in range(2)]
    w_down = [g_down[:, l].reshape(DFF, D) for l in range(2)]
    y0, x1, f0 = _mm_w_ep(cat, w_out, _ep_resid_norm, [xs], [g1, ffnn[0], sc2, sh2], [F32, F32, BF16], [],
                          name="out_proj_norm")[:3]
    *u0, h0 = _ffn_up_glu(f0, w_up[0], fconv_w[0], fcb[0], name="ffn_up_glu_l0")
    z0 = _mm_w(h0, w_down[0], name="ffn_down_l0")

    x2, mixed = _pool_fwd(x1, z0, g2, mixn[1], sc1b, sh1b, pool_w, name="pool_fwd")
    x3, f1 = _norm_mod(x2, ffnn[1], sc2b, sh2b, y=mixed, g=g1b, ymul=pool_scale, name="ffn_norm_l1")
    *u1, h1 = _ffn_up_glu(f1, w_up[1], fconv_w[1], fcb[1], name="ffn_up_glu_l1")
    dx4, dz1, loss_part, dg2b = _mm_w_ep(h1, w_down[1], _ep_loss(D), [x3, tgt], [g2b], [F32, BF16], [128, D],
                                         name="ffn_down_loss")
    loss = lax.psum(loss_part[0, 0], ("x", "y", "c"))

    (dx3,), (dsh2b, dsc2b, dffn1), (dup1g, dup1v, ddown1, dfcw1, dfcb1) = _ffn_bwd(
        dx4, dz1, x3, f1, u1, h1, ffnn[1], sc2b, w_up[1], fconv_w[1], w_down[1], "l1")
    dx2, dpool_w, dg1b, dpscale, dsh1b, dsc1b, dmix1, dz0, dg2 = _pool_bwd(
        dx3, mixed, x2, g1b, pool_scale, mixn[1], sc1b, sh1b, pool_w, z0, g2, name="pool_bwd")

    def up_shards(dg, dv):
        return jnp.concatenate([dg, dv], axis=1).reshape(D, NDEV, -1).transpose(1, 0, 2)

    s_pool = dpool_w.astype(BF16).reshape(4, NDEV, PG // NDEV, PG).transpose(1, 0, 2, 3)
    h_g1 = _exchange_start([s_pool, up_shards(dup1g, dup1v), ddown1.reshape(NDEV, DFF // NDEV, D)], scatter=True,
                           name="grads1_start")

    (dx1, dy0), (dsh2, dsc2, dffn0, dg1), (dup0g, dup0v, ddown0, dfcw0, dfcb0) = _ffn_bwd(
        dx2, dz0, x1, f0, u0, h0, ffnn[0], sc2, w_up[0], fconv_w[0] + h_g1["token"][0, 0], w_down[0], "l0",
        gate_y=y0, gate_g=g1)
    h_g0 = _exchange_start([up_shards(dup0g, dup0v), ddown0.reshape(NDEV, DFF // NDEV, D)], scatter=True,
                           name="grads0_start")
    dcat = _mm_w(dy0, w_out, tb=True, name="out_proj_dx", tm=512)
    d_wout = _mm_tn((cat, dy0), name="out_proj_dw")
    dp_conv, dconv_w = _conv_gate_bwd(dcat, p_lat, conv_w + h_g0["token"][0, 0], name="conv_gate_bwd")
    dq_r, dk_all, dv_all = _attn_bwd(q_r, k_all, v_all, dcat, o_attn, lse, name="attn_bwd")
    dp_qkv, dqg_l, dkg_l = _qkv_bwd(p_lat, dq_r, dk_all, dv_all, qg, kg, cs_t, sn_t, has_q=True, kv_col=1,
                                    kv_row_off=lc, name="qkv_bwd")
    dp_ctx, _zero_qg, dkg_c = _qkv_bwd(p_ctx, None, dk_all, dv_all, qg, kg, None, None, has_q=False, kv_col=0,
                                       kv_row_off=0, name="qkv_bwd_ctx")
    da_ctx = _mm(dp_ctx, w_in[:, :D], tb=True, name="in_proj_dx_ctx", tm=256, tn=512, tk=1024)
    d_win_qkv = _mm_tn([(a_lat, dp_qkv), (a_ctx, dp_ctx)], name="in_proj_dw_qkv")
    d_win_conv = _mm_tn((a_lat, dp_conv), name="in_proj_dw_conv")
    d_win = jnp.concatenate([d_win_qkv, d_win_conv], axis=1)
    da_lat = _mm_w([dp_qkv, dp_conv], w_in, tb=True, name="in_proj_dx", tm=512)
    grad_x, dsh1, dsc1, dmix0 = _norm_mod_bwd(da_lat, xs, mixn[0], sc1, dres=dx1, name="mix0_norm_bwd")
    _dctx, dcsh1, dcsc1, dmix0c = _norm_mod_bwd(da_ctx, ctxs, mixn[0], csc1, name="mix0_norm_bwd_ctx")

    z1k = jnp.zeros((1, D), F32)
    pack = jnp.concatenate(
        [v.reshape(-1) for v in (dsh1, dsc1, dg1, dsh2, dsc2, dg2, dsh1b, dsc1b, dg1b, dsh2b, dsc2b, dg2b,
                                 dcsh1, dcsc1, z1k, z1k, z1k, z1k,
                                 dmix0, dmix1, dmix0c, z1k, dffn0, dffn1, dqg_l, dkg_l + dkg_c,
                                 dfcb0, dfcb1, dconv_w, dfcw0, dfcw1, dpscale)])
    npack = pack.shape[0]
    pack = jnp.pad(pack, (0, (-npack) % 1024)).reshape(-1, 128)
    (g_pack,) = _all_gather([pack], name="gather_small_grads")
    gp = g_pack.reshape(NDEV, -1)
    off = [0]

    def take(size):
        seg = gp[:, off[0]:off[0] + size]
        off[0] += size
        return seg

    dmod_all = take(12 * D).reshape(NDEV, 2, 6 * D)
    dmodc_all = take(6 * D).reshape(NDEV, 1, 6 * D)
    dmix_all = take(4 * D).reshape(NDEV, 2, 2, D)
    dffn_all = take(2 * D).reshape(NDEV, 2, D)
    dqg_all = take(HD).reshape(NDEV, 1, HD)
    dkg_all = take(HD).reshape(NDEV, 1, HD)
    dfcb_all = take(2 * DFF).reshape(NDEV, 2, DFF)
    dconvw_all = take(3 * CW).reshape(NDEV, 3, CW)
    dfcw_all = take(6 * DFF).reshape(NDEV, 2, 3, DFF)
    dpscale_all = take(D).reshape(NDEV, D)

    dmodc_sum = dmodc_all[0]
    for dev in range(1, NDEV):
        dmodc_sum = dmodc_sum + dmodc_all[dev]
    my_cols = lambda a: lax.dynamic_slice_in_dim(a, me * acols, acols, axis=a.ndim - 1)
    rows0 = jnp.concatenate([my_cols(dmod_all[:, 0]), my_cols(dmodc_sum), jnp.zeros((7, acols), F32)], axis=0)
    rows1 = jnp.concatenate([my_cols(dmod_all[:, 1]), jnp.zeros((8, acols), F32)], axis=0)
    d_ada = jnp.stack([_mm(mraw, rows, ta=True, silu_a=True, name=f"ada_dw_{l}", tm=512, tn=256, tk=16)
                       for l, rows in enumerate((rows0, rows1))])
    dscc_part = _mm(rows0, ada_w[0], tb=True, name="ada_dcctx", tm=16, tn=512, tk=256)
    (g_dscc,) = _all_gather([dscc_part[8:16]], name="gather_dcctx")

    attn_shards = [d_win.reshape(D, NDEV, -1).transpose(1, 0, 2), d_wout.reshape(NDEV, D // NDEV, D)]
    attn_shards, g_dscc = lax.optimization_barrier((attn_shards, g_dscc))
    h_ga = _exchange_start(attn_shards, scatter=True, name="grads_attn_start")
    dmod_all = dmod_all + h_ga["token"][0, 0]

    outs = {}

    def put(nm, res):
        outs["grad_" + nm], outs["delta_" + nm], outs["new_m_" + nm], outs["new_v_" + nm] = res

    dmodc_pad = jnp.concatenate([dmodc_all, jnp.zeros_like(dmodc_all)], axis=1)
    put("ada_b", _adamw_nd(jnp.concatenate([dmod_all, dmodc_pad], axis=0), ada_b, m_ada_b, v_ada_b, name="adam_ada_b"))
    put("mix_norm", _adamw_nd(jnp.concatenate([dmix_all[:, 0], dmix_all[:, 1]], axis=0), mix_norm, m_mix_norm,
                              v_mix_norm, name="adam_mix_norm"))
    put("ffn_norm", _adamw_nd(dffn_all, ffn_norm, m_ffn_norm, v_ffn_norm, name="adam_ffn_norm"))
    put("even_q_gain", _adamw_nd(dqg_all, even_q_gain, m_even_q_gain, v_even_q_gain, name="adam_q_gain"))
    put("even_k_gain", _adamw_nd(dkg_all, even_k_gain, m_even_k_gain, v_even_k_gain, name="adam_k_gain"))
    put("ffn_conv_b", _adamw_nd(dfcb_all, ffn_conv_b, m_ffn_conv_b, v_ffn_conv_b, name="adam_ffn_conv_b"))
    my_convw = lax.dynamic_slice_in_dim(dconvw_all, me * ecw, ecw, axis=2)[:, None]
    put("even_conv_w", _adamw_nd(my_convw, even_conv_w, m_even_conv_w, v_even_conv_w, name="adam_even_conv_w"))
    my_fcw = lax.dynamic_slice_in_dim(dfcw_all, me * fcw, fcw, axis=3)
    put("ffn_conv_w", _adamw_nd(my_fcw, ffn_conv_w, m_ffn_conv_w, v_ffn_conv_w, name="adam_ffn_conv_w"))
    my_ps = lax.dynamic_slice_in_dim(dpscale_all, me * (D // NDEV), D // NDEV, axis=1)[:, None]
    put("odd_pool_scale", _adamw_nd(my_ps, odd_pool_scale, m_odd_pool_scale, v_odd_pool_scale, name="adam_pool_scale"))

    put("ada_w", _adamw_nd(d_ada[None], ada_w, m_ada_w, v_ada_w, name="adam_ada_w"))
    put("c_ctx", _adamw_nd(g_dscc[:, 0:1, :].reshape(NDEV, D), c_ctx, m_c_ctx, v_c_ctx, name="adam_c_ctx",
                           silu_grad_of=c_ctx))

    r_pool, r_up1, r_down1 = _exchange_wait(h_g1, outs["grad_ada_b"], name="grads1_wait")
    r_up0, r_down0 = _exchange_wait(h_g0, outs["grad_mix_norm"], name="grads0_wait")
    r_win, r_wout = _exchange_wait(h_ga, outs["grad_c_ctx"], name="grads_attn_wait")
    put("even_w_in", _adamw_nd(r_win[:, None], even_w_in, m_even_w_in, v_even_w_in, name="adam_w_in"))
    put("even_w_out", _adamw_nd(r_wout[:, None], even_w_out, m_even_w_out, v_even_w_out, name="adam_w_out"))
    put("odd_pool_w", _adamw_nd(r_pool[:, None], odd_pool_w, m_odd_pool_w, v_odd_pool_w, name="adam_pool_w"))
    put("ffn_w_up", _adamw_nd([r_up0, r_up1], ffn_w_up, m_ffn_w_up, v_ffn_w_up, name="adam_w_up"))
    put("ffn_w_down", _adamw_nd([r_down0, r_down1], ffn_w_down, m_ffn_w_down, v_ffn_w_down, name="adam_w_down"))

    names = ["c_ctx", "ada_w", "ada_b", "mix_norm", "ffn_norm", "even_w_in", "even_q_gain", "even_k_gain",
             "even_conv_w", "even_w_out", "odd_pool_w", "odd_pool_scale", "ffn_w_up", "ffn_conv_w", "ffn_conv_b",
             "ffn_w_down"]
    result = [loss, grad_x[None]]
    for kind in ("grad_", "delta_", "new_m_", "new_v_"):
        result += [outs[kind + nm] for nm in names]
    return tuple(result)
```

```python
import functools
import math

import jax
import jax.numpy as jnp
from jax import lax
from jax.experimental import pallas as pl
from jax.experimental.pallas import tpu as pltpu

F32 = jnp.float32
BF16 = jnp.bfloat16

D = 1024
HD = 128
NQ = 4
NKV = 2
AW = NQ * HD
CW = D - AW
DFF = 2816
GRID_W = 64
ROPE_THETA = 10000.0
POOL_WINDOWS = (2, 4, 8, 16)
PG = D // 4
EPS = 1e-6
NDEV = 8
HALO = 8
MESH = pl.DeviceIdType.MESH

ADAM_LR = 0.001
ADAM_B1 = 0.9
ADAM_B2 = 0.999
ADAM_EPS = 1e-08
ADAM_WD = 0.01
ADAM_STEP = 10


def _pick(dim, prefs):
    for p in prefs:
        if dim % p == 0:
            return p
    return dim


def _params(*sem):
    return pltpu.CompilerParams(dimension_semantics=sem)


_NT = (((1,), (1,)), ((), ()))
_TN = (((0,), (0,)), ((), ()))
_SCALE = HD ** -0.5
_QSCALE = _SCALE * math.log2(math.e)
_LN2 = math.log(2.0)


def _mm(a_list, b, *, name, ta=False, tb=False, out_dtype=F32, silu_a=False, bias=None, tm=None, tn=None, tk=None):
    if not isinstance(a_list, (list, tuple)):
        a_list = [a_list]
    na = len(a_list)
    assert not (ta and na > 1)
    if ta:
        kdim, m = a_list[0].shape
        ks = [kdim]
    else:
        m = a_list[0].shape[0]
        ks = [a.shape[1] for a in a_list]
        kdim = sum(ks)
    n = b.shape[0] if tb else b.shape[1]
    assert (b.shape[1] if tb else b.shape[0]) == kdim
    kunit = math.gcd(*ks) if na > 1 else kdim
    tm = min(tm, m) if tm else _pick(m, (512, 256, 128, 64, 32, 16, 8))
    tn = min(tn, n) if tn else _pick(n, (512, 256, 128))
    tk = min(tk, kunit) if tk else _pick(kunit, (1024, 768, 512, 256, 128))
    assert m % tm == 0 and n % tn == 0 and all(k % tk == 0 for k in ks)
    nks = [k // tk for k in ks]
    starts = [sum(nks[:i]) for i in range(na)]
    nk = sum(nks)
    has_bias = bias is not None

    def body(*refs):
        a_refs = refs[:na]
        b_ref = refs[na]
        bias_ref = refs[na + 1] if has_bias else None
        o_ref = refs[na + 1 + has_bias]
        acc = refs[-1]
        k = pl.program_id(2)

        @pl.when(k == 0)
        def _():
            acc[...] = jnp.zeros_like(acc)

        bv = b_ref[...].astype(BF16)
        dn = (((0 if ta else 1,), (1 if tb else 0,)), ((), ()))
        for idx in range(na):
            def step(idx=idx):
                av = a_refs[idx][...]
                if silu_a:
                    av = av * jax.nn.sigmoid(av)
                acc[...] += lax.dot_general(av.astype(BF16), bv, dn, preferred_element_type=F32)
            if na == 1:
                step()
            else:
                pl.when((k >= starts[idx]) & (k < starts[idx] + nks[idx]))(step)

        @pl.when(k == nk - 1)
        def _():
            r = acc[...]
            if has_bias:
                r = r + bias_ref[...]
            o_ref[...] = r.astype(o_ref.dtype)

    in_specs = []
    for idx in range(na):
        if ta:
            in_specs.append(pl.BlockSpec((tk, tm), lambda i, j, k: (k, i)))
        else:
            lo, cnt = starts[idx], nks[idx]
            in_specs.append(pl.BlockSpec((tm, tk), lambda i, j, k, lo=lo, cnt=cnt: (i, jnp.clip(k - lo, 0, cnt - 1))))
    if tb:
        in_specs.append(pl.BlockSpec((tn, tk), lambda i, j, k: (j, k)))
    else:
        in_specs.append(pl.BlockSpec((tk, tn), lambda i, j, k: (k, j)))
    args = list(a_list) + [b]
    if has_bias:
        in_specs.append(pl.BlockSpec((1, tn), lambda i, j, k: (0, j)))
        args.append(bias)
    return pl.pallas_call(
        body, grid=(m // tm, n // tn, nk), in_specs=in_specs,
        out_specs=pl.BlockSpec((tm, tn), lambda i, j, k: (i, j)),
        out_shape=jax.ShapeDtypeStruct((m, n), out_dtype),
        scratch_shapes=[pltpu.VMEM((tm, tn), F32)], name=name,
        compiler_params=_params("parallel", "parallel", "arbitrary"))(*args)


def _mm_w(a_list, w, *, name, tb=False, tm=256, out_dtype=F32):
    if not isinstance(a_list, (list, tuple)):
        a_list = [a_list]
    na = len(a_list)
    m = a_list[0].shape[0]
    ks = [a.shape[1] for a in a_list]
    offs = [sum(ks[:i]) for i in range(na)]
    n = w.shape[0] if tb else w.shape[1]
    assert (w.shape[1] if tb else w.shape[0]) == sum(ks)
    tm = min(tm, m)
    assert m % tm == 0

    def body(*refs):
        a_refs, w_ref, o_ref = refs[:na], refs[na], refs[na + 1]
        acc = None
        for idx in range(na):
            av = a_refs[idx][...].astype(BF16)
            if tb:
                part = lax.dot_general(av, w_ref[:, offs[idx]:offs[idx] + ks[idx]], _NT, preferred_element_type=F32)
            else:
                part = jnp.dot(av, w_ref[offs[idx]:offs[idx] + ks[idx], :], preferred_element_type=F32)
            acc = part if acc is None else acc + part
        o_ref[...] = acc.astype(o_ref.dtype)

    in_specs = [pl.BlockSpec((tm, k), lambda i: (i, 0)) for k in ks] + [pl.BlockSpec(w.shape, lambda i: (0, 0))]
    return pl.pallas_call(
        body, grid=(m // tm,), in_specs=in_specs, out_specs=pl.BlockSpec((tm, n), lambda i: (i, 0)),
        out_shape=jax.ShapeDtypeStruct((m, n), out_dtype), name=name, compiler_params=_params("parallel"))(*a_list, w)


def _mm_w_ep(a_list, w, epilogue, row_in, vec_in, out_dtypes, sum_widths, *, name, tb=False, tm=256, sub=256):
    if not isinstance(a_list, (list, tuple)):
        a_list = [a_list]
    na, nr, nv, no, ns = len(a_list), len(row_in), len(vec_in), len(out_dtypes), len(sum_widths)
    m = a_list[0].shape[0]
    ks = [a.shape[1] for a in a_list]
    offs = [sum(ks[:i]) for i in range(na)]
    n = w.shape[0] if tb else w.shape[1]
    assert (w.shape[1] if tb else w.shape[0]) == sum(ks)
    tm = min(tm, m)
    sub = min(sub, tm)
    assert m % tm == 0 and tm % sub == 0

    def body(*refs):
        a_refs, w_ref = refs[:na], refs[na]
        row_refs = refs[na + 1:na + 1 + nr]
        vec_refs = refs[na + 1 + nr:na + 1 + nr + nv]
        out_refs = refs[na + 1 + nr + nv:na + 1 + nr + nv + no]
        sum_refs = refs[na + 1 + nr + nv + no:]

        @pl.when(pl.program_id(0) == 0)
        def _():
            for s_ref in sum_refs:
                s_ref[...] = jnp.zeros_like(s_ref)

        vecs = [v[...] for v in vec_refs]
        for r0 in range(0, tm, sub):
            acc = None
            for idx in range(na):
                av = a_refs[idx][r0:r0 + sub, :].astype(BF16)
                if tb:
                    part = lax.dot_general(av, w_ref[:, offs[idx]:offs[idx] + ks[idx]], _NT, preferred_element_type=F32)
                else:
                    part = jnp.dot(av, w_ref[offs[idx]:offs[idx] + ks[idx], :], preferred_element_type=F32)
                acc = part if acc is None else acc + part
            outs, sums = epilogue(acc, [r[r0:r0 + sub, :] for r in row_refs], vecs)
            for o_ref, o in zip(out_refs, outs):
                o_ref[r0:r0 + sub, :] = o.astype(o_ref.dtype)
            for s_ref, s in zip(sum_refs, sums):
                s_ref[...] += s

    row = pl.BlockSpec((tm, n), lambda i: (i, 0))
    in_specs = ([pl.BlockSpec((tm, k), lambda i: (i, 0)) for k in ks] + [pl.BlockSpec(w.shape, lambda i: (0, 0))]
                + [row] * nr + [_vec(n)] * nv)
    return pl.pallas_call(
        body, grid=(m // tm,), in_specs=in_specs, out_specs=[row] * no + [_vec(sw) for sw in sum_widths],
        out_shape=[jax.ShapeDtypeStruct((m, n), dt) for dt in out_dtypes]
        + [jax.ShapeDtypeStruct((1, sw), F32) for sw in sum_widths],
        name=name, compiler_params=_params("arbitrary" if ns else "parallel"))(*a_list, w, *row_in, *vec_in)


def _ep_norm_bwd(has_gate):
    def ep(dav, rows, vecs):
        xv = rows[0]
        gain, scv = vecs[0], vecs[1]
        r = lax.rsqrt(jnp.mean(xv * xv, axis=-1, keepdims=True) + EPS)
        xh = xv * r
        nrm = xh * gain
        dn = dav * (1.0 + scv)
        dxh = dn * gain
        dx = r * (dxh - xh * jnp.mean(dxh * xh, axis=-1, keepdims=True)) + rows[1]
        outs, sums = [dx], [_colsum(dav), _colsum(dav * nrm), _colsum(dn * xh)]
        if has_gate:
            outs.append(dx * vecs[2])
            sums.append(_colsum(dx * rows[2]))
        return outs, sums
    return ep


def _ep_loss(d):
    def ep(zv, rows, vecs):
        xv, tv = rows
        gv = vecs[0]
        diff = (xv + gv * zv) - tv
        dx = diff * (1.0 / d)
        part = 0.5 * jnp.sum(jnp.mean(diff * diff, axis=-1, keepdims=True), axis=0, keepdims=True)
        return [dx, dx * gv], [jnp.broadcast_to(part, (1, 128)), _colsum(dx * zv)]
    return ep


def _ep_resid_norm(yv, rows, vecs):
    g, gain, scv, shv = vecs
    xv = rows[0] + g * yv
    r = lax.rsqrt(jnp.mean(xv * xv, axis=-1, keepdims=True) + EPS)
    return [yv, xv, ((xv * r) * gain) * (1.0 + scv) + shv], []


def _mm_tn(pairs, *, name, tk=1024, out_dtype=BF16):
    if not isinstance(pairs, list):
        pairs = [pairs]
    m, n = pairs[0][0].shape[1], pairs[0][1].shape[1]
    tks = [min(tk, a.shape[0]) for a, _ in pairs]
    nks = [a.shape[0] // t for (a, _), t in zip(pairs, tks)]
    assert all(a.shape[0] == b.shape[0] and a.shape[0] % t == 0 for (a, b), t in zip(pairs, tks))
    starts = [sum(nks[:i]) for i in range(len(pairs))]
    nk = sum(nks)

    def body(*refs):
        o_ref, acc = refs[-2], refs[-1]
        k = pl.program_id(0)

        @pl.when(k == 0)
        def _():
            acc[...] = jnp.zeros_like(acc)

        for idx in range(len(pairs)):
            a_ref, b_ref = refs[2 * idx], refs[2 * idx + 1]

            def step(a_ref=a_ref, b_ref=b_ref):
                acc[...] += lax.dot_general(a_ref[...], b_ref[...], _TN, preferred_element_type=F32)

            if len(pairs) == 1:
                step()
            else:
                pl.when((k >= starts[idx]) & (k < starts[idx] + nks[idx]))(step)

        @pl.when(k == nk - 1)
        def _():
            o_ref[...] = acc[...].astype(o_ref.dtype)

    in_specs, args = [], []
    for (a, b), t, lo, cnt in zip(pairs, tks, starts, nks):
        idx_map = lambda k, lo=lo, cnt=cnt: (jnp.clip(k - lo, 0, cnt - 1), 0)
        in_specs += [pl.BlockSpec((t, m), idx_map), pl.BlockSpec((t, n), idx_map)]
        args += [a, b]
    return pl.pallas_call(
        body, grid=(nk,), in_specs=in_specs, out_specs=pl.BlockSpec((m, n), lambda k: (0, 0)),
        out_shape=jax.ShapeDtypeStruct((m, n), out_dtype), scratch_shapes=[pltpu.VMEM((m, n), F32)], name=name,
        compiler_params=_params("arbitrary"))(*args)


def _vec(d, col=None):
    if col is None:
        return pl.BlockSpec((1, d), lambda i, *_: (0, 0))
    return pl.BlockSpec((1, d), col)


def _halo_specs(tm, width, nrows, colblk=0, row_off=0):
    r = tm // HALO
    off = row_off // HALO
    last = nrows // HALO - 1
    prev = pl.BlockSpec((HALO, width), lambda i, *_: (off + jnp.maximum(i * r - 1, 0), colblk))
    nxt = pl.BlockSpec((HALO, width), lambda i, *_: (off + jnp.minimum((i + 1) * r, last), colblk))
    return prev, nxt


def _ext(prev_ref, main_ref, next_ref, i, ni):
    p = jnp.where(i > 0, prev_ref[...], 0.0)
    n = jnp.where(i < ni - 1, next_ref[...], 0.0)
    return jnp.concatenate([p, main_ref[...], n], axis=0)


def _sh(ext, k, tm):
    if k == 0:
        return ext[HALO:HALO + tm]
    rows = ext.shape[0]
    return pltpu.roll(ext, (-k) % rows, axis=0)[HALO:HALO + tm]


def _roll_rows(v, k):
    rows = v.shape[0]
    return pltpu.roll(v, (-k) % rows, axis=0) if k % rows else v


def _conv3(ext, w_ref, tm):
    return _sh(ext, -1, tm) * w_ref[0:1, :] + _sh(ext, 0, tm) * w_ref[1:2, :] + _sh(ext, 1, tm) * w_ref[2:3, :]


def _colsum(v):
    return jnp.sum(v, axis=0, keepdims=True)


def _acc_out(ref, i, val):
    @pl.when(i == 0)
    def _():
        ref[...] = jnp.zeros_like(ref)

    ref[...] += val


def _sigmoid(v):
    return jax.nn.sigmoid(v)


def _norm_mod(x, gain, sc, sh, *, name, y=None, g=None, ymul=None, tm=512):
    n, d = x.shape
    tm = min(tm, n)
    has_res = y is not None
    has_mul = ymul is not None

    def body(*refs):
        it = iter(refs)
        x_ref = next(it)
        y_ref = next(it) if has_res else None
        g_ref = next(it) if has_res else None
        m_ref = next(it) if has_mul else None
        gain_ref, sc_ref, sh_ref = next(it), next(it), next(it)
        xo_ref = next(it) if has_res else None
        a_ref = next(it)
        xv = x_ref[...]
        if has_res:
            yv = y_ref[...]
            if has_mul:
                yv = yv * m_ref[...]
            xv = xv + g_ref[...] * yv
            xo_ref[...] = xv
        r = lax.rsqrt(jnp.mean(xv * xv, axis=-1, keepdims=True) + EPS)
        nrm = (xv * r) * gain_ref[...]
        a_ref[...] = (nrm * (1.0 + sc_ref[...]) + sh_ref[...]).astype(BF16)

    row = pl.BlockSpec((tm, d), lambda i: (i, 0))
    in_specs, args = [row], [x]
    if has_res:
        in_specs += [row, _vec(d)]
        args += [y, g]
    if has_mul:
        in_specs.append(_vec(d))
        args.append(ymul)
    in_specs += [_vec(d)] * 3
    args += [gain, sc, sh]
    out_specs, out_shape = [], []
    if has_res:
        out_specs.append(row)
        out_shape.append(jax.ShapeDtypeStruct((n, d), F32))
    out_specs.append(row)
    out_shape.append(jax.ShapeDtypeStruct((n, d), BF16))
    res = pl.pallas_call(body, grid=(n // tm,), in_specs=in_specs, out_specs=out_specs, out_shape=out_shape,
                         name=name, compiler_params=_params("parallel"))(*args)
    return res if has_res else res[0]


def _norm_mod_bwd(da, x, gain, sc, *, name, dres=None, gate_y=None, gate_g=None, tm=512):
    n, d = x.shape
    tm = min(tm, n)
    has_res = dres is not None
    has_gate = gate_y is not None

    def body(*refs):
        it = iter(refs)
        da_ref, x_ref = next(it), next(it)
        r_ref = next(it) if has_res else None
        y_ref = next(it) if has_gate else None
        g_ref = next(it) if has_gate else None
        gain_ref, sc_ref = next(it), next(it)
        dx_ref, dsh_ref, dsc_ref, dgn_ref = next(it), next(it), next(it), next(it)
        dy_ref = next(it) if has_gate else None
        dg_ref = next(it) if has_gate else None
        i = pl.program_id(0)
        xv = x_ref[...]
        dav = da_ref[...]
        r = lax.rsqrt(jnp.mean(xv * xv, axis=-1, keepdims=True) + EPS)
        xh = xv * r
        nrm = xh * gain_ref[...]
        dn = dav * (1.0 + sc_ref[...])
        dxh = dn * gain_ref[...]
        dx = r * (dxh - xh * jnp.mean(dxh * xh, axis=-1, keepdims=True))
        if has_res:
            dx = dx + r_ref[...]
        dx_ref[...] = dx
        _acc_out(dsh_ref, i, _colsum(dav))
        _acc_out(dsc_ref, i, _colsum(dav * nrm))
        _acc_out(dgn_ref, i, _colsum(dn * xh))
        if has_gate:
            dy_ref[...] = (dx * g_ref[...]).astype(BF16)
            _acc_out(dg_ref, i, _colsum(dx * y_ref[...]))

    row = pl.BlockSpec((tm, d), lambda i: (i, 0))
    in_specs, args = [row, row], [da, x]
    if has_res:
        in_specs.append(row)
        args.append(dres)
    if has_gate:
        in_specs += [row, _vec(d)]
        args += [gate_y, gate_g]
    in_specs += [_vec(d)] * 2
    args += [gain, sc]
    vec_shape = jax.ShapeDtypeStruct((1, d), F32)
    out_specs = [row, _vec(d), _vec(d), _vec(d)]
    out_shape = [jax.ShapeDtypeStruct((n, d), F32), vec_shape, vec_shape, vec_shape]
    if has_gate:
        out_specs += [row, _vec(d)]
        out_shape += [jax.ShapeDtypeStruct((n, d), BF16), vec_shape]
    return pl.pallas_call(
        body, grid=(n // tm,), in_specs=in_specs, out_specs=out_specs, out_shape=out_shape,
        name=name, compiler_params=_params("arbitrary"))(*args)


def _ffn_up_glu(f, w_up, cw, cb, *, name, tm=256, tc=256):
    n, d = f.shape
    tm = min(tm, n)
    ni = n // tm
    nc = DFF // tc
    halo = 16
    rows = tm + 2 * halo
    r = tm // halo
    last = n // halo - 1

    def body(f_ref, fp_ref, fn_ref, w_ref, cw_ref, cb_ref, u_ref, gc_ref, h_ref):
        i = pl.program_id(0)
        a = f_ref[...]
        aext = jnp.concatenate([jnp.where(i > 0, fp_ref[...], jnp.zeros_like(fp_ref[...])), a,
                                jnp.where(i < ni - 1, fn_ref[...], jnp.zeros_like(fn_ref[...]))], axis=0)
        for j in range(nc):
            cols = slice(j * tc, (j + 1) * tc)
            vcols = slice(DFF + j * tc, DFF + (j + 1) * tc)
            gext = jnp.dot(aext, w_ref[:, cols], preferred_element_type=F32)
            val = jnp.dot(a, w_ref[:, vcols], preferred_element_type=F32)
            gate = gext[halo:halo + tm]
            gc = (pltpu.roll(gext, 1, axis=0)[halo:halo + tm] * cw_ref[0:1, cols] + gate * cw_ref[1:2, cols]
                  + pltpu.roll(gext, rows - 1, axis=0)[halo:halo + tm] * cw_ref[2:3, cols]) + cb_ref[:, cols]
            u_ref[:, cols] = gate
            u_ref[:, vcols] = val
            gc_ref[:, cols] = gc
            h_ref[:, cols] = (gc * _sigmoid(gc) * val).astype(BF16)

    return pl.pallas_call(
        body, grid=(ni,),
        in_specs=[pl.BlockSpec((tm, d), lambda i: (i, 0)),
                  pl.BlockSpec((halo, d), lambda i: (jnp.maximum(i * r - 1, 0), 0)),
                  pl.BlockSpec((halo, d), lambda i: (jnp.minimum((i + 1) * r, last), 0)),
                  pl.BlockSpec(w_up.shape, lambda i: (0, 0)), pl.BlockSpec((3, DFF), lambda i: (0, 0)),
                  pl.BlockSpec((1, DFF), lambda i: (0, 0))],
        out_specs=[pl.BlockSpec((tm, 2 * DFF), lambda i: (i, 0)), pl.BlockSpec((tm, DFF), lambda i: (i, 0)),
                   pl.BlockSpec((tm, DFF), lambda i: (i, 0))],
        out_shape=[jax.ShapeDtypeStruct((n, 2 * DFF), F32), jax.ShapeDtypeStruct((n, DFF), F32),
                   jax.ShapeDtypeStruct((n, DFF), BF16)], name=name,
        compiler_params=_params("parallel"))(f, f, f, w_up, cw, cb)


def _ffn_down_glu_bwd(dz, w_down, u, gc, cw, *, name, tm=256, tc=256):
    n, d = dz.shape
    tm = min(tm, n)
    ni = n // tm
    nc = DFF // tc
    rows = tm + 2 * HALO

    def body(z_ref, zp_ref, zn_ref, w_ref, u_ref, vp_ref, vn_ref, c_ref, cp_ref, cn_ref, cw_ref,
             dg_ref, dv_ref, dcw_ref, dcb_ref):
        i = pl.program_id(0)

        @pl.when(i == 0)
        def _():
            dcw_ref[...] = jnp.zeros_like(dcw_ref)
            dcb_ref[...] = jnp.zeros_like(dcb_ref)

        zext = jnp.concatenate([jnp.where(i > 0, zp_ref[...], jnp.zeros_like(zp_ref[...])), z_ref[...],
                                jnp.where(i < ni - 1, zn_ref[...], jnp.zeros_like(zn_ref[...]))], axis=0)
        for j in range(nc):
            cols = slice(j * tc, (j + 1) * tc)
            vcols = slice(DFF + j * tc, DFF + (j + 1) * tc)
            dh = lax.dot_general(zext, w_ref[cols, :], _NT, preferred_element_type=F32)[HALO:HALO + rows]
            gcx = jnp.concatenate([cp_ref[:, cols], c_ref[:, cols], cn_ref[:, cols]], axis=0)
            vext = jnp.concatenate([vp_ref[:, cols], u_ref[:, vcols], vn_ref[:, cols]], axis=0)
            sg = _sigmoid(gcx)
            dgc = dh * vext * (sg * (1.0 + gcx * (1.0 - sg)))
            dv_ref[:, cols] = (dh[HALO:HALO + tm] * (gcx[HALO:HALO + tm] * sg[HALO:HALO + tm])).astype(BF16)
            d_next = pltpu.roll(dgc, rows - 1, axis=0)[HALO:HALO + tm]
            d_prev = pltpu.roll(dgc, 1, axis=0)[HALO:HALO + tm]
            d_here = dgc[HALO:HALO + tm]
            dg_ref[:, cols] = (d_next * cw_ref[0:1, cols] + d_here * cw_ref[1:2, cols]
                               + d_prev * cw_ref[2:3, cols]).astype(BF16)
            gate = u_ref[:, cols]
            dcw_ref[:, cols] += jnp.concatenate([_colsum(d_next * gate), _colsum(d_here * gate),
                                                 _colsum(d_prev * gate)], axis=0)
            dcb_ref[:, cols] += _colsum(d_here)

    def trio(width, halo, tile_width=None, colblk=0):
        r, last = tm // halo, n // halo - 1
        return [pl.BlockSpec((tm, tile_width or width), lambda i: (i, 0)),
                pl.BlockSpec((halo, width), lambda i: (jnp.maximum(i * r - 1, 0), colblk)),
                pl.BlockSpec((halo, width), lambda i: (jnp.minimum((i + 1) * r, last), colblk))]

    whole = lambda shape: pl.BlockSpec(shape, lambda i: (0, 0))
    return pl.pallas_call(
        body, grid=(ni,),
        in_specs=(trio(d, 16) + [whole(w_down.shape)] + trio(DFF, HALO, tile_width=2 * DFF, colblk=1)
                  + trio(DFF, HALO) + [whole((3, DFF))]),
        out_specs=[pl.BlockSpec((tm, DFF), lambda i: (i, 0)), pl.BlockSpec((tm, DFF), lambda i: (i, 0)),
                   whole((3, DFF)), whole((1, DFF))],
        out_shape=[jax.ShapeDtypeStruct((n, DFF), BF16), jax.ShapeDtypeStruct((n, DFF), BF16),
                   jax.ShapeDtypeStruct((3, DFF), F32), jax.ShapeDtypeStruct((1, DFF), F32)],
        name=name, compiler_params=_params("arbitrary"))(dz, dz, dz, w_down, u, u, u, gc, gc, gc, cw)


def _rope_tables(n):
    rows = n // GRID_W
    axis_dim = HD // 2
    inv_freq = jnp.power(ROPE_THETA, -jnp.arange(0, axis_dim, 2, dtype=F32) / axis_dim)
    ar = jnp.arange(rows, dtype=F32)[:, None] * inv_freq
    ac = jnp.arange(GRID_W, dtype=F32)[:, None] * inv_freq
    by_row = lambda a: jnp.repeat(a, GRID_W, axis=0)
    by_col = lambda a: jnp.tile(a, (rows, 1))
    cr, sr, cc, sc = by_row(jnp.cos(ar)), by_row(jnp.sin(ar)), by_col(jnp.cos(ac)), by_col(jnp.sin(ac))
    return jnp.concatenate([cr, cr, cc, cc], axis=1), jnp.concatenate([-sr, sr, -sc, sc], axis=1)


def _partner(v):
    lane = lax.broadcasted_iota(jnp.int32, v.shape, 1)
    return jnp.where((lane % 64) < 32, pltpu.roll(v, HD - 32, axis=1), pltpu.roll(v, 32, axis=1))


def _qkv_prep(p, q_gain, k_gain, cs, sn, *, name, has_q, kv_col, kv_rows=None, kv_row_off=0, kv_into=None, tm=256):
    n = p.shape[0]
    rope = cs is not None
    kv_rows = kv_rows or n
    rb = kv_row_off // tm

    def body(*refs):
        it = iter(refs)
        q_ref = next(it) if has_q else None
        kv_ref = next(it)
        qg_ref, kg_ref = next(it), next(it)
        cs_ref = next(it) if rope else None
        sn_ref = next(it) if rope else None
        if kv_into is not None:
            next(it), next(it)
        qo_ref = next(it) if has_q else None
        ko_ref, vo_ref = next(it), next(it)

        def norm_rope(xh, gain, mul=None):
            r = lax.rsqrt(jnp.mean(xh * xh, axis=-1, keepdims=True) + EPS)
            xn = (xh * r) * gain
            if rope:
                xn = xn * cs_ref[...] + _partner(xn) * sn_ref[...]
            if mul is not None:
                xn = xn * mul
            return xn.astype(BF16)

        if has_q:
            for h in range(NQ):
                qo_ref[h] = norm_rope(q_ref[:, h * HD:(h + 1) * HD], qg_ref[...], _QSCALE)
        for h in range(NKV):
            ko_ref[h] = norm_rope(kv_ref[:, h * HD:(h + 1) * HD], kg_ref[...])
            vo_ref[h] = kv_ref[:, (NKV + h) * HD:(NKV + h + 1) * HD].astype(BF16)

    in_specs, args = [], []
    if has_q:
        in_specs.append(pl.BlockSpec((tm, AW), lambda i: (i, 0)))
        args.append(p)
    in_specs += [pl.BlockSpec((tm, 2 * NKV * HD), lambda i: (i, kv_col)), _vec(HD), _vec(HD)]
    args += [p, q_gain, k_gain]
    if rope:
        in_specs += [pl.BlockSpec((tm, HD), lambda i: (i, 0))] * 2
        args += [cs, sn]
    out_specs, out_shape = [], []
    if has_q:
        out_specs.append(pl.BlockSpec((NQ, tm, HD), lambda i: (0, i, 0)))
        out_shape.append(jax.ShapeDtypeStruct((NQ, n, HD), BF16))
    out_specs += [pl.BlockSpec((NKV, tm, HD), lambda i: (0, rb + i, 0))] * 2
    out_shape += [jax.ShapeDtypeStruct((NKV, kv_rows, HD), BF16)] * 2
    aliases = {}
    if kv_into is not None:
        aliases = {len(args): int(has_q), len(args) + 1: int(has_q) + 1}
        in_specs += [pl.BlockSpec(memory_space=pl.ANY)] * 2
        args += list(kv_into)
    return pl.pallas_call(body, grid=(n // tm,), in_specs=in_specs, out_specs=out_specs, out_shape=out_shape,
                          input_output_aliases=aliases, name=name, compiler_params=_params("parallel"))(*args)


def _qkv_bwd(p, dq, dk, dv, q_gain, k_gain, cs, sn, *, name, has_q, kv_col, kv_row_off, tm=256):
    n = p.shape[0]
    rope = cs is not None
    rb = kv_row_off // tm

    def body(*refs):
        it = iter(refs)
        q_ref = next(it) if has_q else None
        kv_ref = next(it)
        dq_ref = next(it) if has_q else None
        dk_ref, dv_ref = next(it), next(it)
        qg_ref, kg_ref = next(it), next(it)
        cs_ref = next(it) if rope else None
        sn_ref = next(it) if rope else None
        dp_ref, dqg_ref, dkg_ref = next(it), next(it), next(it)
        i = pl.program_id(0)

        def back(xh, dout, gain):
            if rope:
                dout = dout * cs_ref[...] + _partner(dout * sn_ref[...])
            r = lax.rsqrt(jnp.mean(xh * xh, axis=-1, keepdims=True) + EPS)
            xhat = xh * r
            dxh = dout * gain
            dx = r * (dxh - xhat * jnp.mean(dxh * xhat, axis=-1, keepdims=True))
            return dx, _colsum(dout * xhat)

        dqg = jnp.zeros((1, HD), F32)
        dkg = jnp.zeros((1, HD), F32)
        if has_q:
            for h in range(NQ):
                dx, dg = back(q_ref[:, h * HD:(h + 1) * HD], dq_ref[h], qg_ref[...])
                dp_ref[:, h * HD:(h + 1) * HD] = dx.astype(BF16)
                dqg = dqg + dg
        else:
            dp_ref[:, 0:AW] = jnp.zeros((tm, AW), BF16)
        for h in range(NKV):
            dx, dg = back(kv_ref[:, h * HD:(h + 1) * HD], dk_ref[h], kg_ref[...])
            dp_ref[:, AW + h * HD:AW + (h + 1) * HD] = dx.astype(BF16)
            dkg = dkg + dg
            dp_ref[:, AW + (NKV + h) * HD:AW + (NKV + h + 1) * HD] = dv_ref[h].astype(BF16)
        _acc_out(dqg_ref, i, dqg)
        _acc_out(dkg_ref, i, dkg)

    in_specs, args = [], []
    if has_q:
        in_specs.append(pl.BlockSpec((tm, AW), lambda i: (i, 0)))
        args.append(p)
    in_specs.append(pl.BlockSpec((tm, 2 * NKV * HD), lambda i: (i, kv_col)))
    args.append(p)
    if has_q:
        in_specs.append(pl.BlockSpec((NQ, tm, HD), lambda i: (0, i, 0)))
        args.append(dq)
    in_specs += [pl.BlockSpec((NKV, tm, HD), lambda i: (0, rb + i, 0))] * 2 + [_vec(HD), _vec(HD)]
    args += [dk, dv, q_gain, k_gain]
    if rope:
        in_specs += [pl.BlockSpec((tm, HD), lambda i: (i, 0))] * 2
        args += [cs, sn]
    return pl.pallas_call(
        body, grid=(n // tm,), in_specs=in_specs,
        out_specs=[pl.BlockSpec((tm, D), lambda i: (i, 0)), _vec(HD), _vec(HD)],
        out_shape=[jax.ShapeDtypeStruct((n, D), BF16), jax.ShapeDtypeStruct((1, HD), F32),
                   jax.ShapeDtypeStruct((1, HD), F32)],
        name=name, compiler_params=_params("arbitrary"))(*args)


def _conv_gate_fwd(p, o, conv_w, *, name, tm=256):
    n = p.shape[0]
    ni = n // tm

    def body(gb_ref, gc_ref, gcp_ref, gcn_ref, xi_ref, xip_ref, xin_ref, o_ref, w_ref, cat_ref):
        i = pl.program_id(0)
        hext = _ext(gcp_ref, gc_ref, gcn_ref, i, ni) * _ext(xip_ref, xi_ref, xin_ref, i, ni)
        cat_ref[:, 0:AW] = o_ref[...].astype(BF16)
        cat_ref[:, AW:D] = (gb_ref[...] * _conv3(hext, w_ref, tm)).astype(BF16)

    gcp, gcn = _halo_specs(tm, CW, n, colblk=3)
    xip, xin = _halo_specs(tm, CW, n, colblk=4)
    return pl.pallas_call(
        body, grid=(ni,),
        in_specs=[pl.BlockSpec((tm, CW), lambda i: (i, 2)), pl.BlockSpec((tm, CW), lambda i: (i, 3)), gcp, gcn,
                  pl.BlockSpec((tm, CW), lambda i: (i, 4)), xip, xin, pl.BlockSpec((tm, AW), lambda i: (i, 0)),
                  pl.BlockSpec((3, CW), lambda i: (0, 0))],
        out_specs=pl.BlockSpec((tm, D), lambda i: (i, 0)), out_shape=jax.ShapeDtypeStruct((n, D), BF16),
        name=name, compiler_params=_params("parallel"))(p, p, p, p, p, p, p, o, conv_w)


def _conv_gate_bwd(dcat, p, conv_w, *, name, tm=256):
    n = p.shape[0]
    ni = n // tm

    def body(dc_ref, dcp_ref, dcn_ref, gb_ref, gbp_ref, gbn_ref, gc_ref, gcp_ref, gcn_ref, xi_ref, xip_ref, xin_ref,
             w_ref, dp_ref, dw_ref):
        i = pl.program_id(0)
        gcext = _ext(gcp_ref, gc_ref, gcn_ref, i, ni)
        xiext = _ext(xip_ref, xi_ref, xin_ref, i, ni)
        hext = gcext * xiext
        dcv = _ext(dcp_ref, dc_ref, dcn_ref, i, ni) * _ext(gbp_ref, gb_ref, gbn_ref, i, ni)
        dp_ref[:, 0:CW] = (dc_ref[...] * _conv3(hext, w_ref, tm)).astype(BF16)
        dh = _sh(dcv, 1, tm) * w_ref[0:1, :] + _sh(dcv, 0, tm) * w_ref[1:2, :] + _sh(dcv, -1, tm) * w_ref[2:3, :]
        dp_ref[:, CW:2 * CW] = (dh * xi_ref[...]).astype(BF16)
        dp_ref[:, 2 * CW:3 * CW] = (dh * gc_ref[...]).astype(BF16)
        dcv_t = dcv[HALO:HALO + tm]
        dw = jnp.concatenate([_colsum(dcv_t * _sh(hext, -1, tm)), _colsum(dcv_t * _sh(hext, 0, tm)),
                              _colsum(dcv_t * _sh(hext, 1, tm))], axis=0)
        _acc_out(dw_ref, i, dw)

    def trio(colblk):
        prev, nxt = _halo_specs(tm, CW, n, colblk=colblk)
        return [pl.BlockSpec((tm, CW), lambda i: (i, colblk)), prev, nxt]

    return pl.pallas_call(
        body, grid=(ni,), in_specs=trio(1) + trio(2) + trio(3) + trio(4) + [pl.BlockSpec((3, CW), lambda i: (0, 0))],
        out_specs=[pl.BlockSpec((tm, 3 * CW), lambda i: (i, 0)), pl.BlockSpec((3, CW), lambda i: (0, 0))],
        out_shape=[jax.ShapeDtypeStruct((n, 3 * CW), BF16), jax.ShapeDtypeStruct((3, CW), F32)],
        name=name, compiler_params=_params("arbitrary"))(dcat, dcat, dcat, p, p, p, p, p, p, p, p, p, conv_w)


def _attn_fwd(q, k, v, *, name, bq=512, sub=256):
    n = q.shape[1]
    t = k.shape[1]
    bq = min(bq, n)
    sub = min(sub, 2 * bq)

    def body(q_ref, k_ref, v_ref, o_ref, lse_ref):
        q2 = q_ref[...].reshape(2 * bq, HD)
        outs, lses = [], []
        for r0 in range(0, 2 * bq, sub):
            s = lax.dot_general(q2[r0:r0 + sub], k_ref[0], _NT, preferred_element_type=F32)
            m = jnp.max(s, axis=-1, keepdims=True)
            pv = jnp.exp2(s - m)
            l = jnp.sum(pv, axis=-1, keepdims=True)
            outs.append(jnp.dot(pv.astype(BF16), v_ref[0], preferred_element_type=F32) / l)
            lses.append(m + jnp.log2(l))
        out = jnp.concatenate(outs, axis=0)
        o_ref[:, 0:HD] = out[0:bq]
        o_ref[:, HD:2 * HD] = out[bq:2 * bq]
        lse_ref[...] = jnp.concatenate(lses, axis=0).reshape(2, bq, 1)

    kspec = pl.BlockSpec((1, t, HD), lambda h, i: (h, 0, 0))
    return pl.pallas_call(
        body, grid=(NKV, n // bq),
        in_specs=[pl.BlockSpec((2, bq, HD), lambda h, i: (h, i, 0)), kspec, kspec],
        out_specs=[pl.BlockSpec((bq, 2 * HD), lambda h, i: (i, h)), pl.BlockSpec((2, bq, 1), lambda h, i: (h, i, 0))],
        out_shape=[jax.ShapeDtypeStruct((n, AW), F32), jax.ShapeDtypeStruct((NQ, n, 1), F32)],
        name=name, compiler_params=_params("parallel", "parallel"))(q, k, v)


def _attn_bwd(q, k, v, dcat, o, lse, *, name, bq=256):
    n = q.shape[1]
    t = k.shape[1]
    bq = min(bq, n)

    def body(q_ref, k_ref, v_ref, dc_ref, o_ref, lse_ref, dq_ref, dk_ref, dv_ref):
        @pl.when(pl.program_id(1) == 0)
        def _():
            dk_ref[...] = jnp.zeros_like(dk_ref)
            dv_ref[...] = jnp.zeros_like(dv_ref)

        q2 = q_ref[...].reshape(2 * bq, HD)
        do_f = jnp.concatenate([dc_ref[:, 0:HD], dc_ref[:, HD:2 * HD]], axis=0)
        o_f = jnp.concatenate([o_ref[:, 0:HD], o_ref[:, HD:2 * HD]], axis=0)
        delta = jnp.sum(do_f * o_f, axis=-1, keepdims=True)
        do2 = do_f.astype(BF16)
        s = lax.dot_general(q2, k_ref[0], _NT, preferred_element_type=F32)
        pv = jnp.exp2(s - lse_ref[...].reshape(2 * bq, 1))
        dp = lax.dot_general(do2, v_ref[0], _NT, preferred_element_type=F32)
        ds = (pv * (dp - delta)).astype(BF16)
        dq_ref[...] = (jnp.dot(ds, k_ref[0], preferred_element_type=F32) * _SCALE).reshape(2, bq, HD)
        dk_ref[0] += lax.dot_general(ds, q2, _TN, preferred_element_type=F32) * _LN2
        dv_ref[0] += lax.dot_general(pv.astype(BF16), do2, _TN, preferred_element_type=F32)

    qspec = pl.BlockSpec((2, bq, HD), lambda h, i: (h, i, 0))
    kspec = pl.BlockSpec((1, t, HD), lambda h, i: (h, 0, 0))
    sspec = pl.BlockSpec((2, bq, 1), lambda h, i: (h, i, 0))
    cspec = pl.BlockSpec((bq, 2 * HD), lambda h, i: (i, h))
    return pl.pallas_call(
        body, grid=(NKV, n // bq), in_specs=[qspec, kspec, kspec, cspec, cspec, sspec], out_specs=[qspec, kspec, kspec],
        out_shape=[jax.ShapeDtypeStruct((NQ, n, HD), F32), jax.ShapeDtypeStruct((NKV, t, HD), F32),
                   jax.ShapeDtypeStruct((NKV, t, HD), F32)],
        name=name, compiler_params=_params("parallel", "arbitrary"))(q, k, v, dcat, o, lse)


def _window_sums(ext, w):
    s, step = ext, 1
    while step < w:
        s = s + _roll_rows(s, step)
        step *= 2
    return s


def _pool_counts(i, tm, n, w, rows, first):
    t = i * tm - HALO + first + lax.broadcasted_iota(jnp.int32, (rows, 1), 0)
    lo = jnp.clip(t - w // 2, 0, n)
    hi = jnp.clip(t + w - w // 2, 0, n)
    return jnp.maximum(hi - lo, 1).astype(F32)


def _norm_mod_ext(xext, gain_ref, sc_ref, sh_ref, i, tm, n):
    rows = xext.shape[0]
    t = i * tm - HALO + lax.broadcasted_iota(jnp.int32, (rows, 1), 0)
    inside = (t >= 0) & (t < n)
    r = lax.rsqrt(jnp.mean(xext * xext, axis=-1, keepdims=True) + EPS)
    xh = xext * r
    a = (xh * gain_ref[...]) * (1.0 + sc_ref[...]) + sh_ref[...]
    return jnp.where(inside, a, 0.0), r, xh


def _pool_fwd(x, y, g, gain, sc, sh, pool_w, *, name, tm=256):
    n, d = x.shape
    ni = n // tm

    def body(x_ref, xp_ref, xn_ref, y_ref, yp_ref, yn_ref, g_ref, gain_ref, sc_ref, sh_ref, w_ref, xo_ref, o_ref):
        i = pl.program_id(0)
        xext = _ext(xp_ref, x_ref, xn_ref, i, ni) + g_ref[...] * _ext(yp_ref, y_ref, yn_ref, i, ni)
        xo_ref[...] = xext[HALO:HALO + tm]
        aext, _, _ = _norm_mod_ext(xext, gain_ref, sc_ref, sh_ref, i, tm, n)
        for gi, w in enumerate(POOL_WINDOWS):
            ag = aext[:, gi * PG:(gi + 1) * PG]
            mean = _sh(_window_sums(ag, w), -(w // 2), tm) / _pool_counts(i, tm, n, w, tm, HALO)
            pooled = mean - ag[HALO:HALO + tm]
            o_ref[:, gi * PG:(gi + 1) * PG] = jnp.dot(pooled.astype(BF16), w_ref[gi], preferred_element_type=F32)

    row = pl.BlockSpec((tm, d), lambda i: (i, 0))
    prev, nxt = _halo_specs(tm, d, n)
    return pl.pallas_call(
        body, grid=(ni,),
        in_specs=[row, prev, nxt, row, prev, nxt, _vec(d), _vec(d), _vec(d), _vec(d),
                  pl.BlockSpec((4, PG, PG), lambda i: (0, 0, 0))],
        out_specs=[row, row], out_shape=[jax.ShapeDtypeStruct((n, d), F32)] * 2,
        name=name, compiler_params=_params("parallel"))(x, x, x, y, y, y, g, gain, sc, sh, pool_w)


def _pool_bwd(dxo, mixed, x, g, scale, gain, sc, sh, pool_w, zprev, gprev, *, name, tm=256):
    n, d = x.shape
    ni = n // tm

    def body(dx_ref, dxp_ref, dxn_ref, mx_ref, x_ref, xp_ref, xn_ref, g_ref, s_ref, gain_ref, sc_ref, sh_ref, w_ref,
             zp_ref, gp_ref, dxi_ref, dw_ref, dg_ref, dsl_ref, dsh_ref, dsc_ref, dgn_ref, dzp_ref, dgp_ref):
        i = pl.program_id(0)

        @pl.when(i == 0)
        def _():
            dw_ref[...] = jnp.zeros_like(dw_ref)

        dxo_t = dx_ref[...]
        mixed_t = mx_ref[...]
        dy_t = dxo_t * g_ref[...]
        _acc_out(dg_ref, i, _colsum(dxo_t * (mixed_t * s_ref[...])))
        _acc_out(dsl_ref, i, _colsum(dy_t * mixed_t))
        dmixed = (_ext(dxp_ref, dx_ref, dxn_ref, i, ni) * g_ref[...]) * s_ref[...]
        xext = _ext(xp_ref, x_ref, xn_ref, i, ni)
        aext, rext, xhext = _norm_mod_ext(xext, gain_ref, sc_ref, sh_ref, i, tm, n)
        rows = tm + 2 * HALO
        da_parts = []
        for gi, w in enumerate(POOL_WINDOWS):
            sl = slice(gi * PG, (gi + 1) * PG)
            ag = aext[:, sl]
            mean = _sh(_window_sums(ag, w), -(w // 2), tm) / _pool_counts(i, tm, n, w, tm, HALO)
            pooled = (mean - ag[HALO:HALO + tm]).astype(BF16)
            dmg = dmixed[:, sl].astype(BF16)
            dw_ref[gi] += lax.dot_general(pooled, dmixed[HALO:HALO + tm, sl].astype(BF16), _TN,
                                          preferred_element_type=F32)
            dpl = lax.dot_general(dmg, w_ref[gi], _NT, preferred_element_type=F32)
            e = dpl / _pool_counts(i, tm, n, w, rows, 0)
            da_parts.append(_sh(_window_sums(e, w), 1 - w // 2, tm) - dpl[HALO:HALO + tm])
        da = jnp.concatenate(da_parts, axis=1)
        r = rext[HALO:HALO + tm]
        xh = xhext[HALO:HALO + tm]
        nrm = xh * gain_ref[...]
        dn = da * (1.0 + sc_ref[...])
        dxh = dn * gain_ref[...]
        dxi = dxo_t + r * (dxh - xh * jnp.mean(dxh * xh, axis=-1, keepdims=True))
        dxi_ref[...] = dxi
        _acc_out(dsh_ref, i, _colsum(da))
        _acc_out(dsc_ref, i, _colsum(da * nrm))
        _acc_out(dgn_ref, i, _colsum(dn * xh))
        dzp_ref[...] = (dxi * gp_ref[...]).astype(BF16)
        _acc_out(dgp_ref, i, _colsum(dxi * zp_ref[...]))

    row = pl.BlockSpec((tm, d), lambda i: (i, 0))
    prev, nxt = _halo_specs(tm, d, n)
    wspec = pl.BlockSpec((4, PG, PG), lambda i: (0, 0, 0))
    vshape = jax.ShapeDtypeStruct((1, d), F32)
    return pl.pallas_call(
        body, grid=(ni,),
        in_specs=[row, prev, nxt, row, row, prev, nxt] + [_vec(d)] * 5 + [wspec, row, _vec(d)],
        out_specs=[row, wspec] + [_vec(d)] * 5 + [row, _vec(d)],
        out_shape=[jax.ShapeDtypeStruct((n, d), F32), jax.ShapeDtypeStruct((4, PG, PG), F32)] + [vshape] * 5
        + [jax.ShapeDtypeStruct((n, d), BF16), vshape],
        name=name, compiler_params=_params("arbitrary"))(dxo, dxo, dxo, mixed, x, x, x, g, scale, gain, sc, sh, pool_w,
                                                         zprev, gprev)


def _adamw(gparts_list, w, m, v, *, name, silu_grad_of=None):
    nl = len(gparts_list)
    nparts, r, c = gparts_list[0].shape
    tr = _pick(r, (256, 128, 64, 32, 16, 8))
    has_c = silu_grad_of is not None

    def body(*refs):
        gp_refs = refs[:nl]
        it = iter(refs[nl:])
        w_ref, m_ref, v_ref = next(it), next(it), next(it)
        c_ref = next(it) if has_c else None
        g_ref, d_ref, mo_ref, vo_ref = next(it), next(it), next(it), next(it)
        layer = pl.program_id(0)

        def update(gp_ref):
            g = gp_ref[0].astype(F32)
            for p in range(1, nparts):
                g = g + gp_ref[p].astype(F32)
            if has_c:
                cv = c_ref[0]
                sg = _sigmoid(cv)
                g = g * (sg * (1.0 + cv * (1.0 - sg)))
            g_ref[0] = g
            mn = ADAM_B1 * m_ref[0] + (1.0 - ADAM_B1) * g
            vn = ADAM_B2 * v_ref[0] + (1.0 - ADAM_B2) * (g * g)
            m_hat = mn / (1.0 - ADAM_B1 ** ADAM_STEP)
            v_hat = vn / (1.0 - ADAM_B2 ** ADAM_STEP)
            d_ref[0] = -ADAM_LR * (m_hat / (jnp.sqrt(v_hat) + ADAM_EPS) + ADAM_WD * w_ref[0])
            mo_ref[0] = mn
            vo_ref[0] = vn

        if nl == 1:
            update(gp_refs[0])
        else:
            for li in range(nl):
                pl.when(layer == li)(functools.partial(update, gp_refs[li]))

    row = pl.BlockSpec((1, tr, c), lambda l, i: (l, i, 0))
    in_specs = [pl.BlockSpec((nparts, tr, c), lambda l, i, li=li: (0, jnp.where(l == li, i, 0), 0)) for li in range(nl)]
    in_specs += [row, row, row]
    args = list(gparts_list) + [w, m, v]
    if has_c:
        in_specs.append(row)
        args.append(silu_grad_of)
    return pl.pallas_call(
        body, grid=(nl, r // tr), in_specs=in_specs, out_specs=[row] * 4,
        out_shape=[jax.ShapeDtypeStruct((nl, r, c), F32)] * 4, name=name,
        compiler_params=_params("arbitrary", "arbitrary"))(*args)


def _adamw_nd(gparts, w, m, v, *, name, silu_grad_of=None):
    shape = w.shape
    c = shape[-1]
    if isinstance(gparts, (list, tuple)):
        nl = len(gparts)
        r = math.prod(shape[1:-1])
    else:
        nl = 1
        r = math.prod(shape[:-1]) if len(shape) > 1 else 1
        gparts = [gparts]
    rs = lambda a: a.reshape(nl, r, c)
    res = _adamw([gp.reshape(gp.shape[0], r, c) for gp in gparts], rs(w), rs(m), rs(v), name=name,
                 silu_grad_of=None if silu_grad_of is None else rs(silu_grad_of))
    return [a.reshape(shape) for a in res]


def _place():
    return lax.axis_index("x"), lax.axis_index("y"), lax.axis_index("c")


def _all_gather(arrs, *, name):
    k_arr = len(arrs)

    def body(*refs):
        ins = refs[:k_arr]
        outs = refs[k_arr:2 * k_arr]
        send_sems, recv_sems, local_sems = refs[2 * k_arr:]
        x, y, c = _place()
        me, sibling = (x, y, c), (x, y, 1 - c)
        chips = [(1 - x, y), (x, 1 - y), (1 - x, 1 - y)]

        def slot(a, px, py, pc):
            return outs[a].at[4 * px + 2 * py + pc]

        def copy(a, s, block, to, src=None):
            return pltpu.make_async_remote_copy(
                src_ref=slot(a, *block) if src is None else src, dst_ref=slot(a, *block),
                send_sem=send_sems.at[a, s], recv_sem=recv_sems.at[a, s], device_id=to, device_id_type=MESH)

        mine = [pltpu.make_async_copy(ins[a], slot(a, *me), local_sems.at[a]) for a in range(k_arr)]
        for cp in mine:
            cp.start()
        first = []
        for a in range(k_arr):
            first.append(copy(a, 0, me, sibling, src=ins[a]))
            first += [copy(a, 1 + j, me, (*chip, c), src=ins[a]) for j, chip in enumerate(chips)]
        for cp in first:
            cp.start()
        passed = []
        for j, chip in enumerate(chips):
            for a in range(k_arr):
                copy(a, 1 + j, (*chip, c), me).wait_recv()
                fw = copy(a, 4 + j, (*chip, c), sibling)
                fw.start()
                passed.append(fw)
        for a in range(k_arr):
            copy(a, 0, sibling, me).wait_recv()
            for j, chip in enumerate(chips):
                copy(a, 4 + j, (*chip, 1 - c), me).wait_recv()
        for cp in first + passed:
            cp.wait_send()
        for cp in mine:
            cp.wait()

    any_spec = pl.BlockSpec(memory_space=pl.ANY)
    return pl.pallas_call(
        body, in_specs=[any_spec] * k_arr, out_specs=[any_spec] * k_arr,
        out_shape=[jax.ShapeDtypeStruct((NDEV,) + a.shape, a.dtype) for a in arrs],
        scratch_shapes=[pltpu.SemaphoreType.DMA((k_arr, 7)), pltpu.SemaphoreType.DMA((k_arr, 7)),
                        pltpu.SemaphoreType.DMA((k_arr,))],
        name=name)(*arrs)


_HBM = pl.BlockSpec(memory_space=pltpu.HBM)
_SEM = pl.BlockSpec(memory_space=pltpu.SEMAPHORE)
_EFFECT = pltpu.SideEffectType.DATAFLOW_SIDE_EFFECTING


def _peers(x, y, c):
    return [(x ^ (rel >> 2), y ^ ((rel >> 1) & 1), c ^ (rel & 1)) for rel in range(1, NDEV)]


def _exchange_copies(srcs, lands, send_sems, recv_sems, scatter):
    x, y, c = _place()
    me = 4 * x + 2 * y + c
    copies = []
    for r, (px, py, pc) in enumerate(_peers(x, y, c)):
        peer = 4 * px + 2 * py + pc
        for a in range(len(srcs)):
            copies.append(pltpu.make_async_remote_copy(
                src_ref=srcs[a].at[peer] if scatter else srcs[a], dst_ref=lands[a].at[me],
                send_sem=send_sems.at[7 * a + r], recv_sem=recv_sems.at[7 * a + r], device_id=(px, py, pc),
                device_id_type=MESH))
    return copies


def _exchange_start(arrs, *, scatter, name):
    k_arr = len(arrs)
    land_shapes = [a.shape if scatter else (NDEV,) + a.shape for a in arrs]
    lands = [pltpu.with_memory_space_constraint(lax.empty(s, a.dtype), pltpu.HBM) for s, a in zip(land_shapes, arrs)]
    srcs = [pltpu.with_memory_space_constraint(a, pltpu.HBM) for a in arrs]

    def body(*refs):
        src_refs, land_refs = refs[:k_arr], refs[k_arr:2 * k_arr]
        send_sems, recv_sems = refs[2 * k_arr], refs[2 * k_arr + 1]
        token = refs[-1]
        for cp in _exchange_copies(src_refs, land_refs, send_sems, recv_sems, scatter):
            cp.start()
        token[...] = jnp.zeros_like(token)

    out_shape = ([pltpu.SemaphoreType.DMA((7 * k_arr,)), pltpu.SemaphoreType.DMA((7 * k_arr,))]
                 + [pltpu.HBM(a.shape, a.dtype) for a in arrs] + [pltpu.HBM(s, a.dtype) for s, a in zip(land_shapes, arrs)]
                 + [jax.ShapeDtypeStruct((8, 128), F32)])
    res = pl.pallas_call(
        body, name=name, out_shape=out_shape, in_specs=[_HBM] * (2 * k_arr),
        out_specs=[_SEM, _SEM] + [_HBM] * (2 * k_arr) + [pl.BlockSpec(memory_space=pltpu.VMEM)],
        input_output_aliases={i: 2 + i for i in range(2 * k_arr)},
        compiler_params=pltpu.CompilerParams(has_side_effects=_EFFECT))(*srcs, *lands)
    return dict(send=res[0], recv=res[1], srcs=list(res[2:2 + k_arr]), lands=list(res[2 + k_arr:2 + 2 * k_arr]),
                token=res[-1], scatter=scatter)


def _exchange_wait(handle, after, *, name):
    k_arr = len(handle["srcs"])
    scatter = handle["scatter"]

    def body(*refs):
        src_refs, land_refs = refs[:k_arr], refs[k_arr:2 * k_arr]
        send_sems, recv_sems = refs[2 * k_arr], refs[2 * k_arr + 1]
        x, y, c = _place()
        me = 4 * x + 2 * y + c
        for r, (px, py, pc) in enumerate(_peers(x, y, c)):
            peer = 4 * px + 2 * py + pc
            for a in range(k_arr):
                cp = pltpu.make_async_remote_copy(
                    src_ref=src_refs[a].at[peer] if scatter else src_refs[a], dst_ref=land_refs[a].at[peer],
                    send_sem=send_sems.at[7 * a + r], recv_sem=recv_sems.at[7 * a + r], device_id=(x, y, c),
                    device_id_type=MESH)
                cp.wait_send()
                cp.wait_recv()

    arrs = handle["srcs"] + handle["lands"]
    res = pl.pallas_call(
        body, name=name, out_shape=[pltpu.HBM(a.shape, a.dtype) for a in arrs],
        in_specs=[_HBM] * (2 * k_arr) + [_SEM, _SEM, pl.BlockSpec(memory_space=pl.ANY)],
        out_specs=[_HBM] * (2 * k_arr), input_output_aliases={i: i for i in range(2 * k_arr)},
        compiler_params=pltpu.CompilerParams(has_side_effects=_EFFECT))(*arrs, handle["send"], handle["recv"], after)
    me = 4 * lax.axis_index("x") + 2 * lax.axis_index("y") + lax.axis_index("c")
    out = []
    for src, land in zip(res[:k_arr], res[k_arr:]):
        own = lax.dynamic_index_in_dim(src, me, 0, keepdims=False) if scatter else src
        out.append(lax.dynamic_update_index_in_dim(land, own, me, 0))
    return out


def _ffn_bwd(dxo, dz, xr, f, u_gc, hmid, gain, sc, w_up, cw, w_down, tag, gate_y=None, gate_g=None):
    d_wdown = _mm_tn((hmid, dz), name=f"ffn_down_dw_{tag}")
    dug, duv, dcw, dcb = _ffn_down_glu_bwd(dz, w_down, u_gc[0], u_gc[1], cw, name=f"ffn_down_glu_bwd_{tag}")
    d_wup_g = _mm_tn((f, dug), name=f"ffn_up_dwg_{tag}")
    d_wup_v = _mm_tn((f, duv), name=f"ffn_up_dwv_{tag}")
    gated = gate_y is not None
    res = _mm_w_ep([dug, duv], w_up, _ep_norm_bwd(gated), [xr, dxo] + ([gate_y] if gated else []),
                   [gain, sc] + ([gate_g] if gated else []), [F32] + ([BF16] if gated else []),
                   [D] * (4 if gated else 3), tb=True, name=f"ffn_up_dx_norm_bwd_{tag}")
    n_out = 2 if gated else 1
    return res[:n_out], res[n_out:], (d_wup_g, d_wup_v, d_wdown, dcw, dcb)


def _split6(mod):
    return [mod[j * D:(j + 1) * D][None, :] for j in range(6)]


def _row(v):
    return v.reshape(1, -1)


def kernel(x, c, ctx, c_ctx, ada_w, ada_b, mix_norm, ffn_norm, even_w_in, even_q_gain, even_k_gain, even_conv_w, even_w_out, odd_pool_w, odd_pool_scale, ffn_w_up, ffn_conv_w, ffn_conv_b, ffn_w_down, loss_target, m_c_ctx, m_ada_w, m_ada_b, m_mix_norm, m_ffn_norm, m_even_w_in, m_even_q_gain, m_even_k_gain, m_even_conv_w, m_even_w_out, m_odd_pool_w, m_odd_pool_scale, m_ffn_w_up, m_ffn_conv_w, m_ffn_conv_b, m_ffn_w_down, v_c_ctx, v_ada_w, v_ada_b, v_mix_norm, v_ffn_norm, v_even_w_in, v_even_q_gain, v_even_k_gain, v_even_conv_w, v_even_w_out, v_odd_pool_w, v_odd_pool_scale, v_ffn_w_up, v_ffn_conv_w, v_ffn_conv_b, v_ffn_w_down):
    n = x.shape[1]
    lc = ctx.shape[1]
    me = 4 * lax.axis_index("x") + 2 * lax.axis_index("y") + lax.axis_index("c")
    xs, ctxs, tgt = x[0], ctx[0], loss_target[0]
    acols = ada_w.shape[2]

    small = jnp.concatenate([even_conv_w.reshape(-1), ffn_conv_w.reshape(-1), odd_pool_scale.reshape(-1)])
    nsmall = small.shape[0]
    small = jnp.pad(small, (0, (-nsmall) % 1024)).reshape(-1, 128)
    c_rows = jnp.pad(c, ((0, 7), (0, 0)))
    g_c, g_win, g_small = _all_gather([c_rows, even_w_in[0].astype(BF16), small], name="gather_first")
    w_in = g_win.transpose(1, 0, 2).reshape(D, -1)
    g_small = g_small.reshape(NDEV, -1)
    ecw = even_conv_w.shape[2]
    fcw = ffn_conv_w.shape[2]
    conv_w = g_small[:, :3 * ecw].reshape(NDEV, 3, ecw).transpose(1, 0, 2).reshape(3, CW)
    o1 = 3 * ecw
    fconv_w = g_small[:, o1:o1 + 6 * fcw].reshape(NDEV, 2, 3, fcw).transpose(1, 2, 0, 3).reshape(2, 3, DFF)
    o2 = o1 + 6 * fcw
    pool_scale = g_small[:, o2:o2 + D // NDEV].reshape(1, D)

    mraw = jnp.concatenate([g_c[:, 0, :], c_ctx[None, :], jnp.zeros((7, D), F32)], axis=0)
    my_bias = lax.dynamic_slice_in_dim(ada_b, me * acols, acols, axis=1)
    modp = jnp.stack([_mm(mraw, ada_w[l], silu_a=True, bias=my_bias[l:l + 1], name=f"ada_proj_{l}", tm=16, tn=256)
                      for l in range(2)])
    (g_mod,) = _all_gather([modp], name="gather_mod")
    mod_rows = g_mod.transpose(1, 2, 0, 3).reshape(2, 16, 6 * D)
    late_shards = [even_w_out[0].astype(BF16), odd_pool_w[0].astype(BF16), ffn_w_up.astype(BF16),
                   ffn_w_down.astype(BF16)]
    late_shards, mod_rows = lax.optimization_barrier((late_shards, mod_rows))
    h_weights = _exchange_start(late_shards, scatter=False, name="weights_start")
    mod_rows = mod_rows + h_weights["token"][0, 0]
    mod = lax.dynamic_index_in_dim(mod_rows, me, axis=1, keepdims=False)
    sh1, sc1, g1, sh2, sc2, g2 = _split6(mod[0])
    sh1b, sc1b, g1b, sh2b, sc2b, g2b = _split6(mod[1])
    csh1, csc1 = _split6(mod_rows[0, 8])[:2]
    mixn = [_row(mix_norm[l]) for l in range(2)]
    ffnn = [_row(ffn_norm[l]) for l in range(2)]
    qg, kg = _row(even_q_gain[0]), _row(even_k_gain[0])
    fcb = [_row(ffn_conv_b[l]) for l in range(2)]

    cs_t, sn_t = _rope_tables(n)
    a_lat = _norm_mod(xs, mixn[0], sc1, sh1, name="mix0_norm")
    a_ctx = _norm_mod(ctxs, mixn[0], csc1, csh1, name="mix0_norm_ctx")
    p_lat = _mm_w(a_lat, w_in, name="in_proj")
    p_ctx = _mm(a_ctx, w_in[:, AW:AW + 4 * HD], name="in_proj_ctx", tm=256, tn=512, tk=1024)
    kv_ctx = _qkv_prep(p_ctx, qg, kg, None, None, has_q=False, kv_col=0, kv_rows=lc + n, name="qkv_prep_ctx")
    q_r, k_all, v_all = _qkv_prep(p_lat, qg, kg, cs_t, sn_t, has_q=True, kv_col=1, kv_rows=lc + n, kv_row_off=lc,
                                  kv_into=kv_ctx, name="qkv_prep")
    o_attn, lse = _attn_fwd(q_r, k_all, v_all, name="attn_fwd")
    cat = _conv_gate_fwd(p_lat, o_attn, conv_w, name="conv_gate")
    g_wout, g_pool, g_up, g_down = _exchange_wait(h_weights, cat, name="weights_wait")
    w_out = g_wout.reshape(D, D)
    pool_w = g_pool.transpose(1, 0, 2, 3).reshape(4, PG, PG)
    w_up = [g_up[:, l].transpose(1, 0, 2).reshape(D, 2 * DFF) for l in range(2)]
    w_down = [g_down[:, l].reshape(DFF, D) for l in range(2)]
    y0, x1, f0 = _mm_w_ep(cat, w_out, _ep_resid_norm, [xs], [g1, ffnn[0], sc2, sh2], [F32, F32, BF16], [],
                          name="out_proj_norm")[:3]
    *u0, h0 = _ffn_up_glu(f0, w_up[0], fconv_w[0], fcb[0], name="ffn_up_glu_l0")
    z0 = _mm_w(h0, w_down[0], name="ffn_down_l0")

    x2, mixed = _pool_fwd(x1, z0, g2, mixn[1], sc1b, sh1b, pool_w, name="pool_fwd")
    x3, f1 = _norm_mod(x2, ffnn[1], sc2b, sh2b, y=mixed, g=g1b, ymul=pool_scale, name="ffn_norm_l1")
    *u1, h1 = _ffn_up_glu(f1, w_up[1], fconv_w[1], fcb[1], name="ffn_up_glu_l1")
    dx4, dz1, loss_part, dg2b = _mm_w_ep(h1, w_down[1], _ep_loss(D), [x3, tgt], [g2b], [F32, BF16], [128, D],
                                         name="ffn_down_loss")
    loss = lax.psum(loss_part[0, 0], ("x", "y", "c"))

    (dx3,), (dsh2b, dsc2b, dffn1), (dup1g, dup1v, ddown1, dfcw1, dfcb1) = _ffn_bwd(
        dx4, dz1, x3, f1, u1, h1, ffnn[1], sc2b, w_up[1], fconv_w[1], w_down[1], "l1")
    dx2, dpool_w, dg1b, dpscale, dsh1b, dsc1b, dmix1, dz0, dg2 = _pool_bwd(
        dx3, mixed, x2, g1b, pool_scale, mixn[1], sc1b, sh1b, pool_w, z0, g2, name="pool_bwd")

    def up_shards(dg, dv):
        return jnp.concatenate([dg, dv], axis=1).reshape(D, NDEV, -1).transpose(1, 0, 2)

    s_pool = dpool_w.astype(BF16).reshape(4, NDEV, PG // NDEV, PG).transpose(1, 0, 2, 3)
    h_g1 = _exchange_start([s_pool, up_shards(dup1g, dup1v), ddown1.reshape(NDEV, DFF // NDEV, D)], scatter=True,
                           name="grads1_start")

    (dx1, dy0), (dsh2, dsc2, dffn0, dg1), (dup0g, dup0v, ddown0, dfcw0, dfcb0) = _ffn_bwd(
        dx2, dz0, x1, f0, u0, h0, ffnn[0], sc2, w_up[0], fconv_w[0] + h_g1["token"][0, 0], w_down[0], "l0",
        gate_y=y0, gate_g=g1)
    h_g0 = _exchange_start([up_shards(dup0g, dup0v), ddown0.reshape(NDEV, DFF // NDEV, D)], scatter=True,
                           name="grads0_start")
    dcat = _mm_w(dy0, w_out, tb=True, name="out_proj_dx", tm=512)
    d_wout = _mm_tn((cat, dy0), name="out_proj_dw")
    dp_conv, dconv_w = _conv_gate_bwd(dcat, p_lat, conv_w + h_g0["token"][0, 0], name="conv_gate_bwd")
    dq_r, dk_all, dv_all = _attn_bwd(q_r, k_all, v_all, dcat, o_attn, lse, name="attn_bwd")
    dp_qkv, dqg_l, dkg_l = _qkv_bwd(p_lat, dq_r, dk_all, dv_all, qg, kg, cs_t, sn_t, has_q=True, kv_col=1,
                                    kv_row_off=lc, name="qkv_bwd")
    dp_ctx, _zero_qg, dkg_c = _qkv_bwd(p_ctx, None, dk_all, dv_all, qg, kg, None, None, has_q=False, kv_col=0,
                                       kv_row_off=0, name="qkv_bwd_ctx")
    da_ctx = _mm(dp_ctx, w_in[:, :D], tb=True, name="in_proj_dx_ctx", tm=256, tn=512, tk=1024)
    d_win_qkv = _mm_tn([(a_lat, dp_qkv), (a_ctx, dp_ctx)], name="in_proj_dw_qkv")
    d_win_conv = _mm_tn((a_lat, dp_conv), name="in_proj_dw_conv")
    d_win = jnp.concatenate([d_win_qkv, d_win_conv], axis=1)
    grad_x, dsh1, dsc1, dmix0 = _mm_w_ep([dp_qkv, dp_conv], w_in, _ep_norm_bwd(False), [xs, dx1], [mixn[0], sc1],
                                         [F32], [D] * 3, tb=True, name="in_proj_dx_norm_bwd")
    _dctx, dcsh1, dcsc1, dmix0c = _norm_mod_bwd(da_ctx, ctxs, mixn[0], csc1, name="mix0_norm_bwd_ctx")

    z1k = jnp.zeros((1, D), F32)
    pack = jnp.concatenate(
        [v.reshape(-1) for v in (dsh1, dsc1, dg1, dsh2, dsc2, dg2, dsh1b, dsc1b, dg1b, dsh2b, dsc2b, dg2b,
                                 dcsh1, dcsc1, z1k, z1k, z1k, z1k,
                                 dmix0, dmix1, dmix0c, z1k, dffn0, dffn1, dqg_l, dkg_l + dkg_c,
                                 dfcb0, dfcb1, dconv_w, dfcw0, dfcw1, dpscale)])
    npack = pack.shape[0]
    pack = jnp.pad(pack, (0, (-npack) % 1024)).reshape(-1, 128)
    (g_pack,) = _all_gather([pack], name="gather_small_grads")
    gp = g_pack.reshape(NDEV, -1)
    off = [0]

    def take(size):
        seg = gp[:, off[0]:off[0] + size]
        off[0] += size
        return seg

    dmod_all = take(12 * D).reshape(NDEV, 2, 6 * D)
    dmodc_all = take(6 * D).reshape(NDEV, 1, 6 * D)
    dmix_all = take(4 * D).reshape(NDEV, 2, 2, D)
    dffn_all = take(2 * D).reshape(NDEV, 2, D)
    dqg_all = take(HD).reshape(NDEV, 1, HD)
    dkg_all = take(HD).reshape(NDEV, 1, HD)
    dfcb_all = take(2 * DFF).reshape(NDEV, 2, DFF)
    dconvw_all = take(3 * CW).reshape(NDEV, 3, CW)
    dfcw_all = take(6 * DFF).reshape(NDEV, 2, 3, DFF)
    dpscale_all = take(D).reshape(NDEV, D)

    dmodc_sum = dmodc_all[0]
    for dev in range(1, NDEV):
        dmodc_sum = dmodc_sum + dmodc_all[dev]
    my_cols = lambda a: lax.dynamic_slice_in_dim(a, me * acols, acols, axis=a.ndim - 1)
    rows0 = jnp.concatenate([my_cols(dmod_all[:, 0]), my_cols(dmodc_sum), jnp.zeros((7, acols), F32)], axis=0)
    rows1 = jnp.concatenate([my_cols(dmod_all[:, 1]), jnp.zeros((8, acols), F32)], axis=0)
    d_ada = jnp.stack([_mm(mraw, rows, ta=True, silu_a=True, name=f"ada_dw_{l}", tm=512, tn=256, tk=16)
                       for l, rows in enumerate((rows0, rows1))])
    dscc_part = _mm(rows0, ada_w[0], tb=True, name="ada_dcctx", tm=16, tn=512, tk=256)
    (g_dscc,) = _all_gather([dscc_part[8:16]], name="gather_dcctx")

    attn_shards = [d_win.reshape(D, NDEV, -1).transpose(1, 0, 2), d_wout.reshape(NDEV, D // NDEV, D)]
    attn_shards, g_dscc = lax.optimization_barrier((attn_shards, g_dscc))
    h_ga = _exchange_start(attn_shards, scatter=True, name="grads_attn_start")
    dmod_all = dmod_all + h_ga["token"][0, 0]

    outs = {}

    def put(nm, res):
        outs["grad_" + nm], outs["delta_" + nm], outs["new_m_" + nm], outs["new_v_" + nm] = res

    dmodc_pad = jnp.concatenate([dmodc_all, jnp.zeros_like(dmodc_all)], axis=1)
    put("ada_b", _adamw_nd(jnp.concatenate([dmod_all, dmodc_pad], axis=0), ada_b, m_ada_b, v_ada_b, name="adam_ada_b"))
    put("mix_norm", _adamw_nd(jnp.concatenate([dmix_all[:, 0], dmix_all[:, 1]], axis=0), mix_norm, m_mix_norm,
                              v_mix_norm, name="adam_mix_norm"))
    put("ffn_norm", _adamw_nd(dffn_all, ffn_norm, m_ffn_norm, v_ffn_norm, name="adam_ffn_norm"))
    put("even_q_gain", _adamw_nd(dqg_all, even_q_gain, m_even_q_gain, v_even_q_gain, name="adam_q_gain"))
    put("even_k_gain", _adamw_nd(dkg_all, even_k_gain, m_even_k_gain, v_even_k_gain, name="adam_k_gain"))
    put("ffn_conv_b", _adamw_nd(dfcb_all, ffn_conv_b, m_ffn_conv_b, v_ffn_conv_b, name="adam_ffn_conv_b"))
    my_convw = lax.dynamic_slice_in_dim(dconvw_all, me * ecw, ecw, axis=2)[:, None]
    put("even_conv_w", _adamw_nd(my_convw, even_conv_w, m_even_conv_w, v_even_conv_w, name="adam_even_conv_w"))
    my_fcw = lax.dynamic_slice_in_dim(dfcw_all, me * fcw, fcw, axis=3)
    put("ffn_conv_w", _adamw_nd(my_fcw, ffn_conv_w, m_ffn_conv_w, v_ffn_conv_w, name="adam_ffn_conv_w"))
    my_ps = lax.dynamic_slice_in_dim(dpscale_all, me * (D // NDEV), D // NDEV, axis=1)[:, None]
    put("odd_pool_scale", _adamw_nd(my_ps, odd_pool_scale, m_odd_pool_scale, v_odd_pool_scale, name="adam_pool_scale"))

    put("ada_w", _adamw_nd(d_ada[None], ada_w, m_ada_w, v_ada_w, name="adam_ada_w"))
    put("c_ctx", _adamw_nd(g_dscc[:, 0:1, :].reshape(NDEV, D), c_ctx, m_c_ctx, v_c_ctx, name="adam_c_ctx",
                           silu_grad_of=c_ctx))

    r_pool, r_up1, r_down1 = _exchange_wait(h_g1, outs["grad_ada_b"], name="grads1_wait")
    r_up0, r_down0 = _exchange_wait(h_g0, outs["grad_mix_norm"], name="grads0_wait")
    r_win, r_wout = _exchange_wait(h_ga, outs["grad_c_ctx"], name="grads_attn_wait")
    put("even_w_in", _adamw_nd(r_win[:, None], even_w_in, m_even_w_in, v_even_w_in, name="adam_w_in"))
    put("even_w_out", _adamw_nd(r_wout[:, None], even_w_out, m_even_w_out, v_even_w_out, name="adam_w_out"))
    put("odd_pool_w", _adamw_nd(r_pool[:, None], odd_pool_w, m_odd_pool_w, v_odd_pool_w, name="adam_pool_w"))
    put("ffn_w_up", _adamw_nd([r_up0, r_up1], ffn_w_up, m_ffn_w_up, v_ffn_w_up, name="adam_w_up"))
    put("ffn_w_down", _adamw_nd([r_down0, r_down1], ffn_w_down, m_ffn_w_down, v_ffn_w_down, name="adam_w_down"))

    names = ["c_ctx", "ada_w", "ada_b", "mix_norm", "ffn_norm", "even_w_in", "even_q_gain", "even_k_gain",
             "even_conv_w", "even_w_out", "odd_pool_w", "odd_pool_scale", "ffn_w_up", "ffn_conv_w", "ffn_conv_b",
             "ffn_w_down"]
    result = [loss, grad_x[None]]
    for kind in ("grad_", "delta_", "new_m_", "new_v_"):
        result += [outs[kind + nm] for nm in names]
    return tuple(result)
```

```python
import functools
import math

import jax
import jax.numpy as jnp
from jax import lax
from jax.experimental import pallas as pl
from jax.experimental.pallas import tpu as pltpu

F32 = jnp.float32
BF16 = jnp.bfloat16

D = 1024
HD = 128
NQ = 4
NKV = 2
AW = NQ * HD
CW = D - AW
DFF = 2816
GRID_W = 64
ROPE_THETA = 10000.0
POOL_WINDOWS = (2, 4, 8, 16)
PG = D // 4
EPS = 1e-6
NDEV = 8
HALO = 8
MESH = pl.DeviceIdType.MESH

ADAM_LR = 0.001
ADAM_B1 = 0.9
ADAM_B2 = 0.999
ADAM_EPS = 1e-08
ADAM_WD = 0.01
ADAM_STEP = 10


def _pick(dim, prefs):
    for p in prefs:
        if dim % p == 0:
            return p
    return dim


def _params(*sem):
    return pltpu.CompilerParams(dimension_semantics=sem)


_NT = (((1,), (1,)), ((), ()))
_TN = (((0,), (0,)), ((), ()))
_SCALE = HD ** -0.5
_QSCALE = _SCALE * math.log2(math.e)
_LN2 = math.log(2.0)


def _mm(a_list, b, *, name, ta=False, tb=False, out_dtype=F32, silu_a=False, bias=None, tm=None, tn=None, tk=None):
    if not isinstance(a_list, (list, tuple)):
        a_list = [a_list]
    na = len(a_list)
    assert not (ta and na > 1)
    if ta:
        kdim, m = a_list[0].shape
        ks = [kdim]
    else:
        m = a_list[0].shape[0]
        ks = [a.shape[1] for a in a_list]
        kdim = sum(ks)
    n = b.shape[0] if tb else b.shape[1]
    assert (b.shape[1] if tb else b.shape[0]) == kdim
    kunit = math.gcd(*ks) if na > 1 else kdim
    tm = min(tm, m) if tm else _pick(m, (512, 256, 128, 64, 32, 16, 8))
    tn = min(tn, n) if tn else _pick(n, (512, 256, 128))
    tk = min(tk, kunit) if tk else _pick(kunit, (1024, 768, 512, 256, 128))
    assert m % tm == 0 and n % tn == 0 and all(k % tk == 0 for k in ks)
    nks = [k // tk for k in ks]
    starts = [sum(nks[:i]) for i in range(na)]
    nk = sum(nks)
    has_bias = bias is not None

    def body(*refs):
        a_refs = refs[:na]
        b_ref = refs[na]
        bias_ref = refs[na + 1] if has_bias else None
        o_ref = refs[na + 1 + has_bias]
        acc = refs[-1]
        k = pl.program_id(2)

        @pl.when(k == 0)
        def _():
            acc[...] = jnp.zeros_like(acc)

        bv = b_ref[...].astype(BF16)
        dn = (((0 if ta else 1,), (1 if tb else 0,)), ((), ()))
        for idx in range(na):
            def step(idx=idx):
                av = a_refs[idx][...]
                if silu_a:
                    av = av * jax.nn.sigmoid(av)
                acc[...] += lax.dot_general(av.astype(BF16), bv, dn, preferred_element_type=F32)
            if na == 1:
                step()
            else:
                pl.when((k >= starts[idx]) & (k < starts[idx] + nks[idx]))(step)

        @pl.when(k == nk - 1)
        def _():
            r = acc[...]
            if has_bias:
                r = r + bias_ref[...]
            o_ref[...] = r.astype(o_ref.dtype)

    in_specs = []
    for idx in range(na):
        if ta:
            in_specs.append(pl.BlockSpec((tk, tm), lambda i, j, k: (k, i)))
        else:
            lo, cnt = starts[idx], nks[idx]
            in_specs.append(pl.BlockSpec((tm, tk), lambda i, j, k, lo=lo, cnt=cnt: (i, jnp.clip(k - lo, 0, cnt - 1))))
    if tb:
        in_specs.append(pl.BlockSpec((tn, tk), lambda i, j, k: (j, k)))
    else:
        in_specs.append(pl.BlockSpec((tk, tn), lambda i, j, k: (k, j)))
    args = list(a_list) + [b]
    if has_bias:
        in_specs.append(pl.BlockSpec((1, tn), lambda i, j, k: (0, j)))
        args.append(bias)
    return pl.pallas_call(
        body, grid=(m // tm, n // tn, nk), in_specs=in_specs,
        out_specs=pl.BlockSpec((tm, tn), lambda i, j, k: (i, j)),
        out_shape=jax.ShapeDtypeStruct((m, n), out_dtype),
        scratch_shapes=[pltpu.VMEM((tm, tn), F32)], name=name,
        compiler_params=_params("parallel", "parallel", "arbitrary"))(*args)


def _mm_w(a_list, w, *, name, tb=False, tm=256, out_dtype=F32):
    if not isinstance(a_list, (list, tuple)):
        a_list = [a_list]
    na = len(a_list)
    m = a_list[0].shape[0]
    ks = [a.shape[1] for a in a_list]
    offs = [sum(ks[:i]) for i in range(na)]
    n = w.shape[0] if tb else w.shape[1]
    assert (w.shape[1] if tb else w.shape[0]) == sum(ks)
    tm = min(tm, m)
    assert m % tm == 0

    def body(*refs):
        a_refs, w_ref, o_ref = refs[:na], refs[na], refs[na + 1]
        acc = None
        for idx in range(na):
            av = a_refs[idx][...].astype(BF16)
            if tb:
                part = lax.dot_general(av, w_ref[:, offs[idx]:offs[idx] + ks[idx]], _NT, preferred_element_type=F32)
            else:
                part = jnp.dot(av, w_ref[offs[idx]:offs[idx] + ks[idx], :], preferred_element_type=F32)
            acc = part if acc is None else acc + part
        o_ref[...] = acc.astype(o_ref.dtype)

    in_specs = [pl.BlockSpec((tm, k), lambda i: (i, 0)) for k in ks] + [pl.BlockSpec(w.shape, lambda i: (0, 0))]
    return pl.pallas_call(
        body, grid=(m // tm,), in_specs=in_specs, out_specs=pl.BlockSpec((tm, n), lambda i: (i, 0)),
        out_shape=jax.ShapeDtypeStruct((m, n), out_dtype), name=name, compiler_params=_params("parallel"))(*a_list, w)


def _mm_w_ep(a_list, w, epilogue, row_in, vec_in, out_dtypes, sum_widths, *, name, tb=False, tm=256, sub=256):
    if not isinstance(a_list, (list, tuple)):
        a_list = [a_list]
    na, nr, nv, no, ns = len(a_list), len(row_in), len(vec_in), len(out_dtypes), len(sum_widths)
    m = a_list[0].shape[0]
    ks = [a.shape[1] for a in a_list]
    offs = [sum(ks[:i]) for i in range(na)]
    n = w.shape[0] if tb else w.shape[1]
    assert (w.shape[1] if tb else w.shape[0]) == sum(ks)
    tm = min(tm, m)
    sub = min(sub, tm)
    assert m % tm == 0 and tm % sub == 0

    def body(*refs):
        a_refs, w_ref = refs[:na], refs[na]
        row_refs = refs[na + 1:na + 1 + nr]
        vec_refs = refs[na + 1 + nr:na + 1 + nr + nv]
        out_refs = refs[na + 1 + nr + nv:na + 1 + nr + nv + no]
        sum_refs = refs[na + 1 + nr + nv + no:]

        @pl.when(pl.program_id(0) == 0)
        def _():
            for s_ref in sum_refs:
                s_ref[...] = jnp.zeros_like(s_ref)

        vecs = [v[...] for v in vec_refs]
        for r0 in range(0, tm, sub):
            acc = None
            for idx in range(na):
                av = a_refs[idx][r0:r0 + sub, :].astype(BF16)
                if tb:
                    part = lax.dot_general(av, w_ref[:, offs[idx]:offs[idx] + ks[idx]], _NT, preferred_element_type=F32)
                else:
                    part = jnp.dot(av, w_ref[offs[idx]:offs[idx] + ks[idx], :], preferred_element_type=F32)
                acc = part if acc is None else acc + part
            outs, sums = epilogue(acc, [r[r0:r0 + sub, :] for r in row_refs], vecs)
            for o_ref, o in zip(out_refs, outs):
                o_ref[r0:r0 + sub, :] = o.astype(o_ref.dtype)
            for s_ref, s in zip(sum_refs, sums):
                s_ref[...] += s

    row = pl.BlockSpec((tm, n), lambda i: (i, 0))
    in_specs = ([pl.BlockSpec((tm, k), lambda i: (i, 0)) for k in ks] + [pl.BlockSpec(w.shape, lambda i: (0, 0))]
                + [row] * nr + [_vec(n)] * nv)
    return pl.pallas_call(
        body, grid=(m // tm,), in_specs=in_specs, out_specs=[row] * no + [_vec(sw) for sw in sum_widths],
        out_shape=[jax.ShapeDtypeStruct((m, n), dt) for dt in out_dtypes]
        + [jax.ShapeDtypeStruct((1, sw), F32) for sw in sum_widths],
        name=name, compiler_params=_params("arbitrary" if ns else "parallel"))(*a_list, w, *row_in, *vec_in)


def _ep_norm_bwd(has_gate):
    def ep(dav, rows, vecs):
        xv = rows[0]
        gain, scv = vecs[0], vecs[1]
        r = lax.rsqrt(jnp.mean(xv * xv, axis=-1, keepdims=True) + EPS)
        xh = xv * r
        nrm = xh * gain
        dn = dav * (1.0 + scv)
        dxh = dn * gain
        dx = r * (dxh - xh * jnp.mean(dxh * xh, axis=-1, keepdims=True)) + rows[1]
        outs, sums = [dx], [_colsum(dav), _colsum(dav * nrm), _colsum(dn * xh)]
        if has_gate:
            outs.append(dx * vecs[2])
            sums.append(_colsum(dx * rows[2]))
        return outs, sums
    return ep


def _ep_loss(d):
    def ep(zv, rows, vecs):
        xv, tv = rows
        gv = vecs[0]
        diff = (xv + gv * zv) - tv
        dx = diff * (1.0 / d)
        part = 0.5 * jnp.sum(jnp.mean(diff * diff, axis=-1, keepdims=True), axis=0, keepdims=True)
        return [dx, dx * gv], [jnp.broadcast_to(part, (1, 128)), _colsum(dx * zv)]
    return ep


def _ep_resid_norm(yv, rows, vecs):
    g, gain, scv, shv = vecs
    xv = rows[0] + g * yv
    r = lax.rsqrt(jnp.mean(xv * xv, axis=-1, keepdims=True) + EPS)
    return [yv, xv, ((xv * r) * gain) * (1.0 + scv) + shv], []


def _mm_tn(pairs, *, name, tk=1024, out_dtype=BF16):
    if not isinstance(pairs, list):
        pairs = [pairs]
    m, n = pairs[0][0].shape[1], pairs[0][1].shape[1]
    tks = [min(tk, a.shape[0]) for a, _ in pairs]
    nks = [a.shape[0] // t for (a, _), t in zip(pairs, tks)]
    assert all(a.shape[0] == b.shape[0] and a.shape[0] % t == 0 for (a, b), t in zip(pairs, tks))
    starts = [sum(nks[:i]) for i in range(len(pairs))]
    nk = sum(nks)

    def body(*refs):
        o_ref, acc = refs[-2], refs[-1]
        k = pl.program_id(0)

        @pl.when(k == 0)
        def _():
            acc[...] = jnp.zeros_like(acc)

        for idx in range(len(pairs)):
            a_ref, b_ref = refs[2 * idx], refs[2 * idx + 1]

            def step(a_ref=a_ref, b_ref=b_ref):
                acc[...] += lax.dot_general(a_ref[...], b_ref[...], _TN, preferred_element_type=F32)

            if len(pairs) == 1:
                step()
            else:
                pl.when((k >= starts[idx]) & (k < starts[idx] + nks[idx]))(step)

        @pl.when(k == nk - 1)
        def _():
            o_ref[...] = acc[...].astype(o_ref.dtype)

    in_specs, args = [], []
    for (a, b), t, lo, cnt in zip(pairs, tks, starts, nks):
        idx_map = lambda k, lo=lo, cnt=cnt: (jnp.clip(k - lo, 0, cnt - 1), 0)
        in_specs += [pl.BlockSpec((t, m), idx_map), pl.BlockSpec((t, n), idx_map)]
        args += [a, b]
    return pl.pallas_call(
        body, grid=(nk,), in_specs=in_specs, out_specs=pl.BlockSpec((m, n), lambda k: (0, 0)),
        out_shape=jax.ShapeDtypeStruct((m, n), out_dtype), scratch_shapes=[pltpu.VMEM((m, n), F32)], name=name,
        compiler_params=_params("arbitrary"))(*args)


def _vec(d, col=None):
    if col is None:
        return pl.BlockSpec((1, d), lambda i, *_: (0, 0))
    return pl.BlockSpec((1, d), col)


def _halo_specs(tm, width, nrows, colblk=0, row_off=0):
    r = tm // HALO
    off = row_off // HALO
    last = nrows // HALO - 1
    prev = pl.BlockSpec((HALO, width), lambda i, *_: (off + jnp.maximum(i * r - 1, 0), colblk))
    nxt = pl.BlockSpec((HALO, width), lambda i, *_: (off + jnp.minimum((i + 1) * r, last), colblk))
    return prev, nxt


def _ext(prev_ref, main_ref, next_ref, i, ni):
    p = jnp.where(i > 0, prev_ref[...], 0.0)
    n = jnp.where(i < ni - 1, next_ref[...], 0.0)
    return jnp.concatenate([p, main_ref[...], n], axis=0)


def _sh(ext, k, tm):
    if k == 0:
        return ext[HALO:HALO + tm]
    rows = ext.shape[0]
    return pltpu.roll(ext, (-k) % rows, axis=0)[HALO:HALO + tm]


def _roll_rows(v, k):
    rows = v.shape[0]
    return pltpu.roll(v, (-k) % rows, axis=0) if k % rows else v


def _conv3(ext, w_ref, tm):
    return _sh(ext, -1, tm) * w_ref[0:1, :] + _sh(ext, 0, tm) * w_ref[1:2, :] + _sh(ext, 1, tm) * w_ref[2:3, :]


def _colsum(v):
    return jnp.sum(v, axis=0, keepdims=True)


def _acc_out(ref, i, val):
    @pl.when(i == 0)
    def _():
        ref[...] = jnp.zeros_like(ref)

    ref[...] += val


def _sigmoid(v):
    return jax.nn.sigmoid(v)


def _norm_mod(x, gain, sc, sh, *, name, y=None, g=None, ymul=None, tm=512):
    n, d = x.shape
    tm = min(tm, n)
    has_res = y is not None
    has_mul = ymul is not None

    def body(*refs):
        it = iter(refs)
        x_ref = next(it)
        y_ref = next(it) if has_res else None
        g_ref = next(it) if has_res else None
        m_ref = next(it) if has_mul else None
        gain_ref, sc_ref, sh_ref = next(it), next(it), next(it)
        xo_ref = next(it) if has_res else None
        a_ref = next(it)
        xv = x_ref[...]
        if has_res:
            yv = y_ref[...]
            if has_mul:
                yv = yv * m_ref[...]
            xv = xv + g_ref[...] * yv
            xo_ref[...] = xv
        r = lax.rsqrt(jnp.mean(xv * xv, axis=-1, keepdims=True) + EPS)
        nrm = (xv * r) * gain_ref[...]
        a_ref[...] = (nrm * (1.0 + sc_ref[...]) + sh_ref[...]).astype(BF16)

    row = pl.BlockSpec((tm, d), lambda i: (i, 0))
    in_specs, args = [row], [x]
    if has_res:
        in_specs += [row, _vec(d)]
        args += [y, g]
    if has_mul:
        in_specs.append(_vec(d))
        args.append(ymul)
    in_specs += [_vec(d)] * 3
    args += [gain, sc, sh]
    out_specs, out_shape = [], []
    if has_res:
        out_specs.append(row)
        out_shape.append(jax.ShapeDtypeStruct((n, d), F32))
    out_specs.append(row)
    out_shape.append(jax.ShapeDtypeStruct((n, d), BF16))
    res = pl.pallas_call(body, grid=(n // tm,), in_specs=in_specs, out_specs=out_specs, out_shape=out_shape,
                         name=name, compiler_params=_params("parallel"))(*args)
    return res if has_res else res[0]


def _norm_mod_bwd(da, x, gain, sc, *, name, dres=None, gate_y=None, gate_g=None, tm=512):
    n, d = x.shape
    tm = min(tm, n)
    has_res = dres is not None
    has_gate = gate_y is not None

    def body(*refs):
        it = iter(refs)
        da_ref, x_ref = next(it), next(it)
        r_ref = next(it) if has_res else None
        y_ref = next(it) if has_gate else None
        g_ref = next(it) if has_gate else None
        gain_ref, sc_ref = next(it), next(it)
        dx_ref, dsh_ref, dsc_ref, dgn_ref = next(it), next(it), next(it), next(it)
        dy_ref = next(it) if has_gate else None
        dg_ref = next(it) if has_gate else None
        i = pl.program_id(0)
        xv = x_ref[...]
        dav = da_ref[...]
        r = lax.rsqrt(jnp.mean(xv * xv, axis=-1, keepdims=True) + EPS)
        xh = xv * r
        nrm = xh * gain_ref[...]
        dn = dav * (1.0 + sc_ref[...])
        dxh = dn * gain_ref[...]
        dx = r * (dxh - xh * jnp.mean(dxh * xh, axis=-1, keepdims=True))
        if has_res:
            dx = dx + r_ref[...]
        dx_ref[...] = dx
        _acc_out(dsh_ref, i, _colsum(dav))
        _acc_out(dsc_ref, i, _colsum(dav * nrm))
        _acc_out(dgn_ref, i, _colsum(dn * xh))
        if has_gate:
            dy_ref[...] = (dx * g_ref[...]).astype(BF16)
            _acc_out(dg_ref, i, _colsum(dx * y_ref[...]))

    row = pl.BlockSpec((tm, d), lambda i: (i, 0))
    in_specs, args = [row, row], [da, x]
    if has_res:
        in_specs.append(row)
        args.append(dres)
    if has_gate:
        in_specs += [row, _vec(d)]
        args += [gate_y, gate_g]
    in_specs += [_vec(d)] * 2
    args += [gain, sc]
    vec_shape = jax.ShapeDtypeStruct((1, d), F32)
    out_specs = [row, _vec(d), _vec(d), _vec(d)]
    out_shape = [jax.ShapeDtypeStruct((n, d), F32), vec_shape, vec_shape, vec_shape]
    if has_gate:
        out_specs += [row, _vec(d)]
        out_shape += [jax.ShapeDtypeStruct((n, d), BF16), vec_shape]
    return pl.pallas_call(
        body, grid=(n // tm,), in_specs=in_specs, out_specs=out_specs, out_shape=out_shape,
        name=name, compiler_params=_params("arbitrary"))(*args)


def _ffn_up_glu(f, w_up, cw, cb, *, name, tm=256, tc=256):
    n, d = f.shape
    tm = min(tm, n)
    ni = n // tm
    nc = DFF // tc
    halo = 16
    rows = tm + 2 * halo
    r = tm // halo
    last = n // halo - 1

    def body(f_ref, fp_ref, fn_ref, w_ref, cw_ref, cb_ref, u_ref, gc_ref, h_ref):
        i = pl.program_id(0)
        a = f_ref[...]
        aext = jnp.concatenate([jnp.where(i > 0, fp_ref[...], jnp.zeros_like(fp_ref[...])), a,
                                jnp.where(i < ni - 1, fn_ref[...], jnp.zeros_like(fn_ref[...]))], axis=0)
        for j in range(nc):
            cols = slice(j * tc, (j + 1) * tc)
            vcols = slice(DFF + j * tc, DFF + (j + 1) * tc)
            gext = jnp.dot(aext, w_ref[:, cols], preferred_element_type=F32)
            val = jnp.dot(a, w_ref[:, vcols], preferred_element_type=F32)
            gate = gext[halo:halo + tm]
            gc = (pltpu.roll(gext, 1, axis=0)[halo:halo + tm] * cw_ref[0:1, cols] + gate * cw_ref[1:2, cols]
                  + pltpu.roll(gext, rows - 1, axis=0)[halo:halo + tm] * cw_ref[2:3, cols]) + cb_ref[:, cols]
            u_ref[:, cols] = gate
            u_ref[:, vcols] = val
            gc_ref[:, cols] = gc
            h_ref[:, cols] = (gc * _sigmoid(gc) * val).astype(BF16)

    return pl.pallas_call(
        body, grid=(ni,),
        in_specs=[pl.BlockSpec((tm, d), lambda i: (i, 0)),
                  pl.BlockSpec((halo, d), lambda i: (jnp.maximum(i * r - 1, 0), 0)),
                  pl.BlockSpec((halo, d), lambda i: (jnp.minimum((i + 1) * r, last), 0)),
                  pl.BlockSpec(w_up.shape, lambda i: (0, 0)), pl.BlockSpec((3, DFF), lambda i: (0, 0)),
                  pl.BlockSpec((1, DFF), lambda i: (0, 0))],
        out_specs=[pl.BlockSpec((tm, 2 * DFF), lambda i: (i, 0)), pl.BlockSpec((tm, DFF), lambda i: (i, 0)),
                   pl.BlockSpec((tm, DFF), lambda i: (i, 0))],
        out_shape=[jax.ShapeDtypeStruct((n, 2 * DFF), F32), jax.ShapeDtypeStruct((n, DFF), F32),
                   jax.ShapeDtypeStruct((n, DFF), BF16)], name=name,
        compiler_params=_params("parallel"))(f, f, f, w_up, cw, cb)


def _ffn_down_glu_bwd(dz, w_down, u, gc, cw, *, name, tm=256, tc=256):
    n, d = dz.shape
    tm = min(tm, n)
    ni = n // tm
    nc = DFF // tc
    rows = tm + 2 * HALO

    def body(z_ref, zp_ref, zn_ref, w_ref, u_ref, vp_ref, vn_ref, c_ref, cp_ref, cn_ref, cw_ref,
             dg_ref, dv_ref, dcw_ref, dcb_ref):
        i = pl.program_id(0)

        @pl.when(i == 0)
        def _():
            dcw_ref[...] = jnp.zeros_like(dcw_ref)
            dcb_ref[...] = jnp.zeros_like(dcb_ref)

        zext = jnp.concatenate([jnp.where(i > 0, zp_ref[...], jnp.zeros_like(zp_ref[...])), z_ref[...],
                                jnp.where(i < ni - 1, zn_ref[...], jnp.zeros_like(zn_ref[...]))], axis=0)
        for j in range(nc):
            cols = slice(j * tc, (j + 1) * tc)
            vcols = slice(DFF + j * tc, DFF + (j + 1) * tc)
            dh = lax.dot_general(zext, w_ref[cols, :], _NT, preferred_element_type=F32)[HALO:HALO + rows]
            gcx = jnp.concatenate([cp_ref[:, cols], c_ref[:, cols], cn_ref[:, cols]], axis=0)
            vext = jnp.concatenate([vp_ref[:, cols], u_ref[:, vcols], vn_ref[:, cols]], axis=0)
            sg = _sigmoid(gcx)
            dgc = dh * vext * (sg * (1.0 + gcx * (1.0 - sg)))
            dv_ref[:, cols] = (dh[HALO:HALO + tm] * (gcx[HALO:HALO + tm] * sg[HALO:HALO + tm])).astype(BF16)
            d_next = pltpu.roll(dgc, rows - 1, axis=0)[HALO:HALO + tm]
            d_prev = pltpu.roll(dgc, 1, axis=0)[HALO:HALO + tm]
            d_here = dgc[HALO:HALO + tm]
            dg_ref[:, cols] = (d_next * cw_ref[0:1, cols] + d_here * cw_ref[1:2, cols]
                               + d_prev * cw_ref[2:3, cols]).astype(BF16)
            gate = u_ref[:, cols]
            dcw_ref[:, cols] += jnp.concatenate([_colsum(d_next * gate), _colsum(d_here * gate),
                                                 _colsum(d_prev * gate)], axis=0)
            dcb_ref[:, cols] += _colsum(d_here)

    def trio(width, halo, tile_width=None, colblk=0):
        r, last = tm // halo, n // halo - 1
        return [pl.BlockSpec((tm, tile_width or width), lambda i: (i, 0)),
                pl.BlockSpec((halo, width), lambda i: (jnp.maximum(i * r - 1, 0), colblk)),
                pl.BlockSpec((halo, width), lambda i: (jnp.minimum((i + 1) * r, last), colblk))]

    whole = lambda shape: pl.BlockSpec(shape, lambda i: (0, 0))
    return pl.pallas_call(
        body, grid=(ni,),
        in_specs=(trio(d, 16) + [whole(w_down.shape)] + trio(DFF, HALO, tile_width=2 * DFF, colblk=1)
                  + trio(DFF, HALO) + [whole((3, DFF))]),
        out_specs=[pl.BlockSpec((tm, DFF), lambda i: (i, 0)), pl.BlockSpec((tm, DFF), lambda i: (i, 0)),
                   whole((3, DFF)), whole((1, DFF))],
        out_shape=[jax.ShapeDtypeStruct((n, DFF), BF16), jax.ShapeDtypeStruct((n, DFF), BF16),
                   jax.ShapeDtypeStruct((3, DFF), F32), jax.ShapeDtypeStruct((1, DFF), F32)],
        name=name, compiler_params=_params("arbitrary"))(dz, dz, dz, w_down, u, u, u, gc, gc, gc, cw)


def _rope_tables(n):
    rows = n // GRID_W
    axis_dim = HD // 2
    inv_freq = jnp.power(ROPE_THETA, -jnp.arange(0, axis_dim, 2, dtype=F32) / axis_dim)
    ar = jnp.arange(rows, dtype=F32)[:, None] * inv_freq
    ac = jnp.arange(GRID_W, dtype=F32)[:, None] * inv_freq
    by_row = lambda a: jnp.repeat(a, GRID_W, axis=0)
    by_col = lambda a: jnp.tile(a, (rows, 1))
    cr, sr, cc, sc = by_row(jnp.cos(ar)), by_row(jnp.sin(ar)), by_col(jnp.cos(ac)), by_col(jnp.sin(ac))
    return jnp.concatenate([cr, cr, cc, cc], axis=1), jnp.concatenate([-sr, sr, -sc, sc], axis=1)


def _partner(v):
    lane = lax.broadcasted_iota(jnp.int32, v.shape, 1)
    return jnp.where((lane % 64) < 32, pltpu.roll(v, HD - 32, axis=1), pltpu.roll(v, 32, axis=1))


def _qkv_prep(p, q_gain, k_gain, cs, sn, *, name, has_q, kv_col, kv_rows=None, kv_row_off=0, kv_into=None, tm=256):
    n = p.shape[0]
    rope = cs is not None
    kv_rows = kv_rows or n
    rb = kv_row_off // tm

    def body(*refs):
        it = iter(refs)
        q_ref = next(it) if has_q else None
        kv_ref = next(it)
        qg_ref, kg_ref = next(it), next(it)
        cs_ref = next(it) if rope else None
        sn_ref = next(it) if rope else None
        if kv_into is not None:
            next(it), next(it)
        qo_ref = next(it) if has_q else None
        ko_ref, vo_ref = next(it), next(it)

        def norm_rope(xh, gain, mul=None):
            r = lax.rsqrt(jnp.mean(xh * xh, axis=-1, keepdims=True) + EPS)
            xn = (xh * r) * gain
            if rope:
                xn = xn * cs_ref[...] + _partner(xn) * sn_ref[...]
            if mul is not None:
                xn = xn * mul
            return xn.astype(BF16)

        if has_q:
            for h in range(NQ):
                qo_ref[h] = norm_rope(q_ref[:, h * HD:(h + 1) * HD], qg_ref[...], _QSCALE)
        for h in range(NKV):
            ko_ref[h] = norm_rope(kv_ref[:, h * HD:(h + 1) * HD], kg_ref[...])
            vo_ref[h] = kv_ref[:, (NKV + h) * HD:(NKV + h + 1) * HD].astype(BF16)

    in_specs, args = [], []
    if has_q:
        in_specs.append(pl.BlockSpec((tm, AW), lambda i: (i, 0)))
        args.append(p)
    in_specs += [pl.BlockSpec((tm, 2 * NKV * HD), lambda i: (i, kv_col)), _vec(HD), _vec(HD)]
    args += [p, q_gain, k_gain]
    if rope:
        in_specs += [pl.BlockSpec((tm, HD), lambda i: (i, 0))] * 2
        args += [cs, sn]
    out_specs, out_shape = [], []
    if has_q:
        out_specs.append(pl.BlockSpec((NQ, tm, HD), lambda i: (0, i, 0)))
        out_shape.append(jax.ShapeDtypeStruct((NQ, n, HD), BF16))
    out_specs += [pl.BlockSpec((NKV, tm, HD), lambda i: (0, rb + i, 0))] * 2
    out_shape += [jax.ShapeDtypeStruct((NKV, kv_rows, HD), BF16)] * 2
    aliases = {}
    if kv_into is not None:
        aliases = {len(args): int(has_q), len(args) + 1: int(has_q) + 1}
        in_specs += [pl.BlockSpec(memory_space=pl.ANY)] * 2
        args += list(kv_into)
    return pl.pallas_call(body, grid=(n // tm,), in_specs=in_specs, out_specs=out_specs, out_shape=out_shape,
                          input_output_aliases=aliases, name=name, compiler_params=_params("parallel"))(*args)


def _qkv_bwd(p, dq, dk, dv, q_gain, k_gain, cs, sn, *, name, has_q, kv_col, kv_row_off, tm=256):
    n = p.shape[0]
    rope = cs is not None
    rb = kv_row_off // tm

    def body(*refs):
        it = iter(refs)
        q_ref = next(it) if has_q else None
        kv_ref = next(it)
        dq_ref = next(it) if has_q else None
        dk_ref, dv_ref = next(it), next(it)
        qg_ref, kg_ref = next(it), next(it)
        cs_ref = next(it) if rope else None
        sn_ref = next(it) if rope else None
        dp_ref, dqg_ref, dkg_ref = next(it), next(it), next(it)
        i = pl.program_id(0)

        def back(xh, dout, gain):
            if rope:
                dout = dout * cs_ref[...] + _partner(dout * sn_ref[...])
            r = lax.rsqrt(jnp.mean(xh * xh, axis=-1, keepdims=True) + EPS)
            xhat = xh * r
            dxh = dout * gain
            dx = r * (dxh - xhat * jnp.mean(dxh * xhat, axis=-1, keepdims=True))
            return dx, _colsum(dout * xhat)

        dqg = jnp.zeros((1, HD), F32)
        dkg = jnp.zeros((1, HD), F32)
        if has_q:
            for h in range(NQ):
                dx, dg = back(q_ref[:, h * HD:(h + 1) * HD], dq_ref[h], qg_ref[...])
                dp_ref[:, h * HD:(h + 1) * HD] = dx.astype(BF16)
                dqg = dqg + dg
        else:
            dp_ref[:, 0:AW] = jnp.zeros((tm, AW), BF16)
        for h in range(NKV):
            dx, dg = back(kv_ref[:, h * HD:(h + 1) * HD], dk_ref[h], kg_ref[...])
            dp_ref[:, AW + h * HD:AW + (h + 1) * HD] = dx.astype(BF16)
            dkg = dkg + dg
            dp_ref[:, AW + (NKV + h) * HD:AW + (NKV + h + 1) * HD] = dv_ref[h].astype(BF16)
        _acc_out(dqg_ref, i, dqg)
        _acc_out(dkg_ref, i, dkg)

    in_specs, args = [], []
    if has_q:
        in_specs.append(pl.BlockSpec((tm, AW), lambda i: (i, 0)))
        args.append(p)
    in_specs.append(pl.BlockSpec((tm, 2 * NKV * HD), lambda i: (i, kv_col)))
    args.append(p)
    if has_q:
        in_specs.append(pl.BlockSpec((NQ, tm, HD), lambda i: (0, i, 0)))
        args.append(dq)
    in_specs += [pl.BlockSpec((NKV, tm, HD), lambda i: (0, rb + i, 0))] * 2 + [_vec(HD), _vec(HD)]
    args += [dk, dv, q_gain, k_gain]
    if rope:
        in_specs += [pl.BlockSpec((tm, HD), lambda i: (i, 0))] * 2
        args += [cs, sn]
    return pl.pallas_call(
        body, grid=(n // tm,), in_specs=in_specs,
        out_specs=[pl.BlockSpec((tm, D), lambda i: (i, 0)), _vec(HD), _vec(HD)],
        out_shape=[jax.ShapeDtypeStruct((n, D), BF16), jax.ShapeDtypeStruct((1, HD), F32),
                   jax.ShapeDtypeStruct((1, HD), F32)],
        name=name, compiler_params=_params("arbitrary"))(*args)


def _conv_gate_fwd(p, o, conv_w, *, name, tm=256):
    n = p.shape[0]
    ni = n // tm

    def body(gb_ref, gc_ref, gcp_ref, gcn_ref, xi_ref, xip_ref, xin_ref, o_ref, w_ref, cat_ref):
        i = pl.program_id(0)
        hext = _ext(gcp_ref, gc_ref, gcn_ref, i, ni) * _ext(xip_ref, xi_ref, xin_ref, i, ni)
        cat_ref[:, 0:AW] = o_ref[...].astype(BF16)
        cat_ref[:, AW:D] = (gb_ref[...] * _conv3(hext, w_ref, tm)).astype(BF16)

    gcp, gcn = _halo_specs(tm, CW, n, colblk=3)
    xip, xin = _halo_specs(tm, CW, n, colblk=4)
    return pl.pallas_call(
        body, grid=(ni,),
        in_specs=[pl.BlockSpec((tm, CW), lambda i: (i, 2)), pl.BlockSpec((tm, CW), lambda i: (i, 3)), gcp, gcn,
                  pl.BlockSpec((tm, CW), lambda i: (i, 4)), xip, xin, pl.BlockSpec((tm, AW), lambda i: (i, 0)),
                  pl.BlockSpec((3, CW), lambda i: (0, 0))],
        out_specs=pl.BlockSpec((tm, D), lambda i: (i, 0)), out_shape=jax.ShapeDtypeStruct((n, D), BF16),
        name=name, compiler_params=_params("parallel"))(p, p, p, p, p, p, p, o, conv_w)


def _conv_gate_bwd(dcat, p, conv_w, *, name, tm=256):
    n = p.shape[0]
    ni = n // tm

    def body(dc_ref, dcp_ref, dcn_ref, gb_ref, gbp_ref, gbn_ref, gc_ref, gcp_ref, gcn_ref, xi_ref, xip_ref, xin_ref,
             w_ref, dp_ref, dw_ref):
        i = pl.program_id(0)
        gcext = _ext(gcp_ref, gc_ref, gcn_ref, i, ni)
        xiext = _ext(xip_ref, xi_ref, xin_ref, i, ni)
        hext = gcext * xiext
        dcv = _ext(dcp_ref, dc_ref, dcn_ref, i, ni) * _ext(gbp_ref, gb_ref, gbn_ref, i, ni)
        dp_ref[:, 0:CW] = (dc_ref[...] * _conv3(hext, w_ref, tm)).astype(BF16)
        dh = _sh(dcv, 1, tm) * w_ref[0:1, :] + _sh(dcv, 0, tm) * w_ref[1:2, :] + _sh(dcv, -1, tm) * w_ref[2:3, :]
        dp_ref[:, CW:2 * CW] = (dh * xi_ref[...]).astype(BF16)
        dp_ref[:, 2 * CW:3 * CW] = (dh * gc_ref[...]).astype(BF16)
        dcv_t = dcv[HALO:HALO + tm]
        dw = jnp.concatenate([_colsum(dcv_t * _sh(hext, -1, tm)), _colsum(dcv_t * _sh(hext, 0, tm)),
                              _colsum(dcv_t * _sh(hext, 1, tm))], axis=0)
        _acc_out(dw_ref, i, dw)

    def trio(colblk):
        prev, nxt = _halo_specs(tm, CW, n, colblk=colblk)
        return [pl.BlockSpec((tm, CW), lambda i: (i, colblk)), prev, nxt]

    return pl.pallas_call(
        body, grid=(ni,), in_specs=trio(1) + trio(2) + trio(3) + trio(4) + [pl.BlockSpec((3, CW), lambda i: (0, 0))],
        out_specs=[pl.BlockSpec((tm, 3 * CW), lambda i: (i, 0)), pl.BlockSpec((3, CW), lambda i: (0, 0))],
        out_shape=[jax.ShapeDtypeStruct((n, 3 * CW), BF16), jax.ShapeDtypeStruct((3, CW), F32)],
        name=name, compiler_params=_params("arbitrary"))(dcat, dcat, dcat, p, p, p, p, p, p, p, p, p, conv_w)


def _attn_fwd(q, k, v, *, name, bq=512, sub=256):
    n = q.shape[1]
    t = k.shape[1]
    bq = min(bq, n)
    sub = min(sub, 2 * bq)

    def body(q_ref, k_ref, v_ref, o_ref, lse_ref):
        q2 = q_ref[...].reshape(2 * bq, HD)
        outs, lses = [], []
        for r0 in range(0, 2 * bq, sub):
            s = lax.dot_general(q2[r0:r0 + sub], k_ref[0], _NT, preferred_element_type=F32)
            m = jnp.max(s, axis=-1, keepdims=True)
            pv = jnp.exp2(s - m)
            l = jnp.sum(pv, axis=-1, keepdims=True)
            outs.append(jnp.dot(pv.astype(BF16), v_ref[0], preferred_element_type=F32) / l)
            lses.append(m + jnp.log2(l))
        out = jnp.concatenate(outs, axis=0)
        o_ref[:, 0:HD] = out[0:bq]
        o_ref[:, HD:2 * HD] = out[bq:2 * bq]
        lse_ref[...] = jnp.concatenate(lses, axis=0).reshape(2, bq, 1)

    kspec = pl.BlockSpec((1, t, HD), lambda h, i: (h, 0, 0))
    return pl.pallas_call(
        body, grid=(NKV, n // bq),
        in_specs=[pl.BlockSpec((2, bq, HD), lambda h, i: (h, i, 0)), kspec, kspec],
        out_specs=[pl.BlockSpec((bq, 2 * HD), lambda h, i: (i, h)), pl.BlockSpec((2, bq, 1), lambda h, i: (h, i, 0))],
        out_shape=[jax.ShapeDtypeStruct((n, AW), F32), jax.ShapeDtypeStruct((NQ, n, 1), F32)],
        name=name, compiler_params=_params("parallel", "parallel"))(q, k, v)


def _attn_bwd(q, k, v, dcat, o, lse, *, name, bq=256):
    n = q.shape[1]
    t = k.shape[1]
    bq = min(bq, n)

    def body(q_ref, k_ref, v_ref, dc_ref, o_ref, lse_ref, dq_ref, dk_ref, dv_ref):
        @pl.when(pl.program_id(1) == 0)
        def _():
            dk_ref[...] = jnp.zeros_like(dk_ref)
            dv_ref[...] = jnp.zeros_like(dv_ref)

        q2 = q_ref[...].reshape(2 * bq, HD)
        do_f = jnp.concatenate([dc_ref[:, 0:HD], dc_ref[:, HD:2 * HD]], axis=0)
        o_f = jnp.concatenate([o_ref[:, 0:HD], o_ref[:, HD:2 * HD]], axis=0)
        delta = jnp.sum(do_f * o_f, axis=-1, keepdims=True)
        do2 = do_f.astype(BF16)
        s = lax.dot_general(q2, k_ref[0], _NT, preferred_element_type=F32)
        pv = jnp.exp2(s - lse_ref[...].reshape(2 * bq, 1))
        dp = lax.dot_general(do2, v_ref[0], _NT, preferred_element_type=F32)
        ds = (pv * (dp - delta)).astype(BF16)
        dq_ref[...] = (jnp.dot(ds, k_ref[0], preferred_element_type=F32) * _SCALE).reshape(2, bq, HD)
        dk_ref[0] += lax.dot_general(ds, q2, _TN, preferred_element_type=F32) * _LN2
        dv_ref[0] += lax.dot_general(pv.astype(BF16), do2, _TN, preferred_element_type=F32)

    qspec = pl.BlockSpec((2, bq, HD), lambda h, i: (h, i, 0))
    kspec = pl.BlockSpec((1, t, HD), lambda h, i: (h, 0, 0))
    sspec = pl.BlockSpec((2, bq, 1), lambda h, i: (h, i, 0))
    cspec = pl.BlockSpec((bq, 2 * HD), lambda h, i: (i, h))
    return pl.pallas_call(
        body, grid=(NKV, n // bq), in_specs=[qspec, kspec, kspec, cspec, cspec, sspec], out_specs=[qspec, kspec, kspec],
        out_shape=[jax.ShapeDtypeStruct((NQ, n, HD), F32), jax.ShapeDtypeStruct((NKV, t, HD), F32),
                   jax.ShapeDtypeStruct((NKV, t, HD), F32)],
        name=name, compiler_params=_params("parallel", "arbitrary"))(q, k, v, dcat, o, lse)


def _window_sums(ext, w):
    s, step = ext, 1
    while step < w:
        s = s + _roll_rows(s, step)
        step *= 2
    return s


def _pool_counts(i, tm, n, w, rows, first):
    t = i * tm - HALO + first + lax.broadcasted_iota(jnp.int32, (rows, 1), 0)
    lo = jnp.clip(t - w // 2, 0, n)
    hi = jnp.clip(t + w - w // 2, 0, n)
    return jnp.maximum(hi - lo, 1).astype(F32)


def _norm_mod_ext(xext, gain_ref, sc_ref, sh_ref, i, tm, n):
    rows = xext.shape[0]
    t = i * tm - HALO + lax.broadcasted_iota(jnp.int32, (rows, 1), 0)
    inside = (t >= 0) & (t < n)
    r = lax.rsqrt(jnp.mean(xext * xext, axis=-1, keepdims=True) + EPS)
    xh = xext * r
    a = (xh * gain_ref[...]) * (1.0 + sc_ref[...]) + sh_ref[...]
    return jnp.where(inside, a, 0.0), r, xh


def _pool_fwd(x, y, g, gain, sc, sh, pool_w, *, name, tm=256):
    n, d = x.shape
    ni = n // tm

    def body(x_ref, xp_ref, xn_ref, y_ref, yp_ref, yn_ref, g_ref, gain_ref, sc_ref, sh_ref, w_ref, xo_ref, o_ref):
        i = pl.program_id(0)
        xext = _ext(xp_ref, x_ref, xn_ref, i, ni) + g_ref[...] * _ext(yp_ref, y_ref, yn_ref, i, ni)
        xo_ref[...] = xext[HALO:HALO + tm]
        aext, _, _ = _norm_mod_ext(xext, gain_ref, sc_ref, sh_ref, i, tm, n)
        for gi, w in enumerate(POOL_WINDOWS):
            ag = aext[:, gi * PG:(gi + 1) * PG]
            mean = _sh(_window_sums(ag, w), -(w // 2), tm) / _pool_counts(i, tm, n, w, tm, HALO)
            pooled = mean - ag[HALO:HALO + tm]
            o_ref[:, gi * PG:(gi + 1) * PG] = jnp.dot(pooled.astype(BF16), w_ref[gi], preferred_element_type=F32)

    row = pl.BlockSpec((tm, d), lambda i: (i, 0))
    prev, nxt = _halo_specs(tm, d, n)
    return pl.pallas_call(
        body, grid=(ni,),
        in_specs=[row, prev, nxt, row, prev, nxt, _vec(d), _vec(d), _vec(d), _vec(d),
                  pl.BlockSpec((4, PG, PG), lambda i: (0, 0, 0))],
        out_specs=[row, row], out_shape=[jax.ShapeDtypeStruct((n, d), F32)] * 2,
        name=name, compiler_params=_params("parallel"))(x, x, x, y, y, y, g, gain, sc, sh, pool_w)


def _pool_bwd(dxo, mixed, x, g, scale, gain, sc, sh, pool_w, zprev, gprev, *, name, tm=256):
    n, d = x.shape
    ni = n // tm

    def body(dx_ref, dxp_ref, dxn_ref, mx_ref, x_ref, xp_ref, xn_ref, g_ref, s_ref, gain_ref, sc_ref, sh_ref, w_ref,
             zp_ref, gp_ref, dxi_ref, dw_ref, dg_ref, dsl_ref, dsh_ref, dsc_ref, dgn_ref, dzp_ref, dgp_ref):
        i = pl.program_id(0)

        @pl.when(i == 0)
        def _():
            dw_ref[...] = jnp.zeros_like(dw_ref)

        dxo_t = dx_ref[...]
        mixed_t = mx_ref[...]
        dy_t = dxo_t * g_ref[...]
        _acc_out(dg_ref, i, _colsum(dxo_t * (mixed_t * s_ref[...])))
        _acc_out(dsl_ref, i, _colsum(dy_t * mixed_t))
        dmixed = (_ext(dxp_ref, dx_ref, dxn_ref, i, ni) * g_ref[...]) * s_ref[...]
        xext = _ext(xp_ref, x_ref, xn_ref, i, ni)
        aext, rext, xhext = _norm_mod_ext(xext, gain_ref, sc_ref, sh_ref, i, tm, n)
        rows = tm + 2 * HALO
        da_parts = []
        for gi, w in enumerate(POOL_WINDOWS):
            sl = slice(gi * PG, (gi + 1) * PG)
            ag = aext[:, sl]
            mean = _sh(_window_sums(ag, w), -(w // 2), tm) / _pool_counts(i, tm, n, w, tm, HALO)
            pooled = (mean - ag[HALO:HALO + tm]).astype(BF16)
            dmg = dmixed[:, sl].astype(BF16)
            dw_ref[gi] += lax.dot_general(pooled, dmixed[HALO:HALO + tm, sl].astype(BF16), _TN,
                                          preferred_element_type=F32)
            dpl = lax.dot_general(dmg, w_ref[gi], _NT, preferred_element_type=F32)
            e = dpl / _pool_counts(i, tm, n, w, rows, 0)
            da_parts.append(_sh(_window_sums(e, w), 1 - w // 2, tm) - dpl[HALO:HALO + tm])
        da = jnp.concatenate(da_parts, axis=1)
        r = rext[HALO:HALO + tm]
        xh = xhext[HALO:HALO + tm]
        nrm = xh * gain_ref[...]
        dn = da * (1.0 + sc_ref[...])
        dxh = dn * gain_ref[...]
        dxi = dxo_t + r * (dxh - xh * jnp.mean(dxh * xh, axis=-1, keepdims=True))
        dxi_ref[...] = dxi
        _acc_out(dsh_ref, i, _colsum(da))
        _acc_out(dsc_ref, i, _colsum(da * nrm))
        _acc_out(dgn_ref, i, _colsum(dn * xh))
        dzp_ref[...] = (dxi * gp_ref[...]).astype(BF16)
        _acc_out(dgp_ref, i, _colsum(dxi * zp_ref[...]))

    row = pl.BlockSpec((tm, d), lambda i: (i, 0))
    prev, nxt = _halo_specs(tm, d, n)
    wspec = pl.BlockSpec((4, PG, PG), lambda i: (0, 0, 0))
    vshape = jax.ShapeDtypeStruct((1, d), F32)
    return pl.pallas_call(
        body, grid=(ni,),
        in_specs=[row, prev, nxt, row, row, prev, nxt] + [_vec(d)] * 5 + [wspec, row, _vec(d)],
        out_specs=[row, wspec] + [_vec(d)] * 5 + [row, _vec(d)],
        out_shape=[jax.ShapeDtypeStruct((n, d), F32), jax.ShapeDtypeStruct((4, PG, PG), F32)] + [vshape] * 5
        + [jax.ShapeDtypeStruct((n, d), BF16), vshape],
        name=name, compiler_params=_params("arbitrary"))(dxo, dxo, dxo, mixed, x, x, x, g, scale, gain, sc, sh, pool_w,
                                                         zprev, gprev)


def _adamw(gparts_list, w, m, v, *, name, silu_grad_of=None):
    nl = len(gparts_list)
    nparts, r, c = gparts_list[0].shape
    tr = _pick(r, (256, 128, 64, 32, 16, 8))
    has_c = silu_grad_of is not None

    def body(*refs):
        gp_refs = refs[:nl]
        it = iter(refs[nl:])
        w_ref, m_ref, v_ref = next(it), next(it), next(it)
        c_ref = next(it) if has_c else None
        g_ref, d_ref, mo_ref, vo_ref = next(it), next(it), next(it), next(it)
        layer = pl.program_id(0)

        def update(gp_ref):
            g = gp_ref[0].astype(F32)
            for p in range(1, nparts):
                g = g + gp_ref[p].astype(F32)
            if has_c:
                cv = c_ref[0]
                sg = _sigmoid(cv)
                g = g * (sg * (1.0 + cv * (1.0 - sg)))
            g_ref[0] = g
            mn = ADAM_B1 * m_ref[0] + (1.0 - ADAM_B1) * g
            vn = ADAM_B2 * v_ref[0] + (1.0 - ADAM_B2) * (g * g)
            m_hat = mn / (1.0 - ADAM_B1 ** ADAM_STEP)
            v_hat = vn / (1.0 - ADAM_B2 ** ADAM_STEP)
            d_ref[0] = -ADAM_LR * (m_hat / (jnp.sqrt(v_hat) + ADAM_EPS) + ADAM_WD * w_ref[0])
            mo_ref[0] = mn
            vo_ref[0] = vn

        if nl == 1:
            update(gp_refs[0])
        else:
            for li in range(nl):
                pl.when(layer == li)(functools.partial(update, gp_refs[li]))

    row = pl.BlockSpec((1, tr, c), lambda l, i: (l, i, 0))
    in_specs = [pl.BlockSpec((nparts, tr, c), lambda l, i, li=li: (0, jnp.where(l == li, i, 0), 0)) for li in range(nl)]
    in_specs += [row, row, row]
    args = list(gparts_list) + [w, m, v]
    if has_c:
        in_specs.append(row)
        args.append(silu_grad_of)
    return pl.pallas_call(
        body, grid=(nl, r // tr), in_specs=in_specs, out_specs=[row] * 4,
        out_shape=[jax.ShapeDtypeStruct((nl, r, c), F32)] * 4, name=name,
        compiler_params=_params("arbitrary", "arbitrary"))(*args)


def _adamw_nd(gparts, w, m, v, *, name, silu_grad_of=None):
    shape = w.shape
    c = shape[-1]
    if isinstance(gparts, (list, tuple)):
        nl = len(gparts)
        r = math.prod(shape[1:-1])
    else:
        nl = 1
        r = math.prod(shape[:-1]) if len(shape) > 1 else 1
        gparts = [gparts]
    rs = lambda a: a.reshape(nl, r, c)
    res = _adamw([gp.reshape(gp.shape[0], r, c) for gp in gparts], rs(w), rs(m), rs(v), name=name,
                 silu_grad_of=None if silu_grad_of is None else rs(silu_grad_of))
    return [a.reshape(shape) for a in res]


def _place():
    return lax.axis_index("x"), lax.axis_index("y"), lax.axis_index("c")


def _all_gather(arrs, *, name):
    k_arr = len(arrs)

    def body(*refs):
        ins = refs[:k_arr]
        outs = refs[k_arr:2 * k_arr]
        send_sems, recv_sems, local_sems = refs[2 * k_arr:]
        x, y, c = _place()
        me, sibling = (x, y, c), (x, y, 1 - c)
        chips = [(1 - x, y), (x, 1 - y), (1 - x, 1 - y)]

        def slot(a, px, py, pc):
            return outs[a].at[4 * px + 2 * py + pc]

        def copy(a, s, block, to, src=None):
            return pltpu.make_async_remote_copy(
                src_ref=slot(a, *block) if src is None else src, dst_ref=slot(a, *block),
                send_sem=send_sems.at[a, s], recv_sem=recv_sems.at[a, s], device_id=to, device_id_type=MESH)

        mine = [pltpu.make_async_copy(ins[a], slot(a, *me), local_sems.at[a]) for a in range(k_arr)]
        for cp in mine:
            cp.start()
        first = []
        for a in range(k_arr):
            first.append(copy(a, 0, me, sibling, src=ins[a]))
            first += [copy(a, 1 + j, me, (*chip, c), src=ins[a]) for j, chip in enumerate(chips)]
        for cp in first:
            cp.start()
        passed = []
        for j, chip in enumerate(chips):
            for a in range(k_arr):
                copy(a, 1 + j, (*chip, c), me).wait_recv()
                fw = copy(a, 4 + j, (*chip, c), sibling)
                fw.start()
                passed.append(fw)
        for a in range(k_arr):
            copy(a, 0, sibling, me).wait_recv()
            for j, chip in enumerate(chips):
                copy(a, 4 + j, (*chip, 1 - c), me).wait_recv()
        for cp in first + passed:
            cp.wait_send()
        for cp in mine:
            cp.wait()

    any_spec = pl.BlockSpec(memory_space=pl.ANY)
    return pl.pallas_call(
        body, in_specs=[any_spec] * k_arr, out_specs=[any_spec] * k_arr,
        out_shape=[jax.ShapeDtypeStruct((NDEV,) + a.shape, a.dtype) for a in arrs],
        scratch_shapes=[pltpu.SemaphoreType.DMA((k_arr, 7)), pltpu.SemaphoreType.DMA((k_arr, 7)),
                        pltpu.SemaphoreType.DMA((k_arr,))],
        name=name)(*arrs)


_HBM = pl.BlockSpec(memory_space=pltpu.HBM)
_SEM = pl.BlockSpec(memory_space=pltpu.SEMAPHORE)
_EFFECT = pltpu.SideEffectType.DATAFLOW_SIDE_EFFECTING


def _peers(x, y, c):
    return [(x ^ (rel >> 2), y ^ ((rel >> 1) & 1), c ^ (rel & 1)) for rel in range(1, NDEV)]


def _exchange_copies(srcs, lands, send_sems, recv_sems, scatter):
    x, y, c = _place()
    me = 4 * x + 2 * y + c
    copies = []
    for r, (px, py, pc) in enumerate(_peers(x, y, c)):
        peer = 4 * px + 2 * py + pc
        for a in range(len(srcs)):
            copies.append(pltpu.make_async_remote_copy(
                src_ref=srcs[a].at[peer] if scatter else srcs[a], dst_ref=lands[a].at[me],
                send_sem=send_sems.at[7 * a + r], recv_sem=recv_sems.at[7 * a + r], device_id=(px, py, pc),
                device_id_type=MESH))
    return copies


def _exchange_start(arrs, *, scatter, name):
    k_arr = len(arrs)
    land_shapes = [a.shape if scatter else (NDEV,) + a.shape for a in arrs]
    lands = [pltpu.with_memory_space_constraint(lax.empty(s, a.dtype), pltpu.HBM) for s, a in zip(land_shapes, arrs)]
    srcs = [pltpu.with_memory_space_constraint(a, pltpu.HBM) for a in arrs]

    def body(*refs):
        src_refs, land_refs = refs[:k_arr], refs[k_arr:2 * k_arr]
        send_sems, recv_sems = refs[2 * k_arr], refs[2 * k_arr + 1]
        token = refs[-1]
        for cp in _exchange_copies(src_refs, land_refs, send_sems, recv_sems, scatter):
            cp.start()
        token[...] = jnp.zeros_like(token)

    out_shape = ([pltpu.SemaphoreType.DMA((7 * k_arr,)), pltpu.SemaphoreType.DMA((7 * k_arr,))]
                 + [pltpu.HBM(a.shape, a.dtype) for a in arrs] + [pltpu.HBM(s, a.dtype) for s, a in zip(land_shapes, arrs)]
                 + [jax.ShapeDtypeStruct((8, 128), F32)])
    res = pl.pallas_call(
        body, name=name, out_shape=out_shape, in_specs=[_HBM] * (2 * k_arr),
        out_specs=[_SEM, _SEM] + [_HBM] * (2 * k_arr) + [pl.BlockSpec(memory_space=pltpu.VMEM)],
        input_output_aliases={i: 2 + i for i in range(2 * k_arr)},
        compiler_params=pltpu.CompilerParams(has_side_effects=_EFFECT))(*srcs, *lands)
    return dict(send=res[0], recv=res[1], srcs=list(res[2:2 + k_arr]), lands=list(res[2 + k_arr:2 + 2 * k_arr]),
                token=res[-1], scatter=scatter)


def _exchange_wait(handle, after, *, name):
    k_arr = len(handle["srcs"])
    scatter = handle["scatter"]

    def body(*refs):
        src_refs, land_refs = refs[:k_arr], refs[k_arr:2 * k_arr]
        send_sems, recv_sems = refs[2 * k_arr], refs[2 * k_arr + 1]
        x, y, c = _place()
        me = 4 * x + 2 * y + c
        for r, (px, py, pc) in enumerate(_peers(x, y, c)):
            peer = 4 * px + 2 * py + pc
            for a in range(k_arr):
                cp = pltpu.make_async_remote_copy(
                    src_ref=src_refs[a].at[peer] if scatter else src_refs[a], dst_ref=land_refs[a].at[peer],
                    send_sem=send_sems.at[7 * a + r], recv_sem=recv_sems.at[7 * a + r], device_id=(x, y, c),
                    device_id_type=MESH)
                cp.wait_send()
                cp.wait_recv()

    arrs = handle["srcs"] + handle["lands"]
    res = pl.pallas_call(
        body, name=name, out_shape=[pltpu.HBM(a.shape, a.dtype) for a in arrs],
        in_specs=[_HBM] * (2 * k_arr) + [_SEM, _SEM, pl.BlockSpec(memory_space=pl.ANY)],
        out_specs=[_HBM] * (2 * k_arr), input_output_aliases={i: i for i in range(2 * k_arr)},
        compiler_params=pltpu.CompilerParams(has_side_effects=_EFFECT))(*arrs, handle["send"], handle["recv"], after)
    me = 4 * lax.axis_index("x") + 2 * lax.axis_index("y") + lax.axis_index("c")
    out = []
    for src, land in zip(res[:k_arr], res[k_arr:]):
        own = lax.dynamic_index_in_dim(src, me, 0, keepdims=False) if scatter else src
        out.append(lax.dynamic_update_index_in_dim(land, own, me, 0))
    return out


def _ffn_bwd(dxo, dz, xr, f, u_gc, hmid, gain, sc, w_up, cw, w_down, tag, gate_y=None, gate_g=None):
    d_wdown = _mm_tn((hmid, dz), name=f"ffn_down_dw_{tag}")
    dug, duv, dcw, dcb = _ffn_down_glu_bwd(dz, w_down, u_gc[0], u_gc[1], cw, name=f"ffn_down_glu_bwd_{tag}")
    d_wup_g = _mm_tn((f, dug), name=f"ffn_up_dwg_{tag}")
    d_wup_v = _mm_tn((f, duv), name=f"ffn_up_dwv_{tag}")
    gated = gate_y is not None
    res = _mm_w_ep([dug, duv], w_up, _ep_norm_bwd(gated), [xr, dxo] + ([gate_y] if gated else []),
                   [gain, sc] + ([gate_g] if gated else []), [F32] + ([BF16] if gated else []),
                   [D] * (4 if gated else 3), tb=True, name=f"ffn_up_dx_norm_bwd_{tag}")
    n_out = 2 if gated else 1
    return res[:n_out], res[n_out:], (d_wup_g, d_wup_v, d_wdown, dcw, dcb)


def _split6(mod):
    return [mod[j * D:(j + 1) * D][None, :] for j in range(6)]


def _row(v):
    return v.reshape(1, -1)


def kernel(x, c, ctx, c_ctx, ada_w, ada_b, mix_norm, ffn_norm, even_w_in, even_q_gain, even_k_gain, even_conv_w, even_w_out, odd_pool_w, odd_pool_scale, ffn_w_up, ffn_conv_w, ffn_conv_b, ffn_w_down, loss_target, m_c_ctx, m_ada_w, m_ada_b, m_mix_norm, m_ffn_norm, m_even_w_in, m_even_q_gain, m_even_k_gain, m_even_conv_w, m_even_w_out, m_odd_pool_w, m_odd_pool_scale, m_ffn_w_up, m_ffn_conv_w, m_ffn_conv_b, m_ffn_w_down, v_c_ctx, v_ada_w, v_ada_b, v_mix_norm, v_ffn_norm, v_even_w_in, v_even_q_gain, v_even_k_gain, v_even_conv_w, v_even_w_out, v_odd_pool_w, v_odd_pool_scale, v_ffn_w_up, v_ffn_conv_w, v_ffn_conv_b, v_ffn_w_down):
    n = x.shape[1]
    lc = ctx.shape[1]
    me = 4 * lax.axis_index("x") + 2 * lax.axis_index("y") + lax.axis_index("c")
    xs, ctxs, tgt = x[0], ctx[0], loss_target[0]
    acols = ada_w.shape[2]

    small = jnp.concatenate([even_conv_w.reshape(-1), ffn_conv_w.reshape(-1), odd_pool_scale.reshape(-1)])
    nsmall = small.shape[0]
    small = jnp.pad(small, (0, (-nsmall) % 1024)).reshape(-1, 128)
    c_rows = jnp.pad(c, ((0, 7), (0, 0)))
    g_c, g_win, g_small = _all_gather([c_rows, even_w_in[0].astype(BF16), small], name="gather_first")
    w_in = jnp.concatenate([g_win[j] for j in range(NDEV)], axis=1)
    g_small = g_small.reshape(NDEV, -1)
    ecw = even_conv_w.shape[2]
    fcw = ffn_conv_w.shape[2]
    conv_w = g_small[:, :3 * ecw].reshape(NDEV, 3, ecw).transpose(1, 0, 2).reshape(3, CW)
    o1 = 3 * ecw
    fconv_w = g_small[:, o1:o1 + 6 * fcw].reshape(NDEV, 2, 3, fcw).transpose(1, 2, 0, 3).reshape(2, 3, DFF)
    o2 = o1 + 6 * fcw
    pool_scale = g_small[:, o2:o2 + D // NDEV].reshape(1, D)

    mraw = jnp.concatenate([g_c[:, 0, :], c_ctx[None, :], jnp.zeros((7, D), F32)], axis=0)
    my_bias = lax.dynamic_slice_in_dim(ada_b, me * acols, acols, axis=1)
    modp = jnp.stack([_mm(mraw, ada_w[l], silu_a=True, bias=my_bias[l:l + 1], name=f"ada_proj_{l}", tm=16, tn=256)
                      for l in range(2)])
    (g_mod,) = _all_gather([modp], name="gather_mod")
    mod_rows = g_mod.transpose(1, 2, 0, 3).reshape(2, 16, 6 * D)
    late_shards = [even_w_out[0].astype(BF16), odd_pool_w[0].astype(BF16), ffn_w_up.astype(BF16),
                   ffn_w_down.astype(BF16)]
    late_shards, mod_rows = lax.optimization_barrier((late_shards, mod_rows))
    h_weights = _exchange_start(late_shards, scatter=False, name="weights_start")
    mod_rows = mod_rows + h_weights["token"][0, 0]
    mod = lax.dynamic_index_in_dim(mod_rows, me, axis=1, keepdims=False)
    sh1, sc1, g1, sh2, sc2, g2 = _split6(mod[0])
    sh1b, sc1b, g1b, sh2b, sc2b, g2b = _split6(mod[1])
    csh1, csc1 = _split6(mod_rows[0, 8])[:2]
    mixn = [_row(mix_norm[l]) for l in range(2)]
    ffnn = [_row(ffn_norm[l]) for l in range(2)]
    qg, kg = _row(even_q_gain[0]), _row(even_k_gain[0])
    fcb = [_row(ffn_conv_b[l]) for l in range(2)]

    cs_t, sn_t = _rope_tables(n)
    a_lat = _norm_mod(xs, mixn[0], sc1, sh1, name="mix0_norm")
    a_ctx = _norm_mod(ctxs, mixn[0], csc1, csh1, name="mix0_norm_ctx")
    p_lat = _mm_w(a_lat, w_in, name="in_proj")
    p_ctx = _mm(a_ctx, w_in[:, AW:AW + 4 * HD], name="in_proj_ctx", tm=256, tn=512, tk=1024)
    kv_ctx = _qkv_prep(p_ctx, qg, kg, None, None, has_q=False, kv_col=0, kv_rows=lc + n, name="qkv_prep_ctx")
    q_r, k_all, v_all = _qkv_prep(p_lat, qg, kg, cs_t, sn_t, has_q=True, kv_col=1, kv_rows=lc + n, kv_row_off=lc,
                                  kv_into=kv_ctx, name="qkv_prep")
    o_attn, lse = _attn_fwd(q_r, k_all, v_all, name="attn_fwd")
    cat = _conv_gate_fwd(p_lat, o_attn, conv_w, name="conv_gate")
    g_wout, g_pool, g_up, g_down = _exchange_wait(h_weights, cat, name="weights_wait")
    w_out = g_wout.reshape(D, D)
    pool_w = g_pool.transpose(1, 0, 2, 3).reshape(4, PG, PG)
    w_up = [jnp.concatenate([g_up[j, l] for j in range(NDEV)], axis=1) for l in range(2)]
    w_down = [g_down[:, l].reshape(DFF, D) for l in range(2)]
    y0, x1, f0 = _mm_w_ep(cat, w_out, _ep_resid_norm, [xs], [g1, ffnn[0], sc2, sh2], [F32, F32, BF16], [],
                          tm=512, name="out_proj_norm")[:3]
    *u0, h0 = _ffn_up_glu(f0, w_up[0], fconv_w[0], fcb[0], name="ffn_up_glu_l0")
    z0 = _mm_w(h0, w_down[0], name="ffn_down_l0")

    x2, mixed = _pool_fwd(x1, z0, g2, mixn[1], sc1b, sh1b, pool_w, name="pool_fwd")
    x3, f1 = _norm_mod(x2, ffnn[1], sc2b, sh2b, y=mixed, g=g1b, ymul=pool_scale, name="ffn_norm_l1")
    *u1, h1 = _ffn_up_glu(f1, w_up[1], fconv_w[1], fcb[1], name="ffn_up_glu_l1")
    dx4, dz1, loss_part, dg2b = _mm_w_ep(h1, w_down[1], _ep_loss(D), [x3, tgt], [g2b], [F32, BF16], [128, D],
                                         tm=512, name="ffn_down_loss")
    loss = lax.psum(loss_part[0, 0], ("x", "y", "c"))

    (dx3,), (dsh2b, dsc2b, dffn1), (dup1g, dup1v, ddown1, dfcw1, dfcb1) = _ffn_bwd(
        dx4, dz1, x3, f1, u1, h1, ffnn[1], sc2b, w_up[1], fconv_w[1], w_down[1], "l1")
    dx2, dpool_w, dg1b, dpscale, dsh1b, dsc1b, dmix1, dz0, dg2 = _pool_bwd(
        dx3, mixed, x2, g1b, pool_scale, mixn[1], sc1b, sh1b, pool_w, z0, g2, name="pool_bwd")

    def col_shards(parts):
        w = sum(p.shape[1] for p in parts) // NDEV
        return jnp.stack([p[:, j * w:(j + 1) * w] for p in parts for j in range(p.shape[1] // w)])

    def up_shards(dg, dv):
        return col_shards([dg, dv])

    s_pool = dpool_w.astype(BF16).reshape(4, NDEV, PG // NDEV, PG).transpose(1, 0, 2, 3)
    h_g1 = _exchange_start([s_pool, up_shards(dup1g, dup1v), ddown1.reshape(NDEV, DFF // NDEV, D)], scatter=True,
                           name="grads1_start")

    (dx1, dy0), (dsh2, dsc2, dffn0, dg1), (dup0g, dup0v, ddown0, dfcw0, dfcb0) = _ffn_bwd(
        dx2, dz0, x1, f0, u0, h0, ffnn[0], sc2, w_up[0], fconv_w[0] + h_g1["token"][0, 0], w_down[0], "l0",
        gate_y=y0, gate_g=g1)
    h_g0 = _exchange_start([up_shards(dup0g, dup0v), ddown0.reshape(NDEV, DFF // NDEV, D)], scatter=True,
                           name="grads0_start")
    dcat = _mm_w(dy0, w_out, tb=True, name="out_proj_dx", tm=512)
    d_wout = _mm_tn((cat, dy0), name="out_proj_dw")
    dp_conv, dconv_w = _conv_gate_bwd(dcat, p_lat, conv_w + h_g0["token"][0, 0], name="conv_gate_bwd")
    dq_r, dk_all, dv_all = _attn_bwd(q_r, k_all, v_all, dcat, o_attn, lse, name="attn_bwd")
    dp_qkv, dqg_l, dkg_l = _qkv_bwd(p_lat, dq_r, dk_all, dv_all, qg, kg, cs_t, sn_t, has_q=True, kv_col=1,
                                    kv_row_off=lc, name="qkv_bwd")
    dp_ctx, _zero_qg, dkg_c = _qkv_bwd(p_ctx, None, dk_all, dv_all, qg, kg, None, None, has_q=False, kv_col=0,
                                       kv_row_off=0, name="qkv_bwd_ctx")
    da_ctx = _mm(dp_ctx, w_in[:, :D], tb=True, name="in_proj_dx_ctx", tm=256, tn=512, tk=1024)
    d_win_qkv = _mm_tn([(a_lat, dp_qkv), (a_ctx, dp_ctx)], name="in_proj_dw_qkv")
    d_win_conv = _mm_tn((a_lat, dp_conv), name="in_proj_dw_conv")
    d_win = jnp.concatenate([d_win_qkv, d_win_conv], axis=1)
    grad_x, dsh1, dsc1, dmix0 = _mm_w_ep([dp_qkv, dp_conv], w_in, _ep_norm_bwd(False), [xs, dx1], [mixn[0], sc1],
                                         [F32], [D] * 3, tb=True, tm=512, name="in_proj_dx_norm_bwd")
    _dctx, dcsh1, dcsc1, dmix0c = _norm_mod_bwd(da_ctx, ctxs, mixn[0], csc1, name="mix0_norm_bwd_ctx")

    z1k = jnp.zeros((1, D), F32)
    pack = jnp.concatenate(
        [v.reshape(-1) for v in (dsh1, dsc1, dg1, dsh2, dsc2, dg2, dsh1b, dsc1b, dg1b, dsh2b, dsc2b, dg2b,
                                 dcsh1, dcsc1, z1k, z1k, z1k, z1k,
                                 dmix0, dmix1, dmix0c, z1k, dffn0, dffn1, dqg_l, dkg_l + dkg_c,
                                 dfcb0, dfcb1, dconv_w, dfcw0, dfcw1, dpscale)])
    npack = pack.shape[0]
    pack = jnp.pad(pack, (0, (-npack) % 1024)).reshape(-1, 128)
    (g_pack,) = _all_gather([pack], name="gather_small_grads")
    gp = g_pack.reshape(NDEV, -1)
    off = [0]

    def take(size):
        seg = gp[:, off[0]:off[0] + size]
        off[0] += size
        return seg

    dmod_all = take(12 * D).reshape(NDEV, 2, 6 * D)
    dmodc_all = take(6 * D).reshape(NDEV, 1, 6 * D)
    dmix_all = take(4 * D).reshape(NDEV, 2, 2, D)
    dffn_all = take(2 * D).reshape(NDEV, 2, D)
    dqg_all = take(HD).reshape(NDEV, 1, HD)
    dkg_all = take(HD).reshape(NDEV, 1, HD)
    dfcb_all = take(2 * DFF).reshape(NDEV, 2, DFF)
    dconvw_all = take(3 * CW).reshape(NDEV, 3, CW)
    dfcw_all = take(6 * DFF).reshape(NDEV, 2, 3, DFF)
    dpscale_all = take(D).reshape(NDEV, D)

    dmodc_sum = dmodc_all[0]
    for dev in range(1, NDEV):
        dmodc_sum = dmodc_sum + dmodc_all[dev]
    my_cols = lambda a: lax.dynamic_slice_in_dim(a, me * acols, acols, axis=a.ndim - 1)
    rows0 = jnp.concatenate([my_cols(dmod_all[:, 0]), my_cols(dmodc_sum), jnp.zeros((7, acols), F32)], axis=0)
    rows1 = jnp.concatenate([my_cols(dmod_all[:, 1]), jnp.zeros((8, acols), F32)], axis=0)
    d_ada = jnp.stack([_mm(mraw, rows, ta=True, silu_a=True, name=f"ada_dw_{l}", tm=512, tn=256, tk=16)
                       for l, rows in enumerate((rows0, rows1))])
    dscc_part = _mm(rows0, ada_w[0], tb=True, name="ada_dcctx", tm=16, tn=512, tk=256)
    (g_dscc,) = _all_gather([dscc_part[8:16]], name="gather_dcctx")

    attn_shards = [col_shards([d_win]), d_wout.reshape(NDEV, D // NDEV, D)]
    attn_shards, g_dscc = lax.optimization_barrier((attn_shards, g_dscc))
    h_ga = _exchange_start(attn_shards, scatter=True, name="grads_attn_start")
    dmod_all = dmod_all + h_ga["token"][0, 0]

    outs = {}

    def put(nm, res):
        outs["grad_" + nm], outs["delta_" + nm], outs["new_m_" + nm], outs["new_v_" + nm] = res

    dmodc_pad = jnp.concatenate([dmodc_all, jnp.zeros_like(dmodc_all)], axis=1)
    put("ada_b", _adamw_nd(jnp.concatenate([dmod_all, dmodc_pad], axis=0), ada_b, m_ada_b, v_ada_b, name="adam_ada_b"))
    put("mix_norm", _adamw_nd(jnp.concatenate([dmix_all[:, 0], dmix_all[:, 1]], axis=0), mix_norm, m_mix_norm,
                              v_mix_norm, name="adam_mix_norm"))
    put("ffn_norm", _adamw_nd(dffn_all, ffn_norm, m_ffn_norm, v_ffn_norm, name="adam_ffn_norm"))
    put("even_q_gain", _adamw_nd(dqg_all, even_q_gain, m_even_q_gain, v_even_q_gain, name="adam_q_gain"))
    put("even_k_gain", _adamw_nd(dkg_all, even_k_gain, m_even_k_gain, v_even_k_gain, name="adam_k_gain"))
    put("ffn_conv_b", _adamw_nd(dfcb_all, ffn_conv_b, m_ffn_conv_b, v_ffn_conv_b, name="adam_ffn_conv_b"))
    my_convw = lax.dynamic_slice_in_dim(dconvw_all, me * ecw, ecw, axis=2)[:, None]
    put("even_conv_w", _adamw_nd(my_convw, even_conv_w, m_even_conv_w, v_even_conv_w, name="adam_even_conv_w"))
    my_fcw = lax.dynamic_slice_in_dim(dfcw_all, me * fcw, fcw, axis=3)
    put("ffn_conv_w", _adamw_nd(my_fcw, ffn_conv_w, m_ffn_conv_w, v_ffn_conv_w, name="adam_ffn_conv_w"))
    my_ps = lax.dynamic_slice_in_dim(dpscale_all, me * (D // NDEV), D // NDEV, axis=1)[:, None]
    put("odd_pool_scale", _adamw_nd(my_ps, odd_pool_scale, m_odd_pool_scale, v_odd_pool_scale, name="adam_pool_scale"))

    put("ada_w", _adamw_nd(d_ada[None], ada_w, m_ada_w, v_ada_w, name="adam_ada_w"))
    put("c_ctx", _adamw_nd(g_dscc[:, 0:1, :].reshape(NDEV, D), c_ctx, m_c_ctx, v_c_ctx, name="adam_c_ctx",
                           silu_grad_of=c_ctx))

    r_pool, r_up1, r_down1 = _exchange_wait(h_g1, outs["grad_ada_b"], name="grads1_wait")
    r_up0, r_down0 = _exchange_wait(h_g0, outs["grad_mix_norm"], name="grads0_wait")
    r_win, r_wout = _exchange_wait(h_ga, outs["grad_c_ctx"], name="grads_attn_wait")
    put("even_w_in", _adamw_nd(r_win[:, None], even_w_in, m_even_w_in, v_even_w_in, name="adam_w_in"))
    put("even_w_out", _adamw_nd(r_wout[:, None], even_w_out, m_even_w_out, v_even_w_out, name="adam_w_out"))
    put("odd_pool_w", _adamw_nd(r_pool[:, None], odd_pool_w, m_odd_pool_w, v_odd_pool_w, name="adam_pool_w"))
    put("ffn_w_up", _adamw_nd([r_up0, r_up1], ffn_w_up, m_ffn_w_up, v_ffn_w_up, name="adam_w_up"))
    put("ffn_w_down", _adamw_nd([r_down0, r_down1], ffn_w_down, m_ffn_w_down, v_ffn_w_down, name="adam_w_down"))

    names = ["c_ctx", "ada_w", "ada_b", "mix_norm", "ffn_norm", "even_w_in", "even_q_gain", "even_k_gain",
             "even_conv_w", "even_w_out", "odd_pool_w", "odd_pool_scale", "ffn_w_up", "ffn_conv_w", "ffn_conv_b",
             "ffn_w_down"]
    result = [loss, grad_x[None]]
    for kind in ("grad_", "delta_", "new_m_", "new_v_"):
        result += [outs[kind + nm] for nm in names]
    return tuple(result)
```

```python
import functools
import math

import jax
import jax.numpy as jnp
from jax import lax
from jax.experimental import pallas as pl
from jax.experimental.pallas import tpu as pltpu

F32 = jnp.float32
BF16 = jnp.bfloat16

D = 1024
HD = 128
NQ = 4
NKV = 2
AW = NQ * HD
CW = D - AW
DFF = 2816
GRID_W = 64
ROPE_THETA = 10000.0
POOL_WINDOWS = (2, 4, 8, 16)
PG = D // 4
EPS = 1e-6
NDEV = 8
HALO = 8
MESH = pl.DeviceIdType.MESH

ADAM_LR = 0.001
ADAM_B1 = 0.9
ADAM_B2 = 0.999
ADAM_EPS = 1e-08
ADAM_WD = 0.01
ADAM_STEP = 10


def _pick(dim, prefs):
    for p in prefs:
        if dim % p == 0:
            return p
    return dim


def _params(*sem):
    return pltpu.CompilerParams(dimension_semantics=sem)


_NT = (((1,), (1,)), ((), ()))
_TN = (((0,), (0,)), ((), ()))
_SCALE = HD ** -0.5
_QSCALE = _SCALE * math.log2(math.e)
_LN2 = math.log(2.0)


def _mm(a_list, b, *, name, ta=False, tb=False, out_dtype=F32, silu_a=False, bias=None, tm=None, tn=None, tk=None):
    if not isinstance(a_list, (list, tuple)):
        a_list = [a_list]
    na = len(a_list)
    assert not (ta and na > 1)
    if ta:
        kdim, m = a_list[0].shape
        ks = [kdim]
    else:
        m = a_list[0].shape[0]
        ks = [a.shape[1] for a in a_list]
        kdim = sum(ks)
    n = b.shape[0] if tb else b.shape[1]
    assert (b.shape[1] if tb else b.shape[0]) == kdim
    kunit = math.gcd(*ks) if na > 1 else kdim
    tm = min(tm, m) if tm else _pick(m, (512, 256, 128, 64, 32, 16, 8))
    tn = min(tn, n) if tn else _pick(n, (512, 256, 128))
    tk = min(tk, kunit) if tk else _pick(kunit, (1024, 768, 512, 256, 128))
    assert m % tm == 0 and n % tn == 0 and all(k % tk == 0 for k in ks)
    nks = [k // tk for k in ks]
    starts = [sum(nks[:i]) for i in range(na)]
    nk = sum(nks)
    has_bias = bias is not None

    def body(*refs):
        a_refs = refs[:na]
        b_ref = refs[na]
        bias_ref = refs[na + 1] if has_bias else None
        o_ref = refs[na + 1 + has_bias]
        acc = refs[-1]
        k = pl.program_id(2)

        @pl.when(k == 0)
        def _():
            acc[...] = jnp.zeros_like(acc)

        bv = b_ref[...].astype(BF16)
        dn = (((0 if ta else 1,), (1 if tb else 0,)), ((), ()))
        for idx in range(na):
            def step(idx=idx):
                av = a_refs[idx][...]
                if silu_a:
                    av = av * jax.nn.sigmoid(av)
                acc[...] += lax.dot_general(av.astype(BF16), bv, dn, preferred_element_type=F32)
            if na == 1:
                step()
            else:
                pl.when((k >= starts[idx]) & (k < starts[idx] + nks[idx]))(step)

        @pl.when(k == nk - 1)
        def _():
            r = acc[...]
            if has_bias:
                r = r + bias_ref[...]
            o_ref[...] = r.astype(o_ref.dtype)

    in_specs = []
    for idx in range(na):
        if ta:
            in_specs.append(pl.BlockSpec((tk, tm), lambda i, j, k: (k, i)))
        else:
            lo, cnt = starts[idx], nks[idx]
            in_specs.append(pl.BlockSpec((tm, tk), lambda i, j, k, lo=lo, cnt=cnt: (i, jnp.clip(k - lo, 0, cnt - 1))))
    if tb:
        in_specs.append(pl.BlockSpec((tn, tk), lambda i, j, k: (j, k)))
    else:
        in_specs.append(pl.BlockSpec((tk, tn), lambda i, j, k: (k, j)))
    args = list(a_list) + [b]
    if has_bias:
        in_specs.append(pl.BlockSpec((1, tn), lambda i, j, k: (0, j)))
        args.append(bias)
    return pl.pallas_call(
        body, grid=(m // tm, n // tn, nk), in_specs=in_specs,
        out_specs=pl.BlockSpec((tm, tn), lambda i, j, k: (i, j)),
        out_shape=jax.ShapeDtypeStruct((m, n), out_dtype),
        scratch_shapes=[pltpu.VMEM((tm, tn), F32)], name=name,
        compiler_params=_params("parallel", "parallel", "arbitrary"))(*args)


def _mm_w(a_list, w, *, name, tb=False, tm=256, out_dtype=F32):
    if not isinstance(a_list, (list, tuple)):
        a_list = [a_list]
    na = len(a_list)
    m = a_list[0].shape[0]
    ks = [a.shape[1] for a in a_list]
    offs = [sum(ks[:i]) for i in range(na)]
    n = w.shape[0] if tb else w.shape[1]
    assert (w.shape[1] if tb else w.shape[0]) == sum(ks)
    tm = min(tm, m)
    assert m % tm == 0

    def body(*refs):
        a_refs, w_ref, o_ref = refs[:na], refs[na], refs[na + 1]
        acc = None
        for idx in range(na):
            av = a_refs[idx][...].astype(BF16)
            if tb:
                part = lax.dot_general(av, w_ref[:, offs[idx]:offs[idx] + ks[idx]], _NT, preferred_element_type=F32)
            else:
                part = jnp.dot(av, w_ref[offs[idx]:offs[idx] + ks[idx], :], preferred_element_type=F32)
            acc = part if acc is None else acc + part
        o_ref[...] = acc.astype(o_ref.dtype)

    in_specs = [pl.BlockSpec((tm, k), lambda i: (i, 0)) for k in ks] + [pl.BlockSpec(w.shape, lambda i: (0, 0))]
    return pl.pallas_call(
        body, grid=(m // tm,), in_specs=in_specs, out_specs=pl.BlockSpec((tm, n), lambda i: (i, 0)),
        out_shape=jax.ShapeDtypeStruct((m, n), out_dtype), name=name, compiler_params=_params("parallel"))(*a_list, w)


def _mm_w_ep(a_list, w, epilogue, row_in, vec_in, out_dtypes, sum_widths, *, name, tb=False, tm=256, sub=256):
    if not isinstance(a_list, (list, tuple)):
        a_list = [a_list]
    na, nr, nv, no, ns = len(a_list), len(row_in), len(vec_in), len(out_dtypes), len(sum_widths)
    m = a_list[0].shape[0]
    ks = [a.shape[1] for a in a_list]
    offs = [sum(ks[:i]) for i in range(na)]
    n = w.shape[0] if tb else w.shape[1]
    assert (w.shape[1] if tb else w.shape[0]) == sum(ks)
    tm = min(tm, m)
    sub = min(sub, tm)
    assert m % tm == 0 and tm % sub == 0

    def body(*refs):
        a_refs, w_ref = refs[:na], refs[na]
        row_refs = refs[na + 1:na + 1 + nr]
        vec_refs = refs[na + 1 + nr:na + 1 + nr + nv]
        out_refs = refs[na + 1 + nr + nv:na + 1 + nr + nv + no]
        sum_refs = refs[na + 1 + nr + nv + no:]

        @pl.when(pl.program_id(0) == 0)
        def _():
            for s_ref in sum_refs:
                s_ref[...] = jnp.zeros_like(s_ref)

        vecs = [v[...] for v in vec_refs]
        for r0 in range(0, tm, sub):
            acc = None
            for idx in range(na):
                av = a_refs[idx][r0:r0 + sub, :].astype(BF16)
                if tb:
                    part = lax.dot_general(av, w_ref[:, offs[idx]:offs[idx] + ks[idx]], _NT, preferred_element_type=F32)
                else:
                    part = jnp.dot(av, w_ref[offs[idx]:offs[idx] + ks[idx], :], preferred_element_type=F32)
                acc = part if acc is None else acc + part
            outs, sums = epilogue(acc, [r[r0:r0 + sub, :] for r in row_refs], vecs)
            for o_ref, o in zip(out_refs, outs):
                o_ref[r0:r0 + sub, :] = o.astype(o_ref.dtype)
            for s_ref, s in zip(sum_refs, sums):
                s_ref[...] += s

    row = pl.BlockSpec((tm, n), lambda i: (i, 0))
    in_specs = ([pl.BlockSpec((tm, k), lambda i: (i, 0)) for k in ks] + [pl.BlockSpec(w.shape, lambda i: (0, 0))]
                + [row] * nr + [_vec(n)] * nv)
    return pl.pallas_call(
        body, grid=(m // tm,), in_specs=in_specs, out_specs=[row] * no + [_vec(sw) for sw in sum_widths],
        out_shape=[jax.ShapeDtypeStruct((m, n), dt) for dt in out_dtypes]
        + [jax.ShapeDtypeStruct((1, sw), F32) for sw in sum_widths],
        name=name, compiler_params=_params("arbitrary" if ns else "parallel"))(*a_list, w, *row_in, *vec_in)


def _ep_norm_bwd(has_gate):
    def ep(dav, rows, vecs):
        xv = rows[0]
        gain, scv = vecs[0], vecs[1]
        r = lax.rsqrt(jnp.mean(xv * xv, axis=-1, keepdims=True) + EPS)
        xh = xv * r
        nrm = xh * gain
        dn = dav * (1.0 + scv)
        dxh = dn * gain
        dx = r * (dxh - xh * jnp.mean(dxh * xh, axis=-1, keepdims=True)) + rows[1]
        outs, sums = [dx], [_colsum(dav), _colsum(dav * nrm), _colsum(dn * xh)]
        if has_gate:
            outs.append(dx * vecs[2])
            sums.append(_colsum(dx * rows[2]))
        return outs, sums
    return ep


def _ep_loss(d):
    def ep(zv, rows, vecs):
        xv, tv = rows
        gv = vecs[0]
        diff = (xv + gv * zv) - tv
        dx = diff * (1.0 / d)
        part = 0.5 * jnp.sum(jnp.mean(diff * diff, axis=-1, keepdims=True), axis=0, keepdims=True)
        return [dx, dx * gv], [jnp.broadcast_to(part, (1, 128)), _colsum(dx * zv)]
    return ep


def _ep_resid_norm(yv, rows, vecs):
    g, gain, scv, shv = vecs
    xv = rows[0] + g * yv
    r = lax.rsqrt(jnp.mean(xv * xv, axis=-1, keepdims=True) + EPS)
    return [yv, xv, ((xv * r) * gain) * (1.0 + scv) + shv], []


def _mm_tn(pairs, *, name, tk=1024, out_dtype=BF16):
    if not isinstance(pairs, list):
        pairs = [pairs]
    m, n = pairs[0][0].shape[1], pairs[0][1].shape[1]
    tks = [min(tk, a.shape[0]) for a, _ in pairs]
    nks = [a.shape[0] // t for (a, _), t in zip(pairs, tks)]
    assert all(a.shape[0] == b.shape[0] and a.shape[0] % t == 0 for (a, b), t in zip(pairs, tks))
    starts = [sum(nks[:i]) for i in range(len(pairs))]
    nk = sum(nks)

    def body(*refs):
        o_ref, acc = refs[-2], refs[-1]
        k = pl.program_id(0)

        @pl.when(k == 0)
        def _():
            acc[...] = jnp.zeros_like(acc)

        for idx in range(len(pairs)):
            a_ref, b_ref = refs[2 * idx], refs[2 * idx + 1]

            def step(a_ref=a_ref, b_ref=b_ref):
                acc[...] += lax.dot_general(a_ref[...], b_ref[...], _TN, preferred_element_type=F32)

            if len(pairs) == 1:
                step()
            else:
                pl.when((k >= starts[idx]) & (k < starts[idx] + nks[idx]))(step)

        @pl.when(k == nk - 1)
        def _():
            o_ref[...] = acc[...].astype(o_ref.dtype)

    in_specs, args = [], []
    for (a, b), t, lo, cnt in zip(pairs, tks, starts, nks):
        idx_map = lambda k, lo=lo, cnt=cnt: (jnp.clip(k - lo, 0, cnt - 1), 0)
        in_specs += [pl.BlockSpec((t, m), idx_map), pl.BlockSpec((t, n), idx_map)]
        args += [a, b]
    return pl.pallas_call(
        body, grid=(nk,), in_specs=in_specs, out_specs=pl.BlockSpec((m, n), lambda k: (0, 0)),
        out_shape=jax.ShapeDtypeStruct((m, n), out_dtype), scratch_shapes=[pltpu.VMEM((m, n), F32)], name=name,
        compiler_params=_params("arbitrary"))(*args)


def _vec(d, col=None):
    if col is None:
        return pl.BlockSpec((1, d), lambda i, *_: (0, 0))
    return pl.BlockSpec((1, d), col)


def _halo_specs(tm, width, nrows, colblk=0, row_off=0):
    r = tm // HALO
    off = row_off // HALO
    last = nrows // HALO - 1
    prev = pl.BlockSpec((HALO, width), lambda i, *_: (off + jnp.maximum(i * r - 1, 0), colblk))
    nxt = pl.BlockSpec((HALO, width), lambda i, *_: (off + jnp.minimum((i + 1) * r, last), colblk))
    return prev, nxt


def _ext(prev_ref, main_ref, next_ref, i, ni):
    p = jnp.where(i > 0, prev_ref[...], 0.0)
    n = jnp.where(i < ni - 1, next_ref[...], 0.0)
    return jnp.concatenate([p, main_ref[...], n], axis=0)


def _sh(ext, k, tm):
    if k == 0:
        return ext[HALO:HALO + tm]
    rows = ext.shape[0]
    return pltpu.roll(ext, (-k) % rows, axis=0)[HALO:HALO + tm]


def _roll_rows(v, k):
    rows = v.shape[0]
    return pltpu.roll(v, (-k) % rows, axis=0) if k % rows else v


def _conv3(ext, w_ref, tm):
    return _sh(ext, -1, tm) * w_ref[0:1, :] + _sh(ext, 0, tm) * w_ref[1:2, :] + _sh(ext, 1, tm) * w_ref[2:3, :]


def _colsum(v):
    return jnp.sum(v, axis=0, keepdims=True)


def _acc_out(ref, i, val):
    @pl.when(i == 0)
    def _():
        ref[...] = jnp.zeros_like(ref)

    ref[...] += val


def _sigmoid(v):
    return jax.nn.sigmoid(v)


def _norm_mod(x, gain, sc, sh, *, name, y=None, g=None, ymul=None, tm=512):
    n, d = x.shape
    tm = min(tm, n)
    has_res = y is not None
    has_mul = ymul is not None

    def body(*refs):
        it = iter(refs)
        x_ref = next(it)
        y_ref = next(it) if has_res else None
        g_ref = next(it) if has_res else None
        m_ref = next(it) if has_mul else None
        gain_ref, sc_ref, sh_ref = next(it), next(it), next(it)
        xo_ref = next(it) if has_res else None
        a_ref = next(it)
        xv = x_ref[...]
        if has_res:
            yv = y_ref[...]
            if has_mul:
                yv = yv * m_ref[...]
            xv = xv + g_ref[...] * yv
            xo_ref[...] = xv
        r = lax.rsqrt(jnp.mean(xv * xv, axis=-1, keepdims=True) + EPS)
        nrm = (xv * r) * gain_ref[...]
        a_ref[...] = (nrm * (1.0 + sc_ref[...]) + sh_ref[...]).astype(BF16)

    row = pl.BlockSpec((tm, d), lambda i: (i, 0))
    in_specs, args = [row], [x]
    if has_res:
        in_specs += [row, _vec(d)]
        args += [y, g]
    if has_mul:
        in_specs.append(_vec(d))
        args.append(ymul)
    in_specs += [_vec(d)] * 3
    args += [gain, sc, sh]
    out_specs, out_shape = [], []
    if has_res:
        out_specs.append(row)
        out_shape.append(jax.ShapeDtypeStruct((n, d), F32))
    out_specs.append(row)
    out_shape.append(jax.ShapeDtypeStruct((n, d), BF16))
    res = pl.pallas_call(body, grid=(n // tm,), in_specs=in_specs, out_specs=out_specs, out_shape=out_shape,
                         name=name, compiler_params=_params("parallel"))(*args)
    return res if has_res else res[0]


def _norm_mod_bwd(da, x, gain, sc, *, name, dres=None, gate_y=None, gate_g=None, tm=512):
    n, d = x.shape
    tm = min(tm, n)
    has_res = dres is not None
    has_gate = gate_y is not None

    def body(*refs):
        it = iter(refs)
        da_ref, x_ref = next(it), next(it)
        r_ref = next(it) if has_res else None
        y_ref = next(it) if has_gate else None
        g_ref = next(it) if has_gate else None
        gain_ref, sc_ref = next(it), next(it)
        dx_ref, dsh_ref, dsc_ref, dgn_ref = next(it), next(it), next(it), next(it)
        dy_ref = next(it) if has_gate else None
        dg_ref = next(it) if has_gate else None
        i = pl.program_id(0)
        xv = x_ref[...]
        dav = da_ref[...]
        r = lax.rsqrt(jnp.mean(xv * xv, axis=-1, keepdims=True) + EPS)
        xh = xv * r
        nrm = xh * gain_ref[...]
        dn = dav * (1.0 + sc_ref[...])
        dxh = dn * gain_ref[...]
        dx = r * (dxh - xh * jnp.mean(dxh * xh, axis=-1, keepdims=True))
        if has_res:
            dx = dx + r_ref[...]
        dx_ref[...] = dx
        _acc_out(dsh_ref, i, _colsum(dav))
        _acc_out(dsc_ref, i, _colsum(dav * nrm))
        _acc_out(dgn_ref, i, _colsum(dn * xh))
        if has_gate:
            dy_ref[...] = (dx * g_ref[...]).astype(BF16)
            _acc_out(dg_ref, i, _colsum(dx * y_ref[...]))

    row = pl.BlockSpec((tm, d), lambda i: (i, 0))
    in_specs, args = [row, row], [da, x]
    if has_res:
        in_specs.append(row)
        args.append(dres)
    if has_gate:
        in_specs += [row, _vec(d)]
        args += [gate_y, gate_g]
    in_specs += [_vec(d)] * 2
    args += [gain, sc]
    vec_shape = jax.ShapeDtypeStruct((1, d), F32)
    out_specs = [row, _vec(d), _vec(d), _vec(d)]
    out_shape = [jax.ShapeDtypeStruct((n, d), F32), vec_shape, vec_shape, vec_shape]
    if has_gate:
        out_specs += [row, _vec(d)]
        out_shape += [jax.ShapeDtypeStruct((n, d), BF16), vec_shape]
    return pl.pallas_call(
        body, grid=(n // tm,), in_specs=in_specs, out_specs=out_specs, out_shape=out_shape,
        name=name, compiler_params=_params("arbitrary"))(*args)


def _ffn_up_glu(f, w_up, cw, cb, *, name, tm=256, tc=256):
    n, d = f.shape
    tm = min(tm, n)
    ni = n // tm
    nc = DFF // tc
    halo = 16
    rows = tm + 2 * halo
    r = tm // halo
    last = n // halo - 1

    def body(f_ref, fp_ref, fn_ref, w_ref, cw_ref, cb_ref, u_ref, gc_ref, h_ref):
        i = pl.program_id(0)
        a = f_ref[...]
        aext = jnp.concatenate([jnp.where(i > 0, fp_ref[...], jnp.zeros_like(fp_ref[...])), a,
                                jnp.where(i < ni - 1, fn_ref[...], jnp.zeros_like(fn_ref[...]))], axis=0)
        for j in range(nc):
            cols = slice(j * tc, (j + 1) * tc)
            vcols = slice(DFF + j * tc, DFF + (j + 1) * tc)
            gext = lax.dot_general(aext, w_ref[cols, :], _NT, preferred_element_type=F32)
            val = lax.dot_general(a, w_ref[vcols, :], _NT, preferred_element_type=F32)
            gate = gext[halo:halo + tm]
            gc = (pltpu.roll(gext, 1, axis=0)[halo:halo + tm] * cw_ref[0:1, cols] + gate * cw_ref[1:2, cols]
                  + pltpu.roll(gext, rows - 1, axis=0)[halo:halo + tm] * cw_ref[2:3, cols]) + cb_ref[:, cols]
            u_ref[:, cols] = gate
            u_ref[:, vcols] = val
            gc_ref[:, cols] = gc
            h_ref[:, cols] = (gc * _sigmoid(gc) * val).astype(BF16)

    return pl.pallas_call(
        body, grid=(ni,),
        in_specs=[pl.BlockSpec((tm, d), lambda i: (i, 0)),
                  pl.BlockSpec((halo, d), lambda i: (jnp.maximum(i * r - 1, 0), 0)),
                  pl.BlockSpec((halo, d), lambda i: (jnp.minimum((i + 1) * r, last), 0)),
                  pl.BlockSpec(w_up.shape, lambda i: (0, 0)), pl.BlockSpec((3, DFF), lambda i: (0, 0)),
                  pl.BlockSpec((1, DFF), lambda i: (0, 0))],
        out_specs=[pl.BlockSpec((tm, 2 * DFF), lambda i: (i, 0)), pl.BlockSpec((tm, DFF), lambda i: (i, 0)),
                   pl.BlockSpec((tm, DFF), lambda i: (i, 0))],
        out_shape=[jax.ShapeDtypeStruct((n, 2 * DFF), F32), jax.ShapeDtypeStruct((n, DFF), F32),
                   jax.ShapeDtypeStruct((n, DFF), BF16)], name=name,
        compiler_params=_params("parallel"))(f, f, f, w_up, cw, cb)


def _ffn_down_glu_bwd(dz, w_down, u, gc, cw, *, name, tm=256, tc=256):
    n, d = dz.shape
    tm = min(tm, n)
    ni = n // tm
    nc = DFF // tc
    rows = tm + 2 * HALO

    def body(z_ref, zp_ref, zn_ref, w_ref, u_ref, vp_ref, vn_ref, c_ref, cp_ref, cn_ref, cw_ref,
             dg_ref, dv_ref, dcw_ref, dcb_ref):
        i = pl.program_id(0)

        @pl.when(i == 0)
        def _():
            dcw_ref[...] = jnp.zeros_like(dcw_ref)
            dcb_ref[...] = jnp.zeros_like(dcb_ref)

        zext = jnp.concatenate([jnp.where(i > 0, zp_ref[...], jnp.zeros_like(zp_ref[...])), z_ref[...],
                                jnp.where(i < ni - 1, zn_ref[...], jnp.zeros_like(zn_ref[...]))], axis=0)
        for j in range(nc):
            cols = slice(j * tc, (j + 1) * tc)
            vcols = slice(DFF + j * tc, DFF + (j + 1) * tc)
            dh = lax.dot_general(zext, w_ref[cols, :], _NT, preferred_element_type=F32)[HALO:HALO + rows]
            gcx = jnp.concatenate([cp_ref[:, cols], c_ref[:, cols], cn_ref[:, cols]], axis=0)
            vext = jnp.concatenate([vp_ref[:, cols], u_ref[:, vcols], vn_ref[:, cols]], axis=0)
            sg = _sigmoid(gcx)
            dgc = dh * vext * (sg * (1.0 + gcx * (1.0 - sg)))
            dv_ref[:, cols] = (dh[HALO:HALO + tm] * (gcx[HALO:HALO + tm] * sg[HALO:HALO + tm])).astype(BF16)
            d_next = pltpu.roll(dgc, rows - 1, axis=0)[HALO:HALO + tm]
            d_prev = pltpu.roll(dgc, 1, axis=0)[HALO:HALO + tm]
            d_here = dgc[HALO:HALO + tm]
            dg_ref[:, cols] = (d_next * cw_ref[0:1, cols] + d_here * cw_ref[1:2, cols]
                               + d_prev * cw_ref[2:3, cols]).astype(BF16)
            gate = u_ref[:, cols]
            dcw_ref[:, cols] += jnp.concatenate([_colsum(d_next * gate), _colsum(d_here * gate),
                                                 _colsum(d_prev * gate)], axis=0)
            dcb_ref[:, cols] += _colsum(d_here)

    def trio(width, halo, tile_width=None, colblk=0):
        r, last = tm // halo, n // halo - 1
        return [pl.BlockSpec((tm, tile_width or width), lambda i: (i, 0)),
                pl.BlockSpec((halo, width), lambda i: (jnp.maximum(i * r - 1, 0), colblk)),
                pl.BlockSpec((halo, width), lambda i: (jnp.minimum((i + 1) * r, last), colblk))]

    whole = lambda shape: pl.BlockSpec(shape, lambda i: (0, 0))
    return pl.pallas_call(
        body, grid=(ni,),
        in_specs=(trio(d, 16) + [whole(w_down.shape)] + trio(DFF, HALO, tile_width=2 * DFF, colblk=1)
                  + trio(DFF, HALO) + [whole((3, DFF))]),
        out_specs=[pl.BlockSpec((tm, DFF), lambda i: (i, 0)), pl.BlockSpec((tm, DFF), lambda i: (i, 0)),
                   whole((3, DFF)), whole((1, DFF))],
        out_shape=[jax.ShapeDtypeStruct((n, DFF), BF16), jax.ShapeDtypeStruct((n, DFF), BF16),
                   jax.ShapeDtypeStruct((3, DFF), F32), jax.ShapeDtypeStruct((1, DFF), F32)],
        name=name, compiler_params=_params("arbitrary"))(dz, dz, dz, w_down, u, u, u, gc, gc, gc, cw)


def _rope_tables(n):
    rows = n // GRID_W
    axis_dim = HD // 2
    inv_freq = jnp.power(ROPE_THETA, -jnp.arange(0, axis_dim, 2, dtype=F32) / axis_dim)
    ar = jnp.arange(rows, dtype=F32)[:, None] * inv_freq
    ac = jnp.arange(GRID_W, dtype=F32)[:, None] * inv_freq
    by_row = lambda a: jnp.repeat(a, GRID_W, axis=0)
    by_col = lambda a: jnp.tile(a, (rows, 1))
    cr, sr, cc, sc = by_row(jnp.cos(ar)), by_row(jnp.sin(ar)), by_col(jnp.cos(ac)), by_col(jnp.sin(ac))
    return jnp.concatenate([cr, cr, cc, cc], axis=1), jnp.concatenate([-sr, sr, -sc, sc], axis=1)


def _partner(v):
    lane = lax.broadcasted_iota(jnp.int32, v.shape, 1)
    return jnp.where((lane % 64) < 32, pltpu.roll(v, HD - 32, axis=1), pltpu.roll(v, 32, axis=1))


def _qkv_prep(p, q_gain, k_gain, cs, sn, *, name, has_q, kv_col, kv_rows=None, kv_row_off=0, kv_into=None, tm=256):
    n = p.shape[0]
    rope = cs is not None
    kv_rows = kv_rows or n
    rb = kv_row_off // tm

    def body(*refs):
        it = iter(refs)
        q_ref = next(it) if has_q else None
        kv_ref = next(it)
        qg_ref, kg_ref = next(it), next(it)
        cs_ref = next(it) if rope else None
        sn_ref = next(it) if rope else None
        if kv_into is not None:
            next(it), next(it)
        qo_ref = next(it) if has_q else None
        ko_ref, vo_ref = next(it), next(it)

        def norm_rope(xh, gain, mul=None):
            r = lax.rsqrt(jnp.mean(xh * xh, axis=-1, keepdims=True) + EPS)
            xn = (xh * r) * gain
            if rope:
                xn = xn * cs_ref[...] + _partner(xn) * sn_ref[...]
            if mul is not None:
                xn = xn * mul
            return xn.astype(BF16)

        if has_q:
            for h in range(NQ):
                qo_ref[h] = norm_rope(q_ref[:, h * HD:(h + 1) * HD], qg_ref[...], _QSCALE)
        for h in range(NKV):
            ko_ref[h] = norm_rope(kv_ref[:, h * HD:(h + 1) * HD], kg_ref[...])
            vo_ref[h] = kv_ref[:, (NKV + h) * HD:(NKV + h + 1) * HD].astype(BF16)

    in_specs, args = [], []
    if has_q:
        in_specs.append(pl.BlockSpec((tm, AW), lambda i: (i, 0)))
        args.append(p)
    in_specs += [pl.BlockSpec((tm, 2 * NKV * HD), lambda i: (i, kv_col)), _vec(HD), _vec(HD)]
    args += [p, q_gain, k_gain]
    if rope:
        in_specs += [pl.BlockSpec((tm, HD), lambda i: (i, 0))] * 2
        args += [cs, sn]
    out_specs, out_shape = [], []
    if has_q:
        out_specs.append(pl.BlockSpec((NQ, tm, HD), lambda i: (0, i, 0)))
        out_shape.append(jax.ShapeDtypeStruct((NQ, n, HD), BF16))
    out_specs += [pl.BlockSpec((NKV, tm, HD), lambda i: (0, rb + i, 0))] * 2
    out_shape += [jax.ShapeDtypeStruct((NKV, kv_rows, HD), BF16)] * 2
    aliases = {}
    if kv_into is not None:
        aliases = {len(args): int(has_q), len(args) + 1: int(has_q) + 1}
        in_specs += [pl.BlockSpec(memory_space=pl.ANY)] * 2
        args += list(kv_into)
    return pl.pallas_call(body, grid=(n // tm,), in_specs=in_specs, out_specs=out_specs, out_shape=out_shape,
                          input_output_aliases=aliases, name=name, compiler_params=_params("parallel"))(*args)


def _qkv_bwd(p, dq, dk, dv, q_gain, k_gain, cs, sn, *, name, has_q, kv_col, kv_row_off, tm=256):
    n = p.shape[0]
    rope = cs is not None
    rb = kv_row_off // tm

    def body(*refs):
        it = iter(refs)
        q_ref = next(it) if has_q else None
        kv_ref = next(it)
        dq_ref = next(it) if has_q else None
        dk_ref, dv_ref = next(it), next(it)
        qg_ref, kg_ref = next(it), next(it)
        cs_ref = next(it) if rope else None
        sn_ref = next(it) if rope else None
        dp_ref, dqg_ref, dkg_ref = next(it), next(it), next(it)
        i = pl.program_id(0)

        def back(xh, dout, gain):
            if rope:
                dout = dout * cs_ref[...] + _partner(dout * sn_ref[...])
            r = lax.rsqrt(jnp.mean(xh * xh, axis=-1, keepdims=True) + EPS)
            xhat = xh * r
            dxh = dout * gain
            dx = r * (dxh - xhat * jnp.mean(dxh * xhat, axis=-1, keepdims=True))
            return dx, _colsum(dout * xhat)

        dqg = jnp.zeros((1, HD), F32)
        dkg = jnp.zeros((1, HD), F32)
        if has_q:
            for h in range(NQ):
                dx, dg = back(q_ref[:, h * HD:(h + 1) * HD], dq_ref[h], qg_ref[...])
                dp_ref[:, h * HD:(h + 1) * HD] = dx.astype(BF16)
                dqg = dqg + dg
        else:
            dp_ref[:, 0:AW] = jnp.zeros((tm, AW), BF16)
        for h in range(NKV):
            dx, dg = back(kv_ref[:, h * HD:(h + 1) * HD], dk_ref[h], kg_ref[...])
            dp_ref[:, AW + h * HD:AW + (h + 1) * HD] = dx.astype(BF16)
            dkg = dkg + dg
            dp_ref[:, AW + (NKV + h) * HD:AW + (NKV + h + 1) * HD] = dv_ref[h].astype(BF16)
        _acc_out(dqg_ref, i, dqg)
        _acc_out(dkg_ref, i, dkg)

    in_specs, args = [], []
    if has_q:
        in_specs.append(pl.BlockSpec((tm, AW), lambda i: (i, 0)))
        args.append(p)
    in_specs.append(pl.BlockSpec((tm, 2 * NKV * HD), lambda i: (i, kv_col)))
    args.append(p)
    if has_q:
        in_specs.append(pl.BlockSpec((NQ, tm, HD), lambda i: (0, i, 0)))
        args.append(dq)
    in_specs += [pl.BlockSpec((NKV, tm, HD), lambda i: (0, rb + i, 0))] * 2 + [_vec(HD), _vec(HD)]
    args += [dk, dv, q_gain, k_gain]
    if rope:
        in_specs += [pl.BlockSpec((tm, HD), lambda i: (i, 0))] * 2
        args += [cs, sn]
    return pl.pallas_call(
        body, grid=(n // tm,), in_specs=in_specs,
        out_specs=[pl.BlockSpec((tm, D), lambda i: (i, 0)), _vec(HD), _vec(HD)],
        out_shape=[jax.ShapeDtypeStruct((n, D), BF16), jax.ShapeDtypeStruct((1, HD), F32),
                   jax.ShapeDtypeStruct((1, HD), F32)],
        name=name, compiler_params=_params("arbitrary"))(*args)


def _conv_gate_fwd(p, o, conv_w, *, name, tm=256):
    n = p.shape[0]
    ni = n // tm

    def body(gb_ref, gc_ref, gcp_ref, gcn_ref, xi_ref, xip_ref, xin_ref, o_ref, w_ref, cat_ref):
        i = pl.program_id(0)
        hext = _ext(gcp_ref, gc_ref, gcn_ref, i, ni) * _ext(xip_ref, xi_ref, xin_ref, i, ni)
        cat_ref[:, 0:AW] = o_ref[...].astype(BF16)
        cat_ref[:, AW:D] = (gb_ref[...] * _conv3(hext, w_ref, tm)).astype(BF16)

    gcp, gcn = _halo_specs(tm, CW, n, colblk=3)
    xip, xin = _halo_specs(tm, CW, n, colblk=4)
    return pl.pallas_call(
        body, grid=(ni,),
        in_specs=[pl.BlockSpec((tm, CW), lambda i: (i, 2)), pl.BlockSpec((tm, CW), lambda i: (i, 3)), gcp, gcn,
                  pl.BlockSpec((tm, CW), lambda i: (i, 4)), xip, xin, pl.BlockSpec((tm, AW), lambda i: (i, 0)),
                  pl.BlockSpec((3, CW), lambda i: (0, 0))],
        out_specs=pl.BlockSpec((tm, D), lambda i: (i, 0)), out_shape=jax.ShapeDtypeStruct((n, D), BF16),
        name=name, compiler_params=_params("parallel"))(p, p, p, p, p, p, p, o, conv_w)


def _conv_gate_bwd(dcat, p, conv_w, *, name, tm=256):
    n = p.shape[0]
    ni = n // tm

    def body(dc_ref, dcp_ref, dcn_ref, gb_ref, gbp_ref, gbn_ref, gc_ref, gcp_ref, gcn_ref, xi_ref, xip_ref, xin_ref,
             w_ref, dp_ref, dw_ref):
        i = pl.program_id(0)
        gcext = _ext(gcp_ref, gc_ref, gcn_ref, i, ni)
        xiext = _ext(xip_ref, xi_ref, xin_ref, i, ni)
        hext = gcext * xiext
        dcv = _ext(dcp_ref, dc_ref, dcn_ref, i, ni) * _ext(gbp_ref, gb_ref, gbn_ref, i, ni)
        dp_ref[:, 0:CW] = (dc_ref[...] * _conv3(hext, w_ref, tm)).astype(BF16)
        dh = _sh(dcv, 1, tm) * w_ref[0:1, :] + _sh(dcv, 0, tm) * w_ref[1:2, :] + _sh(dcv, -1, tm) * w_ref[2:3, :]
        dp_ref[:, CW:2 * CW] = (dh * xi_ref[...]).astype(BF16)
        dp_ref[:, 2 * CW:3 * CW] = (dh * gc_ref[...]).astype(BF16)
        dcv_t = dcv[HALO:HALO + tm]
        dw = jnp.concatenate([_colsum(dcv_t * _sh(hext, -1, tm)), _colsum(dcv_t * _sh(hext, 0, tm)),
                              _colsum(dcv_t * _sh(hext, 1, tm))], axis=0)
        _acc_out(dw_ref, i, dw)

    def trio(colblk):
        prev, nxt = _halo_specs(tm, CW, n, colblk=colblk)
        return [pl.BlockSpec((tm, CW), lambda i: (i, colblk)), prev, nxt]

    return pl.pallas_call(
        body, grid=(ni,), in_specs=trio(1) + trio(2) + trio(3) + trio(4) + [pl.BlockSpec((3, CW), lambda i: (0, 0))],
        out_specs=[pl.BlockSpec((tm, 3 * CW), lambda i: (i, 0)), pl.BlockSpec((3, CW), lambda i: (0, 0))],
        out_shape=[jax.ShapeDtypeStruct((n, 3 * CW), BF16), jax.ShapeDtypeStruct((3, CW), F32)],
        name=name, compiler_params=_params("arbitrary"))(dcat, dcat, dcat, p, p, p, p, p, p, p, p, p, conv_w)


def _attn_fwd(q, k, v, *, name, bq=512, sub=256):
    n = q.shape[1]
    t = k.shape[1]
    bq = min(bq, n)
    sub = min(sub, 2 * bq)

    def body(q_ref, k_ref, v_ref, o_ref, lse_ref):
        q2 = q_ref[...].reshape(2 * bq, HD)
        outs, lses = [], []
        for r0 in range(0, 2 * bq, sub):
            s = lax.dot_general(q2[r0:r0 + sub], k_ref[0], _NT, preferred_element_type=F32)
            m = jnp.max(s, axis=-1, keepdims=True)
            pv = jnp.exp2(s - m)
            l = jnp.sum(pv, axis=-1, keepdims=True)
            outs.append(jnp.dot(pv.astype(BF16), v_ref[0], preferred_element_type=F32) / l)
            lses.append(m + jnp.log2(l))
        out = jnp.concatenate(outs, axis=0)
        o_ref[:, 0:HD] = out[0:bq]
        o_ref[:, HD:2 * HD] = out[bq:2 * bq]
        lse_ref[...] = jnp.concatenate(lses, axis=0).reshape(2, bq, 1)

    kspec = pl.BlockSpec((1, t, HD), lambda h, i: (h, 0, 0))
    return pl.pallas_call(
        body, grid=(NKV, n // bq),
        in_specs=[pl.BlockSpec((2, bq, HD), lambda h, i: (h, i, 0)), kspec, kspec],
        out_specs=[pl.BlockSpec((bq, 2 * HD), lambda h, i: (i, h)), pl.BlockSpec((2, bq, 1), lambda h, i: (h, i, 0))],
        out_shape=[jax.ShapeDtypeStruct((n, AW), F32), jax.ShapeDtypeStruct((NQ, n, 1), F32)],
        name=name, compiler_params=_params("parallel", "parallel"))(q, k, v)


def _attn_bwd(q, k, v, dcat, o, lse, *, name, bq=256):
    n = q.shape[1]
    t = k.shape[1]
    bq = min(bq, n)

    def body(q_ref, k_ref, v_ref, dc_ref, o_ref, lse_ref, dq_ref, dk_ref, dv_ref):
        @pl.when(pl.program_id(1) == 0)
        def _():
            dk_ref[...] = jnp.zeros_like(dk_ref)
            dv_ref[...] = jnp.zeros_like(dv_ref)

        q2 = q_ref[...].reshape(2 * bq, HD)
        do_f = jnp.concatenate([dc_ref[:, 0:HD], dc_ref[:, HD:2 * HD]], axis=0)
        o_f = jnp.concatenate([o_ref[:, 0:HD], o_ref[:, HD:2 * HD]], axis=0)
        delta = jnp.sum(do_f * o_f, axis=-1, keepdims=True)
        do2 = do_f.astype(BF16)
        s = lax.dot_general(q2, k_ref[0], _NT, preferred_element_type=F32)
        pv = jnp.exp2(s - lse_ref[...].reshape(2 * bq, 1))
        dp = lax.dot_general(do2, v_ref[0], _NT, preferred_element_type=F32)
        ds = (pv * (dp - delta)).astype(BF16)
        dq_ref[...] = (jnp.dot(ds, k_ref[0], preferred_element_type=F32) * _SCALE).reshape(2, bq, HD)
        dk_ref[0] += lax.dot_general(ds, q2, _TN, preferred_element_type=F32) * _LN2
        dv_ref[0] += lax.dot_general(pv.astype(BF16), do2, _TN, preferred_element_type=F32)

    qspec = pl.BlockSpec((2, bq, HD), lambda h, i: (h, i, 0))
    kspec = pl.BlockSpec((1, t, HD), lambda h, i: (h, 0, 0))
    sspec = pl.BlockSpec((2, bq, 1), lambda h, i: (h, i, 0))
    cspec = pl.BlockSpec((bq, 2 * HD), lambda h, i: (i, h))
    return pl.pallas_call(
        body, grid=(NKV, n // bq), in_specs=[qspec, kspec, kspec, cspec, cspec, sspec], out_specs=[qspec, kspec, kspec],
        out_shape=[jax.ShapeDtypeStruct((NQ, n, HD), F32), jax.ShapeDtypeStruct((NKV, t, HD), F32),
                   jax.ShapeDtypeStruct((NKV, t, HD), F32)],
        name=name, compiler_params=_params("parallel", "arbitrary"))(q, k, v, dcat, o, lse)


def _window_sums(ext, w):
    s, step = ext, 1
    while step < w:
        s = s + _roll_rows(s, step)
        step *= 2
    return s


def _pool_counts(i, tm, n, w, rows, first):
    t = i * tm - HALO + first + lax.broadcasted_iota(jnp.int32, (rows, 1), 0)
    lo = jnp.clip(t - w // 2, 0, n)
    hi = jnp.clip(t + w - w // 2, 0, n)
    return jnp.maximum(hi - lo, 1).astype(F32)


def _norm_mod_ext(xext, gain_ref, sc_ref, sh_ref, i, tm, n):
    rows = xext.shape[0]
    t = i * tm - HALO + lax.broadcasted_iota(jnp.int32, (rows, 1), 0)
    inside = (t >= 0) & (t < n)
    r = lax.rsqrt(jnp.mean(xext * xext, axis=-1, keepdims=True) + EPS)
    xh = xext * r
    a = (xh * gain_ref[...]) * (1.0 + sc_ref[...]) + sh_ref[...]
    return jnp.where(inside, a, 0.0), r, xh


def _pool_fwd(x, y, g, gain, sc, sh, pool_w, *, name, tm=256):
    n, d = x.shape
    ni = n // tm

    def body(x_ref, xp_ref, xn_ref, y_ref, yp_ref, yn_ref, g_ref, gain_ref, sc_ref, sh_ref, w_ref, xo_ref, o_ref):
        i = pl.program_id(0)
        xext = _ext(xp_ref, x_ref, xn_ref, i, ni) + g_ref[...] * _ext(yp_ref, y_ref, yn_ref, i, ni)
        xo_ref[...] = xext[HALO:HALO + tm]
        aext, _, _ = _norm_mod_ext(xext, gain_ref, sc_ref, sh_ref, i, tm, n)
        for gi, w in enumerate(POOL_WINDOWS):
            ag = aext[:, gi * PG:(gi + 1) * PG]
            mean = _sh(_window_sums(ag, w), -(w // 2), tm) / _pool_counts(i, tm, n, w, tm, HALO)
            pooled = mean - ag[HALO:HALO + tm]
            o_ref[:, gi * PG:(gi + 1) * PG] = jnp.dot(pooled.astype(BF16), w_ref[gi], preferred_element_type=F32)

    row = pl.BlockSpec((tm, d), lambda i: (i, 0))
    prev, nxt = _halo_specs(tm, d, n)
    return pl.pallas_call(
        body, grid=(ni,),
        in_specs=[row, prev, nxt, row, prev, nxt, _vec(d), _vec(d), _vec(d), _vec(d),
                  pl.BlockSpec((4, PG, PG), lambda i: (0, 0, 0))],
        out_specs=[row, row], out_shape=[jax.ShapeDtypeStruct((n, d), F32)] * 2,
        name=name, compiler_params=_params("parallel"))(x, x, x, y, y, y, g, gain, sc, sh, pool_w)


def _pool_bwd(dxo, mixed, x, g, scale, gain, sc, sh, pool_w, zprev, gprev, *, name, tm=256):
    n, d = x.shape
    ni = n // tm

    def body(dx_ref, dxp_ref, dxn_ref, mx_ref, x_ref, xp_ref, xn_ref, g_ref, s_ref, gain_ref, sc_ref, sh_ref, w_ref,
             zp_ref, gp_ref, dxi_ref, dw_ref, dg_ref, dsl_ref, dsh_ref, dsc_ref, dgn_ref, dzp_ref, dgp_ref):
        i = pl.program_id(0)

        @pl.when(i == 0)
        def _():
            dw_ref[...] = jnp.zeros_like(dw_ref)

        dxo_t = dx_ref[...]
        mixed_t = mx_ref[...]
        dy_t = dxo_t * g_ref[...]
        _acc_out(dg_ref, i, _colsum(dxo_t * (mixed_t * s_ref[...])))
        _acc_out(dsl_ref, i, _colsum(dy_t * mixed_t))
        dmixed = (_ext(dxp_ref, dx_ref, dxn_ref, i, ni) * g_ref[...]) * s_ref[...]
        xext = _ext(xp_ref, x_ref, xn_ref, i, ni)
        aext, rext, xhext = _norm_mod_ext(xext, gain_ref, sc_ref, sh_ref, i, tm, n)
        rows = tm + 2 * HALO
        da_parts = []
        for gi, w in enumerate(POOL_WINDOWS):
            sl = slice(gi * PG, (gi + 1) * PG)
            ag = aext[:, sl]
            mean = _sh(_window_sums(ag, w), -(w // 2), tm) / _pool_counts(i, tm, n, w, tm, HALO)
            pooled = (mean - ag[HALO:HALO + tm]).astype(BF16)
            dmg = dmixed[:, sl].astype(BF16)
            dw_ref[gi] += lax.dot_general(pooled, dmixed[HALO:HALO + tm, sl].astype(BF16), _TN,
                                          preferred_element_type=F32)
            dpl = lax.dot_general(dmg, w_ref[gi], _NT, preferred_element_type=F32)
            e = dpl / _pool_counts(i, tm, n, w, rows, 0)
            da_parts.append(_sh(_window_sums(e, w), 1 - w // 2, tm) - dpl[HALO:HALO + tm])
        da = jnp.concatenate(da_parts, axis=1)
        r = rext[HALO:HALO + tm]
        xh = xhext[HALO:HALO + tm]
        nrm = xh * gain_ref[...]
        dn = da * (1.0 + sc_ref[...])
        dxh = dn * gain_ref[...]
        dxi = dxo_t + r * (dxh - xh * jnp.mean(dxh * xh, axis=-1, keepdims=True))
        dxi_ref[...] = dxi
        _acc_out(dsh_ref, i, _colsum(da))
        _acc_out(dsc_ref, i, _colsum(da * nrm))
        _acc_out(dgn_ref, i, _colsum(dn * xh))
        dzp_ref[...] = (dxi * gp_ref[...]).astype(BF16)
        _acc_out(dgp_ref, i, _colsum(dxi * zp_ref[...]))

    row = pl.BlockSpec((tm, d), lambda i: (i, 0))
    prev, nxt = _halo_specs(tm, d, n)
    wspec = pl.BlockSpec((4, PG, PG), lambda i: (0, 0, 0))
    vshape = jax.ShapeDtypeStruct((1, d), F32)
    return pl.pallas_call(
        body, grid=(ni,),
        in_specs=[row, prev, nxt, row, row, prev, nxt] + [_vec(d)] * 5 + [wspec, row, _vec(d)],
        out_specs=[row, wspec] + [_vec(d)] * 5 + [row, _vec(d)],
        out_shape=[jax.ShapeDtypeStruct((n, d), F32), jax.ShapeDtypeStruct((4, PG, PG), F32)] + [vshape] * 5
        + [jax.ShapeDtypeStruct((n, d), BF16), vshape],
        name=name, compiler_params=_params("arbitrary"))(dxo, dxo, dxo, mixed, x, x, x, g, scale, gain, sc, sh, pool_w,
                                                         zprev, gprev)


def _adamw(gparts_list, w, m, v, *, name, silu_grad_of=None):
    nl = len(gparts_list)
    nparts, r, c = gparts_list[0].shape
    tr = _pick(r, (256, 128, 64, 32, 16, 8))
    has_c = silu_grad_of is not None

    def body(*refs):
        gp_refs = refs[:nl]
        it = iter(refs[nl:])
        w_ref, m_ref, v_ref = next(it), next(it), next(it)
        c_ref = next(it) if has_c else None
        g_ref, d_ref, mo_ref, vo_ref = next(it), next(it), next(it), next(it)
        layer = pl.program_id(0)

        def update(gp_ref):
            g = gp_ref[0].astype(F32)
            for p in range(1, nparts):
                g = g + gp_ref[p].astype(F32)
            if has_c:
                cv = c_ref[0]
                sg = _sigmoid(cv)
                g = g * (sg * (1.0 + cv * (1.0 - sg)))
            g_ref[0] = g
            mn = ADAM_B1 * m_ref[0] + (1.0 - ADAM_B1) * g
            vn = ADAM_B2 * v_ref[0] + (1.0 - ADAM_B2) * (g * g)
            m_hat = mn / (1.0 - ADAM_B1 ** ADAM_STEP)
            v_hat = vn / (1.0 - ADAM_B2 ** ADAM_STEP)
            d_ref[0] = -ADAM_LR * (m_hat / (jnp.sqrt(v_hat) + ADAM_EPS) + ADAM_WD * w_ref[0])
            mo_ref[0] = mn
            vo_ref[0] = vn

        if nl == 1:
            update(gp_refs[0])
        else:
            for li in range(nl):
                pl.when(layer == li)(functools.partial(update, gp_refs[li]))

    row = pl.BlockSpec((1, tr, c), lambda l, i: (l, i, 0))
    in_specs = [pl.BlockSpec((nparts, tr, c), lambda l, i, li=li: (0, jnp.where(l == li, i, 0), 0)) for li in range(nl)]
    in_specs += [row, row, row]
    args = list(gparts_list) + [w, m, v]
    if has_c:
        in_specs.append(row)
        args.append(silu_grad_of)
    return pl.pallas_call(
        body, grid=(nl, r // tr), in_specs=in_specs, out_specs=[row] * 4,
        out_shape=[jax.ShapeDtypeStruct((nl, r, c), F32)] * 4, name=name,
        compiler_params=_params("arbitrary", "arbitrary"))(*args)


def _adamw_nd(gparts, w, m, v, *, name, silu_grad_of=None):
    shape = w.shape
    c = shape[-1]
    if isinstance(gparts, (list, tuple)):
        nl = len(gparts)
        r = math.prod(shape[1:-1])
    else:
        nl = 1
        r = math.prod(shape[:-1]) if len(shape) > 1 else 1
        gparts = [gparts]
    rs = lambda a: a.reshape(nl, r, c)
    res = _adamw([gp.reshape(gp.shape[0], r, c) for gp in gparts], rs(w), rs(m), rs(v), name=name,
                 silu_grad_of=None if silu_grad_of is None else rs(silu_grad_of))
    return [a.reshape(shape) for a in res]


def _place():
    return lax.axis_index("x"), lax.axis_index("y"), lax.axis_index("c")


def _all_gather(arrs, *, name):
    k_arr = len(arrs)

    def body(*refs):
        ins = refs[:k_arr]
        outs = refs[k_arr:2 * k_arr]
        send_sems, recv_sems, local_sems = refs[2 * k_arr:]
        x, y, c = _place()
        me, sibling = (x, y, c), (x, y, 1 - c)
        chips = [(1 - x, y), (x, 1 - y), (1 - x, 1 - y)]

        def slot(a, px, py, pc):
            return outs[a].at[4 * px + 2 * py + pc]

        def copy(a, s, block, to, src=None):
            return pltpu.make_async_remote_copy(
                src_ref=slot(a, *block) if src is None else src, dst_ref=slot(a, *block),
                send_sem=send_sems.at[a, s], recv_sem=recv_sems.at[a, s], device_id=to, device_id_type=MESH)

        mine = [pltpu.make_async_copy(ins[a], slot(a, *me), local_sems.at[a]) for a in range(k_arr)]
        for cp in mine:
            cp.start()
        first = []
        for a in range(k_arr):
            first.append(copy(a, 0, me, sibling, src=ins[a]))
            first += [copy(a, 1 + j, me, (*chip, c), src=ins[a]) for j, chip in enumerate(chips)]
        for cp in first:
            cp.start()
        passed = []
        for j, chip in enumerate(chips):
            for a in range(k_arr):
                copy(a, 1 + j, (*chip, c), me).wait_recv()
                fw = copy(a, 4 + j, (*chip, c), sibling)
                fw.start()
                passed.append(fw)
        for a in range(k_arr):
            copy(a, 0, sibling, me).wait_recv()
            for j, chip in enumerate(chips):
                copy(a, 4 + j, (*chip, 1 - c), me).wait_recv()
        for cp in first + passed:
            cp.wait_send()
        for cp in mine:
            cp.wait()

    any_spec = pl.BlockSpec(memory_space=pl.ANY)
    return pl.pallas_call(
        body, in_specs=[any_spec] * k_arr, out_specs=[any_spec] * k_arr,
        out_shape=[jax.ShapeDtypeStruct((NDEV,) + a.shape, a.dtype) for a in arrs],
        scratch_shapes=[pltpu.SemaphoreType.DMA((k_arr, 7)), pltpu.SemaphoreType.DMA((k_arr, 7)),
                        pltpu.SemaphoreType.DMA((k_arr,))],
        name=name)(*arrs)


_HBM = pl.BlockSpec(memory_space=pltpu.HBM)
_SEM = pl.BlockSpec(memory_space=pltpu.SEMAPHORE)
_EFFECT = pltpu.SideEffectType.DATAFLOW_SIDE_EFFECTING


def _peers(x, y, c):
    return [(x ^ (rel >> 2), y ^ ((rel >> 1) & 1), c ^ (rel & 1)) for rel in range(1, NDEV)]


def _exchange_copies(srcs, lands, send_sems, recv_sems, scatter):
    x, y, c = _place()
    me = 4 * x + 2 * y + c
    copies = []
    for r, (px, py, pc) in enumerate(_peers(x, y, c)):
        peer = 4 * px + 2 * py + pc
        for a in range(len(srcs)):
            copies.append(pltpu.make_async_remote_copy(
                src_ref=srcs[a].at[peer] if scatter else srcs[a], dst_ref=lands[a].at[me],
                send_sem=send_sems.at[7 * a + r], recv_sem=recv_sems.at[7 * a + r], device_id=(px, py, pc),
                device_id_type=MESH))
    return copies


def _exchange_start(arrs, *, scatter, name):
    k_arr = len(arrs)
    land_shapes = [a.shape if scatter else (NDEV,) + a.shape for a in arrs]
    lands = [pltpu.with_memory_space_constraint(lax.empty(s, a.dtype), pltpu.HBM) for s, a in zip(land_shapes, arrs)]
    srcs = [pltpu.with_memory_space_constraint(a, pltpu.HBM) for a in arrs]

    def body(*refs):
        src_refs, land_refs = refs[:k_arr], refs[k_arr:2 * k_arr]
        send_sems, recv_sems = refs[2 * k_arr], refs[2 * k_arr + 1]
        token = refs[-1]
        for cp in _exchange_copies(src_refs, land_refs, send_sems, recv_sems, scatter):
            cp.start()
        token[...] = jnp.zeros_like(token)

    out_shape = ([pltpu.SemaphoreType.DMA((7 * k_arr,)), pltpu.SemaphoreType.DMA((7 * k_arr,))]
                 + [pltpu.HBM(a.shape, a.dtype) for a in arrs] + [pltpu.HBM(s, a.dtype) for s, a in zip(land_shapes, arrs)]
                 + [jax.ShapeDtypeStruct((8, 128), F32)])
    res = pl.pallas_call(
        body, name=name, out_shape=out_shape, in_specs=[_HBM] * (2 * k_arr),
        out_specs=[_SEM, _SEM] + [_HBM] * (2 * k_arr) + [pl.BlockSpec(memory_space=pltpu.VMEM)],
        input_output_aliases={i: 2 + i for i in range(2 * k_arr)},
        compiler_params=pltpu.CompilerParams(has_side_effects=_EFFECT))(*srcs, *lands)
    return dict(send=res[0], recv=res[1], srcs=list(res[2:2 + k_arr]), lands=list(res[2 + k_arr:2 + 2 * k_arr]),
                token=res[-1], scatter=scatter)


def _exchange_wait(handle, after, *, name):
    k_arr = len(handle["srcs"])
    scatter = handle["scatter"]

    def body(*refs):
        src_refs, land_refs = refs[:k_arr], refs[k_arr:2 * k_arr]
        send_sems, recv_sems = refs[2 * k_arr], refs[2 * k_arr + 1]
        x, y, c = _place()
        me = 4 * x + 2 * y + c
        for r, (px, py, pc) in enumerate(_peers(x, y, c)):
            peer = 4 * px + 2 * py + pc
            for a in range(k_arr):
                cp = pltpu.make_async_remote_copy(
                    src_ref=src_refs[a].at[peer] if scatter else src_refs[a], dst_ref=land_refs[a].at[peer],
                    send_sem=send_sems.at[7 * a + r], recv_sem=recv_sems.at[7 * a + r], device_id=(x, y, c),
                    device_id_type=MESH)
                cp.wait_send()
                cp.wait_recv()

    arrs = handle["srcs"] + handle["lands"]
    res = pl.pallas_call(
        body, name=name, out_shape=[pltpu.HBM(a.shape, a.dtype) for a in arrs],
        in_specs=[_HBM] * (2 * k_arr) + [_SEM, _SEM, pl.BlockSpec(memory_space=pl.ANY)],
        out_specs=[_HBM] * (2 * k_arr), input_output_aliases={i: i for i in range(2 * k_arr)},
        compiler_params=pltpu.CompilerParams(has_side_effects=_EFFECT))(*arrs, handle["send"], handle["recv"], after)
    me = 4 * lax.axis_index("x") + 2 * lax.axis_index("y") + lax.axis_index("c")
    out = []
    for src, land in zip(res[:k_arr], res[k_arr:]):
        own = lax.dynamic_index_in_dim(src, me, 0, keepdims=False) if scatter else src
        out.append(lax.dynamic_update_index_in_dim(land, own, me, 0))
    return out


def _ffn_bwd(dxo, dz, xr, f, u_gc, hmid, gain, sc, w_up, cw, w_down, tag, gate_y=None, gate_g=None):
    d_wdown = _mm_tn((hmid, dz), name=f"ffn_down_dw_{tag}")
    dug, duv, dcw, dcb = _ffn_down_glu_bwd(dz, w_down, u_gc[0], u_gc[1], cw, name=f"ffn_down_glu_bwd_{tag}")
    d_wup_g = _mm_tn((dug, f), name=f"ffn_up_dwg_{tag}")
    d_wup_v = _mm_tn((duv, f), name=f"ffn_up_dwv_{tag}")
    gated = gate_y is not None
    res = _mm_w_ep([dug, duv], w_up, _ep_norm_bwd(gated), [xr, dxo] + ([gate_y] if gated else []),
                   [gain, sc] + ([gate_g] if gated else []), [F32] + ([BF16] if gated else []),
                   [D] * (4 if gated else 3), name=f"ffn_up_dx_norm_bwd_{tag}")
    n_out = 2 if gated else 1
    return res[:n_out], res[n_out:], (d_wup_g, d_wup_v, d_wdown, dcw, dcb)


def _split6(mod):
    return [mod[j * D:(j + 1) * D][None, :] for j in range(6)]


def _row(v):
    return v.reshape(1, -1)


def kernel(x, c, ctx, c_ctx, ada_w, ada_b, mix_norm, ffn_norm, even_w_in, even_q_gain, even_k_gain, even_conv_w, even_w_out, odd_pool_w, odd_pool_scale, ffn_w_up, ffn_conv_w, ffn_conv_b, ffn_w_down, loss_target, m_c_ctx, m_ada_w, m_ada_b, m_mix_norm, m_ffn_norm, m_even_w_in, m_even_q_gain, m_even_k_gain, m_even_conv_w, m_even_w_out, m_odd_pool_w, m_odd_pool_scale, m_ffn_w_up, m_ffn_conv_w, m_ffn_conv_b, m_ffn_w_down, v_c_ctx, v_ada_w, v_ada_b, v_mix_norm, v_ffn_norm, v_even_w_in, v_even_q_gain, v_even_k_gain, v_even_conv_w, v_even_w_out, v_odd_pool_w, v_odd_pool_scale, v_ffn_w_up, v_ffn_conv_w, v_ffn_conv_b, v_ffn_w_down):
    n = x.shape[1]
    lc = ctx.shape[1]
    me = 4 * lax.axis_index("x") + 2 * lax.axis_index("y") + lax.axis_index("c")
    xs, ctxs, tgt = x[0], ctx[0], loss_target[0]
    acols = ada_w.shape[2]

    small = jnp.concatenate([even_conv_w.reshape(-1), ffn_conv_w.reshape(-1), odd_pool_scale.reshape(-1)])
    nsmall = small.shape[0]
    small = jnp.pad(small, (0, (-nsmall) % 1024)).reshape(-1, 128)
    c_rows = jnp.pad(c, ((0, 7), (0, 0)))
    tr = lambda a: jnp.swapaxes(a, -1, -2)
    g_c, g_win, g_small = _all_gather([c_rows, tr(even_w_in[0]).astype(BF16), small], name="gather_first")
    w_in_t = g_win.reshape(-1, D)
    g_small = g_small.reshape(NDEV, -1)
    ecw = even_conv_w.shape[2]
    fcw = ffn_conv_w.shape[2]
    conv_w = g_small[:, :3 * ecw].reshape(NDEV, 3, ecw).transpose(1, 0, 2).reshape(3, CW)
    o1 = 3 * ecw
    fconv_w = g_small[:, o1:o1 + 6 * fcw].reshape(NDEV, 2, 3, fcw).transpose(1, 2, 0, 3).reshape(2, 3, DFF)
    o2 = o1 + 6 * fcw
    pool_scale = g_small[:, o2:o2 + D // NDEV].reshape(1, D)

    mraw = jnp.concatenate([g_c[:, 0, :], c_ctx[None, :], jnp.zeros((7, D), F32)], axis=0)
    my_bias = lax.dynamic_slice_in_dim(ada_b, me * acols, acols, axis=1)
    modp = jnp.stack([_mm(mraw, ada_w[l], silu_a=True, bias=my_bias[l:l + 1], name=f"ada_proj_{l}", tm=16, tn=256)
                      for l in range(2)])
    (g_mod,) = _all_gather([modp], name="gather_mod")
    mod_rows = g_mod.transpose(1, 2, 0, 3).reshape(2, 16, 6 * D)
    late_shards = [even_w_out[0].astype(BF16), odd_pool_w[0].astype(BF16), tr(ffn_w_up[0]).astype(BF16),
                   tr(ffn_w_up[1]).astype(BF16), ffn_w_down[0].astype(BF16), ffn_w_down[1].astype(BF16)]
    late_shards, mod_rows = lax.optimization_barrier((late_shards, mod_rows))
    h_weights = _exchange_start(late_shards, scatter=False, name="weights_start")
    mod_rows = mod_rows + h_weights["token"][0, 0]
    mod = lax.dynamic_index_in_dim(mod_rows, me, axis=1, keepdims=False)
    sh1, sc1, g1, sh2, sc2, g2 = _split6(mod[0])
    sh1b, sc1b, g1b, sh2b, sc2b, g2b = _split6(mod[1])
    csh1, csc1 = _split6(mod_rows[0, 8])[:2]
    mixn = [_row(mix_norm[l]) for l in range(2)]
    ffnn = [_row(ffn_norm[l]) for l in range(2)]
    qg, kg = _row(even_q_gain[0]), _row(even_k_gain[0])
    fcb = [_row(ffn_conv_b[l]) for l in range(2)]

    cs_t, sn_t = _rope_tables(n)
    a_lat = _norm_mod(xs, mixn[0], sc1, sh1, name="mix0_norm")
    a_ctx = _norm_mod(ctxs, mixn[0], csc1, csh1, name="mix0_norm_ctx")
    p_lat = _mm_w(a_lat, w_in_t, tb=True, name="in_proj")
    p_ctx = _mm(a_ctx, w_in_t[AW:AW + 4 * HD], tb=True, name="in_proj_ctx", tm=256, tn=512, tk=1024)
    kv_ctx = _qkv_prep(p_ctx, qg, kg, None, None, has_q=False, kv_col=0, kv_rows=lc + n, name="qkv_prep_ctx")
    q_r, k_all, v_all = _qkv_prep(p_lat, qg, kg, cs_t, sn_t, has_q=True, kv_col=1, kv_rows=lc + n, kv_row_off=lc,
                                  kv_into=kv_ctx, name="qkv_prep")
    o_attn, lse = _attn_fwd(q_r, k_all, v_all, name="attn_fwd")
    cat = _conv_gate_fwd(p_lat, o_attn, conv_w, name="conv_gate")
    g_wout, g_pool, g_up0, g_up1, g_down0, g_down1 = _exchange_wait(h_weights, cat, name="weights_wait")
    w_out = g_wout.reshape(D, D)
    pool_w = g_pool.transpose(1, 0, 2, 3).reshape(4, PG, PG)
    w_up_t = [g_up0.reshape(2 * DFF, D), g_up1.reshape(2 * DFF, D)]
    w_down = [g_down0.reshape(DFF, D), g_down1.reshape(DFF, D)]
    y0, x1, f0 = _mm_w_ep(cat, w_out, _ep_resid_norm, [xs], [g1, ffnn[0], sc2, sh2], [F32, F32, BF16], [],
                          tm=512, name="out_proj_norm")[:3]
    *u0, h0 = _ffn_up_glu(f0, w_up_t[0], fconv_w[0], fcb[0], name="ffn_up_glu_l0")
    z0 = _mm_w(h0, w_down[0], name="ffn_down_l0")

    x2, mixed = _pool_fwd(x1, z0, g2, mixn[1], sc1b, sh1b, pool_w, name="pool_fwd")
    x3, f1 = _norm_mod(x2, ffnn[1], sc2b, sh2b, y=mixed, g=g1b, ymul=pool_scale, name="ffn_norm_l1")
    *u1, h1 = _ffn_up_glu(f1, w_up_t[1], fconv_w[1], fcb[1], name="ffn_up_glu_l1")
    dx4, dz1, loss_part, dg2b = _mm_w_ep(h1, w_down[1], _ep_loss(D), [x3, tgt], [g2b], [F32, BF16], [128, D],
                                         tm=512, name="ffn_down_loss")
    loss = lax.psum(loss_part[0, 0], ("x", "y", "c"))

    (dx3,), (dsh2b, dsc2b, dffn1), (dup1g, dup1v, ddown1, dfcw1, dfcb1) = _ffn_bwd(
        dx4, dz1, x3, f1, u1, h1, ffnn[1], sc2b, w_up_t[1], fconv_w[1], w_down[1], "l1")
    dx2, dpool_w, dg1b, dpscale, dsh1b, dsc1b, dmix1, dz0, dg2 = _pool_bwd(
        dx3, mixed, x2, g1b, pool_scale, mixn[1], sc1b, sh1b, pool_w, z0, g2, name="pool_bwd")

    def up_shards(dg_t, dv_t):
        return jnp.concatenate([dg_t, dv_t], axis=0).reshape(NDEV, -1, D)

    s_pool = dpool_w.astype(BF16).reshape(4, NDEV, PG // NDEV, PG).transpose(1, 0, 2, 3)
    h_g1 = _exchange_start([s_pool, up_shards(dup1g, dup1v), ddown1.reshape(NDEV, DFF // NDEV, D)], scatter=True,
                           name="grads1_start")

    (dx1, dy0), (dsh2, dsc2, dffn0, dg1), (dup0g, dup0v, ddown0, dfcw0, dfcb0) = _ffn_bwd(
        dx2, dz0, x1, f0, u0, h0, ffnn[0], sc2, w_up_t[0], fconv_w[0] + h_g1["token"][0, 0], w_down[0], "l0",
        gate_y=y0, gate_g=g1)
    h_g0 = _exchange_start([up_shards(dup0g, dup0v), ddown0.reshape(NDEV, DFF // NDEV, D)], scatter=True,
                           name="grads0_start")
    dcat = _mm_w(dy0, w_out, tb=True, name="out_proj_dx", tm=512)
    d_wout = _mm_tn((cat, dy0), name="out_proj_dw")
    dp_conv, dconv_w = _conv_gate_bwd(dcat, p_lat, conv_w + h_g0["token"][0, 0], name="conv_gate_bwd")
    dq_r, dk_all, dv_all = _attn_bwd(q_r, k_all, v_all, dcat, o_attn, lse, name="attn_bwd")
    dp_qkv, dqg_l, dkg_l = _qkv_bwd(p_lat, dq_r, dk_all, dv_all, qg, kg, cs_t, sn_t, has_q=True, kv_col=1,
                                    kv_row_off=lc, name="qkv_bwd")
    dp_ctx, _zero_qg, dkg_c = _qkv_bwd(p_ctx, None, dk_all, dv_all, qg, kg, None, None, has_q=False, kv_col=0,
                                       kv_row_off=0, name="qkv_bwd_ctx")
    da_ctx = _mm(dp_ctx, w_in_t[:D], name="in_proj_dx_ctx", tm=256, tn=512, tk=1024)
    d_win_qkv = _mm_tn([(dp_qkv, a_lat), (dp_ctx, a_ctx)], name="in_proj_dw_qkv")
    d_win_conv = _mm_tn((dp_conv, a_lat), name="in_proj_dw_conv")
    d_win_t = jnp.concatenate([d_win_qkv, d_win_conv], axis=0)
    grad_x, dsh1, dsc1, dmix0 = _mm_w_ep([dp_qkv, dp_conv], w_in_t, _ep_norm_bwd(False), [xs, dx1], [mixn[0], sc1],
                                         [F32], [D] * 3, tm=512, name="in_proj_dx_norm_bwd")
    _dctx, dcsh1, dcsc1, dmix0c = _norm_mod_bwd(da_ctx, ctxs, mixn[0], csc1, name="mix0_norm_bwd_ctx")

    z1k = jnp.zeros((1, D), F32)
    pack = jnp.concatenate(
        [v.reshape(-1) for v in (dsh1, dsc1, dg1, dsh2, dsc2, dg2, dsh1b, dsc1b, dg1b, dsh2b, dsc2b, dg2b,
                                 dcsh1, dcsc1, z1k, z1k, z1k, z1k,
                                 dmix0, dmix1, dmix0c, z1k, dffn0, dffn1, dqg_l, dkg_l + dkg_c,
                                 dfcb0, dfcb1, dconv_w, dfcw0, dfcw1, dpscale)])
    npack = pack.shape[0]
    pack = jnp.pad(pack, (0, (-npack) % 1024)).reshape(-1, 128)
    (g_pack,) = _all_gather([pack], name="gather_small_grads")
    gp = g_pack.reshape(NDEV, -1)
    off = [0]

    def take(size):
        seg = gp[:, off[0]:off[0] + size]
        off[0] += size
        return seg

    dmod_all = take(12 * D).reshape(NDEV, 2, 6 * D)
    dmodc_all = take(6 * D).reshape(NDEV, 1, 6 * D)
    dmix_all = take(4 * D).reshape(NDEV, 2, 2, D)
    dffn_all = take(2 * D).reshape(NDEV, 2, D)
    dqg_all = take(HD).reshape(NDEV, 1, HD)
    dkg_all = take(HD).reshape(NDEV, 1, HD)
    dfcb_all = take(2 * DFF).reshape(NDEV, 2, DFF)
    dconvw_all = take(3 * CW).reshape(NDEV, 3, CW)
    dfcw_all = take(6 * DFF).reshape(NDEV, 2, 3, DFF)
    dpscale_all = take(D).reshape(NDEV, D)

    dmodc_sum = dmodc_all[0]
    for dev in range(1, NDEV):
        dmodc_sum = dmodc_sum + dmodc_all[dev]
    my_cols = lambda a: lax.dynamic_slice_in_dim(a, me * acols, acols, axis=a.ndim - 1)
    rows0 = jnp.concatenate([my_cols(dmod_all[:, 0]), my_cols(dmodc_sum), jnp.zeros((7, acols), F32)], axis=0)
    rows1 = jnp.concatenate([my_cols(dmod_all[:, 1]), jnp.zeros((8, acols), F32)], axis=0)
    d_ada = jnp.stack([_mm(mraw, rows, ta=True, silu_a=True, name=f"ada_dw_{l}", tm=512, tn=256, tk=16)
                       for l, rows in enumerate((rows0, rows1))])
    dscc_part = _mm(rows0, ada_w[0], tb=True, name="ada_dcctx", tm=16, tn=512, tk=256)
    (g_dscc,) = _all_gather([dscc_part[8:16]], name="gather_dcctx")

    attn_shards = [d_win_t.reshape(NDEV, -1, D), d_wout.reshape(NDEV, D // NDEV, D)]
    attn_shards, g_dscc = lax.optimization_barrier((attn_shards, g_dscc))
    h_ga = _exchange_start(attn_shards, scatter=True, name="grads_attn_start")
    dmod_all = dmod_all + h_ga["token"][0, 0]

    outs = {}

    def put(nm, res):
        outs["grad_" + nm], outs["delta_" + nm], outs["new_m_" + nm], outs["new_v_" + nm] = res

    dmodc_pad = jnp.concatenate([dmodc_all, jnp.zeros_like(dmodc_all)], axis=1)
    put("ada_b", _adamw_nd(jnp.concatenate([dmod_all, dmodc_pad], axis=0), ada_b, m_ada_b, v_ada_b, name="adam_ada_b"))
    put("mix_norm", _adamw_nd(jnp.concatenate([dmix_all[:, 0], dmix_all[:, 1]], axis=0), mix_norm, m_mix_norm,
                              v_mix_norm, name="adam_mix_norm"))
    put("ffn_norm", _adamw_nd(dffn_all, ffn_norm, m_ffn_norm, v_ffn_norm, name="adam_ffn_norm"))
    put("even_q_gain", _adamw_nd(dqg_all, even_q_gain, m_even_q_gain, v_even_q_gain, name="adam_q_gain"))
    put("even_k_gain", _adamw_nd(dkg_all, even_k_gain, m_even_k_gain, v_even_k_gain, name="adam_k_gain"))
    put("ffn_conv_b", _adamw_nd(dfcb_all, ffn_conv_b, m_ffn_conv_b, v_ffn_conv_b, name="adam_ffn_conv_b"))
    my_convw = lax.dynamic_slice_in_dim(dconvw_all, me * ecw, ecw, axis=2)[:, None]
    put("even_conv_w", _adamw_nd(my_convw, even_conv_w, m_even_conv_w, v_even_conv_w, name="adam_even_conv_w"))
    my_fcw = lax.dynamic_slice_in_dim(dfcw_all, me * fcw, fcw, axis=3)
    put("ffn_conv_w", _adamw_nd(my_fcw, ffn_conv_w, m_ffn_conv_w, v_ffn_conv_w, name="adam_ffn_conv_w"))
    my_ps = lax.dynamic_slice_in_dim(dpscale_all, me * (D // NDEV), D // NDEV, axis=1)[:, None]
    put("odd_pool_scale", _adamw_nd(my_ps, odd_pool_scale, m_odd_pool_scale, v_odd_pool_scale, name="adam_pool_scale"))

    put("ada_w", _adamw_nd(d_ada[None], ada_w, m_ada_w, v_ada_w, name="adam_ada_w"))
    put("c_ctx", _adamw_nd(g_dscc[:, 0:1, :].reshape(NDEV, D), c_ctx, m_c_ctx, v_c_ctx, name="adam_c_ctx",
                           silu_grad_of=c_ctx))

    r_pool, r_up1, r_down1 = _exchange_wait(h_g1, outs["grad_ada_b"], name="grads1_wait")
    r_up0, r_down0 = _exchange_wait(h_g0, outs["grad_mix_norm"], name="grads0_wait")
    r_win, r_wout = _exchange_wait(h_ga, outs["grad_c_ctx"], name="grads_attn_wait")
    put("even_w_in", [tr(a) for a in _adamw_nd(r_win[:, None], tr(even_w_in), tr(m_even_w_in), tr(v_even_w_in),
                                               name="adam_w_in")])
    put("even_w_out", _adamw_nd(r_wout[:, None], even_w_out, m_even_w_out, v_even_w_out, name="adam_w_out"))
    put("odd_pool_w", _adamw_nd(r_pool[:, None], odd_pool_w, m_odd_pool_w, v_odd_pool_w, name="adam_pool_w"))
    put("ffn_w_up", [tr(a) for a in _adamw_nd([r_up0, r_up1], tr(ffn_w_up), tr(m_ffn_w_up), tr(v_ffn_w_up),
                                              name="adam_w_up")])
    put("ffn_w_down", _adamw_nd([r_down0, r_down1], ffn_w_down, m_ffn_w_down, v_ffn_w_down, name="adam_w_down"))

    names = ["c_ctx", "ada_w", "ada_b", "mix_norm", "ffn_norm", "even_w_in", "even_q_gain", "even_k_gain",
             "even_conv_w", "even_w_out", "odd_pool_w", "odd_pool_scale", "ffn_w_up", "ffn_conv_w", "ffn_conv_b",
             "ffn_w_down"]
    result = [loss, grad_x[None]]
    for kind in ("grad_", "delta_", "new_m_", "new_v_"):
        result += [outs[kind + nm] for nm in names]
    return tuple(result)
```

```python
import functools
import math

import jax
import jax.numpy as jnp
from jax import lax
from jax.experimental import pallas as pl
from jax.experimental.pallas import tpu as pltpu

F32 = jnp.float32
BF16 = jnp.bfloat16

D = 1024
HD = 128
NQ = 4
NKV = 2
AW = NQ * HD
CW = D - AW
DFF = 2816
GRID_W = 64
ROPE_THETA = 10000.0
POOL_WINDOWS = (2, 4, 8, 16)
PG = D // 4
EPS = 1e-6
NDEV = 8
HALO = 8
MESH = pl.DeviceIdType.MESH

ADAM_LR = 0.001
ADAM_B1 = 0.9
ADAM_B2 = 0.999
ADAM_EPS = 1e-08
ADAM_WD = 0.01
ADAM_STEP = 10


def _pick(dim, prefs):
    for p in prefs:
        if dim % p == 0:
            return p
    return dim


def _params(*sem):
    return pltpu.CompilerParams(dimension_semantics=sem)


_NT = (((1,), (1,)), ((), ()))
_TN = (((0,), (0,)), ((), ()))
_SCALE = HD ** -0.5
_QSCALE = _SCALE * math.log2(math.e)
_LN2 = math.log(2.0)


def _mm(a_list, b, *, name, ta=False, tb=False, out_dtype=F32, silu_a=False, bias=None, tm=None, tn=None, tk=None):
    if not isinstance(a_list, (list, tuple)):
        a_list = [a_list]
    na = len(a_list)
    assert not (ta and na > 1)
    if ta:
        kdim, m = a_list[0].shape
        ks = [kdim]
    else:
        m = a_list[0].shape[0]
        ks = [a.shape[1] for a in a_list]
        kdim = sum(ks)
    n = b.shape[0] if tb else b.shape[1]
    assert (b.shape[1] if tb else b.shape[0]) == kdim
    kunit = math.gcd(*ks) if na > 1 else kdim
    tm = min(tm, m) if tm else _pick(m, (512, 256, 128, 64, 32, 16, 8))
    tn = min(tn, n) if tn else _pick(n, (512, 256, 128))
    tk = min(tk, kunit) if tk else _pick(kunit, (1024, 768, 512, 256, 128))
    assert m % tm == 0 and n % tn == 0 and all(k % tk == 0 for k in ks)
    nks = [k // tk for k in ks]
    starts = [sum(nks[:i]) for i in range(na)]
    nk = sum(nks)
    has_bias = bias is not None

    def body(*refs):
        a_refs = refs[:na]
        b_ref = refs[na]
        bias_ref = refs[na + 1] if has_bias else None
        o_ref = refs[na + 1 + has_bias]
        acc = refs[-1]
        k = pl.program_id(2)

        @pl.when(k == 0)
        def _():
            acc[...] = jnp.zeros_like(acc)

        bv = b_ref[...].astype(BF16)
        dn = (((0 if ta else 1,), (1 if tb else 0,)), ((), ()))
        for idx in range(na):
            def step(idx=idx):
                av = a_refs[idx][...]
                if silu_a:
                    av = av * jax.nn.sigmoid(av)
                acc[...] += lax.dot_general(av.astype(BF16), bv, dn, preferred_element_type=F32)
            if na == 1:
                step()
            else:
                pl.when((k >= starts[idx]) & (k < starts[idx] + nks[idx]))(step)

        @pl.when(k == nk - 1)
        def _():
            r = acc[...]
            if has_bias:
                r = r + bias_ref[...]
            o_ref[...] = r.astype(o_ref.dtype)

    in_specs = []
    for idx in range(na):
        if ta:
            in_specs.append(pl.BlockSpec((tk, tm), lambda i, j, k: (k, i)))
        else:
            lo, cnt = starts[idx], nks[idx]
            in_specs.append(pl.BlockSpec((tm, tk), lambda i, j, k, lo=lo, cnt=cnt: (i, jnp.clip(k - lo, 0, cnt - 1))))
    if tb:
        in_specs.append(pl.BlockSpec((tn, tk), lambda i, j, k: (j, k)))
    else:
        in_specs.append(pl.BlockSpec((tk, tn), lambda i, j, k: (k, j)))
    args = list(a_list) + [b]
    if has_bias:
        in_specs.append(pl.BlockSpec((1, tn), lambda i, j, k: (0, j)))
        args.append(bias)
    return pl.pallas_call(
        body, grid=(m // tm, n // tn, nk), in_specs=in_specs,
        out_specs=pl.BlockSpec((tm, tn), lambda i, j, k: (i, j)),
        out_shape=jax.ShapeDtypeStruct((m, n), out_dtype),
        scratch_shapes=[pltpu.VMEM((tm, tn), F32)], name=name,
        compiler_params=_params("parallel", "parallel", "arbitrary"))(*args)


def _mm_w(a_list, w, *, name, tb=False, tm=256, out_dtype=F32):
    if not isinstance(a_list, (list, tuple)):
        a_list = [a_list]
    na = len(a_list)
    m = a_list[0].shape[0]
    ks = [a.shape[1] for a in a_list]
    offs = [sum(ks[:i]) for i in range(na)]
    n = w.shape[0] if tb else w.shape[1]
    assert (w.shape[1] if tb else w.shape[0]) == sum(ks)
    tm = min(tm, m)
    assert m % tm == 0

    def body(*refs):
        a_refs, w_ref, o_ref = refs[:na], refs[na], refs[na + 1]
        acc = None
        for idx in range(na):
            av = a_refs[idx][...].astype(BF16)
            if tb:
                part = lax.dot_general(av, w_ref[:, offs[idx]:offs[idx] + ks[idx]], _NT, preferred_element_type=F32)
            else:
                part = jnp.dot(av, w_ref[offs[idx]:offs[idx] + ks[idx], :], preferred_element_type=F32)
            acc = part if acc is None else acc + part
        o_ref[...] = acc.astype(o_ref.dtype)

    in_specs = [pl.BlockSpec((tm, k), lambda i: (i, 0)) for k in ks] + [pl.BlockSpec(w.shape, lambda i: (0, 0))]
    return pl.pallas_call(
        body, grid=(m // tm,), in_specs=in_specs, out_specs=pl.BlockSpec((tm, n), lambda i: (i, 0)),
        out_shape=jax.ShapeDtypeStruct((m, n), out_dtype), name=name, compiler_params=_params("parallel"))(*a_list, w)


def _mm_w_ep(a_list, w, epilogue, row_in, vec_in, out_dtypes, sum_widths, *, name, tb=False, tm=256, sub=256):
    if not isinstance(a_list, (list, tuple)):
        a_list = [a_list]
    na, nr, nv, no, ns = len(a_list), len(row_in), len(vec_in), len(out_dtypes), len(sum_widths)
    m = a_list[0].shape[0]
    ks = [a.shape[1] for a in a_list]
    offs = [sum(ks[:i]) for i in range(na)]
    n = w.shape[0] if tb else w.shape[1]
    assert (w.shape[1] if tb else w.shape[0]) == sum(ks)
    tm = min(tm, m)
    sub = min(sub, tm)
    assert m % tm == 0 and tm % sub == 0

    def body(*refs):
        a_refs, w_ref = refs[:na], refs[na]
        row_refs = refs[na + 1:na + 1 + nr]
        vec_refs = refs[na + 1 + nr:na + 1 + nr + nv]
        out_refs = refs[na + 1 + nr + nv:na + 1 + nr + nv + no]
        sum_refs = refs[na + 1 + nr + nv + no:]

        @pl.when(pl.program_id(0) == 0)
        def _():
            for s_ref in sum_refs:
                s_ref[...] = jnp.zeros_like(s_ref)

        vecs = [v[...] for v in vec_refs]
        for r0 in range(0, tm, sub):
            acc = None
            for idx in range(na):
                av = a_refs[idx][r0:r0 + sub, :].astype(BF16)
                if tb:
                    part = lax.dot_general(av, w_ref[:, offs[idx]:offs[idx] + ks[idx]], _NT, preferred_element_type=F32)
                else:
                    part = jnp.dot(av, w_ref[offs[idx]:offs[idx] + ks[idx], :], preferred_element_type=F32)
                acc = part if acc is None else acc + part
            outs, sums = epilogue(acc, [r[r0:r0 + sub, :] for r in row_refs], vecs)
            for o_ref, o in zip(out_refs, outs):
                o_ref[r0:r0 + sub, :] = o.astype(o_ref.dtype)
            for s_ref, s in zip(sum_refs, sums):
                s_ref[...] += s

    row = pl.BlockSpec((tm, n), lambda i: (i, 0))
    in_specs = ([pl.BlockSpec((tm, k), lambda i: (i, 0)) for k in ks] + [pl.BlockSpec(w.shape, lambda i: (0, 0))]
                + [row] * nr + [_vec(n)] * nv)
    return pl.pallas_call(
        body, grid=(m // tm,), in_specs=in_specs, out_specs=[row] * no + [_vec(sw) for sw in sum_widths],
        out_shape=[jax.ShapeDtypeStruct((m, n), dt) for dt in out_dtypes]
        + [jax.ShapeDtypeStruct((1, sw), F32) for sw in sum_widths],
        name=name, compiler_params=_params("arbitrary" if ns else "parallel"))(*a_list, w, *row_in, *vec_in)


def _ep_norm_bwd(has_gate):
    def ep(dav, rows, vecs):
        xv = rows[0]
        gain, scv = vecs[0], vecs[1]
        r = lax.rsqrt(jnp.mean(xv * xv, axis=-1, keepdims=True) + EPS)
        xh = xv * r
        nrm = xh * gain
        dn = dav * (1.0 + scv)
        dxh = dn * gain
        dx = r * (dxh - xh * jnp.mean(dxh * xh, axis=-1, keepdims=True)) + rows[1]
        outs, sums = [dx], [_colsum(dav), _colsum(dav * nrm), _colsum(dn * xh)]
        if has_gate:
            outs.append(dx * vecs[2])
            sums.append(_colsum(dx * rows[2]))
        return outs, sums
    return ep


def _ep_loss(d):
    def ep(zv, rows, vecs):
        xv, tv = rows
        gv = vecs[0]
        diff = (xv + gv * zv) - tv
        dx = diff * (1.0 / d)
        part = 0.5 * jnp.sum(jnp.mean(diff * diff, axis=-1, keepdims=True), axis=0, keepdims=True)
        return [dx, dx * gv], [jnp.broadcast_to(part, (1, 128)), _colsum(dx * zv)]
    return ep


def _ep_resid_norm(yv, rows, vecs):
    g, gain, scv, shv = vecs
    xv = rows[0] + g * yv
    r = lax.rsqrt(jnp.mean(xv * xv, axis=-1, keepdims=True) + EPS)
    return [yv, xv, ((xv * r) * gain) * (1.0 + scv) + shv], []


def _mm_tn(pairs, *, name, tk=1024, out_dtype=BF16, blocks=1, block=0, into=None):
    if not isinstance(pairs, list):
        pairs = [pairs]
    m, n = pairs[0][0].shape[1], pairs[0][1].shape[1]
    tks = [min(tk, a.shape[0]) for a, _ in pairs]
    nks = [a.shape[0] // t for (a, _), t in zip(pairs, tks)]
    assert all(a.shape[0] == b.shape[0] and a.shape[0] % t == 0 for (a, b), t in zip(pairs, tks))
    starts = [sum(nks[:i]) for i in range(len(pairs))]
    nk = sum(nks)

    def body(*refs):
        o_ref, acc = refs[-2], refs[-1]
        k = pl.program_id(0)

        @pl.when(k == 0)
        def _():
            acc[...] = jnp.zeros_like(acc)

        for idx in range(len(pairs)):
            a_ref, b_ref = refs[2 * idx], refs[2 * idx + 1]

            def step(a_ref=a_ref, b_ref=b_ref):
                acc[...] += lax.dot_general(a_ref[...], b_ref[...], _TN, preferred_element_type=F32)

            if len(pairs) == 1:
                step()
            else:
                pl.when((k >= starts[idx]) & (k < starts[idx] + nks[idx]))(step)

        @pl.when(k == nk - 1)
        def _():
            o_ref[...] = acc[...].astype(o_ref.dtype)

    in_specs, args = [], []
    for (a, b), t, lo, cnt in zip(pairs, tks, starts, nks):
        idx_map = lambda k, lo=lo, cnt=cnt: (jnp.clip(k - lo, 0, cnt - 1), 0)
        in_specs += [pl.BlockSpec((t, m), idx_map), pl.BlockSpec((t, n), idx_map)]
        args += [a, b]
    aliases = {}
    if into is not None:
        aliases = {len(args): 0}
        in_specs.append(pl.BlockSpec(memory_space=pl.ANY))
        args.append(into)
    return pl.pallas_call(
        body, grid=(nk,), in_specs=in_specs, out_specs=pl.BlockSpec((m, n), lambda k: (block, 0)),
        out_shape=jax.ShapeDtypeStruct((m * blocks, n), out_dtype), scratch_shapes=[pltpu.VMEM((m, n), F32)],
        input_output_aliases=aliases, name=name, compiler_params=_params("arbitrary"))(*args)


def _vec(d, col=None):
    if col is None:
        return pl.BlockSpec((1, d), lambda i, *_: (0, 0))
    return pl.BlockSpec((1, d), col)


def _halo_specs(tm, width, nrows, colblk=0, row_off=0):
    r = tm // HALO
    off = row_off // HALO
    last = nrows // HALO - 1
    prev = pl.BlockSpec((HALO, width), lambda i, *_: (off + jnp.maximum(i * r - 1, 0), colblk))
    nxt = pl.BlockSpec((HALO, width), lambda i, *_: (off + jnp.minimum((i + 1) * r, last), colblk))
    return prev, nxt


def _ext(prev_ref, main_ref, next_ref, i, ni):
    p = jnp.where(i > 0, prev_ref[...], 0.0)
    n = jnp.where(i < ni - 1, next_ref[...], 0.0)
    return jnp.concatenate([p, main_ref[...], n], axis=0)


def _sh(ext, k, tm):
    if k == 0:
        return ext[HALO:HALO + tm]
    rows = ext.shape[0]
    return pltpu.roll(ext, (-k) % rows, axis=0)[HALO:HALO + tm]


def _roll_rows(v, k):
    rows = v.shape[0]
    return pltpu.roll(v, (-k) % rows, axis=0) if k % rows else v


def _conv3(ext, w_ref, tm):
    return _sh(ext, -1, tm) * w_ref[0:1, :] + _sh(ext, 0, tm) * w_ref[1:2, :] + _sh(ext, 1, tm) * w_ref[2:3, :]


def _colsum(v):
    return jnp.sum(v, axis=0, keepdims=True)


def _acc_out(ref, i, val):
    @pl.when(i == 0)
    def _():
        ref[...] = jnp.zeros_like(ref)

    ref[...] += val


def _sigmoid(v):
    return jax.nn.sigmoid(v)


def _norm_mod(x, gain, sc, sh, *, name, y=None, g=None, ymul=None, tm=512):
    n, d = x.shape
    tm = min(tm, n)
    has_res = y is not None
    has_mul = ymul is not None

    def body(*refs):
        it = iter(refs)
        x_ref = next(it)
        y_ref = next(it) if has_res else None
        g_ref = next(it) if has_res else None
        m_ref = next(it) if has_mul else None
        gain_ref, sc_ref, sh_ref = next(it), next(it), next(it)
        xo_ref = next(it) if has_res else None
        a_ref = next(it)
        xv = x_ref[...]
        if has_res:
            yv = y_ref[...]
            if has_mul:
                yv = yv * m_ref[...]
            xv = xv + g_ref[...] * yv
            xo_ref[...] = xv
        r = lax.rsqrt(jnp.mean(xv * xv, axis=-1, keepdims=True) + EPS)
        nrm = (xv * r) * gain_ref[...]
        a_ref[...] = (nrm * (1.0 + sc_ref[...]) + sh_ref[...]).astype(BF16)

    row = pl.BlockSpec((tm, d), lambda i: (i, 0))
    in_specs, args = [row], [x]
    if has_res:
        in_specs += [row, _vec(d)]
        args += [y, g]
    if has_mul:
        in_specs.append(_vec(d))
        args.append(ymul)
    in_specs += [_vec(d)] * 3
    args += [gain, sc, sh]
    out_specs, out_shape = [], []
    if has_res:
        out_specs.append(row)
        out_shape.append(jax.ShapeDtypeStruct((n, d), F32))
    out_specs.append(row)
    out_shape.append(jax.ShapeDtypeStruct((n, d), BF16))
    res = pl.pallas_call(body, grid=(n // tm,), in_specs=in_specs, out_specs=out_specs, out_shape=out_shape,
                         name=name, compiler_params=_params("parallel"))(*args)
    return res if has_res else res[0]


def _norm_mod_bwd(da, x, gain, sc, *, name, dres=None, gate_y=None, gate_g=None, tm=512):
    n, d = x.shape
    tm = min(tm, n)
    has_res = dres is not None
    has_gate = gate_y is not None

    def body(*refs):
        it = iter(refs)
        da_ref, x_ref = next(it), next(it)
        r_ref = next(it) if has_res else None
        y_ref = next(it) if has_gate else None
        g_ref = next(it) if has_gate else None
        gain_ref, sc_ref = next(it), next(it)
        dx_ref, dsh_ref, dsc_ref, dgn_ref = next(it), next(it), next(it), next(it)
        dy_ref = next(it) if has_gate else None
        dg_ref = next(it) if has_gate else None
        i = pl.program_id(0)
        xv = x_ref[...]
        dav = da_ref[...]
        r = lax.rsqrt(jnp.mean(xv * xv, axis=-1, keepdims=True) + EPS)
        xh = xv * r
        nrm = xh * gain_ref[...]
        dn = dav * (1.0 + sc_ref[...])
        dxh = dn * gain_ref[...]
        dx = r * (dxh - xh * jnp.mean(dxh * xh, axis=-1, keepdims=True))
        if has_res:
            dx = dx + r_ref[...]
        dx_ref[...] = dx
        _acc_out(dsh_ref, i, _colsum(dav))
        _acc_out(dsc_ref, i, _colsum(dav * nrm))
        _acc_out(dgn_ref, i, _colsum(dn * xh))
        if has_gate:
            dy_ref[...] = (dx * g_ref[...]).astype(BF16)
            _acc_out(dg_ref, i, _colsum(dx * y_ref[...]))

    row = pl.BlockSpec((tm, d), lambda i: (i, 0))
    in_specs, args = [row, row], [da, x]
    if has_res:
        in_specs.append(row)
        args.append(dres)
    if has_gate:
        in_specs += [row, _vec(d)]
        args += [gate_y, gate_g]
    in_specs += [_vec(d)] * 2
    args += [gain, sc]
    vec_shape = jax.ShapeDtypeStruct((1, d), F32)
    out_specs = [row, _vec(d), _vec(d), _vec(d)]
    out_shape = [jax.ShapeDtypeStruct((n, d), F32), vec_shape, vec_shape, vec_shape]
    if has_gate:
        out_specs += [row, _vec(d)]
        out_shape += [jax.ShapeDtypeStruct((n, d), BF16), vec_shape]
    return pl.pallas_call(
        body, grid=(n // tm,), in_specs=in_specs, out_specs=out_specs, out_shape=out_shape,
        name=name, compiler_params=_params("arbitrary"))(*args)


def _ffn_up_glu(f, w_up, cw, cb, *, name, tm=256, tc=256):
    n, d = f.shape
    tm = min(tm, n)
    ni = n // tm
    nc = DFF // tc
    halo = 16
    rows = tm + 2 * halo
    r = tm // halo
    last = n // halo - 1

    def body(f_ref, fp_ref, fn_ref, w_ref, cw_ref, cb_ref, u_ref, gc_ref, h_ref):
        i = pl.program_id(0)
        a = f_ref[...]
        aext = jnp.concatenate([jnp.where(i > 0, fp_ref[...], jnp.zeros_like(fp_ref[...])), a,
                                jnp.where(i < ni - 1, fn_ref[...], jnp.zeros_like(fn_ref[...]))], axis=0)
        for j in range(nc):
            cols = slice(j * tc, (j + 1) * tc)
            vcols = slice(DFF + j * tc, DFF + (j + 1) * tc)
            gext = jnp.dot(aext, w_ref[:, cols], preferred_element_type=F32)
            val = jnp.dot(a, w_ref[:, vcols], preferred_element_type=F32)
            gate = gext[halo:halo + tm]
            gc = (pltpu.roll(gext, 1, axis=0)[halo:halo + tm] * cw_ref[0:1, cols] + gate * cw_ref[1:2, cols]
                  + pltpu.roll(gext, rows - 1, axis=0)[halo:halo + tm] * cw_ref[2:3, cols]) + cb_ref[:, cols]
            u_ref[:, cols] = gate
            u_ref[:, vcols] = val
            gc_ref[:, cols] = gc
            h_ref[:, cols] = (gc * _sigmoid(gc) * val).astype(BF16)

    return pl.pallas_call(
        body, grid=(ni,),
        in_specs=[pl.BlockSpec((tm, d), lambda i: (i, 0)),
                  pl.BlockSpec((halo, d), lambda i: (jnp.maximum(i * r - 1, 0), 0)),
                  pl.BlockSpec((halo, d), lambda i: (jnp.minimum((i + 1) * r, last), 0)),
                  pl.BlockSpec(w_up.shape, lambda i: (0, 0)), pl.BlockSpec((3, DFF), lambda i: (0, 0)),
                  pl.BlockSpec((1, DFF), lambda i: (0, 0))],
        out_specs=[pl.BlockSpec((tm, 2 * DFF), lambda i: (i, 0)), pl.BlockSpec((tm, DFF), lambda i: (i, 0)),
                   pl.BlockSpec((tm, DFF), lambda i: (i, 0))],
        out_shape=[jax.ShapeDtypeStruct((n, 2 * DFF), F32), jax.ShapeDtypeStruct((n, DFF), F32),
                   jax.ShapeDtypeStruct((n, DFF), BF16)], name=name,
        compiler_params=_params("parallel"))(f, f, f, w_up, cw, cb)


def _ffn_down_glu_bwd(dz, w_down, u, gc, cw, *, name, tm=256, tc=256):
    n, d = dz.shape
    tm = min(tm, n)
    ni = n // tm
    nc = DFF // tc
    rows = tm + 2 * HALO

    def body(z_ref, zp_ref, zn_ref, w_ref, u_ref, vp_ref, vn_ref, c_ref, cp_ref, cn_ref, cw_ref,
             dg_ref, dv_ref, dcw_ref, dcb_ref):
        i = pl.program_id(0)

        @pl.when(i == 0)
        def _():
            dcw_ref[...] = jnp.zeros_like(dcw_ref)
            dcb_ref[...] = jnp.zeros_like(dcb_ref)

        zext = jnp.concatenate([jnp.where(i > 0, zp_ref[...], jnp.zeros_like(zp_ref[...])), z_ref[...],
                                jnp.where(i < ni - 1, zn_ref[...], jnp.zeros_like(zn_ref[...]))], axis=0)
        for j in range(nc):
            cols = slice(j * tc, (j + 1) * tc)
            vcols = slice(DFF + j * tc, DFF + (j + 1) * tc)
            dh = lax.dot_general(zext, w_ref[cols, :], _NT, preferred_element_type=F32)[HALO:HALO + rows]
            gcx = jnp.concatenate([cp_ref[:, cols], c_ref[:, cols], cn_ref[:, cols]], axis=0)
            vext = jnp.concatenate([vp_ref[:, cols], u_ref[:, vcols], vn_ref[:, cols]], axis=0)
            sg = _sigmoid(gcx)
            dgc = dh * vext * (sg * (1.0 + gcx * (1.0 - sg)))
            dv_ref[:, cols] = (dh[HALO:HALO + tm] * (gcx[HALO:HALO + tm] * sg[HALO:HALO + tm])).astype(BF16)
            d_next = pltpu.roll(dgc, rows - 1, axis=0)[HALO:HALO + tm]
            d_prev = pltpu.roll(dgc, 1, axis=0)[HALO:HALO + tm]
            d_here = dgc[HALO:HALO + tm]
            dg_ref[:, cols] = (d_next * cw_ref[0:1, cols] + d_here * cw_ref[1:2, cols]
                               + d_prev * cw_ref[2:3, cols]).astype(BF16)
            gate = u_ref[:, cols]
            dcw_ref[:, cols] += jnp.concatenate([_colsum(d_next * gate), _colsum(d_here * gate),
                                                 _colsum(d_prev * gate)], axis=0)
            dcb_ref[:, cols] += _colsum(d_here)

    def trio(width, halo, tile_width=None, colblk=0):
        r, last = tm // halo, n // halo - 1
        return [pl.BlockSpec((tm, tile_width or width), lambda i: (i, 0)),
                pl.BlockSpec((halo, width), lambda i: (jnp.maximum(i * r - 1, 0), colblk)),
                pl.BlockSpec((halo, width), lambda i: (jnp.minimum((i + 1) * r, last), colblk))]

    whole = lambda shape: pl.BlockSpec(shape, lambda i: (0, 0))
    return pl.pallas_call(
        body, grid=(ni,),
        in_specs=(trio(d, 16) + [whole(w_down.shape)] + trio(DFF, HALO, tile_width=2 * DFF, colblk=1)
                  + trio(DFF, HALO) + [whole((3, DFF))]),
        out_specs=[pl.BlockSpec((tm, DFF), lambda i: (i, 0)), pl.BlockSpec((tm, DFF), lambda i: (i, 0)),
                   whole((3, DFF)), whole((1, DFF))],
        out_shape=[jax.ShapeDtypeStruct((n, DFF), BF16), jax.ShapeDtypeStruct((n, DFF), BF16),
                   jax.ShapeDtypeStruct((3, DFF), F32), jax.ShapeDtypeStruct((1, DFF), F32)],
        name=name, compiler_params=_params("arbitrary"))(dz, dz, dz, w_down, u, u, u, gc, gc, gc, cw)


def _rope_tables(n):
    rows = n // GRID_W
    axis_dim = HD // 2
    inv_freq = jnp.power(ROPE_THETA, -jnp.arange(0, axis_dim, 2, dtype=F32) / axis_dim)
    ar = jnp.arange(rows, dtype=F32)[:, None] * inv_freq
    ac = jnp.arange(GRID_W, dtype=F32)[:, None] * inv_freq
    by_row = lambda a: jnp.repeat(a, GRID_W, axis=0)
    by_col = lambda a: jnp.tile(a, (rows, 1))
    cr, sr, cc, sc = by_row(jnp.cos(ar)), by_row(jnp.sin(ar)), by_col(jnp.cos(ac)), by_col(jnp.sin(ac))
    return jnp.concatenate([cr, cr, cc, cc], axis=1), jnp.concatenate([-sr, sr, -sc, sc], axis=1)


def _partner(v):
    lane = lax.broadcasted_iota(jnp.int32, v.shape, 1)
    return jnp.where((lane % 64) < 32, pltpu.roll(v, HD - 32, axis=1), pltpu.roll(v, 32, axis=1))


def _qkv_prep(p, q_gain, k_gain, cs, sn, *, name, has_q, kv_col, kv_rows=None, kv_row_off=0, kv_into=None, tm=256):
    n = p.shape[0]
    rope = cs is not None
    kv_rows = kv_rows or n
    rb = kv_row_off // tm

    def body(*refs):
        it = iter(refs)
        q_ref = next(it) if has_q else None
        kv_ref = next(it)
        qg_ref, kg_ref = next(it), next(it)
        cs_ref = next(it) if rope else None
        sn_ref = next(it) if rope else None
        if kv_into is not None:
            next(it), next(it)
        qo_ref = next(it) if has_q else None
        ko_ref, vo_ref = next(it), next(it)

        def norm_rope(xh, gain, mul=None):
            r = lax.rsqrt(jnp.mean(xh * xh, axis=-1, keepdims=True) + EPS)
            xn = (xh * r) * gain
            if rope:
                xn = xn * cs_ref[...] + _partner(xn) * sn_ref[...]
            if mul is not None:
                xn = xn * mul
            return xn.astype(BF16)

        if has_q:
            for h in range(NQ):
                qo_ref[h] = norm_rope(q_ref[:, h * HD:(h + 1) * HD], qg_ref[...], _QSCALE)
        for h in range(NKV):
            ko_ref[h] = norm_rope(kv_ref[:, h * HD:(h + 1) * HD], kg_ref[...])
            vo_ref[h] = kv_ref[:, (NKV + h) * HD:(NKV + h + 1) * HD].astype(BF16)

    in_specs, args = [], []
    if has_q:
        in_specs.append(pl.BlockSpec((tm, AW), lambda i: (i, 0)))
        args.append(p)
    in_specs += [pl.BlockSpec((tm, 2 * NKV * HD), lambda i: (i, kv_col)), _vec(HD), _vec(HD)]
    args += [p, q_gain, k_gain]
    if rope:
        in_specs += [pl.BlockSpec((tm, HD), lambda i: (i, 0))] * 2
        args += [cs, sn]
    out_specs, out_shape = [], []
    if has_q:
        out_specs.append(pl.BlockSpec((NQ, tm, HD), lambda i: (0, i, 0)))
        out_shape.append(jax.ShapeDtypeStruct((NQ, n, HD), BF16))
    out_specs += [pl.BlockSpec((NKV, tm, HD), lambda i: (0, rb + i, 0))] * 2
    out_shape += [jax.ShapeDtypeStruct((NKV, kv_rows, HD), BF16)] * 2
    aliases = {}
    if kv_into is not None:
        aliases = {len(args): int(has_q), len(args) + 1: int(has_q) + 1}
        in_specs += [pl.BlockSpec(memory_space=pl.ANY)] * 2
        args += list(kv_into)
    return pl.pallas_call(body, grid=(n // tm,), in_specs=in_specs, out_specs=out_specs, out_shape=out_shape,
                          input_output_aliases=aliases, name=name, compiler_params=_params("parallel"))(*args)


def _qkv_bwd(p, dq, dk, dv, q_gain, k_gain, cs, sn, *, name, has_q, kv_col, kv_row_off, tm=256):
    n = p.shape[0]
    rope = cs is not None
    rb = kv_row_off // tm

    def body(*refs):
        it = iter(refs)
        q_ref = next(it) if has_q else None
        kv_ref = next(it)
        dq_ref = next(it) if has_q else None
        dk_ref, dv_ref = next(it), next(it)
        qg_ref, kg_ref = next(it), next(it)
        cs_ref = next(it) if rope else None
        sn_ref = next(it) if rope else None
        dp_ref, dqg_ref, dkg_ref = next(it), next(it), next(it)
        i = pl.program_id(0)

        def back(xh, dout, gain):
            if rope:
                dout = dout * cs_ref[...] + _partner(dout * sn_ref[...])
            r = lax.rsqrt(jnp.mean(xh * xh, axis=-1, keepdims=True) + EPS)
            xhat = xh * r
            dxh = dout * gain
            dx = r * (dxh - xhat * jnp.mean(dxh * xhat, axis=-1, keepdims=True))
            return dx, _colsum(dout * xhat)

        dqg = jnp.zeros((1, HD), F32)
        dkg = jnp.zeros((1, HD), F32)
        if has_q:
            for h in range(NQ):
                dx, dg = back(q_ref[:, h * HD:(h + 1) * HD], dq_ref[h], qg_ref[...])
                dp_ref[:, h * HD:(h + 1) * HD] = dx.astype(BF16)
                dqg = dqg + dg
        else:
            dp_ref[:, 0:AW] = jnp.zeros((tm, AW), BF16)
        for h in range(NKV):
            dx, dg = back(kv_ref[:, h * HD:(h + 1) * HD], dk_ref[h], kg_ref[...])
            dp_ref[:, AW + h * HD:AW + (h + 1) * HD] = dx.astype(BF16)
            dkg = dkg + dg
            dp_ref[:, AW + (NKV + h) * HD:AW + (NKV + h + 1) * HD] = dv_ref[h].astype(BF16)
        _acc_out(dqg_ref, i, dqg)
        _acc_out(dkg_ref, i, dkg)

    in_specs, args = [], []
    if has_q:
        in_specs.append(pl.BlockSpec((tm, AW), lambda i: (i, 0)))
        args.append(p)
    in_specs.append(pl.BlockSpec((tm, 2 * NKV * HD), lambda i: (i, kv_col)))
    args.append(p)
    if has_q:
        in_specs.append(pl.BlockSpec((NQ, tm, HD), lambda i: (0, i, 0)))
        args.append(dq)
    in_specs += [pl.BlockSpec((NKV, tm, HD), lambda i: (0, rb + i, 0))] * 2 + [_vec(HD), _vec(HD)]
    args += [dk, dv, q_gain, k_gain]
    if rope:
        in_specs += [pl.BlockSpec((tm, HD), lambda i: (i, 0))] * 2
        args += [cs, sn]
    return pl.pallas_call(
        body, grid=(n // tm,), in_specs=in_specs,
        out_specs=[pl.BlockSpec((tm, D), lambda i: (i, 0)), _vec(HD), _vec(HD)],
        out_shape=[jax.ShapeDtypeStruct((n, D), BF16), jax.ShapeDtypeStruct((1, HD), F32),
                   jax.ShapeDtypeStruct((1, HD), F32)],
        name=name, compiler_params=_params("arbitrary"))(*args)


def _conv_gate_fwd(p, o, conv_w, *, name, tm=256):
    n = p.shape[0]
    ni = n // tm

    def body(gb_ref, gc_ref, gcp_ref, gcn_ref, xi_ref, xip_ref, xin_ref, o_ref, w_ref, cat_ref):
        i = pl.program_id(0)
        hext = _ext(gcp_ref, gc_ref, gcn_ref, i, ni) * _ext(xip_ref, xi_ref, xin_ref, i, ni)
        cat_ref[:, 0:AW] = o_ref[...].astype(BF16)
        cat_ref[:, AW:D] = (gb_ref[...] * _conv3(hext, w_ref, tm)).astype(BF16)

    gcp, gcn = _halo_specs(tm, CW, n, colblk=3)
    xip, xin = _halo_specs(tm, CW, n, colblk=4)
    return pl.pallas_call(
        body, grid=(ni,),
        in_specs=[pl.BlockSpec((tm, CW), lambda i: (i, 2)), pl.BlockSpec((tm, CW), lambda i: (i, 3)), gcp, gcn,
                  pl.BlockSpec((tm, CW), lambda i: (i, 4)), xip, xin, pl.BlockSpec((tm, AW), lambda i: (i, 0)),
                  pl.BlockSpec((3, CW), lambda i: (0, 0))],
        out_specs=pl.BlockSpec((tm, D), lambda i: (i, 0)), out_shape=jax.ShapeDtypeStruct((n, D), BF16),
        name=name, compiler_params=_params("parallel"))(p, p, p, p, p, p, p, o, conv_w)


def _conv_gate_bwd(dcat, p, conv_w, *, name, tm=256):
    n = p.shape[0]
    ni = n // tm

    def body(dc_ref, dcp_ref, dcn_ref, gb_ref, gbp_ref, gbn_ref, gc_ref, gcp_ref, gcn_ref, xi_ref, xip_ref, xin_ref,
             w_ref, dp_ref, dw_ref):
        i = pl.program_id(0)
        gcext = _ext(gcp_ref, gc_ref, gcn_ref, i, ni)
        xiext = _ext(xip_ref, xi_ref, xin_ref, i, ni)
        hext = gcext * xiext
        dcv = _ext(dcp_ref, dc_ref, dcn_ref, i, ni) * _ext(gbp_ref, gb_ref, gbn_ref, i, ni)
        dp_ref[:, 0:CW] = (dc_ref[...] * _conv3(hext, w_ref, tm)).astype(BF16)
        dh = _sh(dcv, 1, tm) * w_ref[0:1, :] + _sh(dcv, 0, tm) * w_ref[1:2, :] + _sh(dcv, -1, tm) * w_ref[2:3, :]
        dp_ref[:, CW:2 * CW] = (dh * xi_ref[...]).astype(BF16)
        dp_ref[:, 2 * CW:3 * CW] = (dh * gc_ref[...]).astype(BF16)
        dcv_t = dcv[HALO:HALO + tm]
        dw = jnp.concatenate([_colsum(dcv_t * _sh(hext, -1, tm)), _colsum(dcv_t * _sh(hext, 0, tm)),
                              _colsum(dcv_t * _sh(hext, 1, tm))], axis=0)
        _acc_out(dw_ref, i, dw)

    def trio(colblk):
        prev, nxt = _halo_specs(tm, CW, n, colblk=colblk)
        return [pl.BlockSpec((tm, CW), lambda i: (i, colblk)), prev, nxt]

    return pl.pallas_call(
        body, grid=(ni,), in_specs=trio(1) + trio(2) + trio(3) + trio(4) + [pl.BlockSpec((3, CW), lambda i: (0, 0))],
        out_specs=[pl.BlockSpec((tm, 3 * CW), lambda i: (i, 0)), pl.BlockSpec((3, CW), lambda i: (0, 0))],
        out_shape=[jax.ShapeDtypeStruct((n, 3 * CW), BF16), jax.ShapeDtypeStruct((3, CW), F32)],
        name=name, compiler_params=_params("arbitrary"))(dcat, dcat, dcat, p, p, p, p, p, p, p, p, p, conv_w)


def _attn_fwd(q, k, v, *, name, bq=512, sub=256):
    n = q.shape[1]
    t = k.shape[1]
    bq = min(bq, n)
    sub = min(sub, 2 * bq)

    def body(q_ref, k_ref, v_ref, o_ref, lse_ref):
        q2 = q_ref[...].reshape(2 * bq, HD)
        outs, lses = [], []
        for r0 in range(0, 2 * bq, sub):
            s = lax.dot_general(q2[r0:r0 + sub], k_ref[0], _NT, preferred_element_type=F32)
            m = jnp.max(s, axis=-1, keepdims=True)
            pv = jnp.exp2(s - m)
            l = jnp.sum(pv, axis=-1, keepdims=True)
            outs.append(jnp.dot(pv.astype(BF16), v_ref[0], preferred_element_type=F32) / l)
            lses.append(m + jnp.log2(l))
        out = jnp.concatenate(outs, axis=0)
        o_ref[:, 0:HD] = out[0:bq]
        o_ref[:, HD:2 * HD] = out[bq:2 * bq]
        lse_ref[...] = jnp.concatenate(lses, axis=0).reshape(2, bq, 1)

    kspec = pl.BlockSpec((1, t, HD), lambda h, i: (h, 0, 0))
    return pl.pallas_call(
        body, grid=(NKV, n // bq),
        in_specs=[pl.BlockSpec((2, bq, HD), lambda h, i: (h, i, 0)), kspec, kspec],
        out_specs=[pl.BlockSpec((bq, 2 * HD), lambda h, i: (i, h)), pl.BlockSpec((2, bq, 1), lambda h, i: (h, i, 0))],
        out_shape=[jax.ShapeDtypeStruct((n, AW), F32), jax.ShapeDtypeStruct((NQ, n, 1), F32)],
        name=name, compiler_params=_params("parallel", "parallel"))(q, k, v)


def _attn_bwd(q, k, v, dcat, o, lse, *, name, bq=256):
    n = q.shape[1]
    t = k.shape[1]
    bq = min(bq, n)

    def body(q_ref, k_ref, v_ref, dc_ref, o_ref, lse_ref, dq_ref, dk_ref, dv_ref):
        @pl.when(pl.program_id(1) == 0)
        def _():
            dk_ref[...] = jnp.zeros_like(dk_ref)
            dv_ref[...] = jnp.zeros_like(dv_ref)

        q2 = q_ref[...].reshape(2 * bq, HD)
        do_f = jnp.concatenate([dc_ref[:, 0:HD], dc_ref[:, HD:2 * HD]], axis=0)
        o_f = jnp.concatenate([o_ref[:, 0:HD], o_ref[:, HD:2 * HD]], axis=0)
        delta = jnp.sum(do_f * o_f, axis=-1, keepdims=True)
        do2 = do_f.astype(BF16)
        s = lax.dot_general(q2, k_ref[0], _NT, preferred_element_type=F32)
        pv = jnp.exp2(s - lse_ref[...].reshape(2 * bq, 1))
        dp = lax.dot_general(do2, v_ref[0], _NT, preferred_element_type=F32)
        ds = (pv * (dp - delta)).astype(BF16)
        dq_ref[...] = (jnp.dot(ds, k_ref[0], preferred_element_type=F32) * _SCALE).reshape(2, bq, HD)
        dk_ref[0] += lax.dot_general(ds, q2, _TN, preferred_element_type=F32) * _LN2
        dv_ref[0] += lax.dot_general(pv.astype(BF16), do2, _TN, preferred_element_type=F32)

    qspec = pl.BlockSpec((2, bq, HD), lambda h, i: (h, i, 0))
    kspec = pl.BlockSpec((1, t, HD), lambda h, i: (h, 0, 0))
    sspec = pl.BlockSpec((2, bq, 1), lambda h, i: (h, i, 0))
    cspec = pl.BlockSpec((bq, 2 * HD), lambda h, i: (i, h))
    return pl.pallas_call(
        body, grid=(NKV, n // bq), in_specs=[qspec, kspec, kspec, cspec, cspec, sspec], out_specs=[qspec, kspec, kspec],
        out_shape=[jax.ShapeDtypeStruct((NQ, n, HD), F32), jax.ShapeDtypeStruct((NKV, t, HD), F32),
                   jax.ShapeDtypeStruct((NKV, t, HD), F32)],
        name=name, compiler_params=_params("parallel", "arbitrary"))(q, k, v, dcat, o, lse)


def _window_sums(ext, w):
    s, step = ext, 1
    while step < w:
        s = s + _roll_rows(s, step)
        step *= 2
    return s


def _pool_counts(i, tm, n, w, rows, first):
    t = i * tm - HALO + first + lax.broadcasted_iota(jnp.int32, (rows, 1), 0)
    lo = jnp.clip(t - w // 2, 0, n)
    hi = jnp.clip(t + w - w // 2, 0, n)
    return jnp.maximum(hi - lo, 1).astype(F32)


def _norm_mod_ext(xext, gain_ref, sc_ref, sh_ref, i, tm, n):
    rows = xext.shape[0]
    t = i * tm - HALO + lax.broadcasted_iota(jnp.int32, (rows, 1), 0)
    inside = (t >= 0) & (t < n)
    r = lax.rsqrt(jnp.mean(xext * xext, axis=-1, keepdims=True) + EPS)
    xh = xext * r
    a = (xh * gain_ref[...]) * (1.0 + sc_ref[...]) + sh_ref[...]
    return jnp.where(inside, a, 0.0), r, xh


def _pool_fwd(x, y, g, gain, sc, sh, pool_w, *, name, tm=256):
    n, d = x.shape
    ni = n // tm

    def body(x_ref, xp_ref, xn_ref, y_ref, yp_ref, yn_ref, g_ref, gain_ref, sc_ref, sh_ref, w_ref, xo_ref, o_ref):
        i = pl.program_id(0)
        xext = _ext(xp_ref, x_ref, xn_ref, i, ni) + g_ref[...] * _ext(yp_ref, y_ref, yn_ref, i, ni)
        xo_ref[...] = xext[HALO:HALO + tm]
        aext, _, _ = _norm_mod_ext(xext, gain_ref, sc_ref, sh_ref, i, tm, n)
        for gi, w in enumerate(POOL_WINDOWS):
            ag = aext[:, gi * PG:(gi + 1) * PG]
            mean = _sh(_window_sums(ag, w), -(w // 2), tm) / _pool_counts(i, tm, n, w, tm, HALO)
            pooled = mean - ag[HALO:HALO + tm]
            o_ref[:, gi * PG:(gi + 1) * PG] = jnp.dot(pooled.astype(BF16), w_ref[gi], preferred_element_type=F32)

    row = pl.BlockSpec((tm, d), lambda i: (i, 0))
    prev, nxt = _halo_specs(tm, d, n)
    return pl.pallas_call(
        body, grid=(ni,),
        in_specs=[row, prev, nxt, row, prev, nxt, _vec(d), _vec(d), _vec(d), _vec(d),
                  pl.BlockSpec((4, PG, PG), lambda i: (0, 0, 0))],
        out_specs=[row, row], out_shape=[jax.ShapeDtypeStruct((n, d), F32)] * 2,
        name=name, compiler_params=_params("parallel"))(x, x, x, y, y, y, g, gain, sc, sh, pool_w)


def _pool_bwd(dxo, mixed, x, g, scale, gain, sc, sh, pool_w, zprev, gprev, *, name, tm=256):
    n, d = x.shape
    ni = n // tm

    def body(dx_ref, dxp_ref, dxn_ref, mx_ref, x_ref, xp_ref, xn_ref, g_ref, s_ref, gain_ref, sc_ref, sh_ref, w_ref,
             zp_ref, gp_ref, dxi_ref, dw_ref, dg_ref, dsl_ref, dsh_ref, dsc_ref, dgn_ref, dzp_ref, dgp_ref):
        i = pl.program_id(0)

        @pl.when(i == 0)
        def _():
            dw_ref[...] = jnp.zeros_like(dw_ref)

        dxo_t = dx_ref[...]
        mixed_t = mx_ref[...]
        dy_t = dxo_t * g_ref[...]
        _acc_out(dg_ref, i, _colsum(dxo_t * (mixed_t * s_ref[...])))
        _acc_out(dsl_ref, i, _colsum(dy_t * mixed_t))
        dmixed = (_ext(dxp_ref, dx_ref, dxn_ref, i, ni) * g_ref[...]) * s_ref[...]
        xext = _ext(xp_ref, x_ref, xn_ref, i, ni)
        aext, rext, xhext = _norm_mod_ext(xext, gain_ref, sc_ref, sh_ref, i, tm, n)
        rows = tm + 2 * HALO
        da_parts = []
        for gi, w in enumerate(POOL_WINDOWS):
            sl = slice(gi * PG, (gi + 1) * PG)
            ag = aext[:, sl]
            mean = _sh(_window_sums(ag, w), -(w // 2), tm) / _pool_counts(i, tm, n, w, tm, HALO)
            pooled = (mean - ag[HALO:HALO + tm]).astype(BF16)
            dmg = dmixed[:, sl].astype(BF16)
            dw_ref[gi] += lax.dot_general(pooled, dmixed[HALO:HALO + tm, sl].astype(BF16), _TN,
                                          preferred_element_type=F32)
            dpl = lax.dot_general(dmg, w_ref[gi], _NT, preferred_element_type=F32)
            e = dpl / _pool_counts(i, tm, n, w, rows, 0)
            da_parts.append(_sh(_window_sums(e, w), 1 - w // 2, tm) - dpl[HALO:HALO + tm])
        da = jnp.concatenate(da_parts, axis=1)
        r = rext[HALO:HALO + tm]
        xh = xhext[HALO:HALO + tm]
        nrm = xh * gain_ref[...]
        dn = da * (1.0 + sc_ref[...])
        dxh = dn * gain_ref[...]
        dxi = dxo_t + r * (dxh - xh * jnp.mean(dxh * xh, axis=-1, keepdims=True))
        dxi_ref[...] = dxi
        _acc_out(dsh_ref, i, _colsum(da))
        _acc_out(dsc_ref, i, _colsum(da * nrm))
        _acc_out(dgn_ref, i, _colsum(dn * xh))
        dzp_ref[...] = (dxi * gp_ref[...]).astype(BF16)
        _acc_out(dgp_ref, i, _colsum(dxi * zp_ref[...]))

    row = pl.BlockSpec((tm, d), lambda i: (i, 0))
    prev, nxt = _halo_specs(tm, d, n)
    wspec = pl.BlockSpec((4, PG, PG), lambda i: (0, 0, 0))
    vshape = jax.ShapeDtypeStruct((1, d), F32)
    return pl.pallas_call(
        body, grid=(ni,),
        in_specs=[row, prev, nxt, row, row, prev, nxt] + [_vec(d)] * 5 + [wspec, row, _vec(d)],
        out_specs=[row, wspec] + [_vec(d)] * 5 + [row, _vec(d)],
        out_shape=[jax.ShapeDtypeStruct((n, d), F32), jax.ShapeDtypeStruct((4, PG, PG), F32)] + [vshape] * 5
        + [jax.ShapeDtypeStruct((n, d), BF16), vshape],
        name=name, compiler_params=_params("arbitrary"))(dxo, dxo, dxo, mixed, x, x, x, g, scale, gain, sc, sh, pool_w,
                                                         zprev, gprev)


def _adamw(gparts_list, w, m, v, *, name, silu_grad_of=None):
    nl = len(gparts_list)
    nparts, r, c = gparts_list[0].shape
    tr = _pick(r, (256, 128, 64, 32, 16, 8))
    has_c = silu_grad_of is not None

    def body(*refs):
        gp_refs = refs[:nl]
        it = iter(refs[nl:])
        w_ref, m_ref, v_ref = next(it), next(it), next(it)
        c_ref = next(it) if has_c else None
        g_ref, d_ref, mo_ref, vo_ref = next(it), next(it), next(it), next(it)
        layer = pl.program_id(0)

        def update(gp_ref):
            g = gp_ref[0].astype(F32)
            for p in range(1, nparts):
                g = g + gp_ref[p].astype(F32)
            if has_c:
                cv = c_ref[0]
                sg = _sigmoid(cv)
                g = g * (sg * (1.0 + cv * (1.0 - sg)))
            g_ref[0] = g
            mn = ADAM_B1 * m_ref[0] + (1.0 - ADAM_B1) * g
            vn = ADAM_B2 * v_ref[0] + (1.0 - ADAM_B2) * (g * g)
            m_hat = mn / (1.0 - ADAM_B1 ** ADAM_STEP)
            v_hat = vn / (1.0 - ADAM_B2 ** ADAM_STEP)
            d_ref[0] = -ADAM_LR * (m_hat / (jnp.sqrt(v_hat) + ADAM_EPS) + ADAM_WD * w_ref[0])
            mo_ref[0] = mn
            vo_ref[0] = vn

        if nl == 1:
            update(gp_refs[0])
        else:
            for li in range(nl):
                pl.when(layer == li)(functools.partial(update, gp_refs[li]))

    row = pl.BlockSpec((1, tr, c), lambda l, i: (l, i, 0))
    in_specs = [pl.BlockSpec((nparts, tr, c), lambda l, i, li=li: (0, jnp.where(l == li, i, 0), 0)) for li in range(nl)]
    in_specs += [row, row, row]
    args = list(gparts_list) + [w, m, v]
    if has_c:
        in_specs.append(row)
        args.append(silu_grad_of)
    return pl.pallas_call(
        body, grid=(nl, r // tr), in_specs=in_specs, out_specs=[row] * 4,
        out_shape=[jax.ShapeDtypeStruct((nl, r, c), F32)] * 4, name=name,
        compiler_params=_params("arbitrary", "arbitrary"))(*args)


def _adamw_nd(gparts, w, m, v, *, name, silu_grad_of=None):
    shape = w.shape
    c = shape[-1]
    if isinstance(gparts, (list, tuple)):
        nl = len(gparts)
        r = math.prod(shape[1:-1])
    else:
        nl = 1
        r = math.prod(shape[:-1]) if len(shape) > 1 else 1
        gparts = [gparts]
    rs = lambda a: a.reshape(nl, r, c)
    res = _adamw([gp.reshape(gp.shape[0], r, c) for gp in gparts], rs(w), rs(m), rs(v), name=name,
                 silu_grad_of=None if silu_grad_of is None else rs(silu_grad_of))
    return [a.reshape(shape) for a in res]


def _place():
    return lax.axis_index("x"), lax.axis_index("y"), lax.axis_index("c")


def _all_gather(arrs, *, name):
    k_arr = len(arrs)

    def body(*refs):
        ins = refs[:k_arr]
        outs = refs[k_arr:2 * k_arr]
        send_sems, recv_sems, local_sems = refs[2 * k_arr:]
        x, y, c = _place()
        me, sibling = (x, y, c), (x, y, 1 - c)
        chips = [(1 - x, y), (x, 1 - y), (1 - x, 1 - y)]

        def slot(a, px, py, pc):
            return outs[a].at[4 * px + 2 * py + pc]

        def copy(a, s, block, to, src=None):
            return pltpu.make_async_remote_copy(
                src_ref=slot(a, *block) if src is None else src, dst_ref=slot(a, *block),
                send_sem=send_sems.at[a, s], recv_sem=recv_sems.at[a, s], device_id=to, device_id_type=MESH)

        mine = [pltpu.make_async_copy(ins[a], slot(a, *me), local_sems.at[a]) for a in range(k_arr)]
        for cp in mine:
            cp.start()
        first = []
        for a in range(k_arr):
            first.append(copy(a, 0, me, sibling, src=ins[a]))
            first += [copy(a, 1 + j, me, (*chip, c), src=ins[a]) for j, chip in enumerate(chips)]
        for cp in first:
            cp.start()
        passed = []
        for j, chip in enumerate(chips):
            for a in range(k_arr):
                copy(a, 1 + j, (*chip, c), me).wait_recv()
                fw = copy(a, 4 + j, (*chip, c), sibling)
                fw.start()
                passed.append(fw)
        for a in range(k_arr):
            copy(a, 0, sibling, me).wait_recv()
            for j, chip in enumerate(chips):
                copy(a, 4 + j, (*chip, 1 - c), me).wait_recv()
        for cp in first + passed:
            cp.wait_send()
        for cp in mine:
            cp.wait()

    any_spec = pl.BlockSpec(memory_space=pl.ANY)
    return pl.pallas_call(
        body, in_specs=[any_spec] * k_arr, out_specs=[any_spec] * k_arr,
        out_shape=[jax.ShapeDtypeStruct((NDEV,) + a.shape, a.dtype) for a in arrs],
        scratch_shapes=[pltpu.SemaphoreType.DMA((k_arr, 7)), pltpu.SemaphoreType.DMA((k_arr, 7)),
                        pltpu.SemaphoreType.DMA((k_arr,))],
        name=name)(*arrs)


_HBM = pl.BlockSpec(memory_space=pltpu.HBM)
_SEM = pl.BlockSpec(memory_space=pltpu.SEMAPHORE)
_EFFECT = pltpu.SideEffectType.DATAFLOW_SIDE_EFFECTING


def _peers(x, y, c):
    return [(x ^ (rel >> 2), y ^ ((rel >> 1) & 1), c ^ (rel & 1)) for rel in range(1, NDEV)]


def _exchange_copies(srcs, lands, send_sems, recv_sems, scatter):
    x, y, c = _place()
    me = 4 * x + 2 * y + c
    copies = []
    for r, (px, py, pc) in enumerate(_peers(x, y, c)):
        peer = 4 * px + 2 * py + pc
        for a in range(len(srcs)):
            copies.append(pltpu.make_async_remote_copy(
                src_ref=srcs[a].at[peer] if scatter else srcs[a], dst_ref=lands[a].at[me],
                send_sem=send_sems.at[7 * a + r], recv_sem=recv_sems.at[7 * a + r], device_id=(px, py, pc),
                device_id_type=MESH))
    return copies


def _exchange_start(arrs, *, scatter, name):
    k_arr = len(arrs)
    land_shapes = [a.shape if scatter else (NDEV,) + a.shape for a in arrs]
    lands = [pltpu.with_memory_space_constraint(lax.empty(s, a.dtype), pltpu.HBM) for s, a in zip(land_shapes, arrs)]
    srcs = [pltpu.with_memory_space_constraint(a, pltpu.HBM) for a in arrs]

    def body(*refs):
        src_refs, land_refs = refs[:k_arr], refs[k_arr:2 * k_arr]
        send_sems, recv_sems = refs[2 * k_arr], refs[2 * k_arr + 1]
        token = refs[-1]
        for cp in _exchange_copies(src_refs, land_refs, send_sems, recv_sems, scatter):
            cp.start()
        token[...] = jnp.zeros_like(token)

    out_shape = ([pltpu.SemaphoreType.DMA((7 * k_arr,)), pltpu.SemaphoreType.DMA((7 * k_arr,))]
                 + [pltpu.HBM(a.shape, a.dtype) for a in arrs] + [pltpu.HBM(s, a.dtype) for s, a in zip(land_shapes, arrs)]
                 + [jax.ShapeDtypeStruct((8, 128), F32)])
    res = pl.pallas_call(
        body, name=name, out_shape=out_shape, in_specs=[_HBM] * (2 * k_arr),
        out_specs=[_SEM, _SEM] + [_HBM] * (2 * k_arr) + [pl.BlockSpec(memory_space=pltpu.VMEM)],
        input_output_aliases={i: 2 + i for i in range(2 * k_arr)},
        compiler_params=pltpu.CompilerParams(has_side_effects=_EFFECT))(*srcs, *lands)
    return dict(send=res[0], recv=res[1], srcs=list(res[2:2 + k_arr]), lands=list(res[2 + k_arr:2 + 2 * k_arr]),
                token=res[-1], scatter=scatter)


def _exchange_wait(handle, after, *, name):
    k_arr = len(handle["srcs"])
    scatter = handle["scatter"]

    def body(*refs):
        src_refs, land_refs = refs[:k_arr], refs[k_arr:2 * k_arr]
        send_sems, recv_sems = refs[2 * k_arr], refs[2 * k_arr + 1]
        x, y, c = _place()
        me = 4 * x + 2 * y + c
        for r, (px, py, pc) in enumerate(_peers(x, y, c)):
            peer = 4 * px + 2 * py + pc
            for a in range(k_arr):
                cp = pltpu.make_async_remote_copy(
                    src_ref=src_refs[a].at[peer] if scatter else src_refs[a], dst_ref=land_refs[a].at[peer],
                    send_sem=send_sems.at[7 * a + r], recv_sem=recv_sems.at[7 * a + r], device_id=(x, y, c),
                    device_id_type=MESH)
                cp.wait_send()
                cp.wait_recv()

    arrs = handle["srcs"] + handle["lands"]
    res = pl.pallas_call(
        body, name=name, out_shape=[pltpu.HBM(a.shape, a.dtype) for a in arrs],
        in_specs=[_HBM] * (2 * k_arr) + [_SEM, _SEM, pl.BlockSpec(memory_space=pl.ANY)],
        out_specs=[_HBM] * (2 * k_arr), input_output_aliases={i: i for i in range(2 * k_arr)},
        compiler_params=pltpu.CompilerParams(has_side_effects=_EFFECT))(*arrs, handle["send"], handle["recv"], after)
    me = 4 * lax.axis_index("x") + 2 * lax.axis_index("y") + lax.axis_index("c")
    out = []
    for src, land in zip(res[:k_arr], res[k_arr:]):
        own = lax.dynamic_index_in_dim(src, me, 0, keepdims=False) if scatter else src
        out.append(lax.dynamic_update_index_in_dim(land, own, me, 0))
    return out


def _ffn_bwd(dxo, dz, xr, f, u_gc, hmid, gain, sc, w_up, cw, w_down, tag, gate_y=None, gate_g=None):
    d_wdown = _mm_tn((hmid, dz), name=f"ffn_down_dw_{tag}")
    dug, duv, dcw, dcb = _ffn_down_glu_bwd(dz, w_down, u_gc[0], u_gc[1], cw, name=f"ffn_down_glu_bwd_{tag}")
    d_wup = _mm_tn((dug, f), blocks=2, block=0, name=f"ffn_up_dwg_{tag}")
    d_wup = _mm_tn((duv, f), blocks=2, block=1, into=d_wup, name=f"ffn_up_dwv_{tag}")
    gated = gate_y is not None
    res = _mm_w_ep([dug, duv], w_up, _ep_norm_bwd(gated), [xr, dxo] + ([gate_y] if gated else []),
                   [gain, sc] + ([gate_g] if gated else []), [F32] + ([BF16] if gated else []),
                   [D] * (4 if gated else 3), name=f"ffn_up_dx_norm_bwd_{tag}")
    n_out = 2 if gated else 1
    return res[:n_out], res[n_out:], (d_wup, d_wdown, dcw, dcb)


def _split6(mod):
    return [mod[j * D:(j + 1) * D][None, :] for j in range(6)]


def _row(v):
    return v.reshape(1, -1)


def kernel(x, c, ctx, c_ctx, ada_w, ada_b, mix_norm, ffn_norm, even_w_in, even_q_gain, even_k_gain, even_conv_w, even_w_out, odd_pool_w, odd_pool_scale, ffn_w_up, ffn_conv_w, ffn_conv_b, ffn_w_down, loss_target, m_c_ctx, m_ada_w, m_ada_b, m_mix_norm, m_ffn_norm, m_even_w_in, m_even_q_gain, m_even_k_gain, m_even_conv_w, m_even_w_out, m_odd_pool_w, m_odd_pool_scale, m_ffn_w_up, m_ffn_conv_w, m_ffn_conv_b, m_ffn_w_down, v_c_ctx, v_ada_w, v_ada_b, v_mix_norm, v_ffn_norm, v_even_w_in, v_even_q_gain, v_even_k_gain, v_even_conv_w, v_even_w_out, v_odd_pool_w, v_odd_pool_scale, v_ffn_w_up, v_ffn_conv_w, v_ffn_conv_b, v_ffn_w_down):
    n = x.shape[1]
    lc = ctx.shape[1]
    me = 4 * lax.axis_index("x") + 2 * lax.axis_index("y") + lax.axis_index("c")
    xs, ctxs, tgt = x[0], ctx[0], loss_target[0]
    acols = ada_w.shape[2]

    small = jnp.concatenate([even_conv_w.reshape(-1), ffn_conv_w.reshape(-1), odd_pool_scale.reshape(-1)])
    nsmall = small.shape[0]
    small = jnp.pad(small, (0, (-nsmall) % 1024)).reshape(-1, 128)
    c_rows = jnp.pad(c, ((0, 7), (0, 0)))
    tr = lambda a: jnp.swapaxes(a, -1, -2)
    g_c, g_win, g_small = _all_gather([c_rows, tr(even_w_in[0]).astype(BF16), small], name="gather_first")
    w_in_t = g_win.reshape(-1, D)
    g_small = g_small.reshape(NDEV, -1)
    ecw = even_conv_w.shape[2]
    fcw = ffn_conv_w.shape[2]
    conv_w = g_small[:, :3 * ecw].reshape(NDEV, 3, ecw).transpose(1, 0, 2).reshape(3, CW)
    o1 = 3 * ecw
    fconv_w = g_small[:, o1:o1 + 6 * fcw].reshape(NDEV, 2, 3, fcw).transpose(1, 2, 0, 3).reshape(2, 3, DFF)
    o2 = o1 + 6 * fcw
    pool_scale = g_small[:, o2:o2 + D // NDEV].reshape(1, D)

    mraw = jnp.concatenate([g_c[:, 0, :], c_ctx[None, :], jnp.zeros((7, D), F32)], axis=0)
    my_bias = lax.dynamic_slice_in_dim(ada_b, me * acols, acols, axis=1)
    modp = jnp.stack([_mm(mraw, ada_w[l], silu_a=True, bias=my_bias[l:l + 1], name=f"ada_proj_{l}", tm=16, tn=256)
                      for l in range(2)])
    (g_mod,) = _all_gather([modp], name="gather_mod")
    mod_rows = g_mod.transpose(1, 2, 0, 3).reshape(2, 16, 6 * D)
    late_shards = [even_w_out[0].astype(BF16), odd_pool_w[0].astype(BF16), tr(ffn_w_up[0]).astype(BF16),
                   tr(ffn_w_up[1]).astype(BF16), ffn_w_down[0].astype(BF16), ffn_w_down[1].astype(BF16)]
    late_shards, mod_rows = lax.optimization_barrier((late_shards, mod_rows))
    h_weights = _exchange_start(late_shards, scatter=False, name="weights_start")
    mod_rows = mod_rows + h_weights["token"][0, 0]
    mod = lax.dynamic_index_in_dim(mod_rows, me, axis=1, keepdims=False)
    sh1, sc1, g1, sh2, sc2, g2 = _split6(mod[0])
    sh1b, sc1b, g1b, sh2b, sc2b, g2b = _split6(mod[1])
    csh1, csc1 = _split6(mod_rows[0, 8])[:2]
    mixn = [_row(mix_norm[l]) for l in range(2)]
    ffnn = [_row(ffn_norm[l]) for l in range(2)]
    qg, kg = _row(even_q_gain[0]), _row(even_k_gain[0])
    fcb = [_row(ffn_conv_b[l]) for l in range(2)]

    cs_t, sn_t = _rope_tables(n)
    a_lat = _norm_mod(xs, mixn[0], sc1, sh1, name="mix0_norm")
    a_ctx = _norm_mod(ctxs, mixn[0], csc1, csh1, name="mix0_norm_ctx")
    p_lat = _mm_w(a_lat, w_in_t, tb=True, name="in_proj")
    p_ctx = _mm(a_ctx, w_in_t[AW:AW + 4 * HD], tb=True, name="in_proj_ctx", tm=256, tn=512, tk=1024)
    kv_ctx = _qkv_prep(p_ctx, qg, kg, None, None, has_q=False, kv_col=0, kv_rows=lc + n, name="qkv_prep_ctx")
    q_r, k_all, v_all = _qkv_prep(p_lat, qg, kg, cs_t, sn_t, has_q=True, kv_col=1, kv_rows=lc + n, kv_row_off=lc,
                                  kv_into=kv_ctx, name="qkv_prep")
    o_attn, lse = _attn_fwd(q_r, k_all, v_all, name="attn_fwd")
    cat = _conv_gate_fwd(p_lat, o_attn, conv_w, name="conv_gate")
    g_wout, g_pool, g_up0, g_up1, g_down0, g_down1 = _exchange_wait(h_weights, cat, name="weights_wait")
    w_out = g_wout.reshape(D, D)
    pool_w = g_pool.transpose(1, 0, 2, 3).reshape(4, PG, PG)
    w_up_t = [g_up0.reshape(2 * DFF, D), g_up1.reshape(2 * DFF, D)]
    w_up = [w.T for w in w_up_t]
    w_down = [g_down0.reshape(DFF, D), g_down1.reshape(DFF, D)]
    y0, x1, f0 = _mm_w_ep(cat, w_out, _ep_resid_norm, [xs], [g1, ffnn[0], sc2, sh2], [F32, F32, BF16], [],
                          tm=512, name="out_proj_norm")[:3]
    *u0, h0 = _ffn_up_glu(f0, w_up[0], fconv_w[0], fcb[0], name="ffn_up_glu_l0")
    z0 = _mm_w(h0, w_down[0], name="ffn_down_l0")

    x2, mixed = _pool_fwd(x1, z0, g2, mixn[1], sc1b, sh1b, pool_w, name="pool_fwd")
    x3, f1 = _norm_mod(x2, ffnn[1], sc2b, sh2b, y=mixed, g=g1b, ymul=pool_scale, name="ffn_norm_l1")
    *u1, h1 = _ffn_up_glu(f1, w_up[1], fconv_w[1], fcb[1], name="ffn_up_glu_l1")
    dx4, dz1, loss_part, dg2b = _mm_w_ep(h1, w_down[1], _ep_loss(D), [x3, tgt], [g2b], [F32, BF16], [128, D],
                                         tm=512, name="ffn_down_loss")

    (dx3,), (dsh2b, dsc2b, dffn1), (dup1, ddown1, dfcw1, dfcb1) = _ffn_bwd(
        dx4, dz1, x3, f1, u1, h1, ffnn[1], sc2b, w_up_t[1], fconv_w[1], w_down[1], "l1")
    dx2, dpool_w, dg1b, dpscale, dsh1b, dsc1b, dmix1, dz0, dg2 = _pool_bwd(
        dx3, mixed, x2, g1b, pool_scale, mixn[1], sc1b, sh1b, pool_w, z0, g2, name="pool_bwd")

    s_pool = dpool_w.astype(BF16).reshape(4, NDEV, PG // NDEV, PG).transpose(1, 0, 2, 3)
    h_g1 = _exchange_start([s_pool, dup1.reshape(NDEV, -1, D), ddown1.reshape(NDEV, DFF // NDEV, D)], scatter=True,
                           name="grads1_start")

    (dx1, dy0), (dsh2, dsc2, dffn0, dg1), (dup0, ddown0, dfcw0, dfcb0) = _ffn_bwd(
        dx2, dz0, x1, f0, u0, h0, ffnn[0], sc2, w_up_t[0], fconv_w[0] + h_g1["token"][0, 0], w_down[0], "l0",
        gate_y=y0, gate_g=g1)
    h_g0 = _exchange_start([dup0.reshape(NDEV, -1, D), ddown0.reshape(NDEV, DFF // NDEV, D)], scatter=True,
                           name="grads0_start")
    dcat = _mm_w(dy0, w_out, tb=True, name="out_proj_dx", tm=512)
    d_wout = _mm_tn((cat, dy0), name="out_proj_dw")
    dp_conv, dconv_w = _conv_gate_bwd(dcat, p_lat, conv_w + h_g0["token"][0, 0], name="conv_gate_bwd")
    dq_r, dk_all, dv_all = _attn_bwd(q_r, k_all, v_all, dcat, o_attn, lse, name="attn_bwd")
    dp_qkv, dqg_l, dkg_l = _qkv_bwd(p_lat, dq_r, dk_all, dv_all, qg, kg, cs_t, sn_t, has_q=True, kv_col=1,
                                    kv_row_off=lc, name="qkv_bwd")
    dp_ctx, _zero_qg, dkg_c = _qkv_bwd(p_ctx, None, dk_all, dv_all, qg, kg, None, None, has_q=False, kv_col=0,
                                       kv_row_off=0, name="qkv_bwd_ctx")
    da_ctx = _mm(dp_ctx, w_in_t[:D], name="in_proj_dx_ctx", tm=256, tn=512, tk=1024)
    d_win_qkv = _mm_tn([(dp_qkv, a_lat), (dp_ctx, a_ctx)], name="in_proj_dw_qkv")
    d_win_conv = _mm_tn((dp_conv, a_lat), name="in_proj_dw_conv")
    d_win_t = jnp.concatenate([d_win_qkv, d_win_conv], axis=0)
    grad_x, dsh1, dsc1, dmix0 = _mm_w_ep([dp_qkv, dp_conv], w_in_t, _ep_norm_bwd(False), [xs, dx1], [mixn[0], sc1],
                                         [F32], [D] * 3, tm=512, name="in_proj_dx_norm_bwd")
    _dctx, dcsh1, dcsc1, dmix0c = _norm_mod_bwd(da_ctx, ctxs, mixn[0], csc1, name="mix0_norm_bwd_ctx")

    z1k = jnp.zeros((1, D), F32)
    pack = jnp.concatenate(
        [v.reshape(-1) for v in (dsh1, dsc1, dg1, dsh2, dsc2, dg2, dsh1b, dsc1b, dg1b, dsh2b, dsc2b, dg2b,
                                 dcsh1, dcsc1, z1k, z1k, z1k, z1k,
                                 dmix0, dmix1, dmix0c, z1k, dffn0, dffn1, dqg_l, dkg_l + dkg_c,
                                 dfcb0, dfcb1, dconv_w, dfcw0, dfcw1, dpscale, loss_part[:, 0:1])])
    npack = pack.shape[0]
    pack = jnp.pad(pack, (0, (-npack) % 1024)).reshape(-1, 128)
    (g_pack,) = _all_gather([pack], name="gather_small_grads")
    gp = g_pack.reshape(NDEV, -1)
    off = [0]

    def take(size):
        seg = gp[:, off[0]:off[0] + size]
        off[0] += size
        return seg

    dmod_all = take(12 * D).reshape(NDEV, 2, 6 * D)
    dmodc_all = take(6 * D).reshape(NDEV, 1, 6 * D)
    dmix_all = take(4 * D).reshape(NDEV, 2, 2, D)
    dffn_all = take(2 * D).reshape(NDEV, 2, D)
    dqg_all = take(HD).reshape(NDEV, 1, HD)
    dkg_all = take(HD).reshape(NDEV, 1, HD)
    dfcb_all = take(2 * DFF).reshape(NDEV, 2, DFF)
    dconvw_all = take(3 * CW).reshape(NDEV, 3, CW)
    dfcw_all = take(6 * DFF).reshape(NDEV, 2, 3, DFF)
    dpscale_all = take(D).reshape(NDEV, D)
    loss_all = take(1)
    loss = loss_all[0, 0]
    for dev in range(1, NDEV):
        loss = loss + loss_all[dev, 0]

    dmodc_sum = dmodc_all[0]
    for dev in range(1, NDEV):
        dmodc_sum = dmodc_sum + dmodc_all[dev]
    my_cols = lambda a: lax.dynamic_slice_in_dim(a, me * acols, acols, axis=a.ndim - 1)
    rows0 = jnp.concatenate([my_cols(dmod_all[:, 0]), my_cols(dmodc_sum), jnp.zeros((7, acols), F32)], axis=0)
    rows1 = jnp.concatenate([my_cols(dmod_all[:, 1]), jnp.zeros((8, acols), F32)], axis=0)
    d_ada = jnp.stack([_mm(mraw, rows, ta=True, silu_a=True, name=f"ada_dw_{l}", tm=512, tn=256, tk=16)
                       for l, rows in enumerate((rows0, rows1))])
    dscc_part = _mm(rows0, ada_w[0], tb=True, name="ada_dcctx", tm=16, tn=512, tk=256)
    (g_dscc,) = _all_gather([dscc_part[8:16]], name="gather_dcctx")

    attn_shards = [d_win_t.reshape(NDEV, -1, D), d_wout.reshape(NDEV, D // NDEV, D)]
    attn_shards, g_dscc = lax.optimization_barrier((attn_shards, g_dscc))
    h_ga = _exchange_start(attn_shards, scatter=True, name="grads_attn_start")
    dmod_all = dmod_all + h_ga["token"][0, 0]

    outs = {}

    def put(nm, res):
        outs["grad_" + nm], outs["delta_" + nm], outs["new_m_" + nm], outs["new_v_" + nm] = res

    dmodc_pad = jnp.concatenate([dmodc_all, jnp.zeros_like(dmodc_all)], axis=1)
    put("ada_b", _adamw_nd(jnp.concatenate([dmod_all, dmodc_pad], axis=0), ada_b, m_ada_b, v_ada_b, name="adam_ada_b"))
    put("mix_norm", _adamw_nd(jnp.concatenate([dmix_all[:, 0], dmix_all[:, 1]], axis=0), mix_norm, m_mix_norm,
                              v_mix_norm, name="adam_mix_norm"))
    put("ffn_norm", _adamw_nd(dffn_all, ffn_norm, m_ffn_norm, v_ffn_norm, name="adam_ffn_norm"))
    put("even_q_gain", _adamw_nd(dqg_all, even_q_gain, m_even_q_gain, v_even_q_gain, name="adam_q_gain"))
    put("even_k_gain", _adamw_nd(dkg_all, even_k_gain, m_even_k_gain, v_even_k_gain, name="adam_k_gain"))
    put("ffn_conv_b", _adamw_nd(dfcb_all, ffn_conv_b, m_ffn_conv_b, v_ffn_conv_b, name="adam_ffn_conv_b"))
    my_convw = lax.dynamic_slice_in_dim(dconvw_all, me * ecw, ecw, axis=2)[:, None]
    put("even_conv_w", _adamw_nd(my_convw, even_conv_w, m_even_conv_w, v_even_conv_w, name="adam_even_conv_w"))
    my_fcw = lax.dynamic_slice_in_dim(dfcw_all, me * fcw, fcw, axis=3)
    put("ffn_conv_w", _adamw_nd(my_fcw, ffn_conv_w, m_ffn_conv_w, v_ffn_conv_w, name="adam_ffn_conv_w"))
    my_ps = lax.dynamic_slice_in_dim(dpscale_all, me * (D // NDEV), D // NDEV, axis=1)[:, None]
    put("odd_pool_scale", _adamw_nd(my_ps, odd_pool_scale, m_odd_pool_scale, v_odd_pool_scale, name="adam_pool_scale"))

    put("ada_w", _adamw_nd(d_ada[None], ada_w, m_ada_w, v_ada_w, name="adam_ada_w"))
    put("c_ctx", _adamw_nd(g_dscc[:, 0:1, :].reshape(NDEV, D), c_ctx, m_c_ctx, v_c_ctx, name="adam_c_ctx",
                           silu_grad_of=c_ctx))

    r_pool, r_up1, r_down1 = _exchange_wait(h_g1, outs["grad_ada_b"], name="grads1_wait")
    r_up0, r_down0 = _exchange_wait(h_g0, outs["grad_mix_norm"], name="grads0_wait")
    r_win, r_wout = _exchange_wait(h_ga, outs["grad_c_ctx"], name="grads_attn_wait")
    put("even_w_in", [tr(a) for a in _adamw_nd(r_win[:, None], tr(even_w_in), tr(m_even_w_in), tr(v_even_w_in),
                                               name="adam_w_in")])
    put("even_w_out", _adamw_nd(r_wout[:, None], even_w_out, m_even_w_out, v_even_w_out, name="adam_w_out"))
    put("odd_pool_w", _adamw_nd(r_pool[:, None], odd_pool_w, m_odd_pool_w, v_odd_pool_w, name="adam_pool_w"))
    put("ffn_w_up", [tr(a) for a in _adamw_nd([r_up0, r_up1], tr(ffn_w_up), tr(m_ffn_w_up), tr(v_ffn_w_up),
                                              name="adam_w_up")])
    put("ffn_w_down", _adamw_nd([r_down0, r_down1], ffn_w_down, m_ffn_w_down, v_ffn_w_down, name="adam_w_down"))

    names = ["c_ctx", "ada_w", "ada_b", "mix_norm", "ffn_norm", "even_w_in", "even_q_gain", "even_k_gain",
             "even_conv_w", "even_w_out", "odd_pool_w", "odd_pool_scale", "ffn_w_up", "ffn_conv_w", "ffn_conv_b",
             "ffn_w_down"]
    result = [loss, grad_x[None]]
    for kind in ("grad_", "delta_", "new_m_", "new_v_"):
        result += [outs[kind + nm] for nm in names]
    return tuple(result)
```

```python
import functools
import math

import jax
import jax.numpy as jnp
from jax import lax
from jax.experimental import pallas as pl
from jax.experimental.pallas import tpu as pltpu

F32 = jnp.float32
BF16 = jnp.bfloat16

D = 1024
HD = 128
NQ = 4
NKV = 2
AW = NQ * HD
CW = D - AW
DFF = 2816
GRID_W = 64
ROPE_THETA = 10000.0
POOL_WINDOWS = (2, 4, 8, 16)
PG = D // 4
EPS = 1e-6
NDEV = 8
HALO = 8
MESH = pl.DeviceIdType.MESH

ADAM_LR = 0.001
ADAM_B1 = 0.9
ADAM_B2 = 0.999
ADAM_EPS = 1e-08
ADAM_WD = 0.01
ADAM_STEP = 10


def _pick(dim, prefs):
    for p in prefs:
        if dim % p == 0:
            return p
    return dim


def _params(*sem):
    return pltpu.CompilerParams(dimension_semantics=sem)


_NT = (((1,), (1,)), ((), ()))
_TN = (((0,), (0,)), ((), ()))
_SCALE = HD ** -0.5
_QSCALE = _SCALE * math.log2(math.e)
_LN2 = math.log(2.0)


def _mm(a_list, b, *, name, ta=False, tb=False, out_dtype=F32, silu_a=False, bias=None, tm=None, tn=None, tk=None):
    if not isinstance(a_list, (list, tuple)):
        a_list = [a_list]
    na = len(a_list)
    assert not (ta and na > 1)
    if ta:
        kdim, m = a_list[0].shape
        ks = [kdim]
    else:
        m = a_list[0].shape[0]
        ks = [a.shape[1] for a in a_list]
        kdim = sum(ks)
    n = b.shape[0] if tb else b.shape[1]
    assert (b.shape[1] if tb else b.shape[0]) == kdim
    kunit = math.gcd(*ks) if na > 1 else kdim
    tm = min(tm, m) if tm else _pick(m, (512, 256, 128, 64, 32, 16, 8))
    tn = min(tn, n) if tn else _pick(n, (512, 256, 128))
    tk = min(tk, kunit) if tk else _pick(kunit, (1024, 768, 512, 256, 128))
    assert m % tm == 0 and n % tn == 0 and all(k % tk == 0 for k in ks)
    nks = [k // tk for k in ks]
    starts = [sum(nks[:i]) for i in range(na)]
    nk = sum(nks)
    has_bias = bias is not None

    def body(*refs):
        a_refs = refs[:na]
        b_ref = refs[na]
        bias_ref = refs[na + 1] if has_bias else None
        o_ref = refs[na + 1 + has_bias]
        acc = refs[-1]
        k = pl.program_id(2)

        @pl.when(k == 0)
        def _():
            acc[...] = jnp.zeros_like(acc)

        bv = b_ref[...].astype(BF16)
        dn = (((0 if ta else 1,), (1 if tb else 0,)), ((), ()))
        for idx in range(na):
            def step(idx=idx):
                av = a_refs[idx][...]
                if silu_a:
                    av = av * jax.nn.sigmoid(av)
                acc[...] += lax.dot_general(av.astype(BF16), bv, dn, preferred_element_type=F32)
            if na == 1:
                step()
            else:
                pl.when((k >= starts[idx]) & (k < starts[idx] + nks[idx]))(step)

        @pl.when(k == nk - 1)
        def _():
            r = acc[...]
            if has_bias:
                r = r + bias_ref[...]
            o_ref[...] = r.astype(o_ref.dtype)

    in_specs = []
    for idx in range(na):
        if ta:
            in_specs.append(pl.BlockSpec((tk, tm), lambda i, j, k: (k, i)))
        else:
            lo, cnt = starts[idx], nks[idx]
            in_specs.append(pl.BlockSpec((tm, tk), lambda i, j, k, lo=lo, cnt=cnt: (i, jnp.clip(k - lo, 0, cnt - 1))))
    if tb:
        in_specs.append(pl.BlockSpec((tn, tk), lambda i, j, k: (j, k)))
    else:
        in_specs.append(pl.BlockSpec((tk, tn), lambda i, j, k: (k, j)))
    args = list(a_list) + [b]
    if has_bias:
        in_specs.append(pl.BlockSpec((1, tn), lambda i, j, k: (0, j)))
        args.append(bias)
    return pl.pallas_call(
        body, grid=(m // tm, n // tn, nk), in_specs=in_specs,
        out_specs=pl.BlockSpec((tm, tn), lambda i, j, k: (i, j)),
        out_shape=jax.ShapeDtypeStruct((m, n), out_dtype),
        scratch_shapes=[pltpu.VMEM((tm, tn), F32)], name=name,
        compiler_params=_params("parallel", "parallel", "arbitrary"))(*args)


def _mm_w(a_list, w, *, name, tb=False, tm=256, out_dtype=F32):
    if not isinstance(a_list, (list, tuple)):
        a_list = [a_list]
    na = len(a_list)
    m = a_list[0].shape[0]
    ks = [a.shape[1] for a in a_list]
    offs = [sum(ks[:i]) for i in range(na)]
    n = w.shape[0] if tb else w.shape[1]
    assert (w.shape[1] if tb else w.shape[0]) == sum(ks)
    tm = min(tm, m)
    assert m % tm == 0

    def body(*refs):
        a_refs, w_ref, o_ref = refs[:na], refs[na], refs[na + 1]
        acc = None
        for idx in range(na):
            av = a_refs[idx][...].astype(BF16)
            if tb:
                part = lax.dot_general(av, w_ref[:, offs[idx]:offs[idx] + ks[idx]], _NT, preferred_element_type=F32)
            else:
                part = jnp.dot(av, w_ref[offs[idx]:offs[idx] + ks[idx], :], preferred_element_type=F32)
            acc = part if acc is None else acc + part
        o_ref[...] = acc.astype(o_ref.dtype)

    in_specs = [pl.BlockSpec((tm, k), lambda i: (i, 0)) for k in ks] + [pl.BlockSpec(w.shape, lambda i: (0, 0))]
    return pl.pallas_call(
        body, grid=(m // tm,), in_specs=in_specs, out_specs=pl.BlockSpec((tm, n), lambda i: (i, 0)),
        out_shape=jax.ShapeDtypeStruct((m, n), out_dtype), name=name, compiler_params=_params("parallel"))(*a_list, w)


def _mm_w_ep(a_list, w, epilogue, row_in, vec_in, out_dtypes, sum_widths, *, name, tb=False, tm=256, sub=256):
    if not isinstance(a_list, (list, tuple)):
        a_list = [a_list]
    na, nr, nv, no, ns = len(a_list), len(row_in), len(vec_in), len(out_dtypes), len(sum_widths)
    m = a_list[0].shape[0]
    ks = [a.shape[1] for a in a_list]
    offs = [sum(ks[:i]) for i in range(na)]
    n = w.shape[0] if tb else w.shape[1]
    assert (w.shape[1] if tb else w.shape[0]) == sum(ks)
    tm = min(tm, m)
    sub = min(sub, tm)
    assert m % tm == 0 and tm % sub == 0

    def body(*refs):
        a_refs, w_ref = refs[:na], refs[na]
        row_refs = refs[na + 1:na + 1 + nr]
        vec_refs = refs[na + 1 + nr:na + 1 + nr + nv]
        out_refs = refs[na + 1 + nr + nv:na + 1 + nr + nv + no]
        sum_refs = refs[na + 1 + nr + nv + no:]

        @pl.when(pl.program_id(0) == 0)
        def _():
            for s_ref in sum_refs:
                s_ref[...] = jnp.zeros_like(s_ref)

        vecs = [v[...] for v in vec_refs]
        for r0 in range(0, tm, sub):
            acc = None
            for idx in range(na):
                av = a_refs[idx][r0:r0 + sub, :].astype(BF16)
                if tb:
                    part = lax.dot_general(av, w_ref[:, offs[idx]:offs[idx] + ks[idx]], _NT, preferred_element_type=F32)
                else:
                    part = jnp.dot(av, w_ref[offs[idx]:offs[idx] + ks[idx], :], preferred_element_type=F32)
                acc = part if acc is None else acc + part
            outs, sums = epilogue(acc, [r[r0:r0 + sub, :] for r in row_refs], vecs)
            for o_ref, o in zip(out_refs, outs):
                o_ref[r0:r0 + sub, :] = o.astype(o_ref.dtype)
            for s_ref, s in zip(sum_refs, sums):
                s_ref[...] += s

    row = pl.BlockSpec((tm, n), lambda i: (i, 0))
    in_specs = ([pl.BlockSpec((tm, k), lambda i: (i, 0)) for k in ks] + [pl.BlockSpec(w.shape, lambda i: (0, 0))]
                + [row] * nr + [_vec(n)] * nv)
    return pl.pallas_call(
        body, grid=(m // tm,), in_specs=in_specs, out_specs=[row] * no + [_vec(sw) for sw in sum_widths],
        out_shape=[jax.ShapeDtypeStruct((m, n), dt) for dt in out_dtypes]
        + [jax.ShapeDtypeStruct((1, sw), F32) for sw in sum_widths],
        name=name, compiler_params=_params("arbitrary" if ns else "parallel"))(*a_list, w, *row_in, *vec_in)


def _ep_norm_bwd(has_gate):
    def ep(dav, rows, vecs):
        xv = rows[0]
        gain, scv = vecs[0], vecs[1]
        r = lax.rsqrt(jnp.mean(xv * xv, axis=-1, keepdims=True) + EPS)
        xh = xv * r
        nrm = xh * gain
        dn = dav * (1.0 + scv)
        dxh = dn * gain
        dx = r * (dxh - xh * jnp.mean(dxh * xh, axis=-1, keepdims=True)) + rows[1]
        outs, sums = [dx], [_colsum(dav), _colsum(dav * nrm), _colsum(dn * xh)]
        if has_gate:
            outs.append(dx * vecs[2])
            sums.append(_colsum(dx * rows[2]))
        return outs, sums
    return ep


def _ep_loss(d):
    def ep(zv, rows, vecs):
        xv, tv = rows
        gv = vecs[0]
        diff = (xv + gv * zv) - tv
        dx = diff * (1.0 / d)
        part = 0.5 * jnp.sum(jnp.mean(diff * diff, axis=-1, keepdims=True), axis=0, keepdims=True)
        return [dx, dx * gv], [jnp.broadcast_to(part, (1, 128)), _colsum(dx * zv)]
    return ep


def _ep_resid_norm(yv, rows, vecs):
    g, gain, scv, shv = vecs
    xv = rows[0] + g * yv
    r = lax.rsqrt(jnp.mean(xv * xv, axis=-1, keepdims=True) + EPS)
    return [yv, xv, ((xv * r) * gain) * (1.0 + scv) + shv], []


def _mm_tn(pairs, *, name, tk=1024, out_dtype=BF16, blocks=1, block=0, into=None):
    if not isinstance(pairs, list):
        pairs = [pairs]
    m, n = pairs[0][0].shape[1], pairs[0][1].shape[1]
    tks = [min(tk, a.shape[0]) for a, _ in pairs]
    nks = [a.shape[0] // t for (a, _), t in zip(pairs, tks)]
    assert all(a.shape[0] == b.shape[0] and a.shape[0] % t == 0 for (a, b), t in zip(pairs, tks))
    starts = [sum(nks[:i]) for i in range(len(pairs))]
    nk = sum(nks)

    def body(*refs):
        o_ref, acc = refs[-2], refs[-1]
        k = pl.program_id(0)

        @pl.when(k == 0)
        def _():
            acc[...] = jnp.zeros_like(acc)

        for idx in range(len(pairs)):
            a_ref, b_ref = refs[2 * idx], refs[2 * idx + 1]

            def step(a_ref=a_ref, b_ref=b_ref):
                acc[...] += lax.dot_general(a_ref[...], b_ref[...], _TN, preferred_element_type=F32)

            if len(pairs) == 1:
                step()
            else:
                pl.when((k >= starts[idx]) & (k < starts[idx] + nks[idx]))(step)

        @pl.when(k == nk - 1)
        def _():
            o_ref[...] = acc[...].astype(o_ref.dtype)

    in_specs, args = [], []
    for (a, b), t, lo, cnt in zip(pairs, tks, starts, nks):
        idx_map = lambda k, lo=lo, cnt=cnt: (jnp.clip(k - lo, 0, cnt - 1), 0)
        in_specs += [pl.BlockSpec((t, m), idx_map), pl.BlockSpec((t, n), idx_map)]
        args += [a, b]
    aliases = {}
    if into is not None:
        aliases = {len(args): 0}
        in_specs.append(pl.BlockSpec(memory_space=pl.ANY))
        args.append(into)
    return pl.pallas_call(
        body, grid=(nk,), in_specs=in_specs, out_specs=pl.BlockSpec((m, n), lambda k: (block, 0)),
        out_shape=jax.ShapeDtypeStruct((m * blocks, n), out_dtype), scratch_shapes=[pltpu.VMEM((m, n), F32)],
        input_output_aliases=aliases, name=name, compiler_params=_params("arbitrary"))(*args)


def _vec(d, col=None):
    if col is None:
        return pl.BlockSpec((1, d), lambda i, *_: (0, 0))
    return pl.BlockSpec((1, d), col)


def _halo_specs(tm, width, nrows, colblk=0, row_off=0):
    r = tm // HALO
    off = row_off // HALO
    last = nrows // HALO - 1
    prev = pl.BlockSpec((HALO, width), lambda i, *_: (off + jnp.maximum(i * r - 1, 0), colblk))
    nxt = pl.BlockSpec((HALO, width), lambda i, *_: (off + jnp.minimum((i + 1) * r, last), colblk))
    return prev, nxt


def _ext(prev_ref, main_ref, next_ref, i, ni):
    p = jnp.where(i > 0, prev_ref[...], 0.0)
    n = jnp.where(i < ni - 1, next_ref[...], 0.0)
    return jnp.concatenate([p, main_ref[...], n], axis=0)


def _sh(ext, k, tm):
    if k == 0:
        return ext[HALO:HALO + tm]
    rows = ext.shape[0]
    return pltpu.roll(ext, (-k) % rows, axis=0)[HALO:HALO + tm]


def _roll_rows(v, k):
    rows = v.shape[0]
    return pltpu.roll(v, (-k) % rows, axis=0) if k % rows else v


def _conv3(ext, w_ref, tm):
    return _sh(ext, -1, tm) * w_ref[0:1, :] + _sh(ext, 0, tm) * w_ref[1:2, :] + _sh(ext, 1, tm) * w_ref[2:3, :]


def _colsum(v):
    return jnp.sum(v, axis=0, keepdims=True)


def _acc_out(ref, i, val):
    @pl.when(i == 0)
    def _():
        ref[...] = jnp.zeros_like(ref)

    ref[...] += val


def _sigmoid(v):
    return jax.nn.sigmoid(v)


def _norm_mod(x, gain, sc, sh, *, name, y=None, g=None, ymul=None, tm=512):
    n, d = x.shape
    tm = min(tm, n)
    has_res = y is not None
    has_mul = ymul is not None

    def body(*refs):
        it = iter(refs)
        x_ref = next(it)
        y_ref = next(it) if has_res else None
        g_ref = next(it) if has_res else None
        m_ref = next(it) if has_mul else None
        gain_ref, sc_ref, sh_ref = next(it), next(it), next(it)
        xo_ref = next(it) if has_res else None
        a_ref = next(it)
        xv = x_ref[...]
        if has_res:
            yv = y_ref[...]
            if has_mul:
                yv = yv * m_ref[...]
            xv = xv + g_ref[...] * yv
            xo_ref[...] = xv
        r = lax.rsqrt(jnp.mean(xv * xv, axis=-1, keepdims=True) + EPS)
        nrm = (xv * r) * gain_ref[...]
        a_ref[...] = (nrm * (1.0 + sc_ref[...]) + sh_ref[...]).astype(BF16)

    row = pl.BlockSpec((tm, d), lambda i: (i, 0))
    in_specs, args = [row], [x]
    if has_res:
        in_specs += [row, _vec(d)]
        args += [y, g]
    if has_mul:
        in_specs.append(_vec(d))
        args.append(ymul)
    in_specs += [_vec(d)] * 3
    args += [gain, sc, sh]
    out_specs, out_shape = [], []
    if has_res:
        out_specs.append(row)
        out_shape.append(jax.ShapeDtypeStruct((n, d), F32))
    out_specs.append(row)
    out_shape.append(jax.ShapeDtypeStruct((n, d), BF16))
    res = pl.pallas_call(body, grid=(n // tm,), in_specs=in_specs, out_specs=out_specs, out_shape=out_shape,
                         name=name, compiler_params=_params("parallel"))(*args)
    return res if has_res else res[0]


def _norm_mod_bwd(da, x, gain, sc, *, name, dres=None, gate_y=None, gate_g=None, tm=512):
    n, d = x.shape
    tm = min(tm, n)
    has_res = dres is not None
    has_gate = gate_y is not None

    def body(*refs):
        it = iter(refs)
        da_ref, x_ref = next(it), next(it)
        r_ref = next(it) if has_res else None
        y_ref = next(it) if has_gate else None
        g_ref = next(it) if has_gate else None
        gain_ref, sc_ref = next(it), next(it)
        dx_ref, dsh_ref, dsc_ref, dgn_ref = next(it), next(it), next(it), next(it)
        dy_ref = next(it) if has_gate else None
        dg_ref = next(it) if has_gate else None
        i = pl.program_id(0)
        xv = x_ref[...]
        dav = da_ref[...]
        r = lax.rsqrt(jnp.mean(xv * xv, axis=-1, keepdims=True) + EPS)
        xh = xv * r
        nrm = xh * gain_ref[...]
        dn = dav * (1.0 + sc_ref[...])
        dxh = dn * gain_ref[...]
        dx = r * (dxh - xh * jnp.mean(dxh * xh, axis=-1, keepdims=True))
        if has_res:
            dx = dx + r_ref[...]
        dx_ref[...] = dx
        _acc_out(dsh_ref, i, _colsum(dav))
        _acc_out(dsc_ref, i, _colsum(dav * nrm))
        _acc_out(dgn_ref, i, _colsum(dn * xh))
        if has_gate:
            dy_ref[...] = (dx * g_ref[...]).astype(BF16)
            _acc_out(dg_ref, i, _colsum(dx * y_ref[...]))

    row = pl.BlockSpec((tm, d), lambda i: (i, 0))
    in_specs, args = [row, row], [da, x]
    if has_res:
        in_specs.append(row)
        args.append(dres)
    if has_gate:
        in_specs += [row, _vec(d)]
        args += [gate_y, gate_g]
    in_specs += [_vec(d)] * 2
    args += [gain, sc]
    vec_shape = jax.ShapeDtypeStruct((1, d), F32)
    out_specs = [row, _vec(d), _vec(d), _vec(d)]
    out_shape = [jax.ShapeDtypeStruct((n, d), F32), vec_shape, vec_shape, vec_shape]
    if has_gate:
        out_specs += [row, _vec(d)]
        out_shape += [jax.ShapeDtypeStruct((n, d), BF16), vec_shape]
    return pl.pallas_call(
        body, grid=(n // tm,), in_specs=in_specs, out_specs=out_specs, out_shape=out_shape,
        name=name, compiler_params=_params("arbitrary"))(*args)


def _ffn_up_glu(f, w_up, cw, cb, *, name, tm=256, tc=256):
    n, d = f.shape
    tm = min(tm, n)
    ni = n // tm
    nc = DFF // tc
    halo = 16
    rows = tm + 2 * halo
    r = tm // halo
    last = n // halo - 1

    def body(f_ref, fp_ref, fn_ref, w_ref, cw_ref, cb_ref, u_ref, gc_ref, h_ref):
        i = pl.program_id(0)
        a = f_ref[...]
        aext = jnp.concatenate([jnp.where(i > 0, fp_ref[...], jnp.zeros_like(fp_ref[...])), a,
                                jnp.where(i < ni - 1, fn_ref[...], jnp.zeros_like(fn_ref[...]))], axis=0)
        for j in range(nc):
            cols = slice(j * tc, (j + 1) * tc)
            vcols = slice(DFF + j * tc, DFF + (j + 1) * tc)
            gext = jnp.dot(aext, w_ref[:, cols], preferred_element_type=F32)
            val = jnp.dot(a, w_ref[:, vcols], preferred_element_type=F32)
            gate = gext[halo:halo + tm]
            gc = (pltpu.roll(gext, 1, axis=0)[halo:halo + tm] * cw_ref[0:1, cols] + gate * cw_ref[1:2, cols]
                  + pltpu.roll(gext, rows - 1, axis=0)[halo:halo + tm] * cw_ref[2:3, cols]) + cb_ref[:, cols]
            u_ref[:, cols] = gate
            u_ref[:, vcols] = val
            gc_ref[:, cols] = gc
            h_ref[:, cols] = (gc * _sigmoid(gc) * val).astype(BF16)

    return pl.pallas_call(
        body, grid=(ni,),
        in_specs=[pl.BlockSpec((tm, d), lambda i: (i, 0)),
                  pl.BlockSpec((halo, d), lambda i: (jnp.maximum(i * r - 1, 0), 0)),
                  pl.BlockSpec((halo, d), lambda i: (jnp.minimum((i + 1) * r, last), 0)),
                  pl.BlockSpec(w_up.shape, lambda i: (0, 0)), pl.BlockSpec((3, DFF), lambda i: (0, 0)),
                  pl.BlockSpec((1, DFF), lambda i: (0, 0))],
        out_specs=[pl.BlockSpec((tm, 2 * DFF), lambda i: (i, 0)), pl.BlockSpec((tm, DFF), lambda i: (i, 0)),
                   pl.BlockSpec((tm, DFF), lambda i: (i, 0))],
        out_shape=[jax.ShapeDtypeStruct((n, 2 * DFF), F32), jax.ShapeDtypeStruct((n, DFF), F32),
                   jax.ShapeDtypeStruct((n, DFF), BF16)], name=name,
        compiler_params=_params("parallel"))(f, f, f, w_up, cw, cb)


def _ffn_down_glu_bwd(dz, w_down, u, gc, cw, *, name, tm=256, tc=256):
    n, d = dz.shape
    tm = min(tm, n)
    ni = n // tm
    nc = DFF // tc
    rows = tm + 2 * HALO

    def body(z_ref, zp_ref, zn_ref, w_ref, u_ref, vp_ref, vn_ref, c_ref, cp_ref, cn_ref, cw_ref,
             dg_ref, dv_ref, dcw_ref, dcb_ref):
        i = pl.program_id(0)

        @pl.when(i == 0)
        def _():
            dcw_ref[...] = jnp.zeros_like(dcw_ref)
            dcb_ref[...] = jnp.zeros_like(dcb_ref)

        zext = jnp.concatenate([jnp.where(i > 0, zp_ref[...], jnp.zeros_like(zp_ref[...])), z_ref[...],
                                jnp.where(i < ni - 1, zn_ref[...], jnp.zeros_like(zn_ref[...]))], axis=0)
        for j in range(nc):
            cols = slice(j * tc, (j + 1) * tc)
            vcols = slice(DFF + j * tc, DFF + (j + 1) * tc)
            dh = lax.dot_general(zext, w_ref[cols, :], _NT, preferred_element_type=F32)[HALO:HALO + rows]
            gcx = jnp.concatenate([cp_ref[:, cols], c_ref[:, cols], cn_ref[:, cols]], axis=0)
            vext = jnp.concatenate([vp_ref[:, cols], u_ref[:, vcols], vn_ref[:, cols]], axis=0)
            sg = _sigmoid(gcx)
            dgc = dh * vext * (sg * (1.0 + gcx * (1.0 - sg)))
            dv_ref[:, cols] = (dh[HALO:HALO + tm] * (gcx[HALO:HALO + tm] * sg[HALO:HALO + tm])).astype(BF16)
            d_next = pltpu.roll(dgc, rows - 1, axis=0)[HALO:HALO + tm]
            d_prev = pltpu.roll(dgc, 1, axis=0)[HALO:HALO + tm]
            d_here = dgc[HALO:HALO + tm]
            dg_ref[:, cols] = (d_next * cw_ref[0:1, cols] + d_here * cw_ref[1:2, cols]
                               + d_prev * cw_ref[2:3, cols]).astype(BF16)
            gate = u_ref[:, cols]
            dcw_ref[:, cols] += jnp.concatenate([_colsum(d_next * gate), _colsum(d_here * gate),
                                                 _colsum(d_prev * gate)], axis=0)
            dcb_ref[:, cols] += _colsum(d_here)

    def trio(width, halo, tile_width=None, colblk=0):
        r, last = tm // halo, n // halo - 1
        return [pl.BlockSpec((tm, tile_width or width), lambda i: (i, 0)),
                pl.BlockSpec((halo, width), lambda i: (jnp.maximum(i * r - 1, 0), colblk)),
                pl.BlockSpec((halo, width), lambda i: (jnp.minimum((i + 1) * r, last), colblk))]

    whole = lambda shape: pl.BlockSpec(shape, lambda i: (0, 0))
    return pl.pallas_call(
        body, grid=(ni,),
        in_specs=(trio(d, 16) + [whole(w_down.shape)] + trio(DFF, HALO, tile_width=2 * DFF, colblk=1)
                  + trio(DFF, HALO) + [whole((3, DFF))]),
        out_specs=[pl.BlockSpec((tm, DFF), lambda i: (i, 0)), pl.BlockSpec((tm, DFF), lambda i: (i, 0)),
                   whole((3, DFF)), whole((1, DFF))],
        out_shape=[jax.ShapeDtypeStruct((n, DFF), BF16), jax.ShapeDtypeStruct((n, DFF), BF16),
                   jax.ShapeDtypeStruct((3, DFF), F32), jax.ShapeDtypeStruct((1, DFF), F32)],
        name=name, compiler_params=_params("arbitrary"))(dz, dz, dz, w_down, u, u, u, gc, gc, gc, cw)


def _rope_tables(n):
    rows = n // GRID_W
    axis_dim = HD // 2
    inv_freq = jnp.power(ROPE_THETA, -jnp.arange(0, axis_dim, 2, dtype=F32) / axis_dim)
    ar = jnp.arange(rows, dtype=F32)[:, None] * inv_freq
    ac = jnp.arange(GRID_W, dtype=F32)[:, None] * inv_freq
    by_row = lambda a: jnp.repeat(a, GRID_W, axis=0)
    by_col = lambda a: jnp.tile(a, (rows, 1))
    cr, sr, cc, sc = by_row(jnp.cos(ar)), by_row(jnp.sin(ar)), by_col(jnp.cos(ac)), by_col(jnp.sin(ac))
    return jnp.concatenate([cr, cr, cc, cc], axis=1), jnp.concatenate([-sr, sr, -sc, sc], axis=1)


def _partner(v):
    lane = lax.broadcasted_iota(jnp.int32, v.shape, 1)
    return jnp.where((lane % 64) < 32, pltpu.roll(v, HD - 32, axis=1), pltpu.roll(v, 32, axis=1))


def _qkv_prep(p, q_gain, k_gain, cs, sn, *, name, has_q, kv_col, kv_rows=None, kv_row_off=0, kv_into=None, tm=256):
    n = p.shape[0]
    rope = cs is not None
    kv_rows = kv_rows or n
    rb = kv_row_off // tm

    def body(*refs):
        it = iter(refs)
        q_ref = next(it) if has_q else None
        kv_ref = next(it)
        qg_ref, kg_ref = next(it), next(it)
        cs_ref = next(it) if rope else None
        sn_ref = next(it) if rope else None
        if kv_into is not None:
            next(it), next(it)
        qo_ref = next(it) if has_q else None
        ko_ref, vo_ref = next(it), next(it)

        def norm_rope(xh, gain, mul=None):
            r = lax.rsqrt(jnp.mean(xh * xh, axis=-1, keepdims=True) + EPS)
            xn = (xh * r) * gain
            if rope:
                xn = xn * cs_ref[...] + _partner(xn) * sn_ref[...]
            if mul is not None:
                xn = xn * mul
            return xn.astype(BF16)

        if has_q:
            for h in range(NQ):
                qo_ref[h] = norm_rope(q_ref[:, h * HD:(h + 1) * HD], qg_ref[...], _QSCALE)
        for h in range(NKV):
            ko_ref[h] = norm_rope(kv_ref[:, h * HD:(h + 1) * HD], kg_ref[...])
            vo_ref[h] = kv_ref[:, (NKV + h) * HD:(NKV + h + 1) * HD].astype(BF16)

    in_specs, args = [], []
    if has_q:
        in_specs.append(pl.BlockSpec((tm, AW), lambda i: (i, 0)))
        args.append(p)
    in_specs += [pl.BlockSpec((tm, 2 * NKV * HD), lambda i: (i, kv_col)), _vec(HD), _vec(HD)]
    args += [p, q_gain, k_gain]
    if rope:
        in_specs += [pl.BlockSpec((tm, HD), lambda i: (i, 0))] * 2
        args += [cs, sn]
    out_specs, out_shape = [], []
    if has_q:
        out_specs.append(pl.BlockSpec((NQ, tm, HD), lambda i: (0, i, 0)))
        out_shape.append(jax.ShapeDtypeStruct((NQ, n, HD), BF16))
    out_specs += [pl.BlockSpec((NKV, tm, HD), lambda i: (0, rb + i, 0))] * 2
    out_shape += [jax.ShapeDtypeStruct((NKV, kv_rows, HD), BF16)] * 2
    aliases = {}
    if kv_into is not None:
        aliases = {len(args): int(has_q), len(args) + 1: int(has_q) + 1}
        in_specs += [pl.BlockSpec(memory_space=pl.ANY)] * 2
        args += list(kv_into)
    return pl.pallas_call(body, grid=(n // tm,), in_specs=in_specs, out_specs=out_specs, out_shape=out_shape,
                          input_output_aliases=aliases, name=name, compiler_params=_params("parallel"))(*args)


def _in_proj_qkv(a, w_in_t, q_gain, k_gain, cs, sn, kv_into, *, name, kv_row_off, tm=256):
    n, d = a.shape
    nproj = w_in_t.shape[0]
    nqkv = AW + 2 * NKV * HD
    rb = kv_row_off // tm

    def body(a_ref, w_ref, qg_ref, kg_ref, cs_ref, sn_ref, _k_in, _v_in, p_ref, qo_ref, ko_ref, vo_ref):
        av = a_ref[...]
        qkv = lax.dot_general(av, w_ref[0:nqkv, :], _NT, preferred_element_type=F32)
        p_ref[:, 0:nqkv] = qkv
        p_ref[:, nqkv:nproj] = lax.dot_general(av, w_ref[nqkv:nproj, :], _NT, preferred_element_type=F32)

        def norm_rope(xh, gain, mul=None):
            r = lax.rsqrt(jnp.mean(xh * xh, axis=-1, keepdims=True) + EPS)
            xn = (xh * r) * gain
            xn = xn * cs_ref[...] + _partner(xn) * sn_ref[...]
            if mul is not None:
                xn = xn * mul
            return xn.astype(BF16)

        for h in range(NQ):
            qo_ref[h] = norm_rope(qkv[:, h * HD:(h + 1) * HD], qg_ref[...], _QSCALE)
        for h in range(NKV):
            ko_ref[h] = norm_rope(qkv[:, AW + h * HD:AW + (h + 1) * HD], kg_ref[...])
            vo_ref[h] = qkv[:, AW + (NKV + h) * HD:AW + (NKV + h + 1) * HD].astype(BF16)

    kv_rows = kv_into[0].shape[1]
    tab = pl.BlockSpec((tm, HD), lambda i: (i, 0))
    any_spec = pl.BlockSpec(memory_space=pl.ANY)
    kv_spec = pl.BlockSpec((NKV, tm, HD), lambda i: (0, rb + i, 0))
    return pl.pallas_call(
        body, grid=(n // tm,),
        in_specs=[pl.BlockSpec((tm, d), lambda i: (i, 0)), pl.BlockSpec(w_in_t.shape, lambda i: (0, 0)), _vec(HD),
                  _vec(HD), tab, tab, any_spec, any_spec],
        out_specs=[pl.BlockSpec((tm, nproj), lambda i: (i, 0)), pl.BlockSpec((NQ, tm, HD), lambda i: (0, i, 0)),
                   kv_spec, kv_spec],
        out_shape=[jax.ShapeDtypeStruct((n, nproj), F32), jax.ShapeDtypeStruct((NQ, n, HD), BF16),
                   jax.ShapeDtypeStruct((NKV, kv_rows, HD), BF16), jax.ShapeDtypeStruct((NKV, kv_rows, HD), BF16)],
        input_output_aliases={6: 2, 7: 3}, name=name,
        compiler_params=_params("parallel"))(a, w_in_t, q_gain, k_gain, cs, sn, *kv_into)


def _qkv_bwd(p, dq, dk, dv, q_gain, k_gain, cs, sn, *, name, has_q, kv_col, kv_row_off, tm=256):
    n = p.shape[0]
    rope = cs is not None
    rb = kv_row_off // tm

    def body(*refs):
        it = iter(refs)
        q_ref = next(it) if has_q else None
        kv_ref = next(it)
        dq_ref = next(it) if has_q else None
        dk_ref, dv_ref = next(it), next(it)
        qg_ref, kg_ref = next(it), next(it)
        cs_ref = next(it) if rope else None
        sn_ref = next(it) if rope else None
        dp_ref, dqg_ref, dkg_ref = next(it), next(it), next(it)
        i = pl.program_id(0)

        def back(xh, dout, gain):
            if rope:
                dout = dout * cs_ref[...] + _partner(dout * sn_ref[...])
            r = lax.rsqrt(jnp.mean(xh * xh, axis=-1, keepdims=True) + EPS)
            xhat = xh * r
            dxh = dout * gain
            dx = r * (dxh - xhat * jnp.mean(dxh * xhat, axis=-1, keepdims=True))
            return dx, _colsum(dout * xhat)

        dqg = jnp.zeros((1, HD), F32)
        dkg = jnp.zeros((1, HD), F32)
        if has_q:
            for h in range(NQ):
                dx, dg = back(q_ref[:, h * HD:(h + 1) * HD], dq_ref[h], qg_ref[...])
                dp_ref[:, h * HD:(h + 1) * HD] = dx.astype(BF16)
                dqg = dqg + dg
        else:
            dp_ref[:, 0:AW] = jnp.zeros((tm, AW), BF16)
        for h in range(NKV):
            dx, dg = back(kv_ref[:, h * HD:(h + 1) * HD], dk_ref[h], kg_ref[...])
            dp_ref[:, AW + h * HD:AW + (h + 1) * HD] = dx.astype(BF16)
            dkg = dkg + dg
            dp_ref[:, AW + (NKV + h) * HD:AW + (NKV + h + 1) * HD] = dv_ref[h].astype(BF16)
        _acc_out(dqg_ref, i, dqg)
        _acc_out(dkg_ref, i, dkg)

    in_specs, args = [], []
    if has_q:
        in_specs.append(pl.BlockSpec((tm, AW), lambda i: (i, 0)))
        args.append(p)
    in_specs.append(pl.BlockSpec((tm, 2 * NKV * HD), lambda i: (i, kv_col)))
    args.append(p)
    if has_q:
        in_specs.append(pl.BlockSpec((NQ, tm, HD), lambda i: (0, i, 0)))
        args.append(dq)
    in_specs += [pl.BlockSpec((NKV, tm, HD), lambda i: (0, rb + i, 0))] * 2 + [_vec(HD), _vec(HD)]
    args += [dk, dv, q_gain, k_gain]
    if rope:
        in_specs += [pl.BlockSpec((tm, HD), lambda i: (i, 0))] * 2
        args += [cs, sn]
    return pl.pallas_call(
        body, grid=(n // tm,), in_specs=in_specs,
        out_specs=[pl.BlockSpec((tm, D), lambda i: (i, 0)), _vec(HD), _vec(HD)],
        out_shape=[jax.ShapeDtypeStruct((n, D), BF16), jax.ShapeDtypeStruct((1, HD), F32),
                   jax.ShapeDtypeStruct((1, HD), F32)],
        name=name, compiler_params=_params("arbitrary"))(*args)


def _conv_gate_fwd(p, o, conv_w, *, name, tm=256):
    n = p.shape[0]
    ni = n // tm

    def body(gb_ref, gc_ref, gcp_ref, gcn_ref, xi_ref, xip_ref, xin_ref, o_ref, w_ref, cat_ref):
        i = pl.program_id(0)
        hext = _ext(gcp_ref, gc_ref, gcn_ref, i, ni) * _ext(xip_ref, xi_ref, xin_ref, i, ni)
        cat_ref[:, 0:AW] = o_ref[...].astype(BF16)
        cat_ref[:, AW:D] = (gb_ref[...] * _conv3(hext, w_ref, tm)).astype(BF16)

    gcp, gcn = _halo_specs(tm, CW, n, colblk=3)
    xip, xin = _halo_specs(tm, CW, n, colblk=4)
    return pl.pallas_call(
        body, grid=(ni,),
        in_specs=[pl.BlockSpec((tm, CW), lambda i: (i, 2)), pl.BlockSpec((tm, CW), lambda i: (i, 3)), gcp, gcn,
                  pl.BlockSpec((tm, CW), lambda i: (i, 4)), xip, xin, pl.BlockSpec((tm, AW), lambda i: (i, 0)),
                  pl.BlockSpec((3, CW), lambda i: (0, 0))],
        out_specs=pl.BlockSpec((tm, D), lambda i: (i, 0)), out_shape=jax.ShapeDtypeStruct((n, D), BF16),
        name=name, compiler_params=_params("parallel"))(p, p, p, p, p, p, p, o, conv_w)


def _conv_gate_bwd(dcat, p, conv_w, *, name, tm=256):
    n = p.shape[0]
    ni = n // tm

    def body(dc_ref, dcp_ref, dcn_ref, gb_ref, gbp_ref, gbn_ref, gc_ref, gcp_ref, gcn_ref, xi_ref, xip_ref, xin_ref,
             w_ref, dp_ref, dw_ref):
        i = pl.program_id(0)
        gcext = _ext(gcp_ref, gc_ref, gcn_ref, i, ni)
        xiext = _ext(xip_ref, xi_ref, xin_ref, i, ni)
        hext = gcext * xiext
        dcv = _ext(dcp_ref, dc_ref, dcn_ref, i, ni) * _ext(gbp_ref, gb_ref, gbn_ref, i, ni)
        dp_ref[:, 0:CW] = (dc_ref[...] * _conv3(hext, w_ref, tm)).astype(BF16)
        dh = _sh(dcv, 1, tm) * w_ref[0:1, :] + _sh(dcv, 0, tm) * w_ref[1:2, :] + _sh(dcv, -1, tm) * w_ref[2:3, :]
        dp_ref[:, CW:2 * CW] = (dh * xi_ref[...]).astype(BF16)
        dp_ref[:, 2 * CW:3 * CW] = (dh * gc_ref[...]).astype(BF16)
        dcv_t = dcv[HALO:HALO + tm]
        dw = jnp.concatenate([_colsum(dcv_t * _sh(hext, -1, tm)), _colsum(dcv_t * _sh(hext, 0, tm)),
                              _colsum(dcv_t * _sh(hext, 1, tm))], axis=0)
        _acc_out(dw_ref, i, dw)

    def trio(colblk):
        prev, nxt = _halo_specs(tm, CW, n, colblk=colblk)
        return [pl.BlockSpec((tm, CW), lambda i: (i, colblk)), prev, nxt]

    return pl.pallas_call(
        body, grid=(ni,), in_specs=trio(1) + trio(2) + trio(3) + trio(4) + [pl.BlockSpec((3, CW), lambda i: (0, 0))],
        out_specs=[pl.BlockSpec((tm, 3 * CW), lambda i: (i, 0)), pl.BlockSpec((3, CW), lambda i: (0, 0))],
        out_shape=[jax.ShapeDtypeStruct((n, 3 * CW), BF16), jax.ShapeDtypeStruct((3, CW), F32)],
        name=name, compiler_params=_params("arbitrary"))(dcat, dcat, dcat, p, p, p, p, p, p, p, p, p, conv_w)


def _attn_fwd(q, k, v, *, name, bq=512, sub=256):
    n = q.shape[1]
    t = k.shape[1]
    bq = min(bq, n)
    sub = min(sub, 2 * bq)

    def body(q_ref, k_ref, v_ref, o_ref, lse_ref):
        q2 = q_ref[...].reshape(2 * bq, HD)
        outs, lses = [], []
        for r0 in range(0, 2 * bq, sub):
            s = lax.dot_general(q2[r0:r0 + sub], k_ref[0], _NT, preferred_element_type=F32)
            m = jnp.max(s, axis=-1, keepdims=True)
            pv = jnp.exp2(s - m)
            l = jnp.sum(pv, axis=-1, keepdims=True)
            outs.append(jnp.dot(pv.astype(BF16), v_ref[0], preferred_element_type=F32) / l)
            lses.append(m + jnp.log2(l))
        out = jnp.concatenate(outs, axis=0)
        o_ref[:, 0:HD] = out[0:bq]
        o_ref[:, HD:2 * HD] = out[bq:2 * bq]
        lse_ref[...] = jnp.concatenate(lses, axis=0).reshape(2, bq, 1)

    kspec = pl.BlockSpec((1, t, HD), lambda h, i: (h, 0, 0))
    return pl.pallas_call(
        body, grid=(NKV, n // bq),
        in_specs=[pl.BlockSpec((2, bq, HD), lambda h, i: (h, i, 0)), kspec, kspec],
        out_specs=[pl.BlockSpec((bq, 2 * HD), lambda h, i: (i, h)), pl.BlockSpec((2, bq, 1), lambda h, i: (h, i, 0))],
        out_shape=[jax.ShapeDtypeStruct((n, AW), F32), jax.ShapeDtypeStruct((NQ, n, 1), F32)],
        name=name, compiler_params=_params("parallel", "parallel"))(q, k, v)


def _attn_bwd(q, k, v, dcat, o, lse, *, name, bq=256):
    n = q.shape[1]
    t = k.shape[1]
    bq = min(bq, n)

    def body(q_ref, k_ref, v_ref, dc_ref, o_ref, lse_ref, dq_ref, dk_ref, dv_ref):
        @pl.when(pl.program_id(1) == 0)
        def _():
            dk_ref[...] = jnp.zeros_like(dk_ref)
            dv_ref[...] = jnp.zeros_like(dv_ref)

        q2 = q_ref[...].reshape(2 * bq, HD)
        do_f = jnp.concatenate([dc_ref[:, 0:HD], dc_ref[:, HD:2 * HD]], axis=0)
        o_f = jnp.concatenate([o_ref[:, 0:HD], o_ref[:, HD:2 * HD]], axis=0)
        delta = jnp.sum(do_f * o_f, axis=-1, keepdims=True)
        do2 = do_f.astype(BF16)
        s = lax.dot_general(q2, k_ref[0], _NT, preferred_element_type=F32)
        pv = jnp.exp2(s - lse_ref[...].reshape(2 * bq, 1))
        dp = lax.dot_general(do2, v_ref[0], _NT, preferred_element_type=F32)
        ds = (pv * (dp - delta)).astype(BF16)
        dq_ref[...] = (jnp.dot(ds, k_ref[0], preferred_element_type=F32) * _SCALE).reshape(2, bq, HD)
        dk_ref[0] += lax.dot_general(ds, q2, _TN, preferred_element_type=F32) * _LN2
        dv_ref[0] += lax.dot_general(pv.astype(BF16), do2, _TN, preferred_element_type=F32)

    qspec = pl.BlockSpec((2, bq, HD), lambda h, i: (h, i, 0))
    kspec = pl.BlockSpec((1, t, HD), lambda h, i: (h, 0, 0))
    sspec = pl.BlockSpec((2, bq, 1), lambda h, i: (h, i, 0))
    cspec = pl.BlockSpec((bq, 2 * HD), lambda h, i: (i, h))
    return pl.pallas_call(
        body, grid=(NKV, n // bq), in_specs=[qspec, kspec, kspec, cspec, cspec, sspec], out_specs=[qspec, kspec, kspec],
        out_shape=[jax.ShapeDtypeStruct((NQ, n, HD), F32), jax.ShapeDtypeStruct((NKV, t, HD), F32),
                   jax.ShapeDtypeStruct((NKV, t, HD), F32)],
        name=name, compiler_params=_params("parallel", "arbitrary"))(q, k, v, dcat, o, lse)


def _window_sums(ext, w):
    s, step = ext, 1
    while step < w:
        s = s + _roll_rows(s, step)
        step *= 2
    return s


def _pool_counts(i, tm, n, w, rows, first):
    t = i * tm - HALO + first + lax.broadcasted_iota(jnp.int32, (rows, 1), 0)
    lo = jnp.clip(t - w // 2, 0, n)
    hi = jnp.clip(t + w - w // 2, 0, n)
    return jnp.maximum(hi - lo, 1).astype(F32)


def _norm_mod_ext(xext, gain_ref, sc_ref, sh_ref, i, tm, n):
    rows = xext.shape[0]
    t = i * tm - HALO + lax.broadcasted_iota(jnp.int32, (rows, 1), 0)
    inside = (t >= 0) & (t < n)
    r = lax.rsqrt(jnp.mean(xext * xext, axis=-1, keepdims=True) + EPS)
    xh = xext * r
    a = (xh * gain_ref[...]) * (1.0 + sc_ref[...]) + sh_ref[...]
    return jnp.where(inside, a, 0.0), r, xh


def _pool_fwd(x, y, g, gain, sc, sh, pool_w, *, name, tm=256):
    n, d = x.shape
    ni = n // tm

    def body(x_ref, xp_ref, xn_ref, y_ref, yp_ref, yn_ref, g_ref, gain_ref, sc_ref, sh_ref, w_ref, xo_ref, o_ref):
        i = pl.program_id(0)
        xext = _ext(xp_ref, x_ref, xn_ref, i, ni) + g_ref[...] * _ext(yp_ref, y_ref, yn_ref, i, ni)
        xo_ref[...] = xext[HALO:HALO + tm]
        aext, _, _ = _norm_mod_ext(xext, gain_ref, sc_ref, sh_ref, i, tm, n)
        for gi, w in enumerate(POOL_WINDOWS):
            ag = aext[:, gi * PG:(gi + 1) * PG]
            mean = _sh(_window_sums(ag, w), -(w // 2), tm) / _pool_counts(i, tm, n, w, tm, HALO)
            pooled = mean - ag[HALO:HALO + tm]
            o_ref[:, gi * PG:(gi + 1) * PG] = jnp.dot(pooled.astype(BF16), w_ref[gi], preferred_element_type=F32)

    row = pl.BlockSpec((tm, d), lambda i: (i, 0))
    prev, nxt = _halo_specs(tm, d, n)
    return pl.pallas_call(
        body, grid=(ni,),
        in_specs=[row, prev, nxt, row, prev, nxt, _vec(d), _vec(d), _vec(d), _vec(d),
                  pl.BlockSpec((4, PG, PG), lambda i: (0, 0, 0))],
        out_specs=[row, row], out_shape=[jax.ShapeDtypeStruct((n, d), F32)] * 2,
        name=name, compiler_params=_params("parallel"))(x, x, x, y, y, y, g, gain, sc, sh, pool_w)


def _pool_bwd(dxo, mixed, x, g, scale, gain, sc, sh, pool_w, zprev, gprev, *, name, tm=256):
    n, d = x.shape
    ni = n // tm

    def body(dx_ref, dxp_ref, dxn_ref, mx_ref, x_ref, xp_ref, xn_ref, g_ref, s_ref, gain_ref, sc_ref, sh_ref, w_ref,
             zp_ref, gp_ref, dxi_ref, dw_ref, dg_ref, dsl_ref, dsh_ref, dsc_ref, dgn_ref, dzp_ref, dgp_ref):
        i = pl.program_id(0)

        @pl.when(i == 0)
        def _():
            dw_ref[...] = jnp.zeros_like(dw_ref)

        dxo_t = dx_ref[...]
        mixed_t = mx_ref[...]
        dy_t = dxo_t * g_ref[...]
        _acc_out(dg_ref, i, _colsum(dxo_t * (mixed_t * s_ref[...])))
        _acc_out(dsl_ref, i, _colsum(dy_t * mixed_t))
        dmixed = (_ext(dxp_ref, dx_ref, dxn_ref, i, ni) * g_ref[...]) * s_ref[...]
        xext = _ext(xp_ref, x_ref, xn_ref, i, ni)
        aext, rext, xhext = _norm_mod_ext(xext, gain_ref, sc_ref, sh_ref, i, tm, n)
        rows = tm + 2 * HALO
        da_parts = []
        for gi, w in enumerate(POOL_WINDOWS):
            sl = slice(gi * PG, (gi + 1) * PG)
            ag = aext[:, sl]
            mean = _sh(_window_sums(ag, w), -(w // 2), tm) / _pool_counts(i, tm, n, w, tm, HALO)
            pooled = (mean - ag[HALO:HALO + tm]).astype(BF16)
            dmg = dmixed[:, sl].astype(BF16)
            dw_ref[gi] += lax.dot_general(pooled, dmixed[HALO:HALO + tm, sl].astype(BF16), _TN,
                                          preferred_element_type=F32)
            dpl = lax.dot_general(dmg, w_ref[gi], _NT, preferred_element_type=F32)
            e = dpl / _pool_counts(i, tm, n, w, rows, 0)
            da_parts.append(_sh(_window_sums(e, w), 1 - w // 2, tm) - dpl[HALO:HALO + tm])
        da = jnp.concatenate(da_parts, axis=1)
        r = rext[HALO:HALO + tm]
        xh = xhext[HALO:HALO + tm]
        nrm = xh * gain_ref[...]
        dn = da * (1.0 + sc_ref[...])
        dxh = dn * gain_ref[...]
        dxi = dxo_t + r * (dxh - xh * jnp.mean(dxh * xh, axis=-1, keepdims=True))
        dxi_ref[...] = dxi
        _acc_out(dsh_ref, i, _colsum(da))
        _acc_out(dsc_ref, i, _colsum(da * nrm))
        _acc_out(dgn_ref, i, _colsum(dn * xh))
        dzp_ref[...] = (dxi * gp_ref[...]).astype(BF16)
        _acc_out(dgp_ref, i, _colsum(dxi * zp_ref[...]))

    row = pl.BlockSpec((tm, d), lambda i: (i, 0))
    prev, nxt = _halo_specs(tm, d, n)
    wspec = pl.BlockSpec((4, PG, PG), lambda i: (0, 0, 0))
    vshape = jax.ShapeDtypeStruct((1, d), F32)
    return pl.pallas_call(
        body, grid=(ni,),
        in_specs=[row, prev, nxt, row, row, prev, nxt] + [_vec(d)] * 5 + [wspec, row, _vec(d)],
        out_specs=[row, wspec] + [_vec(d)] * 5 + [row, _vec(d)],
        out_shape=[jax.ShapeDtypeStruct((n, d), F32), jax.ShapeDtypeStruct((4, PG, PG), F32)] + [vshape] * 5
        + [jax.ShapeDtypeStruct((n, d), BF16), vshape],
        name=name, compiler_params=_params("arbitrary"))(dxo, dxo, dxo, mixed, x, x, x, g, scale, gain, sc, sh, pool_w,
                                                         zprev, gprev)


def _adamw(gparts_list, w, m, v, *, name, silu_grad_of=None):
    nl = len(gparts_list)
    nparts, r, c = gparts_list[0].shape
    tr = _pick(r, (256, 128, 64, 32, 16, 8))
    has_c = silu_grad_of is not None

    def body(*refs):
        gp_refs = refs[:nl]
        it = iter(refs[nl:])
        w_ref, m_ref, v_ref = next(it), next(it), next(it)
        c_ref = next(it) if has_c else None
        g_ref, d_ref, mo_ref, vo_ref = next(it), next(it), next(it), next(it)
        layer = pl.program_id(0)

        def update(gp_ref):
            g = gp_ref[0].astype(F32)
            for p in range(1, nparts):
                g = g + gp_ref[p].astype(F32)
            if has_c:
                cv = c_ref[0]
                sg = _sigmoid(cv)
                g = g * (sg * (1.0 + cv * (1.0 - sg)))
            g_ref[0] = g
            mn = ADAM_B1 * m_ref[0] + (1.0 - ADAM_B1) * g
            vn = ADAM_B2 * v_ref[0] + (1.0 - ADAM_B2) * (g * g)
            m_hat = mn / (1.0 - ADAM_B1 ** ADAM_STEP)
            v_hat = vn / (1.0 - ADAM_B2 ** ADAM_STEP)
            d_ref[0] = -ADAM_LR * (m_hat / (jnp.sqrt(v_hat) + ADAM_EPS) + ADAM_WD * w_ref[0])
            mo_ref[0] = mn
            vo_ref[0] = vn

        if nl == 1:
            update(gp_refs[0])
        else:
            for li in range(nl):
                pl.when(layer == li)(functools.partial(update, gp_refs[li]))

    row = pl.BlockSpec((1, tr, c), lambda l, i: (l, i, 0))
    in_specs = [pl.BlockSpec((nparts, tr, c), lambda l, i, li=li: (0, jnp.where(l == li, i, 0), 0)) for li in range(nl)]
    in_specs += [row, row, row]
    args = list(gparts_list) + [w, m, v]
    if has_c:
        in_specs.append(row)
        args.append(silu_grad_of)
    return pl.pallas_call(
        body, grid=(nl, r // tr), in_specs=in_specs, out_specs=[row] * 4,
        out_shape=[jax.ShapeDtypeStruct((nl, r, c), F32)] * 4, name=name,
        compiler_params=_params("arbitrary", "arbitrary"))(*args)


def _adamw_nd(gparts, w, m, v, *, name, silu_grad_of=None):
    shape = w.shape
    c = shape[-1]
    if isinstance(gparts, (list, tuple)):
        nl = len(gparts)
        r = math.prod(shape[1:-1])
    else:
        nl = 1
        r = math.prod(shape[:-1]) if len(shape) > 1 else 1
        gparts = [gparts]
    rs = lambda a: a.reshape(nl, r, c)
    res = _adamw([gp.reshape(gp.shape[0], r, c) for gp in gparts], rs(w), rs(m), rs(v), name=name,
                 silu_grad_of=None if silu_grad_of is None else rs(silu_grad_of))
    return [a.reshape(shape) for a in res]


def _place():
    return lax.axis_index("x"), lax.axis_index("y"), lax.axis_index("c")


def _all_gather(arrs, *, name):
    k_arr = len(arrs)

    def body(*refs):
        ins = refs[:k_arr]
        outs = refs[k_arr:2 * k_arr]
        send_sems, recv_sems, local_sems = refs[2 * k_arr:]
        x, y, c = _place()
        me, sibling = (x, y, c), (x, y, 1 - c)
        chips = [(1 - x, y), (x, 1 - y), (1 - x, 1 - y)]

        def slot(a, px, py, pc):
            return outs[a].at[4 * px + 2 * py + pc]

        def copy(a, s, block, to, src=None):
            return pltpu.make_async_remote_copy(
                src_ref=slot(a, *block) if src is None else src, dst_ref=slot(a, *block),
                send_sem=send_sems.at[a, s], recv_sem=recv_sems.at[a, s], device_id=to, device_id_type=MESH)

        mine = [pltpu.make_async_copy(ins[a], slot(a, *me), local_sems.at[a]) for a in range(k_arr)]
        for cp in mine:
            cp.start()
        first = []
        for a in range(k_arr):
            first.append(copy(a, 0, me, sibling, src=ins[a]))
            first += [copy(a, 1 + j, me, (*chip, c), src=ins[a]) for j, chip in enumerate(chips)]
        for cp in first:
            cp.start()
        passed = []
        for j, chip in enumerate(chips):
            for a in range(k_arr):
                copy(a, 1 + j, (*chip, c), me).wait_recv()
                fw = copy(a, 4 + j, (*chip, c), sibling)
                fw.start()
                passed.append(fw)
        for a in range(k_arr):
            copy(a, 0, sibling, me).wait_recv()
            for j, chip in enumerate(chips):
                copy(a, 4 + j, (*chip, 1 - c), me).wait_recv()
        for cp in first + passed:
            cp.wait_send()
        for cp in mine:
            cp.wait()

    any_spec = pl.BlockSpec(memory_space=pl.ANY)
    return pl.pallas_call(
        body, in_specs=[any_spec] * k_arr, out_specs=[any_spec] * k_arr,
        out_shape=[jax.ShapeDtypeStruct((NDEV,) + a.shape, a.dtype) for a in arrs],
        scratch_shapes=[pltpu.SemaphoreType.DMA((k_arr, 7)), pltpu.SemaphoreType.DMA((k_arr, 7)),
                        pltpu.SemaphoreType.DMA((k_arr,))],
        name=name)(*arrs)


_HBM = pl.BlockSpec(memory_space=pltpu.HBM)
_SEM = pl.BlockSpec(memory_space=pltpu.SEMAPHORE)
_EFFECT = pltpu.SideEffectType.DATAFLOW_SIDE_EFFECTING


def _peers(x, y, c):
    return [(x ^ (rel >> 2), y ^ ((rel >> 1) & 1), c ^ (rel & 1)) for rel in range(1, NDEV)]


def _exchange_copies(srcs, lands, send_sems, recv_sems, scatter):
    x, y, c = _place()
    me = 4 * x + 2 * y + c
    copies = []
    for r, (px, py, pc) in enumerate(_peers(x, y, c)):
        peer = 4 * px + 2 * py + pc
        for a in range(len(srcs)):
            copies.append(pltpu.make_async_remote_copy(
                src_ref=srcs[a].at[peer] if scatter else srcs[a], dst_ref=lands[a].at[me],
                send_sem=send_sems.at[7 * a + r], recv_sem=recv_sems.at[7 * a + r], device_id=(px, py, pc),
                device_id_type=MESH))
    return copies


def _exchange_start(arrs, *, scatter, name):
    k_arr = len(arrs)
    land_shapes = [a.shape if scatter else (NDEV,) + a.shape for a in arrs]
    lands = [pltpu.with_memory_space_constraint(lax.empty(s, a.dtype), pltpu.HBM) for s, a in zip(land_shapes, arrs)]
    srcs = [pltpu.with_memory_space_constraint(a, pltpu.HBM) for a in arrs]

    def body(*refs):
        src_refs, land_refs = refs[:k_arr], refs[k_arr:2 * k_arr]
        send_sems, recv_sems = refs[2 * k_arr], refs[2 * k_arr + 1]
        token = refs[-1]
        for cp in _exchange_copies(src_refs, land_refs, send_sems, recv_sems, scatter):
            cp.start()
        token[...] = jnp.zeros_like(token)

    out_shape = ([pltpu.SemaphoreType.DMA((7 * k_arr,)), pltpu.SemaphoreType.DMA((7 * k_arr,))]
                 + [pltpu.HBM(a.shape, a.dtype) for a in arrs] + [pltpu.HBM(s, a.dtype) for s, a in zip(land_shapes, arrs)]
                 + [jax.ShapeDtypeStruct((8, 128), F32)])
    res = pl.pallas_call(
        body, name=name, out_shape=out_shape, in_specs=[_HBM] * (2 * k_arr),
        out_specs=[_SEM, _SEM] + [_HBM] * (2 * k_arr) + [pl.BlockSpec(memory_space=pltpu.VMEM)],
        input_output_aliases={i: 2 + i for i in range(2 * k_arr)},
        compiler_params=pltpu.CompilerParams(has_side_effects=_EFFECT))(*srcs, *lands)
    return dict(send=res[0], recv=res[1], srcs=list(res[2:2 + k_arr]), lands=list(res[2 + k_arr:2 + 2 * k_arr]),
                token=res[-1], scatter=scatter)


def _exchange_wait(handle, after, *, name):
    k_arr = len(handle["srcs"])
    scatter = handle["scatter"]

    def body(*refs):
        src_refs, land_refs = refs[:k_arr], refs[k_arr:2 * k_arr]
        send_sems, recv_sems = refs[2 * k_arr], refs[2 * k_arr + 1]
        x, y, c = _place()
        me = 4 * x + 2 * y + c
        for r, (px, py, pc) in enumerate(_peers(x, y, c)):
            peer = 4 * px + 2 * py + pc
            for a in range(k_arr):
                cp = pltpu.make_async_remote_copy(
                    src_ref=src_refs[a].at[peer] if scatter else src_refs[a], dst_ref=land_refs[a].at[peer],
                    send_sem=send_sems.at[7 * a + r], recv_sem=recv_sems.at[7 * a + r], device_id=(x, y, c),
                    device_id_type=MESH)
                cp.wait_send()
                cp.wait_recv()

    arrs = handle["srcs"] + handle["lands"]
    res = pl.pallas_call(
        body, name=name, out_shape=[pltpu.HBM(a.shape, a.dtype) for a in arrs],
        in_specs=[_HBM] * (2 * k_arr) + [_SEM, _SEM, pl.BlockSpec(memory_space=pl.ANY)],
        out_specs=[_HBM] * (2 * k_arr), input_output_aliases={i: i for i in range(2 * k_arr)},
        compiler_params=pltpu.CompilerParams(has_side_effects=_EFFECT))(*arrs, handle["send"], handle["recv"], after)
    me = 4 * lax.axis_index("x") + 2 * lax.axis_index("y") + lax.axis_index("c")
    out = []
    for src, land in zip(res[:k_arr], res[k_arr:]):
        own = lax.dynamic_index_in_dim(src, me, 0, keepdims=False) if scatter else src
        out.append(lax.dynamic_update_index_in_dim(land, own, me, 0))
    return out


def _ffn_bwd(dxo, dz, xr, f, u_gc, hmid, gain, sc, w_up, cw, w_down, tag, gate_y=None, gate_g=None):
    d_wdown = _mm_tn((hmid, dz), name=f"ffn_down_dw_{tag}")
    dug, duv, dcw, dcb = _ffn_down_glu_bwd(dz, w_down, u_gc[0], u_gc[1], cw, name=f"ffn_down_glu_bwd_{tag}")
    d_wup = _mm_tn((dug, f), blocks=2, block=0, name=f"ffn_up_dwg_{tag}")
    d_wup = _mm_tn((duv, f), blocks=2, block=1, into=d_wup, name=f"ffn_up_dwv_{tag}")
    gated = gate_y is not None
    res = _mm_w_ep([dug, duv], w_up, _ep_norm_bwd(gated), [xr, dxo] + ([gate_y] if gated else []),
                   [gain, sc] + ([gate_g] if gated else []), [F32] + ([BF16] if gated else []),
                   [D] * (4 if gated else 3), name=f"ffn_up_dx_norm_bwd_{tag}")
    n_out = 2 if gated else 1
    return res[:n_out], res[n_out:], (d_wup, d_wdown, dcw, dcb)


def _split6(mod):
    return [mod[j * D:(j + 1) * D][None, :] for j in range(6)]


def _row(v):
    return v.reshape(1, -1)


def kernel(x, c, ctx, c_ctx, ada_w, ada_b, mix_norm, ffn_norm, even_w_in, even_q_gain, even_k_gain, even_conv_w, even_w_out, odd_pool_w, odd_pool_scale, ffn_w_up, ffn_conv_w, ffn_conv_b, ffn_w_down, loss_target, m_c_ctx, m_ada_w, m_ada_b, m_mix_norm, m_ffn_norm, m_even_w_in, m_even_q_gain, m_even_k_gain, m_even_conv_w, m_even_w_out, m_odd_pool_w, m_odd_pool_scale, m_ffn_w_up, m_ffn_conv_w, m_ffn_conv_b, m_ffn_w_down, v_c_ctx, v_ada_w, v_ada_b, v_mix_norm, v_ffn_norm, v_even_w_in, v_even_q_gain, v_even_k_gain, v_even_conv_w, v_even_w_out, v_odd_pool_w, v_odd_pool_scale, v_ffn_w_up, v_ffn_conv_w, v_ffn_conv_b, v_ffn_w_down):
    n = x.shape[1]
    lc = ctx.shape[1]
    me = 4 * lax.axis_index("x") + 2 * lax.axis_index("y") + lax.axis_index("c")
    xs, ctxs, tgt = x[0], ctx[0], loss_target[0]
    acols = ada_w.shape[2]

    small = jnp.concatenate([even_conv_w.reshape(-1), ffn_conv_w.reshape(-1), odd_pool_scale.reshape(-1)])
    nsmall = small.shape[0]
    small = jnp.pad(small, (0, (-nsmall) % 1024)).reshape(-1, 128)
    c_rows = jnp.pad(c, ((0, 7), (0, 0)))
    tr = lambda a: jnp.swapaxes(a, -1, -2)
    g_c, g_win, g_small = _all_gather([c_rows, tr(even_w_in[0]).astype(BF16), small], name="gather_first")
    w_in_t = g_win.reshape(-1, D)
    g_small = g_small.reshape(NDEV, -1)
    ecw = even_conv_w.shape[2]
    fcw = ffn_conv_w.shape[2]
    conv_w = g_small[:, :3 * ecw].reshape(NDEV, 3, ecw).transpose(1, 0, 2).reshape(3, CW)
    o1 = 3 * ecw
    fconv_w = g_small[:, o1:o1 + 6 * fcw].reshape(NDEV, 2, 3, fcw).transpose(1, 2, 0, 3).reshape(2, 3, DFF)
    o2 = o1 + 6 * fcw
    pool_scale = g_small[:, o2:o2 + D // NDEV].reshape(1, D)

    mraw = jnp.concatenate([g_c[:, 0, :], c_ctx[None, :], jnp.zeros((7, D), F32)], axis=0)
    my_bias = lax.dynamic_slice_in_dim(ada_b, me * acols, acols, axis=1)
    modp = jnp.stack([_mm(mraw, ada_w[l], silu_a=True, bias=my_bias[l:l + 1], name=f"ada_proj_{l}", tm=16, tn=256)
                      for l in range(2)])
    (g_mod,) = _all_gather([modp], name="gather_mod")
    mod_rows = g_mod.transpose(1, 2, 0, 3).reshape(2, 16, 6 * D)
    late_shards = [even_w_out[0].astype(BF16), odd_pool_w[0].astype(BF16), tr(ffn_w_up[0]).astype(BF16),
                   tr(ffn_w_up[1]).astype(BF16), ffn_w_down[0].astype(BF16), ffn_w_down[1].astype(BF16)]
    late_shards, mod_rows = lax.optimization_barrier((late_shards, mod_rows))
    h_weights = _exchange_start(late_shards, scatter=False, name="weights_start")
    mod_rows = mod_rows + h_weights["token"][0, 0]
    mod = lax.dynamic_index_in_dim(mod_rows, me, axis=1, keepdims=False)
    sh1, sc1, g1, sh2, sc2, g2 = _split6(mod[0])
    sh1b, sc1b, g1b, sh2b, sc2b, g2b = _split6(mod[1])
    csh1, csc1 = _split6(mod_rows[0, 8])[:2]
    mixn = [_row(mix_norm[l]) for l in range(2)]
    ffnn = [_row(ffn_norm[l]) for l in range(2)]
    qg, kg = _row(even_q_gain[0]), _row(even_k_gain[0])
    fcb = [_row(ffn_conv_b[l]) for l in range(2)]

    cs_t, sn_t = _rope_tables(n)
    a_lat = _norm_mod(xs, mixn[0], sc1, sh1, name="mix0_norm")
    a_ctx = _norm_mod(ctxs, mixn[0], csc1, csh1, name="mix0_norm_ctx")
    p_ctx = _mm(a_ctx, w_in_t[AW:AW + 4 * HD], tb=True, name="in_proj_ctx", tm=256, tn=512, tk=1024)
    kv_ctx = _qkv_prep(p_ctx, qg, kg, None, None, has_q=False, kv_col=0, kv_rows=lc + n, name="qkv_prep_ctx")
    p_lat, q_r, k_all, v_all = _in_proj_qkv(a_lat, w_in_t, qg, kg, cs_t, sn_t, kv_ctx, kv_row_off=lc,
                                            name="in_proj_qkv")
    o_attn, lse = _attn_fwd(q_r, k_all, v_all, name="attn_fwd")
    cat = _conv_gate_fwd(p_lat, o_attn, conv_w, name="conv_gate")
    g_wout, g_pool, g_up0, g_up1, g_down0, g_down1 = _exchange_wait(h_weights, cat, name="weights_wait")
    w_out = g_wout.reshape(D, D)
    pool_w = g_pool.transpose(1, 0, 2, 3).reshape(4, PG, PG)
    w_up_t = [g_up0.reshape(2 * DFF, D), g_up1.reshape(2 * DFF, D)]
    w_up = [w.T for w in w_up_t]
    w_down = [g_down0.reshape(DFF, D), g_down1.reshape(DFF, D)]
    y0, x1, f0 = _mm_w_ep(cat, w_out, _ep_resid_norm, [xs], [g1, ffnn[0], sc2, sh2], [F32, F32, BF16], [],
                          tm=512, name="out_proj_norm")[:3]
    *u0, h0 = _ffn_up_glu(f0, w_up[0], fconv_w[0], fcb[0], name="ffn_up_glu_l0")
    z0 = _mm_w(h0, w_down[0], name="ffn_down_l0")

    x2, mixed = _pool_fwd(x1, z0, g2, mixn[1], sc1b, sh1b, pool_w, name="pool_fwd")
    x3, f1 = _norm_mod(x2, ffnn[1], sc2b, sh2b, y=mixed, g=g1b, ymul=pool_scale, name="ffn_norm_l1")
    *u1, h1 = _ffn_up_glu(f1, w_up[1], fconv_w[1], fcb[1], name="ffn_up_glu_l1")
    dx4, dz1, loss_part, dg2b = _mm_w_ep(h1, w_down[1], _ep_loss(D), [x3, tgt], [g2b], [F32, BF16], [128, D],
                                         tm=512, name="ffn_down_loss")

    (dx3,), (dsh2b, dsc2b, dffn1), (dup1, ddown1, dfcw1, dfcb1) = _ffn_bwd(
        dx4, dz1, x3, f1, u1, h1, ffnn[1], sc2b, w_up_t[1], fconv_w[1], w_down[1], "l1")
    dx2, dpool_w, dg1b, dpscale, dsh1b, dsc1b, dmix1, dz0, dg2 = _pool_bwd(
        dx3, mixed, x2, g1b, pool_scale, mixn[1], sc1b, sh1b, pool_w, z0, g2, name="pool_bwd")

    s_pool = dpool_w.astype(BF16).reshape(4, NDEV, PG // NDEV, PG).transpose(1, 0, 2, 3)
    h_g1 = _exchange_start([s_pool, dup1.reshape(NDEV, -1, D), ddown1.reshape(NDEV, DFF // NDEV, D)], scatter=True,
                           name="grads1_start")

    (dx1, dy0), (dsh2, dsc2, dffn0, dg1), (dup0, ddown0, dfcw0, dfcb0) = _ffn_bwd(
        dx2, dz0, x1, f0, u0, h0, ffnn[0], sc2, w_up_t[0], fconv_w[0] + h_g1["token"][0, 0], w_down[0], "l0",
        gate_y=y0, gate_g=g1)
    h_g0 = _exchange_start([dup0.reshape(NDEV, -1, D), ddown0.reshape(NDEV, DFF // NDEV, D)], scatter=True,
                           name="grads0_start")
    dcat = _mm_w(dy0, w_out, tb=True, name="out_proj_dx", tm=512)
    d_wout = _mm_tn((cat, dy0), name="out_proj_dw")
    dp_conv, dconv_w = _conv_gate_bwd(dcat, p_lat, conv_w + h_g0["token"][0, 0], name="conv_gate_bwd")
    dq_r, dk_all, dv_all = _attn_bwd(q_r, k_all, v_all, dcat, o_attn, lse, name="attn_bwd")
    dp_qkv, dqg_l, dkg_l = _qkv_bwd(p_lat, dq_r, dk_all, dv_all, qg, kg, cs_t, sn_t, has_q=True, kv_col=1,
                                    kv_row_off=lc, name="qkv_bwd")
    dp_ctx, _zero_qg, dkg_c = _qkv_bwd(p_ctx, None, dk_all, dv_all, qg, kg, None, None, has_q=False, kv_col=0,
                                       kv_row_off=0, name="qkv_bwd_ctx")
    da_ctx = _mm(dp_ctx, w_in_t[:D], name="in_proj_dx_ctx", tm=256, tn=512, tk=1024)
    d_win_qkv = _mm_tn([(dp_qkv, a_lat), (dp_ctx, a_ctx)], name="in_proj_dw_qkv")
    d_win_conv = _mm_tn((dp_conv, a_lat), name="in_proj_dw_conv")
    d_win_t = jnp.concatenate([d_win_qkv, d_win_conv], axis=0)
    grad_x, dsh1, dsc1, dmix0 = _mm_w_ep([dp_qkv, dp_conv], w_in_t, _ep_norm_bwd(False), [xs, dx1], [mixn[0], sc1],
                                         [F32], [D] * 3, tm=512, name="in_proj_dx_norm_bwd")
    _dctx, dcsh1, dcsc1, dmix0c = _norm_mod_bwd(da_ctx, ctxs, mixn[0], csc1, name="mix0_norm_bwd_ctx")

    z1k = jnp.zeros((1, D), F32)
    pack = jnp.concatenate(
        [v.reshape(-1) for v in (dsh1, dsc1, dg1, dsh2, dsc2, dg2, dsh1b, dsc1b, dg1b, dsh2b, dsc2b, dg2b,
                                 dcsh1, dcsc1, z1k, z1k, z1k, z1k,
                                 dmix0, dmix1, dmix0c, z1k, dffn0, dffn1, dqg_l, dkg_l + dkg_c,
                                 dfcb0, dfcb1, dconv_w, dfcw0, dfcw1, dpscale, loss_part[:, 0:1])])
    npack = pack.shape[0]
    pack = jnp.pad(pack, (0, (-npack) % 1024)).reshape(-1, 128)
    (g_pack,) = _all_gather([pack], name="gather_small_grads")
    gp = g_pack.reshape(NDEV, -1)
    off = [0]

    def take(size):
        seg = gp[:, off[0]:off[0] + size]
        off[0] += size
        return seg

    dmod_all = take(12 * D).reshape(NDEV, 2, 6 * D)
    dmodc_all = take(6 * D).reshape(NDEV, 1, 6 * D)
    dmix_all = take(4 * D).reshape(NDEV, 2, 2, D)
    dffn_all = take(2 * D).reshape(NDEV, 2, D)
    dqg_all = take(HD).reshape(NDEV, 1, HD)
    dkg_all = take(HD).reshape(NDEV, 1, HD)
    dfcb_all = take(2 * DFF).reshape(NDEV, 2, DFF)
    dconvw_all = take(3 * CW).reshape(NDEV, 3, CW)
    dfcw_all = take(6 * DFF).reshape(NDEV, 2, 3, DFF)
    dpscale_all = take(D).reshape(NDEV, D)
    loss_all = take(1)
    loss = loss_all[0, 0]
    for dev in range(1, NDEV):
        loss = loss + loss_all[dev, 0]

    dmodc_sum = dmodc_all[0]
    for dev in range(1, NDEV):
        dmodc_sum = dmodc_sum + dmodc_all[dev]
    my_cols = lambda a: lax.dynamic_slice_in_dim(a, me * acols, acols, axis=a.ndim - 1)
    rows0 = jnp.concatenate([my_cols(dmod_all[:, 0]), my_cols(dmodc_sum), jnp.zeros((7, acols), F32)], axis=0)
    rows1 = jnp.concatenate([my_cols(dmod_all[:, 1]), jnp.zeros((8, acols), F32)], axis=0)
    d_ada = jnp.stack([_mm(mraw, rows, ta=True, silu_a=True, name=f"ada_dw_{l}", tm=512, tn=256, tk=16)
                       for l, rows in enumerate((rows0, rows1))])
    dscc_part = _mm(rows0, ada_w[0], tb=True, name="ada_dcctx", tm=16, tn=512, tk=256)
    (g_dscc,) = _all_gather([dscc_part[8:16]], name="gather_dcctx")

    attn_shards = [d_win_t.reshape(NDEV, -1, D), d_wout.reshape(NDEV, D // NDEV, D)]
    attn_shards, g_dscc = lax.optimization_barrier((attn_shards, g_dscc))
    h_ga = _exchange_start(attn_shards, scatter=True, name="grads_attn_start")
    dmod_all = dmod_all + h_ga["token"][0, 0]

    outs = {}

    def put(nm, res):
        outs["grad_" + nm], outs["delta_" + nm], outs["new_m_" + nm], outs["new_v_" + nm] = res

    dmodc_pad = jnp.concatenate([dmodc_all, jnp.zeros_like(dmodc_all)], axis=1)
    put("ada_b", _adamw_nd(jnp.concatenate([dmod_all, dmodc_pad], axis=0), ada_b, m_ada_b, v_ada_b, name="adam_ada_b"))
    put("mix_norm", _adamw_nd(jnp.concatenate([dmix_all[:, 0], dmix_all[:, 1]], axis=0), mix_norm, m_mix_norm,
                              v_mix_norm, name="adam_mix_norm"))
    put("ffn_norm", _adamw_nd(dffn_all, ffn_norm, m_ffn_norm, v_ffn_norm, name="adam_ffn_norm"))
    put("even_q_gain", _adamw_nd(dqg_all, even_q_gain, m_even_q_gain, v_even_q_gain, name="adam_q_gain"))
    put("even_k_gain", _adamw_nd(dkg_all, even_k_gain, m_even_k_gain, v_even_k_gain, name="adam_k_gain"))
    put("ffn_conv_b", _adamw_nd(dfcb_all, ffn_conv_b, m_ffn_conv_b, v_ffn_conv_b, name="adam_ffn_conv_b"))
    my_convw = lax.dynamic_slice_in_dim(dconvw_all, me * ecw, ecw, axis=2)[:, None]
    put("even_conv_w", _adamw_nd(my_convw, even_conv_w, m_even_conv_w, v_even_conv_w, name="adam_even_conv_w"))
    my_fcw = lax.dynamic_slice_in_dim(dfcw_all, me * fcw, fcw, axis=3)
    put("ffn_conv_w", _adamw_nd(my_fcw, ffn_conv_w, m_ffn_conv_w, v_ffn_conv_w, name="adam_ffn_conv_w"))
    my_ps = lax.dynamic_slice_in_dim(dpscale_all, me * (D // NDEV), D // NDEV, axis=1)[:, None]
    put("odd_pool_scale", _adamw_nd(my_ps, odd_pool_scale, m_odd_pool_scale, v_odd_pool_scale, name="adam_pool_scale"))

    put("ada_w", _adamw_nd(d_ada[None], ada_w, m_ada_w, v_ada_w, name="adam_ada_w"))
    put("c_ctx", _adamw_nd(g_dscc[:, 0:1, :].reshape(NDEV, D), c_ctx, m_c_ctx, v_c_ctx, name="adam_c_ctx",
                           silu_grad_of=c_ctx))

    r_pool, r_up1, r_down1 = _exchange_wait(h_g1, outs["grad_ada_b"], name="grads1_wait")
    r_up0, r_down0 = _exchange_wait(h_g0, outs["grad_mix_norm"], name="grads0_wait")
    r_win, r_wout = _exchange_wait(h_ga, outs["grad_c_ctx"], name="grads_attn_wait")
    put("even_w_in", [tr(a) for a in _adamw_nd(r_win[:, None], tr(even_w_in), tr(m_even_w_in), tr(v_even_w_in),
                                               name="adam_w_in")])
    put("even_w_out", _adamw_nd(r_wout[:, None], even_w_out, m_even_w_out, v_even_w_out, name="adam_w_out"))
    put("odd_pool_w", _adamw_nd(r_pool[:, None], odd_pool_w, m_odd_pool_w, v_odd_pool_w, name="adam_pool_w"))
    put("ffn_w_up", [tr(a) for a in _adamw_nd([r_up0, r_up1], tr(ffn_w_up), tr(m_ffn_w_up), tr(v_ffn_w_up),
                                              name="adam_w_up")])
    put("ffn_w_down", _adamw_nd([r_down0, r_down1], ffn_w_down, m_ffn_w_down, v_ffn_w_down, name="adam_w_down"))

    names = ["c_ctx", "ada_w", "ada_b", "mix_norm", "ffn_norm", "even_w_in", "even_q_gain", "even_k_gain",
             "even_conv_w", "even_w_out", "odd_pool_w", "odd_pool_scale", "ffn_w_up", "ffn_conv_w", "ffn_conv_b",
             "ffn_w_down"]
    result = [loss, grad_x[None]]
    for kind in ("grad_", "delta_", "new_m_", "new_v_"):
        result += [outs[kind + nm] for nm in names]
    return tuple(result)
```

```python
import functools
import math

import jax
import jax.numpy as jnp
from jax import lax
from jax.experimental import pallas as pl
from jax.experimental.pallas import tpu as pltpu

F32 = jnp.float32
BF16 = jnp.bfloat16

D = 1024
HD = 128
NQ = 4
NKV = 2
AW = NQ * HD
CW = D - AW
DFF = 2816
GRID_W = 64
ROPE_THETA = 10000.0
POOL_WINDOWS = (2, 4, 8, 16)
PG = D // 4
EPS = 1e-6
NDEV = 8
HALO = 8
MESH = pl.DeviceIdType.MESH

ADAM_LR = 0.001
ADAM_B1 = 0.9
ADAM_B2 = 0.999
ADAM_EPS = 1e-08
ADAM_WD = 0.01
ADAM_STEP = 10


def _pick(dim, prefs):
    for p in prefs:
        if dim % p == 0:
            return p
    return dim


def _params(*sem):
    return pltpu.CompilerParams(dimension_semantics=sem)


_NT = (((1,), (1,)), ((), ()))
_TN = (((0,), (0,)), ((), ()))
_SCALE = HD ** -0.5
_QSCALE = _SCALE * math.log2(math.e)
_LN2 = math.log(2.0)


def _mm(a_list, b, *, name, ta=False, tb=False, out_dtype=F32, silu_a=False, bias=None, tm=None, tn=None, tk=None):
    if not isinstance(a_list, (list, tuple)):
        a_list = [a_list]
    na = len(a_list)
    assert not (ta and na > 1)
    if ta:
        kdim, m = a_list[0].shape
        ks = [kdim]
    else:
        m = a_list[0].shape[0]
        ks = [a.shape[1] for a in a_list]
        kdim = sum(ks)
    n = b.shape[0] if tb else b.shape[1]
    assert (b.shape[1] if tb else b.shape[0]) == kdim
    kunit = math.gcd(*ks) if na > 1 else kdim
    tm = min(tm, m) if tm else _pick(m, (512, 256, 128, 64, 32, 16, 8))
    tn = min(tn, n) if tn else _pick(n, (512, 256, 128))
    tk = min(tk, kunit) if tk else _pick(kunit, (1024, 768, 512, 256, 128))
    assert m % tm == 0 and n % tn == 0 and all(k % tk == 0 for k in ks)
    nks = [k // tk for k in ks]
    starts = [sum(nks[:i]) for i in range(na)]
    nk = sum(nks)
    has_bias = bias is not None

    def body(*refs):
        a_refs = refs[:na]
        b_ref = refs[na]
        bias_ref = refs[na + 1] if has_bias else None
        o_ref = refs[na + 1 + has_bias]
        acc = refs[-1]
        k = pl.program_id(2)

        @pl.when(k == 0)
        def _():
            acc[...] = jnp.zeros_like(acc)

        bv = b_ref[...].astype(BF16)
        dn = (((0 if ta else 1,), (1 if tb else 0,)), ((), ()))
        for idx in range(na):
            def step(idx=idx):
                av = a_refs[idx][...]
                if silu_a:
                    av = av * jax.nn.sigmoid(av)
                acc[...] += lax.dot_general(av.astype(BF16), bv, dn, preferred_element_type=F32)
            if na == 1:
                step()
            else:
                pl.when((k >= starts[idx]) & (k < starts[idx] + nks[idx]))(step)

        @pl.when(k == nk - 1)
        def _():
            r = acc[...]
            if has_bias:
                r = r + bias_ref[...]
            o_ref[...] = r.astype(o_ref.dtype)

    in_specs = []
    for idx in range(na):
        if ta:
            in_specs.append(pl.BlockSpec((tk, tm), lambda i, j, k: (k, i)))
        else:
            lo, cnt = starts[idx], nks[idx]
            in_specs.append(pl.BlockSpec((tm, tk), lambda i, j, k, lo=lo, cnt=cnt: (i, jnp.clip(k - lo, 0, cnt - 1))))
    if tb:
        in_specs.append(pl.BlockSpec((tn, tk), lambda i, j, k: (j, k)))
    else:
        in_specs.append(pl.BlockSpec((tk, tn), lambda i, j, k: (k, j)))
    args = list(a_list) + [b]
    if has_bias:
        in_specs.append(pl.BlockSpec((1, tn), lambda i, j, k: (0, j)))
        args.append(bias)
    return pl.pallas_call(
        body, grid=(m // tm, n // tn, nk), in_specs=in_specs,
        out_specs=pl.BlockSpec((tm, tn), lambda i, j, k: (i, j)),
        out_shape=jax.ShapeDtypeStruct((m, n), out_dtype),
        scratch_shapes=[pltpu.VMEM((tm, tn), F32)], name=name,
        compiler_params=_params("parallel", "parallel", "arbitrary"))(*args)


def _mm_w(a_list, w, *, name, tb=False, tm=256, out_dtype=F32):
    if not isinstance(a_list, (list, tuple)):
        a_list = [a_list]
    na = len(a_list)
    m = a_list[0].shape[0]
    ks = [a.shape[1] for a in a_list]
    offs = [sum(ks[:i]) for i in range(na)]
    n = w.shape[0] if tb else w.shape[1]
    assert (w.shape[1] if tb else w.shape[0]) == sum(ks)
    tm = min(tm, m)
    assert m % tm == 0

    def body(*refs):
        a_refs, w_ref, o_ref = refs[:na], refs[na], refs[na + 1]
        acc = None
        for idx in range(na):
            av = a_refs[idx][...].astype(BF16)
            if tb:
                part = lax.dot_general(av, w_ref[:, offs[idx]:offs[idx] + ks[idx]], _NT, preferred_element_type=F32)
            else:
                part = jnp.dot(av, w_ref[offs[idx]:offs[idx] + ks[idx], :], preferred_element_type=F32)
            acc = part if acc is None else acc + part
        o_ref[...] = acc.astype(o_ref.dtype)

    in_specs = [pl.BlockSpec((tm, k), lambda i: (i, 0)) for k in ks] + [pl.BlockSpec(w.shape, lambda i: (0, 0))]
    return pl.pallas_call(
        body, grid=(m // tm,), in_specs=in_specs, out_specs=pl.BlockSpec((tm, n), lambda i: (i, 0)),
        out_shape=jax.ShapeDtypeStruct((m, n), out_dtype), name=name, compiler_params=_params("parallel"))(*a_list, w)


def _mm_w_ep(a_list, w, epilogue, row_in, vec_in, out_dtypes, sum_widths, *, name, tb=False, tm=256, sub=256):
    if not isinstance(a_list, (list, tuple)):
        a_list = [a_list]
    na, nr, nv, no, ns = len(a_list), len(row_in), len(vec_in), len(out_dtypes), len(sum_widths)
    m = a_list[0].shape[0]
    ks = [a.shape[1] for a in a_list]
    offs = [sum(ks[:i]) for i in range(na)]
    n = w.shape[0] if tb else w.shape[1]
    assert (w.shape[1] if tb else w.shape[0]) == sum(ks)
    tm = min(tm, m)
    sub = min(sub, tm)
    assert m % tm == 0 and tm % sub == 0

    def body(*refs):
        a_refs, w_ref = refs[:na], refs[na]
        row_refs = refs[na + 1:na + 1 + nr]
        vec_refs = refs[na + 1 + nr:na + 1 + nr + nv]
        out_refs = refs[na + 1 + nr + nv:na + 1 + nr + nv + no]
        sum_refs = refs[na + 1 + nr + nv + no:]

        @pl.when(pl.program_id(0) == 0)
        def _():
            for s_ref in sum_refs:
                s_ref[...] = jnp.zeros_like(s_ref)

        vecs = [v[...] for v in vec_refs]
        for r0 in range(0, tm, sub):
            acc = None
            for idx in range(na):
                av = a_refs[idx][r0:r0 + sub, :].astype(BF16)
                if tb:
                    part = lax.dot_general(av, w_ref[:, offs[idx]:offs[idx] + ks[idx]], _NT, preferred_element_type=F32)
                else:
                    part = jnp.dot(av, w_ref[offs[idx]:offs[idx] + ks[idx], :], preferred_element_type=F32)
                acc = part if acc is None else acc + part
            outs, sums = epilogue(acc, [r[r0:r0 + sub, :] for r in row_refs], vecs)
            for o_ref, o in zip(out_refs, outs):
                o_ref[r0:r0 + sub, :] = o.astype(o_ref.dtype)
            for s_ref, s in zip(sum_refs, sums):
                s_ref[...] += s

    row = pl.BlockSpec((tm, n), lambda i: (i, 0))
    in_specs = ([pl.BlockSpec((tm, k), lambda i: (i, 0)) for k in ks] + [pl.BlockSpec(w.shape, lambda i: (0, 0))]
                + [row] * nr + [_vec(n)] * nv)
    return pl.pallas_call(
        body, grid=(m // tm,), in_specs=in_specs, out_specs=[row] * no + [_vec(sw) for sw in sum_widths],
        out_shape=[jax.ShapeDtypeStruct((m, n), dt) for dt in out_dtypes]
        + [jax.ShapeDtypeStruct((1, sw), F32) for sw in sum_widths],
        name=name, compiler_params=_params("arbitrary" if ns else "parallel"))(*a_list, w, *row_in, *vec_in)


def _ep_norm_bwd(has_gate):
    def ep(dav, rows, vecs):
        xv = rows[0]
        gain, scv = vecs[0], vecs[1]
        r = lax.rsqrt(jnp.mean(xv * xv, axis=-1, keepdims=True) + EPS)
        xh = xv * r
        nrm = xh * gain
        dn = dav * (1.0 + scv)
        dxh = dn * gain
        dx = r * (dxh - xh * jnp.mean(dxh * xh, axis=-1, keepdims=True)) + rows[1]
        outs, sums = [dx], [_colsum(dav), _colsum(dav * nrm), _colsum(dn * xh)]
        if has_gate:
            outs.append(dx * vecs[2])
            sums.append(_colsum(dx * rows[2]))
        return outs, sums
    return ep


def _ep_loss(d):
    def ep(zv, rows, vecs):
        xv, tv = rows
        gv = vecs[0]
        diff = (xv + gv * zv) - tv
        dx = diff * (1.0 / d)
        part = 0.5 * jnp.sum(jnp.mean(diff * diff, axis=-1, keepdims=True), axis=0, keepdims=True)
        return [dx, dx * gv], [jnp.broadcast_to(part, (1, 128)), _colsum(dx * zv)]
    return ep


def _ep_resid_norm(yv, rows, vecs):
    g, gain, scv, shv = vecs
    xv = rows[0] + g * yv
    r = lax.rsqrt(jnp.mean(xv * xv, axis=-1, keepdims=True) + EPS)
    return [yv, xv, ((xv * r) * gain) * (1.0 + scv) + shv], []


def _mm_tn(pairs, *, name, tk=1024, out_dtype=BF16, blocks=1, block=0, into=None):
    if not isinstance(pairs, list):
        pairs = [pairs]
    m, n = pairs[0][0].shape[1], pairs[0][1].shape[1]
    tks = [min(tk, a.shape[0]) for a, _ in pairs]
    nks = [a.shape[0] // t for (a, _), t in zip(pairs, tks)]
    assert all(a.shape[0] == b.shape[0] and a.shape[0] % t == 0 for (a, b), t in zip(pairs, tks))
    starts = [sum(nks[:i]) for i in range(len(pairs))]
    nk = sum(nks)

    def body(*refs):
        o_ref, acc = refs[-2], refs[-1]
        k = pl.program_id(0)

        @pl.when(k == 0)
        def _():
            acc[...] = jnp.zeros_like(acc)

        for idx in range(len(pairs)):
            a_ref, b_ref = refs[2 * idx], refs[2 * idx + 1]

            def step(a_ref=a_ref, b_ref=b_ref):
                acc[...] += lax.dot_general(a_ref[...].astype(BF16), b_ref[...].astype(BF16), _TN,
                                            preferred_element_type=F32)

            if len(pairs) == 1:
                step()
            else:
                pl.when((k >= starts[idx]) & (k < starts[idx] + nks[idx]))(step)

        @pl.when(k == nk - 1)
        def _():
            o_ref[...] = acc[...].astype(o_ref.dtype)

    in_specs, args = [], []
    for (a, b), t, lo, cnt in zip(pairs, tks, starts, nks):
        idx_map = lambda k, lo=lo, cnt=cnt: (jnp.clip(k - lo, 0, cnt - 1), 0)
        in_specs += [pl.BlockSpec((t, m), idx_map), pl.BlockSpec((t, n), idx_map)]
        args += [a, b]
    aliases = {}
    if into is not None:
        aliases = {len(args): 0}
        in_specs.append(pl.BlockSpec(memory_space=pl.ANY))
        args.append(into)
    return pl.pallas_call(
        body, grid=(nk,), in_specs=in_specs, out_specs=pl.BlockSpec((m, n), lambda k: (block, 0)),
        out_shape=jax.ShapeDtypeStruct((m * blocks, n), out_dtype), scratch_shapes=[pltpu.VMEM((m, n), F32)],
        input_output_aliases=aliases, name=name, compiler_params=_params("arbitrary"))(*args)


def _vec(d, col=None):
    if col is None:
        return pl.BlockSpec((1, d), lambda i, *_: (0, 0))
    return pl.BlockSpec((1, d), col)


def _halo_specs(tm, width, nrows, colblk=0, row_off=0):
    r = tm // HALO
    off = row_off // HALO
    last = nrows // HALO - 1
    prev = pl.BlockSpec((HALO, width), lambda i, *_: (off + jnp.maximum(i * r - 1, 0), colblk))
    nxt = pl.BlockSpec((HALO, width), lambda i, *_: (off + jnp.minimum((i + 1) * r, last), colblk))
    return prev, nxt


def _ext(prev_ref, main_ref, next_ref, i, ni):
    p = jnp.where(i > 0, prev_ref[...], 0.0)
    n = jnp.where(i < ni - 1, next_ref[...], 0.0)
    return jnp.concatenate([p, main_ref[...], n], axis=0)


def _sh(ext, k, tm):
    if k == 0:
        return ext[HALO:HALO + tm]
    rows = ext.shape[0]
    return pltpu.roll(ext, (-k) % rows, axis=0)[HALO:HALO + tm]


def _roll_rows(v, k):
    rows = v.shape[0]
    return pltpu.roll(v, (-k) % rows, axis=0) if k % rows else v


def _conv3(ext, w_ref, tm):
    return _sh(ext, -1, tm) * w_ref[0:1, :] + _sh(ext, 0, tm) * w_ref[1:2, :] + _sh(ext, 1, tm) * w_ref[2:3, :]


def _colsum(v):
    return jnp.sum(v, axis=0, keepdims=True)


def _acc_out(ref, i, val):
    @pl.when(i == 0)
    def _():
        ref[...] = jnp.zeros_like(ref)

    ref[...] += val


def _sigmoid(v):
    return jax.nn.sigmoid(v)


def _norm_mod(x, gain, sc, sh, *, name, y=None, g=None, ymul=None, tm=512):
    n, d = x.shape
    tm = min(tm, n)
    has_res = y is not None
    has_mul = ymul is not None

    def body(*refs):
        it = iter(refs)
        x_ref = next(it)
        y_ref = next(it) if has_res else None
        g_ref = next(it) if has_res else None
        m_ref = next(it) if has_mul else None
        gain_ref, sc_ref, sh_ref = next(it), next(it), next(it)
        xo_ref = next(it) if has_res else None
        a_ref = next(it)
        xv = x_ref[...]
        if has_res:
            yv = y_ref[...]
            if has_mul:
                yv = yv * m_ref[...]
            xv = xv + g_ref[...] * yv
            xo_ref[...] = xv
        r = lax.rsqrt(jnp.mean(xv * xv, axis=-1, keepdims=True) + EPS)
        nrm = (xv * r) * gain_ref[...]
        a_ref[...] = (nrm * (1.0 + sc_ref[...]) + sh_ref[...]).astype(BF16)

    row = pl.BlockSpec((tm, d), lambda i: (i, 0))
    in_specs, args = [row], [x]
    if has_res:
        in_specs += [row, _vec(d)]
        args += [y, g]
    if has_mul:
        in_specs.append(_vec(d))
        args.append(ymul)
    in_specs += [_vec(d)] * 3
    args += [gain, sc, sh]
    out_specs, out_shape = [], []
    if has_res:
        out_specs.append(row)
        out_shape.append(jax.ShapeDtypeStruct((n, d), F32))
    out_specs.append(row)
    out_shape.append(jax.ShapeDtypeStruct((n, d), BF16))
    res = pl.pallas_call(body, grid=(n // tm,), in_specs=in_specs, out_specs=out_specs, out_shape=out_shape,
                         name=name, compiler_params=_params("parallel"))(*args)
    return res if has_res else res[0]


def _norm_mod_bwd(da, x, gain, sc, *, name, dres=None, gate_y=None, gate_g=None, tm=512):
    n, d = x.shape
    tm = min(tm, n)
    has_res = dres is not None
    has_gate = gate_y is not None

    def body(*refs):
        it = iter(refs)
        da_ref, x_ref = next(it), next(it)
        r_ref = next(it) if has_res else None
        y_ref = next(it) if has_gate else None
        g_ref = next(it) if has_gate else None
        gain_ref, sc_ref = next(it), next(it)
        dx_ref, dsh_ref, dsc_ref, dgn_ref = next(it), next(it), next(it), next(it)
        dy_ref = next(it) if has_gate else None
        dg_ref = next(it) if has_gate else None
        i = pl.program_id(0)
        xv = x_ref[...]
        dav = da_ref[...]
        r = lax.rsqrt(jnp.mean(xv * xv, axis=-1, keepdims=True) + EPS)
        xh = xv * r
        nrm = xh * gain_ref[...]
        dn = dav * (1.0 + sc_ref[...])
        dxh = dn * gain_ref[...]
        dx = r * (dxh - xh * jnp.mean(dxh * xh, axis=-1, keepdims=True))
        if has_res:
            dx = dx + r_ref[...]
        dx_ref[...] = dx
        _acc_out(dsh_ref, i, _colsum(dav))
        _acc_out(dsc_ref, i, _colsum(dav * nrm))
        _acc_out(dgn_ref, i, _colsum(dn * xh))
        if has_gate:
            dy_ref[...] = (dx * g_ref[...]).astype(BF16)
            _acc_out(dg_ref, i, _colsum(dx * y_ref[...]))

    row = pl.BlockSpec((tm, d), lambda i: (i, 0))
    in_specs, args = [row, row], [da, x]
    if has_res:
        in_specs.append(row)
        args.append(dres)
    if has_gate:
        in_specs += [row, _vec(d)]
        args += [gate_y, gate_g]
    in_specs += [_vec(d)] * 2
    args += [gain, sc]
    vec_shape = jax.ShapeDtypeStruct((1, d), F32)
    out_specs = [row, _vec(d), _vec(d), _vec(d)]
    out_shape = [jax.ShapeDtypeStruct((n, d), F32), vec_shape, vec_shape, vec_shape]
    if has_gate:
        out_specs += [row, _vec(d)]
        out_shape += [jax.ShapeDtypeStruct((n, d), BF16), vec_shape]
    return pl.pallas_call(
        body, grid=(n // tm,), in_specs=in_specs, out_specs=out_specs, out_shape=out_shape,
        name=name, compiler_params=_params("arbitrary"))(*args)


def _ffn_up_glu(f, w_up, cw, cb, *, name, tm=256, tc=256):
    n, d = f.shape
    tm = min(tm, n)
    ni = n // tm
    nc = DFF // tc
    halo = 16
    rows = tm + 2 * halo
    r = tm // halo
    last = n // halo - 1

    def body(f_ref, fp_ref, fn_ref, w_ref, cw_ref, cb_ref, u_ref, gc_ref, h_ref):
        i = pl.program_id(0)
        a = f_ref[...]
        aext = jnp.concatenate([jnp.where(i > 0, fp_ref[...], jnp.zeros_like(fp_ref[...])), a,
                                jnp.where(i < ni - 1, fn_ref[...], jnp.zeros_like(fn_ref[...]))], axis=0)
        for j in range(nc):
            cols = slice(j * tc, (j + 1) * tc)
            vcols = slice(DFF + j * tc, DFF + (j + 1) * tc)
            gext = jnp.dot(aext, w_ref[:, cols], preferred_element_type=F32)
            val = jnp.dot(a, w_ref[:, vcols], preferred_element_type=F32)
            gate = gext[halo:halo + tm]
            gc = (pltpu.roll(gext, 1, axis=0)[halo:halo + tm] * cw_ref[0:1, cols] + gate * cw_ref[1:2, cols]
                  + pltpu.roll(gext, rows - 1, axis=0)[halo:halo + tm] * cw_ref[2:3, cols]) + cb_ref[:, cols]
            u_ref[:, cols] = gate
            u_ref[:, vcols] = val
            gc_ref[:, cols] = gc
            h_ref[:, cols] = (gc * _sigmoid(gc) * val).astype(BF16)

    return pl.pallas_call(
        body, grid=(ni,),
        in_specs=[pl.BlockSpec((tm, d), lambda i: (i, 0)),
                  pl.BlockSpec((halo, d), lambda i: (jnp.maximum(i * r - 1, 0), 0)),
                  pl.BlockSpec((halo, d), lambda i: (jnp.minimum((i + 1) * r, last), 0)),
                  pl.BlockSpec(w_up.shape, lambda i: (0, 0)), pl.BlockSpec((3, DFF), lambda i: (0, 0)),
                  pl.BlockSpec((1, DFF), lambda i: (0, 0))],
        out_specs=[pl.BlockSpec((tm, 2 * DFF), lambda i: (i, 0)), pl.BlockSpec((tm, DFF), lambda i: (i, 0)),
                   pl.BlockSpec((tm, DFF), lambda i: (i, 0))],
        out_shape=[jax.ShapeDtypeStruct((n, 2 * DFF), F32), jax.ShapeDtypeStruct((n, DFF), F32),
                   jax.ShapeDtypeStruct((n, DFF), BF16)], name=name,
        compiler_params=_params("parallel"))(f, f, f, w_up, cw, cb)


def _ffn_down_glu_bwd(dz, w_down, u, gc, cw, *, name, tm=256, tc=256):
    n, d = dz.shape
    tm = min(tm, n)
    ni = n // tm
    nc = DFF // tc
    rows = tm + 2 * HALO

    def body(z_ref, zp_ref, zn_ref, w_ref, u_ref, vp_ref, vn_ref, c_ref, cp_ref, cn_ref, cw_ref,
             dg_ref, dv_ref, dcw_ref, dcb_ref):
        i = pl.program_id(0)

        @pl.when(i == 0)
        def _():
            dcw_ref[...] = jnp.zeros_like(dcw_ref)
            dcb_ref[...] = jnp.zeros_like(dcb_ref)

        zext = jnp.concatenate([jnp.where(i > 0, zp_ref[...], jnp.zeros_like(zp_ref[...])), z_ref[...],
                                jnp.where(i < ni - 1, zn_ref[...], jnp.zeros_like(zn_ref[...]))], axis=0)
        for j in range(nc):
            cols = slice(j * tc, (j + 1) * tc)
            vcols = slice(DFF + j * tc, DFF + (j + 1) * tc)
            dh = lax.dot_general(zext, w_ref[cols, :], _NT, preferred_element_type=F32)[HALO:HALO + rows]
            gcx = jnp.concatenate([cp_ref[:, cols], c_ref[:, cols], cn_ref[:, cols]], axis=0)
            vext = jnp.concatenate([vp_ref[:, cols], u_ref[:, vcols], vn_ref[:, cols]], axis=0)
            sg = _sigmoid(gcx)
            dgc = dh * vext * (sg * (1.0 + gcx * (1.0 - sg)))
            dv_ref[:, cols] = (dh[HALO:HALO + tm] * (gcx[HALO:HALO + tm] * sg[HALO:HALO + tm])).astype(BF16)
            d_next = pltpu.roll(dgc, rows - 1, axis=0)[HALO:HALO + tm]
            d_prev = pltpu.roll(dgc, 1, axis=0)[HALO:HALO + tm]
            d_here = dgc[HALO:HALO + tm]
            dg_ref[:, cols] = (d_next * cw_ref[0:1, cols] + d_here * cw_ref[1:2, cols]
                               + d_prev * cw_ref[2:3, cols]).astype(BF16)
            gate = u_ref[:, cols]
            dcw_ref[:, cols] += jnp.concatenate([_colsum(d_next * gate), _colsum(d_here * gate),
                                                 _colsum(d_prev * gate)], axis=0)
            dcb_ref[:, cols] += _colsum(d_here)

    def trio(width, halo, tile_width=None, colblk=0):
        r, last = tm // halo, n // halo - 1
        return [pl.BlockSpec((tm, tile_width or width), lambda i: (i, 0)),
                pl.BlockSpec((halo, width), lambda i: (jnp.maximum(i * r - 1, 0), colblk)),
                pl.BlockSpec((halo, width), lambda i: (jnp.minimum((i + 1) * r, last), colblk))]

    whole = lambda shape: pl.BlockSpec(shape, lambda i: (0, 0))
    return pl.pallas_call(
        body, grid=(ni,),
        in_specs=(trio(d, 16) + [whole(w_down.shape)] + trio(DFF, HALO, tile_width=2 * DFF, colblk=1)
                  + trio(DFF, HALO) + [whole((3, DFF))]),
        out_specs=[pl.BlockSpec((tm, DFF), lambda i: (i, 0)), pl.BlockSpec((tm, DFF), lambda i: (i, 0)),
                   whole((3, DFF)), whole((1, DFF))],
        out_shape=[jax.ShapeDtypeStruct((n, DFF), BF16), jax.ShapeDtypeStruct((n, DFF), BF16),
                   jax.ShapeDtypeStruct((3, DFF), F32), jax.ShapeDtypeStruct((1, DFF), F32)],
        name=name, compiler_params=_params("arbitrary"))(dz, dz, dz, w_down, u, u, u, gc, gc, gc, cw)


def _rope_tables(n):
    rows = n // GRID_W
    axis_dim = HD // 2
    inv_freq = jnp.power(ROPE_THETA, -jnp.arange(0, axis_dim, 2, dtype=F32) / axis_dim)
    ar = jnp.arange(rows, dtype=F32)[:, None] * inv_freq
    ac = jnp.arange(GRID_W, dtype=F32)[:, None] * inv_freq
    by_row = lambda a: jnp.repeat(a, GRID_W, axis=0)
    by_col = lambda a: jnp.tile(a, (rows, 1))
    cr, sr, cc, sc = by_row(jnp.cos(ar)), by_row(jnp.sin(ar)), by_col(jnp.cos(ac)), by_col(jnp.sin(ac))
    return jnp.concatenate([cr, cr, cc, cc], axis=1), jnp.concatenate([-sr, sr, -sc, sc], axis=1)


def _partner(v):
    lane = lax.broadcasted_iota(jnp.int32, v.shape, 1)
    return jnp.where((lane % 64) < 32, pltpu.roll(v, HD - 32, axis=1), pltpu.roll(v, 32, axis=1))


def _qkv_prep(p, q_gain, k_gain, cs, sn, *, name, has_q, kv_col, kv_rows=None, kv_row_off=0, kv_into=None, tm=256):
    n = p.shape[0]
    rope = cs is not None
    kv_rows = kv_rows or n
    rb = kv_row_off // tm

    def body(*refs):
        it = iter(refs)
        q_ref = next(it) if has_q else None
        kv_ref = next(it)
        qg_ref, kg_ref = next(it), next(it)
        cs_ref = next(it) if rope else None
        sn_ref = next(it) if rope else None
        if kv_into is not None:
            next(it), next(it)
        qo_ref = next(it) if has_q else None
        ko_ref, vo_ref = next(it), next(it)

        def norm_rope(xh, gain, mul=None):
            r = lax.rsqrt(jnp.mean(xh * xh, axis=-1, keepdims=True) + EPS)
            xn = (xh * r) * gain
            if rope:
                xn = xn * cs_ref[...] + _partner(xn) * sn_ref[...]
            if mul is not None:
                xn = xn * mul
            return xn.astype(BF16)

        if has_q:
            for h in range(NQ):
                qo_ref[h] = norm_rope(q_ref[:, h * HD:(h + 1) * HD], qg_ref[...], _QSCALE)
        for h in range(NKV):
            ko_ref[h] = norm_rope(kv_ref[:, h * HD:(h + 1) * HD], kg_ref[...])
            vo_ref[h] = kv_ref[:, (NKV + h) * HD:(NKV + h + 1) * HD].astype(BF16)

    in_specs, args = [], []
    if has_q:
        in_specs.append(pl.BlockSpec((tm, AW), lambda i: (i, 0)))
        args.append(p)
    in_specs += [pl.BlockSpec((tm, 2 * NKV * HD), lambda i: (i, kv_col)), _vec(HD), _vec(HD)]
    args += [p, q_gain, k_gain]
    if rope:
        in_specs += [pl.BlockSpec((tm, HD), lambda i: (i, 0))] * 2
        args += [cs, sn]
    out_specs, out_shape = [], []
    if has_q:
        out_specs.append(pl.BlockSpec((NQ, tm, HD), lambda i: (0, i, 0)))
        out_shape.append(jax.ShapeDtypeStruct((NQ, n, HD), BF16))
    out_specs += [pl.BlockSpec((NKV, tm, HD), lambda i: (0, rb + i, 0))] * 2
    out_shape += [jax.ShapeDtypeStruct((NKV, kv_rows, HD), BF16)] * 2
    aliases = {}
    if kv_into is not None:
        aliases = {len(args): int(has_q), len(args) + 1: int(has_q) + 1}
        in_specs += [pl.BlockSpec(memory_space=pl.ANY)] * 2
        args += list(kv_into)
    return pl.pallas_call(body, grid=(n // tm,), in_specs=in_specs, out_specs=out_specs, out_shape=out_shape,
                          input_output_aliases=aliases, name=name, compiler_params=_params("parallel"))(*args)


def _in_proj_qkv(a, w_in_t, q_gain, k_gain, cs, sn, conv_w, kv_into, *, name, kv_row_off, tm=256):
    n, d = a.shape
    nproj = w_in_t.shape[0]
    nqkv = AW + 2 * NKV * HD
    rb = kv_row_off // tm
    ni = n // tm
    halo = 16
    rows = tm + 2 * halo
    r = tm // halo
    last = n // halo - 1

    def body(a_ref, ap_ref, an_ref, w_ref, qg_ref, kg_ref, cs_ref, sn_ref, cw_ref, _k_in, _v_in,
             p_ref, qo_ref, ko_ref, vo_ref, conv_ref):
        i = pl.program_id(0)
        av = a_ref[...]
        aext = jnp.concatenate([jnp.where(i > 0, ap_ref[...], jnp.zeros_like(ap_ref[...])), av,
                                jnp.where(i < ni - 1, an_ref[...], jnp.zeros_like(an_ref[...]))], axis=0)
        qkv = lax.dot_general(av, w_ref[0:nqkv, :], _NT, preferred_element_type=F32)
        p_ref[:, 0:nqkv] = qkv
        cext = lax.dot_general(aext, w_ref[nqkv:nproj, :], _NT, preferred_element_type=F32)
        p_ref[:, nqkv:nproj] = cext[halo:halo + tm]
        hext = cext[:, CW:2 * CW] * cext[:, 2 * CW:3 * CW]
        cv3 = (pltpu.roll(hext, 1, axis=0)[halo:halo + tm] * cw_ref[0:1, :] + hext[halo:halo + tm] * cw_ref[1:2, :]
               + pltpu.roll(hext, rows - 1, axis=0)[halo:halo + tm] * cw_ref[2:3, :])
        conv_ref[...] = (cext[halo:halo + tm, 0:CW] * cv3).astype(BF16)

        def norm_rope(xh, gain, mul=None):
            r = lax.rsqrt(jnp.mean(xh * xh, axis=-1, keepdims=True) + EPS)
            xn = (xh * r) * gain
            xn = xn * cs_ref[...] + _partner(xn) * sn_ref[...]
            if mul is not None:
                xn = xn * mul
            return xn.astype(BF16)

        for h in range(NQ):
            qo_ref[h] = norm_rope(qkv[:, h * HD:(h + 1) * HD], qg_ref[...], _QSCALE)
        for h in range(NKV):
            ko_ref[h] = norm_rope(qkv[:, AW + h * HD:AW + (h + 1) * HD], kg_ref[...])
            vo_ref[h] = qkv[:, AW + (NKV + h) * HD:AW + (NKV + h + 1) * HD].astype(BF16)

    kv_rows = kv_into[0].shape[1]
    tab = pl.BlockSpec((tm, HD), lambda i: (i, 0))
    any_spec = pl.BlockSpec(memory_space=pl.ANY)
    kv_spec = pl.BlockSpec((NKV, tm, HD), lambda i: (0, rb + i, 0))
    return pl.pallas_call(
        body, grid=(ni,),
        in_specs=[pl.BlockSpec((tm, d), lambda i: (i, 0)),
                  pl.BlockSpec((halo, d), lambda i: (jnp.maximum(i * r - 1, 0), 0)),
                  pl.BlockSpec((halo, d), lambda i: (jnp.minimum((i + 1) * r, last), 0)),
                  pl.BlockSpec(w_in_t.shape, lambda i: (0, 0)), _vec(HD), _vec(HD), tab, tab,
                  pl.BlockSpec((3, CW), lambda i: (0, 0)), any_spec, any_spec],
        out_specs=[pl.BlockSpec((tm, nproj), lambda i: (i, 0)), pl.BlockSpec((NQ, tm, HD), lambda i: (0, i, 0)),
                   kv_spec, kv_spec, pl.BlockSpec((tm, CW), lambda i: (i, 0))],
        out_shape=[jax.ShapeDtypeStruct((n, nproj), F32), jax.ShapeDtypeStruct((NQ, n, HD), BF16),
                   jax.ShapeDtypeStruct((NKV, kv_rows, HD), BF16), jax.ShapeDtypeStruct((NKV, kv_rows, HD), BF16),
                   jax.ShapeDtypeStruct((n, CW), BF16)],
        input_output_aliases={9: 2, 10: 3}, name=name,
        compiler_params=_params("parallel"))(a, a, a, w_in_t, q_gain, k_gain, cs, sn, conv_w, *kv_into)


def _qkv_bwd(p, dq, dk, dv, q_gain, k_gain, cs, sn, *, name, has_q, kv_col, kv_row_off, tm=256):
    n = p.shape[0]
    rope = cs is not None
    rb = kv_row_off // tm

    def body(*refs):
        it = iter(refs)
        q_ref = next(it) if has_q else None
        kv_ref = next(it)
        dq_ref = next(it) if has_q else None
        dk_ref, dv_ref = next(it), next(it)
        qg_ref, kg_ref = next(it), next(it)
        cs_ref = next(it) if rope else None
        sn_ref = next(it) if rope else None
        dp_ref, dqg_ref, dkg_ref = next(it), next(it), next(it)
        i = pl.program_id(0)

        def back(xh, dout, gain):
            if rope:
                dout = dout * cs_ref[...] + _partner(dout * sn_ref[...])
            r = lax.rsqrt(jnp.mean(xh * xh, axis=-1, keepdims=True) + EPS)
            xhat = xh * r
            dxh = dout * gain
            dx = r * (dxh - xhat * jnp.mean(dxh * xhat, axis=-1, keepdims=True))
            return dx, _colsum(dout * xhat)

        dqg = jnp.zeros((1, HD), F32)
        dkg = jnp.zeros((1, HD), F32)
        if has_q:
            for h in range(NQ):
                dx, dg = back(q_ref[:, h * HD:(h + 1) * HD], dq_ref[h], qg_ref[...])
                dp_ref[:, h * HD:(h + 1) * HD] = dx.astype(BF16)
                dqg = dqg + dg
        else:
            dp_ref[:, 0:AW] = jnp.zeros((tm, AW), BF16)
        for h in range(NKV):
            dx, dg = back(kv_ref[:, h * HD:(h + 1) * HD], dk_ref[h], kg_ref[...])
            dp_ref[:, AW + h * HD:AW + (h + 1) * HD] = dx.astype(BF16)
            dkg = dkg + dg
            dp_ref[:, AW + (NKV + h) * HD:AW + (NKV + h + 1) * HD] = dv_ref[h].astype(BF16)
        _acc_out(dqg_ref, i, dqg)
        _acc_out(dkg_ref, i, dkg)

    in_specs, args = [], []
    if has_q:
        in_specs.append(pl.BlockSpec((tm, AW), lambda i: (i, 0)))
        args.append(p)
    in_specs.append(pl.BlockSpec((tm, 2 * NKV * HD), lambda i: (i, kv_col)))
    args.append(p)
    if has_q:
        in_specs.append(pl.BlockSpec((NQ, tm, HD), lambda i: (0, i, 0)))
        args.append(dq)
    in_specs += [pl.BlockSpec((NKV, tm, HD), lambda i: (0, rb + i, 0))] * 2 + [_vec(HD), _vec(HD)]
    args += [dk, dv, q_gain, k_gain]
    if rope:
        in_specs += [pl.BlockSpec((tm, HD), lambda i: (i, 0))] * 2
        args += [cs, sn]
    return pl.pallas_call(
        body, grid=(n // tm,), in_specs=in_specs,
        out_specs=[pl.BlockSpec((tm, D), lambda i: (i, 0)), _vec(HD), _vec(HD)],
        out_shape=[jax.ShapeDtypeStruct((n, D), BF16), jax.ShapeDtypeStruct((1, HD), F32),
                   jax.ShapeDtypeStruct((1, HD), F32)],
        name=name, compiler_params=_params("arbitrary"))(*args)


def _conv_gate_bwd(dcat, p, conv_w, *, name, tm=256):
    n = p.shape[0]
    ni = n // tm

    def body(dc_ref, dcp_ref, dcn_ref, gb_ref, gbp_ref, gbn_ref, gc_ref, gcp_ref, gcn_ref, xi_ref, xip_ref, xin_ref,
             w_ref, dp_ref, dw_ref):
        i = pl.program_id(0)
        gcext = _ext(gcp_ref, gc_ref, gcn_ref, i, ni)
        xiext = _ext(xip_ref, xi_ref, xin_ref, i, ni)
        hext = gcext * xiext
        dcv = _ext(dcp_ref, dc_ref, dcn_ref, i, ni) * _ext(gbp_ref, gb_ref, gbn_ref, i, ni)
        dp_ref[:, 0:CW] = (dc_ref[...] * _conv3(hext, w_ref, tm)).astype(BF16)
        dh = _sh(dcv, 1, tm) * w_ref[0:1, :] + _sh(dcv, 0, tm) * w_ref[1:2, :] + _sh(dcv, -1, tm) * w_ref[2:3, :]
        dp_ref[:, CW:2 * CW] = (dh * xi_ref[...]).astype(BF16)
        dp_ref[:, 2 * CW:3 * CW] = (dh * gc_ref[...]).astype(BF16)
        dcv_t = dcv[HALO:HALO + tm]
        dw = jnp.concatenate([_colsum(dcv_t * _sh(hext, -1, tm)), _colsum(dcv_t * _sh(hext, 0, tm)),
                              _colsum(dcv_t * _sh(hext, 1, tm))], axis=0)
        _acc_out(dw_ref, i, dw)

    def trio(colblk):
        prev, nxt = _halo_specs(tm, CW, n, colblk=colblk)
        return [pl.BlockSpec((tm, CW), lambda i: (i, colblk)), prev, nxt]

    return pl.pallas_call(
        body, grid=(ni,), in_specs=trio(1) + trio(2) + trio(3) + trio(4) + [pl.BlockSpec((3, CW), lambda i: (0, 0))],
        out_specs=[pl.BlockSpec((tm, 3 * CW), lambda i: (i, 0)), pl.BlockSpec((3, CW), lambda i: (0, 0))],
        out_shape=[jax.ShapeDtypeStruct((n, 3 * CW), BF16), jax.ShapeDtypeStruct((3, CW), F32)],
        name=name, compiler_params=_params("arbitrary"))(dcat, dcat, dcat, p, p, p, p, p, p, p, p, p, conv_w)


def _attn_fwd(q, k, v, *, name, bq=512, sub=256):
    n = q.shape[1]
    t = k.shape[1]
    bq = min(bq, n)
    sub = min(sub, 2 * bq)

    def body(q_ref, k_ref, v_ref, o_ref, lse_ref):
        q2 = q_ref[...].reshape(2 * bq, HD)
        outs, lses = [], []
        for r0 in range(0, 2 * bq, sub):
            s = lax.dot_general(q2[r0:r0 + sub], k_ref[0], _NT, preferred_element_type=F32)
            m = jnp.max(s, axis=-1, keepdims=True)
            pv = jnp.exp2(s - m)
            l = jnp.sum(pv, axis=-1, keepdims=True)
            outs.append(jnp.dot(pv.astype(BF16), v_ref[0], preferred_element_type=F32) / l)
            lses.append(m + jnp.log2(l))
        out = jnp.concatenate(outs, axis=0)
        o_ref[:, 0:HD] = out[0:bq]
        o_ref[:, HD:2 * HD] = out[bq:2 * bq]
        lse_ref[...] = jnp.concatenate(lses, axis=0).reshape(2, bq, 1)

    kspec = pl.BlockSpec((1, t, HD), lambda h, i: (h, 0, 0))
    return pl.pallas_call(
        body, grid=(NKV, n // bq),
        in_specs=[pl.BlockSpec((2, bq, HD), lambda h, i: (h, i, 0)), kspec, kspec],
        out_specs=[pl.BlockSpec((bq, 2 * HD), lambda h, i: (i, h)), pl.BlockSpec((2, bq, 1), lambda h, i: (h, i, 0))],
        out_shape=[jax.ShapeDtypeStruct((n, AW), F32), jax.ShapeDtypeStruct((NQ, n, 1), F32)],
        name=name, compiler_params=_params("parallel", "parallel"))(q, k, v)


def _attn_bwd(q, k, v, dcat, o, lse, *, name, bq=256):
    n = q.shape[1]
    t = k.shape[1]
    bq = min(bq, n)

    def body(q_ref, k_ref, v_ref, dc_ref, o_ref, lse_ref, dq_ref, dk_ref, dv_ref):
        @pl.when(pl.program_id(1) == 0)
        def _():
            dk_ref[...] = jnp.zeros_like(dk_ref)
            dv_ref[...] = jnp.zeros_like(dv_ref)

        q2 = q_ref[...].reshape(2 * bq, HD)
        do_f = jnp.concatenate([dc_ref[:, 0:HD], dc_ref[:, HD:2 * HD]], axis=0)
        o_f = jnp.concatenate([o_ref[:, 0:HD], o_ref[:, HD:2 * HD]], axis=0)
        delta = jnp.sum(do_f * o_f, axis=-1, keepdims=True)
        do2 = do_f.astype(BF16)
        s = lax.dot_general(q2, k_ref[0], _NT, preferred_element_type=F32)
        pv = jnp.exp2(s - lse_ref[...].reshape(2 * bq, 1))
        dp = lax.dot_general(do2, v_ref[0], _NT, preferred_element_type=F32)
        ds = (pv * (dp - delta)).astype(BF16)
        dq_ref[...] = (jnp.dot(ds, k_ref[0], preferred_element_type=F32) * _SCALE).reshape(2, bq, HD)
        dk_ref[0] += lax.dot_general(ds, q2, _TN, preferred_element_type=F32) * _LN2
        dv_ref[0] += lax.dot_general(pv.astype(BF16), do2, _TN, preferred_element_type=F32)

    qspec = pl.BlockSpec((2, bq, HD), lambda h, i: (h, i, 0))
    kspec = pl.BlockSpec((1, t, HD), lambda h, i: (h, 0, 0))
    sspec = pl.BlockSpec((2, bq, 1), lambda h, i: (h, i, 0))
    cspec = pl.BlockSpec((bq, 2 * HD), lambda h, i: (i, h))
    return pl.pallas_call(
        body, grid=(NKV, n // bq), in_specs=[qspec, kspec, kspec, cspec, cspec, sspec], out_specs=[qspec, kspec, kspec],
        out_shape=[jax.ShapeDtypeStruct((NQ, n, HD), F32), jax.ShapeDtypeStruct((NKV, t, HD), F32),
                   jax.ShapeDtypeStruct((NKV, t, HD), F32)],
        name=name, compiler_params=_params("parallel", "arbitrary"))(q, k, v, dcat, o, lse)


def _window_sums(ext, w):
    s, step = ext, 1
    while step < w:
        s = s + _roll_rows(s, step)
        step *= 2
    return s


def _pool_counts(i, tm, n, w, rows, first):
    t = i * tm - HALO + first + lax.broadcasted_iota(jnp.int32, (rows, 1), 0)
    lo = jnp.clip(t - w // 2, 0, n)
    hi = jnp.clip(t + w - w // 2, 0, n)
    return jnp.maximum(hi - lo, 1).astype(F32)


def _norm_mod_ext(xext, gain_ref, sc_ref, sh_ref, i, tm, n):
    rows = xext.shape[0]
    t = i * tm - HALO + lax.broadcasted_iota(jnp.int32, (rows, 1), 0)
    inside = (t >= 0) & (t < n)
    r = lax.rsqrt(jnp.mean(xext * xext, axis=-1, keepdims=True) + EPS)
    xh = xext * r
    a = (xh * gain_ref[...]) * (1.0 + sc_ref[...]) + sh_ref[...]
    return jnp.where(inside, a, 0.0), r, xh


def _pool_fwd(x, y, g, gain, sc, sh, pool_w, *, name, tm=256):
    n, d = x.shape
    ni = n // tm

    def body(x_ref, xp_ref, xn_ref, y_ref, yp_ref, yn_ref, g_ref, gain_ref, sc_ref, sh_ref, w_ref, xo_ref, o_ref):
        i = pl.program_id(0)
        xext = _ext(xp_ref, x_ref, xn_ref, i, ni) + g_ref[...] * _ext(yp_ref, y_ref, yn_ref, i, ni)
        xo_ref[...] = xext[HALO:HALO + tm]
        aext, _, _ = _norm_mod_ext(xext, gain_ref, sc_ref, sh_ref, i, tm, n)
        for gi, w in enumerate(POOL_WINDOWS):
            ag = aext[:, gi * PG:(gi + 1) * PG]
            mean = _sh(_window_sums(ag, w), -(w // 2), tm) / _pool_counts(i, tm, n, w, tm, HALO)
            pooled = mean - ag[HALO:HALO + tm]
            o_ref[:, gi * PG:(gi + 1) * PG] = jnp.dot(pooled.astype(BF16), w_ref[gi], preferred_element_type=F32)

    row = pl.BlockSpec((tm, d), lambda i: (i, 0))
    prev, nxt = _halo_specs(tm, d, n)
    return pl.pallas_call(
        body, grid=(ni,),
        in_specs=[row, prev, nxt, row, prev, nxt, _vec(d), _vec(d), _vec(d), _vec(d),
                  pl.BlockSpec((4, PG, PG), lambda i: (0, 0, 0))],
        out_specs=[row, row], out_shape=[jax.ShapeDtypeStruct((n, d), F32)] * 2,
        name=name, compiler_params=_params("parallel"))(x, x, x, y, y, y, g, gain, sc, sh, pool_w)


def _pool_bwd(dxo, mixed, x, g, scale, gain, sc, sh, pool_w, zprev, gprev, *, name, tm=256):
    n, d = x.shape
    ni = n // tm

    def body(dx_ref, dxp_ref, dxn_ref, mx_ref, x_ref, xp_ref, xn_ref, g_ref, s_ref, gain_ref, sc_ref, sh_ref, w_ref,
             zp_ref, gp_ref, dxi_ref, dw_ref, dg_ref, dsl_ref, dsh_ref, dsc_ref, dgn_ref, dzp_ref, dgp_ref):
        i = pl.program_id(0)

        @pl.when(i == 0)
        def _():
            dw_ref[...] = jnp.zeros_like(dw_ref)

        dxo_t = dx_ref[...]
        mixed_t = mx_ref[...]
        dy_t = dxo_t * g_ref[...]
        _acc_out(dg_ref, i, _colsum(dxo_t * (mixed_t * s_ref[...])))
        _acc_out(dsl_ref, i, _colsum(dy_t * mixed_t))
        dmixed = (_ext(dxp_ref, dx_ref, dxn_ref, i, ni) * g_ref[...]) * s_ref[...]
        xext = _ext(xp_ref, x_ref, xn_ref, i, ni)
        aext, rext, xhext = _norm_mod_ext(xext, gain_ref, sc_ref, sh_ref, i, tm, n)
        rows = tm + 2 * HALO
        da_parts = []
        for gi, w in enumerate(POOL_WINDOWS):
            sl = slice(gi * PG, (gi + 1) * PG)
            ag = aext[:, sl]
            mean = _sh(_window_sums(ag, w), -(w // 2), tm) / _pool_counts(i, tm, n, w, tm, HALO)
            pooled = (mean - ag[HALO:HALO + tm]).astype(BF16)
            dmg = dmixed[:, sl].astype(BF16)
            dw_ref[gi] += lax.dot_general(pooled, dmixed[HALO:HALO + tm, sl].astype(BF16), _TN,
                                          preferred_element_type=F32)
            dpl = lax.dot_general(dmg, w_ref[gi], _NT, preferred_element_type=F32)
            e = dpl / _pool_counts(i, tm, n, w, rows, 0)
            da_parts.append(_sh(_window_sums(e, w), 1 - w // 2, tm) - dpl[HALO:HALO + tm])
        da = jnp.concatenate(da_parts, axis=1)
        r = rext[HALO:HALO + tm]
        xh = xhext[HALO:HALO + tm]
        nrm = xh * gain_ref[...]
        dn = da * (1.0 + sc_ref[...])
        dxh = dn * gain_ref[...]
        dxi = dxo_t + r * (dxh - xh * jnp.mean(dxh * xh, axis=-1, keepdims=True))
        dxi_ref[...] = dxi
        _acc_out(dsh_ref, i, _colsum(da))
        _acc_out(dsc_ref, i, _colsum(da * nrm))
        _acc_out(dgn_ref, i, _colsum(dn * xh))
        dzp_ref[...] = (dxi * gp_ref[...]).astype(BF16)
        _acc_out(dgp_ref, i, _colsum(dxi * zp_ref[...]))

    row = pl.BlockSpec((tm, d), lambda i: (i, 0))
    prev, nxt = _halo_specs(tm, d, n)
    wspec = pl.BlockSpec((4, PG, PG), lambda i: (0, 0, 0))
    vshape = jax.ShapeDtypeStruct((1, d), F32)
    return pl.pallas_call(
        body, grid=(ni,),
        in_specs=[row, prev, nxt, row, row, prev, nxt] + [_vec(d)] * 5 + [wspec, row, _vec(d)],
        out_specs=[row, wspec] + [_vec(d)] * 5 + [row, _vec(d)],
        out_shape=[jax.ShapeDtypeStruct((n, d), F32), jax.ShapeDtypeStruct((4, PG, PG), F32)] + [vshape] * 5
        + [jax.ShapeDtypeStruct((n, d), BF16), vshape],
        name=name, compiler_params=_params("arbitrary"))(dxo, dxo, dxo, mixed, x, x, x, g, scale, gain, sc, sh, pool_w,
                                                         zprev, gprev)


def _adamw(gparts_list, w, m, v, *, name, silu_grad_of=None):
    nl = len(gparts_list)
    nparts, r, c = gparts_list[0].shape
    tr = _pick(r, (256, 128, 64, 32, 16, 8))
    has_c = silu_grad_of is not None

    def body(*refs):
        gp_refs = refs[:nl]
        it = iter(refs[nl:])
        w_ref, m_ref, v_ref = next(it), next(it), next(it)
        c_ref = next(it) if has_c else None
        g_ref, d_ref, mo_ref, vo_ref = next(it), next(it), next(it), next(it)
        layer = pl.program_id(0)

        def update(gp_ref):
            g = gp_ref[0].astype(F32)
            for p in range(1, nparts):
                g = g + gp_ref[p].astype(F32)
            if has_c:
                cv = c_ref[0]
                sg = _sigmoid(cv)
                g = g * (sg * (1.0 + cv * (1.0 - sg)))
            g_ref[0] = g
            mn = ADAM_B1 * m_ref[0] + (1.0 - ADAM_B1) * g
            vn = ADAM_B2 * v_ref[0] + (1.0 - ADAM_B2) * (g * g)
            m_hat = mn / (1.0 - ADAM_B1 ** ADAM_STEP)
            v_hat = vn / (1.0 - ADAM_B2 ** ADAM_STEP)
            d_ref[0] = -ADAM_LR * (m_hat / (jnp.sqrt(v_hat) + ADAM_EPS) + ADAM_WD * w_ref[0])
            mo_ref[0] = mn
            vo_ref[0] = vn

        if nl == 1:
            update(gp_refs[0])
        else:
            for li in range(nl):
                pl.when(layer == li)(functools.partial(update, gp_refs[li]))

    row = pl.BlockSpec((1, tr, c), lambda l, i: (l, i, 0))
    in_specs = [pl.BlockSpec((nparts, tr, c), lambda l, i, li=li: (0, jnp.where(l == li, i, 0), 0)) for li in range(nl)]
    in_specs += [row, row, row]
    args = list(gparts_list) + [w, m, v]
    if has_c:
        in_specs.append(row)
        args.append(silu_grad_of)
    return pl.pallas_call(
        body, grid=(nl, r // tr), in_specs=in_specs, out_specs=[row] * 4,
        out_shape=[jax.ShapeDtypeStruct((nl, r, c), F32)] * 4, name=name,
        compiler_params=_params("arbitrary", "arbitrary"))(*args)


def _adamw_nd(gparts, w, m, v, *, name, silu_grad_of=None):
    shape = w.shape
    c = shape[-1]
    if isinstance(gparts, (list, tuple)):
        nl = len(gparts)
        r = math.prod(shape[1:-1])
    else:
        nl = 1
        r = math.prod(shape[:-1]) if len(shape) > 1 else 1
        gparts = [gparts]
    rs = lambda a: a.reshape(nl, r, c)
    res = _adamw([gp.reshape(gp.shape[0], r, c) for gp in gparts], rs(w), rs(m), rs(v), name=name,
                 silu_grad_of=None if silu_grad_of is None else rs(silu_grad_of))
    return [a.reshape(shape) for a in res]


def _place():
    return lax.axis_index("x"), lax.axis_index("y"), lax.axis_index("c")


def _all_gather(arrs, *, name):
    k_arr = len(arrs)

    def body(*refs):
        ins = refs[:k_arr]
        outs = refs[k_arr:2 * k_arr]
        send_sems, recv_sems, local_sems = refs[2 * k_arr:]
        x, y, c = _place()
        me, sibling = (x, y, c), (x, y, 1 - c)
        chips = [(1 - x, y), (x, 1 - y), (1 - x, 1 - y)]

        def slot(a, px, py, pc):
            return outs[a].at[4 * px + 2 * py + pc]

        def copy(a, s, block, to, src=None):
            return pltpu.make_async_remote_copy(
                src_ref=slot(a, *block) if src is None else src, dst_ref=slot(a, *block),
                send_sem=send_sems.at[a, s], recv_sem=recv_sems.at[a, s], device_id=to, device_id_type=MESH)

        mine = [pltpu.make_async_copy(ins[a], slot(a, *me), local_sems.at[a]) for a in range(k_arr)]
        for cp in mine:
            cp.start()
        first = []
        for a in range(k_arr):
            first.append(copy(a, 0, me, sibling, src=ins[a]))
            first += [copy(a, 1 + j, me, (*chip, c), src=ins[a]) for j, chip in enumerate(chips)]
        for cp in first:
            cp.start()
        passed = []
        for j, chip in enumerate(chips):
            for a in range(k_arr):
                copy(a, 1 + j, (*chip, c), me).wait_recv()
                fw = copy(a, 4 + j, (*chip, c), sibling)
                fw.start()
                passed.append(fw)
        for a in range(k_arr):
            copy(a, 0, sibling, me).wait_recv()
            for j, chip in enumerate(chips):
                copy(a, 4 + j, (*chip, 1 - c), me).wait_recv()
        for cp in first + passed:
            cp.wait_send()
        for cp in mine:
            cp.wait()

    any_spec = pl.BlockSpec(memory_space=pl.ANY)
    return pl.pallas_call(
        body, in_specs=[any_spec] * k_arr, out_specs=[any_spec] * k_arr,
        out_shape=[jax.ShapeDtypeStruct((NDEV,) + a.shape, a.dtype) for a in arrs],
        scratch_shapes=[pltpu.SemaphoreType.DMA((k_arr, 7)), pltpu.SemaphoreType.DMA((k_arr, 7)),
                        pltpu.SemaphoreType.DMA((k_arr,))],
        name=name)(*arrs)


_HBM = pl.BlockSpec(memory_space=pltpu.HBM)
_SEM = pl.BlockSpec(memory_space=pltpu.SEMAPHORE)
_EFFECT = pltpu.SideEffectType.DATAFLOW_SIDE_EFFECTING


def _peers(x, y, c):
    return [(x ^ (rel >> 2), y ^ ((rel >> 1) & 1), c ^ (rel & 1)) for rel in range(1, NDEV)]


def _exchange_copies(srcs, lands, send_sems, recv_sems, scatter):
    x, y, c = _place()
    me = 4 * x + 2 * y + c
    copies = []
    for r, (px, py, pc) in enumerate(_peers(x, y, c)):
        peer = 4 * px + 2 * py + pc
        for a in range(len(srcs)):
            copies.append(pltpu.make_async_remote_copy(
                src_ref=srcs[a].at[peer] if scatter else srcs[a], dst_ref=lands[a].at[me],
                send_sem=send_sems.at[7 * a + r], recv_sem=recv_sems.at[7 * a + r], device_id=(px, py, pc),
                device_id_type=MESH))
    return copies


def _exchange_start(arrs, *, scatter, name):
    k_arr = len(arrs)
    land_shapes = [a.shape if scatter else (NDEV,) + a.shape for a in arrs]
    lands = [pltpu.with_memory_space_constraint(lax.empty(s, a.dtype), pltpu.HBM) for s, a in zip(land_shapes, arrs)]
    srcs = [pltpu.with_memory_space_constraint(a, pltpu.HBM) for a in arrs]

    def body(*refs):
        src_refs, land_refs = refs[:k_arr], refs[k_arr:2 * k_arr]
        send_sems, recv_sems = refs[2 * k_arr], refs[2 * k_arr + 1]
        token = refs[-1]
        for cp in _exchange_copies(src_refs, land_refs, send_sems, recv_sems, scatter):
            cp.start()
        token[...] = jnp.zeros_like(token)

    out_shape = ([pltpu.SemaphoreType.DMA((7 * k_arr,)), pltpu.SemaphoreType.DMA((7 * k_arr,))]
                 + [pltpu.HBM(a.shape, a.dtype) for a in arrs] + [pltpu.HBM(s, a.dtype) for s, a in zip(land_shapes, arrs)]
                 + [jax.ShapeDtypeStruct((8, 128), F32)])
    res = pl.pallas_call(
        body, name=name, out_shape=out_shape, in_specs=[_HBM] * (2 * k_arr),
        out_specs=[_SEM, _SEM] + [_HBM] * (2 * k_arr) + [pl.BlockSpec(memory_space=pltpu.VMEM)],
        input_output_aliases={i: 2 + i for i in range(2 * k_arr)},
        compiler_params=pltpu.CompilerParams(has_side_effects=_EFFECT))(*srcs, *lands)
    return dict(send=res[0], recv=res[1], srcs=list(res[2:2 + k_arr]), lands=list(res[2 + k_arr:2 + 2 * k_arr]),
                token=res[-1], scatter=scatter)


def _exchange_wait(handle, after, *, name):
    k_arr = len(handle["srcs"])
    scatter = handle["scatter"]

    def body(*refs):
        src_refs, land_refs = refs[:k_arr], refs[k_arr:2 * k_arr]
        send_sems, recv_sems = refs[2 * k_arr], refs[2 * k_arr + 1]
        x, y, c = _place()
        me = 4 * x + 2 * y + c
        for r, (px, py, pc) in enumerate(_peers(x, y, c)):
            peer = 4 * px + 2 * py + pc
            for a in range(k_arr):
                cp = pltpu.make_async_remote_copy(
                    src_ref=src_refs[a].at[peer] if scatter else src_refs[a], dst_ref=land_refs[a].at[peer],
                    send_sem=send_sems.at[7 * a + r], recv_sem=recv_sems.at[7 * a + r], device_id=(x, y, c),
                    device_id_type=MESH)
                cp.wait_send()
                cp.wait_recv()

    arrs = handle["srcs"] + handle["lands"]
    res = pl.pallas_call(
        body, name=name, out_shape=[pltpu.HBM(a.shape, a.dtype) for a in arrs],
        in_specs=[_HBM] * (2 * k_arr) + [_SEM, _SEM, pl.BlockSpec(memory_space=pl.ANY)],
        out_specs=[_HBM] * (2 * k_arr), input_output_aliases={i: i for i in range(2 * k_arr)},
        compiler_params=pltpu.CompilerParams(has_side_effects=_EFFECT))(*arrs, handle["send"], handle["recv"], after)
    me = 4 * lax.axis_index("x") + 2 * lax.axis_index("y") + lax.axis_index("c")
    out = []
    for src, land in zip(res[:k_arr], res[k_arr:]):
        own = lax.dynamic_index_in_dim(src, me, 0, keepdims=False) if scatter else src
        out.append(lax.dynamic_update_index_in_dim(land, own, me, 0))
    return out


def _ffn_bwd(dxo, dz, xr, f, u_gc, hmid, gain, sc, w_up, cw, w_down, tag, gate_y=None, gate_g=None):
    d_wdown = _mm_tn((hmid, dz), name=f"ffn_down_dw_{tag}")
    dug, duv, dcw, dcb = _ffn_down_glu_bwd(dz, w_down, u_gc[0], u_gc[1], cw, name=f"ffn_down_glu_bwd_{tag}")
    d_wup = _mm_tn((dug, f), blocks=2, block=0, name=f"ffn_up_dwg_{tag}")
    d_wup = _mm_tn((duv, f), blocks=2, block=1, into=d_wup, name=f"ffn_up_dwv_{tag}")
    gated = gate_y is not None
    res = _mm_w_ep([dug, duv], w_up, _ep_norm_bwd(gated), [xr, dxo] + ([gate_y] if gated else []),
                   [gain, sc] + ([gate_g] if gated else []), [F32] + ([BF16] if gated else []),
                   [D] * (4 if gated else 3), name=f"ffn_up_dx_norm_bwd_{tag}")
    n_out = 2 if gated else 1
    return res[:n_out], res[n_out:], (d_wup, d_wdown, dcw, dcb)


def _split6(mod):
    return [mod[j * D:(j + 1) * D][None, :] for j in range(6)]


def _row(v):
    return v.reshape(1, -1)


def kernel(x, c, ctx, c_ctx, ada_w, ada_b, mix_norm, ffn_norm, even_w_in, even_q_gain, even_k_gain, even_conv_w, even_w_out, odd_pool_w, odd_pool_scale, ffn_w_up, ffn_conv_w, ffn_conv_b, ffn_w_down, loss_target, m_c_ctx, m_ada_w, m_ada_b, m_mix_norm, m_ffn_norm, m_even_w_in, m_even_q_gain, m_even_k_gain, m_even_conv_w, m_even_w_out, m_odd_pool_w, m_odd_pool_scale, m_ffn_w_up, m_ffn_conv_w, m_ffn_conv_b, m_ffn_w_down, v_c_ctx, v_ada_w, v_ada_b, v_mix_norm, v_ffn_norm, v_even_w_in, v_even_q_gain, v_even_k_gain, v_even_conv_w, v_even_w_out, v_odd_pool_w, v_odd_pool_scale, v_ffn_w_up, v_ffn_conv_w, v_ffn_conv_b, v_ffn_w_down):
    n = x.shape[1]
    lc = ctx.shape[1]
    me = 4 * lax.axis_index("x") + 2 * lax.axis_index("y") + lax.axis_index("c")
    xs, ctxs, tgt = x[0], ctx[0], loss_target[0]
    acols = ada_w.shape[2]

    small = jnp.concatenate([even_conv_w.reshape(-1), ffn_conv_w.reshape(-1), odd_pool_scale.reshape(-1)])
    nsmall = small.shape[0]
    small = jnp.pad(small, (0, (-nsmall) % 1024)).reshape(-1, 128)
    c_rows = jnp.pad(c, ((0, 7), (0, 0)))
    tr = lambda a: jnp.swapaxes(a, -1, -2)
    g_c, g_win, g_small = _all_gather([c_rows, tr(even_w_in[0]).astype(BF16), small], name="gather_first")
    w_in_t = g_win.reshape(-1, D)
    g_small = g_small.reshape(NDEV, -1)
    ecw = even_conv_w.shape[2]
    fcw = ffn_conv_w.shape[2]
    conv_w = g_small[:, :3 * ecw].reshape(NDEV, 3, ecw).transpose(1, 0, 2).reshape(3, CW)
    o1 = 3 * ecw
    fconv_w = g_small[:, o1:o1 + 6 * fcw].reshape(NDEV, 2, 3, fcw).transpose(1, 2, 0, 3).reshape(2, 3, DFF)
    o2 = o1 + 6 * fcw
    pool_scale = g_small[:, o2:o2 + D // NDEV].reshape(1, D)

    mraw = jnp.concatenate([g_c[:, 0, :], c_ctx[None, :], jnp.zeros((7, D), F32)], axis=0)
    my_bias = lax.dynamic_slice_in_dim(ada_b, me * acols, acols, axis=1)
    modp = jnp.stack([_mm(mraw, ada_w[l], silu_a=True, bias=my_bias[l:l + 1], name=f"ada_proj_{l}", tm=16, tn=256)
                      for l in range(2)])
    (g_mod,) = _all_gather([modp], name="gather_mod")
    mod_rows = g_mod.transpose(1, 2, 0, 3).reshape(2, 16, 6 * D)
    late_shards = [even_w_out[0].astype(BF16), odd_pool_w[0].astype(BF16), tr(ffn_w_up[0]).astype(BF16),
                   tr(ffn_w_up[1]).astype(BF16), ffn_w_down[0].astype(BF16), ffn_w_down[1].astype(BF16)]
    late_shards, mod_rows = lax.optimization_barrier((late_shards, mod_rows))
    h_weights = _exchange_start(late_shards, scatter=False, name="weights_start")
    mod_rows = mod_rows + h_weights["token"][0, 0]
    mod = lax.dynamic_index_in_dim(mod_rows, me, axis=1, keepdims=False)
    sh1, sc1, g1, sh2, sc2, g2 = _split6(mod[0])
    sh1b, sc1b, g1b, sh2b, sc2b, g2b = _split6(mod[1])
    csh1, csc1 = _split6(mod_rows[0, 8])[:2]
    mixn = [_row(mix_norm[l]) for l in range(2)]
    ffnn = [_row(ffn_norm[l]) for l in range(2)]
    qg, kg = _row(even_q_gain[0]), _row(even_k_gain[0])
    fcb = [_row(ffn_conv_b[l]) for l in range(2)]

    cs_t, sn_t = _rope_tables(n)
    a_lat = _norm_mod(xs, mixn[0], sc1, sh1, name="mix0_norm")
    a_ctx = _norm_mod(ctxs, mixn[0], csc1, csh1, name="mix0_norm_ctx")
    p_ctx = _mm(a_ctx, w_in_t[AW:AW + 4 * HD], tb=True, name="in_proj_ctx", tm=256, tn=512, tk=1024)
    kv_ctx = _qkv_prep(p_ctx, qg, kg, None, None, has_q=False, kv_col=0, kv_rows=lc + n, name="qkv_prep_ctx")
    p_lat, q_r, k_all, v_all, conv = _in_proj_qkv(a_lat, w_in_t, qg, kg, cs_t, sn_t, conv_w, kv_ctx, kv_row_off=lc,
                                                  name="in_proj_qkv")
    o_attn, lse = _attn_fwd(q_r, k_all, v_all, name="attn_fwd")
    g_wout, g_pool, g_up0, g_up1, g_down0, g_down1 = _exchange_wait(h_weights, o_attn, name="weights_wait")
    w_out = g_wout.reshape(D, D)
    pool_w = g_pool.transpose(1, 0, 2, 3).reshape(4, PG, PG)
    w_up_t = [g_up0.reshape(2 * DFF, D), g_up1.reshape(2 * DFF, D)]
    w_up = [w.T for w in w_up_t]
    w_down = [g_down0.reshape(DFF, D), g_down1.reshape(DFF, D)]
    y0, x1, f0 = _mm_w_ep([o_attn, conv], w_out, _ep_resid_norm, [xs], [g1, ffnn[0], sc2, sh2], [F32, F32, BF16], [],
                          tm=512, name="out_proj_norm")[:3]
    *u0, h0 = _ffn_up_glu(f0, w_up[0], fconv_w[0], fcb[0], name="ffn_up_glu_l0")
    z0 = _mm_w(h0, w_down[0], name="ffn_down_l0")

    x2, mixed = _pool_fwd(x1, z0, g2, mixn[1], sc1b, sh1b, pool_w, name="pool_fwd")
    x3, f1 = _norm_mod(x2, ffnn[1], sc2b, sh2b, y=mixed, g=g1b, ymul=pool_scale, name="ffn_norm_l1")
    *u1, h1 = _ffn_up_glu(f1, w_up[1], fconv_w[1], fcb[1], name="ffn_up_glu_l1")
    dx4, dz1, loss_part, dg2b = _mm_w_ep(h1, w_down[1], _ep_loss(D), [x3, tgt], [g2b], [F32, BF16], [128, D],
                                         tm=512, name="ffn_down_loss")

    (dx3,), (dsh2b, dsc2b, dffn1), (dup1, ddown1, dfcw1, dfcb1) = _ffn_bwd(
        dx4, dz1, x3, f1, u1, h1, ffnn[1], sc2b, w_up_t[1], fconv_w[1], w_down[1], "l1")
    dx2, dpool_w, dg1b, dpscale, dsh1b, dsc1b, dmix1, dz0, dg2 = _pool_bwd(
        dx3, mixed, x2, g1b, pool_scale, mixn[1], sc1b, sh1b, pool_w, z0, g2, name="pool_bwd")

    s_pool = dpool_w.astype(BF16).reshape(4, NDEV, PG // NDEV, PG).transpose(1, 0, 2, 3)
    h_g1 = _exchange_start([s_pool, dup1.reshape(NDEV, -1, D), ddown1.reshape(NDEV, DFF // NDEV, D)], scatter=True,
                           name="grads1_start")

    (dx1, dy0), (dsh2, dsc2, dffn0, dg1), (dup0, ddown0, dfcw0, dfcb0) = _ffn_bwd(
        dx2, dz0, x1, f0, u0, h0, ffnn[0], sc2, w_up_t[0], fconv_w[0] + h_g1["token"][0, 0], w_down[0], "l0",
        gate_y=y0, gate_g=g1)
    h_g0 = _exchange_start([dup0.reshape(NDEV, -1, D), ddown0.reshape(NDEV, DFF // NDEV, D)], scatter=True,
                           name="grads0_start")
    dcat = _mm_w(dy0, w_out, tb=True, name="out_proj_dx", tm=512)
    d_wout = _mm_tn((o_attn, dy0), blocks=2, block=0, name="out_proj_dw_attn")
    d_wout = _mm_tn((conv, dy0), blocks=2, block=1, into=d_wout, name="out_proj_dw_conv")
    dp_conv, dconv_w = _conv_gate_bwd(dcat, p_lat, conv_w + h_g0["token"][0, 0], name="conv_gate_bwd")
    dq_r, dk_all, dv_all = _attn_bwd(q_r, k_all, v_all, dcat, o_attn, lse, name="attn_bwd")
    dp_qkv, dqg_l, dkg_l = _qkv_bwd(p_lat, dq_r, dk_all, dv_all, qg, kg, cs_t, sn_t, has_q=True, kv_col=1,
                                    kv_row_off=lc, name="qkv_bwd")
    dp_ctx, _zero_qg, dkg_c = _qkv_bwd(p_ctx, None, dk_all, dv_all, qg, kg, None, None, has_q=False, kv_col=0,
                                       kv_row_off=0, name="qkv_bwd_ctx")
    da_ctx = _mm(dp_ctx, w_in_t[:D], name="in_proj_dx_ctx", tm=256, tn=512, tk=1024)
    d_win_qkv = _mm_tn([(dp_qkv, a_lat), (dp_ctx, a_ctx)], name="in_proj_dw_qkv")
    d_win_conv = _mm_tn((dp_conv, a_lat), name="in_proj_dw_conv")
    d_win_t = jnp.concatenate([d_win_qkv, d_win_conv], axis=0)
    grad_x, dsh1, dsc1, dmix0 = _mm_w_ep([dp_qkv, dp_conv], w_in_t, _ep_norm_bwd(False), [xs, dx1], [mixn[0], sc1],
                                         [F32], [D] * 3, tm=512, name="in_proj_dx_norm_bwd")
    _dctx, dcsh1, dcsc1, dmix0c = _norm_mod_bwd(da_ctx, ctxs, mixn[0], csc1, name="mix0_norm_bwd_ctx")

    z1k = jnp.zeros((1, D), F32)
    pack = jnp.concatenate(
        [v.reshape(-1) for v in (dsh1, dsc1, dg1, dsh2, dsc2, dg2, dsh1b, dsc1b, dg1b, dsh2b, dsc2b, dg2b,
                                 dcsh1, dcsc1, z1k, z1k, z1k, z1k,
                                 dmix0, dmix1, dmix0c, z1k, dffn0, dffn1, dqg_l, dkg_l + dkg_c,
                                 dfcb0, dfcb1, dconv_w, dfcw0, dfcw1, dpscale, loss_part[:, 0:1])])
    npack = pack.shape[0]
    pack = jnp.pad(pack, (0, (-npack) % 1024)).reshape(-1, 128)
    (g_pack,) = _all_gather([pack], name="gather_small_grads")
    gp = g_pack.reshape(NDEV, -1)
    off = [0]

    def take(size):
        seg = gp[:, off[0]:off[0] + size]
        off[0] += size
        return seg

    dmod_all = take(12 * D).reshape(NDEV, 2, 6 * D)
    dmodc_all = take(6 * D).reshape(NDEV, 1, 6 * D)
    dmix_all = take(4 * D).reshape(NDEV, 2, 2, D)
    dffn_all = take(2 * D).reshape(NDEV, 2, D)
    dqg_all = take(HD).reshape(NDEV, 1, HD)
    dkg_all = take(HD).reshape(NDEV, 1, HD)
    dfcb_all = take(2 * DFF).reshape(NDEV, 2, DFF)
    dconvw_all = take(3 * CW).reshape(NDEV, 3, CW)
    dfcw_all = take(6 * DFF).reshape(NDEV, 2, 3, DFF)
    dpscale_all = take(D).reshape(NDEV, D)
    loss_all = take(1)
    loss = loss_all[0, 0]
    for dev in range(1, NDEV):
        loss = loss + loss_all[dev, 0]

    dmodc_sum = dmodc_all[0]
    for dev in range(1, NDEV):
        dmodc_sum = dmodc_sum + dmodc_all[dev]
    my_cols = lambda a: lax.dynamic_slice_in_dim(a, me * acols, acols, axis=a.ndim - 1)
    rows0 = jnp.concatenate([my_cols(dmod_all[:, 0]), my_cols(dmodc_sum), jnp.zeros((7, acols), F32)], axis=0)
    rows1 = jnp.concatenate([my_cols(dmod_all[:, 1]), jnp.zeros((8, acols), F32)], axis=0)
    d_ada = jnp.stack([_mm(mraw, rows, ta=True, silu_a=True, name=f"ada_dw_{l}", tm=512, tn=256, tk=16)
                       for l, rows in enumerate((rows0, rows1))])
    dscc_part = _mm(rows0, ada_w[0], tb=True, name="ada_dcctx", tm=16, tn=512, tk=256)
    (g_dscc,) = _all_gather([dscc_part[8:16]], name="gather_dcctx")

    attn_shards = [d_win_t.reshape(NDEV, -1, D), d_wout.reshape(NDEV, D // NDEV, D)]
    attn_shards, g_dscc = lax.optimization_barrier((attn_shards, g_dscc))
    h_ga = _exchange_start(attn_shards, scatter=True, name="grads_attn_start")
    dmod_all = dmod_all + h_ga["token"][0, 0]

    outs = {}

    def put(nm, res):
        outs["grad_" + nm], outs["delta_" + nm], outs["new_m_" + nm], outs["new_v_" + nm] = res

    dmodc_pad = jnp.concatenate([dmodc_all, jnp.zeros_like(dmodc_all)], axis=1)
    put("ada_b", _adamw_nd(jnp.concatenate([dmod_all, dmodc_pad], axis=0), ada_b, m_ada_b, v_ada_b, name="adam_ada_b"))
    put("mix_norm", _adamw_nd(jnp.concatenate([dmix_all[:, 0], dmix_all[:, 1]], axis=0), mix_norm, m_mix_norm,
                              v_mix_norm, name="adam_mix_norm"))
    put("ffn_norm", _adamw_nd(dffn_all, ffn_norm, m_ffn_norm, v_ffn_norm, name="adam_ffn_norm"))
    put("even_q_gain", _adamw_nd(dqg_all, even_q_gain, m_even_q_gain, v_even_q_gain, name="adam_q_gain"))
    put("even_k_gain", _adamw_nd(dkg_all, even_k_gain, m_even_k_gain, v_even_k_gain, name="adam_k_gain"))
    put("ffn_conv_b", _adamw_nd(dfcb_all, ffn_conv_b, m_ffn_conv_b, v_ffn_conv_b, name="adam_ffn_conv_b"))
    my_convw = lax.dynamic_slice_in_dim(dconvw_all, me * ecw, ecw, axis=2)[:, None]
    put("even_conv_w", _adamw_nd(my_convw, even_conv_w, m_even_conv_w, v_even_conv_w, name="adam_even_conv_w"))
    my_fcw = lax.dynamic_slice_in_dim(dfcw_all, me * fcw, fcw, axis=3)
    put("ffn_conv_w", _adamw_nd(my_fcw, ffn_conv_w, m_ffn_conv_w, v_ffn_conv_w, name="adam_ffn_conv_w"))
    my_ps = lax.dynamic_slice_in_dim(dpscale_all, me * (D // NDEV), D // NDEV, axis=1)[:, None]
    put("odd_pool_scale", _adamw_nd(my_ps, odd_pool_scale, m_odd_pool_scale, v_odd_pool_scale, name="adam_pool_scale"))

    put("ada_w", _adamw_nd(d_ada[None], ada_w, m_ada_w, v_ada_w, name="adam_ada_w"))
    put("c_ctx", _adamw_nd(g_dscc[:, 0:1, :].reshape(NDEV, D), c_ctx, m_c_ctx, v_c_ctx, name="adam_c_ctx",
                           silu_grad_of=c_ctx))

    r_pool, r_up1, r_down1 = _exchange_wait(h_g1, outs["grad_ada_b"], name="grads1_wait")
    r_up0, r_down0 = _exchange_wait(h_g0, outs["grad_mix_norm"], name="grads0_wait")
    r_win, r_wout = _exchange_wait(h_ga, outs["grad_c_ctx"], name="grads_attn_wait")
    put("even_w_in", [tr(a) for a in _adamw_nd(r_win[:, None], tr(even_w_in), tr(m_even_w_in), tr(v_even_w_in),
                                               name="adam_w_in")])
    put("even_w_out", _adamw_nd(r_wout[:, None], even_w_out, m_even_w_out, v_even_w_out, name="adam_w_out"))
    put("odd_pool_w", _adamw_nd(r_pool[:, None], odd_pool_w, m_odd_pool_w, v_odd_pool_w, name="adam_pool_w"))
    put("ffn_w_up", [tr(a) for a in _adamw_nd([r_up0, r_up1], tr(ffn_w_up), tr(m_ffn_w_up), tr(v_ffn_w_up),
                                              name="adam_w_up")])
    put("ffn_w_down", _adamw_nd([r_down0, r_down1], ffn_w_down, m_ffn_w_down, v_ffn_w_down, name="adam_w_down"))

    names = ["c_ctx", "ada_w", "ada_b", "mix_norm", "ffn_norm", "even_w_in", "even_q_gain", "even_k_gain",
             "even_conv_w", "even_w_out", "odd_pool_w", "odd_pool_scale", "ffn_w_up", "ffn_conv_w", "ffn_conv_b",
             "ffn_w_down"]
    result = [loss, grad_x[None]]
    for kind in ("grad_", "delta_", "new_m_", "new_v_"):
        result += [outs[kind + nm] for nm in names]
    return tuple(result)
```

```python
import functools
import math

import jax
import jax.numpy as jnp
from jax import lax
from jax.experimental import pallas as pl
from jax.experimental.pallas import tpu as pltpu

F32 = jnp.float32
BF16 = jnp.bfloat16

D = 1024
HD = 128
NQ = 4
NKV = 2
AW = NQ * HD
CW = D - AW
DFF = 2816
GRID_W = 64
ROPE_THETA = 10000.0
POOL_WINDOWS = (2, 4, 8, 16)
PG = D // 4
EPS = 1e-6
NDEV = 8
HALO = 8
MESH = pl.DeviceIdType.MESH

ADAM_LR = 0.001
ADAM_B1 = 0.9
ADAM_B2 = 0.999
ADAM_EPS = 1e-08
ADAM_WD = 0.01
ADAM_STEP = 10


def _pick(dim, prefs):
    for p in prefs:
        if dim % p == 0:
            return p
    return dim


def _params(*sem):
    return pltpu.CompilerParams(dimension_semantics=sem)


_NT = (((1,), (1,)), ((), ()))
_TN = (((0,), (0,)), ((), ()))
_SCALE = HD ** -0.5
_QSCALE = _SCALE * math.log2(math.e)
_LN2 = math.log(2.0)


def _mm(a_list, b, *, name, ta=False, tb=False, out_dtype=F32, silu_a=False, bias=None, tm=None, tn=None, tk=None):
    if not isinstance(a_list, (list, tuple)):
        a_list = [a_list]
    na = len(a_list)
    assert not (ta and na > 1)
    if ta:
        kdim, m = a_list[0].shape
        ks = [kdim]
    else:
        m = a_list[0].shape[0]
        ks = [a.shape[1] for a in a_list]
        kdim = sum(ks)
    n = b.shape[0] if tb else b.shape[1]
    assert (b.shape[1] if tb else b.shape[0]) == kdim
    kunit = math.gcd(*ks) if na > 1 else kdim
    tm = min(tm, m) if tm else _pick(m, (512, 256, 128, 64, 32, 16, 8))
    tn = min(tn, n) if tn else _pick(n, (512, 256, 128))
    tk = min(tk, kunit) if tk else _pick(kunit, (1024, 768, 512, 256, 128))
    assert m % tm == 0 and n % tn == 0 and all(k % tk == 0 for k in ks)
    nks = [k // tk for k in ks]
    starts = [sum(nks[:i]) for i in range(na)]
    nk = sum(nks)
    has_bias = bias is not None

    def body(*refs):
        a_refs = refs[:na]
        b_ref = refs[na]
        bias_ref = refs[na + 1] if has_bias else None
        o_ref = refs[na + 1 + has_bias]
        acc = refs[-1]
        k = pl.program_id(2)

        @pl.when(k == 0)
        def _():
            acc[...] = jnp.zeros_like(acc)

        bv = b_ref[...].astype(BF16)
        dn = (((0 if ta else 1,), (1 if tb else 0,)), ((), ()))
        for idx in range(na):
            def step(idx=idx):
                av = a_refs[idx][...]
                if silu_a:
                    av = av * jax.nn.sigmoid(av)
                acc[...] += lax.dot_general(av.astype(BF16), bv, dn, preferred_element_type=F32)
            if na == 1:
                step()
            else:
                pl.when((k >= starts[idx]) & (k < starts[idx] + nks[idx]))(step)

        @pl.when(k == nk - 1)
        def _():
            r = acc[...]
            if has_bias:
                r = r + bias_ref[...]
            o_ref[...] = r.astype(o_ref.dtype)

    in_specs = []
    for idx in range(na):
        if ta:
            in_specs.append(pl.BlockSpec((tk, tm), lambda i, j, k: (k, i)))
        else:
            lo, cnt = starts[idx], nks[idx]
            in_specs.append(pl.BlockSpec((tm, tk), lambda i, j, k, lo=lo, cnt=cnt: (i, jnp.clip(k - lo, 0, cnt - 1))))
    if tb:
        in_specs.append(pl.BlockSpec((tn, tk), lambda i, j, k: (j, k)))
    else:
        in_specs.append(pl.BlockSpec((tk, tn), lambda i, j, k: (k, j)))
    args = list(a_list) + [b]
    if has_bias:
        in_specs.append(pl.BlockSpec((1, tn), lambda i, j, k: (0, j)))
        args.append(bias)
    return pl.pallas_call(
        body, grid=(m // tm, n // tn, nk), in_specs=in_specs,
        out_specs=pl.BlockSpec((tm, tn), lambda i, j, k: (i, j)),
        out_shape=jax.ShapeDtypeStruct((m, n), out_dtype),
        scratch_shapes=[pltpu.VMEM((tm, tn), F32)], name=name,
        compiler_params=_params("parallel", "parallel", "arbitrary"))(*args)


def _mm_w(a_list, w, *, name, tb=False, tm=256, out_dtype=F32):
    if not isinstance(a_list, (list, tuple)):
        a_list = [a_list]
    na = len(a_list)
    m = a_list[0].shape[0]
    ks = [a.shape[1] for a in a_list]
    offs = [sum(ks[:i]) for i in range(na)]
    n = w.shape[0] if tb else w.shape[1]
    assert (w.shape[1] if tb else w.shape[0]) == sum(ks)
    tm = min(tm, m)
    assert m % tm == 0

    def body(*refs):
        a_refs, w_ref, o_ref = refs[:na], refs[na], refs[na + 1]
        acc = None
        for idx in range(na):
            av = a_refs[idx][...].astype(BF16)
            if tb:
                part = lax.dot_general(av, w_ref[:, offs[idx]:offs[idx] + ks[idx]], _NT, preferred_element_type=F32)
            else:
                part = jnp.dot(av, w_ref[offs[idx]:offs[idx] + ks[idx], :], preferred_element_type=F32)
            acc = part if acc is None else acc + part
        o_ref[...] = acc.astype(o_ref.dtype)

    in_specs = [pl.BlockSpec((tm, k), lambda i: (i, 0)) for k in ks] + [pl.BlockSpec(w.shape, lambda i: (0, 0))]
    return pl.pallas_call(
        body, grid=(m // tm,), in_specs=in_specs, out_specs=pl.BlockSpec((tm, n), lambda i: (i, 0)),
        out_shape=jax.ShapeDtypeStruct((m, n), out_dtype), name=name, compiler_params=_params("parallel"))(*a_list, w)


def _mm_w_ep(a_list, w, epilogue, row_in, vec_in, out_dtypes, sum_widths, *, name, tb=False, tm=256, sub=256):
    if not isinstance(a_list, (list, tuple)):
        a_list = [a_list]
    na, nr, nv, no, ns = len(a_list), len(row_in), len(vec_in), len(out_dtypes), len(sum_widths)
    m = a_list[0].shape[0]
    ks = [a.shape[1] for a in a_list]
    offs = [sum(ks[:i]) for i in range(na)]
    n = w.shape[0] if tb else w.shape[1]
    assert (w.shape[1] if tb else w.shape[0]) == sum(ks)
    tm = min(tm, m)
    sub = min(sub, tm)
    assert m % tm == 0 and tm % sub == 0

    def body(*refs):
        a_refs, w_ref = refs[:na], refs[na]
        row_refs = refs[na + 1:na + 1 + nr]
        vec_refs = refs[na + 1 + nr:na + 1 + nr + nv]
        out_refs = refs[na + 1 + nr + nv:na + 1 + nr + nv + no]
        sum_refs = refs[na + 1 + nr + nv + no:]

        @pl.when(pl.program_id(0) == 0)
        def _():
            for s_ref in sum_refs:
                s_ref[...] = jnp.zeros_like(s_ref)

        vecs = [v[...] for v in vec_refs]
        for r0 in range(0, tm, sub):
            acc = None
            for idx in range(na):
                av = a_refs[idx][r0:r0 + sub, :].astype(BF16)
                if tb:
                    part = lax.dot_general(av, w_ref[:, offs[idx]:offs[idx] + ks[idx]], _NT, preferred_element_type=F32)
                else:
                    part = jnp.dot(av, w_ref[offs[idx]:offs[idx] + ks[idx], :], preferred_element_type=F32)
                acc = part if acc is None else acc + part
            outs, sums = epilogue(acc, [r[r0:r0 + sub, :] for r in row_refs], vecs)
            for o_ref, o in zip(out_refs, outs):
                o_ref[r0:r0 + sub, :] = o.astype(o_ref.dtype)
            for s_ref, s in zip(sum_refs, sums):
                s_ref[...] += s

    row = pl.BlockSpec((tm, n), lambda i: (i, 0))
    in_specs = ([pl.BlockSpec((tm, k), lambda i: (i, 0)) for k in ks] + [pl.BlockSpec(w.shape, lambda i: (0, 0))]
                + [row] * nr + [_vec(n)] * nv)
    return pl.pallas_call(
        body, grid=(m // tm,), in_specs=in_specs, out_specs=[row] * no + [_vec(sw) for sw in sum_widths],
        out_shape=[jax.ShapeDtypeStruct((m, n), dt) for dt in out_dtypes]
        + [jax.ShapeDtypeStruct((1, sw), F32) for sw in sum_widths],
        name=name, compiler_params=_params("arbitrary" if ns else "parallel"))(*a_list, w, *row_in, *vec_in)


def _ep_norm_bwd(has_gate):
    def ep(dav, rows, vecs):
        xv = rows[0]
        gain, scv = vecs[0], vecs[1]
        r = lax.rsqrt(jnp.mean(xv * xv, axis=-1, keepdims=True) + EPS)
        xh = xv * r
        nrm = xh * gain
        dn = dav * (1.0 + scv)
        dxh = dn * gain
        dx = r * (dxh - xh * jnp.mean(dxh * xh, axis=-1, keepdims=True)) + rows[1]
        outs, sums = [dx], [_colsum(dav), _colsum(dav * nrm), _colsum(dn * xh)]
        if has_gate:
            outs.append(dx * vecs[2])
            sums.append(_colsum(dx * rows[2]))
        return outs, sums
    return ep


def _ep_loss(d):
    def ep(zv, rows, vecs):
        xv, tv = rows
        gv = vecs[0]
        diff = (xv + gv * zv) - tv
        dx = diff * (1.0 / d)
        part = 0.5 * jnp.sum(jnp.mean(diff * diff, axis=-1, keepdims=True), axis=0, keepdims=True)
        return [dx, dx * gv], [jnp.broadcast_to(part, (1, 128)), _colsum(dx * zv)]
    return ep


def _ep_resid_norm(yv, rows, vecs):
    g, gain, scv, shv = vecs
    xv = rows[0] + g * yv
    r = lax.rsqrt(jnp.mean(xv * xv, axis=-1, keepdims=True) + EPS)
    return [yv, xv, ((xv * r) * gain) * (1.0 + scv) + shv], []


def _mm_tn(pairs, *, name, tk=1024, out_dtype=BF16, blocks=1, block=0, into=None):
    if not isinstance(pairs, list):
        pairs = [pairs]
    m, n = pairs[0][0].shape[1], pairs[0][1].shape[1]
    tks = [min(tk, a.shape[0]) for a, _ in pairs]
    nks = [a.shape[0] // t for (a, _), t in zip(pairs, tks)]
    assert all(a.shape[0] == b.shape[0] and a.shape[0] % t == 0 for (a, b), t in zip(pairs, tks))
    starts = [sum(nks[:i]) for i in range(len(pairs))]
    nk = sum(nks)

    def body(*refs):
        o_ref, acc = refs[-2], refs[-1]
        k = pl.program_id(0)

        @pl.when(k == 0)
        def _():
            acc[...] = jnp.zeros_like(acc)

        for idx in range(len(pairs)):
            a_ref, b_ref = refs[2 * idx], refs[2 * idx + 1]

            def step(a_ref=a_ref, b_ref=b_ref):
                acc[...] += lax.dot_general(a_ref[...].astype(BF16), b_ref[...].astype(BF16), _TN,
                                            preferred_element_type=F32)

            if len(pairs) == 1:
                step()
            else:
                pl.when((k >= starts[idx]) & (k < starts[idx] + nks[idx]))(step)

        @pl.when(k == nk - 1)
        def _():
            o_ref[...] = acc[...].astype(o_ref.dtype)

    in_specs, args = [], []
    for (a, b), t, lo, cnt in zip(pairs, tks, starts, nks):
        idx_map = lambda k, lo=lo, cnt=cnt: (jnp.clip(k - lo, 0, cnt - 1), 0)
        in_specs += [pl.BlockSpec((t, m), idx_map), pl.BlockSpec((t, n), idx_map)]
        args += [a, b]
    aliases = {}
    if into is not None:
        aliases = {len(args): 0}
        in_specs.append(pl.BlockSpec(memory_space=pl.ANY))
        args.append(into)
    return pl.pallas_call(
        body, grid=(nk,), in_specs=in_specs, out_specs=pl.BlockSpec((m, n), lambda k: (block, 0)),
        out_shape=jax.ShapeDtypeStruct((m * blocks, n), out_dtype), scratch_shapes=[pltpu.VMEM((m, n), F32)],
        input_output_aliases=aliases, name=name, compiler_params=_params("arbitrary"))(*args)


def _vec(d, col=None):
    if col is None:
        return pl.BlockSpec((1, d), lambda i, *_: (0, 0))
    return pl.BlockSpec((1, d), col)


def _halo_specs(tm, width, nrows, colblk=0, row_off=0):
    r = tm // HALO
    off = row_off // HALO
    last = nrows // HALO - 1
    prev = pl.BlockSpec((HALO, width), lambda i, *_: (off + jnp.maximum(i * r - 1, 0), colblk))
    nxt = pl.BlockSpec((HALO, width), lambda i, *_: (off + jnp.minimum((i + 1) * r, last), colblk))
    return prev, nxt


def _ext(prev_ref, main_ref, next_ref, i, ni):
    p = jnp.where(i > 0, prev_ref[...], 0.0)
    n = jnp.where(i < ni - 1, next_ref[...], 0.0)
    return jnp.concatenate([p, main_ref[...], n], axis=0)


def _sh(ext, k, tm):
    if k == 0:
        return ext[HALO:HALO + tm]
    rows = ext.shape[0]
    return pltpu.roll(ext, (-k) % rows, axis=0)[HALO:HALO + tm]


def _roll_rows(v, k):
    rows = v.shape[0]
    return pltpu.roll(v, (-k) % rows, axis=0) if k % rows else v


def _conv3(ext, w_ref, tm):
    return _sh(ext, -1, tm) * w_ref[0:1, :] + _sh(ext, 0, tm) * w_ref[1:2, :] + _sh(ext, 1, tm) * w_ref[2:3, :]


def _colsum(v):
    return jnp.sum(v, axis=0, keepdims=True)


def _acc_out(ref, i, val):
    @pl.when(i == 0)
    def _():
        ref[...] = jnp.zeros_like(ref)

    ref[...] += val


def _sigmoid(v):
    return jax.nn.sigmoid(v)


def _norm_mod(x, gain, sc, sh, *, name, y=None, g=None, ymul=None, tm=512):
    n, d = x.shape
    tm = min(tm, n)
    has_res = y is not None
    has_mul = ymul is not None

    def body(*refs):
        it = iter(refs)
        x_ref = next(it)
        y_ref = next(it) if has_res else None
        g_ref = next(it) if has_res else None
        m_ref = next(it) if has_mul else None
        gain_ref, sc_ref, sh_ref = next(it), next(it), next(it)
        xo_ref = next(it) if has_res else None
        a_ref = next(it)
        xv = x_ref[...]
        if has_res:
            yv = y_ref[...]
            if has_mul:
                yv = yv * m_ref[...]
            xv = xv + g_ref[...] * yv
            xo_ref[...] = xv
        r = lax.rsqrt(jnp.mean(xv * xv, axis=-1, keepdims=True) + EPS)
        nrm = (xv * r) * gain_ref[...]
        a_ref[...] = (nrm * (1.0 + sc_ref[...]) + sh_ref[...]).astype(BF16)

    row = pl.BlockSpec((tm, d), lambda i: (i, 0))
    in_specs, args = [row], [x]
    if has_res:
        in_specs += [row, _vec(d)]
        args += [y, g]
    if has_mul:
        in_specs.append(_vec(d))
        args.append(ymul)
    in_specs += [_vec(d)] * 3
    args += [gain, sc, sh]
    out_specs, out_shape = [], []
    if has_res:
        out_specs.append(row)
        out_shape.append(jax.ShapeDtypeStruct((n, d), F32))
    out_specs.append(row)
    out_shape.append(jax.ShapeDtypeStruct((n, d), BF16))
    res = pl.pallas_call(body, grid=(n // tm,), in_specs=in_specs, out_specs=out_specs, out_shape=out_shape,
                         name=name, compiler_params=_params("parallel"))(*args)
    return res if has_res else res[0]


def _norm_mod_bwd(da, x, gain, sc, *, name, dres=None, gate_y=None, gate_g=None, tm=512):
    n, d = x.shape
    tm = min(tm, n)
    has_res = dres is not None
    has_gate = gate_y is not None

    def body(*refs):
        it = iter(refs)
        da_ref, x_ref = next(it), next(it)
        r_ref = next(it) if has_res else None
        y_ref = next(it) if has_gate else None
        g_ref = next(it) if has_gate else None
        gain_ref, sc_ref = next(it), next(it)
        dx_ref, dsh_ref, dsc_ref, dgn_ref = next(it), next(it), next(it), next(it)
        dy_ref = next(it) if has_gate else None
        dg_ref = next(it) if has_gate else None
        i = pl.program_id(0)
        xv = x_ref[...]
        dav = da_ref[...]
        r = lax.rsqrt(jnp.mean(xv * xv, axis=-1, keepdims=True) + EPS)
        xh = xv * r
        nrm = xh * gain_ref[...]
        dn = dav * (1.0 + sc_ref[...])
        dxh = dn * gain_ref[...]
        dx = r * (dxh - xh * jnp.mean(dxh * xh, axis=-1, keepdims=True))
        if has_res:
            dx = dx + r_ref[...]
        dx_ref[...] = dx
        _acc_out(dsh_ref, i, _colsum(dav))
        _acc_out(dsc_ref, i, _colsum(dav * nrm))
        _acc_out(dgn_ref, i, _colsum(dn * xh))
        if has_gate:
            dy_ref[...] = (dx * g_ref[...]).astype(BF16)
            _acc_out(dg_ref, i, _colsum(dx * y_ref[...]))

    row = pl.BlockSpec((tm, d), lambda i: (i, 0))
    in_specs, args = [row, row], [da, x]
    if has_res:
        in_specs.append(row)
        args.append(dres)
    if has_gate:
        in_specs += [row, _vec(d)]
        args += [gate_y, gate_g]
    in_specs += [_vec(d)] * 2
    args += [gain, sc]
    vec_shape = jax.ShapeDtypeStruct((1, d), F32)
    out_specs = [row, _vec(d), _vec(d), _vec(d)]
    out_shape = [jax.ShapeDtypeStruct((n, d), F32), vec_shape, vec_shape, vec_shape]
    if has_gate:
        out_specs += [row, _vec(d)]
        out_shape += [jax.ShapeDtypeStruct((n, d), BF16), vec_shape]
    return pl.pallas_call(
        body, grid=(n // tm,), in_specs=in_specs, out_specs=out_specs, out_shape=out_shape,
        name=name, compiler_params=_params("arbitrary"))(*args)


def _ffn_up_glu(f, w_up, cw, cb, *, name, tm=256, tc=256):
    n, d = f.shape
    tm = min(tm, n)
    ni = n // tm
    nc = DFF // tc
    halo = 16
    rows = tm + 2 * halo
    r = tm // halo
    last = n // halo - 1

    def body(f_ref, fp_ref, fn_ref, w_ref, cw_ref, cb_ref, u_ref, gc_ref, h_ref):
        i = pl.program_id(0)
        a = f_ref[...]
        aext = jnp.concatenate([jnp.where(i > 0, fp_ref[...], jnp.zeros_like(fp_ref[...])), a,
                                jnp.where(i < ni - 1, fn_ref[...], jnp.zeros_like(fn_ref[...]))], axis=0)
        for j in range(nc):
            cols = slice(j * tc, (j + 1) * tc)
            vcols = slice(DFF + j * tc, DFF + (j + 1) * tc)
            gext = jnp.dot(aext, w_ref[:, cols], preferred_element_type=F32)
            val = jnp.dot(a, w_ref[:, vcols], preferred_element_type=F32)
            gate = gext[halo:halo + tm]
            gc = (pltpu.roll(gext, 1, axis=0)[halo:halo + tm] * cw_ref[0:1, cols] + gate * cw_ref[1:2, cols]
                  + pltpu.roll(gext, rows - 1, axis=0)[halo:halo + tm] * cw_ref[2:3, cols]) + cb_ref[:, cols]
            u_ref[:, cols] = gate
            u_ref[:, vcols] = val
            gc_ref[:, cols] = gc
            h_ref[:, cols] = (gc * _sigmoid(gc) * val).astype(BF16)

    return pl.pallas_call(
        body, grid=(ni,),
        in_specs=[pl.BlockSpec((tm, d), lambda i: (i, 0)),
                  pl.BlockSpec((halo, d), lambda i: (jnp.maximum(i * r - 1, 0), 0)),
                  pl.BlockSpec((halo, d), lambda i: (jnp.minimum((i + 1) * r, last), 0)),
                  pl.BlockSpec(w_up.shape, lambda i: (0, 0)), pl.BlockSpec((3, DFF), lambda i: (0, 0)),
                  pl.BlockSpec((1, DFF), lambda i: (0, 0))],
        out_specs=[pl.BlockSpec((tm, 2 * DFF), lambda i: (i, 0)), pl.BlockSpec((tm, DFF), lambda i: (i, 0)),
                   pl.BlockSpec((tm, DFF), lambda i: (i, 0))],
        out_shape=[jax.ShapeDtypeStruct((n, 2 * DFF), F32), jax.ShapeDtypeStruct((n, DFF), F32),
                   jax.ShapeDtypeStruct((n, DFF), BF16)], name=name,
        compiler_params=_params("parallel"))(f, f, f, w_up, cw, cb)


def _ffn_down_glu_bwd(dz, w_down, u, gc, cw, *, name, tm=256, tc=256):
    n, d = dz.shape
    tm = min(tm, n)
    ni = n // tm
    nc = DFF // tc
    rows = tm + 2 * HALO

    def body(z_ref, zp_ref, zn_ref, w_ref, u_ref, vp_ref, vn_ref, c_ref, cp_ref, cn_ref, cw_ref,
             dg_ref, dv_ref, dcw_ref, dcb_ref):
        i = pl.program_id(0)

        @pl.when(i == 0)
        def _():
            dcw_ref[...] = jnp.zeros_like(dcw_ref)
            dcb_ref[...] = jnp.zeros_like(dcb_ref)

        zext = jnp.concatenate([jnp.where(i > 0, zp_ref[...], jnp.zeros_like(zp_ref[...])), z_ref[...],
                                jnp.where(i < ni - 1, zn_ref[...], jnp.zeros_like(zn_ref[...]))], axis=0)
        for j in range(nc):
            cols = slice(j * tc, (j + 1) * tc)
            vcols = slice(DFF + j * tc, DFF + (j + 1) * tc)
            dh = lax.dot_general(zext, w_ref[cols, :], _NT, preferred_element_type=F32)[HALO:HALO + rows]
            gcx = jnp.concatenate([cp_ref[:, cols], c_ref[:, cols], cn_ref[:, cols]], axis=0)
            vext = jnp.concatenate([vp_ref[:, cols], u_ref[:, vcols], vn_ref[:, cols]], axis=0)
            sg = _sigmoid(gcx)
            dgc = dh * vext * (sg * (1.0 + gcx * (1.0 - sg)))
            dv_ref[:, cols] = (dh[HALO:HALO + tm] * (gcx[HALO:HALO + tm] * sg[HALO:HALO + tm])).astype(BF16)
            d_next = pltpu.roll(dgc, rows - 1, axis=0)[HALO:HALO + tm]
            d_prev = pltpu.roll(dgc, 1, axis=0)[HALO:HALO + tm]
            d_here = dgc[HALO:HALO + tm]
            dg_ref[:, cols] = (d_next * cw_ref[0:1, cols] + d_here * cw_ref[1:2, cols]
                               + d_prev * cw_ref[2:3, cols]).astype(BF16)
            gate = u_ref[:, cols]
            dcw_ref[:, cols] += jnp.concatenate([_colsum(d_next * gate), _colsum(d_here * gate),
                                                 _colsum(d_prev * gate)], axis=0)
            dcb_ref[:, cols] += _colsum(d_here)

    def trio(width, halo, tile_width=None, colblk=0):
        r, last = tm // halo, n // halo - 1
        return [pl.BlockSpec((tm, tile_width or width), lambda i: (i, 0)),
                pl.BlockSpec((halo, width), lambda i: (jnp.maximum(i * r - 1, 0), colblk)),
                pl.BlockSpec((halo, width), lambda i: (jnp.minimum((i + 1) * r, last), colblk))]

    whole = lambda shape: pl.BlockSpec(shape, lambda i: (0, 0))
    return pl.pallas_call(
        body, grid=(ni,),
        in_specs=(trio(d, 16) + [whole(w_down.shape)] + trio(DFF, HALO, tile_width=2 * DFF, colblk=1)
                  + trio(DFF, HALO) + [whole((3, DFF))]),
        out_specs=[pl.BlockSpec((tm, DFF), lambda i: (i, 0)), pl.BlockSpec((tm, DFF), lambda i: (i, 0)),
                   whole((3, DFF)), whole((1, DFF))],
        out_shape=[jax.ShapeDtypeStruct((n, DFF), BF16), jax.ShapeDtypeStruct((n, DFF), BF16),
                   jax.ShapeDtypeStruct((3, DFF), F32), jax.ShapeDtypeStruct((1, DFF), F32)],
        name=name, compiler_params=_params("arbitrary"))(dz, dz, dz, w_down, u, u, u, gc, gc, gc, cw)


def _rope_tables(n):
    rows = n // GRID_W
    axis_dim = HD // 2
    inv_freq = jnp.power(ROPE_THETA, -jnp.arange(0, axis_dim, 2, dtype=F32) / axis_dim)
    ar = jnp.arange(rows, dtype=F32)[:, None] * inv_freq
    ac = jnp.arange(GRID_W, dtype=F32)[:, None] * inv_freq
    by_row = lambda a: jnp.repeat(a, GRID_W, axis=0)
    by_col = lambda a: jnp.tile(a, (rows, 1))
    cr, sr, cc, sc = by_row(jnp.cos(ar)), by_row(jnp.sin(ar)), by_col(jnp.cos(ac)), by_col(jnp.sin(ac))
    return jnp.concatenate([cr, cr, cc, cc], axis=1), jnp.concatenate([-sr, sr, -sc, sc], axis=1)


def _partner(v):
    lane = lax.broadcasted_iota(jnp.int32, v.shape, 1)
    return jnp.where((lane % 64) < 32, pltpu.roll(v, HD - 32, axis=1), pltpu.roll(v, 32, axis=1))


def _qkv_prep(p, q_gain, k_gain, cs, sn, *, name, has_q, kv_col, kv_rows=None, kv_row_off=0, kv_into=None, tm=256):
    n = p.shape[0]
    rope = cs is not None
    kv_rows = kv_rows or n
    rb = kv_row_off // tm

    def body(*refs):
        it = iter(refs)
        q_ref = next(it) if has_q else None
        kv_ref = next(it)
        qg_ref, kg_ref = next(it), next(it)
        cs_ref = next(it) if rope else None
        sn_ref = next(it) if rope else None
        if kv_into is not None:
            next(it), next(it)
        qo_ref = next(it) if has_q else None
        ko_ref, vo_ref = next(it), next(it)

        def norm_rope(xh, gain, mul=None):
            r = lax.rsqrt(jnp.mean(xh * xh, axis=-1, keepdims=True) + EPS)
            xn = (xh * r) * gain
            if rope:
                xn = xn * cs_ref[...] + _partner(xn) * sn_ref[...]
            if mul is not None:
                xn = xn * mul
            return xn.astype(BF16)

        if has_q:
            for h in range(NQ):
                qo_ref[h] = norm_rope(q_ref[:, h * HD:(h + 1) * HD], qg_ref[...], _QSCALE)
        for h in range(NKV):
            ko_ref[h] = norm_rope(kv_ref[:, h * HD:(h + 1) * HD], kg_ref[...])
            vo_ref[h] = kv_ref[:, (NKV + h) * HD:(NKV + h + 1) * HD].astype(BF16)

    in_specs, args = [], []
    if has_q:
        in_specs.append(pl.BlockSpec((tm, AW), lambda i: (i, 0)))
        args.append(p)
    in_specs += [pl.BlockSpec((tm, 2 * NKV * HD), lambda i: (i, kv_col)), _vec(HD), _vec(HD)]
    args += [p, q_gain, k_gain]
    if rope:
        in_specs += [pl.BlockSpec((tm, HD), lambda i: (i, 0))] * 2
        args += [cs, sn]
    out_specs, out_shape = [], []
    if has_q:
        out_specs.append(pl.BlockSpec((NQ, tm, HD), lambda i: (0, i, 0)))
        out_shape.append(jax.ShapeDtypeStruct((NQ, n, HD), BF16))
    out_specs += [pl.BlockSpec((NKV, tm, HD), lambda i: (0, rb + i, 0))] * 2
    out_shape += [jax.ShapeDtypeStruct((NKV, kv_rows, HD), BF16)] * 2
    aliases = {}
    if kv_into is not None:
        aliases = {len(args): int(has_q), len(args) + 1: int(has_q) + 1}
        in_specs += [pl.BlockSpec(memory_space=pl.ANY)] * 2
        args += list(kv_into)
    return pl.pallas_call(body, grid=(n // tm,), in_specs=in_specs, out_specs=out_specs, out_shape=out_shape,
                          input_output_aliases=aliases, name=name, compiler_params=_params("parallel"))(*args)


def _in_proj_qkv(a, w_in_t, q_gain, k_gain, cs, sn, conv_w, kv_into, *, name, kv_row_off, tm=256):
    n, d = a.shape
    nproj = w_in_t.shape[0]
    nqkv = AW + 2 * NKV * HD
    rb = kv_row_off // tm
    ni = n // tm
    halo = 16
    rows = tm + 2 * halo
    r = tm // halo
    last = n // halo - 1

    def body(a_ref, ap_ref, an_ref, w_ref, qg_ref, kg_ref, cs_ref, sn_ref, cw_ref, _k_in, _v_in,
             p_ref, qo_ref, ko_ref, vo_ref, conv_ref):
        i = pl.program_id(0)
        av = a_ref[...]
        aext = jnp.concatenate([jnp.where(i > 0, ap_ref[...], jnp.zeros_like(ap_ref[...])), av,
                                jnp.where(i < ni - 1, an_ref[...], jnp.zeros_like(an_ref[...]))], axis=0)
        qkv = lax.dot_general(av, w_ref[0:nqkv, :], _NT, preferred_element_type=F32)
        p_ref[:, 0:nqkv] = qkv
        cext = lax.dot_general(aext, w_ref[nqkv:nproj, :], _NT, preferred_element_type=F32)
        p_ref[:, nqkv:nproj] = cext[halo:halo + tm]
        hext = cext[:, CW:2 * CW] * cext[:, 2 * CW:3 * CW]
        cv3 = (pltpu.roll(hext, 1, axis=0)[halo:halo + tm] * cw_ref[0:1, :] + hext[halo:halo + tm] * cw_ref[1:2, :]
               + pltpu.roll(hext, rows - 1, axis=0)[halo:halo + tm] * cw_ref[2:3, :])
        conv_ref[...] = (cext[halo:halo + tm, 0:CW] * cv3).astype(BF16)

        def norm_rope(xh, gain, mul=None):
            r = lax.rsqrt(jnp.mean(xh * xh, axis=-1, keepdims=True) + EPS)
            xn = (xh * r) * gain
            xn = xn * cs_ref[...] + _partner(xn) * sn_ref[...]
            if mul is not None:
                xn = xn * mul
            return xn.astype(BF16)

        for h in range(NQ):
            qo_ref[h] = norm_rope(qkv[:, h * HD:(h + 1) * HD], qg_ref[...], _QSCALE)
        for h in range(NKV):
            ko_ref[h] = norm_rope(qkv[:, AW + h * HD:AW + (h + 1) * HD], kg_ref[...])
            vo_ref[h] = qkv[:, AW + (NKV + h) * HD:AW + (NKV + h + 1) * HD].astype(BF16)

    kv_rows = kv_into[0].shape[1]
    tab = pl.BlockSpec((tm, HD), lambda i: (i, 0))
    any_spec = pl.BlockSpec(memory_space=pl.ANY)
    kv_spec = pl.BlockSpec((NKV, tm, HD), lambda i: (0, rb + i, 0))
    return pl.pallas_call(
        body, grid=(ni,),
        in_specs=[pl.BlockSpec((tm, d), lambda i: (i, 0)),
                  pl.BlockSpec((halo, d), lambda i: (jnp.maximum(i * r - 1, 0), 0)),
                  pl.BlockSpec((halo, d), lambda i: (jnp.minimum((i + 1) * r, last), 0)),
                  pl.BlockSpec(w_in_t.shape, lambda i: (0, 0)), _vec(HD), _vec(HD), tab, tab,
                  pl.BlockSpec((3, CW), lambda i: (0, 0)), any_spec, any_spec],
        out_specs=[pl.BlockSpec((tm, nproj), lambda i: (i, 0)), pl.BlockSpec((NQ, tm, HD), lambda i: (0, i, 0)),
                   kv_spec, kv_spec, pl.BlockSpec((tm, CW), lambda i: (i, 0))],
        out_shape=[jax.ShapeDtypeStruct((n, nproj), F32), jax.ShapeDtypeStruct((NQ, n, HD), BF16),
                   jax.ShapeDtypeStruct((NKV, kv_rows, HD), BF16), jax.ShapeDtypeStruct((NKV, kv_rows, HD), BF16),
                   jax.ShapeDtypeStruct((n, CW), BF16)],
        input_output_aliases={9: 2, 10: 3}, name=name,
        compiler_params=_params("parallel"))(a, a, a, w_in_t, q_gain, k_gain, cs, sn, conv_w, *kv_into)


def _qkv_bwd(p, dq, dk, dv, q_gain, k_gain, cs, sn, *, name, has_q, kv_col, kv_row_off, tm=256):
    n = p.shape[0]
    rope = cs is not None
    rb = kv_row_off // tm

    def body(*refs):
        it = iter(refs)
        q_ref = next(it) if has_q else None
        kv_ref = next(it)
        dq_ref = next(it) if has_q else None
        dk_ref, dv_ref = next(it), next(it)
        qg_ref, kg_ref = next(it), next(it)
        cs_ref = next(it) if rope else None
        sn_ref = next(it) if rope else None
        dp_ref, dqg_ref, dkg_ref = next(it), next(it), next(it)
        i = pl.program_id(0)

        def back(xh, dout, gain):
            if rope:
                dout = dout * cs_ref[...] + _partner(dout * sn_ref[...])
            r = lax.rsqrt(jnp.mean(xh * xh, axis=-1, keepdims=True) + EPS)
            xhat = xh * r
            dxh = dout * gain
            dx = r * (dxh - xhat * jnp.mean(dxh * xhat, axis=-1, keepdims=True))
            return dx, _colsum(dout * xhat)

        dqg = jnp.zeros((1, HD), F32)
        dkg = jnp.zeros((1, HD), F32)
        if has_q:
            for h in range(NQ):
                dx, dg = back(q_ref[:, h * HD:(h + 1) * HD], dq_ref[h], qg_ref[...])
                dp_ref[:, h * HD:(h + 1) * HD] = dx.astype(BF16)
                dqg = dqg + dg
        else:
            dp_ref[:, 0:AW] = jnp.zeros((tm, AW), BF16)
        for h in range(NKV):
            dx, dg = back(kv_ref[:, h * HD:(h + 1) * HD], dk_ref[h], kg_ref[...])
            dp_ref[:, AW + h * HD:AW + (h + 1) * HD] = dx.astype(BF16)
            dkg = dkg + dg
            dp_ref[:, AW + (NKV + h) * HD:AW + (NKV + h + 1) * HD] = dv_ref[h].astype(BF16)
        _acc_out(dqg_ref, i, dqg)
        _acc_out(dkg_ref, i, dkg)

    in_specs, args = [], []
    if has_q:
        in_specs.append(pl.BlockSpec((tm, AW), lambda i: (i, 0)))
        args.append(p)
    in_specs.append(pl.BlockSpec((tm, 2 * NKV * HD), lambda i: (i, kv_col)))
    args.append(p)
    if has_q:
        in_specs.append(pl.BlockSpec((NQ, tm, HD), lambda i: (0, i, 0)))
        args.append(dq)
    in_specs += [pl.BlockSpec((NKV, tm, HD), lambda i: (0, rb + i, 0))] * 2 + [_vec(HD), _vec(HD)]
    args += [dk, dv, q_gain, k_gain]
    if rope:
        in_specs += [pl.BlockSpec((tm, HD), lambda i: (i, 0))] * 2
        args += [cs, sn]
    return pl.pallas_call(
        body, grid=(n // tm,), in_specs=in_specs,
        out_specs=[pl.BlockSpec((tm, D), lambda i: (i, 0)), _vec(HD), _vec(HD)],
        out_shape=[jax.ShapeDtypeStruct((n, D), BF16), jax.ShapeDtypeStruct((1, HD), F32),
                   jax.ShapeDtypeStruct((1, HD), F32)],
        name=name, compiler_params=_params("arbitrary"))(*args)


def _out_proj_dx_conv_bwd(dy, w_out, p, conv_w, *, name, tm=256):
    n, d = dy.shape
    ni = n // tm
    rows = tm + 2 * HALO

    def body(z_ref, zp_ref, zn_ref, wo_ref, gb_ref, gbp_ref, gbn_ref, gc_ref, gcp_ref, gcn_ref, xi_ref, xip_ref,
             xin_ref, w_ref, do_ref, dp_ref, dw_ref):
        i = pl.program_id(0)
        zext = jnp.concatenate([jnp.where(i > 0, zp_ref[...], jnp.zeros_like(zp_ref[...])), z_ref[...],
                                jnp.where(i < ni - 1, zn_ref[...], jnp.zeros_like(zn_ref[...]))], axis=0)
        do_ref[...] = lax.dot_general(z_ref[...], wo_ref[0:AW, :], _NT, preferred_element_type=F32)
        dconv = lax.dot_general(zext, wo_ref[AW:D, :], _NT, preferred_element_type=F32)[HALO:HALO + rows]
        gcext = _ext(gcp_ref, gc_ref, gcn_ref, i, ni)
        xiext = _ext(xip_ref, xi_ref, xin_ref, i, ni)
        hext = gcext * xiext
        dcv = dconv * _ext(gbp_ref, gb_ref, gbn_ref, i, ni)
        dp_ref[:, 0:CW] = (dconv[HALO:HALO + tm] * _conv3(hext, w_ref, tm)).astype(BF16)
        dh = _sh(dcv, 1, tm) * w_ref[0:1, :] + _sh(dcv, 0, tm) * w_ref[1:2, :] + _sh(dcv, -1, tm) * w_ref[2:3, :]
        dp_ref[:, CW:2 * CW] = (dh * xi_ref[...]).astype(BF16)
        dp_ref[:, 2 * CW:3 * CW] = (dh * gc_ref[...]).astype(BF16)
        dcv_t = dcv[HALO:HALO + tm]
        dw = jnp.concatenate([_colsum(dcv_t * _sh(hext, -1, tm)), _colsum(dcv_t * _sh(hext, 0, tm)),
                              _colsum(dcv_t * _sh(hext, 1, tm))], axis=0)
        _acc_out(dw_ref, i, dw)

    def trio(colblk):
        prev, nxt = _halo_specs(tm, CW, n, colblk=colblk)
        return [pl.BlockSpec((tm, CW), lambda i: (i, colblk)), prev, nxt]

    r16, last16 = tm // 16, n // 16 - 1
    zspecs = [pl.BlockSpec((tm, d), lambda i: (i, 0)),
              pl.BlockSpec((16, d), lambda i: (jnp.maximum(i * r16 - 1, 0), 0)),
              pl.BlockSpec((16, d), lambda i: (jnp.minimum((i + 1) * r16, last16), 0))]
    return pl.pallas_call(
        body, grid=(ni,),
        in_specs=zspecs + [pl.BlockSpec(w_out.shape, lambda i: (0, 0))] + trio(2) + trio(3) + trio(4)
        + [pl.BlockSpec((3, CW), lambda i: (0, 0))],
        out_specs=[pl.BlockSpec((tm, AW), lambda i: (i, 0)), pl.BlockSpec((tm, 3 * CW), lambda i: (i, 0)),
                   pl.BlockSpec((3, CW), lambda i: (0, 0))],
        out_shape=[jax.ShapeDtypeStruct((n, AW), F32), jax.ShapeDtypeStruct((n, 3 * CW), BF16),
                   jax.ShapeDtypeStruct((3, CW), F32)],
        name=name, compiler_params=_params("arbitrary"))(dy, dy, dy, w_out, p, p, p, p, p, p, p, p, p, conv_w)


def _attn_fwd(q, k, v, *, name, bq=512, sub=256):
    n = q.shape[1]
    t = k.shape[1]
    bq = min(bq, n)
    sub = min(sub, 2 * bq)

    def body(q_ref, k_ref, v_ref, o_ref, lse_ref):
        q2 = q_ref[...].reshape(2 * bq, HD)
        outs, lses = [], []
        for r0 in range(0, 2 * bq, sub):
            s = lax.dot_general(q2[r0:r0 + sub], k_ref[0], _NT, preferred_element_type=F32)
            m = jnp.max(s, axis=-1, keepdims=True)
            pv = jnp.exp2(s - m)
            l = jnp.sum(pv, axis=-1, keepdims=True)
            outs.append(jnp.dot(pv.astype(BF16), v_ref[0], preferred_element_type=F32) / l)
            lses.append(m + jnp.log2(l))
        out = jnp.concatenate(outs, axis=0)
        o_ref[:, 0:HD] = out[0:bq]
        o_ref[:, HD:2 * HD] = out[bq:2 * bq]
        lse_ref[...] = jnp.concatenate(lses, axis=0).reshape(2, bq, 1)

    kspec = pl.BlockSpec((1, t, HD), lambda h, i: (h, 0, 0))
    return pl.pallas_call(
        body, grid=(NKV, n // bq),
        in_specs=[pl.BlockSpec((2, bq, HD), lambda h, i: (h, i, 0)), kspec, kspec],
        out_specs=[pl.BlockSpec((bq, 2 * HD), lambda h, i: (i, h)), pl.BlockSpec((2, bq, 1), lambda h, i: (h, i, 0))],
        out_shape=[jax.ShapeDtypeStruct((n, AW), F32), jax.ShapeDtypeStruct((NQ, n, 1), F32)],
        name=name, compiler_params=_params("parallel", "parallel"))(q, k, v)


def _attn_bwd(q, k, v, dcat, o, lse, *, name, bq=256):
    n = q.shape[1]
    t = k.shape[1]
    bq = min(bq, n)

    def body(q_ref, k_ref, v_ref, dc_ref, o_ref, lse_ref, dq_ref, dk_ref, dv_ref):
        @pl.when(pl.program_id(1) == 0)
        def _():
            dk_ref[...] = jnp.zeros_like(dk_ref)
            dv_ref[...] = jnp.zeros_like(dv_ref)

        q2 = q_ref[...].reshape(2 * bq, HD)
        do_f = jnp.concatenate([dc_ref[:, 0:HD], dc_ref[:, HD:2 * HD]], axis=0)
        o_f = jnp.concatenate([o_ref[:, 0:HD], o_ref[:, HD:2 * HD]], axis=0)
        delta = jnp.sum(do_f * o_f, axis=-1, keepdims=True)
        do2 = do_f.astype(BF16)
        s = lax.dot_general(q2, k_ref[0], _NT, preferred_element_type=F32)
        pv = jnp.exp2(s - lse_ref[...].reshape(2 * bq, 1))
        dp = lax.dot_general(do2, v_ref[0], _NT, preferred_element_type=F32)
        ds = (pv * (dp - delta)).astype(BF16)
        dq_ref[...] = (jnp.dot(ds, k_ref[0], preferred_element_type=F32) * _SCALE).reshape(2, bq, HD)
        dk_ref[0] += lax.dot_general(ds, q2, _TN, preferred_element_type=F32) * _LN2
        dv_ref[0] += lax.dot_general(pv.astype(BF16), do2, _TN, preferred_element_type=F32)

    qspec = pl.BlockSpec((2, bq, HD), lambda h, i: (h, i, 0))
    kspec = pl.BlockSpec((1, t, HD), lambda h, i: (h, 0, 0))
    sspec = pl.BlockSpec((2, bq, 1), lambda h, i: (h, i, 0))
    cspec = pl.BlockSpec((bq, 2 * HD), lambda h, i: (i, h))
    return pl.pallas_call(
        body, grid=(NKV, n // bq), in_specs=[qspec, kspec, kspec, cspec, cspec, sspec], out_specs=[qspec, kspec, kspec],
        out_shape=[jax.ShapeDtypeStruct((NQ, n, HD), F32), jax.ShapeDtypeStruct((NKV, t, HD), F32),
                   jax.ShapeDtypeStruct((NKV, t, HD), F32)],
        name=name, compiler_params=_params("parallel", "arbitrary"))(q, k, v, dcat, o, lse)


def _window_sums(ext, w):
    s, step = ext, 1
    while step < w:
        s = s + _roll_rows(s, step)
        step *= 2
    return s


def _pool_counts(i, tm, n, w, rows, first):
    t = i * tm - HALO + first + lax.broadcasted_iota(jnp.int32, (rows, 1), 0)
    lo = jnp.clip(t - w // 2, 0, n)
    hi = jnp.clip(t + w - w // 2, 0, n)
    return jnp.maximum(hi - lo, 1).astype(F32)


def _norm_mod_ext(xext, gain_ref, sc_ref, sh_ref, i, tm, n):
    rows = xext.shape[0]
    t = i * tm - HALO + lax.broadcasted_iota(jnp.int32, (rows, 1), 0)
    inside = (t >= 0) & (t < n)
    r = lax.rsqrt(jnp.mean(xext * xext, axis=-1, keepdims=True) + EPS)
    xh = xext * r
    a = (xh * gain_ref[...]) * (1.0 + sc_ref[...]) + sh_ref[...]
    return jnp.where(inside, a, 0.0), r, xh


def _pool_fwd(x, y, g, gain, sc, sh, pool_w, *, name, tm=256):
    n, d = x.shape
    ni = n // tm

    def body(x_ref, xp_ref, xn_ref, y_ref, yp_ref, yn_ref, g_ref, gain_ref, sc_ref, sh_ref, w_ref, xo_ref, o_ref):
        i = pl.program_id(0)
        xext = _ext(xp_ref, x_ref, xn_ref, i, ni) + g_ref[...] * _ext(yp_ref, y_ref, yn_ref, i, ni)
        xo_ref[...] = xext[HALO:HALO + tm]
        aext, _, _ = _norm_mod_ext(xext, gain_ref, sc_ref, sh_ref, i, tm, n)
        for gi, w in enumerate(POOL_WINDOWS):
            ag = aext[:, gi * PG:(gi + 1) * PG]
            mean = _sh(_window_sums(ag, w), -(w // 2), tm) / _pool_counts(i, tm, n, w, tm, HALO)
            pooled = mean - ag[HALO:HALO + tm]
            o_ref[:, gi * PG:(gi + 1) * PG] = jnp.dot(pooled.astype(BF16), w_ref[gi], preferred_element_type=F32)

    row = pl.BlockSpec((tm, d), lambda i: (i, 0))
    prev, nxt = _halo_specs(tm, d, n)
    return pl.pallas_call(
        body, grid=(ni,),
        in_specs=[row, prev, nxt, row, prev, nxt, _vec(d), _vec(d), _vec(d), _vec(d),
                  pl.BlockSpec((4, PG, PG), lambda i: (0, 0, 0))],
        out_specs=[row, row], out_shape=[jax.ShapeDtypeStruct((n, d), F32)] * 2,
        name=name, compiler_params=_params("parallel"))(x, x, x, y, y, y, g, gain, sc, sh, pool_w)


def _pool_bwd(dxo, mixed, x, g, scale, gain, sc, sh, pool_w, zprev, gprev, *, name, tm=256):
    n, d = x.shape
    ni = n // tm

    def body(dx_ref, dxp_ref, dxn_ref, mx_ref, x_ref, xp_ref, xn_ref, g_ref, s_ref, gain_ref, sc_ref, sh_ref, w_ref,
             zp_ref, gp_ref, dxi_ref, dw_ref, dg_ref, dsl_ref, dsh_ref, dsc_ref, dgn_ref, dzp_ref, dgp_ref):
        i = pl.program_id(0)

        @pl.when(i == 0)
        def _():
            dw_ref[...] = jnp.zeros_like(dw_ref)

        dxo_t = dx_ref[...]
        mixed_t = mx_ref[...]
        dy_t = dxo_t * g_ref[...]
        _acc_out(dg_ref, i, _colsum(dxo_t * (mixed_t * s_ref[...])))
        _acc_out(dsl_ref, i, _colsum(dy_t * mixed_t))
        dmixed = (_ext(dxp_ref, dx_ref, dxn_ref, i, ni) * g_ref[...]) * s_ref[...]
        xext = _ext(xp_ref, x_ref, xn_ref, i, ni)
        aext, rext, xhext = _norm_mod_ext(xext, gain_ref, sc_ref, sh_ref, i, tm, n)
        rows = tm + 2 * HALO
        da_parts = []
        for gi, w in enumerate(POOL_WINDOWS):
            sl = slice(gi * PG, (gi + 1) * PG)
            ag = aext[:, sl]
            mean = _sh(_window_sums(ag, w), -(w // 2), tm) / _pool_counts(i, tm, n, w, tm, HALO)
            pooled = (mean - ag[HALO:HALO + tm]).astype(BF16)
            dmg = dmixed[:, sl].astype(BF16)
            dw_ref[gi] += lax.dot_general(pooled, dmixed[HALO:HALO + tm, sl].astype(BF16), _TN,
                                          preferred_element_type=F32)
            dpl = lax.dot_general(dmg, w_ref[gi], _NT, preferred_element_type=F32)
            e = dpl / _pool_counts(i, tm, n, w, rows, 0)
            da_parts.append(_sh(_window_sums(e, w), 1 - w // 2, tm) - dpl[HALO:HALO + tm])
        da = jnp.concatenate(da_parts, axis=1)
        r = rext[HALO:HALO + tm]
        xh = xhext[HALO:HALO + tm]
        nrm = xh * gain_ref[...]
        dn = da * (1.0 + sc_ref[...])
        dxh = dn * gain_ref[...]
        dxi = dxo_t + r * (dxh - xh * jnp.mean(dxh * xh, axis=-1, keepdims=True))
        dxi_ref[...] = dxi
        _acc_out(dsh_ref, i, _colsum(da))
        _acc_out(dsc_ref, i, _colsum(da * nrm))
        _acc_out(dgn_ref, i, _colsum(dn * xh))
        dzp_ref[...] = (dxi * gp_ref[...]).astype(BF16)
        _acc_out(dgp_ref, i, _colsum(dxi * zp_ref[...]))

    row = pl.BlockSpec((tm, d), lambda i: (i, 0))
    prev, nxt = _halo_specs(tm, d, n)
    wspec = pl.BlockSpec((4, PG, PG), lambda i: (0, 0, 0))
    vshape = jax.ShapeDtypeStruct((1, d), F32)
    return pl.pallas_call(
        body, grid=(ni,),
        in_specs=[row, prev, nxt, row, row, prev, nxt] + [_vec(d)] * 5 + [wspec, row, _vec(d)],
        out_specs=[row, wspec] + [_vec(d)] * 5 + [row, _vec(d)],
        out_shape=[jax.ShapeDtypeStruct((n, d), F32), jax.ShapeDtypeStruct((4, PG, PG), F32)] + [vshape] * 5
        + [jax.ShapeDtypeStruct((n, d), BF16), vshape],
        name=name, compiler_params=_params("arbitrary"))(dxo, dxo, dxo, mixed, x, x, x, g, scale, gain, sc, sh, pool_w,
                                                         zprev, gprev)


def _adamw(gparts_list, w, m, v, *, name, silu_grad_of=None):
    nl = len(gparts_list)
    nparts, r, c = gparts_list[0].shape
    tr = _pick(r, (256, 128, 64, 32, 16, 8))
    has_c = silu_grad_of is not None

    def body(*refs):
        gp_refs = refs[:nl]
        it = iter(refs[nl:])
        w_ref, m_ref, v_ref = next(it), next(it), next(it)
        c_ref = next(it) if has_c else None
        g_ref, d_ref, mo_ref, vo_ref = next(it), next(it), next(it), next(it)
        layer = pl.program_id(0)

        def update(gp_ref):
            g = gp_ref[0].astype(F32)
            for p in range(1, nparts):
                g = g + gp_ref[p].astype(F32)
            if has_c:
                cv = c_ref[0]
                sg = _sigmoid(cv)
                g = g * (sg * (1.0 + cv * (1.0 - sg)))
            g_ref[0] = g
            mn = ADAM_B1 * m_ref[0] + (1.0 - ADAM_B1) * g
            vn = ADAM_B2 * v_ref[0] + (1.0 - ADAM_B2) * (g * g)
            m_hat = mn / (1.0 - ADAM_B1 ** ADAM_STEP)
            v_hat = vn / (1.0 - ADAM_B2 ** ADAM_STEP)
            d_ref[0] = -ADAM_LR * (m_hat / (jnp.sqrt(v_hat) + ADAM_EPS) + ADAM_WD * w_ref[0])
            mo_ref[0] = mn
            vo_ref[0] = vn

        if nl == 1:
            update(gp_refs[0])
        else:
            for li in range(nl):
                pl.when(layer == li)(functools.partial(update, gp_refs[li]))

    row = pl.BlockSpec((1, tr, c), lambda l, i: (l, i, 0))
    in_specs = [pl.BlockSpec((nparts, tr, c), lambda l, i, li=li: (0, jnp.where(l == li, i, 0), 0)) for li in range(nl)]
    in_specs += [row, row, row]
    args = list(gparts_list) + [w, m, v]
    if has_c:
        in_specs.append(row)
        args.append(silu_grad_of)
    return pl.pallas_call(
        body, grid=(nl, r // tr), in_specs=in_specs, out_specs=[row] * 4,
        out_shape=[jax.ShapeDtypeStruct((nl, r, c), F32)] * 4, name=name,
        compiler_params=_params("arbitrary", "arbitrary"))(*args)


def _adamw_nd(gparts, w, m, v, *, name, silu_grad_of=None):
    shape = w.shape
    c = shape[-1]
    if isinstance(gparts, (list, tuple)):
        nl = len(gparts)
        r = math.prod(shape[1:-1])
    else:
        nl = 1
        r = math.prod(shape[:-1]) if len(shape) > 1 else 1
        gparts = [gparts]
    rs = lambda a: a.reshape(nl, r, c)
    res = _adamw([gp.reshape(gp.shape[0], r, c) for gp in gparts], rs(w), rs(m), rs(v), name=name,
                 silu_grad_of=None if silu_grad_of is None else rs(silu_grad_of))
    return [a.reshape(shape) for a in res]


def _place():
    return lax.axis_index("x"), lax.axis_index("y"), lax.axis_index("c")


def _all_gather(arrs, *, name):
    k_arr = len(arrs)

    def body(*refs):
        ins = refs[:k_arr]
        outs = refs[k_arr:2 * k_arr]
        send_sems, recv_sems, local_sems = refs[2 * k_arr:]
        x, y, c = _place()
        me, sibling = (x, y, c), (x, y, 1 - c)
        chips = [(1 - x, y), (x, 1 - y), (1 - x, 1 - y)]

        def slot(a, px, py, pc):
            return outs[a].at[4 * px + 2 * py + pc]

        def copy(a, s, block, to, src=None):
            return pltpu.make_async_remote_copy(
                src_ref=slot(a, *block) if src is None else src, dst_ref=slot(a, *block),
                send_sem=send_sems.at[a, s], recv_sem=recv_sems.at[a, s], device_id=to, device_id_type=MESH)

        mine = [pltpu.make_async_copy(ins[a], slot(a, *me), local_sems.at[a]) for a in range(k_arr)]
        for cp in mine:
            cp.start()
        first = []
        for a in range(k_arr):
            first.append(copy(a, 0, me, sibling, src=ins[a]))
            first += [copy(a, 1 + j, me, (*chip, c), src=ins[a]) for j, chip in enumerate(chips)]
        for cp in first:
            cp.start()
        passed = []
        for j, chip in enumerate(chips):
            for a in range(k_arr):
                copy(a, 1 + j, (*chip, c), me).wait_recv()
                fw = copy(a, 4 + j, (*chip, c), sibling)
                fw.start()
                passed.append(fw)
        for a in range(k_arr):
            copy(a, 0, sibling, me).wait_recv()
            for j, chip in enumerate(chips):
                copy(a, 4 + j, (*chip, 1 - c), me).wait_recv()
        for cp in first + passed:
            cp.wait_send()
        for cp in mine:
            cp.wait()

    any_spec = pl.BlockSpec(memory_space=pl.ANY)
    return pl.pallas_call(
        body, in_specs=[any_spec] * k_arr, out_specs=[any_spec] * k_arr,
        out_shape=[jax.ShapeDtypeStruct((NDEV,) + a.shape, a.dtype) for a in arrs],
        scratch_shapes=[pltpu.SemaphoreType.DMA((k_arr, 7)), pltpu.SemaphoreType.DMA((k_arr, 7)),
                        pltpu.SemaphoreType.DMA((k_arr,))],
        name=name)(*arrs)


_HBM = pl.BlockSpec(memory_space=pltpu.HBM)
_SEM = pl.BlockSpec(memory_space=pltpu.SEMAPHORE)
_EFFECT = pltpu.SideEffectType.DATAFLOW_SIDE_EFFECTING


def _peers(x, y, c):
    return [(x ^ (rel >> 2), y ^ ((rel >> 1) & 1), c ^ (rel & 1)) for rel in range(1, NDEV)]


def _exchange_copies(srcs, lands, send_sems, recv_sems, scatter):
    x, y, c = _place()
    me = 4 * x + 2 * y + c
    copies = []
    for r, (px, py, pc) in enumerate(_peers(x, y, c)):
        peer = 4 * px + 2 * py + pc
        for a in range(len(srcs)):
            copies.append(pltpu.make_async_remote_copy(
                src_ref=srcs[a].at[peer] if scatter else srcs[a], dst_ref=lands[a].at[me],
                send_sem=send_sems.at[7 * a + r], recv_sem=recv_sems.at[7 * a + r], device_id=(px, py, pc),
                device_id_type=MESH))
    return copies


def _exchange_start(arrs, *, scatter, name):
    k_arr = len(arrs)
    land_shapes = [a.shape if scatter else (NDEV,) + a.shape for a in arrs]
    lands = [pltpu.with_memory_space_constraint(lax.empty(s, a.dtype), pltpu.HBM) for s, a in zip(land_shapes, arrs)]
    srcs = [pltpu.with_memory_space_constraint(a, pltpu.HBM) for a in arrs]

    def body(*refs):
        src_refs, land_refs = refs[:k_arr], refs[k_arr:2 * k_arr]
        send_sems, recv_sems = refs[2 * k_arr], refs[2 * k_arr + 1]
        token = refs[-1]
        for cp in _exchange_copies(src_refs, land_refs, send_sems, recv_sems, scatter):
            cp.start()
        token[...] = jnp.zeros_like(token)

    out_shape = ([pltpu.SemaphoreType.DMA((7 * k_arr,)), pltpu.SemaphoreType.DMA((7 * k_arr,))]
                 + [pltpu.HBM(a.shape, a.dtype) for a in arrs] + [pltpu.HBM(s, a.dtype) for s, a in zip(land_shapes, arrs)]
                 + [jax.ShapeDtypeStruct((8, 128), F32)])
    res = pl.pallas_call(
        body, name=name, out_shape=out_shape, in_specs=[_HBM] * (2 * k_arr),
        out_specs=[_SEM, _SEM] + [_HBM] * (2 * k_arr) + [pl.BlockSpec(memory_space=pltpu.VMEM)],
        input_output_aliases={i: 2 + i for i in range(2 * k_arr)},
        compiler_params=pltpu.CompilerParams(has_side_effects=_EFFECT))(*srcs, *lands)
    return dict(send=res[0], recv=res[1], srcs=list(res[2:2 + k_arr]), lands=list(res[2 + k_arr:2 + 2 * k_arr]),
                token=res[-1], scatter=scatter)


def _exchange_wait(handle, after, *, name):
    k_arr = len(handle["srcs"])
    scatter = handle["scatter"]

    def body(*refs):
        src_refs, land_refs = refs[:k_arr], refs[k_arr:2 * k_arr]
        send_sems, recv_sems = refs[2 * k_arr], refs[2 * k_arr + 1]
        x, y, c = _place()
        me = 4 * x + 2 * y + c
        for r, (px, py, pc) in enumerate(_peers(x, y, c)):
            peer = 4 * px + 2 * py + pc
            for a in range(k_arr):
                cp = pltpu.make_async_remote_copy(
                    src_ref=src_refs[a].at[peer] if scatter else src_refs[a], dst_ref=land_refs[a].at[peer],
                    send_sem=send_sems.at[7 * a + r], recv_sem=recv_sems.at[7 * a + r], device_id=(x, y, c),
                    device_id_type=MESH)
                cp.wait_send()
                cp.wait_recv()

    arrs = handle["srcs"] + handle["lands"]
    res = pl.pallas_call(
        body, name=name, out_shape=[pltpu.HBM(a.shape, a.dtype) for a in arrs],
        in_specs=[_HBM] * (2 * k_arr) + [_SEM, _SEM, pl.BlockSpec(memory_space=pl.ANY)],
        out_specs=[_HBM] * (2 * k_arr), input_output_aliases={i: i for i in range(2 * k_arr)},
        compiler_params=pltpu.CompilerParams(has_side_effects=_EFFECT))(*arrs, handle["send"], handle["recv"], after)
    me = 4 * lax.axis_index("x") + 2 * lax.axis_index("y") + lax.axis_index("c")
    out = []
    for src, land in zip(res[:k_arr], res[k_arr:]):
        own = lax.dynamic_index_in_dim(src, me, 0, keepdims=False) if scatter else src
        out.append(lax.dynamic_update_index_in_dim(land, own, me, 0))
    return out


def _ffn_bwd(dxo, dz, xr, f, u_gc, hmid, gain, sc, w_up, cw, w_down, tag, gate_y=None, gate_g=None):
    d_wdown = _mm_tn((hmid, dz), name=f"ffn_down_dw_{tag}")
    dug, duv, dcw, dcb = _ffn_down_glu_bwd(dz, w_down, u_gc[0], u_gc[1], cw, name=f"ffn_down_glu_bwd_{tag}")
    d_wup = _mm_tn((dug, f), blocks=2, block=0, name=f"ffn_up_dwg_{tag}")
    d_wup = _mm_tn((duv, f), blocks=2, block=1, into=d_wup, name=f"ffn_up_dwv_{tag}")
    gated = gate_y is not None
    res = _mm_w_ep([dug, duv], w_up, _ep_norm_bwd(gated), [xr, dxo] + ([gate_y] if gated else []),
                   [gain, sc] + ([gate_g] if gated else []), [F32] + ([BF16] if gated else []),
                   [D] * (4 if gated else 3), name=f"ffn_up_dx_norm_bwd_{tag}")
    n_out = 2 if gated else 1
    return res[:n_out], res[n_out:], (d_wup, d_wdown, dcw, dcb)


def _split6(mod):
    return [mod[j * D:(j + 1) * D][None, :] for j in range(6)]


def _row(v):
    return v.reshape(1, -1)


def kernel(x, c, ctx, c_ctx, ada_w, ada_b, mix_norm, ffn_norm, even_w_in, even_q_gain, even_k_gain, even_conv_w, even_w_out, odd_pool_w, odd_pool_scale, ffn_w_up, ffn_conv_w, ffn_conv_b, ffn_w_down, loss_target, m_c_ctx, m_ada_w, m_ada_b, m_mix_norm, m_ffn_norm, m_even_w_in, m_even_q_gain, m_even_k_gain, m_even_conv_w, m_even_w_out, m_odd_pool_w, m_odd_pool_scale, m_ffn_w_up, m_ffn_conv_w, m_ffn_conv_b, m_ffn_w_down, v_c_ctx, v_ada_w, v_ada_b, v_mix_norm, v_ffn_norm, v_even_w_in, v_even_q_gain, v_even_k_gain, v_even_conv_w, v_even_w_out, v_odd_pool_w, v_odd_pool_scale, v_ffn_w_up, v_ffn_conv_w, v_ffn_conv_b, v_ffn_w_down):
    n = x.shape[1]
    lc = ctx.shape[1]
    me = 4 * lax.axis_index("x") + 2 * lax.axis_index("y") + lax.axis_index("c")
    xs, ctxs, tgt = x[0], ctx[0], loss_target[0]
    acols = ada_w.shape[2]

    small = jnp.concatenate([even_conv_w.reshape(-1), ffn_conv_w.reshape(-1), odd_pool_scale.reshape(-1)])
    nsmall = small.shape[0]
    small = jnp.pad(small, (0, (-nsmall) % 1024)).reshape(-1, 128)
    c_rows = jnp.pad(c, ((0, 7), (0, 0)))
    tr = lambda a: jnp.swapaxes(a, -1, -2)
    g_c, g_win, g_small = _all_gather([c_rows, tr(even_w_in[0]).astype(BF16), small], name="gather_first")
    w_in_t = g_win.reshape(-1, D)
    g_small = g_small.reshape(NDEV, -1)
    ecw = even_conv_w.shape[2]
    fcw = ffn_conv_w.shape[2]
    conv_w = g_small[:, :3 * ecw].reshape(NDEV, 3, ecw).transpose(1, 0, 2).reshape(3, CW)
    o1 = 3 * ecw
    fconv_w = g_small[:, o1:o1 + 6 * fcw].reshape(NDEV, 2, 3, fcw).transpose(1, 2, 0, 3).reshape(2, 3, DFF)
    o2 = o1 + 6 * fcw
    pool_scale = g_small[:, o2:o2 + D // NDEV].reshape(1, D)

    mraw = jnp.concatenate([g_c[:, 0, :], c_ctx[None, :], jnp.zeros((7, D), F32)], axis=0)
    my_bias = lax.dynamic_slice_in_dim(ada_b, me * acols, acols, axis=1)
    modp = jnp.stack([_mm(mraw, ada_w[l], silu_a=True, bias=my_bias[l:l + 1], name=f"ada_proj_{l}", tm=16, tn=256)
                      for l in range(2)])
    (g_mod,) = _all_gather([modp], name="gather_mod")
    mod_rows = g_mod.transpose(1, 2, 0, 3).reshape(2, 16, 6 * D)
    late_shards = [even_w_out[0].astype(BF16), odd_pool_w[0].astype(BF16), tr(ffn_w_up[0]).astype(BF16),
                   tr(ffn_w_up[1]).astype(BF16), ffn_w_down[0].astype(BF16), ffn_w_down[1].astype(BF16)]
    late_shards, mod_rows = lax.optimization_barrier((late_shards, mod_rows))
    h_weights = _exchange_start(late_shards, scatter=False, name="weights_start")
    mod_rows = mod_rows + h_weights["token"][0, 0]
    mod = lax.dynamic_index_in_dim(mod_rows, me, axis=1, keepdims=False)
    sh1, sc1, g1, sh2, sc2, g2 = _split6(mod[0])
    sh1b, sc1b, g1b, sh2b, sc2b, g2b = _split6(mod[1])
    csh1, csc1 = _split6(mod_rows[0, 8])[:2]
    mixn = [_row(mix_norm[l]) for l in range(2)]
    ffnn = [_row(ffn_norm[l]) for l in range(2)]
    qg, kg = _row(even_q_gain[0]), _row(even_k_gain[0])
    fcb = [_row(ffn_conv_b[l]) for l in range(2)]

    cs_t, sn_t = _rope_tables(n)
    a_lat = _norm_mod(xs, mixn[0], sc1, sh1, name="mix0_norm")
    a_ctx = _norm_mod(ctxs, mixn[0], csc1, csh1, name="mix0_norm_ctx")
    p_ctx = _mm(a_ctx, w_in_t[AW:AW + 4 * HD], tb=True, name="in_proj_ctx", tm=256, tn=512, tk=1024)
    kv_ctx = _qkv_prep(p_ctx, qg, kg, None, None, has_q=False, kv_col=0, kv_rows=lc + n, name="qkv_prep_ctx")
    p_lat, q_r, k_all, v_all, conv = _in_proj_qkv(a_lat, w_in_t, qg, kg, cs_t, sn_t, conv_w, kv_ctx, kv_row_off=lc,
                                                  name="in_proj_qkv")
    o_attn, lse = _attn_fwd(q_r, k_all, v_all, name="attn_fwd")
    g_wout, g_pool, g_up0, g_up1, g_down0, g_down1 = _exchange_wait(h_weights, o_attn, name="weights_wait")
    w_out = g_wout.reshape(D, D)
    pool_w = g_pool.transpose(1, 0, 2, 3).reshape(4, PG, PG)
    w_up_t = [g_up0.reshape(2 * DFF, D), g_up1.reshape(2 * DFF, D)]
    w_up = [w.T for w in w_up_t]
    w_down = [g_down0.reshape(DFF, D), g_down1.reshape(DFF, D)]
    y0, x1, f0 = _mm_w_ep([o_attn, conv], w_out, _ep_resid_norm, [xs], [g1, ffnn[0], sc2, sh2], [F32, F32, BF16], [],
                          tm=512, name="out_proj_norm")[:3]
    *u0, h0 = _ffn_up_glu(f0, w_up[0], fconv_w[0], fcb[0], name="ffn_up_glu_l0")
    z0 = _mm_w(h0, w_down[0], name="ffn_down_l0")

    x2, mixed = _pool_fwd(x1, z0, g2, mixn[1], sc1b, sh1b, pool_w, name="pool_fwd")
    x3, f1 = _norm_mod(x2, ffnn[1], sc2b, sh2b, y=mixed, g=g1b, ymul=pool_scale, name="ffn_norm_l1")
    *u1, h1 = _ffn_up_glu(f1, w_up[1], fconv_w[1], fcb[1], name="ffn_up_glu_l1")
    dx4, dz1, loss_part, dg2b = _mm_w_ep(h1, w_down[1], _ep_loss(D), [x3, tgt], [g2b], [F32, BF16], [128, D],
                                         tm=512, name="ffn_down_loss")

    (dx3,), (dsh2b, dsc2b, dffn1), (dup1, ddown1, dfcw1, dfcb1) = _ffn_bwd(
        dx4, dz1, x3, f1, u1, h1, ffnn[1], sc2b, w_up_t[1], fconv_w[1], w_down[1], "l1")
    dx2, dpool_w, dg1b, dpscale, dsh1b, dsc1b, dmix1, dz0, dg2 = _pool_bwd(
        dx3, mixed, x2, g1b, pool_scale, mixn[1], sc1b, sh1b, pool_w, z0, g2, name="pool_bwd")

    s_pool = dpool_w.astype(BF16).reshape(4, NDEV, PG // NDEV, PG).transpose(1, 0, 2, 3)
    h_g1 = _exchange_start([s_pool, dup1.reshape(NDEV, -1, D), ddown1.reshape(NDEV, DFF // NDEV, D)], scatter=True,
                           name="grads1_start")

    (dx1, dy0), (dsh2, dsc2, dffn0, dg1), (dup0, ddown0, dfcw0, dfcb0) = _ffn_bwd(
        dx2, dz0, x1, f0, u0, h0, ffnn[0], sc2, w_up_t[0], fconv_w[0] + h_g1["token"][0, 0], w_down[0], "l0",
        gate_y=y0, gate_g=g1)
    h_g0 = _exchange_start([dup0.reshape(NDEV, -1, D), ddown0.reshape(NDEV, DFF // NDEV, D)], scatter=True,
                           name="grads0_start")
    d_attn, dp_conv, dconv_w = _out_proj_dx_conv_bwd(dy0, w_out, p_lat, conv_w + h_g0["token"][0, 0],
                                                     name="out_proj_dx_conv_bwd")
    d_wout = _mm_tn((o_attn, dy0), blocks=2, block=0, name="out_proj_dw_attn")
    d_wout = _mm_tn((conv, dy0), blocks=2, block=1, into=d_wout, name="out_proj_dw_conv")
    dq_r, dk_all, dv_all = _attn_bwd(q_r, k_all, v_all, d_attn, o_attn, lse, name="attn_bwd")
    dp_qkv, dqg_l, dkg_l = _qkv_bwd(p_lat, dq_r, dk_all, dv_all, qg, kg, cs_t, sn_t, has_q=True, kv_col=1,
                                    kv_row_off=lc, name="qkv_bwd")
    dp_ctx, _zero_qg, dkg_c = _qkv_bwd(p_ctx, None, dk_all, dv_all, qg, kg, None, None, has_q=False, kv_col=0,
                                       kv_row_off=0, name="qkv_bwd_ctx")
    da_ctx = _mm(dp_ctx, w_in_t[:D], name="in_proj_dx_ctx", tm=256, tn=512, tk=1024)
    d_win_qkv = _mm_tn([(dp_qkv, a_lat), (dp_ctx, a_ctx)], name="in_proj_dw_qkv")
    d_win_conv = _mm_tn((dp_conv, a_lat), name="in_proj_dw_conv")
    d_win_t = jnp.concatenate([d_win_qkv, d_win_conv], axis=0)
    grad_x, dsh1, dsc1, dmix0 = _mm_w_ep([dp_qkv, dp_conv], w_in_t, _ep_norm_bwd(False), [xs, dx1], [mixn[0], sc1],
                                         [F32], [D] * 3, tm=512, name="in_proj_dx_norm_bwd")
    _dctx, dcsh1, dcsc1, dmix0c = _norm_mod_bwd(da_ctx, ctxs, mixn[0], csc1, name="mix0_norm_bwd_ctx")

    z1k = jnp.zeros((1, D), F32)
    pack = jnp.concatenate(
        [v.reshape(-1) for v in (dsh1, dsc1, dg1, dsh2, dsc2, dg2, dsh1b, dsc1b, dg1b, dsh2b, dsc2b, dg2b,
                                 dcsh1, dcsc1, z1k, z1k, z1k, z1k,
                                 dmix0, dmix1, dmix0c, z1k, dffn0, dffn1, dqg_l, dkg_l + dkg_c,
                                 dfcb0, dfcb1, dconv_w, dfcw0, dfcw1, dpscale, loss_part[:, 0:1])])
    npack = pack.shape[0]
    pack = jnp.pad(pack, (0, (-npack) % 1024)).reshape(-1, 128)
    (g_pack,) = _all_gather([pack], name="gather_small_grads")
    gp = g_pack.reshape(NDEV, -1)
    off = [0]

    def take(size):
        seg = gp[:, off[0]:off[0] + size]
        off[0] += size
        return seg

    dmod_all = take(12 * D).reshape(NDEV, 2, 6 * D)
    dmodc_all = take(6 * D).reshape(NDEV, 1, 6 * D)
    dmix_all = take(4 * D).reshape(NDEV, 2, 2, D)
    dffn_all = take(2 * D).reshape(NDEV, 2, D)
    dqg_all = take(HD).reshape(NDEV, 1, HD)
    dkg_all = take(HD).reshape(NDEV, 1, HD)
    dfcb_all = take(2 * DFF).reshape(NDEV, 2, DFF)
    dconvw_all = take(3 * CW).reshape(NDEV, 3, CW)
    dfcw_all = take(6 * DFF).reshape(NDEV, 2, 3, DFF)
    dpscale_all = take(D).reshape(NDEV, D)
    loss_all = take(1)
    loss = loss_all[0, 0]
    for dev in range(1, NDEV):
        loss = loss + loss_all[dev, 0]

    dmodc_sum = dmodc_all[0]
    for dev in range(1, NDEV):
        dmodc_sum = dmodc_sum + dmodc_all[dev]
    my_cols = lambda a: lax.dynamic_slice_in_dim(a, me * acols, acols, axis=a.ndim - 1)
    rows0 = jnp.concatenate([my_cols(dmod_all[:, 0]), my_cols(dmodc_sum), jnp.zeros((7, acols), F32)], axis=0)
    rows1 = jnp.concatenate([my_cols(dmod_all[:, 1]), jnp.zeros((8, acols), F32)], axis=0)
    d_ada = jnp.stack([_mm(mraw, rows, ta=True, silu_a=True, name=f"ada_dw_{l}", tm=512, tn=256, tk=16)
                       for l, rows in enumerate((rows0, rows1))])
    dscc_part = _mm(rows0, ada_w[0], tb=True, name="ada_dcctx", tm=16, tn=512, tk=256)
    (g_dscc,) = _all_gather([dscc_part[8:16]], name="gather_dcctx")

    attn_shards = [d_win_t.reshape(NDEV, -1, D), d_wout.reshape(NDEV, D // NDEV, D)]
    attn_shards, g_dscc = lax.optimization_barrier((attn_shards, g_dscc))
    h_ga = _exchange_start(attn_shards, scatter=True, name="grads_attn_start")
    dmod_all = dmod_all + h_ga["token"][0, 0]

    outs = {}

    def put(nm, res):
        outs["grad_" + nm], outs["delta_" + nm], outs["new_m_" + nm], outs["new_v_" + nm] = res

    dmodc_pad = jnp.concatenate([dmodc_all, jnp.zeros_like(dmodc_all)], axis=1)
    put("ada_b", _adamw_nd(jnp.concatenate([dmod_all, dmodc_pad], axis=0), ada_b, m_ada_b, v_ada_b, name="adam_ada_b"))
    put("mix_norm", _adamw_nd(jnp.concatenate([dmix_all[:, 0], dmix_all[:, 1]], axis=0), mix_norm, m_mix_norm,
                              v_mix_norm, name="adam_mix_norm"))
    put("ffn_norm", _adamw_nd(dffn_all, ffn_norm, m_ffn_norm, v_ffn_norm, name="adam_ffn_norm"))
    put("even_q_gain", _adamw_nd(dqg_all, even_q_gain, m_even_q_gain, v_even_q_gain, name="adam_q_gain"))
    put("even_k_gain", _adamw_nd(dkg_all, even_k_gain, m_even_k_gain, v_even_k_gain, name="adam_k_gain"))
    put("ffn_conv_b", _adamw_nd(dfcb_all, ffn_conv_b, m_ffn_conv_b, v_ffn_conv_b, name="adam_ffn_conv_b"))
    my_convw = lax.dynamic_slice_in_dim(dconvw_all, me * ecw, ecw, axis=2)[:, None]
    put("even_conv_w", _adamw_nd(my_convw, even_conv_w, m_even_conv_w, v_even_conv_w, name="adam_even_conv_w"))
    my_fcw = lax.dynamic_slice_in_dim(dfcw_all, me * fcw, fcw, axis=3)
    put("ffn_conv_w", _adamw_nd(my_fcw, ffn_conv_w, m_ffn_conv_w, v_ffn_conv_w, name="adam_ffn_conv_w"))
    my_ps = lax.dynamic_slice_in_dim(dpscale_all, me * (D // NDEV), D // NDEV, axis=1)[:, None]
    put("odd_pool_scale", _adamw_nd(my_ps, odd_pool_scale, m_odd_pool_scale, v_odd_pool_scale, name="adam_pool_scale"))

    put("ada_w", _adamw_nd(d_ada[None], ada_w, m_ada_w, v_ada_w, name="adam_ada_w"))
    put("c_ctx", _adamw_nd(g_dscc[:, 0:1, :].reshape(NDEV, D), c_ctx, m_c_ctx, v_c_ctx, name="adam_c_ctx",
                           silu_grad_of=c_ctx))

    r_pool, r_up1, r_down1 = _exchange_wait(h_g1, outs["grad_ada_b"], name="grads1_wait")
    r_up0, r_down0 = _exchange_wait(h_g0, outs["grad_mix_norm"], name="grads0_wait")
    r_win, r_wout = _exchange_wait(h_ga, outs["grad_c_ctx"], name="grads_attn_wait")
    put("even_w_in", [tr(a) for a in _adamw_nd(r_win[:, None], tr(even_w_in), tr(m_even_w_in), tr(v_even_w_in),
                                               name="adam_w_in")])
    put("even_w_out", _adamw_nd(r_wout[:, None], even_w_out, m_even_w_out, v_even_w_out, name="adam_w_out"))
    put("odd_pool_w", _adamw_nd(r_pool[:, None], odd_pool_w, m_odd_pool_w, v_odd_pool_w, name="adam_pool_w"))
    put("ffn_w_up", [tr(a) for a in _adamw_nd([r_up0, r_up1], tr(ffn_w_up), tr(m_ffn_w_up), tr(v_ffn_w_up),
                                              name="adam_w_up")])
    put("ffn_w_down", _adamw_nd([r_down0, r_down1], ffn_w_down, m_ffn_w_down, v_ffn_w_down, name="adam_w_down"))

    names = ["c_ctx", "ada_w", "ada_b", "mix_norm", "ffn_norm", "even_w_in", "even_q_gain", "even_k_gain",
             "even_conv_w", "even_w_out", "odd_pool_w", "odd_pool_scale", "ffn_w_up", "ffn_conv_w", "ffn_conv_b",
             "ffn_w_down"]
    result = [loss, grad_x[None]]
    for kind in ("grad_", "delta_", "new_m_", "new_v_"):
        result += [outs[kind + nm] for nm in names]
    return tuple(result)
```

```python
import functools
import math

import jax
import jax.numpy as jnp
from jax import lax
from jax.experimental import pallas as pl
from jax.experimental.pallas import tpu as pltpu

F32 = jnp.float32
BF16 = jnp.bfloat16

D = 1024
HD = 128
NQ = 4
NKV = 2
AW = NQ * HD
CW = D - AW
DFF = 2816
GRID_W = 64
ROPE_THETA = 10000.0
POOL_WINDOWS = (2, 4, 8, 16)
PG = D // 4
EPS = 1e-6
NDEV = 8
HALO = 8
MESH = pl.DeviceIdType.MESH

ADAM_LR = 0.001
ADAM_B1 = 0.9
ADAM_B2 = 0.999
ADAM_EPS = 1e-08
ADAM_WD = 0.01
ADAM_STEP = 10


def _pick(dim, prefs):
    for p in prefs:
        if dim % p == 0:
            return p
    return dim


def _params(*sem):
    return pltpu.CompilerParams(dimension_semantics=sem)


_NT = (((1,), (1,)), ((), ()))
_TN = (((0,), (0,)), ((), ()))
_SCALE = HD ** -0.5
_QSCALE = _SCALE * math.log2(math.e)
_LN2 = math.log(2.0)


def _mm(a_list, b, *, name, ta=False, tb=False, out_dtype=F32, silu_a=False, bias=None, tm=None, tn=None, tk=None):
    if not isinstance(a_list, (list, tuple)):
        a_list = [a_list]
    na = len(a_list)
    assert not (ta and na > 1)
    if ta:
        kdim, m = a_list[0].shape
        ks = [kdim]
    else:
        m = a_list[0].shape[0]
        ks = [a.shape[1] for a in a_list]
        kdim = sum(ks)
    n = b.shape[0] if tb else b.shape[1]
    assert (b.shape[1] if tb else b.shape[0]) == kdim
    kunit = math.gcd(*ks) if na > 1 else kdim
    tm = min(tm, m) if tm else _pick(m, (512, 256, 128, 64, 32, 16, 8))
    tn = min(tn, n) if tn else _pick(n, (512, 256, 128))
    tk = min(tk, kunit) if tk else _pick(kunit, (1024, 768, 512, 256, 128))
    assert m % tm == 0 and n % tn == 0 and all(k % tk == 0 for k in ks)
    nks = [k // tk for k in ks]
    starts = [sum(nks[:i]) for i in range(na)]
    nk = sum(nks)
    has_bias = bias is not None

    def body(*refs):
        a_refs = refs[:na]
        b_ref = refs[na]
        bias_ref = refs[na + 1] if has_bias else None
        o_ref = refs[na + 1 + has_bias]
        acc = refs[-1]
        k = pl.program_id(2)

        @pl.when(k == 0)
        def _():
            acc[...] = jnp.zeros_like(acc)

        bv = b_ref[...].astype(BF16)
        dn = (((0 if ta else 1,), (1 if tb else 0,)), ((), ()))
        for idx in range(na):
            def step(idx=idx):
                av = a_refs[idx][...]
                if silu_a:
                    av = av * jax.nn.sigmoid(av)
                acc[...] += lax.dot_general(av.astype(BF16), bv, dn, preferred_element_type=F32)
            if na == 1:
                step()
            else:
                pl.when((k >= starts[idx]) & (k < starts[idx] + nks[idx]))(step)

        @pl.when(k == nk - 1)
        def _():
            r = acc[...]
            if has_bias:
                r = r + bias_ref[...]
            o_ref[...] = r.astype(o_ref.dtype)

    in_specs = []
    for idx in range(na):
        if ta:
            in_specs.append(pl.BlockSpec((tk, tm), lambda i, j, k: (k, i)))
        else:
            lo, cnt = starts[idx], nks[idx]
            in_specs.append(pl.BlockSpec((tm, tk), lambda i, j, k, lo=lo, cnt=cnt: (i, jnp.clip(k - lo, 0, cnt - 1))))
    if tb:
        in_specs.append(pl.BlockSpec((tn, tk), lambda i, j, k: (j, k)))
    else:
        in_specs.append(pl.BlockSpec((tk, tn), lambda i, j, k: (k, j)))
    args = list(a_list) + [b]
    if has_bias:
        in_specs.append(pl.BlockSpec((1, tn), lambda i, j, k: (0, j)))
        args.append(bias)
    return pl.pallas_call(
        body, grid=(m // tm, n // tn, nk), in_specs=in_specs,
        out_specs=pl.BlockSpec((tm, tn), lambda i, j, k: (i, j)),
        out_shape=jax.ShapeDtypeStruct((m, n), out_dtype),
        scratch_shapes=[pltpu.VMEM((tm, tn), F32)], name=name,
        compiler_params=_params("parallel", "parallel", "arbitrary"))(*args)


def _mm_w(a_list, w, *, name, tb=False, tm=256, out_dtype=F32):
    if not isinstance(a_list, (list, tuple)):
        a_list = [a_list]
    na = len(a_list)
    m = a_list[0].shape[0]
    ks = [a.shape[1] for a in a_list]
    offs = [sum(ks[:i]) for i in range(na)]
    n = w.shape[0] if tb else w.shape[1]
    assert (w.shape[1] if tb else w.shape[0]) == sum(ks)
    tm = min(tm, m)
    assert m % tm == 0

    def body(*refs):
        a_refs, w_ref, o_ref = refs[:na], refs[na], refs[na + 1]
        acc = None
        for idx in range(na):
            av = a_refs[idx][...].astype(BF16)
            if tb:
                part = lax.dot_general(av, w_ref[:, offs[idx]:offs[idx] + ks[idx]], _NT, preferred_element_type=F32)
            else:
                part = jnp.dot(av, w_ref[offs[idx]:offs[idx] + ks[idx], :], preferred_element_type=F32)
            acc = part if acc is None else acc + part
        o_ref[...] = acc.astype(o_ref.dtype)

    in_specs = [pl.BlockSpec((tm, k), lambda i: (i, 0)) for k in ks] + [pl.BlockSpec(w.shape, lambda i: (0, 0))]
    return pl.pallas_call(
        body, grid=(m // tm,), in_specs=in_specs, out_specs=pl.BlockSpec((tm, n), lambda i: (i, 0)),
        out_shape=jax.ShapeDtypeStruct((m, n), out_dtype), name=name, compiler_params=_params("parallel"))(*a_list, w)


def _mm_w_ep(a_list, w, epilogue, row_in, vec_in, out_dtypes, sum_widths, *, name, tb=False, tm=256, sub=256):
    if not isinstance(a_list, (list, tuple)):
        a_list = [a_list]
    na, nr, nv, no, ns = len(a_list), len(row_in), len(vec_in), len(out_dtypes), len(sum_widths)
    m = a_list[0].shape[0]
    ks = [a.shape[1] for a in a_list]
    offs = [sum(ks[:i]) for i in range(na)]
    n = w.shape[0] if tb else w.shape[1]
    assert (w.shape[1] if tb else w.shape[0]) == sum(ks)
    tm = min(tm, m)
    sub = min(sub, tm)
    assert m % tm == 0 and tm % sub == 0

    def body(*refs):
        a_refs, w_ref = refs[:na], refs[na]
        row_refs = refs[na + 1:na + 1 + nr]
        vec_refs = refs[na + 1 + nr:na + 1 + nr + nv]
        out_refs = refs[na + 1 + nr + nv:na + 1 + nr + nv + no]
        sum_refs = refs[na + 1 + nr + nv + no:]

        @pl.when(pl.program_id(0) == 0)
        def _():
            for s_ref in sum_refs:
                s_ref[...] = jnp.zeros_like(s_ref)

        vecs = [v[...] for v in vec_refs]
        for r0 in range(0, tm, sub):
            acc = None
            for idx in range(na):
                av = a_refs[idx][r0:r0 + sub, :].astype(BF16)
                if tb:
                    part = lax.dot_general(av, w_ref[:, offs[idx]:offs[idx] + ks[idx]], _NT, preferred_element_type=F32)
                else:
                    part = jnp.dot(av, w_ref[offs[idx]:offs[idx] + ks[idx], :], preferred_element_type=F32)
                acc = part if acc is None else acc + part
            outs, sums = epilogue(acc, [r[r0:r0 + sub, :] for r in row_refs], vecs)
            for o_ref, o in zip(out_refs, outs):
                o_ref[r0:r0 + sub, :] = o.astype(o_ref.dtype)
            for s_ref, s in zip(sum_refs, sums):
                s_ref[...] += s

    row = pl.BlockSpec((tm, n), lambda i: (i, 0))
    in_specs = ([pl.BlockSpec((tm, k), lambda i: (i, 0)) for k in ks] + [pl.BlockSpec(w.shape, lambda i: (0, 0))]
                + [row] * nr + [_vec(n)] * nv)
    return pl.pallas_call(
        body, grid=(m // tm,), in_specs=in_specs, out_specs=[row] * no + [_vec(sw) for sw in sum_widths],
        out_shape=[jax.ShapeDtypeStruct((m, n), dt) for dt in out_dtypes]
        + [jax.ShapeDtypeStruct((1, sw), F32) for sw in sum_widths],
        name=name, compiler_params=_params("arbitrary" if ns else "parallel"))(*a_list, w, *row_in, *vec_in)


def _ep_norm_bwd(has_gate):
    def ep(dav, rows, vecs):
        xv = rows[0]
        gain, scv = vecs[0], vecs[1]
        r = lax.rsqrt(jnp.mean(xv * xv, axis=-1, keepdims=True) + EPS)
        xh = xv * r
        nrm = xh * gain
        dn = dav * (1.0 + scv)
        dxh = dn * gain
        dx = r * (dxh - xh * jnp.mean(dxh * xh, axis=-1, keepdims=True)) + rows[1]
        outs, sums = [dx], [_colsum(dav), _colsum(dav * nrm), _colsum(dn * xh)]
        if has_gate:
            outs.append(dx * vecs[2])
            sums.append(_colsum(dx * rows[2]))
        return outs, sums
    return ep


def _ep_loss(d):
    def ep(zv, rows, vecs):
        xv, tv = rows
        gv = vecs[0]
        diff = (xv + gv * zv) - tv
        dx = diff * (1.0 / d)
        part = 0.5 * jnp.sum(jnp.mean(diff * diff, axis=-1, keepdims=True), axis=0, keepdims=True)
        return [dx, dx * gv], [jnp.broadcast_to(part, (1, 128)), _colsum(dx * zv)]
    return ep


def _ep_resid_norm(yv, rows, vecs):
    g, gain, scv, shv = vecs
    xv = rows[0] + g * yv
    r = lax.rsqrt(jnp.mean(xv * xv, axis=-1, keepdims=True) + EPS)
    return [yv, xv, ((xv * r) * gain) * (1.0 + scv) + shv], []


def _mm_tn(pairs, *, name, tk=1024, out_dtype=BF16, blocks=1, block=0, into=None):
    if not isinstance(pairs, list):
        pairs = [pairs]
    m, n = pairs[0][0].shape[1], pairs[0][1].shape[1]
    tks = [min(tk, a.shape[0]) for a, _ in pairs]
    nks = [a.shape[0] // t for (a, _), t in zip(pairs, tks)]
    assert all(a.shape[0] == b.shape[0] and a.shape[0] % t == 0 for (a, b), t in zip(pairs, tks))
    starts = [sum(nks[:i]) for i in range(len(pairs))]
    nk = sum(nks)

    def body(*refs):
        o_ref, acc = refs[-2], refs[-1]
        k = pl.program_id(0)

        @pl.when(k == 0)
        def _():
            acc[...] = jnp.zeros_like(acc)

        for idx in range(len(pairs)):
            a_ref, b_ref = refs[2 * idx], refs[2 * idx + 1]

            def step(a_ref=a_ref, b_ref=b_ref):
                acc[...] += lax.dot_general(a_ref[...].astype(BF16), b_ref[...].astype(BF16), _TN,
                                            preferred_element_type=F32)

            if len(pairs) == 1:
                step()
            else:
                pl.when((k >= starts[idx]) & (k < starts[idx] + nks[idx]))(step)

        @pl.when(k == nk - 1)
        def _():
            o_ref[...] = acc[...].astype(o_ref.dtype)

    in_specs, args = [], []
    for (a, b), t, lo, cnt in zip(pairs, tks, starts, nks):
        idx_map = lambda k, lo=lo, cnt=cnt: (jnp.clip(k - lo, 0, cnt - 1), 0)
        in_specs += [pl.BlockSpec((t, m), idx_map), pl.BlockSpec((t, n), idx_map)]
        args += [a, b]
    aliases = {}
    if into is not None:
        aliases = {len(args): 0}
        in_specs.append(pl.BlockSpec(memory_space=pl.ANY))
        args.append(into)
    return pl.pallas_call(
        body, grid=(nk,), in_specs=in_specs, out_specs=pl.BlockSpec((m, n), lambda k: (block, 0)),
        out_shape=jax.ShapeDtypeStruct((m * blocks, n), out_dtype), scratch_shapes=[pltpu.VMEM((m, n), F32)],
        input_output_aliases=aliases, name=name, compiler_params=_params("arbitrary"))(*args)


def _vec(d, col=None):
    if col is None:
        return pl.BlockSpec((1, d), lambda i, *_: (0, 0))
    return pl.BlockSpec((1, d), col)


def _halo_specs(tm, width, nrows, colblk=0, row_off=0):
    r = tm // HALO
    off = row_off // HALO
    last = nrows // HALO - 1
    prev = pl.BlockSpec((HALO, width), lambda i, *_: (off + jnp.maximum(i * r - 1, 0), colblk))
    nxt = pl.BlockSpec((HALO, width), lambda i, *_: (off + jnp.minimum((i + 1) * r, last), colblk))
    return prev, nxt


def _ext(prev_ref, main_ref, next_ref, i, ni):
    p = jnp.where(i > 0, prev_ref[...], 0.0)
    n = jnp.where(i < ni - 1, next_ref[...], 0.0)
    return jnp.concatenate([p, main_ref[...], n], axis=0)


def _sh(ext, k, tm):
    if k == 0:
        return ext[HALO:HALO + tm]
    rows = ext.shape[0]
    return pltpu.roll(ext, (-k) % rows, axis=0)[HALO:HALO + tm]


def _roll_rows(v, k):
    rows = v.shape[0]
    return pltpu.roll(v, (-k) % rows, axis=0) if k % rows else v


def _conv3(ext, w_ref, tm):
    return _sh(ext, -1, tm) * w_ref[0:1, :] + _sh(ext, 0, tm) * w_ref[1:2, :] + _sh(ext, 1, tm) * w_ref[2:3, :]


def _colsum(v):
    return jnp.sum(v, axis=0, keepdims=True)


def _acc_out(ref, i, val):
    @pl.when(i == 0)
    def _():
        ref[...] = jnp.zeros_like(ref)

    ref[...] += val


def _sigmoid(v):
    return jax.nn.sigmoid(v)


def _norm_mod(x, gain, sc, sh, *, name, y=None, g=None, ymul=None, tm=512):
    n, d = x.shape
    tm = min(tm, n)
    has_res = y is not None
    has_mul = ymul is not None

    def body(*refs):
        it = iter(refs)
        x_ref = next(it)
        y_ref = next(it) if has_res else None
        g_ref = next(it) if has_res else None
        m_ref = next(it) if has_mul else None
        gain_ref, sc_ref, sh_ref = next(it), next(it), next(it)
        xo_ref = next(it) if has_res else None
        a_ref = next(it)
        xv = x_ref[...]
        if has_res:
            yv = y_ref[...]
            if has_mul:
                yv = yv * m_ref[...]
            xv = xv + g_ref[...] * yv
            xo_ref[...] = xv
        r = lax.rsqrt(jnp.mean(xv * xv, axis=-1, keepdims=True) + EPS)
        nrm = (xv * r) * gain_ref[...]
        a_ref[...] = (nrm * (1.0 + sc_ref[...]) + sh_ref[...]).astype(BF16)

    row = pl.BlockSpec((tm, d), lambda i: (i, 0))
    in_specs, args = [row], [x]
    if has_res:
        in_specs += [row, _vec(d)]
        args += [y, g]
    if has_mul:
        in_specs.append(_vec(d))
        args.append(ymul)
    in_specs += [_vec(d)] * 3
    args += [gain, sc, sh]
    out_specs, out_shape = [], []
    if has_res:
        out_specs.append(row)
        out_shape.append(jax.ShapeDtypeStruct((n, d), F32))
    out_specs.append(row)
    out_shape.append(jax.ShapeDtypeStruct((n, d), BF16))
    res = pl.pallas_call(body, grid=(n // tm,), in_specs=in_specs, out_specs=out_specs, out_shape=out_shape,
                         name=name, compiler_params=_params("parallel"))(*args)
    return res if has_res else res[0]


def _norm_mod_bwd(da, x, gain, sc, *, name, dres=None, gate_y=None, gate_g=None, tm=512):
    n, d = x.shape
    tm = min(tm, n)
    has_res = dres is not None
    has_gate = gate_y is not None

    def body(*refs):
        it = iter(refs)
        da_ref, x_ref = next(it), next(it)
        r_ref = next(it) if has_res else None
        y_ref = next(it) if has_gate else None
        g_ref = next(it) if has_gate else None
        gain_ref, sc_ref = next(it), next(it)
        dx_ref, dsh_ref, dsc_ref, dgn_ref = next(it), next(it), next(it), next(it)
        dy_ref = next(it) if has_gate else None
        dg_ref = next(it) if has_gate else None
        i = pl.program_id(0)
        xv = x_ref[...]
        dav = da_ref[...]
        r = lax.rsqrt(jnp.mean(xv * xv, axis=-1, keepdims=True) + EPS)
        xh = xv * r
        nrm = xh * gain_ref[...]
        dn = dav * (1.0 + sc_ref[...])
        dxh = dn * gain_ref[...]
        dx = r * (dxh - xh * jnp.mean(dxh * xh, axis=-1, keepdims=True))
        if has_res:
            dx = dx + r_ref[...]
        dx_ref[...] = dx
        _acc_out(dsh_ref, i, _colsum(dav))
        _acc_out(dsc_ref, i, _colsum(dav * nrm))
        _acc_out(dgn_ref, i, _colsum(dn * xh))
        if has_gate:
            dy_ref[...] = (dx * g_ref[...]).astype(BF16)
            _acc_out(dg_ref, i, _colsum(dx * y_ref[...]))

    row = pl.BlockSpec((tm, d), lambda i: (i, 0))
    in_specs, args = [row, row], [da, x]
    if has_res:
        in_specs.append(row)
        args.append(dres)
    if has_gate:
        in_specs += [row, _vec(d)]
        args += [gate_y, gate_g]
    in_specs += [_vec(d)] * 2
    args += [gain, sc]
    vec_shape = jax.ShapeDtypeStruct((1, d), F32)
    out_specs = [row, _vec(d), _vec(d), _vec(d)]
    out_shape = [jax.ShapeDtypeStruct((n, d), F32), vec_shape, vec_shape, vec_shape]
    if has_gate:
        out_specs += [row, _vec(d)]
        out_shape += [jax.ShapeDtypeStruct((n, d), BF16), vec_shape]
    return pl.pallas_call(
        body, grid=(n // tm,), in_specs=in_specs, out_specs=out_specs, out_shape=out_shape,
        name=name, compiler_params=_params("arbitrary"))(*args)


def _ffn_up_glu(f, w_up, cw, cb, *, name, tm=256, tc=256):
    n, d = f.shape
    tm = min(tm, n)
    ni = n // tm
    nc = DFF // tc
    halo = 16
    rows = tm + 2 * halo
    r = tm // halo
    last = n // halo - 1

    def body(f_ref, fp_ref, fn_ref, w_ref, cw_ref, cb_ref, u_ref, gc_ref, h_ref):
        i = pl.program_id(0)
        a = f_ref[...]
        aext = jnp.concatenate([jnp.where(i > 0, fp_ref[...], jnp.zeros_like(fp_ref[...])), a,
                                jnp.where(i < ni - 1, fn_ref[...], jnp.zeros_like(fn_ref[...]))], axis=0)
        for j in range(nc):
            cols = slice(j * tc, (j + 1) * tc)
            vcols = slice(DFF + j * tc, DFF + (j + 1) * tc)
            gext = jnp.dot(aext, w_ref[:, cols], preferred_element_type=F32)
            val = jnp.dot(a, w_ref[:, vcols], preferred_element_type=F32)
            gate = gext[halo:halo + tm]
            gc = (pltpu.roll(gext, 1, axis=0)[halo:halo + tm] * cw_ref[0:1, cols] + gate * cw_ref[1:2, cols]
                  + pltpu.roll(gext, rows - 1, axis=0)[halo:halo + tm] * cw_ref[2:3, cols]) + cb_ref[:, cols]
            u_ref[:, cols] = gate
            u_ref[:, vcols] = val
            gc_ref[:, cols] = gc
            h_ref[:, cols] = (gc * _sigmoid(gc) * val).astype(BF16)

    return pl.pallas_call(
        body, grid=(ni,),
        in_specs=[pl.BlockSpec((tm, d), lambda i: (i, 0)),
                  pl.BlockSpec((halo, d), lambda i: (jnp.maximum(i * r - 1, 0), 0)),
                  pl.BlockSpec((halo, d), lambda i: (jnp.minimum((i + 1) * r, last), 0)),
                  pl.BlockSpec(w_up.shape, lambda i: (0, 0)), pl.BlockSpec((3, DFF), lambda i: (0, 0)),
                  pl.BlockSpec((1, DFF), lambda i: (0, 0))],
        out_specs=[pl.BlockSpec((tm, 2 * DFF), lambda i: (i, 0)), pl.BlockSpec((tm, DFF), lambda i: (i, 0)),
                   pl.BlockSpec((tm, DFF), lambda i: (i, 0))],
        out_shape=[jax.ShapeDtypeStruct((n, 2 * DFF), F32), jax.ShapeDtypeStruct((n, DFF), F32),
                   jax.ShapeDtypeStruct((n, DFF), BF16)], name=name,
        compiler_params=_params("parallel"))(f, f, f, w_up, cw, cb)


def _ffn_down_glu_bwd(dz, w_down, u, gc, cw, *, name, tm=256, tc=256):
    n, d = dz.shape
    tm = min(tm, n)
    ni = n // tm
    nc = DFF // tc
    rows = tm + 2 * HALO

    def body(z_ref, zp_ref, zn_ref, w_ref, u_ref, vp_ref, vn_ref, c_ref, cp_ref, cn_ref, cw_ref,
             dg_ref, dv_ref, dcw_ref, dcb_ref):
        i = pl.program_id(0)

        @pl.when(i == 0)
        def _():
            dcw_ref[...] = jnp.zeros_like(dcw_ref)
            dcb_ref[...] = jnp.zeros_like(dcb_ref)

        zext = jnp.concatenate([jnp.where(i > 0, zp_ref[...], jnp.zeros_like(zp_ref[...])), z_ref[...],
                                jnp.where(i < ni - 1, zn_ref[...], jnp.zeros_like(zn_ref[...]))], axis=0)
        for j in range(nc):
            cols = slice(j * tc, (j + 1) * tc)
            vcols = slice(DFF + j * tc, DFF + (j + 1) * tc)
            dh = lax.dot_general(zext, w_ref[cols, :], _NT, preferred_element_type=F32)[HALO:HALO + rows]
            gcx = jnp.concatenate([cp_ref[:, cols], c_ref[:, cols], cn_ref[:, cols]], axis=0)
            vext = jnp.concatenate([vp_ref[:, cols], u_ref[:, vcols], vn_ref[:, cols]], axis=0)
            sg = _sigmoid(gcx)
            dgc = dh * vext * (sg * (1.0 + gcx * (1.0 - sg)))
            dv_ref[:, cols] = (dh[HALO:HALO + tm] * (gcx[HALO:HALO + tm] * sg[HALO:HALO + tm])).astype(BF16)
            d_next = pltpu.roll(dgc, rows - 1, axis=0)[HALO:HALO + tm]
            d_prev = pltpu.roll(dgc, 1, axis=0)[HALO:HALO + tm]
            d_here = dgc[HALO:HALO + tm]
            dg_ref[:, cols] = (d_next * cw_ref[0:1, cols] + d_here * cw_ref[1:2, cols]
                               + d_prev * cw_ref[2:3, cols]).astype(BF16)
            gate = u_ref[:, cols]
            dcw_ref[:, cols] += jnp.concatenate([_colsum(d_next * gate), _colsum(d_here * gate),
                                                 _colsum(d_prev * gate)], axis=0)
            dcb_ref[:, cols] += _colsum(d_here)

    def trio(width, halo, tile_width=None, colblk=0):
        r, last = tm // halo, n // halo - 1
        return [pl.BlockSpec((tm, tile_width or width), lambda i: (i, 0)),
                pl.BlockSpec((halo, width), lambda i: (jnp.maximum(i * r - 1, 0), colblk)),
                pl.BlockSpec((halo, width), lambda i: (jnp.minimum((i + 1) * r, last), colblk))]

    whole = lambda shape: pl.BlockSpec(shape, lambda i: (0, 0))
    return pl.pallas_call(
        body, grid=(ni,),
        in_specs=(trio(d, 16) + [whole(w_down.shape)] + trio(DFF, HALO, tile_width=2 * DFF, colblk=1)
                  + trio(DFF, HALO) + [whole((3, DFF))]),
        out_specs=[pl.BlockSpec((tm, DFF), lambda i: (i, 0)), pl.BlockSpec((tm, DFF), lambda i: (i, 0)),
                   whole((3, DFF)), whole((1, DFF))],
        out_shape=[jax.ShapeDtypeStruct((n, DFF), BF16), jax.ShapeDtypeStruct((n, DFF), BF16),
                   jax.ShapeDtypeStruct((3, DFF), F32), jax.ShapeDtypeStruct((1, DFF), F32)],
        name=name, compiler_params=_params("arbitrary"))(dz, dz, dz, w_down, u, u, u, gc, gc, gc, cw)


def _rope_tables(n):
    rows = n // GRID_W
    axis_dim = HD // 2
    inv_freq = jnp.power(ROPE_THETA, -jnp.arange(0, axis_dim, 2, dtype=F32) / axis_dim)
    ar = jnp.arange(rows, dtype=F32)[:, None] * inv_freq
    ac = jnp.arange(GRID_W, dtype=F32)[:, None] * inv_freq
    by_row = lambda a: jnp.repeat(a, GRID_W, axis=0)
    by_col = lambda a: jnp.tile(a, (rows, 1))
    cr, sr, cc, sc = by_row(jnp.cos(ar)), by_row(jnp.sin(ar)), by_col(jnp.cos(ac)), by_col(jnp.sin(ac))
    return jnp.concatenate([cr, cr, cc, cc], axis=1), jnp.concatenate([-sr, sr, -sc, sc], axis=1)


def _partner(v):
    lane = lax.broadcasted_iota(jnp.int32, v.shape, 1)
    return jnp.where((lane % 64) < 32, pltpu.roll(v, HD - 32, axis=1), pltpu.roll(v, 32, axis=1))


def _qkv_prep(p, q_gain, k_gain, cs, sn, *, name, has_q, kv_col, kv_rows=None, kv_row_off=0, kv_into=None, tm=256):
    n = p.shape[0]
    rope = cs is not None
    kv_rows = kv_rows or n
    rb = kv_row_off // tm

    def body(*refs):
        it = iter(refs)
        q_ref = next(it) if has_q else None
        kv_ref = next(it)
        qg_ref, kg_ref = next(it), next(it)
        cs_ref = next(it) if rope else None
        sn_ref = next(it) if rope else None
        if kv_into is not None:
            next(it), next(it)
        qo_ref = next(it) if has_q else None
        ko_ref, vo_ref = next(it), next(it)

        def norm_rope(xh, gain, mul=None):
            r = lax.rsqrt(jnp.mean(xh * xh, axis=-1, keepdims=True) + EPS)
            xn = (xh * r) * gain
            if rope:
                xn = xn * cs_ref[...] + _partner(xn) * sn_ref[...]
            if mul is not None:
                xn = xn * mul
            return xn.astype(BF16)

        if has_q:
            for h in range(NQ):
                qo_ref[h] = norm_rope(q_ref[:, h * HD:(h + 1) * HD], qg_ref[...], _QSCALE)
        for h in range(NKV):
            ko_ref[h] = norm_rope(kv_ref[:, h * HD:(h + 1) * HD], kg_ref[...])
            vo_ref[h] = kv_ref[:, (NKV + h) * HD:(NKV + h + 1) * HD].astype(BF16)

    in_specs, args = [], []
    if has_q:
        in_specs.append(pl.BlockSpec((tm, AW), lambda i: (i, 0)))
        args.append(p)
    in_specs += [pl.BlockSpec((tm, 2 * NKV * HD), lambda i: (i, kv_col)), _vec(HD), _vec(HD)]
    args += [p, q_gain, k_gain]
    if rope:
        in_specs += [pl.BlockSpec((tm, HD), lambda i: (i, 0))] * 2
        args += [cs, sn]
    out_specs, out_shape = [], []
    if has_q:
        out_specs.append(pl.BlockSpec((NQ, tm, HD), lambda i: (0, i, 0)))
        out_shape.append(jax.ShapeDtypeStruct((NQ, n, HD), BF16))
    out_specs += [pl.BlockSpec((NKV, tm, HD), lambda i: (0, rb + i, 0))] * 2
    out_shape += [jax.ShapeDtypeStruct((NKV, kv_rows, HD), BF16)] * 2
    aliases = {}
    if kv_into is not None:
        aliases = {len(args): int(has_q), len(args) + 1: int(has_q) + 1}
        in_specs += [pl.BlockSpec(memory_space=pl.ANY)] * 2
        args += list(kv_into)
    return pl.pallas_call(body, grid=(n // tm,), in_specs=in_specs, out_specs=out_specs, out_shape=out_shape,
                          input_output_aliases=aliases, name=name, compiler_params=_params("parallel"))(*args)


def _in_proj_qkv(a, w_in_t, q_gain, k_gain, cs, sn, conv_w, kv_into, *, name, kv_row_off, tm=256):
    n, d = a.shape
    nproj = w_in_t.shape[0]
    nqkv = AW + 2 * NKV * HD
    rb = kv_row_off // tm
    ni = n // tm
    halo = 16
    rows = tm + 2 * halo
    r = tm // halo
    last = n // halo - 1

    def body(a_ref, ap_ref, an_ref, w_ref, qg_ref, kg_ref, cs_ref, sn_ref, cw_ref, _k_in, _v_in,
             p_ref, qo_ref, ko_ref, vo_ref, conv_ref):
        i = pl.program_id(0)
        av = a_ref[...]
        aext = jnp.concatenate([jnp.where(i > 0, ap_ref[...], jnp.zeros_like(ap_ref[...])), av,
                                jnp.where(i < ni - 1, an_ref[...], jnp.zeros_like(an_ref[...]))], axis=0)
        qkv = lax.dot_general(av, w_ref[0:nqkv, :], _NT, preferred_element_type=F32)
        p_ref[:, 0:nqkv] = qkv
        cext = lax.dot_general(aext, w_ref[nqkv:nproj, :], _NT, preferred_element_type=F32)
        p_ref[:, nqkv:nproj] = cext[halo:halo + tm]
        hext = cext[:, CW:2 * CW] * cext[:, 2 * CW:3 * CW]
        cv3 = (pltpu.roll(hext, 1, axis=0)[halo:halo + tm] * cw_ref[0:1, :] + hext[halo:halo + tm] * cw_ref[1:2, :]
               + pltpu.roll(hext, rows - 1, axis=0)[halo:halo + tm] * cw_ref[2:3, :])
        conv_ref[...] = (cext[halo:halo + tm, 0:CW] * cv3).astype(BF16)

        def norm_rope(xh, gain, mul=None):
            r = lax.rsqrt(jnp.mean(xh * xh, axis=-1, keepdims=True) + EPS)
            xn = (xh * r) * gain
            xn = xn * cs_ref[...] + _partner(xn) * sn_ref[...]
            if mul is not None:
                xn = xn * mul
            return xn.astype(BF16)

        for h in range(NQ):
            qo_ref[h] = norm_rope(qkv[:, h * HD:(h + 1) * HD], qg_ref[...], _QSCALE)
        for h in range(NKV):
            ko_ref[h] = norm_rope(qkv[:, AW + h * HD:AW + (h + 1) * HD], kg_ref[...])
            vo_ref[h] = qkv[:, AW + (NKV + h) * HD:AW + (NKV + h + 1) * HD].astype(BF16)

    kv_rows = kv_into[0].shape[1]
    tab = pl.BlockSpec((tm, HD), lambda i: (i, 0))
    any_spec = pl.BlockSpec(memory_space=pl.ANY)
    kv_spec = pl.BlockSpec((NKV, tm, HD), lambda i: (0, rb + i, 0))
    return pl.pallas_call(
        body, grid=(ni,),
        in_specs=[pl.BlockSpec((tm, d), lambda i: (i, 0)),
                  pl.BlockSpec((halo, d), lambda i: (jnp.maximum(i * r - 1, 0), 0)),
                  pl.BlockSpec((halo, d), lambda i: (jnp.minimum((i + 1) * r, last), 0)),
                  pl.BlockSpec(w_in_t.shape, lambda i: (0, 0)), _vec(HD), _vec(HD), tab, tab,
                  pl.BlockSpec((3, CW), lambda i: (0, 0)), any_spec, any_spec],
        out_specs=[pl.BlockSpec((tm, nproj), lambda i: (i, 0)), pl.BlockSpec((NQ, tm, HD), lambda i: (0, i, 0)),
                   kv_spec, kv_spec, pl.BlockSpec((tm, CW), lambda i: (i, 0))],
        out_shape=[jax.ShapeDtypeStruct((n, nproj), F32), jax.ShapeDtypeStruct((NQ, n, HD), BF16),
                   jax.ShapeDtypeStruct((NKV, kv_rows, HD), BF16), jax.ShapeDtypeStruct((NKV, kv_rows, HD), BF16),
                   jax.ShapeDtypeStruct((n, CW), BF16)],
        input_output_aliases={9: 2, 10: 3}, name=name,
        compiler_params=_params("parallel"))(a, a, a, w_in_t, q_gain, k_gain, cs, sn, conv_w, *kv_into)


def _qkv_bwd(p, dq, dk, dv, q_gain, k_gain, cs, sn, *, name, has_q, kv_col, kv_row_off, tm=256):
    n = p.shape[0]
    rope = cs is not None
    rb = kv_row_off // tm

    def body(*refs):
        it = iter(refs)
        q_ref = next(it) if has_q else None
        kv_ref = next(it)
        dq_ref = next(it) if has_q else None
        dk_ref, dv_ref = next(it), next(it)
        qg_ref, kg_ref = next(it), next(it)
        cs_ref = next(it) if rope else None
        sn_ref = next(it) if rope else None
        dp_ref, dqg_ref, dkg_ref = next(it), next(it), next(it)
        i = pl.program_id(0)

        def back(xh, dout, gain):
            if rope:
                dout = dout * cs_ref[...] + _partner(dout * sn_ref[...])
            r = lax.rsqrt(jnp.mean(xh * xh, axis=-1, keepdims=True) + EPS)
            xhat = xh * r
            dxh = dout * gain
            dx = r * (dxh - xhat * jnp.mean(dxh * xhat, axis=-1, keepdims=True))
            return dx, _colsum(dout * xhat)

        dqg = jnp.zeros((1, HD), F32)
        dkg = jnp.zeros((1, HD), F32)
        if has_q:
            for h in range(NQ):
                dx, dg = back(q_ref[:, h * HD:(h + 1) * HD], dq_ref[h], qg_ref[...])
                dp_ref[:, h * HD:(h + 1) * HD] = dx.astype(BF16)
                dqg = dqg + dg
        else:
            dp_ref[:, 0:AW] = jnp.zeros((tm, AW), BF16)
        for h in range(NKV):
            dx, dg = back(kv_ref[:, h * HD:(h + 1) * HD], dk_ref[h], kg_ref[...])
            dp_ref[:, AW + h * HD:AW + (h + 1) * HD] = dx.astype(BF16)
            dkg = dkg + dg
            dp_ref[:, AW + (NKV + h) * HD:AW + (NKV + h + 1) * HD] = dv_ref[h].astype(BF16)
        _acc_out(dqg_ref, i, dqg)
        _acc_out(dkg_ref, i, dkg)

    in_specs, args = [], []
    if has_q:
        in_specs.append(pl.BlockSpec((tm, AW), lambda i: (i, 0)))
        args.append(p)
    in_specs.append(pl.BlockSpec((tm, 2 * NKV * HD), lambda i: (i, kv_col)))
    args.append(p)
    if has_q:
        in_specs.append(pl.BlockSpec((NQ, tm, HD), lambda i: (0, i, 0)))
        args.append(dq)
    in_specs += [pl.BlockSpec((NKV, tm, HD), lambda i: (0, rb + i, 0))] * 2 + [_vec(HD), _vec(HD)]
    args += [dk, dv, q_gain, k_gain]
    if rope:
        in_specs += [pl.BlockSpec((tm, HD), lambda i: (i, 0))] * 2
        args += [cs, sn]
    return pl.pallas_call(
        body, grid=(n // tm,), in_specs=in_specs,
        out_specs=[pl.BlockSpec((tm, D), lambda i: (i, 0)), _vec(HD), _vec(HD)],
        out_shape=[jax.ShapeDtypeStruct((n, D), BF16), jax.ShapeDtypeStruct((1, HD), F32),
                   jax.ShapeDtypeStruct((1, HD), F32)],
        name=name, compiler_params=_params("arbitrary"))(*args)


def _out_proj_dx_conv_bwd(dy, w_out, p, conv_w, *, name, tm=256):
    n, d = dy.shape
    ni = n // tm
    rows = tm + 2 * HALO

    def body(z_ref, zp_ref, zn_ref, wo_ref, gb_ref, gbp_ref, gbn_ref, gc_ref, gcp_ref, gcn_ref, xi_ref, xip_ref,
             xin_ref, w_ref, do_ref, dp_ref, dw_ref):
        i = pl.program_id(0)
        zext = jnp.concatenate([jnp.where(i > 0, zp_ref[...], jnp.zeros_like(zp_ref[...])), z_ref[...],
                                jnp.where(i < ni - 1, zn_ref[...], jnp.zeros_like(zn_ref[...]))], axis=0)
        do_ref[...] = lax.dot_general(z_ref[...], wo_ref[0:AW, :], _NT, preferred_element_type=F32)
        dconv = lax.dot_general(zext, wo_ref[AW:D, :], _NT, preferred_element_type=F32)[HALO:HALO + rows]
        gcext = _ext(gcp_ref, gc_ref, gcn_ref, i, ni)
        xiext = _ext(xip_ref, xi_ref, xin_ref, i, ni)
        hext = gcext * xiext
        dcv = dconv * _ext(gbp_ref, gb_ref, gbn_ref, i, ni)
        dp_ref[:, 0:CW] = (dconv[HALO:HALO + tm] * _conv3(hext, w_ref, tm)).astype(BF16)
        dh = _sh(dcv, 1, tm) * w_ref[0:1, :] + _sh(dcv, 0, tm) * w_ref[1:2, :] + _sh(dcv, -1, tm) * w_ref[2:3, :]
        dp_ref[:, CW:2 * CW] = (dh * xi_ref[...]).astype(BF16)
        dp_ref[:, 2 * CW:3 * CW] = (dh * gc_ref[...]).astype(BF16)
        dcv_t = dcv[HALO:HALO + tm]
        dw = jnp.concatenate([_colsum(dcv_t * _sh(hext, -1, tm)), _colsum(dcv_t * _sh(hext, 0, tm)),
                              _colsum(dcv_t * _sh(hext, 1, tm))], axis=0)
        _acc_out(dw_ref, i, dw)

    def trio(colblk):
        prev, nxt = _halo_specs(tm, CW, n, colblk=colblk)
        return [pl.BlockSpec((tm, CW), lambda i: (i, colblk)), prev, nxt]

    r16, last16 = tm // 16, n // 16 - 1
    zspecs = [pl.BlockSpec((tm, d), lambda i: (i, 0)),
              pl.BlockSpec((16, d), lambda i: (jnp.maximum(i * r16 - 1, 0), 0)),
              pl.BlockSpec((16, d), lambda i: (jnp.minimum((i + 1) * r16, last16), 0))]
    return pl.pallas_call(
        body, grid=(ni,),
        in_specs=zspecs + [pl.BlockSpec(w_out.shape, lambda i: (0, 0))] + trio(2) + trio(3) + trio(4)
        + [pl.BlockSpec((3, CW), lambda i: (0, 0))],
        out_specs=[pl.BlockSpec((tm, AW), lambda i: (i, 0)), pl.BlockSpec((tm, 3 * CW), lambda i: (i, 0)),
                   pl.BlockSpec((3, CW), lambda i: (0, 0))],
        out_shape=[jax.ShapeDtypeStruct((n, AW), F32), jax.ShapeDtypeStruct((n, 3 * CW), BF16),
                   jax.ShapeDtypeStruct((3, CW), F32)],
        name=name, compiler_params=_params("arbitrary"))(dy, dy, dy, w_out, p, p, p, p, p, p, p, p, p, conv_w)


def _attn_fwd(q, k, v, *, name, bq=512, sub=256):
    n = q.shape[1]
    t = k.shape[1]
    bq = min(bq, n)
    sub = min(sub, 2 * bq)

    def body(q_ref, k_ref, v_ref, o_ref, lse_ref):
        q2 = q_ref[...].reshape(2 * bq, HD)
        outs, lses = [], []
        for r0 in range(0, 2 * bq, sub):
            s = lax.dot_general(q2[r0:r0 + sub], k_ref[0], _NT, preferred_element_type=F32)
            m = jnp.max(s, axis=-1, keepdims=True)
            pv = jnp.exp2(s - m)
            l = jnp.sum(pv, axis=-1, keepdims=True)
            outs.append(jnp.dot(pv.astype(BF16), v_ref[0], preferred_element_type=F32) / l)
            lses.append(m + jnp.log2(l))
        out = jnp.concatenate(outs, axis=0)
        o_ref[:, 0:HD] = out[0:bq]
        o_ref[:, HD:2 * HD] = out[bq:2 * bq]
        lse_ref[...] = jnp.concatenate(lses, axis=0).reshape(2, bq, 1)

    kspec = pl.BlockSpec((1, t, HD), lambda h, i: (h, 0, 0))
    return pl.pallas_call(
        body, grid=(NKV, n // bq),
        in_specs=[pl.BlockSpec((2, bq, HD), lambda h, i: (h, i, 0)), kspec, kspec],
        out_specs=[pl.BlockSpec((bq, 2 * HD), lambda h, i: (i, h)), pl.BlockSpec((2, bq, 1), lambda h, i: (h, i, 0))],
        out_shape=[jax.ShapeDtypeStruct((n, AW), F32), jax.ShapeDtypeStruct((NQ, n, 1), F32)],
        name=name, compiler_params=_params("parallel", "parallel"))(q, k, v)


def _attn_bwd(q, k, v, dcat, o, lse, *, name, bq=256):
    n = q.shape[1]
    t = k.shape[1]
    bq = min(bq, n)

    def body(q_ref, k_ref, v_ref, dc_ref, o_ref, lse_ref, dq_ref, dk_ref, dv_ref):
        @pl.when(pl.program_id(1) == 0)
        def _():
            dk_ref[...] = jnp.zeros_like(dk_ref)
            dv_ref[...] = jnp.zeros_like(dv_ref)

        q2 = q_ref[...].reshape(2 * bq, HD)
        do_f = jnp.concatenate([dc_ref[:, 0:HD], dc_ref[:, HD:2 * HD]], axis=0)
        o_f = jnp.concatenate([o_ref[:, 0:HD], o_ref[:, HD:2 * HD]], axis=0)
        delta = jnp.sum(do_f * o_f, axis=-1, keepdims=True)
        do2 = do_f.astype(BF16)
        s = lax.dot_general(q2, k_ref[0], _NT, preferred_element_type=F32)
        pv = jnp.exp2(s - lse_ref[...].reshape(2 * bq, 1))
        dp = lax.dot_general(do2, v_ref[0], _NT, preferred_element_type=F32)
        ds = (pv * (dp - delta)).astype(BF16)
        dq_ref[...] = (jnp.dot(ds, k_ref[0], preferred_element_type=F32) * _SCALE).reshape(2, bq, HD)
        dk_ref[0] += lax.dot_general(ds, q2, _TN, preferred_element_type=F32) * _LN2
        dv_ref[0] += lax.dot_general(pv.astype(BF16), do2, _TN, preferred_element_type=F32)

    qspec = pl.BlockSpec((2, bq, HD), lambda h, i: (h, i, 0))
    kspec = pl.BlockSpec((1, t, HD), lambda h, i: (h, 0, 0))
    sspec = pl.BlockSpec((2, bq, 1), lambda h, i: (h, i, 0))
    cspec = pl.BlockSpec((bq, 2 * HD), lambda h, i: (i, h))
    return pl.pallas_call(
        body, grid=(NKV, n // bq), in_specs=[qspec, kspec, kspec, cspec, cspec, sspec], out_specs=[qspec, kspec, kspec],
        out_shape=[jax.ShapeDtypeStruct((NQ, n, HD), F32), jax.ShapeDtypeStruct((NKV, t, HD), F32),
                   jax.ShapeDtypeStruct((NKV, t, HD), F32)],
        name=name, compiler_params=_params("parallel", "arbitrary"))(q, k, v, dcat, o, lse)


def _window_sums(ext, w):
    s, step = ext, 1
    while step < w:
        s = s + _roll_rows(s, step)
        step *= 2
    return s


def _pool_counts(i, tm, n, w, rows, first):
    t = i * tm - HALO + first + lax.broadcasted_iota(jnp.int32, (rows, 1), 0)
    lo = jnp.clip(t - w // 2, 0, n)
    hi = jnp.clip(t + w - w // 2, 0, n)
    return jnp.maximum(hi - lo, 1).astype(F32)


def _norm_mod_ext(xext, gain_ref, sc_ref, sh_ref, i, tm, n):
    rows = xext.shape[0]
    t = i * tm - HALO + lax.broadcasted_iota(jnp.int32, (rows, 1), 0)
    inside = (t >= 0) & (t < n)
    r = lax.rsqrt(jnp.mean(xext * xext, axis=-1, keepdims=True) + EPS)
    xh = xext * r
    a = (xh * gain_ref[...]) * (1.0 + sc_ref[...]) + sh_ref[...]
    return jnp.where(inside, a, 0.0), r, xh


def _pool_fwd(x, y, g, gain, sc, sh, pool_w, *, name, tm=256):
    n, d = x.shape
    ni = n // tm

    def body(x_ref, xp_ref, xn_ref, y_ref, yp_ref, yn_ref, g_ref, gain_ref, sc_ref, sh_ref, w_ref, xo_ref, o_ref):
        i = pl.program_id(0)
        xext = _ext(xp_ref, x_ref, xn_ref, i, ni) + g_ref[...] * _ext(yp_ref, y_ref, yn_ref, i, ni)
        xo_ref[...] = xext[HALO:HALO + tm]
        aext, _, _ = _norm_mod_ext(xext, gain_ref, sc_ref, sh_ref, i, tm, n)
        for gi, w in enumerate(POOL_WINDOWS):
            ag = aext[:, gi * PG:(gi + 1) * PG]
            mean = _sh(_window_sums(ag, w), -(w // 2), tm) / _pool_counts(i, tm, n, w, tm, HALO)
            pooled = mean - ag[HALO:HALO + tm]
            o_ref[:, gi * PG:(gi + 1) * PG] = jnp.dot(pooled.astype(BF16), w_ref[gi], preferred_element_type=F32)

    row = pl.BlockSpec((tm, d), lambda i: (i, 0))
    prev, nxt = _halo_specs(tm, d, n)
    return pl.pallas_call(
        body, grid=(ni,),
        in_specs=[row, prev, nxt, row, prev, nxt, _vec(d), _vec(d), _vec(d), _vec(d),
                  pl.BlockSpec((4, PG, PG), lambda i: (0, 0, 0))],
        out_specs=[row, row], out_shape=[jax.ShapeDtypeStruct((n, d), F32)] * 2,
        name=name, compiler_params=_params("parallel"))(x, x, x, y, y, y, g, gain, sc, sh, pool_w)


def _pool_bwd(dxo, mixed, x, g, scale, gain, sc, sh, pool_w, zprev, gprev, *, name, tm=256):
    n, d = x.shape
    ni = n // tm

    def body(dx_ref, dxp_ref, dxn_ref, mx_ref, x_ref, xp_ref, xn_ref, g_ref, s_ref, gain_ref, sc_ref, sh_ref, w_ref,
             zp_ref, gp_ref, dxi_ref, dw_ref, dg_ref, dsl_ref, dsh_ref, dsc_ref, dgn_ref, dzp_ref, dgp_ref):
        i = pl.program_id(0)

        @pl.when(i == 0)
        def _():
            dw_ref[...] = jnp.zeros_like(dw_ref)

        dxo_t = dx_ref[...]
        mixed_t = mx_ref[...]
        dy_t = dxo_t * g_ref[...]
        _acc_out(dg_ref, i, _colsum(dxo_t * (mixed_t * s_ref[...])))
        _acc_out(dsl_ref, i, _colsum(dy_t * mixed_t))
        dmixed = (_ext(dxp_ref, dx_ref, dxn_ref, i, ni) * g_ref[...]) * s_ref[...]
        xext = _ext(xp_ref, x_ref, xn_ref, i, ni)
        aext, rext, xhext = _norm_mod_ext(xext, gain_ref, sc_ref, sh_ref, i, tm, n)
        rows = tm + 2 * HALO
        da_parts = []
        for gi, w in enumerate(POOL_WINDOWS):
            sl = slice(gi * PG, (gi + 1) * PG)
            ag = aext[:, sl]
            mean = _sh(_window_sums(ag, w), -(w // 2), tm) / _pool_counts(i, tm, n, w, tm, HALO)
            pooled = (mean - ag[HALO:HALO + tm]).astype(BF16)
            dmg = dmixed[:, sl].astype(BF16)
            dw_ref[gi] += lax.dot_general(pooled, dmixed[HALO:HALO + tm, sl].astype(BF16), _TN,
                                          preferred_element_type=F32)
            dpl = lax.dot_general(dmg, w_ref[gi], _NT, preferred_element_type=F32)
            e = dpl / _pool_counts(i, tm, n, w, rows, 0)
            da_parts.append(_sh(_window_sums(e, w), 1 - w // 2, tm) - dpl[HALO:HALO + tm])
        da = jnp.concatenate(da_parts, axis=1)
        r = rext[HALO:HALO + tm]
        xh = xhext[HALO:HALO + tm]
        nrm = xh * gain_ref[...]
        dn = da * (1.0 + sc_ref[...])
        dxh = dn * gain_ref[...]
        dxi = dxo_t + r * (dxh - xh * jnp.mean(dxh * xh, axis=-1, keepdims=True))
        dxi_ref[...] = dxi
        _acc_out(dsh_ref, i, _colsum(da))
        _acc_out(dsc_ref, i, _colsum(da * nrm))
        _acc_out(dgn_ref, i, _colsum(dn * xh))
        dzp_ref[...] = (dxi * gp_ref[...]).astype(BF16)
        _acc_out(dgp_ref, i, _colsum(dxi * zp_ref[...]))

    row = pl.BlockSpec((tm, d), lambda i: (i, 0))
    prev, nxt = _halo_specs(tm, d, n)
    wspec = pl.BlockSpec((4, PG, PG), lambda i: (0, 0, 0))
    vshape = jax.ShapeDtypeStruct((1, d), F32)
    return pl.pallas_call(
        body, grid=(ni,),
        in_specs=[row, prev, nxt, row, row, prev, nxt] + [_vec(d)] * 5 + [wspec, row, _vec(d)],
        out_specs=[row, wspec] + [_vec(d)] * 5 + [row, _vec(d)],
        out_shape=[jax.ShapeDtypeStruct((n, d), F32), jax.ShapeDtypeStruct((4, PG, PG), F32)] + [vshape] * 5
        + [jax.ShapeDtypeStruct((n, d), BF16), vshape],
        name=name, compiler_params=_params("arbitrary"))(dxo, dxo, dxo, mixed, x, x, x, g, scale, gain, sc, sh, pool_w,
                                                         zprev, gprev)


def _adamw(gparts_list, w, m, v, *, name, silu_grad_of=None):
    nl = len(gparts_list)
    nparts, r, c = gparts_list[0].shape
    tr = _pick(r, (256, 128, 64, 32, 16, 8))
    has_c = silu_grad_of is not None

    def body(*refs):
        gp_refs = refs[:nl]
        it = iter(refs[nl:])
        w_ref, m_ref, v_ref = next(it), next(it), next(it)
        c_ref = next(it) if has_c else None
        g_ref, d_ref, mo_ref, vo_ref = next(it), next(it), next(it), next(it)
        layer = pl.program_id(0)

        def update(gp_ref):
            g = gp_ref[0].astype(F32)
            for p in range(1, nparts):
                g = g + gp_ref[p].astype(F32)
            if has_c:
                cv = c_ref[0]
                sg = _sigmoid(cv)
                g = g * (sg * (1.0 + cv * (1.0 - sg)))
            g_ref[0] = g
            mn = ADAM_B1 * m_ref[0] + (1.0 - ADAM_B1) * g
            vn = ADAM_B2 * v_ref[0] + (1.0 - ADAM_B2) * (g * g)
            m_hat = mn / (1.0 - ADAM_B1 ** ADAM_STEP)
            v_hat = vn / (1.0 - ADAM_B2 ** ADAM_STEP)
            d_ref[0] = -ADAM_LR * (m_hat / (jnp.sqrt(v_hat) + ADAM_EPS) + ADAM_WD * w_ref[0])
            mo_ref[0] = mn
            vo_ref[0] = vn

        if nl == 1:
            update(gp_refs[0])
        else:
            for li in range(nl):
                pl.when(layer == li)(functools.partial(update, gp_refs[li]))

    row = pl.BlockSpec((1, tr, c), lambda l, i: (l, i, 0))
    in_specs = [pl.BlockSpec((nparts, tr, c), lambda l, i, li=li: (0, jnp.where(l == li, i, 0), 0)) for li in range(nl)]
    in_specs += [row, row, row]
    args = list(gparts_list) + [w, m, v]
    if has_c:
        in_specs.append(row)
        args.append(silu_grad_of)
    return pl.pallas_call(
        body, grid=(nl, r // tr), in_specs=in_specs, out_specs=[row] * 4,
        out_shape=[jax.ShapeDtypeStruct((nl, r, c), F32)] * 4, name=name,
        compiler_params=_params("arbitrary", "arbitrary"))(*args)


def _adamw_nd(gparts, w, m, v, *, name, silu_grad_of=None):
    shape = w.shape
    c = shape[-1]
    if isinstance(gparts, (list, tuple)):
        nl = len(gparts)
        r = math.prod(shape[1:-1])
    else:
        nl = 1
        r = math.prod(shape[:-1]) if len(shape) > 1 else 1
        gparts = [gparts]
    rs = lambda a: a.reshape(nl, r, c)
    res = _adamw([gp.reshape(gp.shape[0], r, c) for gp in gparts], rs(w), rs(m), rs(v), name=name,
                 silu_grad_of=None if silu_grad_of is None else rs(silu_grad_of))
    return [a.reshape(shape) for a in res]


def _place():
    return lax.axis_index("x"), lax.axis_index("y"), lax.axis_index("c")


def _all_gather(arrs, *, name):
    k_arr = len(arrs)

    def body(*refs):
        ins = refs[:k_arr]
        outs = refs[k_arr:2 * k_arr]
        send_sems, recv_sems, local_sems = refs[2 * k_arr:]
        x, y, c = _place()
        me, sibling = (x, y, c), (x, y, 1 - c)
        chips = [(1 - x, y), (x, 1 - y), (1 - x, 1 - y)]

        def slot(a, px, py, pc):
            return outs[a].at[4 * px + 2 * py + pc]

        def copy(a, s, block, to, src=None):
            return pltpu.make_async_remote_copy(
                src_ref=slot(a, *block) if src is None else src, dst_ref=slot(a, *block),
                send_sem=send_sems.at[a, s], recv_sem=recv_sems.at[a, s], device_id=to, device_id_type=MESH)

        mine = [pltpu.make_async_copy(ins[a], slot(a, *me), local_sems.at[a]) for a in range(k_arr)]
        for cp in mine:
            cp.start()
        first = []
        for a in range(k_arr):
            first.append(copy(a, 0, me, sibling, src=ins[a]))
            first += [copy(a, 1 + j, me, (*chip, c), src=ins[a]) for j, chip in enumerate(chips)]
        for cp in first:
            cp.start()
        passed = []
        for j, chip in enumerate(chips):
            for a in range(k_arr):
                copy(a, 1 + j, (*chip, c), me).wait_recv()
                fw = copy(a, 4 + j, (*chip, c), sibling)
                fw.start()
                passed.append(fw)
        for a in range(k_arr):
            copy(a, 0, sibling, me).wait_recv()
            for j, chip in enumerate(chips):
                copy(a, 4 + j, (*chip, 1 - c), me).wait_recv()
        for cp in first + passed:
            cp.wait_send()
        for cp in mine:
            cp.wait()

    any_spec = pl.BlockSpec(memory_space=pl.ANY)
    return pl.pallas_call(
        body, in_specs=[any_spec] * k_arr, out_specs=[any_spec] * k_arr,
        out_shape=[jax.ShapeDtypeStruct((NDEV,) + a.shape, a.dtype) for a in arrs],
        scratch_shapes=[pltpu.SemaphoreType.DMA((k_arr, 7)), pltpu.SemaphoreType.DMA((k_arr, 7)),
                        pltpu.SemaphoreType.DMA((k_arr,))],
        name=name)(*arrs)


_HBM = pl.BlockSpec(memory_space=pltpu.HBM)
_SEM = pl.BlockSpec(memory_space=pltpu.SEMAPHORE)
_EFFECT = pltpu.SideEffectType.DATAFLOW_SIDE_EFFECTING


def _peers(x, y, c):
    return [(x ^ (rel >> 2), y ^ ((rel >> 1) & 1), c ^ (rel & 1)) for rel in range(1, NDEV)]


def _exchange_copies(srcs, lands, send_sems, recv_sems, scatter):
    x, y, c = _place()
    me = 4 * x + 2 * y + c
    copies = []
    for r, (px, py, pc) in enumerate(_peers(x, y, c)):
        peer = 4 * px + 2 * py + pc
        for a in range(len(srcs)):
            copies.append(pltpu.make_async_remote_copy(
                src_ref=srcs[a].at[peer] if scatter else srcs[a], dst_ref=lands[a].at[me],
                send_sem=send_sems.at[7 * a + r], recv_sem=recv_sems.at[7 * a + r], device_id=(px, py, pc),
                device_id_type=MESH))
    return copies


def _exchange_start(arrs, *, scatter, name):
    k_arr = len(arrs)
    land_shapes = [a.shape if scatter else (NDEV,) + a.shape for a in arrs]
    lands = [pltpu.with_memory_space_constraint(lax.empty(s, a.dtype), pltpu.HBM) for s, a in zip(land_shapes, arrs)]
    srcs = [pltpu.with_memory_space_constraint(a, pltpu.HBM) for a in arrs]

    def body(*refs):
        src_refs, land_refs = refs[:k_arr], refs[k_arr:2 * k_arr]
        send_sems, recv_sems = refs[2 * k_arr], refs[2 * k_arr + 1]
        token = refs[-1]
        for cp in _exchange_copies(src_refs, land_refs, send_sems, recv_sems, scatter):
            cp.start()
        token[...] = jnp.zeros_like(token)

    out_shape = ([pltpu.SemaphoreType.DMA((7 * k_arr,)), pltpu.SemaphoreType.DMA((7 * k_arr,))]
                 + [pltpu.HBM(a.shape, a.dtype) for a in arrs] + [pltpu.HBM(s, a.dtype) for s, a in zip(land_shapes, arrs)]
                 + [jax.ShapeDtypeStruct((8, 128), F32)])
    res = pl.pallas_call(
        body, name=name, out_shape=out_shape, in_specs=[_HBM] * (2 * k_arr),
        out_specs=[_SEM, _SEM] + [_HBM] * (2 * k_arr) + [pl.BlockSpec(memory_space=pltpu.VMEM)],
        input_output_aliases={i: 2 + i for i in range(2 * k_arr)},
        compiler_params=pltpu.CompilerParams(has_side_effects=_EFFECT))(*srcs, *lands)
    return dict(send=res[0], recv=res[1], srcs=list(res[2:2 + k_arr]), lands=list(res[2 + k_arr:2 + 2 * k_arr]),
                token=res[-1], scatter=scatter)


def _exchange_wait(handle, after, *, name):
    k_arr = len(handle["srcs"])
    scatter = handle["scatter"]

    def body(*refs):
        src_refs, land_refs = refs[:k_arr], refs[k_arr:2 * k_arr]
        send_sems, recv_sems = refs[2 * k_arr], refs[2 * k_arr + 1]
        x, y, c = _place()
        me = 4 * x + 2 * y + c
        for r, (px, py, pc) in enumerate(_peers(x, y, c)):
            peer = 4 * px + 2 * py + pc
            for a in range(k_arr):
                cp = pltpu.make_async_remote_copy(
                    src_ref=src_refs[a].at[peer] if scatter else src_refs[a], dst_ref=land_refs[a].at[peer],
                    send_sem=send_sems.at[7 * a + r], recv_sem=recv_sems.at[7 * a + r], device_id=(x, y, c),
                    device_id_type=MESH)
                cp.wait_send()
                cp.wait_recv()

    arrs = handle["srcs"] + handle["lands"]
    res = pl.pallas_call(
        body, name=name, out_shape=[pltpu.HBM(a.shape, a.dtype) for a in arrs],
        in_specs=[_HBM] * (2 * k_arr) + [_SEM, _SEM, pl.BlockSpec(memory_space=pl.ANY)],
        out_specs=[_HBM] * (2 * k_arr), input_output_aliases={i: i for i in range(2 * k_arr)},
        compiler_params=pltpu.CompilerParams(has_side_effects=_EFFECT))(*arrs, handle["send"], handle["recv"], after)
    me = 4 * lax.axis_index("x") + 2 * lax.axis_index("y") + lax.axis_index("c")
    out = []
    for src, land in zip(res[:k_arr], res[k_arr:]):
        own = lax.dynamic_index_in_dim(src, me, 0, keepdims=False) if scatter else src
        out.append(lax.dynamic_update_index_in_dim(land, own, me, 0))
    return out


def _ffn_bwd(dxo, dz, xr, f, u_gc, hmid, gain, sc, w_up, cw, w_down, tag, gate_y=None, gate_g=None):
    d_wdown = _mm_tn((hmid, dz), name=f"ffn_down_dw_{tag}")
    dug, duv, dcw, dcb = _ffn_down_glu_bwd(dz, w_down, u_gc[0], u_gc[1], cw, name=f"ffn_down_glu_bwd_{tag}")
    d_wup = _mm_tn((dug, f), blocks=2, block=0, name=f"ffn_up_dwg_{tag}")
    d_wup = _mm_tn((duv, f), blocks=2, block=1, into=d_wup, name=f"ffn_up_dwv_{tag}")
    gated = gate_y is not None
    res = _mm_w_ep([dug, duv], w_up, _ep_norm_bwd(gated), [xr, dxo] + ([gate_y] if gated else []),
                   [gain, sc] + ([gate_g] if gated else []), [F32] + ([BF16] if gated else []),
                   [D] * (4 if gated else 3), name=f"ffn_up_dx_norm_bwd_{tag}")
    n_out = 2 if gated else 1
    return res[:n_out], res[n_out:], (d_wup, d_wdown, dcw, dcb)


def _split6(mod):
    return [mod[j * D:(j + 1) * D][None, :] for j in range(6)]


def _row(v):
    return v.reshape(1, -1)


def kernel(x, c, ctx, c_ctx, ada_w, ada_b, mix_norm, ffn_norm, even_w_in, even_q_gain, even_k_gain, even_conv_w, even_w_out, odd_pool_w, odd_pool_scale, ffn_w_up, ffn_conv_w, ffn_conv_b, ffn_w_down, loss_target, m_c_ctx, m_ada_w, m_ada_b, m_mix_norm, m_ffn_norm, m_even_w_in, m_even_q_gain, m_even_k_gain, m_even_conv_w, m_even_w_out, m_odd_pool_w, m_odd_pool_scale, m_ffn_w_up, m_ffn_conv_w, m_ffn_conv_b, m_ffn_w_down, v_c_ctx, v_ada_w, v_ada_b, v_mix_norm, v_ffn_norm, v_even_w_in, v_even_q_gain, v_even_k_gain, v_even_conv_w, v_even_w_out, v_odd_pool_w, v_odd_pool_scale, v_ffn_w_up, v_ffn_conv_w, v_ffn_conv_b, v_ffn_w_down):
    n = x.shape[1]
    lc = ctx.shape[1]
    me = 4 * lax.axis_index("x") + 2 * lax.axis_index("y") + lax.axis_index("c")
    xs, ctxs, tgt = x[0], ctx[0], loss_target[0]
    acols = ada_w.shape[2]

    small = jnp.concatenate([even_conv_w.reshape(-1), ffn_conv_w.reshape(-1), odd_pool_scale.reshape(-1)])
    nsmall = small.shape[0]
    small = jnp.pad(small, (0, (-nsmall) % 1024)).reshape(-1, 128)
    c_rows = jnp.pad(c, ((0, 7), (0, 0)))
    tr = lambda a: jnp.swapaxes(a, -1, -2)
    g_c, g_win, g_small = _all_gather([c_rows, tr(even_w_in[0]).astype(BF16), small], name="gather_first")
    w_in_t = g_win.reshape(-1, D)
    g_small = g_small.reshape(NDEV, -1)
    ecw = even_conv_w.shape[2]
    fcw = ffn_conv_w.shape[2]
    conv_w = g_small[:, :3 * ecw].reshape(NDEV, 3, ecw).transpose(1, 0, 2).reshape(3, CW)
    o1 = 3 * ecw
    fconv_w = g_small[:, o1:o1 + 6 * fcw].reshape(NDEV, 2, 3, fcw).transpose(1, 2, 0, 3).reshape(2, 3, DFF)
    o2 = o1 + 6 * fcw
    pool_scale = g_small[:, o2:o2 + D // NDEV].reshape(1, D)

    mraw = jnp.concatenate([g_c[:, 0, :], c_ctx[None, :], jnp.zeros((7, D), F32)], axis=0)
    my_bias = lax.dynamic_slice_in_dim(ada_b, me * acols, acols, axis=1)
    modp = jnp.stack([_mm(mraw, ada_w[l], silu_a=True, bias=my_bias[l:l + 1], name=f"ada_proj_{l}", tm=16, tn=256)
                      for l in range(2)])
    (g_mod,) = _all_gather([modp], name="gather_mod")
    mod_rows = g_mod.transpose(1, 2, 0, 3).reshape(2, 16, 6 * D)
    late_shards = [even_w_out[0].astype(BF16), odd_pool_w[0].astype(BF16), tr(ffn_w_up[0]).astype(BF16),
                   tr(ffn_w_up[1]).astype(BF16), ffn_w_down[0].astype(BF16), ffn_w_down[1].astype(BF16)]
    late_shards, mod_rows = lax.optimization_barrier((late_shards, mod_rows))
    h_weights = _exchange_start(late_shards, scatter=False, name="weights_start")
    mod_rows = mod_rows + h_weights["token"][0, 0]
    mod = lax.dynamic_index_in_dim(mod_rows, me, axis=1, keepdims=False)
    sh1, sc1, g1, sh2, sc2, g2 = _split6(mod[0])
    sh1b, sc1b, g1b, sh2b, sc2b, g2b = _split6(mod[1])
    csh1, csc1 = _split6(mod_rows[0, 8])[:2]
    mixn = [_row(mix_norm[l]) for l in range(2)]
    ffnn = [_row(ffn_norm[l]) for l in range(2)]
    qg, kg = _row(even_q_gain[0]), _row(even_k_gain[0])
    fcb = [_row(ffn_conv_b[l]) for l in range(2)]

    cs_t, sn_t = _rope_tables(n)
    a_lat = _norm_mod(xs, mixn[0], sc1, sh1, name="mix0_norm")
    a_ctx = _norm_mod(ctxs, mixn[0], csc1, csh1, name="mix0_norm_ctx")
    p_ctx = _mm(a_ctx, w_in_t[AW:AW + 4 * HD], tb=True, name="in_proj_ctx", tm=256, tn=512, tk=1024)
    kv_ctx = _qkv_prep(p_ctx, qg, kg, None, None, has_q=False, kv_col=0, kv_rows=lc + n, name="qkv_prep_ctx")
    p_lat, q_r, k_all, v_all, conv = _in_proj_qkv(a_lat, w_in_t, qg, kg, cs_t, sn_t, conv_w, kv_ctx, kv_row_off=lc,
                                                  name="in_proj_qkv")
    o_attn, lse = _attn_fwd(q_r, k_all, v_all, name="attn_fwd")
    g_wout, g_pool, g_up0, g_up1, g_down0, g_down1 = _exchange_wait(h_weights, o_attn, name="weights_wait")
    w_out = g_wout.reshape(D, D)
    pool_w = g_pool.transpose(1, 0, 2, 3).reshape(4, PG, PG)
    w_up_t = [g_up0.reshape(2 * DFF, D), g_up1.reshape(2 * DFF, D)]
    w_up = [w.T for w in w_up_t]
    w_down = [g_down0.reshape(DFF, D), g_down1.reshape(DFF, D)]
    y0, x1, f0 = _mm_w_ep([o_attn, conv], w_out, _ep_resid_norm, [xs], [g1, ffnn[0], sc2, sh2], [F32, F32, BF16], [],
                          tm=512, name="out_proj_norm")[:3]
    *u0, h0 = _ffn_up_glu(f0, w_up[0], fconv_w[0], fcb[0], name="ffn_up_glu_l0")
    z0 = _mm_w(h0, w_down[0], name="ffn_down_l0")

    x2, mixed = _pool_fwd(x1, z0, g2, mixn[1], sc1b, sh1b, pool_w, name="pool_fwd")
    x3, f1 = _norm_mod(x2, ffnn[1], sc2b, sh2b, y=mixed, g=g1b, ymul=pool_scale, name="ffn_norm_l1")
    *u1, h1 = _ffn_up_glu(f1, w_up[1], fconv_w[1], fcb[1], name="ffn_up_glu_l1")
    dx4, dz1, loss_part, dg2b = _mm_w_ep(h1, w_down[1], _ep_loss(D), [x3, tgt], [g2b], [F32, BF16], [128, D],
                                         tm=512, name="ffn_down_loss")

    (dx3,), (dsh2b, dsc2b, dffn1), (dup1, ddown1, dfcw1, dfcb1) = _ffn_bwd(
        dx4, dz1, x3, f1, u1, h1, ffnn[1], sc2b, w_up_t[1], fconv_w[1], w_down[1], "l1")
    dx2, dpool_w, dg1b, dpscale, dsh1b, dsc1b, dmix1, dz0, dg2 = _pool_bwd(
        dx3, mixed, x2, g1b, pool_scale, mixn[1], sc1b, sh1b, pool_w, z0, g2, name="pool_bwd")

    s_pool = dpool_w.astype(BF16).reshape(4, NDEV, PG // NDEV, PG).transpose(1, 0, 2, 3)
    h_g1 = _exchange_start([s_pool, dup1.reshape(NDEV, -1, D), ddown1.reshape(NDEV, DFF // NDEV, D)], scatter=True,
                           name="grads1_start")

    (dx1, dy0), (dsh2, dsc2, dffn0, dg1), (dup0, ddown0, dfcw0, dfcb0) = _ffn_bwd(
        dx2, dz0, x1, f0, u0, h0, ffnn[0], sc2, w_up_t[0], fconv_w[0] + h_g1["token"][0, 0], w_down[0], "l0",
        gate_y=y0, gate_g=g1)
    h_g0 = _exchange_start([dup0.reshape(NDEV, -1, D), ddown0.reshape(NDEV, DFF // NDEV, D)], scatter=True,
                           name="grads0_start")
    d_attn, dp_conv, dconv_w = _out_proj_dx_conv_bwd(dy0, w_out, p_lat, conv_w + h_g0["token"][0, 0],
                                                     name="out_proj_dx_conv_bwd")
    d_wout = _mm_tn((o_attn, dy0), blocks=2, block=0, name="out_proj_dw_attn")
    d_wout = _mm_tn((conv, dy0), blocks=2, block=1, into=d_wout, name="out_proj_dw_conv")
    dq_r, dk_all, dv_all = _attn_bwd(q_r, k_all, v_all, d_attn, o_attn, lse, name="attn_bwd")
    dp_qkv, dqg_l, dkg_l = _qkv_bwd(p_lat, dq_r, dk_all, dv_all, qg, kg, cs_t, sn_t, has_q=True, kv_col=1,
                                    kv_row_off=lc, name="qkv_bwd")
    dp_ctx, _zero_qg, dkg_c = _qkv_bwd(p_ctx, None, dk_all, dv_all, qg, kg, None, None, has_q=False, kv_col=0,
                                       kv_row_off=0, name="qkv_bwd_ctx")
    da_ctx = _mm(dp_ctx, w_in_t[:D], name="in_proj_dx_ctx", tm=256, tn=512, tk=1024)
    d_win_qkv = _mm_tn([(dp_qkv, a_lat), (dp_ctx, a_ctx)], name="in_proj_dw_qkv")
    d_win_conv = _mm_tn((dp_conv, a_lat), name="in_proj_dw_conv")
    d_win_t = jnp.concatenate([d_win_qkv, d_win_conv], axis=0)
    grad_x, dsh1, dsc1, dmix0 = _mm_w_ep([dp_qkv, dp_conv], w_in_t, _ep_norm_bwd(False), [xs, dx1], [mixn[0], sc1],
                                         [F32], [D] * 3, tm=512, name="in_proj_dx_norm_bwd")
    _dctx, dcsh1, dcsc1, dmix0c = _norm_mod_bwd(da_ctx, ctxs, mixn[0], csc1, name="mix0_norm_bwd_ctx")

    z1k = jnp.zeros((1, D), F32)
    pack = jnp.concatenate(
        [v.reshape(-1) for v in (dsh1, dsc1, dg1, dsh2, dsc2, dg2, dsh1b, dsc1b, dg1b, dsh2b, dsc2b, dg2b,
                                 dcsh1, dcsc1, z1k, z1k, z1k, z1k,
                                 dmix0, dmix1, dmix0c, z1k, dffn0, dffn1, dqg_l, dkg_l + dkg_c,
                                 dfcb0, dfcb1, dconv_w, dfcw0, dfcw1, dpscale, loss_part[:, 0:1])])
    npack = pack.shape[0]
    pack = jnp.pad(pack, (0, (-npack) % 1024)).reshape(-1, 128)
    (g_pack,) = _all_gather([pack], name="gather_small_grads")
    def split(gp):
        off = [0]

        def take(size):
            seg = gp[:, off[0]:off[0] + size]
            off[0] += size
            return seg

        return (take(12 * D).reshape(NDEV, 2, 6 * D),
                take(6 * D).reshape(NDEV, 1, 6 * D),
                take(4 * D).reshape(NDEV, 2, 2, D),
                take(2 * D).reshape(NDEV, 2, D), take(HD).reshape(NDEV, 1, HD), take(HD).reshape(NDEV, 1, HD),
                take(2 * DFF).reshape(NDEV, 2, DFF), take(3 * CW).reshape(NDEV, 3, CW),
                take(6 * DFF).reshape(NDEV, 2, 3, DFF), take(D).reshape(NDEV, D), take(1))

    gp_all = g_pack.reshape(NDEV, -1)
    dmod_all, dmodc_all = split(gp_all)[:2]

    dmodc_sum = dmodc_all[0]
    for dev in range(1, NDEV):
        dmodc_sum = dmodc_sum + dmodc_all[dev]
    my_cols = lambda a: lax.dynamic_slice_in_dim(a, me * acols, acols, axis=a.ndim - 1)
    rows0 = jnp.concatenate([my_cols(dmod_all[:, 0]), my_cols(dmodc_sum), jnp.zeros((7, acols), F32)], axis=0)
    rows1 = jnp.concatenate([my_cols(dmod_all[:, 1]), jnp.zeros((8, acols), F32)], axis=0)
    d_ada = jnp.stack([_mm(mraw, rows, ta=True, silu_a=True, name=f"ada_dw_{l}", tm=512, tn=256, tk=16)
                       for l, rows in enumerate((rows0, rows1))])
    dscc_part = _mm(rows0, ada_w[0], tb=True, name="ada_dcctx", tm=16, tn=512, tk=256)
    (g_dscc,) = _all_gather([dscc_part[8:16]], name="gather_dcctx")

    attn_shards = [d_win_t.reshape(NDEV, -1, D), d_wout.reshape(NDEV, D // NDEV, D)]
    attn_shards, g_dscc = lax.optimization_barrier((attn_shards, g_dscc))
    h_ga = _exchange_start(attn_shards, scatter=True, name="grads_attn_start")
    (dmod_all, dmodc_all, dmix_all, dffn_all, dqg_all, dkg_all, dfcb_all, dconvw_all, dfcw_all, dpscale_all,
     loss_all) = split(gp_all + h_ga["token"][0, 0])
    d_ada = d_ada + h_ga["token"][0, 0]
    loss = loss_all[0, 0]
    for dev in range(1, NDEV):
        loss = loss + loss_all[dev, 0]

    outs = {}

    def put(nm, res):
        outs["grad_" + nm], outs["delta_" + nm], outs["new_m_" + nm], outs["new_v_" + nm] = res

    dmodc_pad = jnp.concatenate([dmodc_all, jnp.zeros_like(dmodc_all)], axis=1)
    put("ada_b", _adamw_nd(jnp.concatenate([dmod_all, dmodc_pad], axis=0), ada_b, m_ada_b, v_ada_b, name="adam_ada_b"))
    put("mix_norm", _adamw_nd(jnp.concatenate([dmix_all[:, 0], dmix_all[:, 1]], axis=0), mix_norm, m_mix_norm,
                              v_mix_norm, name="adam_mix_norm"))
    put("ffn_norm", _adamw_nd(dffn_all, ffn_norm, m_ffn_norm, v_ffn_norm, name="adam_ffn_norm"))
    put("even_q_gain", _adamw_nd(dqg_all, even_q_gain, m_even_q_gain, v_even_q_gain, name="adam_q_gain"))
    put("even_k_gain", _adamw_nd(dkg_all, even_k_gain, m_even_k_gain, v_even_k_gain, name="adam_k_gain"))
    put("ffn_conv_b", _adamw_nd(dfcb_all, ffn_conv_b, m_ffn_conv_b, v_ffn_conv_b, name="adam_ffn_conv_b"))
    my_convw = lax.dynamic_slice_in_dim(dconvw_all, me * ecw, ecw, axis=2)[:, None]
    put("even_conv_w", _adamw_nd(my_convw, even_conv_w, m_even_conv_w, v_even_conv_w, name="adam_even_conv_w"))
    my_fcw = lax.dynamic_slice_in_dim(dfcw_all, me * fcw, fcw, axis=3)
    put("ffn_conv_w", _adamw_nd(my_fcw, ffn_conv_w, m_ffn_conv_w, v_ffn_conv_w, name="adam_ffn_conv_w"))
    my_ps = lax.dynamic_slice_in_dim(dpscale_all, me * (D // NDEV), D // NDEV, axis=1)[:, None]
    put("odd_pool_scale", _adamw_nd(my_ps, odd_pool_scale, m_odd_pool_scale, v_odd_pool_scale, name="adam_pool_scale"))

    put("ada_w", _adamw_nd(d_ada[None], ada_w, m_ada_w, v_ada_w, name="adam_ada_w"))
    put("c_ctx", _adamw_nd(g_dscc[:, 0:1, :].reshape(NDEV, D), c_ctx, m_c_ctx, v_c_ctx, name="adam_c_ctx",
                           silu_grad_of=c_ctx))

    r_pool, r_up1, r_down1 = _exchange_wait(h_g1, outs["grad_ada_b"], name="grads1_wait")
    r_up0, r_down0 = _exchange_wait(h_g0, outs["grad_mix_norm"], name="grads0_wait")
    put("odd_pool_w", _adamw_nd(r_pool[:, None], odd_pool_w, m_odd_pool_w, v_odd_pool_w, name="adam_pool_w"))
    put("ffn_w_up", [tr(a) for a in _adamw_nd([r_up0, r_up1], tr(ffn_w_up), tr(m_ffn_w_up), tr(v_ffn_w_up),
                                              name="adam_w_up")])
    put("ffn_w_down", _adamw_nd([r_down0, r_down1], ffn_w_down, m_ffn_w_down, v_ffn_w_down, name="adam_w_down"))
    r_win, r_wout = _exchange_wait(h_ga, outs["grad_ffn_w_down"], name="grads_attn_wait")
    put("even_w_in", [tr(a) for a in _adamw_nd(r_win[:, None], tr(even_w_in), tr(m_even_w_in), tr(v_even_w_in),
                                               name="adam_w_in")])
    put("even_w_out", _adamw_nd(r_wout[:, None], even_w_out, m_even_w_out, v_even_w_out, name="adam_w_out"))

    names = ["c_ctx", "ada_w", "ada_b", "mix_norm", "ffn_norm", "even_w_in", "even_q_gain", "even_k_gain",
             "even_conv_w", "even_w_out", "odd_pool_w", "odd_pool_scale", "ffn_w_up", "ffn_conv_w", "ffn_conv_b",
             "ffn_w_down"]
    result = [loss, grad_x[None]]
    for kind in ("grad_", "delta_", "new_m_", "new_v_"):
        result += [outs[kind + nm] for nm in names]
    return tuple(result)
```

```python
import functools
import math

import jax
import jax.numpy as jnp
from jax import lax
from jax.experimental import pallas as pl
from jax.experimental.pallas import tpu as pltpu

F32 = jnp.float32
BF16 = jnp.bfloat16

D = 1024
HD = 128
NQ = 4
NKV = 2
AW = NQ * HD
CW = D - AW
DFF = 2816
GRID_W = 64
ROPE_THETA = 10000.0
POOL_WINDOWS = (2, 4, 8, 16)
PG = D // 4
EPS = 1e-6
NDEV = 8
HALO = 8
MESH = pl.DeviceIdType.MESH

ADAM_LR = 0.001
ADAM_B1 = 0.9
ADAM_B2 = 0.999
ADAM_EPS = 1e-08
ADAM_WD = 0.01
ADAM_STEP = 10


def _pick(dim, prefs):
    for p in prefs:
        if dim % p == 0:
            return p
    return dim


def _params(*sem):
    return pltpu.CompilerParams(dimension_semantics=sem)


_NT = (((1,), (1,)), ((), ()))
_TN = (((0,), (0,)), ((), ()))
_SCALE = HD ** -0.5
_QSCALE = _SCALE * math.log2(math.e)
_LN2 = math.log(2.0)


def _mm(a_list, b, *, name, ta=False, tb=False, out_dtype=F32, silu_a=False, bias=None, tm=None, tn=None, tk=None):
    if not isinstance(a_list, (list, tuple)):
        a_list = [a_list]
    na = len(a_list)
    assert not (ta and na > 1)
    if ta:
        kdim, m = a_list[0].shape
        ks = [kdim]
    else:
        m = a_list[0].shape[0]
        ks = [a.shape[1] for a in a_list]
        kdim = sum(ks)
    n = b.shape[0] if tb else b.shape[1]
    assert (b.shape[1] if tb else b.shape[0]) == kdim
    kunit = math.gcd(*ks) if na > 1 else kdim
    tm = min(tm, m) if tm else _pick(m, (512, 256, 128, 64, 32, 16, 8))
    tn = min(tn, n) if tn else _pick(n, (512, 256, 128))
    tk = min(tk, kunit) if tk else _pick(kunit, (1024, 768, 512, 256, 128))
    assert m % tm == 0 and n % tn == 0 and all(k % tk == 0 for k in ks)
    nks = [k // tk for k in ks]
    starts = [sum(nks[:i]) for i in range(na)]
    nk = sum(nks)
    has_bias = bias is not None

    def body(*refs):
        a_refs = refs[:na]
        b_ref = refs[na]
        bias_ref = refs[na + 1] if has_bias else None
        o_ref = refs[na + 1 + has_bias]
        acc = refs[-1]
        k = pl.program_id(2)

        @pl.when(k == 0)
        def _():
            acc[...] = jnp.zeros_like(acc)

        bv = b_ref[...].astype(BF16)
        dn = (((0 if ta else 1,), (1 if tb else 0,)), ((), ()))
        for idx in range(na):
            def step(idx=idx):
                av = a_refs[idx][...]
                if silu_a:
                    av = av * jax.nn.sigmoid(av)
                acc[...] += lax.dot_general(av.astype(BF16), bv, dn, preferred_element_type=F32)
            if na == 1:
                step()
            else:
                pl.when((k >= starts[idx]) & (k < starts[idx] + nks[idx]))(step)

        @pl.when(k == nk - 1)
        def _():
            r = acc[...]
            if has_bias:
                r = r + bias_ref[...]
            o_ref[...] = r.astype(o_ref.dtype)

    in_specs = []
    for idx in range(na):
        if ta:
            in_specs.append(pl.BlockSpec((tk, tm), lambda i, j, k: (k, i)))
        else:
            lo, cnt = starts[idx], nks[idx]
            in_specs.append(pl.BlockSpec((tm, tk), lambda i, j, k, lo=lo, cnt=cnt: (i, jnp.clip(k - lo, 0, cnt - 1))))
    if tb:
        in_specs.append(pl.BlockSpec((tn, tk), lambda i, j, k: (j, k)))
    else:
        in_specs.append(pl.BlockSpec((tk, tn), lambda i, j, k: (k, j)))
    args = list(a_list) + [b]
    if has_bias:
        in_specs.append(pl.BlockSpec((1, tn), lambda i, j, k: (0, j)))
        args.append(bias)
    return pl.pallas_call(
        body, grid=(m // tm, n // tn, nk), in_specs=in_specs,
        out_specs=pl.BlockSpec((tm, tn), lambda i, j, k: (i, j)),
        out_shape=jax.ShapeDtypeStruct((m, n), out_dtype),
        scratch_shapes=[pltpu.VMEM((tm, tn), F32)], name=name,
        compiler_params=_params("parallel", "parallel", "arbitrary"))(*args)


def _mm_w(a_list, w, *, name, tb=False, tm=256, out_dtype=F32):
    if not isinstance(a_list, (list, tuple)):
        a_list = [a_list]
    na = len(a_list)
    m = a_list[0].shape[0]
    ks = [a.shape[1] for a in a_list]
    offs = [sum(ks[:i]) for i in range(na)]
    n = w.shape[0] if tb else w.shape[1]
    assert (w.shape[1] if tb else w.shape[0]) == sum(ks)
    tm = min(tm, m)
    assert m % tm == 0

    def body(*refs):
        a_refs, w_ref, o_ref = refs[:na], refs[na], refs[na + 1]
        acc = None
        for idx in range(na):
            av = a_refs[idx][...].astype(BF16)
            if tb:
                part = lax.dot_general(av, w_ref[:, offs[idx]:offs[idx] + ks[idx]], _NT, preferred_element_type=F32)
            else:
                part = jnp.dot(av, w_ref[offs[idx]:offs[idx] + ks[idx], :], preferred_element_type=F32)
            acc = part if acc is None else acc + part
        o_ref[...] = acc.astype(o_ref.dtype)

    in_specs = [pl.BlockSpec((tm, k), lambda i: (i, 0)) for k in ks] + [pl.BlockSpec(w.shape, lambda i: (0, 0))]
    return pl.pallas_call(
        body, grid=(m // tm,), in_specs=in_specs, out_specs=pl.BlockSpec((tm, n), lambda i: (i, 0)),
        out_shape=jax.ShapeDtypeStruct((m, n), out_dtype), name=name, compiler_params=_params("parallel"))(*a_list, w)


def _mm_w_ep(a_list, w, epilogue, row_in, vec_in, out_dtypes, sum_widths, *, name, tb=False, tm=256, sub=256):
    if not isinstance(a_list, (list, tuple)):
        a_list = [a_list]
    na, nr, nv, no, ns = len(a_list), len(row_in), len(vec_in), len(out_dtypes), len(sum_widths)
    m = a_list[0].shape[0]
    ks = [a.shape[1] for a in a_list]
    offs = [sum(ks[:i]) for i in range(na)]
    n = w.shape[0] if tb else w.shape[1]
    assert (w.shape[1] if tb else w.shape[0]) == sum(ks)
    tm = min(tm, m)
    sub = min(sub, tm)
    assert m % tm == 0 and tm % sub == 0

    def body(*refs):
        a_refs, w_ref = refs[:na], refs[na]
        row_refs = refs[na + 1:na + 1 + nr]
        vec_refs = refs[na + 1 + nr:na + 1 + nr + nv]
        out_refs = refs[na + 1 + nr + nv:na + 1 + nr + nv + no]
        sum_refs = refs[na + 1 + nr + nv + no:]

        @pl.when(pl.program_id(0) == 0)
        def _():
            for s_ref in sum_refs:
                s_ref[...] = jnp.zeros_like(s_ref)

        vecs = [v[...] for v in vec_refs]
        for r0 in range(0, tm, sub):
            acc = None
            for idx in range(na):
                av = a_refs[idx][r0:r0 + sub, :].astype(BF16)
                if tb:
                    part = lax.dot_general(av, w_ref[:, offs[idx]:offs[idx] + ks[idx]], _NT, preferred_element_type=F32)
                else:
                    part = jnp.dot(av, w_ref[offs[idx]:offs[idx] + ks[idx], :], preferred_element_type=F32)
                acc = part if acc is None else acc + part
            outs, sums = epilogue(acc, [r[r0:r0 + sub, :] for r in row_refs], vecs)
            for o_ref, o in zip(out_refs, outs):
                o_ref[r0:r0 + sub, :] = o.astype(o_ref.dtype)
            for s_ref, s in zip(sum_refs, sums):
                s_ref[...] += s

    row = pl.BlockSpec((tm, n), lambda i: (i, 0))
    in_specs = ([pl.BlockSpec((tm, k), lambda i: (i, 0)) for k in ks] + [pl.BlockSpec(w.shape, lambda i: (0, 0))]
                + [row] * nr + [_vec(n)] * nv)
    return pl.pallas_call(
        body, grid=(m // tm,), in_specs=in_specs, out_specs=[row] * no + [_vec(sw) for sw in sum_widths],
        out_shape=[jax.ShapeDtypeStruct((m, n), dt) for dt in out_dtypes]
        + [jax.ShapeDtypeStruct((1, sw), F32) for sw in sum_widths],
        name=name, compiler_params=_params("arbitrary" if ns else "parallel"))(*a_list, w, *row_in, *vec_in)


def _ep_norm_bwd(has_gate):
    def ep(dav, rows, vecs):
        xv = rows[0]
        gain, scv = vecs[0], vecs[1]
        r = lax.rsqrt(jnp.mean(xv * xv, axis=-1, keepdims=True) + EPS)
        xh = xv * r
        nrm = xh * gain
        dn = dav * (1.0 + scv)
        dxh = dn * gain
        dx = r * (dxh - xh * jnp.mean(dxh * xh, axis=-1, keepdims=True)) + rows[1]
        outs, sums = [dx], [_colsum(dav), _colsum(dav * nrm), _colsum(dn * xh)]
        if has_gate:
            outs.append(dx * vecs[2])
            sums.append(_colsum(dx * rows[2]))
        return outs, sums
    return ep


def _ep_loss(d):
    def ep(zv, rows, vecs):
        xv, tv = rows
        gv = vecs[0]
        diff = (xv + gv * zv) - tv
        dx = diff * (1.0 / d)
        part = 0.5 * jnp.sum(jnp.mean(diff * diff, axis=-1, keepdims=True), axis=0, keepdims=True)
        return [dx, dx * gv], [jnp.broadcast_to(part, (1, 128)), _colsum(dx * zv)]
    return ep


def _ep_resid(zv, rows, vecs):
    return [zv, rows[0] + vecs[0] * zv], []


def _ep_resid_norm(yv, rows, vecs):
    g, gain, scv, shv = vecs
    xv = rows[0] + g * yv
    r = lax.rsqrt(jnp.mean(xv * xv, axis=-1, keepdims=True) + EPS)
    return [yv, xv, ((xv * r) * gain) * (1.0 + scv) + shv], []


def _mm_tn(pairs, *, name, tk=1024, out_dtype=BF16, blocks=1, block=0, into=None):
    if not isinstance(pairs, list):
        pairs = [pairs]
    m, n = pairs[0][0].shape[1], pairs[0][1].shape[1]
    tks = [min(tk, a.shape[0]) for a, _ in pairs]
    nks = [a.shape[0] // t for (a, _), t in zip(pairs, tks)]
    assert all(a.shape[0] == b.shape[0] and a.shape[0] % t == 0 for (a, b), t in zip(pairs, tks))
    starts = [sum(nks[:i]) for i in range(len(pairs))]
    nk = sum(nks)

    def body(*refs):
        o_ref, acc = refs[-2], refs[-1]
        k = pl.program_id(0)

        @pl.when(k == 0)
        def _():
            acc[...] = jnp.zeros_like(acc)

        for idx in range(len(pairs)):
            a_ref, b_ref = refs[2 * idx], refs[2 * idx + 1]

            def step(a_ref=a_ref, b_ref=b_ref):
                acc[...] += lax.dot_general(a_ref[...].astype(BF16), b_ref[...].astype(BF16), _TN,
                                            preferred_element_type=F32)

            if len(pairs) == 1:
                step()
            else:
                pl.when((k >= starts[idx]) & (k < starts[idx] + nks[idx]))(step)

        @pl.when(k == nk - 1)
        def _():
            o_ref[...] = acc[...].astype(o_ref.dtype)

    in_specs, args = [], []
    for (a, b), t, lo, cnt in zip(pairs, tks, starts, nks):
        idx_map = lambda k, lo=lo, cnt=cnt: (jnp.clip(k - lo, 0, cnt - 1), 0)
        in_specs += [pl.BlockSpec((t, m), idx_map), pl.BlockSpec((t, n), idx_map)]
        args += [a, b]
    aliases = {}
    if into is not None:
        aliases = {len(args): 0}
        in_specs.append(pl.BlockSpec(memory_space=pl.ANY))
        args.append(into)
    return pl.pallas_call(
        body, grid=(nk,), in_specs=in_specs, out_specs=pl.BlockSpec((m, n), lambda k: (block, 0)),
        out_shape=jax.ShapeDtypeStruct((m * blocks, n), out_dtype), scratch_shapes=[pltpu.VMEM((m, n), F32)],
        input_output_aliases=aliases, name=name, compiler_params=_params("arbitrary"))(*args)


def _vec(d, col=None):
    if col is None:
        return pl.BlockSpec((1, d), lambda i, *_: (0, 0))
    return pl.BlockSpec((1, d), col)


def _halo_specs(tm, width, nrows, colblk=0, row_off=0):
    r = tm // HALO
    off = row_off // HALO
    last = nrows // HALO - 1
    prev = pl.BlockSpec((HALO, width), lambda i, *_: (off + jnp.maximum(i * r - 1, 0), colblk))
    nxt = pl.BlockSpec((HALO, width), lambda i, *_: (off + jnp.minimum((i + 1) * r, last), colblk))
    return prev, nxt


def _ext(prev_ref, main_ref, next_ref, i, ni):
    p = jnp.where(i > 0, prev_ref[...], 0.0)
    n = jnp.where(i < ni - 1, next_ref[...], 0.0)
    return jnp.concatenate([p, main_ref[...], n], axis=0)


def _sh(ext, k, tm):
    if k == 0:
        return ext[HALO:HALO + tm]
    rows = ext.shape[0]
    return pltpu.roll(ext, (-k) % rows, axis=0)[HALO:HALO + tm]


def _roll_rows(v, k):
    rows = v.shape[0]
    return pltpu.roll(v, (-k) % rows, axis=0) if k % rows else v


def _conv3(ext, w_ref, tm):
    return _sh(ext, -1, tm) * w_ref[0:1, :] + _sh(ext, 0, tm) * w_ref[1:2, :] + _sh(ext, 1, tm) * w_ref[2:3, :]


def _colsum(v):
    return jnp.sum(v, axis=0, keepdims=True)


def _acc_out(ref, i, val):
    @pl.when(i == 0)
    def _():
        ref[...] = jnp.zeros_like(ref)

    ref[...] += val


def _sigmoid(v):
    return jax.nn.sigmoid(v)


def _norm_mod(x, gain, sc, sh, *, name, y=None, g=None, ymul=None, tm=512):
    n, d = x.shape
    tm = min(tm, n)
    has_res = y is not None
    has_mul = ymul is not None

    def body(*refs):
        it = iter(refs)
        x_ref = next(it)
        y_ref = next(it) if has_res else None
        g_ref = next(it) if has_res else None
        m_ref = next(it) if has_mul else None
        gain_ref, sc_ref, sh_ref = next(it), next(it), next(it)
        xo_ref = next(it) if has_res else None
        a_ref = next(it)
        xv = x_ref[...]
        if has_res:
            yv = y_ref[...]
            if has_mul:
                yv = yv * m_ref[...]
            xv = xv + g_ref[...] * yv
            xo_ref[...] = xv
        r = lax.rsqrt(jnp.mean(xv * xv, axis=-1, keepdims=True) + EPS)
        nrm = (xv * r) * gain_ref[...]
        a_ref[...] = (nrm * (1.0 + sc_ref[...]) + sh_ref[...]).astype(BF16)

    row = pl.BlockSpec((tm, d), lambda i: (i, 0))
    in_specs, args = [row], [x]
    if has_res:
        in_specs += [row, _vec(d)]
        args += [y, g]
    if has_mul:
        in_specs.append(_vec(d))
        args.append(ymul)
    in_specs += [_vec(d)] * 3
    args += [gain, sc, sh]
    out_specs, out_shape = [], []
    if has_res:
        out_specs.append(row)
        out_shape.append(jax.ShapeDtypeStruct((n, d), F32))
    out_specs.append(row)
    out_shape.append(jax.ShapeDtypeStruct((n, d), BF16))
    res = pl.pallas_call(body, grid=(n // tm,), in_specs=in_specs, out_specs=out_specs, out_shape=out_shape,
                         name=name, compiler_params=_params("parallel"))(*args)
    return res if has_res else res[0]


def _norm_mod_bwd(da, x, gain, sc, *, name, dres=None, gate_y=None, gate_g=None, tm=512):
    n, d = x.shape
    tm = min(tm, n)
    has_res = dres is not None
    has_gate = gate_y is not None

    def body(*refs):
        it = iter(refs)
        da_ref, x_ref = next(it), next(it)
        r_ref = next(it) if has_res else None
        y_ref = next(it) if has_gate else None
        g_ref = next(it) if has_gate else None
        gain_ref, sc_ref = next(it), next(it)
        dx_ref, dsh_ref, dsc_ref, dgn_ref = next(it), next(it), next(it), next(it)
        dy_ref = next(it) if has_gate else None
        dg_ref = next(it) if has_gate else None
        i = pl.program_id(0)
        xv = x_ref[...]
        dav = da_ref[...]
        r = lax.rsqrt(jnp.mean(xv * xv, axis=-1, keepdims=True) + EPS)
        xh = xv * r
        nrm = xh * gain_ref[...]
        dn = dav * (1.0 + sc_ref[...])
        dxh = dn * gain_ref[...]
        dx = r * (dxh - xh * jnp.mean(dxh * xh, axis=-1, keepdims=True))
        if has_res:
            dx = dx + r_ref[...]
        dx_ref[...] = dx
        _acc_out(dsh_ref, i, _colsum(dav))
        _acc_out(dsc_ref, i, _colsum(dav * nrm))
        _acc_out(dgn_ref, i, _colsum(dn * xh))
        if has_gate:
            dy_ref[...] = (dx * g_ref[...]).astype(BF16)
            _acc_out(dg_ref, i, _colsum(dx * y_ref[...]))

    row = pl.BlockSpec((tm, d), lambda i: (i, 0))
    in_specs, args = [row, row], [da, x]
    if has_res:
        in_specs.append(row)
        args.append(dres)
    if has_gate:
        in_specs += [row, _vec(d)]
        args += [gate_y, gate_g]
    in_specs += [_vec(d)] * 2
    args += [gain, sc]
    vec_shape = jax.ShapeDtypeStruct((1, d), F32)
    out_specs = [row, _vec(d), _vec(d), _vec(d)]
    out_shape = [jax.ShapeDtypeStruct((n, d), F32), vec_shape, vec_shape, vec_shape]
    if has_gate:
        out_specs += [row, _vec(d)]
        out_shape += [jax.ShapeDtypeStruct((n, d), BF16), vec_shape]
    return pl.pallas_call(
        body, grid=(n // tm,), in_specs=in_specs, out_specs=out_specs, out_shape=out_shape,
        name=name, compiler_params=_params("arbitrary"))(*args)


def _ffn_up_glu(f, w_up, cw, cb, *, name, tm=256, tc=256):
    n, d = f.shape
    tm = min(tm, n)
    ni = n // tm
    nc = DFF // tc
    halo = 16
    rows = tm + 2 * halo
    r = tm // halo
    last = n // halo - 1

    def body(f_ref, fp_ref, fn_ref, w_ref, cw_ref, cb_ref, u_ref, gc_ref, h_ref):
        i = pl.program_id(0)
        a = f_ref[...]
        aext = jnp.concatenate([jnp.where(i > 0, fp_ref[...], jnp.zeros_like(fp_ref[...])), a,
                                jnp.where(i < ni - 1, fn_ref[...], jnp.zeros_like(fn_ref[...]))], axis=0)
        for j in range(nc):
            cols = slice(j * tc, (j + 1) * tc)
            vcols = slice(DFF + j * tc, DFF + (j + 1) * tc)
            gext = jnp.dot(aext, w_ref[:, cols], preferred_element_type=F32)
            val = jnp.dot(a, w_ref[:, vcols], preferred_element_type=F32)
            gate = gext[halo:halo + tm]
            gc = (pltpu.roll(gext, 1, axis=0)[halo:halo + tm] * cw_ref[0:1, cols] + gate * cw_ref[1:2, cols]
                  + pltpu.roll(gext, rows - 1, axis=0)[halo:halo + tm] * cw_ref[2:3, cols]) + cb_ref[:, cols]
            u_ref[:, cols] = gate
            u_ref[:, vcols] = val
            gc_ref[:, cols] = gc
            h_ref[:, cols] = (gc * _sigmoid(gc) * val).astype(BF16)

    return pl.pallas_call(
        body, grid=(ni,),
        in_specs=[pl.BlockSpec((tm, d), lambda i: (i, 0)),
                  pl.BlockSpec((halo, d), lambda i: (jnp.maximum(i * r - 1, 0), 0)),
                  pl.BlockSpec((halo, d), lambda i: (jnp.minimum((i + 1) * r, last), 0)),
                  pl.BlockSpec(w_up.shape, lambda i: (0, 0)), pl.BlockSpec((3, DFF), lambda i: (0, 0)),
                  pl.BlockSpec((1, DFF), lambda i: (0, 0))],
        out_specs=[pl.BlockSpec((tm, 2 * DFF), lambda i: (i, 0)), pl.BlockSpec((tm, DFF), lambda i: (i, 0)),
                   pl.BlockSpec((tm, DFF), lambda i: (i, 0))],
        out_shape=[jax.ShapeDtypeStruct((n, 2 * DFF), F32), jax.ShapeDtypeStruct((n, DFF), F32),
                   jax.ShapeDtypeStruct((n, DFF), BF16)], name=name,
        compiler_params=_params("parallel"))(f, f, f, w_up, cw, cb)


def _ffn_down_glu_bwd(dz, w_down, u, gc, cw, *, name, tm=256, tc=256):
    n, d = dz.shape
    tm = min(tm, n)
    ni = n // tm
    nc = DFF // tc
    rows = tm + 2 * HALO

    def body(z_ref, zp_ref, zn_ref, w_ref, u_ref, vp_ref, vn_ref, c_ref, cp_ref, cn_ref, cw_ref,
             dg_ref, dv_ref, dcw_ref, dcb_ref):
        i = pl.program_id(0)

        @pl.when(i == 0)
        def _():
            dcw_ref[...] = jnp.zeros_like(dcw_ref)
            dcb_ref[...] = jnp.zeros_like(dcb_ref)

        zext = jnp.concatenate([jnp.where(i > 0, zp_ref[...], jnp.zeros_like(zp_ref[...])), z_ref[...],
                                jnp.where(i < ni - 1, zn_ref[...], jnp.zeros_like(zn_ref[...]))], axis=0)
        for j in range(nc):
            cols = slice(j * tc, (j + 1) * tc)
            vcols = slice(DFF + j * tc, DFF + (j + 1) * tc)
            dh = lax.dot_general(zext, w_ref[cols, :], _NT, preferred_element_type=F32)[HALO:HALO + rows]
            gcx = jnp.concatenate([cp_ref[:, cols], c_ref[:, cols], cn_ref[:, cols]], axis=0)
            vext = jnp.concatenate([vp_ref[:, cols], u_ref[:, vcols], vn_ref[:, cols]], axis=0)
            sg = _sigmoid(gcx)
            dgc = dh * vext * (sg * (1.0 + gcx * (1.0 - sg)))
            dv_ref[:, cols] = (dh[HALO:HALO + tm] * (gcx[HALO:HALO + tm] * sg[HALO:HALO + tm])).astype(BF16)
            d_next = pltpu.roll(dgc, rows - 1, axis=0)[HALO:HALO + tm]
            d_prev = pltpu.roll(dgc, 1, axis=0)[HALO:HALO + tm]
            d_here = dgc[HALO:HALO + tm]
            dg_ref[:, cols] = (d_next * cw_ref[0:1, cols] + d_here * cw_ref[1:2, cols]
                               + d_prev * cw_ref[2:3, cols]).astype(BF16)
            gate = u_ref[:, cols]
            dcw_ref[:, cols] += jnp.concatenate([_colsum(d_next * gate), _colsum(d_here * gate),
                                                 _colsum(d_prev * gate)], axis=0)
            dcb_ref[:, cols] += _colsum(d_here)

    def trio(width, halo, tile_width=None, colblk=0):
        r, last = tm // halo, n // halo - 1
        return [pl.BlockSpec((tm, tile_width or width), lambda i: (i, 0)),
                pl.BlockSpec((halo, width), lambda i: (jnp.maximum(i * r - 1, 0), colblk)),
                pl.BlockSpec((halo, width), lambda i: (jnp.minimum((i + 1) * r, last), colblk))]

    whole = lambda shape: pl.BlockSpec(shape, lambda i: (0, 0))
    return pl.pallas_call(
        body, grid=(ni,),
        in_specs=(trio(d, 16) + [whole(w_down.shape)] + trio(DFF, HALO, tile_width=2 * DFF, colblk=1)
                  + trio(DFF, HALO) + [whole((3, DFF))]),
        out_specs=[pl.BlockSpec((tm, DFF), lambda i: (i, 0)), pl.BlockSpec((tm, DFF), lambda i: (i, 0)),
                   whole((3, DFF)), whole((1, DFF))],
        out_shape=[jax.ShapeDtypeStruct((n, DFF), BF16), jax.ShapeDtypeStruct((n, DFF), BF16),
                   jax.ShapeDtypeStruct((3, DFF), F32), jax.ShapeDtypeStruct((1, DFF), F32)],
        name=name, compiler_params=_params("arbitrary"))(dz, dz, dz, w_down, u, u, u, gc, gc, gc, cw)


def _rope_tables(n):
    rows = n // GRID_W
    axis_dim = HD // 2
    inv_freq = jnp.power(ROPE_THETA, -jnp.arange(0, axis_dim, 2, dtype=F32) / axis_dim)
    ar = jnp.arange(rows, dtype=F32)[:, None] * inv_freq
    ac = jnp.arange(GRID_W, dtype=F32)[:, None] * inv_freq
    by_row = lambda a: jnp.repeat(a, GRID_W, axis=0)
    by_col = lambda a: jnp.tile(a, (rows, 1))
    cr, sr, cc, sc = by_row(jnp.cos(ar)), by_row(jnp.sin(ar)), by_col(jnp.cos(ac)), by_col(jnp.sin(ac))
    return jnp.concatenate([cr, cr, cc, cc], axis=1), jnp.concatenate([-sr, sr, -sc, sc], axis=1)


def _partner(v):
    lane = lax.broadcasted_iota(jnp.int32, v.shape, 1)
    return jnp.where((lane % 64) < 32, pltpu.roll(v, HD - 32, axis=1), pltpu.roll(v, 32, axis=1))


def _qkv_prep(p, q_gain, k_gain, cs, sn, *, name, has_q, kv_col, kv_rows=None, kv_row_off=0, kv_into=None, tm=256):
    n = p.shape[0]
    rope = cs is not None
    kv_rows = kv_rows or n
    rb = kv_row_off // tm

    def body(*refs):
        it = iter(refs)
        q_ref = next(it) if has_q else None
        kv_ref = next(it)
        qg_ref, kg_ref = next(it), next(it)
        cs_ref = next(it) if rope else None
        sn_ref = next(it) if rope else None
        if kv_into is not None:
            next(it), next(it)
        qo_ref = next(it) if has_q else None
        ko_ref, vo_ref = next(it), next(it)

        def norm_rope(xh, gain, mul=None):
            r = lax.rsqrt(jnp.mean(xh * xh, axis=-1, keepdims=True) + EPS)
            xn = (xh * r) * gain
            if rope:
                xn = xn * cs_ref[...] + _partner(xn) * sn_ref[...]
            if mul is not None:
                xn = xn * mul
            return xn.astype(BF16)

        if has_q:
            for h in range(NQ):
                qo_ref[h] = norm_rope(q_ref[:, h * HD:(h + 1) * HD], qg_ref[...], _QSCALE)
        for h in range(NKV):
            ko_ref[h] = norm_rope(kv_ref[:, h * HD:(h + 1) * HD], kg_ref[...])
            vo_ref[h] = kv_ref[:, (NKV + h) * HD:(NKV + h + 1) * HD].astype(BF16)

    in_specs, args = [], []
    if has_q:
        in_specs.append(pl.BlockSpec((tm, AW), lambda i: (i, 0)))
        args.append(p)
    in_specs += [pl.BlockSpec((tm, 2 * NKV * HD), lambda i: (i, kv_col)), _vec(HD), _vec(HD)]
    args += [p, q_gain, k_gain]
    if rope:
        in_specs += [pl.BlockSpec((tm, HD), lambda i: (i, 0))] * 2
        args += [cs, sn]
    out_specs, out_shape = [], []
    if has_q:
        out_specs.append(pl.BlockSpec((NQ, tm, HD), lambda i: (0, i, 0)))
        out_shape.append(jax.ShapeDtypeStruct((NQ, n, HD), BF16))
    out_specs += [pl.BlockSpec((NKV, tm, HD), lambda i: (0, rb + i, 0))] * 2
    out_shape += [jax.ShapeDtypeStruct((NKV, kv_rows, HD), BF16)] * 2
    aliases = {}
    if kv_into is not None:
        aliases = {len(args): int(has_q), len(args) + 1: int(has_q) + 1}
        in_specs += [pl.BlockSpec(memory_space=pl.ANY)] * 2
        args += list(kv_into)
    return pl.pallas_call(body, grid=(n // tm,), in_specs=in_specs, out_specs=out_specs, out_shape=out_shape,
                          input_output_aliases=aliases, name=name, compiler_params=_params("parallel"))(*args)


def _in_proj_qkv(a, w_in_t, q_gain, k_gain, cs, sn, conv_w, kv_into, *, name, kv_row_off, tm=256):
    n, d = a.shape
    nproj = w_in_t.shape[0]
    nqkv = AW + 2 * NKV * HD
    rb = kv_row_off // tm
    ni = n // tm
    halo = 16
    rows = tm + 2 * halo
    r = tm // halo
    last = n // halo - 1

    def body(a_ref, ap_ref, an_ref, w_ref, qg_ref, kg_ref, cs_ref, sn_ref, cw_ref, _k_in, _v_in,
             p_ref, qo_ref, ko_ref, vo_ref, conv_ref):
        i = pl.program_id(0)
        av = a_ref[...]
        aext = jnp.concatenate([jnp.where(i > 0, ap_ref[...], jnp.zeros_like(ap_ref[...])), av,
                                jnp.where(i < ni - 1, an_ref[...], jnp.zeros_like(an_ref[...]))], axis=0)
        qkv = lax.dot_general(av, w_ref[0:nqkv, :], _NT, preferred_element_type=F32)
        p_ref[:, 0:nqkv] = qkv
        cext = lax.dot_general(aext, w_ref[nqkv:nproj, :], _NT, preferred_element_type=F32)
        p_ref[:, nqkv:nproj] = cext[halo:halo + tm]
        hext = cext[:, CW:2 * CW] * cext[:, 2 * CW:3 * CW]
        cv3 = (pltpu.roll(hext, 1, axis=0)[halo:halo + tm] * cw_ref[0:1, :] + hext[halo:halo + tm] * cw_ref[1:2, :]
               + pltpu.roll(hext, rows - 1, axis=0)[halo:halo + tm] * cw_ref[2:3, :])
        conv_ref[...] = (cext[halo:halo + tm, 0:CW] * cv3).astype(BF16)

        def norm_rope(xh, gain, mul=None):
            r = lax.rsqrt(jnp.mean(xh * xh, axis=-1, keepdims=True) + EPS)
            xn = (xh * r) * gain
            xn = xn * cs_ref[...] + _partner(xn) * sn_ref[...]
            if mul is not None:
                xn = xn * mul
            return xn.astype(BF16)

        for h in range(NQ):
            qo_ref[h] = norm_rope(qkv[:, h * HD:(h + 1) * HD], qg_ref[...], _QSCALE)
        for h in range(NKV):
            ko_ref[h] = norm_rope(qkv[:, AW + h * HD:AW + (h + 1) * HD], kg_ref[...])
            vo_ref[h] = qkv[:, AW + (NKV + h) * HD:AW + (NKV + h + 1) * HD].astype(BF16)

    kv_rows = kv_into[0].shape[1]
    tab = pl.BlockSpec((tm, HD), lambda i: (i, 0))
    any_spec = pl.BlockSpec(memory_space=pl.ANY)
    kv_spec = pl.BlockSpec((NKV, tm, HD), lambda i: (0, rb + i, 0))
    return pl.pallas_call(
        body, grid=(ni,),
        in_specs=[pl.BlockSpec((tm, d), lambda i: (i, 0)),
                  pl.BlockSpec((halo, d), lambda i: (jnp.maximum(i * r - 1, 0), 0)),
                  pl.BlockSpec((halo, d), lambda i: (jnp.minimum((i + 1) * r, last), 0)),
                  pl.BlockSpec(w_in_t.shape, lambda i: (0, 0)), _vec(HD), _vec(HD), tab, tab,
                  pl.BlockSpec((3, CW), lambda i: (0, 0)), any_spec, any_spec],
        out_specs=[pl.BlockSpec((tm, nproj), lambda i: (i, 0)), pl.BlockSpec((NQ, tm, HD), lambda i: (0, i, 0)),
                   kv_spec, kv_spec, pl.BlockSpec((tm, CW), lambda i: (i, 0))],
        out_shape=[jax.ShapeDtypeStruct((n, nproj), F32), jax.ShapeDtypeStruct((NQ, n, HD), BF16),
                   jax.ShapeDtypeStruct((NKV, kv_rows, HD), BF16), jax.ShapeDtypeStruct((NKV, kv_rows, HD), BF16),
                   jax.ShapeDtypeStruct((n, CW), BF16)],
        input_output_aliases={9: 2, 10: 3}, name=name,
        compiler_params=_params("parallel"))(a, a, a, w_in_t, q_gain, k_gain, cs, sn, conv_w, *kv_into)


def _qkv_bwd(p, dq, dk, dv, q_gain, k_gain, cs, sn, *, name, has_q, kv_col, kv_row_off, tm=256):
    n = p.shape[0]
    rope = cs is not None
    rb = kv_row_off // tm

    def body(*refs):
        it = iter(refs)
        q_ref = next(it) if has_q else None
        kv_ref = next(it)
        dq_ref = next(it) if has_q else None
        dk_ref, dv_ref = next(it), next(it)
        qg_ref, kg_ref = next(it), next(it)
        cs_ref = next(it) if rope else None
        sn_ref = next(it) if rope else None
        dp_ref, dqg_ref, dkg_ref = next(it), next(it), next(it)
        i = pl.program_id(0)

        def back(xh, dout, gain):
            if rope:
                dout = dout * cs_ref[...] + _partner(dout * sn_ref[...])
            r = lax.rsqrt(jnp.mean(xh * xh, axis=-1, keepdims=True) + EPS)
            xhat = xh * r
            dxh = dout * gain
            dx = r * (dxh - xhat * jnp.mean(dxh * xhat, axis=-1, keepdims=True))
            return dx, _colsum(dout * xhat)

        dqg = jnp.zeros((1, HD), F32)
        dkg = jnp.zeros((1, HD), F32)
        if has_q:
            for h in range(NQ):
                dx, dg = back(q_ref[:, h * HD:(h + 1) * HD], dq_ref[h], qg_ref[...])
                dp_ref[:, h * HD:(h + 1) * HD] = dx.astype(BF16)
                dqg = dqg + dg
        else:
            dp_ref[:, 0:AW] = jnp.zeros((tm, AW), BF16)
        for h in range(NKV):
            dx, dg = back(kv_ref[:, h * HD:(h + 1) * HD], dk_ref[h], kg_ref[...])
            dp_ref[:, AW + h * HD:AW + (h + 1) * HD] = dx.astype(BF16)
            dkg = dkg + dg
            dp_ref[:, AW + (NKV + h) * HD:AW + (NKV + h + 1) * HD] = dv_ref[h].astype(BF16)
        _acc_out(dqg_ref, i, dqg)
        _acc_out(dkg_ref, i, dkg)

    in_specs, args = [], []
    if has_q:
        in_specs.append(pl.BlockSpec((tm, AW), lambda i: (i, 0)))
        args.append(p)
    in_specs.append(pl.BlockSpec((tm, 2 * NKV * HD), lambda i: (i, kv_col)))
    args.append(p)
    if has_q:
        in_specs.append(pl.BlockSpec((NQ, tm, HD), lambda i: (0, i, 0)))
        args.append(dq)
    in_specs += [pl.BlockSpec((NKV, tm, HD), lambda i: (0, rb + i, 0))] * 2 + [_vec(HD), _vec(HD)]
    args += [dk, dv, q_gain, k_gain]
    if rope:
        in_specs += [pl.BlockSpec((tm, HD), lambda i: (i, 0))] * 2
        args += [cs, sn]
    return pl.pallas_call(
        body, grid=(n // tm,), in_specs=in_specs,
        out_specs=[pl.BlockSpec((tm, D), lambda i: (i, 0)), _vec(HD), _vec(HD)],
        out_shape=[jax.ShapeDtypeStruct((n, D), BF16), jax.ShapeDtypeStruct((1, HD), F32),
                   jax.ShapeDtypeStruct((1, HD), F32)],
        name=name, compiler_params=_params("arbitrary"))(*args)


def _out_proj_dx_conv_bwd(dy, w_out, p, conv_w, *, name, tm=256):
    n, d = dy.shape
    ni = n // tm
    rows = tm + 2 * HALO

    def body(z_ref, zp_ref, zn_ref, wo_ref, gb_ref, gbp_ref, gbn_ref, gc_ref, gcp_ref, gcn_ref, xi_ref, xip_ref,
             xin_ref, w_ref, do_ref, dp_ref, dw_ref):
        i = pl.program_id(0)
        zext = jnp.concatenate([jnp.where(i > 0, zp_ref[...], jnp.zeros_like(zp_ref[...])), z_ref[...],
                                jnp.where(i < ni - 1, zn_ref[...], jnp.zeros_like(zn_ref[...]))], axis=0)
        do_ref[...] = lax.dot_general(z_ref[...], wo_ref[0:AW, :], _NT, preferred_element_type=F32)
        dconv = lax.dot_general(zext, wo_ref[AW:D, :], _NT, preferred_element_type=F32)[HALO:HALO + rows]
        gcext = _ext(gcp_ref, gc_ref, gcn_ref, i, ni)
        xiext = _ext(xip_ref, xi_ref, xin_ref, i, ni)
        hext = gcext * xiext
        dcv = dconv * _ext(gbp_ref, gb_ref, gbn_ref, i, ni)
        dp_ref[:, 0:CW] = (dconv[HALO:HALO + tm] * _conv3(hext, w_ref, tm)).astype(BF16)
        dh = _sh(dcv, 1, tm) * w_ref[0:1, :] + _sh(dcv, 0, tm) * w_ref[1:2, :] + _sh(dcv, -1, tm) * w_ref[2:3, :]
        dp_ref[:, CW:2 * CW] = (dh * xi_ref[...]).astype(BF16)
        dp_ref[:, 2 * CW:3 * CW] = (dh * gc_ref[...]).astype(BF16)
        dcv_t = dcv[HALO:HALO + tm]
        dw = jnp.concatenate([_colsum(dcv_t * _sh(hext, -1, tm)), _colsum(dcv_t * _sh(hext, 0, tm)),
                              _colsum(dcv_t * _sh(hext, 1, tm))], axis=0)
        _acc_out(dw_ref, i, dw)

    def trio(colblk):
        prev, nxt = _halo_specs(tm, CW, n, colblk=colblk)
        return [pl.BlockSpec((tm, CW), lambda i: (i, colblk)), prev, nxt]

    r16, last16 = tm // 16, n // 16 - 1
    zspecs = [pl.BlockSpec((tm, d), lambda i: (i, 0)),
              pl.BlockSpec((16, d), lambda i: (jnp.maximum(i * r16 - 1, 0), 0)),
              pl.BlockSpec((16, d), lambda i: (jnp.minimum((i + 1) * r16, last16), 0))]
    return pl.pallas_call(
        body, grid=(ni,),
        in_specs=zspecs + [pl.BlockSpec(w_out.shape, lambda i: (0, 0))] + trio(2) + trio(3) + trio(4)
        + [pl.BlockSpec((3, CW), lambda i: (0, 0))],
        out_specs=[pl.BlockSpec((tm, AW), lambda i: (i, 0)), pl.BlockSpec((tm, 3 * CW), lambda i: (i, 0)),
                   pl.BlockSpec((3, CW), lambda i: (0, 0))],
        out_shape=[jax.ShapeDtypeStruct((n, AW), F32), jax.ShapeDtypeStruct((n, 3 * CW), BF16),
                   jax.ShapeDtypeStruct((3, CW), F32)],
        name=name, compiler_params=_params("arbitrary"))(dy, dy, dy, w_out, p, p, p, p, p, p, p, p, p, conv_w)


def _attn_fwd(q, k, v, *, name, bq=512, sub=256):
    n = q.shape[1]
    t = k.shape[1]
    bq = min(bq, n)
    sub = min(sub, 2 * bq)

    def body(q_ref, k_ref, v_ref, o_ref, lse_ref):
        q2 = q_ref[...].reshape(2 * bq, HD)
        outs, lses = [], []
        for r0 in range(0, 2 * bq, sub):
            s = lax.dot_general(q2[r0:r0 + sub], k_ref[0], _NT, preferred_element_type=F32)
            m = jnp.max(s, axis=-1, keepdims=True)
            pv = jnp.exp2(s - m)
            l = jnp.sum(pv, axis=-1, keepdims=True)
            outs.append(jnp.dot(pv.astype(BF16), v_ref[0], preferred_element_type=F32) / l)
            lses.append(m + jnp.log2(l))
        out = jnp.concatenate(outs, axis=0)
        o_ref[:, 0:HD] = out[0:bq]
        o_ref[:, HD:2 * HD] = out[bq:2 * bq]
        lse_ref[...] = jnp.concatenate(lses, axis=0).reshape(2, bq, 1)

    kspec = pl.BlockSpec((1, t, HD), lambda h, i: (h, 0, 0))
    return pl.pallas_call(
        body, grid=(NKV, n // bq),
        in_specs=[pl.BlockSpec((2, bq, HD), lambda h, i: (h, i, 0)), kspec, kspec],
        out_specs=[pl.BlockSpec((bq, 2 * HD), lambda h, i: (i, h)), pl.BlockSpec((2, bq, 1), lambda h, i: (h, i, 0))],
        out_shape=[jax.ShapeDtypeStruct((n, AW), F32), jax.ShapeDtypeStruct((NQ, n, 1), F32)],
        name=name, compiler_params=_params("parallel", "parallel"))(q, k, v)


def _attn_bwd(q, k, v, dcat, o, lse, *, name, bq=256):
    n = q.shape[1]
    t = k.shape[1]
    bq = min(bq, n)

    def body(q_ref, k_ref, v_ref, dc_ref, o_ref, lse_ref, dq_ref, dk_ref, dv_ref):
        @pl.when(pl.program_id(1) == 0)
        def _():
            dk_ref[...] = jnp.zeros_like(dk_ref)
            dv_ref[...] = jnp.zeros_like(dv_ref)

        q2 = q_ref[...].reshape(2 * bq, HD)
        do_f = jnp.concatenate([dc_ref[:, 0:HD], dc_ref[:, HD:2 * HD]], axis=0)
        o_f = jnp.concatenate([o_ref[:, 0:HD], o_ref[:, HD:2 * HD]], axis=0)
        delta = jnp.sum(do_f * o_f, axis=-1, keepdims=True)
        do2 = do_f.astype(BF16)
        s = lax.dot_general(q2, k_ref[0], _NT, preferred_element_type=F32)
        pv = jnp.exp2(s - lse_ref[...].reshape(2 * bq, 1))
        dp = lax.dot_general(do2, v_ref[0], _NT, preferred_element_type=F32)
        ds = (pv * (dp - delta)).astype(BF16)
        dq_ref[...] = (jnp.dot(ds, k_ref[0], preferred_element_type=F32) * _SCALE).reshape(2, bq, HD)
        dk_ref[0] += lax.dot_general(ds, q2, _TN, preferred_element_type=F32) * _LN2
        dv_ref[0] += lax.dot_general(pv.astype(BF16), do2, _TN, preferred_element_type=F32)

    qspec = pl.BlockSpec((2, bq, HD), lambda h, i: (h, i, 0))
    kspec = pl.BlockSpec((1, t, HD), lambda h, i: (h, 0, 0))
    sspec = pl.BlockSpec((2, bq, 1), lambda h, i: (h, i, 0))
    cspec = pl.BlockSpec((bq, 2 * HD), lambda h, i: (i, h))
    return pl.pallas_call(
        body, grid=(NKV, n // bq), in_specs=[qspec, kspec, kspec, cspec, cspec, sspec], out_specs=[qspec, kspec, kspec],
        out_shape=[jax.ShapeDtypeStruct((NQ, n, HD), F32), jax.ShapeDtypeStruct((NKV, t, HD), F32),
                   jax.ShapeDtypeStruct((NKV, t, HD), F32)],
        name=name, compiler_params=_params("parallel", "arbitrary"))(q, k, v, dcat, o, lse)


def _window_sums(ext, w):
    s, step = ext, 1
    while step < w:
        s = s + _roll_rows(s, step)
        step *= 2
    return s


def _pool_counts(i, tm, n, w, rows, first):
    t = i * tm - HALO + first + lax.broadcasted_iota(jnp.int32, (rows, 1), 0)
    lo = jnp.clip(t - w // 2, 0, n)
    hi = jnp.clip(t + w - w // 2, 0, n)
    return jnp.maximum(hi - lo, 1).astype(F32)


def _norm_mod_ext(xext, gain_ref, sc_ref, sh_ref, i, tm, n):
    rows = xext.shape[0]
    t = i * tm - HALO + lax.broadcasted_iota(jnp.int32, (rows, 1), 0)
    inside = (t >= 0) & (t < n)
    r = lax.rsqrt(jnp.mean(xext * xext, axis=-1, keepdims=True) + EPS)
    xh = xext * r
    a = (xh * gain_ref[...]) * (1.0 + sc_ref[...]) + sh_ref[...]
    return jnp.where(inside, a, 0.0), r, xh


def _pool_fwd(x, gain, sc, sh, pool_w, *, name, tm=256):
    n, d = x.shape
    ni = n // tm

    def body(x_ref, xp_ref, xn_ref, gain_ref, sc_ref, sh_ref, w_ref, o_ref):
        i = pl.program_id(0)
        xext = _ext(xp_ref, x_ref, xn_ref, i, ni)
        aext, _, _ = _norm_mod_ext(xext, gain_ref, sc_ref, sh_ref, i, tm, n)
        for gi, w in enumerate(POOL_WINDOWS):
            ag = aext[:, gi * PG:(gi + 1) * PG]
            mean = _sh(_window_sums(ag, w), -(w // 2), tm) / _pool_counts(i, tm, n, w, tm, HALO)
            pooled = mean - ag[HALO:HALO + tm]
            o_ref[:, gi * PG:(gi + 1) * PG] = jnp.dot(pooled.astype(BF16), w_ref[gi], preferred_element_type=F32)

    row = pl.BlockSpec((tm, d), lambda i: (i, 0))
    prev, nxt = _halo_specs(tm, d, n)
    return pl.pallas_call(
        body, grid=(ni,),
        in_specs=[row, prev, nxt, _vec(d), _vec(d), _vec(d), pl.BlockSpec((4, PG, PG), lambda i: (0, 0, 0))],
        out_specs=row, out_shape=jax.ShapeDtypeStruct((n, d), F32),
        name=name, compiler_params=_params("parallel"))(x, x, x, gain, sc, sh, pool_w)


def _pool_bwd(dxo, mixed, x, g, scale, gain, sc, sh, pool_w, zprev, gprev, *, name, tm=256):
    n, d = x.shape
    ni = n // tm

    def body(dx_ref, dxp_ref, dxn_ref, mx_ref, x_ref, xp_ref, xn_ref, g_ref, s_ref, gain_ref, sc_ref, sh_ref, w_ref,
             zp_ref, gp_ref, dxi_ref, dw_ref, dg_ref, dsl_ref, dsh_ref, dsc_ref, dgn_ref, dzp_ref, dgp_ref):
        i = pl.program_id(0)

        @pl.when(i == 0)
        def _():
            dw_ref[...] = jnp.zeros_like(dw_ref)

        dxo_t = dx_ref[...]
        mixed_t = mx_ref[...]
        dy_t = dxo_t * g_ref[...]
        _acc_out(dg_ref, i, _colsum(dxo_t * (mixed_t * s_ref[...])))
        _acc_out(dsl_ref, i, _colsum(dy_t * mixed_t))
        dmixed = (_ext(dxp_ref, dx_ref, dxn_ref, i, ni) * g_ref[...]) * s_ref[...]
        xext = _ext(xp_ref, x_ref, xn_ref, i, ni)
        aext, rext, xhext = _norm_mod_ext(xext, gain_ref, sc_ref, sh_ref, i, tm, n)
        rows = tm + 2 * HALO
        da_parts = []
        for gi, w in enumerate(POOL_WINDOWS):
            sl = slice(gi * PG, (gi + 1) * PG)
            ag = aext[:, sl]
            mean = _sh(_window_sums(ag, w), -(w // 2), tm) / _pool_counts(i, tm, n, w, tm, HALO)
            pooled = (mean - ag[HALO:HALO + tm]).astype(BF16)
            dmg = dmixed[:, sl].astype(BF16)
            dw_ref[gi] += lax.dot_general(pooled, dmixed[HALO:HALO + tm, sl].astype(BF16), _TN,
                                          preferred_element_type=F32)
            dpl = lax.dot_general(dmg, w_ref[gi], _NT, preferred_element_type=F32)
            e = dpl / _pool_counts(i, tm, n, w, rows, 0)
            da_parts.append(_sh(_window_sums(e, w), 1 - w // 2, tm) - dpl[HALO:HALO + tm])
        da = jnp.concatenate(da_parts, axis=1)
        r = rext[HALO:HALO + tm]
        xh = xhext[HALO:HALO + tm]
        nrm = xh * gain_ref[...]
        dn = da * (1.0 + sc_ref[...])
        dxh = dn * gain_ref[...]
        dxi = dxo_t + r * (dxh - xh * jnp.mean(dxh * xh, axis=-1, keepdims=True))
        dxi_ref[...] = dxi
        _acc_out(dsh_ref, i, _colsum(da))
        _acc_out(dsc_ref, i, _colsum(da * nrm))
        _acc_out(dgn_ref, i, _colsum(dn * xh))
        dzp_ref[...] = (dxi * gp_ref[...]).astype(BF16)
        _acc_out(dgp_ref, i, _colsum(dxi * zp_ref[...]))

    row = pl.BlockSpec((tm, d), lambda i: (i, 0))
    prev, nxt = _halo_specs(tm, d, n)
    wspec = pl.BlockSpec((4, PG, PG), lambda i: (0, 0, 0))
    vshape = jax.ShapeDtypeStruct((1, d), F32)
    return pl.pallas_call(
        body, grid=(ni,),
        in_specs=[row, prev, nxt, row, row, prev, nxt] + [_vec(d)] * 5 + [wspec, row, _vec(d)],
        out_specs=[row, wspec] + [_vec(d)] * 5 + [row, _vec(d)],
        out_shape=[jax.ShapeDtypeStruct((n, d), F32), jax.ShapeDtypeStruct((4, PG, PG), F32)] + [vshape] * 5
        + [jax.ShapeDtypeStruct((n, d), BF16), vshape],
        name=name, compiler_params=_params("arbitrary"))(dxo, dxo, dxo, mixed, x, x, x, g, scale, gain, sc, sh, pool_w,
                                                         zprev, gprev)


def _adamw(gparts_list, w, m, v, *, name, silu_grad_of=None):
    nl = len(gparts_list)
    nparts, r, c = gparts_list[0].shape
    tr = _pick(r, (256, 128, 64, 32, 16, 8))
    has_c = silu_grad_of is not None

    def body(*refs):
        gp_refs = refs[:nl]
        it = iter(refs[nl:])
        w_ref, m_ref, v_ref = next(it), next(it), next(it)
        c_ref = next(it) if has_c else None
        g_ref, d_ref, mo_ref, vo_ref = next(it), next(it), next(it), next(it)
        layer = pl.program_id(0)

        def update(gp_ref):
            g = gp_ref[0].astype(F32)
            for p in range(1, nparts):
                g = g + gp_ref[p].astype(F32)
            if has_c:
                cv = c_ref[0]
                sg = _sigmoid(cv)
                g = g * (sg * (1.0 + cv * (1.0 - sg)))
            g_ref[0] = g
            mn = ADAM_B1 * m_ref[0] + (1.0 - ADAM_B1) * g
            vn = ADAM_B2 * v_ref[0] + (1.0 - ADAM_B2) * (g * g)
            m_hat = mn / (1.0 - ADAM_B1 ** ADAM_STEP)
            v_hat = vn / (1.0 - ADAM_B2 ** ADAM_STEP)
            d_ref[0] = -ADAM_LR * (m_hat / (jnp.sqrt(v_hat) + ADAM_EPS) + ADAM_WD * w_ref[0])
            mo_ref[0] = mn
            vo_ref[0] = vn

        if nl == 1:
            update(gp_refs[0])
        else:
            for li in range(nl):
                pl.when(layer == li)(functools.partial(update, gp_refs[li]))

    row = pl.BlockSpec((1, tr, c), lambda l, i: (l, i, 0))
    in_specs = [pl.BlockSpec((nparts, tr, c), lambda l, i, li=li: (0, jnp.where(l == li, i, 0), 0)) for li in range(nl)]
    in_specs += [row, row, row]
    args = list(gparts_list) + [w, m, v]
    if has_c:
        in_specs.append(row)
        args.append(silu_grad_of)
    return pl.pallas_call(
        body, grid=(nl, r // tr), in_specs=in_specs, out_specs=[row] * 4,
        out_shape=[jax.ShapeDtypeStruct((nl, r, c), F32)] * 4, name=name,
        compiler_params=_params("arbitrary", "arbitrary"))(*args)


def _adamw_nd(gparts, w, m, v, *, name, silu_grad_of=None):
    shape = w.shape
    c = shape[-1]
    if isinstance(gparts, (list, tuple)):
        nl = len(gparts)
        r = math.prod(shape[1:-1])
    else:
        nl = 1
        r = math.prod(shape[:-1]) if len(shape) > 1 else 1
        gparts = [gparts]
    rs = lambda a: a.reshape(nl, r, c)
    res = _adamw([gp.reshape(gp.shape[0], r, c) for gp in gparts], rs(w), rs(m), rs(v), name=name,
                 silu_grad_of=None if silu_grad_of is None else rs(silu_grad_of))
    return [a.reshape(shape) for a in res]


def _place():
    return lax.axis_index("x"), lax.axis_index("y"), lax.axis_index("c")


def _all_gather(arrs, *, name):
    k_arr = len(arrs)

    def body(*refs):
        ins = refs[:k_arr]
        outs = refs[k_arr:2 * k_arr]
        send_sems, recv_sems, local_sems = refs[2 * k_arr:]
        x, y, c = _place()
        me, sibling = (x, y, c), (x, y, 1 - c)
        chips = [(1 - x, y), (x, 1 - y), (1 - x, 1 - y)]

        def slot(a, px, py, pc):
            return outs[a].at[4 * px + 2 * py + pc]

        def copy(a, s, block, to, src=None):
            return pltpu.make_async_remote_copy(
                src_ref=slot(a, *block) if src is None else src, dst_ref=slot(a, *block),
                send_sem=send_sems.at[a, s], recv_sem=recv_sems.at[a, s], device_id=to, device_id_type=MESH)

        mine = [pltpu.make_async_copy(ins[a], slot(a, *me), local_sems.at[a]) for a in range(k_arr)]
        for cp in mine:
            cp.start()
        first = []
        for a in range(k_arr):
            first.append(copy(a, 0, me, sibling, src=ins[a]))
            first += [copy(a, 1 + j, me, (*chip, c), src=ins[a]) for j, chip in enumerate(chips)]
        for cp in first:
            cp.start()
        passed = []
        for j, chip in enumerate(chips):
            for a in range(k_arr):
                copy(a, 1 + j, (*chip, c), me).wait_recv()
                fw = copy(a, 4 + j, (*chip, c), sibling)
                fw.start()
                passed.append(fw)
        for a in range(k_arr):
            copy(a, 0, sibling, me).wait_recv()
            for j, chip in enumerate(chips):
                copy(a, 4 + j, (*chip, 1 - c), me).wait_recv()
        for cp in first + passed:
            cp.wait_send()
        for cp in mine:
            cp.wait()

    any_spec = pl.BlockSpec(memory_space=pl.ANY)
    return pl.pallas_call(
        body, in_specs=[any_spec] * k_arr, out_specs=[any_spec] * k_arr,
        out_shape=[jax.ShapeDtypeStruct((NDEV,) + a.shape, a.dtype) for a in arrs],
        scratch_shapes=[pltpu.SemaphoreType.DMA((k_arr, 7)), pltpu.SemaphoreType.DMA((k_arr, 7)),
                        pltpu.SemaphoreType.DMA((k_arr,))],
        name=name)(*arrs)


_HBM = pl.BlockSpec(memory_space=pltpu.HBM)
_SEM = pl.BlockSpec(memory_space=pltpu.SEMAPHORE)
_EFFECT = pltpu.SideEffectType.DATAFLOW_SIDE_EFFECTING


def _peers(x, y, c):
    return [(x ^ (rel >> 2), y ^ ((rel >> 1) & 1), c ^ (rel & 1)) for rel in range(1, NDEV)]


def _exchange_copies(srcs, lands, send_sems, recv_sems, scatter):
    x, y, c = _place()
    me = 4 * x + 2 * y + c
    copies = []
    for r, (px, py, pc) in enumerate(_peers(x, y, c)):
        peer = 4 * px + 2 * py + pc
        for a in range(len(srcs)):
            copies.append(pltpu.make_async_remote_copy(
                src_ref=srcs[a].at[peer] if scatter else srcs[a], dst_ref=lands[a].at[me],
                send_sem=send_sems.at[7 * a + r], recv_sem=recv_sems.at[7 * a + r], device_id=(px, py, pc),
                device_id_type=MESH))
    return copies


def _exchange_start(arrs, *, scatter, name):
    k_arr = len(arrs)
    land_shapes = [a.shape if scatter else (NDEV,) + a.shape for a in arrs]
    lands = [pltpu.with_memory_space_constraint(lax.empty(s, a.dtype), pltpu.HBM) for s, a in zip(land_shapes, arrs)]
    srcs = [pltpu.with_memory_space_constraint(a, pltpu.HBM) for a in arrs]

    def body(*refs):
        src_refs, land_refs = refs[:k_arr], refs[k_arr:2 * k_arr]
        send_sems, recv_sems = refs[2 * k_arr], refs[2 * k_arr + 1]
        token = refs[-1]
        for cp in _exchange_copies(src_refs, land_refs, send_sems, recv_sems, scatter):
            cp.start()
        token[...] = jnp.zeros_like(token)

    out_shape = ([pltpu.SemaphoreType.DMA((7 * k_arr,)), pltpu.SemaphoreType.DMA((7 * k_arr,))]
                 + [pltpu.HBM(a.shape, a.dtype) for a in arrs] + [pltpu.HBM(s, a.dtype) for s, a in zip(land_shapes, arrs)]
                 + [jax.ShapeDtypeStruct((8, 128), F32)])
    res = pl.pallas_call(
        body, name=name, out_shape=out_shape, in_specs=[_HBM] * (2 * k_arr),
        out_specs=[_SEM, _SEM] + [_HBM] * (2 * k_arr) + [pl.BlockSpec(memory_space=pltpu.VMEM)],
        input_output_aliases={i: 2 + i for i in range(2 * k_arr)},
        compiler_params=pltpu.CompilerParams(has_side_effects=_EFFECT))(*srcs, *lands)
    return dict(send=res[0], recv=res[1], srcs=list(res[2:2 + k_arr]), lands=list(res[2 + k_arr:2 + 2 * k_arr]),
                token=res[-1], scatter=scatter)


def _exchange_wait(handle, after, *, name):
    k_arr = len(handle["srcs"])
    scatter = handle["scatter"]

    def body(*refs):
        src_refs, land_refs = refs[:k_arr], refs[k_arr:2 * k_arr]
        send_sems, recv_sems = refs[2 * k_arr], refs[2 * k_arr + 1]
        x, y, c = _place()
        me = 4 * x + 2 * y + c
        for r, (px, py, pc) in enumerate(_peers(x, y, c)):
            peer = 4 * px + 2 * py + pc
            for a in range(k_arr):
                cp = pltpu.make_async_remote_copy(
                    src_ref=src_refs[a].at[peer] if scatter else src_refs[a], dst_ref=land_refs[a].at[peer],
                    send_sem=send_sems.at[7 * a + r], recv_sem=recv_sems.at[7 * a + r], device_id=(x, y, c),
                    device_id_type=MESH)
                cp.wait_send()
                cp.wait_recv()

    arrs = handle["srcs"] + handle["lands"]
    res = pl.pallas_call(
        body, name=name, out_shape=[pltpu.HBM(a.shape, a.dtype) for a in arrs],
        in_specs=[_HBM] * (2 * k_arr) + [_SEM, _SEM, pl.BlockSpec(memory_space=pl.ANY)],
        out_specs=[_HBM] * (2 * k_arr), input_output_aliases={i: i for i in range(2 * k_arr)},
        compiler_params=pltpu.CompilerParams(has_side_effects=_EFFECT))(*arrs, handle["send"], handle["recv"], after)
    me = 4 * lax.axis_index("x") + 2 * lax.axis_index("y") + lax.axis_index("c")
    out = []
    for src, land in zip(res[:k_arr], res[k_arr:]):
        own = lax.dynamic_index_in_dim(src, me, 0, keepdims=False) if scatter else src
        out.append(lax.dynamic_update_index_in_dim(land, own, me, 0))
    return out


def _ffn_bwd(dxo, dz, xr, f, u_gc, hmid, gain, sc, w_up, cw, w_down, tag, gate_y=None, gate_g=None):
    d_wdown = _mm_tn((hmid, dz), name=f"ffn_down_dw_{tag}")
    dug, duv, dcw, dcb = _ffn_down_glu_bwd(dz, w_down, u_gc[0], u_gc[1], cw, name=f"ffn_down_glu_bwd_{tag}")
    d_wup = _mm_tn((dug, f), blocks=2, block=0, name=f"ffn_up_dwg_{tag}")
    d_wup = _mm_tn((duv, f), blocks=2, block=1, into=d_wup, name=f"ffn_up_dwv_{tag}")
    gated = gate_y is not None
    res = _mm_w_ep([dug, duv], w_up, _ep_norm_bwd(gated), [xr, dxo] + ([gate_y] if gated else []),
                   [gain, sc] + ([gate_g] if gated else []), [F32] + ([BF16] if gated else []),
                   [D] * (4 if gated else 3), name=f"ffn_up_dx_norm_bwd_{tag}")
    n_out = 2 if gated else 1
    return res[:n_out], res[n_out:], (d_wup, d_wdown, dcw, dcb)


def _split6(mod):
    return [mod[j * D:(j + 1) * D][None, :] for j in range(6)]


def _row(v):
    return v.reshape(1, -1)


def kernel(x, c, ctx, c_ctx, ada_w, ada_b, mix_norm, ffn_norm, even_w_in, even_q_gain, even_k_gain, even_conv_w, even_w_out, odd_pool_w, odd_pool_scale, ffn_w_up, ffn_conv_w, ffn_conv_b, ffn_w_down, loss_target, m_c_ctx, m_ada_w, m_ada_b, m_mix_norm, m_ffn_norm, m_even_w_in, m_even_q_gain, m_even_k_gain, m_even_conv_w, m_even_w_out, m_odd_pool_w, m_odd_pool_scale, m_ffn_w_up, m_ffn_conv_w, m_ffn_conv_b, m_ffn_w_down, v_c_ctx, v_ada_w, v_ada_b, v_mix_norm, v_ffn_norm, v_even_w_in, v_even_q_gain, v_even_k_gain, v_even_conv_w, v_even_w_out, v_odd_pool_w, v_odd_pool_scale, v_ffn_w_up, v_ffn_conv_w, v_ffn_conv_b, v_ffn_w_down):
    n = x.shape[1]
    lc = ctx.shape[1]
    me = 4 * lax.axis_index("x") + 2 * lax.axis_index("y") + lax.axis_index("c")
    xs, ctxs, tgt = x[0], ctx[0], loss_target[0]
    acols = ada_w.shape[2]

    small = jnp.concatenate([even_conv_w.reshape(-1), ffn_conv_w.reshape(-1), odd_pool_scale.reshape(-1)])
    nsmall = small.shape[0]
    small = jnp.pad(small, (0, (-nsmall) % 1024)).reshape(-1, 128)
    c_rows = jnp.pad(c, ((0, 7), (0, 0)))
    tr = lambda a: jnp.swapaxes(a, -1, -2)
    g_c, g_win, g_small = _all_gather([c_rows, tr(even_w_in[0]).astype(BF16), small], name="gather_first")
    w_in_t = g_win.reshape(-1, D)
    g_small = g_small.reshape(NDEV, -1)
    ecw = even_conv_w.shape[2]
    fcw = ffn_conv_w.shape[2]
    conv_w = g_small[:, :3 * ecw].reshape(NDEV, 3, ecw).transpose(1, 0, 2).reshape(3, CW)
    o1 = 3 * ecw
    fconv_w = g_small[:, o1:o1 + 6 * fcw].reshape(NDEV, 2, 3, fcw).transpose(1, 2, 0, 3).reshape(2, 3, DFF)
    o2 = o1 + 6 * fcw
    pool_scale = g_small[:, o2:o2 + D // NDEV].reshape(1, D)

    mraw = jnp.concatenate([g_c[:, 0, :], c_ctx[None, :], jnp.zeros((7, D), F32)], axis=0)
    my_bias = lax.dynamic_slice_in_dim(ada_b, me * acols, acols, axis=1)
    modp = jnp.stack([_mm(mraw, ada_w[l], silu_a=True, bias=my_bias[l:l + 1], name=f"ada_proj_{l}", tm=16, tn=256)
                      for l in range(2)])
    (g_mod,) = _all_gather([modp], name="gather_mod")
    mod_rows = g_mod.transpose(1, 2, 0, 3).reshape(2, 16, 6 * D)
    late_shards = [even_w_out[0].astype(BF16), odd_pool_w[0].astype(BF16), tr(ffn_w_up[0]).astype(BF16),
                   tr(ffn_w_up[1]).astype(BF16), ffn_w_down[0].astype(BF16), ffn_w_down[1].astype(BF16)]
    late_shards, mod_rows = lax.optimization_barrier((late_shards, mod_rows))
    h_weights = _exchange_start(late_shards, scatter=False, name="weights_start")
    mod_rows = mod_rows + h_weights["token"][0, 0]
    mod = lax.dynamic_index_in_dim(mod_rows, me, axis=1, keepdims=False)
    sh1, sc1, g1, sh2, sc2, g2 = _split6(mod[0])
    sh1b, sc1b, g1b, sh2b, sc2b, g2b = _split6(mod[1])
    csh1, csc1 = _split6(mod_rows[0, 8])[:2]
    mixn = [_row(mix_norm[l]) for l in range(2)]
    ffnn = [_row(ffn_norm[l]) for l in range(2)]
    qg, kg = _row(even_q_gain[0]), _row(even_k_gain[0])
    fcb = [_row(ffn_conv_b[l]) for l in range(2)]

    cs_t, sn_t = _rope_tables(n)
    a_lat = _norm_mod(xs, mixn[0], sc1, sh1, name="mix0_norm")
    a_ctx = _norm_mod(ctxs, mixn[0], csc1, csh1, name="mix0_norm_ctx")
    p_ctx = _mm(a_ctx, w_in_t[AW:AW + 4 * HD], tb=True, name="in_proj_ctx", tm=256, tn=512, tk=1024)
    kv_ctx = _qkv_prep(p_ctx, qg, kg, None, None, has_q=False, kv_col=0, kv_rows=lc + n, name="qkv_prep_ctx")
    p_lat, q_r, k_all, v_all, conv = _in_proj_qkv(a_lat, w_in_t, qg, kg, cs_t, sn_t, conv_w, kv_ctx, kv_row_off=lc,
                                                  name="in_proj_qkv")
    o_attn, lse = _attn_fwd(q_r, k_all, v_all, name="attn_fwd")
    g_wout, g_pool, g_up0, g_up1, g_down0, g_down1 = _exchange_wait(h_weights, o_attn, name="weights_wait")
    w_out = g_wout.reshape(D, D)
    pool_w = g_pool.transpose(1, 0, 2, 3).reshape(4, PG, PG)
    w_up_t = [g_up0.reshape(2 * DFF, D), g_up1.reshape(2 * DFF, D)]
    w_up = [w.T for w in w_up_t]
    w_down = [g_down0.reshape(DFF, D), g_down1.reshape(DFF, D)]
    y0, x1, f0 = _mm_w_ep([o_attn, conv], w_out, _ep_resid_norm, [xs], [g1, ffnn[0], sc2, sh2], [F32, F32, BF16], [],
                          tm=512, name="out_proj_norm")[:3]
    *u0, h0 = _ffn_up_glu(f0, w_up[0], fconv_w[0], fcb[0], name="ffn_up_glu_l0")
    z0, x2 = _mm_w_ep(h0, w_down[0], _ep_resid, [x1], [g2], [F32, F32], [], tm=512, name="ffn_down_resid_l0")[:2]

    mixed = _pool_fwd(x2, mixn[1], sc1b, sh1b, pool_w, name="pool_fwd")
    x3, f1 = _norm_mod(x2, ffnn[1], sc2b, sh2b, y=mixed, g=g1b, ymul=pool_scale, name="ffn_norm_l1")
    *u1, h1 = _ffn_up_glu(f1, w_up[1], fconv_w[1], fcb[1], name="ffn_up_glu_l1")
    dx4, dz1, loss_part, dg2b = _mm_w_ep(h1, w_down[1], _ep_loss(D), [x3, tgt], [g2b], [F32, BF16], [128, D],
                                         tm=512, name="ffn_down_loss")

    (dx3,), (dsh2b, dsc2b, dffn1), (dup1, ddown1, dfcw1, dfcb1) = _ffn_bwd(
        dx4, dz1, x3, f1, u1, h1, ffnn[1], sc2b, w_up_t[1], fconv_w[1], w_down[1], "l1")
    dx2, dpool_w, dg1b, dpscale, dsh1b, dsc1b, dmix1, dz0, dg2 = _pool_bwd(
        dx3, mixed, x2, g1b, pool_scale, mixn[1], sc1b, sh1b, pool_w, z0, g2, name="pool_bwd")

    s_pool = dpool_w.astype(BF16).reshape(4, NDEV, PG // NDEV, PG).transpose(1, 0, 2, 3)
    h_g1 = _exchange_start([s_pool, dup1.reshape(NDEV, -1, D), ddown1.reshape(NDEV, DFF // NDEV, D)], scatter=True,
                           name="grads1_start")

    (dx1, dy0), (dsh2, dsc2, dffn0, dg1), (dup0, ddown0, dfcw0, dfcb0) = _ffn_bwd(
        dx2, dz0, x1, f0, u0, h0, ffnn[0], sc2, w_up_t[0], fconv_w[0] + h_g1["token"][0, 0], w_down[0], "l0",
        gate_y=y0, gate_g=g1)
    h_g0 = _exchange_start([dup0.reshape(NDEV, -1, D), ddown0.reshape(NDEV, DFF // NDEV, D)], scatter=True,
                           name="grads0_start")
    d_attn, dp_conv, dconv_w = _out_proj_dx_conv_bwd(dy0, w_out, p_lat, conv_w + h_g0["token"][0, 0],
                                                     name="out_proj_dx_conv_bwd")
    d_wout = _mm_tn((o_attn, dy0), blocks=2, block=0, name="out_proj_dw_attn")
    d_wout = _mm_tn((conv, dy0), blocks=2, block=1, into=d_wout, name="out_proj_dw_conv")
    dq_r, dk_all, dv_all = _attn_bwd(q_r, k_all, v_all, d_attn, o_attn, lse, name="attn_bwd")
    dp_qkv, dqg_l, dkg_l = _qkv_bwd(p_lat, dq_r, dk_all, dv_all, qg, kg, cs_t, sn_t, has_q=True, kv_col=1,
                                    kv_row_off=lc, name="qkv_bwd")
    dp_ctx, _zero_qg, dkg_c = _qkv_bwd(p_ctx, None, dk_all, dv_all, qg, kg, None, None, has_q=False, kv_col=0,
                                       kv_row_off=0, name="qkv_bwd_ctx")
    da_ctx = _mm(dp_ctx, w_in_t[:D], name="in_proj_dx_ctx", tm=256, tn=512, tk=1024)
    d_win_qkv = _mm_tn([(dp_qkv, a_lat), (dp_ctx, a_ctx)], name="in_proj_dw_qkv")
    d_win_conv = _mm_tn((dp_conv, a_lat), name="in_proj_dw_conv")
    d_win_t = jnp.concatenate([d_win_qkv, d_win_conv], axis=0)
    grad_x, dsh1, dsc1, dmix0 = _mm_w_ep([dp_qkv, dp_conv], w_in_t, _ep_norm_bwd(False), [xs, dx1], [mixn[0], sc1],
                                         [F32], [D] * 3, tm=512, name="in_proj_dx_norm_bwd")
    _dctx, dcsh1, dcsc1, dmix0c = _norm_mod_bwd(da_ctx, ctxs, mixn[0], csc1, name="mix0_norm_bwd_ctx")

    z1k = jnp.zeros((1, D), F32)
    pack = jnp.concatenate(
        [v.reshape(-1) for v in (dsh1, dsc1, dg1, dsh2, dsc2, dg2, dsh1b, dsc1b, dg1b, dsh2b, dsc2b, dg2b,
                                 dcsh1, dcsc1, z1k, z1k, z1k, z1k,
                                 dmix0, dmix1, dmix0c, z1k, dffn0, dffn1, dqg_l, dkg_l + dkg_c,
                                 dfcb0, dfcb1, dconv_w, dfcw0, dfcw1, dpscale, loss_part[:, 0:1])])
    npack = pack.shape[0]
    pack = jnp.pad(pack, (0, (-npack) % 1024)).reshape(-1, 128)
    (g_pack,) = _all_gather([pack], name="gather_small_grads")
    def split(gp):
        off = [0]

        def take(size):
            seg = gp[:, off[0]:off[0] + size]
            off[0] += size
            return seg

        return (take(12 * D).reshape(NDEV, 2, 6 * D),
                take(6 * D).reshape(NDEV, 1, 6 * D),
                take(4 * D).reshape(NDEV, 2, 2, D),
                take(2 * D).reshape(NDEV, 2, D), take(HD).reshape(NDEV, 1, HD), take(HD).reshape(NDEV, 1, HD),
                take(2 * DFF).reshape(NDEV, 2, DFF), take(3 * CW).reshape(NDEV, 3, CW),
                take(6 * DFF).reshape(NDEV, 2, 3, DFF), take(D).reshape(NDEV, D), take(1))

    gp_all = g_pack.reshape(NDEV, -1)
    dmod_all, dmodc_all = split(gp_all)[:2]

    dmodc_sum = dmodc_all[0]
    for dev in range(1, NDEV):
        dmodc_sum = dmodc_sum + dmodc_all[dev]
    my_cols = lambda a: lax.dynamic_slice_in_dim(a, me * acols, acols, axis=a.ndim - 1)
    rows0 = jnp.concatenate([my_cols(dmod_all[:, 0]), my_cols(dmodc_sum), jnp.zeros((7, acols), F32)], axis=0)
    rows1 = jnp.concatenate([my_cols(dmod_all[:, 1]), jnp.zeros((8, acols), F32)], axis=0)
    d_ada = jnp.stack([_mm(mraw, rows, ta=True, silu_a=True, name=f"ada_dw_{l}", tm=512, tn=256, tk=16)
                       for l, rows in enumerate((rows0, rows1))])
    dscc_part = _mm(rows0, ada_w[0], tb=True, name="ada_dcctx", tm=16, tn=512, tk=256)
    (g_dscc,) = _all_gather([dscc_part[8:16]], name="gather_dcctx")

    attn_shards = [d_win_t.reshape(NDEV, -1, D), d_wout.reshape(NDEV, D // NDEV, D)]
    attn_shards, g_dscc = lax.optimization_barrier((attn_shards, g_dscc))
    h_ga = _exchange_start(attn_shards, scatter=True, name="grads_attn_start")
    (dmod_all, dmodc_all, dmix_all, dffn_all, dqg_all, dkg_all, dfcb_all, dconvw_all, dfcw_all, dpscale_all,
     loss_all) = split(gp_all + h_ga["token"][0, 0])
    d_ada = d_ada + h_ga["token"][0, 0]
    loss = loss_all[0, 0]
    for dev in range(1, NDEV):
        loss = loss + loss_all[dev, 0]

    outs = {}

    def put(nm, res):
        outs["grad_" + nm], outs["delta_" + nm], outs["new_m_" + nm], outs["new_v_" + nm] = res

    dmodc_pad = jnp.concatenate([dmodc_all, jnp.zeros_like(dmodc_all)], axis=1)
    put("ada_b", _adamw_nd(jnp.concatenate([dmod_all, dmodc_pad], axis=0), ada_b, m_ada_b, v_ada_b, name="adam_ada_b"))
    put("mix_norm", _adamw_nd(jnp.concatenate([dmix_all[:, 0], dmix_all[:, 1]], axis=0), mix_norm, m_mix_norm,
                              v_mix_norm, name="adam_mix_norm"))
    put("ffn_norm", _adamw_nd(dffn_all, ffn_norm, m_ffn_norm, v_ffn_norm, name="adam_ffn_norm"))
    put("even_q_gain", _adamw_nd(dqg_all, even_q_gain, m_even_q_gain, v_even_q_gain, name="adam_q_gain"))
    put("even_k_gain", _adamw_nd(dkg_all, even_k_gain, m_even_k_gain, v_even_k_gain, name="adam_k_gain"))
    put("ffn_conv_b", _adamw_nd(dfcb_all, ffn_conv_b, m_ffn_conv_b, v_ffn_conv_b, name="adam_ffn_conv_b"))
    my_convw = lax.dynamic_slice_in_dim(dconvw_all, me * ecw, ecw, axis=2)[:, None]
    put("even_conv_w", _adamw_nd(my_convw, even_conv_w, m_even_conv_w, v_even_conv_w, name="adam_even_conv_w"))
    my_fcw = lax.dynamic_slice_in_dim(dfcw_all, me * fcw, fcw, axis=3)
    put("ffn_conv_w", _adamw_nd(my_fcw, ffn_conv_w, m_ffn_conv_w, v_ffn_conv_w, name="adam_ffn_conv_w"))
    my_ps = lax.dynamic_slice_in_dim(dpscale_all, me * (D // NDEV), D // NDEV, axis=1)[:, None]
    put("odd_pool_scale", _adamw_nd(my_ps, odd_pool_scale, m_odd_pool_scale, v_odd_pool_scale, name="adam_pool_scale"))

    put("ada_w", _adamw_nd(d_ada[None], ada_w, m_ada_w, v_ada_w, name="adam_ada_w"))
    put("c_ctx", _adamw_nd(g_dscc[:, 0:1, :].reshape(NDEV, D), c_ctx, m_c_ctx, v_c_ctx, name="adam_c_ctx",
                           silu_grad_of=c_ctx))

    r_pool, r_up1, r_down1 = _exchange_wait(h_g1, outs["grad_ada_b"], name="grads1_wait")
    r_up0, r_down0 = _exchange_wait(h_g0, outs["grad_mix_norm"], name="grads0_wait")
    put("odd_pool_w", _adamw_nd(r_pool[:, None], odd_pool_w, m_odd_pool_w, v_odd_pool_w, name="adam_pool_w"))
    put("ffn_w_up", [tr(a) for a in _adamw_nd([r_up0, r_up1], tr(ffn_w_up), tr(m_ffn_w_up), tr(v_ffn_w_up),
                                              name="adam_w_up")])
    put("ffn_w_down", _adamw_nd([r_down0, r_down1], ffn_w_down, m_ffn_w_down, v_ffn_w_down, name="adam_w_down"))
    r_win, r_wout = _exchange_wait(h_ga, outs["grad_ffn_w_down"], name="grads_attn_wait")
    put("even_w_in", [tr(a) for a in _adamw_nd(r_win[:, None], tr(even_w_in), tr(m_even_w_in), tr(v_even_w_in),
                                               name="adam_w_in")])
    put("even_w_out", _adamw_nd(r_wout[:, None], even_w_out, m_even_w_out, v_even_w_out, name="adam_w_out"))

    names = ["c_ctx", "ada_w", "ada_b", "mix_norm", "ffn_norm", "even_w_in", "even_q_gain", "even_k_gain",
             "even_conv_w", "even_w_out", "odd_pool_w", "odd_pool_scale", "ffn_w_up", "ffn_conv_w", "ffn_conv_b",
             "ffn_w_down"]
    result = [loss, grad_x[None]]
    for kind in ("grad_", "delta_", "new_m_", "new_v_"):
        result += [outs[kind + nm] for nm in names]
    return tuple(result)
```

```python
import functools
import math

import jax
import jax.numpy as jnp
from jax import lax
from jax.experimental import pallas as pl
from jax.experimental.pallas import tpu as pltpu

F32 = jnp.float32
BF16 = jnp.bfloat16

D = 1024
HD = 128
NQ = 4
NKV = 2
AW = NQ * HD
CW = D - AW
DFF = 2816
GRID_W = 64
ROPE_THETA = 10000.0
POOL_WINDOWS = (2, 4, 8, 16)
PG = D // 4
EPS = 1e-6
NDEV = 8
HALO = 8
MESH = pl.DeviceIdType.MESH

ADAM_LR = 0.001
ADAM_B1 = 0.9
ADAM_B2 = 0.999
ADAM_EPS = 1e-08
ADAM_WD = 0.01
ADAM_STEP = 10


def _pick(dim, prefs):
    for p in prefs:
        if dim % p == 0:
            return p
    return dim


def _params(*sem):
    return pltpu.CompilerParams(dimension_semantics=sem)


_NT = (((1,), (1,)), ((), ()))
_TN = (((0,), (0,)), ((), ()))
_SCALE = HD ** -0.5
_QSCALE = _SCALE * math.log2(math.e)
_LN2 = math.log(2.0)


def _mm(a_list, b, *, name, ta=False, tb=False, out_dtype=F32, silu_a=False, bias=None, tm=None, tn=None, tk=None):
    if not isinstance(a_list, (list, tuple)):
        a_list = [a_list]
    na = len(a_list)
    assert not (ta and na > 1)
    if ta:
        kdim, m = a_list[0].shape
        ks = [kdim]
    else:
        m = a_list[0].shape[0]
        ks = [a.shape[1] for a in a_list]
        kdim = sum(ks)
    n = b.shape[0] if tb else b.shape[1]
    assert (b.shape[1] if tb else b.shape[0]) == kdim
    kunit = math.gcd(*ks) if na > 1 else kdim
    tm = min(tm, m) if tm else _pick(m, (512, 256, 128, 64, 32, 16, 8))
    tn = min(tn, n) if tn else _pick(n, (512, 256, 128))
    tk = min(tk, kunit) if tk else _pick(kunit, (1024, 768, 512, 256, 128))
    assert m % tm == 0 and n % tn == 0 and all(k % tk == 0 for k in ks)
    nks = [k // tk for k in ks]
    starts = [sum(nks[:i]) for i in range(na)]
    nk = sum(nks)
    has_bias = bias is not None

    def body(*refs):
        a_refs = refs[:na]
        b_ref = refs[na]
        bias_ref = refs[na + 1] if has_bias else None
        o_ref = refs[na + 1 + has_bias]
        acc = refs[-1]
        k = pl.program_id(2)

        @pl.when(k == 0)
        def _():
            acc[...] = jnp.zeros_like(acc)

        bv = b_ref[...].astype(BF16)
        dn = (((0 if ta else 1,), (1 if tb else 0,)), ((), ()))
        for idx in range(na):
            def step(idx=idx):
                av = a_refs[idx][...]
                if silu_a:
                    av = av * jax.nn.sigmoid(av)
                acc[...] += lax.dot_general(av.astype(BF16), bv, dn, preferred_element_type=F32)
            if na == 1:
                step()
            else:
                pl.when((k >= starts[idx]) & (k < starts[idx] + nks[idx]))(step)

        @pl.when(k == nk - 1)
        def _():
            r = acc[...]
            if has_bias:
                r = r + bias_ref[...]
            o_ref[...] = r.astype(o_ref.dtype)

    in_specs = []
    for idx in range(na):
        if ta:
            in_specs.append(pl.BlockSpec((tk, tm), lambda i, j, k: (k, i)))
        else:
            lo, cnt = starts[idx], nks[idx]
            in_specs.append(pl.BlockSpec((tm, tk), lambda i, j, k, lo=lo, cnt=cnt: (i, jnp.clip(k - lo, 0, cnt - 1))))
    if tb:
        in_specs.append(pl.BlockSpec((tn, tk), lambda i, j, k: (j, k)))
    else:
        in_specs.append(pl.BlockSpec((tk, tn), lambda i, j, k: (k, j)))
    args = list(a_list) + [b]
    if has_bias:
        in_specs.append(pl.BlockSpec((1, tn), lambda i, j, k: (0, j)))
        args.append(bias)
    return pl.pallas_call(
        body, grid=(m // tm, n // tn, nk), in_specs=in_specs,
        out_specs=pl.BlockSpec((tm, tn), lambda i, j, k: (i, j)),
        out_shape=jax.ShapeDtypeStruct((m, n), out_dtype),
        scratch_shapes=[pltpu.VMEM((tm, tn), F32)], name=name,
        compiler_params=_params("parallel", "parallel", "arbitrary"))(*args)


def _mm_w(a_list, w, *, name, tb=False, tm=256, out_dtype=F32):
    if not isinstance(a_list, (list, tuple)):
        a_list = [a_list]
    na = len(a_list)
    m = a_list[0].shape[0]
    ks = [a.shape[1] for a in a_list]
    offs = [sum(ks[:i]) for i in range(na)]
    n = w.shape[0] if tb else w.shape[1]
    assert (w.shape[1] if tb else w.shape[0]) == sum(ks)
    tm = min(tm, m)
    assert m % tm == 0

    def body(*refs):
        a_refs, w_ref, o_ref = refs[:na], refs[na], refs[na + 1]
        acc = None
        for idx in range(na):
            av = a_refs[idx][...].astype(BF16)
            if tb:
                part = lax.dot_general(av, w_ref[:, offs[idx]:offs[idx] + ks[idx]], _NT, preferred_element_type=F32)
            else:
                part = jnp.dot(av, w_ref[offs[idx]:offs[idx] + ks[idx], :], preferred_element_type=F32)
            acc = part if acc is None else acc + part
        o_ref[...] = acc.astype(o_ref.dtype)

    in_specs = [pl.BlockSpec((tm, k), lambda i: (i, 0)) for k in ks] + [pl.BlockSpec(w.shape, lambda i: (0, 0))]
    return pl.pallas_call(
        body, grid=(m // tm,), in_specs=in_specs, out_specs=pl.BlockSpec((tm, n), lambda i: (i, 0)),
        out_shape=jax.ShapeDtypeStruct((m, n), out_dtype), name=name, compiler_params=_params("parallel"))(*a_list, w)


def _mm_w_ep(a_list, w, epilogue, row_in, vec_in, out_dtypes, sum_widths, *, name, tb=False, tm=256, sub=256):
    if not isinstance(a_list, (list, tuple)):
        a_list = [a_list]
    na, nr, nv, no, ns = len(a_list), len(row_in), len(vec_in), len(out_dtypes), len(sum_widths)
    m = a_list[0].shape[0]
    ks = [a.shape[1] for a in a_list]
    offs = [sum(ks[:i]) for i in range(na)]
    n = w.shape[0] if tb else w.shape[1]
    assert (w.shape[1] if tb else w.shape[0]) == sum(ks)
    tm = min(tm, m)
    sub = min(sub, tm)
    assert m % tm == 0 and tm % sub == 0

    def body(*refs):
        a_refs, w_ref = refs[:na], refs[na]
        row_refs = refs[na + 1:na + 1 + nr]
        vec_refs = refs[na + 1 + nr:na + 1 + nr + nv]
        out_refs = refs[na + 1 + nr + nv:na + 1 + nr + nv + no]
        sum_refs = refs[na + 1 + nr + nv + no:]

        @pl.when(pl.program_id(0) == 0)
        def _():
            for s_ref in sum_refs:
                s_ref[...] = jnp.zeros_like(s_ref)

        vecs = [v[...] for v in vec_refs]
        for r0 in range(0, tm, sub):
            acc = None
            for idx in range(na):
                av = a_refs[idx][r0:r0 + sub, :].astype(BF16)
                if tb:
                    part = lax.dot_general(av, w_ref[:, offs[idx]:offs[idx] + ks[idx]], _NT, preferred_element_type=F32)
                else:
                    part = jnp.dot(av, w_ref[offs[idx]:offs[idx] + ks[idx], :], preferred_element_type=F32)
                acc = part if acc is None else acc + part
            outs, sums = epilogue(acc, [r[r0:r0 + sub, :] for r in row_refs], vecs)
            for o_ref, o in zip(out_refs, outs):
                o_ref[r0:r0 + sub, :] = o.astype(o_ref.dtype)
            for s_ref, s in zip(sum_refs, sums):
                s_ref[...] += s

    row = pl.BlockSpec((tm, n), lambda i: (i, 0))
    in_specs = ([pl.BlockSpec((tm, k), lambda i: (i, 0)) for k in ks] + [pl.BlockSpec(w.shape, lambda i: (0, 0))]
                + [row] * nr + [_vec(n)] * nv)
    return pl.pallas_call(
        body, grid=(m // tm,), in_specs=in_specs, out_specs=[row] * no + [_vec(sw) for sw in sum_widths],
        out_shape=[jax.ShapeDtypeStruct((m, n), dt) for dt in out_dtypes]
        + [jax.ShapeDtypeStruct((1, sw), F32) for sw in sum_widths],
        name=name, compiler_params=_params("arbitrary" if ns else "parallel"))(*a_list, w, *row_in, *vec_in)


def _ep_norm_bwd(has_gate):
    def ep(dav, rows, vecs):
        xv = rows[0]
        gain, scv = vecs[0], vecs[1]
        r = lax.rsqrt(jnp.mean(xv * xv, axis=-1, keepdims=True) + EPS)
        xh = xv * r
        nrm = xh * gain
        dn = dav * (1.0 + scv)
        dxh = dn * gain
        dx = r * (dxh - xh * jnp.mean(dxh * xh, axis=-1, keepdims=True)) + rows[1]
        outs, sums = [dx], [_colsum(dav), _colsum(dav * nrm), _colsum(dn * xh)]
        if has_gate:
            outs.append(dx * vecs[2])
            sums.append(_colsum(dx * rows[2]))
        return outs, sums
    return ep


def _ep_loss(d):
    def ep(zv, rows, vecs):
        xv, tv = rows
        gv = vecs[0]
        diff = (xv + gv * zv) - tv
        dx = diff * (1.0 / d)
        part = 0.5 * jnp.sum(jnp.mean(diff * diff, axis=-1, keepdims=True), axis=0, keepdims=True)
        return [dx, dx * gv], [jnp.broadcast_to(part, (1, 128)), _colsum(dx * zv)]
    return ep


def _ep_resid(zv, rows, vecs):
    return [zv, rows[0] + vecs[0] * zv], []


def _ep_resid_norm(yv, rows, vecs):
    g, gain, scv, shv = vecs
    xv = rows[0] + g * yv
    r = lax.rsqrt(jnp.mean(xv * xv, axis=-1, keepdims=True) + EPS)
    return [yv, xv, ((xv * r) * gain) * (1.0 + scv) + shv], []


def _mm_tn(pairs, *, name, tk=1024, out_dtype=BF16, blocks=1, block=0, into=None):
    if not isinstance(pairs, list):
        pairs = [pairs]
    m, n = pairs[0][0].shape[1], pairs[0][1].shape[1]
    tks = [min(tk, a.shape[0]) for a, _ in pairs]
    nks = [a.shape[0] // t for (a, _), t in zip(pairs, tks)]
    assert all(a.shape[0] == b.shape[0] and a.shape[0] % t == 0 for (a, b), t in zip(pairs, tks))
    starts = [sum(nks[:i]) for i in range(len(pairs))]
    nk = sum(nks)

    def body(*refs):
        o_ref, acc = refs[-2], refs[-1]
        k = pl.program_id(0)

        @pl.when(k == 0)
        def _():
            acc[...] = jnp.zeros_like(acc)

        for idx in range(len(pairs)):
            a_ref, b_ref = refs[2 * idx], refs[2 * idx + 1]

            def step(a_ref=a_ref, b_ref=b_ref):
                acc[...] += lax.dot_general(a_ref[...].astype(BF16), b_ref[...].astype(BF16), _TN,
                                            preferred_element_type=F32)

            if len(pairs) == 1:
                step()
            else:
                pl.when((k >= starts[idx]) & (k < starts[idx] + nks[idx]))(step)

        @pl.when(k == nk - 1)
        def _():
            o_ref[...] = acc[...].astype(o_ref.dtype)

    in_specs, args = [], []
    for (a, b), t, lo, cnt in zip(pairs, tks, starts, nks):
        idx_map = lambda k, lo=lo, cnt=cnt: (jnp.clip(k - lo, 0, cnt - 1), 0)
        in_specs += [pl.BlockSpec((t, m), idx_map), pl.BlockSpec((t, n), idx_map)]
        args += [a, b]
    aliases = {}
    if into is not None:
        aliases = {len(args): 0}
        in_specs.append(pl.BlockSpec(memory_space=pl.ANY))
        args.append(into)
    return pl.pallas_call(
        body, grid=(nk,), in_specs=in_specs, out_specs=pl.BlockSpec((m, n), lambda k: (block, 0)),
        out_shape=jax.ShapeDtypeStruct((m * blocks, n), out_dtype), scratch_shapes=[pltpu.VMEM((m, n), F32)],
        input_output_aliases=aliases, name=name, compiler_params=_params("arbitrary"))(*args)


def _vec(d, col=None):
    if col is None:
        return pl.BlockSpec((1, d), lambda i, *_: (0, 0))
    return pl.BlockSpec((1, d), col)


def _halo_specs(tm, width, nrows, colblk=0, row_off=0):
    r = tm // HALO
    off = row_off // HALO
    last = nrows // HALO - 1
    prev = pl.BlockSpec((HALO, width), lambda i, *_: (off + jnp.maximum(i * r - 1, 0), colblk))
    nxt = pl.BlockSpec((HALO, width), lambda i, *_: (off + jnp.minimum((i + 1) * r, last), colblk))
    return prev, nxt


def _ext(prev_ref, main_ref, next_ref, i, ni):
    p = jnp.where(i > 0, prev_ref[...], 0.0)
    n = jnp.where(i < ni - 1, next_ref[...], 0.0)
    return jnp.concatenate([p, main_ref[...], n], axis=0)


def _sh(ext, k, tm):
    if k == 0:
        return ext[HALO:HALO + tm]
    rows = ext.shape[0]
    return pltpu.roll(ext, (-k) % rows, axis=0)[HALO:HALO + tm]


def _roll_rows(v, k):
    rows = v.shape[0]
    return pltpu.roll(v, (-k) % rows, axis=0) if k % rows else v


def _conv3(ext, w_ref, tm):
    return _sh(ext, -1, tm) * w_ref[0:1, :] + _sh(ext, 0, tm) * w_ref[1:2, :] + _sh(ext, 1, tm) * w_ref[2:3, :]


def _colsum(v):
    return jnp.sum(v, axis=0, keepdims=True)


def _acc_out(ref, i, val):
    @pl.when(i == 0)
    def _():
        ref[...] = jnp.zeros_like(ref)

    ref[...] += val


def _sigmoid(v):
    return jax.nn.sigmoid(v)


def _norm_mod(x, gain, sc, sh, *, name, y=None, g=None, ymul=None, tm=512):
    n, d = x.shape
    tm = min(tm, n)
    has_res = y is not None
    has_mul = ymul is not None

    def body(*refs):
        it = iter(refs)
        x_ref = next(it)
        y_ref = next(it) if has_res else None
        g_ref = next(it) if has_res else None
        m_ref = next(it) if has_mul else None
        gain_ref, sc_ref, sh_ref = next(it), next(it), next(it)
        xo_ref = next(it) if has_res else None
        a_ref = next(it)
        xv = x_ref[...]
        if has_res:
            yv = y_ref[...]
            if has_mul:
                yv = yv * m_ref[...]
            xv = xv + g_ref[...] * yv
            xo_ref[...] = xv
        r = lax.rsqrt(jnp.mean(xv * xv, axis=-1, keepdims=True) + EPS)
        nrm = (xv * r) * gain_ref[...]
        a_ref[...] = (nrm * (1.0 + sc_ref[...]) + sh_ref[...]).astype(BF16)

    row = pl.BlockSpec((tm, d), lambda i: (i, 0))
    in_specs, args = [row], [x]
    if has_res:
        in_specs += [row, _vec(d)]
        args += [y, g]
    if has_mul:
        in_specs.append(_vec(d))
        args.append(ymul)
    in_specs += [_vec(d)] * 3
    args += [gain, sc, sh]
    out_specs, out_shape = [], []
    if has_res:
        out_specs.append(row)
        out_shape.append(jax.ShapeDtypeStruct((n, d), F32))
    out_specs.append(row)
    out_shape.append(jax.ShapeDtypeStruct((n, d), BF16))
    res = pl.pallas_call(body, grid=(n // tm,), in_specs=in_specs, out_specs=out_specs, out_shape=out_shape,
                         name=name, compiler_params=_params("parallel"))(*args)
    return res if has_res else res[0]


def _norm_mod_bwd(da, x, gain, sc, *, name, dres=None, gate_y=None, gate_g=None, tm=512):
    n, d = x.shape
    tm = min(tm, n)
    has_res = dres is not None
    has_gate = gate_y is not None

    def body(*refs):
        it = iter(refs)
        da_ref, x_ref = next(it), next(it)
        r_ref = next(it) if has_res else None
        y_ref = next(it) if has_gate else None
        g_ref = next(it) if has_gate else None
        gain_ref, sc_ref = next(it), next(it)
        dx_ref, dsh_ref, dsc_ref, dgn_ref = next(it), next(it), next(it), next(it)
        dy_ref = next(it) if has_gate else None
        dg_ref = next(it) if has_gate else None
        i = pl.program_id(0)
        xv = x_ref[...]
        dav = da_ref[...]
        r = lax.rsqrt(jnp.mean(xv * xv, axis=-1, keepdims=True) + EPS)
        xh = xv * r
        nrm = xh * gain_ref[...]
        dn = dav * (1.0 + sc_ref[...])
        dxh = dn * gain_ref[...]
        dx = r * (dxh - xh * jnp.mean(dxh * xh, axis=-1, keepdims=True))
        if has_res:
            dx = dx + r_ref[...]
        dx_ref[...] = dx
        _acc_out(dsh_ref, i, _colsum(dav))
        _acc_out(dsc_ref, i, _colsum(dav * nrm))
        _acc_out(dgn_ref, i, _colsum(dn * xh))
        if has_gate:
            dy_ref[...] = (dx * g_ref[...]).astype(BF16)
            _acc_out(dg_ref, i, _colsum(dx * y_ref[...]))

    row = pl.BlockSpec((tm, d), lambda i: (i, 0))
    in_specs, args = [row, row], [da, x]
    if has_res:
        in_specs.append(row)
        args.append(dres)
    if has_gate:
        in_specs += [row, _vec(d)]
        args += [gate_y, gate_g]
    in_specs += [_vec(d)] * 2
    args += [gain, sc]
    vec_shape = jax.ShapeDtypeStruct((1, d), F32)
    out_specs = [row, _vec(d), _vec(d), _vec(d)]
    out_shape = [jax.ShapeDtypeStruct((n, d), F32), vec_shape, vec_shape, vec_shape]
    if has_gate:
        out_specs += [row, _vec(d)]
        out_shape += [jax.ShapeDtypeStruct((n, d), BF16), vec_shape]
    return pl.pallas_call(
        body, grid=(n // tm,), in_specs=in_specs, out_specs=out_specs, out_shape=out_shape,
        name=name, compiler_params=_params("arbitrary"))(*args)


def _ffn_up_glu(f, w_up, cw, cb, *, name, tm=256, tc=256):
    n, d = f.shape
    tm = min(tm, n)
    ni = n // tm
    nc = DFF // tc
    halo = 16
    rows = tm + 2 * halo
    r = tm // halo
    last = n // halo - 1

    def body(f_ref, fp_ref, fn_ref, w_ref, cw_ref, cb_ref, u_ref, gc_ref, h_ref):
        i = pl.program_id(0)
        a = f_ref[...]
        aext = jnp.concatenate([jnp.where(i > 0, fp_ref[...], jnp.zeros_like(fp_ref[...])), a,
                                jnp.where(i < ni - 1, fn_ref[...], jnp.zeros_like(fn_ref[...]))], axis=0)
        for j in range(nc):
            cols = slice(j * tc, (j + 1) * tc)
            vcols = slice(DFF + j * tc, DFF + (j + 1) * tc)
            gext = jnp.dot(aext, w_ref[:, cols], preferred_element_type=F32)
            val = jnp.dot(a, w_ref[:, vcols], preferred_element_type=F32)
            gate = gext[halo:halo + tm]
            gc = (pltpu.roll(gext, 1, axis=0)[halo:halo + tm] * cw_ref[0:1, cols] + gate * cw_ref[1:2, cols]
                  + pltpu.roll(gext, rows - 1, axis=0)[halo:halo + tm] * cw_ref[2:3, cols]) + cb_ref[:, cols]
            u_ref[:, cols] = gate
            u_ref[:, vcols] = val
            gc_ref[:, cols] = gc
            h_ref[:, cols] = (gc * _sigmoid(gc) * val).astype(BF16)

    return pl.pallas_call(
        body, grid=(ni,),
        in_specs=[pl.BlockSpec((tm, d), lambda i: (i, 0)),
                  pl.BlockSpec((halo, d), lambda i: (jnp.maximum(i * r - 1, 0), 0)),
                  pl.BlockSpec((halo, d), lambda i: (jnp.minimum((i + 1) * r, last), 0)),
                  pl.BlockSpec(w_up.shape, lambda i: (0, 0)), pl.BlockSpec((3, DFF), lambda i: (0, 0)),
                  pl.BlockSpec((1, DFF), lambda i: (0, 0))],
        out_specs=[pl.BlockSpec((tm, 2 * DFF), lambda i: (i, 0)), pl.BlockSpec((tm, DFF), lambda i: (i, 0)),
                   pl.BlockSpec((tm, DFF), lambda i: (i, 0))],
        out_shape=[jax.ShapeDtypeStruct((n, 2 * DFF), F32), jax.ShapeDtypeStruct((n, DFF), F32),
                   jax.ShapeDtypeStruct((n, DFF), BF16)], name=name,
        compiler_params=_params("parallel"))(f, f, f, w_up, cw, cb)


def _ffn_down_glu_bwd(dz, w_down, u, gc, cw, *, name, tm=256, tc=256):
    n, d = dz.shape
    tm = min(tm, n)
    ni = n // tm
    nc = DFF // tc
    rows = tm + 2 * HALO

    def body(z_ref, zp_ref, zn_ref, w_ref, u_ref, vp_ref, vn_ref, c_ref, cp_ref, cn_ref, cw_ref,
             dg_ref, dv_ref, dcw_ref, dcb_ref):
        i = pl.program_id(0)

        @pl.when(i == 0)
        def _():
            dcw_ref[...] = jnp.zeros_like(dcw_ref)
            dcb_ref[...] = jnp.zeros_like(dcb_ref)

        zext = jnp.concatenate([jnp.where(i > 0, zp_ref[...], jnp.zeros_like(zp_ref[...])), z_ref[...],
                                jnp.where(i < ni - 1, zn_ref[...], jnp.zeros_like(zn_ref[...]))], axis=0)
        for j in range(nc):
            cols = slice(j * tc, (j + 1) * tc)
            vcols = slice(DFF + j * tc, DFF + (j + 1) * tc)
            dh = lax.dot_general(zext, w_ref[cols, :], _NT, preferred_element_type=F32)[HALO:HALO + rows]
            gcx = jnp.concatenate([cp_ref[:, cols], c_ref[:, cols], cn_ref[:, cols]], axis=0)
            vext = jnp.concatenate([vp_ref[:, cols], u_ref[:, vcols], vn_ref[:, cols]], axis=0)
            sg = _sigmoid(gcx)
            dgc = dh * vext * (sg * (1.0 + gcx * (1.0 - sg)))
            dv_ref[:, cols] = (dh[HALO:HALO + tm] * (gcx[HALO:HALO + tm] * sg[HALO:HALO + tm])).astype(BF16)
            d_next = pltpu.roll(dgc, rows - 1, axis=0)[HALO:HALO + tm]
            d_prev = pltpu.roll(dgc, 1, axis=0)[HALO:HALO + tm]
            d_here = dgc[HALO:HALO + tm]
            dg_ref[:, cols] = (d_next * cw_ref[0:1, cols] + d_here * cw_ref[1:2, cols]
                               + d_prev * cw_ref[2:3, cols]).astype(BF16)
            gate = u_ref[:, cols]
            dcw_ref[:, cols] += jnp.concatenate([_colsum(d_next * gate), _colsum(d_here * gate),
                                                 _colsum(d_prev * gate)], axis=0)
            dcb_ref[:, cols] += _colsum(d_here)

    def trio(width, halo, tile_width=None, colblk=0):
        r, last = tm // halo, n // halo - 1
        return [pl.BlockSpec((tm, tile_width or width), lambda i: (i, 0)),
                pl.BlockSpec((halo, width), lambda i: (jnp.maximum(i * r - 1, 0), colblk)),
                pl.BlockSpec((halo, width), lambda i: (jnp.minimum((i + 1) * r, last), colblk))]

    whole = lambda shape: pl.BlockSpec(shape, lambda i: (0, 0))
    return pl.pallas_call(
        body, grid=(ni,),
        in_specs=(trio(d, 16) + [whole(w_down.shape)] + trio(DFF, HALO, tile_width=2 * DFF, colblk=1)
                  + trio(DFF, HALO) + [whole((3, DFF))]),
        out_specs=[pl.BlockSpec((tm, DFF), lambda i: (i, 0)), pl.BlockSpec((tm, DFF), lambda i: (i, 0)),
                   whole((3, DFF)), whole((1, DFF))],
        out_shape=[jax.ShapeDtypeStruct((n, DFF), BF16), jax.ShapeDtypeStruct((n, DFF), BF16),
                   jax.ShapeDtypeStruct((3, DFF), F32), jax.ShapeDtypeStruct((1, DFF), F32)],
        name=name, compiler_params=_params("arbitrary"))(dz, dz, dz, w_down, u, u, u, gc, gc, gc, cw)


def _rope_tables(n):
    rows = n // GRID_W
    axis_dim = HD // 2
    inv_freq = jnp.power(ROPE_THETA, -jnp.arange(0, axis_dim, 2, dtype=F32) / axis_dim)
    ar = jnp.arange(rows, dtype=F32)[:, None] * inv_freq
    ac = jnp.arange(GRID_W, dtype=F32)[:, None] * inv_freq
    by_row = lambda a: jnp.repeat(a, GRID_W, axis=0)
    by_col = lambda a: jnp.tile(a, (rows, 1))
    cr, sr, cc, sc = by_row(jnp.cos(ar)), by_row(jnp.sin(ar)), by_col(jnp.cos(ac)), by_col(jnp.sin(ac))
    return jnp.concatenate([cr, cr, cc, cc], axis=1), jnp.concatenate([-sr, sr, -sc, sc], axis=1)


def _partner(v):
    lane = lax.broadcasted_iota(jnp.int32, v.shape, 1)
    return jnp.where((lane % 64) < 32, pltpu.roll(v, HD - 32, axis=1), pltpu.roll(v, 32, axis=1))


def _qkv_prep(p, q_gain, k_gain, cs, sn, *, name, has_q, kv_col, kv_rows=None, kv_row_off=0, kv_into=None, tm=256):
    n = p.shape[0]
    rope = cs is not None
    kv_rows = kv_rows or n
    rb = kv_row_off // tm

    def body(*refs):
        it = iter(refs)
        q_ref = next(it) if has_q else None
        kv_ref = next(it)
        qg_ref, kg_ref = next(it), next(it)
        cs_ref = next(it) if rope else None
        sn_ref = next(it) if rope else None
        if kv_into is not None:
            next(it), next(it)
        qo_ref = next(it) if has_q else None
        ko_ref, vo_ref = next(it), next(it)

        def norm_rope(xh, gain, mul=None):
            r = lax.rsqrt(jnp.mean(xh * xh, axis=-1, keepdims=True) + EPS)
            xn = (xh * r) * gain
            if rope:
                xn = xn * cs_ref[...] + _partner(xn) * sn_ref[...]
            if mul is not None:
                xn = xn * mul
            return xn.astype(BF16)

        if has_q:
            for h in range(NQ):
                qo_ref[h] = norm_rope(q_ref[:, h * HD:(h + 1) * HD], qg_ref[...], _QSCALE)
        for h in range(NKV):
            ko_ref[h] = norm_rope(kv_ref[:, h * HD:(h + 1) * HD], kg_ref[...])
            vo_ref[h] = kv_ref[:, (NKV + h) * HD:(NKV + h + 1) * HD].astype(BF16)

    in_specs, args = [], []
    if has_q:
        in_specs.append(pl.BlockSpec((tm, AW), lambda i: (i, 0)))
        args.append(p)
    in_specs += [pl.BlockSpec((tm, 2 * NKV * HD), lambda i: (i, kv_col)), _vec(HD), _vec(HD)]
    args += [p, q_gain, k_gain]
    if rope:
        in_specs += [pl.BlockSpec((tm, HD), lambda i: (i, 0))] * 2
        args += [cs, sn]
    out_specs, out_shape = [], []
    if has_q:
        out_specs.append(pl.BlockSpec((NQ, tm, HD), lambda i: (0, i, 0)))
        out_shape.append(jax.ShapeDtypeStruct((NQ, n, HD), BF16))
    out_specs += [pl.BlockSpec((NKV, tm, HD), lambda i: (0, rb + i, 0))] * 2
    out_shape += [jax.ShapeDtypeStruct((NKV, kv_rows, HD), BF16)] * 2
    aliases = {}
    if kv_into is not None:
        aliases = {len(args): int(has_q), len(args) + 1: int(has_q) + 1}
        in_specs += [pl.BlockSpec(memory_space=pl.ANY)] * 2
        args += list(kv_into)
    return pl.pallas_call(body, grid=(n // tm,), in_specs=in_specs, out_specs=out_specs, out_shape=out_shape,
                          input_output_aliases=aliases, name=name, compiler_params=_params("parallel"))(*args)


def _in_proj_qkv(a, w_in_t, q_gain, k_gain, cs, sn, conv_w, kv_into, *, name, kv_row_off, tm=256):
    n, d = a.shape
    nproj = w_in_t.shape[0]
    nqkv = AW + 2 * NKV * HD
    rb = kv_row_off // tm
    ni = n // tm
    halo = 16
    rows = tm + 2 * halo
    r = tm // halo
    last = n // halo - 1

    def body(a_ref, ap_ref, an_ref, w_ref, qg_ref, kg_ref, cs_ref, sn_ref, cw_ref, _k_in, _v_in,
             p_ref, qo_ref, ko_ref, vo_ref, conv_ref):
        i = pl.program_id(0)
        av = a_ref[...]
        aext = jnp.concatenate([jnp.where(i > 0, ap_ref[...], jnp.zeros_like(ap_ref[...])), av,
                                jnp.where(i < ni - 1, an_ref[...], jnp.zeros_like(an_ref[...]))], axis=0)
        qkv = lax.dot_general(av, w_ref[0:nqkv, :], _NT, preferred_element_type=F32)
        p_ref[:, 0:nqkv] = qkv
        cext = lax.dot_general(aext, w_ref[nqkv:nproj, :], _NT, preferred_element_type=F32)
        p_ref[:, nqkv:nproj] = cext[halo:halo + tm]
        hext = cext[:, CW:2 * CW] * cext[:, 2 * CW:3 * CW]
        cv3 = (pltpu.roll(hext, 1, axis=0)[halo:halo + tm] * cw_ref[0:1, :] + hext[halo:halo + tm] * cw_ref[1:2, :]
               + pltpu.roll(hext, rows - 1, axis=0)[halo:halo + tm] * cw_ref[2:3, :])
        conv_ref[...] = (cext[halo:halo + tm, 0:CW] * cv3).astype(BF16)

        def norm_rope(xh, gain, mul=None):
            r = lax.rsqrt(jnp.mean(xh * xh, axis=-1, keepdims=True) + EPS)
            xn = (xh * r) * gain
            xn = xn * cs_ref[...] + _partner(xn) * sn_ref[...]
            if mul is not None:
                xn = xn * mul
            return xn.astype(BF16)

        for h in range(NQ):
            qo_ref[h] = norm_rope(qkv[:, h * HD:(h + 1) * HD], qg_ref[...], _QSCALE)
        for h in range(NKV):
            ko_ref[h] = norm_rope(qkv[:, AW + h * HD:AW + (h + 1) * HD], kg_ref[...])
            vo_ref[h] = qkv[:, AW + (NKV + h) * HD:AW + (NKV + h + 1) * HD].astype(BF16)

    kv_rows = kv_into[0].shape[1]
    tab = pl.BlockSpec((tm, HD), lambda i: (i, 0))
    any_spec = pl.BlockSpec(memory_space=pl.ANY)
    kv_spec = pl.BlockSpec((NKV, tm, HD), lambda i: (0, rb + i, 0))
    return pl.pallas_call(
        body, grid=(ni,),
        in_specs=[pl.BlockSpec((tm, d), lambda i: (i, 0)),
                  pl.BlockSpec((halo, d), lambda i: (jnp.maximum(i * r - 1, 0), 0)),
                  pl.BlockSpec((halo, d), lambda i: (jnp.minimum((i + 1) * r, last), 0)),
                  pl.BlockSpec(w_in_t.shape, lambda i: (0, 0)), _vec(HD), _vec(HD), tab, tab,
                  pl.BlockSpec((3, CW), lambda i: (0, 0)), any_spec, any_spec],
        out_specs=[pl.BlockSpec((tm, nproj), lambda i: (i, 0)), pl.BlockSpec((NQ, tm, HD), lambda i: (0, i, 0)),
                   kv_spec, kv_spec, pl.BlockSpec((tm, CW), lambda i: (i, 0))],
        out_shape=[jax.ShapeDtypeStruct((n, nproj), F32), jax.ShapeDtypeStruct((NQ, n, HD), BF16),
                   jax.ShapeDtypeStruct((NKV, kv_rows, HD), BF16), jax.ShapeDtypeStruct((NKV, kv_rows, HD), BF16),
                   jax.ShapeDtypeStruct((n, CW), BF16)],
        input_output_aliases={9: 2, 10: 3}, name=name,
        compiler_params=_params("parallel"))(a, a, a, w_in_t, q_gain, k_gain, cs, sn, conv_w, *kv_into)


def _qkv_bwd(p, dq, dk, dv, q_gain, k_gain, cs, sn, *, name, has_q, kv_col, kv_row_off, tm=256):
    n = p.shape[0]
    rope = cs is not None
    rb = kv_row_off // tm

    def body(*refs):
        it = iter(refs)
        q_ref = next(it) if has_q else None
        kv_ref = next(it)
        dq_ref = next(it) if has_q else None
        dk_ref, dv_ref = next(it), next(it)
        qg_ref, kg_ref = next(it), next(it)
        cs_ref = next(it) if rope else None
        sn_ref = next(it) if rope else None
        dp_ref, dqg_ref, dkg_ref = next(it), next(it), next(it)
        i = pl.program_id(0)

        def back(xh, dout, gain):
            if rope:
                dout = dout * cs_ref[...] + _partner(dout * sn_ref[...])
            r = lax.rsqrt(jnp.mean(xh * xh, axis=-1, keepdims=True) + EPS)
            xhat = xh * r
            dxh = dout * gain
            dx = r * (dxh - xhat * jnp.mean(dxh * xhat, axis=-1, keepdims=True))
            return dx, _colsum(dout * xhat)

        dqg = jnp.zeros((1, HD), F32)
        dkg = jnp.zeros((1, HD), F32)
        if has_q:
            for h in range(NQ):
                dx, dg = back(q_ref[:, h * HD:(h + 1) * HD], dq_ref[h], qg_ref[...])
                dp_ref[:, h * HD:(h + 1) * HD] = dx.astype(BF16)
                dqg = dqg + dg
        else:
            dp_ref[:, 0:AW] = jnp.zeros((tm, AW), BF16)
        for h in range(NKV):
            dx, dg = back(kv_ref[:, h * HD:(h + 1) * HD], dk_ref[h], kg_ref[...])
            dp_ref[:, AW + h * HD:AW + (h + 1) * HD] = dx.astype(BF16)
            dkg = dkg + dg
            dp_ref[:, AW + (NKV + h) * HD:AW + (NKV + h + 1) * HD] = dv_ref[h].astype(BF16)
        _acc_out(dqg_ref, i, dqg)
        _acc_out(dkg_ref, i, dkg)

    in_specs, args = [], []
    if has_q:
        in_specs.append(pl.BlockSpec((tm, AW), lambda i: (i, 0)))
        args.append(p)
    in_specs.append(pl.BlockSpec((tm, 2 * NKV * HD), lambda i: (i, kv_col)))
    args.append(p)
    if has_q:
        in_specs.append(pl.BlockSpec((NQ, tm, HD), lambda i: (0, i, 0)))
        args.append(dq)
    in_specs += [pl.BlockSpec((NKV, tm, HD), lambda i: (0, rb + i, 0))] * 2 + [_vec(HD), _vec(HD)]
    args += [dk, dv, q_gain, k_gain]
    if rope:
        in_specs += [pl.BlockSpec((tm, HD), lambda i: (i, 0))] * 2
        args += [cs, sn]
    return pl.pallas_call(
        body, grid=(n // tm,), in_specs=in_specs,
        out_specs=[pl.BlockSpec((tm, D), lambda i: (i, 0)), _vec(HD), _vec(HD)],
        out_shape=[jax.ShapeDtypeStruct((n, D), BF16), jax.ShapeDtypeStruct((1, HD), F32),
                   jax.ShapeDtypeStruct((1, HD), F32)],
        name=name, compiler_params=_params("arbitrary"))(*args)


def _out_proj_dx_conv_bwd(dy, w_out, p, conv_w, *, name, tm=256):
    n, d = dy.shape
    ni = n // tm
    rows = tm + 2 * HALO

    def body(z_ref, zp_ref, zn_ref, wo_ref, gb_ref, gbp_ref, gbn_ref, gc_ref, gcp_ref, gcn_ref, xi_ref, xip_ref,
             xin_ref, w_ref, do_ref, dp_ref, dw_ref):
        i = pl.program_id(0)
        zext = jnp.concatenate([jnp.where(i > 0, zp_ref[...], jnp.zeros_like(zp_ref[...])), z_ref[...],
                                jnp.where(i < ni - 1, zn_ref[...], jnp.zeros_like(zn_ref[...]))], axis=0)
        do_ref[...] = lax.dot_general(z_ref[...], wo_ref[0:AW, :], _NT, preferred_element_type=F32)
        dconv = lax.dot_general(zext, wo_ref[AW:D, :], _NT, preferred_element_type=F32)[HALO:HALO + rows]
        gcext = _ext(gcp_ref, gc_ref, gcn_ref, i, ni)
        xiext = _ext(xip_ref, xi_ref, xin_ref, i, ni)
        hext = gcext * xiext
        dcv = dconv * _ext(gbp_ref, gb_ref, gbn_ref, i, ni)
        dp_ref[:, 0:CW] = (dconv[HALO:HALO + tm] * _conv3(hext, w_ref, tm)).astype(BF16)
        dh = _sh(dcv, 1, tm) * w_ref[0:1, :] + _sh(dcv, 0, tm) * w_ref[1:2, :] + _sh(dcv, -1, tm) * w_ref[2:3, :]
        dp_ref[:, CW:2 * CW] = (dh * xi_ref[...]).astype(BF16)
        dp_ref[:, 2 * CW:3 * CW] = (dh * gc_ref[...]).astype(BF16)
        dcv_t = dcv[HALO:HALO + tm]
        dw = jnp.concatenate([_colsum(dcv_t * _sh(hext, -1, tm)), _colsum(dcv_t * _sh(hext, 0, tm)),
                              _colsum(dcv_t * _sh(hext, 1, tm))], axis=0)
        _acc_out(dw_ref, i, dw)

    def trio(colblk):
        prev, nxt = _halo_specs(tm, CW, n, colblk=colblk)
        return [pl.BlockSpec((tm, CW), lambda i: (i, colblk)), prev, nxt]

    r16, last16 = tm // 16, n // 16 - 1
    zspecs = [pl.BlockSpec((tm, d), lambda i: (i, 0)),
              pl.BlockSpec((16, d), lambda i: (jnp.maximum(i * r16 - 1, 0), 0)),
              pl.BlockSpec((16, d), lambda i: (jnp.minimum((i + 1) * r16, last16), 0))]
    return pl.pallas_call(
        body, grid=(ni,),
        in_specs=zspecs + [pl.BlockSpec(w_out.shape, lambda i: (0, 0))] + trio(2) + trio(3) + trio(4)
        + [pl.BlockSpec((3, CW), lambda i: (0, 0))],
        out_specs=[pl.BlockSpec((tm, AW), lambda i: (i, 0)), pl.BlockSpec((tm, 3 * CW), lambda i: (i, 0)),
                   pl.BlockSpec((3, CW), lambda i: (0, 0))],
        out_shape=[jax.ShapeDtypeStruct((n, AW), F32), jax.ShapeDtypeStruct((n, 3 * CW), BF16),
                   jax.ShapeDtypeStruct((3, CW), F32)],
        name=name, compiler_params=_params("arbitrary"))(dy, dy, dy, w_out, p, p, p, p, p, p, p, p, p, conv_w)


def _attn_fwd(q, k, v, *, name, bq=512, sub=256):
    n = q.shape[1]
    t = k.shape[1]
    bq = min(bq, n)
    sub = min(sub, 2 * bq)

    def body(q_ref, k_ref, v_ref, o_ref, lse_ref):
        q2 = q_ref[...].reshape(2 * bq, HD)
        outs, lses = [], []
        for r0 in range(0, 2 * bq, sub):
            s = lax.dot_general(q2[r0:r0 + sub], k_ref[0], _NT, preferred_element_type=F32)
            m = jnp.max(s, axis=-1, keepdims=True)
            pv = jnp.exp2(s - m)
            l = jnp.sum(pv, axis=-1, keepdims=True)
            outs.append(jnp.dot(pv.astype(BF16), v_ref[0], preferred_element_type=F32) / l)
            lses.append(m + jnp.log2(l))
        out = jnp.concatenate(outs, axis=0)
        o_ref[:, 0:HD] = out[0:bq]
        o_ref[:, HD:2 * HD] = out[bq:2 * bq]
        lse_ref[...] = jnp.concatenate(lses, axis=0).reshape(2, bq, 1)

    kspec = pl.BlockSpec((1, t, HD), lambda h, i: (h, 0, 0))
    return pl.pallas_call(
        body, grid=(NKV, n // bq),
        in_specs=[pl.BlockSpec((2, bq, HD), lambda h, i: (h, i, 0)), kspec, kspec],
        out_specs=[pl.BlockSpec((bq, 2 * HD), lambda h, i: (i, h)), pl.BlockSpec((2, bq, 1), lambda h, i: (h, i, 0))],
        out_shape=[jax.ShapeDtypeStruct((n, AW), F32), jax.ShapeDtypeStruct((NQ, n, 1), F32)],
        name=name, compiler_params=_params("parallel", "parallel"))(q, k, v)


def _attn_bwd(q, k, v, dcat, o, lse, *, name, bq=256):
    n = q.shape[1]
    t = k.shape[1]
    bq = min(bq, n)

    def body(q_ref, k_ref, v_ref, dc_ref, o_ref, lse_ref, dq_ref, dk_ref, dv_ref):
        @pl.when(pl.program_id(1) == 0)
        def _():
            dk_ref[...] = jnp.zeros_like(dk_ref)
            dv_ref[...] = jnp.zeros_like(dv_ref)

        q2 = q_ref[...].reshape(2 * bq, HD)
        do_f = jnp.concatenate([dc_ref[:, 0:HD], dc_ref[:, HD:2 * HD]], axis=0)
        o_f = jnp.concatenate([o_ref[:, 0:HD], o_ref[:, HD:2 * HD]], axis=0)
        delta = jnp.sum(do_f * o_f, axis=-1, keepdims=True)
        do2 = do_f.astype(BF16)
        s = lax.dot_general(q2, k_ref[0], _NT, preferred_element_type=F32)
        pv = jnp.exp2(s - lse_ref[...].reshape(2 * bq, 1))
        dp = lax.dot_general(do2, v_ref[0], _NT, preferred_element_type=F32)
        ds = (pv * (dp - delta)).astype(BF16)
        dq_ref[...] = (jnp.dot(ds, k_ref[0], preferred_element_type=F32) * _SCALE).reshape(2, bq, HD)
        dk_ref[0] += lax.dot_general(ds, q2, _TN, preferred_element_type=F32) * _LN2
        dv_ref[0] += lax.dot_general(pv.astype(BF16), do2, _TN, preferred_element_type=F32)

    qspec = pl.BlockSpec((2, bq, HD), lambda h, i: (h, i, 0))
    kspec = pl.BlockSpec((1, t, HD), lambda h, i: (h, 0, 0))
    sspec = pl.BlockSpec((2, bq, 1), lambda h, i: (h, i, 0))
    cspec = pl.BlockSpec((bq, 2 * HD), lambda h, i: (i, h))
    return pl.pallas_call(
        body, grid=(NKV, n // bq), in_specs=[qspec, kspec, kspec, cspec, cspec, sspec], out_specs=[qspec, kspec, kspec],
        out_shape=[jax.ShapeDtypeStruct((NQ, n, HD), F32), jax.ShapeDtypeStruct((NKV, t, HD), F32),
                   jax.ShapeDtypeStruct((NKV, t, HD), F32)],
        name=name, compiler_params=_params("parallel", "arbitrary"))(q, k, v, dcat, o, lse)


def _window_sums(ext, w):
    s, step = ext, 1
    while step < w:
        s = s + _roll_rows(s, step)
        step *= 2
    return s


def _pool_counts(i, tm, n, w, rows, first):
    t = i * tm - HALO + first + lax.broadcasted_iota(jnp.int32, (rows, 1), 0)
    lo = jnp.clip(t - w // 2, 0, n)
    hi = jnp.clip(t + w - w // 2, 0, n)
    return jnp.maximum(hi - lo, 1).astype(F32)


def _norm_mod_ext(xext, gain_ref, sc_ref, sh_ref, i, tm, n):
    rows = xext.shape[0]
    t = i * tm - HALO + lax.broadcasted_iota(jnp.int32, (rows, 1), 0)
    inside = (t >= 0) & (t < n)
    r = lax.rsqrt(jnp.mean(xext * xext, axis=-1, keepdims=True) + EPS)
    xh = xext * r
    a = (xh * gain_ref[...]) * (1.0 + sc_ref[...]) + sh_ref[...]
    return jnp.where(inside, a, 0.0), r, xh


def _pool_fwd(x, gain, sc, sh, pool_w, *, name, tm=256):
    n, d = x.shape
    ni = n // tm

    def body(x_ref, xp_ref, xn_ref, gain_ref, sc_ref, sh_ref, w_ref, o_ref):
        i = pl.program_id(0)
        xext = _ext(xp_ref, x_ref, xn_ref, i, ni)
        aext, _, _ = _norm_mod_ext(xext, gain_ref, sc_ref, sh_ref, i, tm, n)
        for gi, w in enumerate(POOL_WINDOWS):
            ag = aext[:, gi * PG:(gi + 1) * PG]
            mean = _sh(_window_sums(ag, w), -(w // 2), tm) / _pool_counts(i, tm, n, w, tm, HALO)
            pooled = mean - ag[HALO:HALO + tm]
            o_ref[:, gi * PG:(gi + 1) * PG] = jnp.dot(pooled.astype(BF16), w_ref[gi], preferred_element_type=F32)

    row = pl.BlockSpec((tm, d), lambda i: (i, 0))
    prev, nxt = _halo_specs(tm, d, n)
    return pl.pallas_call(
        body, grid=(ni,),
        in_specs=[row, prev, nxt, _vec(d), _vec(d), _vec(d), pl.BlockSpec((4, PG, PG), lambda i: (0, 0, 0))],
        out_specs=row, out_shape=jax.ShapeDtypeStruct((n, d), F32),
        name=name, compiler_params=_params("parallel"))(x, x, x, gain, sc, sh, pool_w)


def _pool_bwd(dxo, mixed, x, g, scale, gain, sc, sh, pool_w, zprev, gprev, *, name, tm=256):
    n, d = x.shape
    ni = n // tm

    def body(dx_ref, dxp_ref, dxn_ref, mx_ref, x_ref, xp_ref, xn_ref, g_ref, s_ref, gain_ref, sc_ref, sh_ref, w_ref,
             zp_ref, gp_ref, dxi_ref, dw_ref, dg_ref, dsl_ref, dsh_ref, dsc_ref, dgn_ref, dzp_ref, dgp_ref):
        i = pl.program_id(0)

        @pl.when(i == 0)
        def _():
            dw_ref[...] = jnp.zeros_like(dw_ref)

        dxo_t = dx_ref[...]
        mixed_t = mx_ref[...]
        dy_t = dxo_t * g_ref[...]
        _acc_out(dg_ref, i, _colsum(dxo_t * (mixed_t * s_ref[...])))
        _acc_out(dsl_ref, i, _colsum(dy_t * mixed_t))
        dmixed = (_ext(dxp_ref, dx_ref, dxn_ref, i, ni) * g_ref[...]) * s_ref[...]
        xext = _ext(xp_ref, x_ref, xn_ref, i, ni)
        aext, rext, xhext = _norm_mod_ext(xext, gain_ref, sc_ref, sh_ref, i, tm, n)
        rows = tm + 2 * HALO
        da_parts = []
        for gi, w in enumerate(POOL_WINDOWS):
            sl = slice(gi * PG, (gi + 1) * PG)
            ag = aext[:, sl]
            mean = _sh(_window_sums(ag, w), -(w // 2), tm) / _pool_counts(i, tm, n, w, tm, HALO)
            pooled = (mean - ag[HALO:HALO + tm]).astype(BF16)
            dmg = dmixed[:, sl].astype(BF16)
            dw_ref[gi] += lax.dot_general(pooled, dmixed[HALO:HALO + tm, sl].astype(BF16), _TN,
                                          preferred_element_type=F32)
            dpl = lax.dot_general(dmg, w_ref[gi], _NT, preferred_element_type=F32)
            e = dpl / _pool_counts(i, tm, n, w, rows, 0)
            da_parts.append(_sh(_window_sums(e, w), 1 - w // 2, tm) - dpl[HALO:HALO + tm])
        da = jnp.concatenate(da_parts, axis=1)
        r = rext[HALO:HALO + tm]
        xh = xhext[HALO:HALO + tm]
        nrm = xh * gain_ref[...]
        dn = da * (1.0 + sc_ref[...])
        dxh = dn * gain_ref[...]
        dxi = dxo_t + r * (dxh - xh * jnp.mean(dxh * xh, axis=-1, keepdims=True))
        dxi_ref[...] = dxi
        _acc_out(dsh_ref, i, _colsum(da))
        _acc_out(dsc_ref, i, _colsum(da * nrm))
        _acc_out(dgn_ref, i, _colsum(dn * xh))
        dzp_ref[...] = (dxi * gp_ref[...]).astype(BF16)
        _acc_out(dgp_ref, i, _colsum(dxi * zp_ref[...]))

    row = pl.BlockSpec((tm, d), lambda i: (i, 0))
    prev, nxt = _halo_specs(tm, d, n)
    wspec = pl.BlockSpec((4, PG, PG), lambda i: (0, 0, 0))
    vshape = jax.ShapeDtypeStruct((1, d), F32)
    return pl.pallas_call(
        body, grid=(ni,),
        in_specs=[row, prev, nxt, row, row, prev, nxt] + [_vec(d)] * 5 + [wspec, row, _vec(d)],
        out_specs=[row, wspec] + [_vec(d)] * 5 + [row, _vec(d)],
        out_shape=[jax.ShapeDtypeStruct((n, d), F32), jax.ShapeDtypeStruct((4, PG, PG), F32)] + [vshape] * 5
        + [jax.ShapeDtypeStruct((n, d), BF16), vshape],
        name=name, compiler_params=_params("arbitrary"))(dxo, dxo, dxo, mixed, x, x, x, g, scale, gain, sc, sh, pool_w,
                                                         zprev, gprev)


def _adamw(gparts_list, w, m, v, *, name, silu_grad_of=None):
    nl = len(gparts_list)
    nparts, r, c = gparts_list[0].shape
    tr = _pick(r, (352, 256, 128, 64, 32, 16, 8))
    has_c = silu_grad_of is not None

    def body(*refs):
        gp_refs = refs[:nl]
        it = iter(refs[nl:])
        w_ref, m_ref, v_ref = next(it), next(it), next(it)
        c_ref = next(it) if has_c else None
        g_ref, d_ref, mo_ref, vo_ref = next(it), next(it), next(it), next(it)
        layer = pl.program_id(0)

        def update(gp_ref):
            g = gp_ref[0].astype(F32)
            for p in range(1, nparts):
                g = g + gp_ref[p].astype(F32)
            if has_c:
                cv = c_ref[0]
                sg = _sigmoid(cv)
                g = g * (sg * (1.0 + cv * (1.0 - sg)))
            g_ref[0] = g
            mn = ADAM_B1 * m_ref[0] + (1.0 - ADAM_B1) * g
            vn = ADAM_B2 * v_ref[0] + (1.0 - ADAM_B2) * (g * g)
            m_hat = mn / (1.0 - ADAM_B1 ** ADAM_STEP)
            v_hat = vn / (1.0 - ADAM_B2 ** ADAM_STEP)
            d_ref[0] = -ADAM_LR * (m_hat / (jnp.sqrt(v_hat) + ADAM_EPS) + ADAM_WD * w_ref[0])
            mo_ref[0] = mn
            vo_ref[0] = vn

        if nl == 1:
            update(gp_refs[0])
        else:
            for li in range(nl):
                pl.when(layer == li)(functools.partial(update, gp_refs[li]))

    row = pl.BlockSpec((1, tr, c), lambda l, i: (l, i, 0))
    in_specs = [pl.BlockSpec((nparts, tr, c), lambda l, i, li=li: (0, jnp.where(l == li, i, 0), 0)) for li in range(nl)]
    in_specs += [row, row, row]
    args = list(gparts_list) + [w, m, v]
    if has_c:
        in_specs.append(row)
        args.append(silu_grad_of)
    return pl.pallas_call(
        body, grid=(nl, r // tr), in_specs=in_specs, out_specs=[row] * 4,
        out_shape=[jax.ShapeDtypeStruct((nl, r, c), F32)] * 4, name=name,
        compiler_params=_params("arbitrary", "arbitrary"))(*args)


def _adamw_nd(gparts, w, m, v, *, name, silu_grad_of=None):
    shape = w.shape
    c = shape[-1]
    if isinstance(gparts, (list, tuple)):
        nl = len(gparts)
        r = math.prod(shape[1:-1])
    else:
        nl = 1
        r = math.prod(shape[:-1]) if len(shape) > 1 else 1
        gparts = [gparts]
    rs = lambda a: a.reshape(nl, r, c)
    res = _adamw([gp.reshape(gp.shape[0], r, c) for gp in gparts], rs(w), rs(m), rs(v), name=name,
                 silu_grad_of=None if silu_grad_of is None else rs(silu_grad_of))
    return [a.reshape(shape) for a in res]


def _place():
    return lax.axis_index("x"), lax.axis_index("y"), lax.axis_index("c")


def _all_gather(arrs, *, name):
    k_arr = len(arrs)

    def body(*refs):
        ins = refs[:k_arr]
        outs = refs[k_arr:2 * k_arr]
        send_sems, recv_sems, local_sems = refs[2 * k_arr:]
        x, y, c = _place()
        me, sibling = (x, y, c), (x, y, 1 - c)
        chips = [(1 - x, y), (x, 1 - y), (1 - x, 1 - y)]

        def slot(a, px, py, pc):
            return outs[a].at[4 * px + 2 * py + pc]

        def copy(a, s, block, to, src=None):
            return pltpu.make_async_remote_copy(
                src_ref=slot(a, *block) if src is None else src, dst_ref=slot(a, *block),
                send_sem=send_sems.at[a, s], recv_sem=recv_sems.at[a, s], device_id=to, device_id_type=MESH)

        mine = [pltpu.make_async_copy(ins[a], slot(a, *me), local_sems.at[a]) for a in range(k_arr)]
        for cp in mine:
            cp.start()
        first = []
        for a in range(k_arr):
            first.append(copy(a, 0, me, sibling, src=ins[a]))
            first += [copy(a, 1 + j, me, (*chip, c), src=ins[a]) for j, chip in enumerate(chips)]
        for cp in first:
            cp.start()
        passed = []
        for j, chip in enumerate(chips):
            for a in range(k_arr):
                copy(a, 1 + j, (*chip, c), me).wait_recv()
                fw = copy(a, 4 + j, (*chip, c), sibling)
                fw.start()
                passed.append(fw)
        for a in range(k_arr):
            copy(a, 0, sibling, me).wait_recv()
            for j, chip in enumerate(chips):
                copy(a, 4 + j, (*chip, 1 - c), me).wait_recv()
        for cp in first + passed:
            cp.wait_send()
        for cp in mine:
            cp.wait()

    any_spec = pl.BlockSpec(memory_space=pl.ANY)
    return pl.pallas_call(
        body, in_specs=[any_spec] * k_arr, out_specs=[any_spec] * k_arr,
        out_shape=[jax.ShapeDtypeStruct((NDEV,) + a.shape, a.dtype) for a in arrs],
        scratch_shapes=[pltpu.SemaphoreType.DMA((k_arr, 7)), pltpu.SemaphoreType.DMA((k_arr, 7)),
                        pltpu.SemaphoreType.DMA((k_arr,))],
        name=name)(*arrs)


_HBM = pl.BlockSpec(memory_space=pltpu.HBM)
_SEM = pl.BlockSpec(memory_space=pltpu.SEMAPHORE)
_EFFECT = pltpu.SideEffectType.DATAFLOW_SIDE_EFFECTING


def _peers(x, y, c):
    return [(x ^ (rel >> 2), y ^ ((rel >> 1) & 1), c ^ (rel & 1)) for rel in range(1, NDEV)]


def _exchange_copies(srcs, lands, send_sems, recv_sems, scatter):
    x, y, c = _place()
    me = 4 * x + 2 * y + c
    copies = []
    for r, (px, py, pc) in enumerate(_peers(x, y, c)):
        peer = 4 * px + 2 * py + pc
        for a in range(len(srcs)):
            copies.append(pltpu.make_async_remote_copy(
                src_ref=srcs[a].at[peer] if scatter else srcs[a], dst_ref=lands[a].at[me],
                send_sem=send_sems.at[7 * a + r], recv_sem=recv_sems.at[7 * a + r], device_id=(px, py, pc),
                device_id_type=MESH))
    return copies


def _exchange_start(arrs, *, scatter, name):
    k_arr = len(arrs)
    land_shapes = [a.shape if scatter else (NDEV,) + a.shape for a in arrs]
    lands = [pltpu.with_memory_space_constraint(lax.empty(s, a.dtype), pltpu.HBM) for s, a in zip(land_shapes, arrs)]
    srcs = [pltpu.with_memory_space_constraint(a, pltpu.HBM) for a in arrs]

    def body(*refs):
        src_refs, land_refs = refs[:k_arr], refs[k_arr:2 * k_arr]
        send_sems, recv_sems = refs[2 * k_arr], refs[2 * k_arr + 1]
        token = refs[-1]
        for cp in _exchange_copies(src_refs, land_refs, send_sems, recv_sems, scatter):
            cp.start()
        token[...] = jnp.zeros_like(token)

    out_shape = ([pltpu.SemaphoreType.DMA((7 * k_arr,)), pltpu.SemaphoreType.DMA((7 * k_arr,))]
                 + [pltpu.HBM(a.shape, a.dtype) for a in arrs] + [pltpu.HBM(s, a.dtype) for s, a in zip(land_shapes, arrs)]
                 + [jax.ShapeDtypeStruct((8, 128), F32)])
    res = pl.pallas_call(
        body, name=name, out_shape=out_shape, in_specs=[_HBM] * (2 * k_arr),
        out_specs=[_SEM, _SEM] + [_HBM] * (2 * k_arr) + [pl.BlockSpec(memory_space=pltpu.VMEM)],
        input_output_aliases={i: 2 + i for i in range(2 * k_arr)},
        compiler_params=pltpu.CompilerParams(has_side_effects=_EFFECT))(*srcs, *lands)
    return dict(send=res[0], recv=res[1], srcs=list(res[2:2 + k_arr]), lands=list(res[2 + k_arr:2 + 2 * k_arr]),
                token=res[-1], scatter=scatter)


def _exchange_wait(handle, after, *, name):
    k_arr = len(handle["srcs"])
    scatter = handle["scatter"]

    def body(*refs):
        src_refs, land_refs = refs[:k_arr], refs[k_arr:2 * k_arr]
        send_sems, recv_sems = refs[2 * k_arr], refs[2 * k_arr + 1]
        x, y, c = _place()
        me = 4 * x + 2 * y + c
        for r, (px, py, pc) in enumerate(_peers(x, y, c)):
            peer = 4 * px + 2 * py + pc
            for a in range(k_arr):
                cp = pltpu.make_async_remote_copy(
                    src_ref=src_refs[a].at[peer] if scatter else src_refs[a], dst_ref=land_refs[a].at[peer],
                    send_sem=send_sems.at[7 * a + r], recv_sem=recv_sems.at[7 * a + r], device_id=(x, y, c),
                    device_id_type=MESH)
                cp.wait_send()
                cp.wait_recv()

    arrs = handle["srcs"] + handle["lands"]
    res = pl.pallas_call(
        body, name=name, out_shape=[pltpu.HBM(a.shape, a.dtype) for a in arrs],
        in_specs=[_HBM] * (2 * k_arr) + [_SEM, _SEM, pl.BlockSpec(memory_space=pl.ANY)],
        out_specs=[_HBM] * (2 * k_arr), input_output_aliases={i: i for i in range(2 * k_arr)},
        compiler_params=pltpu.CompilerParams(has_side_effects=_EFFECT))(*arrs, handle["send"], handle["recv"], after)
    me = 4 * lax.axis_index("x") + 2 * lax.axis_index("y") + lax.axis_index("c")
    out = []
    for src, land in zip(res[:k_arr], res[k_arr:]):
        own = lax.dynamic_index_in_dim(src, me, 0, keepdims=False) if scatter else src
        out.append(lax.dynamic_update_index_in_dim(land, own, me, 0))
    return out


def _ffn_bwd(dxo, dz, xr, f, u_gc, hmid, gain, sc, w_up, cw, w_down, tag, gate_y=None, gate_g=None):
    d_wdown = _mm_tn((hmid, dz), name=f"ffn_down_dw_{tag}")
    dug, duv, dcw, dcb = _ffn_down_glu_bwd(dz, w_down, u_gc[0], u_gc[1], cw, name=f"ffn_down_glu_bwd_{tag}")
    d_wup = _mm_tn((dug, f), blocks=2, block=0, name=f"ffn_up_dwg_{tag}")
    d_wup = _mm_tn((duv, f), blocks=2, block=1, into=d_wup, name=f"ffn_up_dwv_{tag}")
    gated = gate_y is not None
    res = _mm_w_ep([dug, duv], w_up, _ep_norm_bwd(gated), [xr, dxo] + ([gate_y] if gated else []),
                   [gain, sc] + ([gate_g] if gated else []), [F32] + ([BF16] if gated else []),
                   [D] * (4 if gated else 3), name=f"ffn_up_dx_norm_bwd_{tag}")
    n_out = 2 if gated else 1
    return res[:n_out], res[n_out:], (d_wup, d_wdown, dcw, dcb)


def _split6(mod):
    return [mod[j * D:(j + 1) * D][None, :] for j in range(6)]


def _row(v):
    return v.reshape(1, -1)


def kernel(x, c, ctx, c_ctx, ada_w, ada_b, mix_norm, ffn_norm, even_w_in, even_q_gain, even_k_gain, even_conv_w, even_w_out, odd_pool_w, odd_pool_scale, ffn_w_up, ffn_conv_w, ffn_conv_b, ffn_w_down, loss_target, m_c_ctx, m_ada_w, m_ada_b, m_mix_norm, m_ffn_norm, m_even_w_in, m_even_q_gain, m_even_k_gain, m_even_conv_w, m_even_w_out, m_odd_pool_w, m_odd_pool_scale, m_ffn_w_up, m_ffn_conv_w, m_ffn_conv_b, m_ffn_w_down, v_c_ctx, v_ada_w, v_ada_b, v_mix_norm, v_ffn_norm, v_even_w_in, v_even_q_gain, v_even_k_gain, v_even_conv_w, v_even_w_out, v_odd_pool_w, v_odd_pool_scale, v_ffn_w_up, v_ffn_conv_w, v_ffn_conv_b, v_ffn_w_down):
    n = x.shape[1]
    lc = ctx.shape[1]
    me = 4 * lax.axis_index("x") + 2 * lax.axis_index("y") + lax.axis_index("c")
    xs, ctxs, tgt = x[0], ctx[0], loss_target[0]
    acols = ada_w.shape[2]

    small = jnp.concatenate([even_conv_w.reshape(-1), ffn_conv_w.reshape(-1), odd_pool_scale.reshape(-1)])
    nsmall = small.shape[0]
    small = jnp.pad(small, (0, (-nsmall) % 1024)).reshape(-1, 128)
    c_rows = jnp.pad(c, ((0, 7), (0, 0)))
    tr = lambda a: jnp.swapaxes(a, -1, -2)
    g_c, g_win, g_small = _all_gather([c_rows, tr(even_w_in[0]).astype(BF16), small], name="gather_first")
    w_in_t = g_win.reshape(-1, D)
    g_small = g_small.reshape(NDEV, -1)
    ecw = even_conv_w.shape[2]
    fcw = ffn_conv_w.shape[2]
    conv_w = g_small[:, :3 * ecw].reshape(NDEV, 3, ecw).transpose(1, 0, 2).reshape(3, CW)
    o1 = 3 * ecw
    fconv_w = g_small[:, o1:o1 + 6 * fcw].reshape(NDEV, 2, 3, fcw).transpose(1, 2, 0, 3).reshape(2, 3, DFF)
    o2 = o1 + 6 * fcw
    pool_scale = g_small[:, o2:o2 + D // NDEV].reshape(1, D)

    mraw = jnp.concatenate([g_c[:, 0, :], c_ctx[None, :], jnp.zeros((7, D), F32)], axis=0)
    my_bias = lax.dynamic_slice_in_dim(ada_b, me * acols, acols, axis=1)
    modp = jnp.stack([_mm(mraw, ada_w[l], silu_a=True, bias=my_bias[l:l + 1], name=f"ada_proj_{l}", tm=16, tn=256)
                      for l in range(2)])
    (g_mod,) = _all_gather([modp], name="gather_mod")
    mod_rows = g_mod.transpose(1, 2, 0, 3).reshape(2, 16, 6 * D)
    late_shards = [even_w_out[0].astype(BF16), odd_pool_w[0].astype(BF16), tr(ffn_w_up[0]).astype(BF16),
                   tr(ffn_w_up[1]).astype(BF16), ffn_w_down[0].astype(BF16), ffn_w_down[1].astype(BF16)]
    late_shards, mod_rows = lax.optimization_barrier((late_shards, mod_rows))
    h_weights = _exchange_start(late_shards, scatter=False, name="weights_start")
    mod_rows = mod_rows + h_weights["token"][0, 0]
    mod = lax.dynamic_index_in_dim(mod_rows, me, axis=1, keepdims=False)
    sh1, sc1, g1, sh2, sc2, g2 = _split6(mod[0])
    sh1b, sc1b, g1b, sh2b, sc2b, g2b = _split6(mod[1])
    csh1, csc1 = _split6(mod_rows[0, 8])[:2]
    mixn = [_row(mix_norm[l]) for l in range(2)]
    ffnn = [_row(ffn_norm[l]) for l in range(2)]
    qg, kg = _row(even_q_gain[0]), _row(even_k_gain[0])
    fcb = [_row(ffn_conv_b[l]) for l in range(2)]

    cs_t, sn_t = _rope_tables(n)
    a_lat = _norm_mod(xs, mixn[0], sc1, sh1, name="mix0_norm")
    a_ctx = _norm_mod(ctxs, mixn[0], csc1, csh1, name="mix0_norm_ctx")
    p_ctx = _mm(a_ctx, w_in_t[AW:AW + 4 * HD], tb=True, name="in_proj_ctx", tm=256, tn=512, tk=1024)
    kv_ctx = _qkv_prep(p_ctx, qg, kg, None, None, has_q=False, kv_col=0, kv_rows=lc + n, name="qkv_prep_ctx")
    p_lat, q_r, k_all, v_all, conv = _in_proj_qkv(a_lat, w_in_t, qg, kg, cs_t, sn_t, conv_w, kv_ctx, kv_row_off=lc,
                                                  name="in_proj_qkv")
    o_attn, lse = _attn_fwd(q_r, k_all, v_all, name="attn_fwd")
    g_wout, g_pool, g_up0, g_up1, g_down0, g_down1 = _exchange_wait(h_weights, o_attn, name="weights_wait")
    w_out = g_wout.reshape(D, D)
    pool_w = g_pool.transpose(1, 0, 2, 3).reshape(4, PG, PG)
    w_up_t = [g_up0.reshape(2 * DFF, D), g_up1.reshape(2 * DFF, D)]
    w_up = [w.T for w in w_up_t]
    w_down = [g_down0.reshape(DFF, D), g_down1.reshape(DFF, D)]
    y0, x1, f0 = _mm_w_ep([o_attn, conv], w_out, _ep_resid_norm, [xs], [g1, ffnn[0], sc2, sh2], [F32, F32, BF16], [],
                          tm=512, name="out_proj_norm")[:3]
    *u0, h0 = _ffn_up_glu(f0, w_up[0], fconv_w[0], fcb[0], name="ffn_up_glu_l0")
    z0, x2 = _mm_w_ep(h0, w_down[0], _ep_resid, [x1], [g2], [F32, F32], [], tm=512, name="ffn_down_resid_l0")[:2]

    mixed = _pool_fwd(x2, mixn[1], sc1b, sh1b, pool_w, name="pool_fwd")
    x3, f1 = _norm_mod(x2, ffnn[1], sc2b, sh2b, y=mixed, g=g1b, ymul=pool_scale, name="ffn_norm_l1")
    *u1, h1 = _ffn_up_glu(f1, w_up[1], fconv_w[1], fcb[1], name="ffn_up_glu_l1")
    dx4, dz1, loss_part, dg2b = _mm_w_ep(h1, w_down[1], _ep_loss(D), [x3, tgt], [g2b], [F32, BF16], [128, D],
                                         tm=512, name="ffn_down_loss")

    (dx3,), (dsh2b, dsc2b, dffn1), (dup1, ddown1, dfcw1, dfcb1) = _ffn_bwd(
        dx4, dz1, x3, f1, u1, h1, ffnn[1], sc2b, w_up_t[1], fconv_w[1], w_down[1], "l1")
    dx2, dpool_w, dg1b, dpscale, dsh1b, dsc1b, dmix1, dz0, dg2 = _pool_bwd(
        dx3, mixed, x2, g1b, pool_scale, mixn[1], sc1b, sh1b, pool_w, z0, g2, name="pool_bwd")

    s_pool = dpool_w.astype(BF16).reshape(4, NDEV, PG // NDEV, PG).transpose(1, 0, 2, 3)
    h_g1 = _exchange_start([s_pool, dup1.reshape(NDEV, -1, D), ddown1.reshape(NDEV, DFF // NDEV, D)], scatter=True,
                           name="grads1_start")

    (dx1, dy0), (dsh2, dsc2, dffn0, dg1), (dup0, ddown0, dfcw0, dfcb0) = _ffn_bwd(
        dx2, dz0, x1, f0, u0, h0, ffnn[0], sc2, w_up_t[0], fconv_w[0] + h_g1["token"][0, 0], w_down[0], "l0",
        gate_y=y0, gate_g=g1)
    h_g0 = _exchange_start([dup0.reshape(NDEV, -1, D), ddown0.reshape(NDEV, DFF // NDEV, D)], scatter=True,
                           name="grads0_start")
    d_attn, dp_conv, dconv_w = _out_proj_dx_conv_bwd(dy0, w_out, p_lat, conv_w + h_g0["token"][0, 0],
                                                     name="out_proj_dx_conv_bwd")
    d_wout = _mm_tn((o_attn, dy0), blocks=2, block=0, name="out_proj_dw_attn")
    d_wout = _mm_tn((conv, dy0), blocks=2, block=1, into=d_wout, name="out_proj_dw_conv")
    dq_r, dk_all, dv_all = _attn_bwd(q_r, k_all, v_all, d_attn, o_attn, lse, name="attn_bwd")
    dp_qkv, dqg_l, dkg_l = _qkv_bwd(p_lat, dq_r, dk_all, dv_all, qg, kg, cs_t, sn_t, has_q=True, kv_col=1,
                                    kv_row_off=lc, name="qkv_bwd")
    dp_ctx, _zero_qg, dkg_c = _qkv_bwd(p_ctx, None, dk_all, dv_all, qg, kg, None, None, has_q=False, kv_col=0,
                                       kv_row_off=0, name="qkv_bwd_ctx")
    da_ctx = _mm(dp_ctx, w_in_t[:D], name="in_proj_dx_ctx", tm=256, tn=512, tk=1024)
    d_win_qkv = _mm_tn([(dp_qkv, a_lat), (dp_ctx, a_ctx)], name="in_proj_dw_qkv")
    d_win_conv = _mm_tn((dp_conv, a_lat), name="in_proj_dw_conv")
    d_win_t = jnp.concatenate([d_win_qkv, d_win_conv], axis=0)
    grad_x, dsh1, dsc1, dmix0 = _mm_w_ep([dp_qkv, dp_conv], w_in_t, _ep_norm_bwd(False), [xs, dx1], [mixn[0], sc1],
                                         [F32], [D] * 3, tm=512, name="in_proj_dx_norm_bwd")
    _dctx, dcsh1, dcsc1, dmix0c = _norm_mod_bwd(da_ctx, ctxs, mixn[0], csc1, name="mix0_norm_bwd_ctx")

    z1k = jnp.zeros((1, D), F32)
    pack = jnp.concatenate(
        [v.reshape(-1) for v in (dsh1, dsc1, dg1, dsh2, dsc2, dg2, dsh1b, dsc1b, dg1b, dsh2b, dsc2b, dg2b,
                                 dcsh1, dcsc1, z1k, z1k, z1k, z1k,
                                 dmix0, dmix1, dmix0c, z1k, dffn0, dffn1, dqg_l, dkg_l + dkg_c,
                                 dfcb0, dfcb1, dconv_w, dfcw0, dfcw1, dpscale, loss_part[:, 0:1])])
    npack = pack.shape[0]
    pack = jnp.pad(pack, (0, (-npack) % 1024)).reshape(-1, 128)
    (g_pack,) = _all_gather([pack], name="gather_small_grads")
    def split(gp):
        off = [0]

        def take(size):
            seg = gp[:, off[0]:off[0] + size]
            off[0] += size
            return seg

        return (take(12 * D).reshape(NDEV, 2, 6 * D),
                take(6 * D).reshape(NDEV, 1, 6 * D),
                take(4 * D).reshape(NDEV, 2, 2, D),
                take(2 * D).reshape(NDEV, 2, D), take(HD).reshape(NDEV, 1, HD), take(HD).reshape(NDEV, 1, HD),
                take(2 * DFF).reshape(NDEV, 2, DFF), take(3 * CW).reshape(NDEV, 3, CW),
                take(6 * DFF).reshape(NDEV, 2, 3, DFF), take(D).reshape(NDEV, D), take(1))

    gp_all = g_pack.reshape(NDEV, -1)
    dmod_all, dmodc_all = split(gp_all)[:2]

    dmodc_sum = dmodc_all[0]
    for dev in range(1, NDEV):
        dmodc_sum = dmodc_sum + dmodc_all[dev]
    my_cols = lambda a: lax.dynamic_slice_in_dim(a, me * acols, acols, axis=a.ndim - 1)
    rows0 = jnp.concatenate([my_cols(dmod_all[:, 0]), my_cols(dmodc_sum), jnp.zeros((7, acols), F32)], axis=0)
    rows1 = jnp.concatenate([my_cols(dmod_all[:, 1]), jnp.zeros((8, acols), F32)], axis=0)
    d_ada = jnp.stack([_mm(mraw, rows, ta=True, silu_a=True, name=f"ada_dw_{l}", tm=512, tn=256, tk=16)
                       for l, rows in enumerate((rows0, rows1))])
    dscc_part = _mm(rows0, ada_w[0], tb=True, name="ada_dcctx", tm=16, tn=512, tk=256)
    (g_dscc,) = _all_gather([dscc_part[8:16]], name="gather_dcctx")

    attn_shards = [d_win_t.reshape(NDEV, -1, D), d_wout.reshape(NDEV, D // NDEV, D)]
    attn_shards, g_dscc = lax.optimization_barrier((attn_shards, g_dscc))
    h_ga = _exchange_start(attn_shards, scatter=True, name="grads_attn_start")
    (dmod_all, dmodc_all, dmix_all, dffn_all, dqg_all, dkg_all, dfcb_all, dconvw_all, dfcw_all, dpscale_all,
     loss_all) = split(gp_all + h_ga["token"][0, 0])
    d_ada = d_ada + h_ga["token"][0, 0]
    loss = loss_all[0, 0]
    for dev in range(1, NDEV):
        loss = loss + loss_all[dev, 0]

    outs = {}

    def put(nm, res):
        outs["grad_" + nm], outs["delta_" + nm], outs["new_m_" + nm], outs["new_v_" + nm] = res

    dmodc_pad = jnp.concatenate([dmodc_all, jnp.zeros_like(dmodc_all)], axis=1)
    put("ada_b", _adamw_nd(jnp.concatenate([dmod_all, dmodc_pad], axis=0), ada_b, m_ada_b, v_ada_b, name="adam_ada_b"))
    put("mix_norm", _adamw_nd(jnp.concatenate([dmix_all[:, 0], dmix_all[:, 1]], axis=0), mix_norm, m_mix_norm,
                              v_mix_norm, name="adam_mix_norm"))
    put("ffn_norm", _adamw_nd(dffn_all, ffn_norm, m_ffn_norm, v_ffn_norm, name="adam_ffn_norm"))
    put("even_q_gain", _adamw_nd(dqg_all, even_q_gain, m_even_q_gain, v_even_q_gain, name="adam_q_gain"))
    put("even_k_gain", _adamw_nd(dkg_all, even_k_gain, m_even_k_gain, v_even_k_gain, name="adam_k_gain"))
    put("ffn_conv_b", _adamw_nd(dfcb_all, ffn_conv_b, m_ffn_conv_b, v_ffn_conv_b, name="adam_ffn_conv_b"))
    my_convw = lax.dynamic_slice_in_dim(dconvw_all, me * ecw, ecw, axis=2)[:, None]
    put("even_conv_w", _adamw_nd(my_convw, even_conv_w, m_even_conv_w, v_even_conv_w, name="adam_even_conv_w"))
    my_fcw = lax.dynamic_slice_in_dim(dfcw_all, me * fcw, fcw, axis=3)
    put("ffn_conv_w", _adamw_nd(my_fcw, ffn_conv_w, m_ffn_conv_w, v_ffn_conv_w, name="adam_ffn_conv_w"))
    my_ps = lax.dynamic_slice_in_dim(dpscale_all, me * (D // NDEV), D // NDEV, axis=1)[:, None]
    put("odd_pool_scale", _adamw_nd(my_ps, odd_pool_scale, m_odd_pool_scale, v_odd_pool_scale, name="adam_pool_scale"))

    put("ada_w", _adamw_nd(d_ada[None], ada_w, m_ada_w, v_ada_w, name="adam_ada_w"))
    put("c_ctx", _adamw_nd(g_dscc[:, 0:1, :].reshape(NDEV, D), c_ctx, m_c_ctx, v_c_ctx, name="adam_c_ctx",
                           silu_grad_of=c_ctx))

    r_pool, r_up1, r_down1 = _exchange_wait(h_g1, outs["grad_ada_b"], name="grads1_wait")
    r_up0, r_down0 = _exchange_wait(h_g0, outs["grad_mix_norm"], name="grads0_wait")
    put("odd_pool_w", _adamw_nd(r_pool[:, None], odd_pool_w, m_odd_pool_w, v_odd_pool_w, name="adam_pool_w"))
    put("ffn_w_up", [tr(a) for a in _adamw_nd([r_up0, r_up1], tr(ffn_w_up), tr(m_ffn_w_up), tr(v_ffn_w_up),
                                              name="adam_w_up")])
    put("ffn_w_down", _adamw_nd([r_down0, r_down1], ffn_w_down, m_ffn_w_down, v_ffn_w_down, name="adam_w_down"))
    r_win, r_wout = _exchange_wait(h_ga, outs["grad_ffn_w_down"], name="grads_attn_wait")
    put("even_w_in", [tr(a) for a in _adamw_nd(r_win[:, None], tr(even_w_in), tr(m_even_w_in), tr(v_even_w_in),
                                               name="adam_w_in")])
    put("even_w_out", _adamw_nd(r_wout[:, None], even_w_out, m_even_w_out, v_even_w_out, name="adam_w_out"))

    names = ["c_ctx", "ada_w", "ada_b", "mix_norm", "ffn_norm", "even_w_in", "even_q_gain", "even_k_gain",
             "even_conv_w", "even_w_out", "odd_pool_w", "odd_pool_scale", "ffn_w_up", "ffn_conv_w", "ffn_conv_b",
             "ffn_w_down"]
    result = [loss, grad_x[None]]
    for kind in ("grad_", "delta_", "new_m_", "new_v_"):
        result += [outs[kind + nm] for nm in names]
    return tuple(result)
```

```python
import functools
import math

import jax
import jax.numpy as jnp
from jax import lax
from jax.experimental import pallas as pl
from jax.experimental.pallas import tpu as pltpu

F32 = jnp.float32
BF16 = jnp.bfloat16

D = 1024
HD = 128
NQ = 4
NKV = 2
AW = NQ * HD
CW = D - AW
DFF = 2816
GRID_W = 64
ROPE_THETA = 10000.0
POOL_WINDOWS = (2, 4, 8, 16)
PG = D // 4
EPS = 1e-6
NDEV = 8
HALO = 8
MESH = pl.DeviceIdType.MESH

ADAM_LR = 0.001
ADAM_B1 = 0.9
ADAM_B2 = 0.999
ADAM_EPS = 1e-08
ADAM_WD = 0.01
ADAM_STEP = 10


def _pick(dim, prefs):
    for p in prefs:
        if dim % p == 0:
            return p
    return dim


def _params(*sem):
    return pltpu.CompilerParams(dimension_semantics=sem)


_NT = (((1,), (1,)), ((), ()))
_TN = (((0,), (0,)), ((), ()))
_SCALE = HD ** -0.5
_QSCALE = _SCALE * math.log2(math.e)
_LN2 = math.log(2.0)


def _mm(a_list, b, *, name, ta=False, tb=False, out_dtype=F32, silu_a=False, bias=None, tm=None, tn=None, tk=None):
    if not isinstance(a_list, (list, tuple)):
        a_list = [a_list]
    na = len(a_list)
    assert not (ta and na > 1)
    if ta:
        kdim, m = a_list[0].shape
        ks = [kdim]
    else:
        m = a_list[0].shape[0]
        ks = [a.shape[1] for a in a_list]
        kdim = sum(ks)
    n = b.shape[0] if tb else b.shape[1]
    assert (b.shape[1] if tb else b.shape[0]) == kdim
    kunit = math.gcd(*ks) if na > 1 else kdim
    tm = min(tm, m) if tm else _pick(m, (512, 256, 128, 64, 32, 16, 8))
    tn = min(tn, n) if tn else _pick(n, (512, 256, 128))
    tk = min(tk, kunit) if tk else _pick(kunit, (1024, 768, 512, 256, 128))
    assert m % tm == 0 and n % tn == 0 and all(k % tk == 0 for k in ks)
    nks = [k // tk for k in ks]
    starts = [sum(nks[:i]) for i in range(na)]
    nk = sum(nks)
    has_bias = bias is not None

    def body(*refs):
        a_refs = refs[:na]
        b_ref = refs[na]
        bias_ref = refs[na + 1] if has_bias else None
        o_ref = refs[na + 1 + has_bias]
        acc = refs[-1]
        k = pl.program_id(2)

        @pl.when(k == 0)
        def _():
            acc[...] = jnp.zeros_like(acc)

        bv = b_ref[...].astype(BF16)
        dn = (((0 if ta else 1,), (1 if tb else 0,)), ((), ()))
        for idx in range(na):
            def step(idx=idx):
                av = a_refs[idx][...]
                if silu_a:
                    av = av * jax.nn.sigmoid(av)
                acc[...] += lax.dot_general(av.astype(BF16), bv, dn, preferred_element_type=F32)
            if na == 1:
                step()
            else:
                pl.when((k >= starts[idx]) & (k < starts[idx] + nks[idx]))(step)

        @pl.when(k == nk - 1)
        def _():
            r = acc[...]
            if has_bias:
                r = r + bias_ref[...]
            o_ref[...] = r.astype(o_ref.dtype)

    in_specs = []
    for idx in range(na):
        if ta:
            in_specs.append(pl.BlockSpec((tk, tm), lambda i, j, k: (k, i)))
        else:
            lo, cnt = starts[idx], nks[idx]
            in_specs.append(pl.BlockSpec((tm, tk), lambda i, j, k, lo=lo, cnt=cnt: (i, jnp.clip(k - lo, 0, cnt - 1))))
    if tb:
        in_specs.append(pl.BlockSpec((tn, tk), lambda i, j, k: (j, k)))
    else:
        in_specs.append(pl.BlockSpec((tk, tn), lambda i, j, k: (k, j)))
    args = list(a_list) + [b]
    if has_bias:
        in_specs.append(pl.BlockSpec((1, tn), lambda i, j, k: (0, j)))
        args.append(bias)
    return pl.pallas_call(
        body, grid=(m // tm, n // tn, nk), in_specs=in_specs,
        out_specs=pl.BlockSpec((tm, tn), lambda i, j, k: (i, j)),
        out_shape=jax.ShapeDtypeStruct((m, n), out_dtype),
        scratch_shapes=[pltpu.VMEM((tm, tn), F32)], name=name,
        compiler_params=_params("parallel", "parallel", "arbitrary"))(*args)


def _mm_w(a_list, w, *, name, tb=False, tm=256, out_dtype=F32):
    if not isinstance(a_list, (list, tuple)):
        a_list = [a_list]
    na = len(a_list)
    m = a_list[0].shape[0]
    ks = [a.shape[1] for a in a_list]
    offs = [sum(ks[:i]) for i in range(na)]
    n = w.shape[0] if tb else w.shape[1]
    assert (w.shape[1] if tb else w.shape[0]) == sum(ks)
    tm = min(tm, m)
    assert m % tm == 0

    def body(*refs):
        a_refs, w_ref, o_ref = refs[:na], refs[na], refs[na + 1]
        acc = None
        for idx in range(na):
            av = a_refs[idx][...].astype(BF16)
            if tb:
                part = lax.dot_general(av, w_ref[:, offs[idx]:offs[idx] + ks[idx]], _NT, preferred_element_type=F32)
            else:
                part = jnp.dot(av, w_ref[offs[idx]:offs[idx] + ks[idx], :], preferred_element_type=F32)
            acc = part if acc is None else acc + part
        o_ref[...] = acc.astype(o_ref.dtype)

    in_specs = [pl.BlockSpec((tm, k), lambda i: (i, 0)) for k in ks] + [pl.BlockSpec(w.shape, lambda i: (0, 0))]
    return pl.pallas_call(
        body, grid=(m // tm,), in_specs=in_specs, out_specs=pl.BlockSpec((tm, n), lambda i: (i, 0)),
        out_shape=jax.ShapeDtypeStruct((m, n), out_dtype), name=name, compiler_params=_params("parallel"))(*a_list, w)


def _mm_w_ep(a_list, w, epilogue, row_in, vec_in, out_dtypes, sum_widths, *, name, tb=False, tm=256, sub=256):
    if not isinstance(a_list, (list, tuple)):
        a_list = [a_list]
    na, nr, nv, no, ns = len(a_list), len(row_in), len(vec_in), len(out_dtypes), len(sum_widths)
    m = a_list[0].shape[0]
    ks = [a.shape[1] for a in a_list]
    offs = [sum(ks[:i]) for i in range(na)]
    n = w.shape[0] if tb else w.shape[1]
    assert (w.shape[1] if tb else w.shape[0]) == sum(ks)
    tm = min(tm, m)
    sub = min(sub, tm)
    assert m % tm == 0 and tm % sub == 0

    def body(*refs):
        a_refs, w_ref = refs[:na], refs[na]
        row_refs = refs[na + 1:na + 1 + nr]
        vec_refs = refs[na + 1 + nr:na + 1 + nr + nv]
        out_refs = refs[na + 1 + nr + nv:na + 1 + nr + nv + no]
        sum_refs = refs[na + 1 + nr + nv + no:]

        @pl.when(pl.program_id(0) == 0)
        def _():
            for s_ref in sum_refs:
                s_ref[...] = jnp.zeros_like(s_ref)

        vecs = [v[...] for v in vec_refs]
        for r0 in range(0, tm, sub):
            acc = None
            for idx in range(na):
                av = a_refs[idx][r0:r0 + sub, :].astype(BF16)
                if tb:
                    part = lax.dot_general(av, w_ref[:, offs[idx]:offs[idx] + ks[idx]], _NT, preferred_element_type=F32)
                else:
                    part = jnp.dot(av, w_ref[offs[idx]:offs[idx] + ks[idx], :], preferred_element_type=F32)
                acc = part if acc is None else acc + part
            outs, sums = epilogue(acc, [r[r0:r0 + sub, :] for r in row_refs], vecs)
            for o_ref, o in zip(out_refs, outs):
                o_ref[r0:r0 + sub, :] = o.astype(o_ref.dtype)
            for s_ref, s in zip(sum_refs, sums):
                s_ref[...] += s

    row = pl.BlockSpec((tm, n), lambda i: (i, 0))
    in_specs = ([pl.BlockSpec((tm, k), lambda i: (i, 0)) for k in ks] + [pl.BlockSpec(w.shape, lambda i: (0, 0))]
                + [row] * nr + [_vec(n)] * nv)
    return pl.pallas_call(
        body, grid=(m // tm,), in_specs=in_specs, out_specs=[row] * no + [_vec(sw) for sw in sum_widths],
        out_shape=[jax.ShapeDtypeStruct((m, n), dt) for dt in out_dtypes]
        + [jax.ShapeDtypeStruct((1, sw), F32) for sw in sum_widths],
        name=name, compiler_params=_params("arbitrary" if ns else "parallel"))(*a_list, w, *row_in, *vec_in)


def _ep_norm_bwd(has_gate):
    def ep(dav, rows, vecs):
        xv = rows[0]
        gain, scv = vecs[0], vecs[1]
        r = lax.rsqrt(jnp.mean(xv * xv, axis=-1, keepdims=True) + EPS)
        xh = xv * r
        nrm = xh * gain
        dn = dav * (1.0 + scv)
        dxh = dn * gain
        dx = r * (dxh - xh * jnp.mean(dxh * xh, axis=-1, keepdims=True)) + rows[1]
        outs, sums = [dx], [_colsum(dav), _colsum(dav * nrm), _colsum(dn * xh)]
        if has_gate:
            outs.append(dx * vecs[2])
            sums.append(_colsum(dx * rows[2]))
        return outs, sums
    return ep


def _ep_loss(d):
    def ep(zv, rows, vecs):
        xv, tv = rows
        gv = vecs[0]
        diff = (xv + gv * zv) - tv
        dx = diff * (1.0 / d)
        part = 0.5 * jnp.sum(jnp.mean(diff * diff, axis=-1, keepdims=True), axis=0, keepdims=True)
        return [dx, dx * gv], [jnp.broadcast_to(part, (1, 128)), _colsum(dx * zv)]
    return ep


def _ep_resid(zv, rows, vecs):
    return [zv, rows[0] + vecs[0] * zv], []


def _ep_resid_norm(yv, rows, vecs):
    g, gain, scv, shv = vecs
    xv = rows[0] + g * yv
    r = lax.rsqrt(jnp.mean(xv * xv, axis=-1, keepdims=True) + EPS)
    return [yv, xv, ((xv * r) * gain) * (1.0 + scv) + shv], []


def _mm_tn(pairs, *, name, tk=1024, out_dtype=BF16, blocks=1, block=0, into=None):
    if not isinstance(pairs, list):
        pairs = [pairs]
    m, n = pairs[0][0].shape[1], pairs[0][1].shape[1]
    tks = [min(tk, a.shape[0]) for a, _ in pairs]
    nks = [a.shape[0] // t for (a, _), t in zip(pairs, tks)]
    assert all(a.shape[0] == b.shape[0] and a.shape[0] % t == 0 for (a, b), t in zip(pairs, tks))
    starts = [sum(nks[:i]) for i in range(len(pairs))]
    nk = sum(nks)

    def body(*refs):
        o_ref, acc = refs[-2], refs[-1]
        k = pl.program_id(0)

        @pl.when(k == 0)
        def _():
            acc[...] = jnp.zeros_like(acc)

        for idx in range(len(pairs)):
            a_ref, b_ref = refs[2 * idx], refs[2 * idx + 1]

            def step(a_ref=a_ref, b_ref=b_ref):
                acc[...] += lax.dot_general(a_ref[...].astype(BF16), b_ref[...].astype(BF16), _TN,
                                            preferred_element_type=F32)

            if len(pairs) == 1:
                step()
            else:
                pl.when((k >= starts[idx]) & (k < starts[idx] + nks[idx]))(step)

        @pl.when(k == nk - 1)
        def _():
            o_ref[...] = acc[...].astype(o_ref.dtype)

    in_specs, args = [], []
    for (a, b), t, lo, cnt in zip(pairs, tks, starts, nks):
        idx_map = lambda k, lo=lo, cnt=cnt: (jnp.clip(k - lo, 0, cnt - 1), 0)
        in_specs += [pl.BlockSpec((t, m), idx_map), pl.BlockSpec((t, n), idx_map)]
        args += [a, b]
    aliases = {}
    if into is not None:
        aliases = {len(args): 0}
        in_specs.append(pl.BlockSpec(memory_space=pl.ANY))
        args.append(into)
    return pl.pallas_call(
        body, grid=(nk,), in_specs=in_specs, out_specs=pl.BlockSpec((m, n), lambda k: (block, 0)),
        out_shape=jax.ShapeDtypeStruct((m * blocks, n), out_dtype), scratch_shapes=[pltpu.VMEM((m, n), F32)],
        input_output_aliases=aliases, name=name, compiler_params=_params("arbitrary"))(*args)


def _vec(d, col=None):
    if col is None:
        return pl.BlockSpec((1, d), lambda i, *_: (0, 0))
    return pl.BlockSpec((1, d), col)


def _halo_specs(tm, width, nrows, colblk=0, row_off=0):
    r = tm // HALO
    off = row_off // HALO
    last = nrows // HALO - 1
    prev = pl.BlockSpec((HALO, width), lambda i, *_: (off + jnp.maximum(i * r - 1, 0), colblk))
    nxt = pl.BlockSpec((HALO, width), lambda i, *_: (off + jnp.minimum((i + 1) * r, last), colblk))
    return prev, nxt


def _ext(prev_ref, main_ref, next_ref, i, ni):
    p = jnp.where(i > 0, prev_ref[...], 0.0)
    n = jnp.where(i < ni - 1, next_ref[...], 0.0)
    return jnp.concatenate([p, main_ref[...], n], axis=0)


def _sh(ext, k, tm):
    if k == 0:
        return ext[HALO:HALO + tm]
    rows = ext.shape[0]
    return pltpu.roll(ext, (-k) % rows, axis=0)[HALO:HALO + tm]


def _roll_rows(v, k):
    rows = v.shape[0]
    return pltpu.roll(v, (-k) % rows, axis=0) if k % rows else v


def _conv3(ext, w_ref, tm):
    return _sh(ext, -1, tm) * w_ref[0:1, :] + _sh(ext, 0, tm) * w_ref[1:2, :] + _sh(ext, 1, tm) * w_ref[2:3, :]


def _colsum(v):
    return jnp.sum(v, axis=0, keepdims=True)


def _acc_out(ref, i, val):
    @pl.when(i == 0)
    def _():
        ref[...] = jnp.zeros_like(ref)

    ref[...] += val


def _sigmoid(v):
    return jax.nn.sigmoid(v)


def _norm_mod(x, gain, sc, sh, *, name, y=None, g=None, ymul=None, tm=512):
    n, d = x.shape
    tm = min(tm, n)
    has_res = y is not None
    has_mul = ymul is not None

    def body(*refs):
        it = iter(refs)
        x_ref = next(it)
        y_ref = next(it) if has_res else None
        g_ref = next(it) if has_res else None
        m_ref = next(it) if has_mul else None
        gain_ref, sc_ref, sh_ref = next(it), next(it), next(it)
        xo_ref = next(it) if has_res else None
        a_ref = next(it)
        xv = x_ref[...]
        if has_res:
            yv = y_ref[...]
            if has_mul:
                yv = yv * m_ref[...]
            xv = xv + g_ref[...] * yv
            xo_ref[...] = xv
        r = lax.rsqrt(jnp.mean(xv * xv, axis=-1, keepdims=True) + EPS)
        nrm = (xv * r) * gain_ref[...]
        a_ref[...] = (nrm * (1.0 + sc_ref[...]) + sh_ref[...]).astype(BF16)

    row = pl.BlockSpec((tm, d), lambda i: (i, 0))
    in_specs, args = [row], [x]
    if has_res:
        in_specs += [row, _vec(d)]
        args += [y, g]
    if has_mul:
        in_specs.append(_vec(d))
        args.append(ymul)
    in_specs += [_vec(d)] * 3
    args += [gain, sc, sh]
    out_specs, out_shape = [], []
    if has_res:
        out_specs.append(row)
        out_shape.append(jax.ShapeDtypeStruct((n, d), F32))
    out_specs.append(row)
    out_shape.append(jax.ShapeDtypeStruct((n, d), BF16))
    res = pl.pallas_call(body, grid=(n // tm,), in_specs=in_specs, out_specs=out_specs, out_shape=out_shape,
                         name=name, compiler_params=_params("parallel"))(*args)
    return res if has_res else res[0]


def _norm_mod_bwd(da, x, gain, sc, *, name, dres=None, gate_y=None, gate_g=None, tm=512):
    n, d = x.shape
    tm = min(tm, n)
    has_res = dres is not None
    has_gate = gate_y is not None

    def body(*refs):
        it = iter(refs)
        da_ref, x_ref = next(it), next(it)
        r_ref = next(it) if has_res else None
        y_ref = next(it) if has_gate else None
        g_ref = next(it) if has_gate else None
        gain_ref, sc_ref = next(it), next(it)
        dx_ref, dsh_ref, dsc_ref, dgn_ref = next(it), next(it), next(it), next(it)
        dy_ref = next(it) if has_gate else None
        dg_ref = next(it) if has_gate else None
        i = pl.program_id(0)
        xv = x_ref[...]
        dav = da_ref[...]
        r = lax.rsqrt(jnp.mean(xv * xv, axis=-1, keepdims=True) + EPS)
        xh = xv * r
        nrm = xh * gain_ref[...]
        dn = dav * (1.0 + sc_ref[...])
        dxh = dn * gain_ref[...]
        dx = r * (dxh - xh * jnp.mean(dxh * xh, axis=-1, keepdims=True))
        if has_res:
            dx = dx + r_ref[...]
        dx_ref[...] = dx
        _acc_out(dsh_ref, i, _colsum(dav))
        _acc_out(dsc_ref, i, _colsum(dav * nrm))
        _acc_out(dgn_ref, i, _colsum(dn * xh))
        if has_gate:
            dy_ref[...] = (dx * g_ref[...]).astype(BF16)
            _acc_out(dg_ref, i, _colsum(dx * y_ref[...]))

    row = pl.BlockSpec((tm, d), lambda i: (i, 0))
    in_specs, args = [row, row], [da, x]
    if has_res:
        in_specs.append(row)
        args.append(dres)
    if has_gate:
        in_specs += [row, _vec(d)]
        args += [gate_y, gate_g]
    in_specs += [_vec(d)] * 2
    args += [gain, sc]
    vec_shape = jax.ShapeDtypeStruct((1, d), F32)
    out_specs = [row, _vec(d), _vec(d), _vec(d)]
    out_shape = [jax.ShapeDtypeStruct((n, d), F32), vec_shape, vec_shape, vec_shape]
    if has_gate:
        out_specs += [row, _vec(d)]
        out_shape += [jax.ShapeDtypeStruct((n, d), BF16), vec_shape]
    return pl.pallas_call(
        body, grid=(n // tm,), in_specs=in_specs, out_specs=out_specs, out_shape=out_shape,
        name=name, compiler_params=_params("arbitrary"))(*args)


def _ffn_up_glu(f, w_up, cw, cb, *, name, tm=256, tc=256):
    n, d = f.shape
    tm = min(tm, n)
    ni = n // tm
    nc = DFF // tc
    halo = 16
    rows = tm + 2 * halo
    r = tm // halo
    last = n // halo - 1

    def body(f_ref, fp_ref, fn_ref, w_ref, cw_ref, cb_ref, u_ref, gc_ref, h_ref):
        i = pl.program_id(0)
        a = f_ref[...]
        aext = jnp.concatenate([jnp.where(i > 0, fp_ref[...], jnp.zeros_like(fp_ref[...])), a,
                                jnp.where(i < ni - 1, fn_ref[...], jnp.zeros_like(fn_ref[...]))], axis=0)
        for j in range(nc):
            cols = slice(j * tc, (j + 1) * tc)
            vcols = slice(DFF + j * tc, DFF + (j + 1) * tc)
            gext = jnp.dot(aext, w_ref[:, cols], preferred_element_type=F32)
            val = jnp.dot(a, w_ref[:, vcols], preferred_element_type=F32)
            gate = gext[halo:halo + tm]
            gc = (pltpu.roll(gext, 1, axis=0)[halo:halo + tm] * cw_ref[0:1, cols] + gate * cw_ref[1:2, cols]
                  + pltpu.roll(gext, rows - 1, axis=0)[halo:halo + tm] * cw_ref[2:3, cols]) + cb_ref[:, cols]
            u_ref[:, cols] = gate
            u_ref[:, vcols] = val
            gc_ref[:, cols] = gc
            h_ref[:, cols] = (gc * _sigmoid(gc) * val).astype(BF16)

    return pl.pallas_call(
        body, grid=(ni,),
        in_specs=[pl.BlockSpec((tm, d), lambda i: (i, 0)),
                  pl.BlockSpec((halo, d), lambda i: (jnp.maximum(i * r - 1, 0), 0)),
                  pl.BlockSpec((halo, d), lambda i: (jnp.minimum((i + 1) * r, last), 0)),
                  pl.BlockSpec(w_up.shape, lambda i: (0, 0)), pl.BlockSpec((3, DFF), lambda i: (0, 0)),
                  pl.BlockSpec((1, DFF), lambda i: (0, 0))],
        out_specs=[pl.BlockSpec((tm, 2 * DFF), lambda i: (i, 0)), pl.BlockSpec((tm, DFF), lambda i: (i, 0)),
                   pl.BlockSpec((tm, DFF), lambda i: (i, 0))],
        out_shape=[jax.ShapeDtypeStruct((n, 2 * DFF), F32), jax.ShapeDtypeStruct((n, DFF), F32),
                   jax.ShapeDtypeStruct((n, DFF), BF16)], name=name,
        compiler_params=_params("parallel"))(f, f, f, w_up, cw, cb)


def _ffn_down_glu_bwd(dz, w_down, u, gc, cw, *, name, tm=256, tc=256):
    n, d = dz.shape
    tm = min(tm, n)
    ni = n // tm
    nc = DFF // tc
    rows = tm + 2 * HALO

    def body(z_ref, zp_ref, zn_ref, w_ref, u_ref, vp_ref, vn_ref, c_ref, cp_ref, cn_ref, cw_ref,
             dg_ref, dv_ref, dcw_ref, dcb_ref):
        i = pl.program_id(0)

        @pl.when(i == 0)
        def _():
            dcw_ref[...] = jnp.zeros_like(dcw_ref)
            dcb_ref[...] = jnp.zeros_like(dcb_ref)

        zext = jnp.concatenate([jnp.where(i > 0, zp_ref[...], jnp.zeros_like(zp_ref[...])), z_ref[...],
                                jnp.where(i < ni - 1, zn_ref[...], jnp.zeros_like(zn_ref[...]))], axis=0)
        for j in range(nc):
            cols = slice(j * tc, (j + 1) * tc)
            vcols = slice(DFF + j * tc, DFF + (j + 1) * tc)
            dh = lax.dot_general(zext, w_ref[cols, :], _NT, preferred_element_type=F32)[HALO:HALO + rows]
            gcx = jnp.concatenate([cp_ref[:, cols], c_ref[:, cols], cn_ref[:, cols]], axis=0)
            vext = jnp.concatenate([vp_ref[:, cols], u_ref[:, vcols], vn_ref[:, cols]], axis=0)
            sg = _sigmoid(gcx)
            dgc = dh * vext * (sg * (1.0 + gcx * (1.0 - sg)))
            dv_ref[:, cols] = (dh[HALO:HALO + tm] * (gcx[HALO:HALO + tm] * sg[HALO:HALO + tm])).astype(BF16)
            d_next = pltpu.roll(dgc, rows - 1, axis=0)[HALO:HALO + tm]
            d_prev = pltpu.roll(dgc, 1, axis=0)[HALO:HALO + tm]
            d_here = dgc[HALO:HALO + tm]
            dg_ref[:, cols] = (d_next * cw_ref[0:1, cols] + d_here * cw_ref[1:2, cols]
                               + d_prev * cw_ref[2:3, cols]).astype(BF16)
            gate = u_ref[:, cols]
            dcw_ref[:, cols] += jnp.concatenate([_colsum(d_next * gate), _colsum(d_here * gate),
                                                 _colsum(d_prev * gate)], axis=0)
            dcb_ref[:, cols] += _colsum(d_here)

    def trio(width, halo, tile_width=None, colblk=0):
        r, last = tm // halo, n // halo - 1
        return [pl.BlockSpec((tm, tile_width or width), lambda i: (i, 0)),
                pl.BlockSpec((halo, width), lambda i: (jnp.maximum(i * r - 1, 0), colblk)),
                pl.BlockSpec((halo, width), lambda i: (jnp.minimum((i + 1) * r, last), colblk))]

    whole = lambda shape: pl.BlockSpec(shape, lambda i: (0, 0))
    return pl.pallas_call(
        body, grid=(ni,),
        in_specs=(trio(d, 16) + [whole(w_down.shape)] + trio(DFF, HALO, tile_width=2 * DFF, colblk=1)
                  + trio(DFF, HALO) + [whole((3, DFF))]),
        out_specs=[pl.BlockSpec((tm, DFF), lambda i: (i, 0)), pl.BlockSpec((tm, DFF), lambda i: (i, 0)),
                   whole((3, DFF)), whole((1, DFF))],
        out_shape=[jax.ShapeDtypeStruct((n, DFF), BF16), jax.ShapeDtypeStruct((n, DFF), BF16),
                   jax.ShapeDtypeStruct((3, DFF), F32), jax.ShapeDtypeStruct((1, DFF), F32)],
        name=name, compiler_params=_params("arbitrary"))(dz, dz, dz, w_down, u, u, u, gc, gc, gc, cw)


def _rope_tables(n):
    rows = n // GRID_W
    axis_dim = HD // 2
    inv_freq = jnp.power(ROPE_THETA, -jnp.arange(0, axis_dim, 2, dtype=F32) / axis_dim)
    ar = jnp.arange(rows, dtype=F32)[:, None] * inv_freq
    ac = jnp.arange(GRID_W, dtype=F32)[:, None] * inv_freq
    by_row = lambda a: jnp.repeat(a, GRID_W, axis=0)
    by_col = lambda a: jnp.tile(a, (rows, 1))
    cr, sr, cc, sc = by_row(jnp.cos(ar)), by_row(jnp.sin(ar)), by_col(jnp.cos(ac)), by_col(jnp.sin(ac))
    return jnp.concatenate([cr, cr, cc, cc], axis=1), jnp.concatenate([-sr, sr, -sc, sc], axis=1)


def _partner(v):
    lane = lax.broadcasted_iota(jnp.int32, v.shape, 1)
    return jnp.where((lane % 64) < 32, pltpu.roll(v, HD - 32, axis=1), pltpu.roll(v, 32, axis=1))


def _qkv_prep(p, q_gain, k_gain, cs, sn, *, name, has_q, kv_col, kv_rows=None, kv_row_off=0, kv_into=None, tm=256):
    n = p.shape[0]
    rope = cs is not None
    kv_rows = kv_rows or n
    rb = kv_row_off // tm

    def body(*refs):
        it = iter(refs)
        q_ref = next(it) if has_q else None
        kv_ref = next(it)
        qg_ref, kg_ref = next(it), next(it)
        cs_ref = next(it) if rope else None
        sn_ref = next(it) if rope else None
        if kv_into is not None:
            next(it), next(it)
        qo_ref = next(it) if has_q else None
        ko_ref, vo_ref = next(it), next(it)

        def norm_rope(xh, gain, mul=None):
            r = lax.rsqrt(jnp.mean(xh * xh, axis=-1, keepdims=True) + EPS)
            xn = (xh * r) * gain
            if rope:
                xn = xn * cs_ref[...] + _partner(xn) * sn_ref[...]
            if mul is not None:
                xn = xn * mul
            return xn.astype(BF16)

        if has_q:
            for h in range(NQ):
                qo_ref[h] = norm_rope(q_ref[:, h * HD:(h + 1) * HD], qg_ref[...], _QSCALE)
        for h in range(NKV):
            ko_ref[h] = norm_rope(kv_ref[:, h * HD:(h + 1) * HD], kg_ref[...])
            vo_ref[h] = kv_ref[:, (NKV + h) * HD:(NKV + h + 1) * HD].astype(BF16)

    in_specs, args = [], []
    if has_q:
        in_specs.append(pl.BlockSpec((tm, AW), lambda i: (i, 0)))
        args.append(p)
    in_specs += [pl.BlockSpec((tm, 2 * NKV * HD), lambda i: (i, kv_col)), _vec(HD), _vec(HD)]
    args += [p, q_gain, k_gain]
    if rope:
        in_specs += [pl.BlockSpec((tm, HD), lambda i: (i, 0))] * 2
        args += [cs, sn]
    out_specs, out_shape = [], []
    if has_q:
        out_specs.append(pl.BlockSpec((NQ, tm, HD), lambda i: (0, i, 0)))
        out_shape.append(jax.ShapeDtypeStruct((NQ, n, HD), BF16))
    out_specs += [pl.BlockSpec((NKV, tm, HD), lambda i: (0, rb + i, 0))] * 2
    out_shape += [jax.ShapeDtypeStruct((NKV, kv_rows, HD), BF16)] * 2
    aliases = {}
    if kv_into is not None:
        aliases = {len(args): int(has_q), len(args) + 1: int(has_q) + 1}
        in_specs += [pl.BlockSpec(memory_space=pl.ANY)] * 2
        args += list(kv_into)
    return pl.pallas_call(body, grid=(n // tm,), in_specs=in_specs, out_specs=out_specs, out_shape=out_shape,
                          input_output_aliases=aliases, name=name, compiler_params=_params("parallel"))(*args)


def _in_proj_qkv(a, w_in_t, q_gain, k_gain, cs, sn, conv_w, kv_into, *, name, kv_row_off, tm=256):
    n, d = a.shape
    nproj = w_in_t.shape[0]
    nqkv = AW + 2 * NKV * HD
    rb = kv_row_off // tm
    ni = n // tm
    halo = 16
    rows = tm + 2 * halo
    r = tm // halo
    last = n // halo - 1

    def body(a_ref, ap_ref, an_ref, w_ref, qg_ref, kg_ref, cs_ref, sn_ref, cw_ref, _k_in, _v_in,
             p_ref, qo_ref, ko_ref, vo_ref, conv_ref):
        i = pl.program_id(0)
        av = a_ref[...]
        aext = jnp.concatenate([jnp.where(i > 0, ap_ref[...], jnp.zeros_like(ap_ref[...])), av,
                                jnp.where(i < ni - 1, an_ref[...], jnp.zeros_like(an_ref[...]))], axis=0)
        qkv = lax.dot_general(av, w_ref[0:nqkv, :], _NT, preferred_element_type=F32)
        p_ref[:, 0:nqkv] = qkv
        cext = lax.dot_general(aext, w_ref[nqkv:nproj, :], _NT, preferred_element_type=F32)
        p_ref[:, nqkv:nproj] = cext[halo:halo + tm]
        hext = cext[:, CW:2 * CW] * cext[:, 2 * CW:3 * CW]
        cv3 = (pltpu.roll(hext, 1, axis=0)[halo:halo + tm] * cw_ref[0:1, :] + hext[halo:halo + tm] * cw_ref[1:2, :]
               + pltpu.roll(hext, rows - 1, axis=0)[halo:halo + tm] * cw_ref[2:3, :])
        conv_ref[...] = (cext[halo:halo + tm, 0:CW] * cv3).astype(BF16)

        def norm_rope(xh, gain, mul=None):
            r = lax.rsqrt(jnp.mean(xh * xh, axis=-1, keepdims=True) + EPS)
            xn = (xh * r) * gain
            xn = xn * cs_ref[...] + _partner(xn) * sn_ref[...]
            if mul is not None:
                xn = xn * mul
            return xn.astype(BF16)

        for h in range(NQ):
            qo_ref[h] = norm_rope(qkv[:, h * HD:(h + 1) * HD], qg_ref[...], _QSCALE)
        for h in range(NKV):
            ko_ref[h] = norm_rope(qkv[:, AW + h * HD:AW + (h + 1) * HD], kg_ref[...])
            vo_ref[h] = qkv[:, AW + (NKV + h) * HD:AW + (NKV + h + 1) * HD].astype(BF16)

    kv_rows = kv_into[0].shape[1]
    tab = pl.BlockSpec((tm, HD), lambda i: (i, 0))
    any_spec = pl.BlockSpec(memory_space=pl.ANY)
    kv_spec = pl.BlockSpec((NKV, tm, HD), lambda i: (0, rb + i, 0))
    return pl.pallas_call(
        body, grid=(ni,),
        in_specs=[pl.BlockSpec((tm, d), lambda i: (i, 0)),
                  pl.BlockSpec((halo, d), lambda i: (jnp.maximum(i * r - 1, 0), 0)),
                  pl.BlockSpec((halo, d), lambda i: (jnp.minimum((i + 1) * r, last), 0)),
                  pl.BlockSpec(w_in_t.shape, lambda i: (0, 0)), _vec(HD), _vec(HD), tab, tab,
                  pl.BlockSpec((3, CW), lambda i: (0, 0)), any_spec, any_spec],
        out_specs=[pl.BlockSpec((tm, nproj), lambda i: (i, 0)), pl.BlockSpec((NQ, tm, HD), lambda i: (0, i, 0)),
                   kv_spec, kv_spec, pl.BlockSpec((tm, CW), lambda i: (i, 0))],
        out_shape=[jax.ShapeDtypeStruct((n, nproj), F32), jax.ShapeDtypeStruct((NQ, n, HD), BF16),
                   jax.ShapeDtypeStruct((NKV, kv_rows, HD), BF16), jax.ShapeDtypeStruct((NKV, kv_rows, HD), BF16),
                   jax.ShapeDtypeStruct((n, CW), BF16)],
        input_output_aliases={9: 2, 10: 3}, name=name,
        compiler_params=_params("parallel"))(a, a, a, w_in_t, q_gain, k_gain, cs, sn, conv_w, *kv_into)


def _qkv_bwd(p, dq, dk, dv, q_gain, k_gain, cs, sn, *, name, has_q, kv_col, kv_row_off, tm=256):
    n = p.shape[0]
    rope = cs is not None
    rb = kv_row_off // tm

    def body(*refs):
        it = iter(refs)
        q_ref = next(it) if has_q else None
        kv_ref = next(it)
        dq_ref = next(it) if has_q else None
        dk_ref, dv_ref = next(it), next(it)
        qg_ref, kg_ref = next(it), next(it)
        cs_ref = next(it) if rope else None
        sn_ref = next(it) if rope else None
        dp_ref, dqg_ref, dkg_ref = next(it), next(it), next(it)
        i = pl.program_id(0)

        def back(xh, dout, gain):
            if rope:
                dout = dout * cs_ref[...] + _partner(dout * sn_ref[...])
            r = lax.rsqrt(jnp.mean(xh * xh, axis=-1, keepdims=True) + EPS)
            xhat = xh * r
            dxh = dout * gain
            dx = r * (dxh - xhat * jnp.mean(dxh * xhat, axis=-1, keepdims=True))
            return dx, _colsum(dout * xhat)

        dqg = jnp.zeros((1, HD), F32)
        dkg = jnp.zeros((1, HD), F32)
        if has_q:
            for h in range(NQ):
                dx, dg = back(q_ref[:, h * HD:(h + 1) * HD], dq_ref[h], qg_ref[...])
                dp_ref[:, h * HD:(h + 1) * HD] = dx.astype(BF16)
                dqg = dqg + dg
        else:
            dp_ref[:, 0:AW] = jnp.zeros((tm, AW), BF16)
        for h in range(NKV):
            dx, dg = back(kv_ref[:, h * HD:(h + 1) * HD], dk_ref[h], kg_ref[...])
            dp_ref[:, AW + h * HD:AW + (h + 1) * HD] = dx.astype(BF16)
            dkg = dkg + dg
            dp_ref[:, AW + (NKV + h) * HD:AW + (NKV + h + 1) * HD] = dv_ref[h].astype(BF16)
        _acc_out(dqg_ref, i, dqg)
        _acc_out(dkg_ref, i, dkg)

    in_specs, args = [], []
    if has_q:
        in_specs.append(pl.BlockSpec((tm, AW), lambda i: (i, 0)))
        args.append(p)
    in_specs.append(pl.BlockSpec((tm, 2 * NKV * HD), lambda i: (i, kv_col)))
    args.append(p)
    if has_q:
        in_specs.append(pl.BlockSpec((NQ, tm, HD), lambda i: (0, i, 0)))
        args.append(dq)
    in_specs += [pl.BlockSpec((NKV, tm, HD), lambda i: (0, rb + i, 0))] * 2 + [_vec(HD), _vec(HD)]
    args += [dk, dv, q_gain, k_gain]
    if rope:
        in_specs += [pl.BlockSpec((tm, HD), lambda i: (i, 0))] * 2
        args += [cs, sn]
    return pl.pallas_call(
        body, grid=(n // tm,), in_specs=in_specs,
        out_specs=[pl.BlockSpec((tm, D), lambda i: (i, 0)), _vec(HD), _vec(HD)],
        out_shape=[jax.ShapeDtypeStruct((n, D), BF16), jax.ShapeDtypeStruct((1, HD), F32),
                   jax.ShapeDtypeStruct((1, HD), F32)],
        name=name, compiler_params=_params("arbitrary"))(*args)


def _out_proj_dx_conv_bwd(dy, w_out, p, conv_w, *, name, tm=256):
    n, d = dy.shape
    ni = n // tm
    rows = tm + 2 * HALO

    def body(z_ref, zp_ref, zn_ref, wo_ref, gb_ref, gbp_ref, gbn_ref, gc_ref, gcp_ref, gcn_ref, xi_ref, xip_ref,
             xin_ref, w_ref, do_ref, dp_ref, dw_ref):
        i = pl.program_id(0)
        zext = jnp.concatenate([jnp.where(i > 0, zp_ref[...], jnp.zeros_like(zp_ref[...])), z_ref[...],
                                jnp.where(i < ni - 1, zn_ref[...], jnp.zeros_like(zn_ref[...]))], axis=0)
        do_ref[...] = lax.dot_general(z_ref[...], wo_ref[0:AW, :], _NT, preferred_element_type=F32)
        dconv = lax.dot_general(zext, wo_ref[AW:D, :], _NT, preferred_element_type=F32)[HALO:HALO + rows]
        gcext = _ext(gcp_ref, gc_ref, gcn_ref, i, ni)
        xiext = _ext(xip_ref, xi_ref, xin_ref, i, ni)
        hext = gcext * xiext
        dcv = dconv * _ext(gbp_ref, gb_ref, gbn_ref, i, ni)
        dp_ref[:, 0:CW] = (dconv[HALO:HALO + tm] * _conv3(hext, w_ref, tm)).astype(BF16)
        dh = _sh(dcv, 1, tm) * w_ref[0:1, :] + _sh(dcv, 0, tm) * w_ref[1:2, :] + _sh(dcv, -1, tm) * w_ref[2:3, :]
        dp_ref[:, CW:2 * CW] = (dh * xi_ref[...]).astype(BF16)
        dp_ref[:, 2 * CW:3 * CW] = (dh * gc_ref[...]).astype(BF16)
        dcv_t = dcv[HALO:HALO + tm]
        dw = jnp.concatenate([_colsum(dcv_t * _sh(hext, -1, tm)), _colsum(dcv_t * _sh(hext, 0, tm)),
                              _colsum(dcv_t * _sh(hext, 1, tm))], axis=0)
        _acc_out(dw_ref, i, dw)

    def trio(colblk):
        prev, nxt = _halo_specs(tm, CW, n, colblk=colblk)
        return [pl.BlockSpec((tm, CW), lambda i: (i, colblk)), prev, nxt]

    r16, last16 = tm // 16, n // 16 - 1
    zspecs = [pl.BlockSpec((tm, d), lambda i: (i, 0)),
              pl.BlockSpec((16, d), lambda i: (jnp.maximum(i * r16 - 1, 0), 0)),
              pl.BlockSpec((16, d), lambda i: (jnp.minimum((i + 1) * r16, last16), 0))]
    return pl.pallas_call(
        body, grid=(ni,),
        in_specs=zspecs + [pl.BlockSpec(w_out.shape, lambda i: (0, 0))] + trio(2) + trio(3) + trio(4)
        + [pl.BlockSpec((3, CW), lambda i: (0, 0))],
        out_specs=[pl.BlockSpec((tm, AW), lambda i: (i, 0)), pl.BlockSpec((tm, 3 * CW), lambda i: (i, 0)),
                   pl.BlockSpec((3, CW), lambda i: (0, 0))],
        out_shape=[jax.ShapeDtypeStruct((n, AW), F32), jax.ShapeDtypeStruct((n, 3 * CW), BF16),
                   jax.ShapeDtypeStruct((3, CW), F32)],
        name=name, compiler_params=_params("arbitrary"))(dy, dy, dy, w_out, p, p, p, p, p, p, p, p, p, conv_w)


def _attn_fwd(q, k, v, *, name, bq=512, sub=256):
    n = q.shape[1]
    t = k.shape[1]
    bq = min(bq, n)
    sub = min(sub, 2 * bq)

    def body(q_ref, k_ref, v_ref, o_ref, lse_ref):
        q2 = q_ref[...].reshape(2 * bq, HD)
        outs, lses = [], []
        for r0 in range(0, 2 * bq, sub):
            s = lax.dot_general(q2[r0:r0 + sub], k_ref[0], _NT, preferred_element_type=F32)
            m = jnp.max(s, axis=-1, keepdims=True)
            pv = jnp.exp2(s - m)
            l = jnp.sum(pv, axis=-1, keepdims=True)
            outs.append(jnp.dot(pv.astype(BF16), v_ref[0], preferred_element_type=F32) / l)
            lses.append(m + jnp.log2(l))
        out = jnp.concatenate(outs, axis=0)
        o_ref[:, 0:HD] = out[0:bq]
        o_ref[:, HD:2 * HD] = out[bq:2 * bq]
        lse_ref[...] = jnp.concatenate(lses, axis=0).reshape(2, bq, 1)

    kspec = pl.BlockSpec((1, t, HD), lambda h, i: (h, 0, 0))
    return pl.pallas_call(
        body, grid=(NKV, n // bq),
        in_specs=[pl.BlockSpec((2, bq, HD), lambda h, i: (h, i, 0)), kspec, kspec],
        out_specs=[pl.BlockSpec((bq, 2 * HD), lambda h, i: (i, h)), pl.BlockSpec((2, bq, 1), lambda h, i: (h, i, 0))],
        out_shape=[jax.ShapeDtypeStruct((n, AW), F32), jax.ShapeDtypeStruct((NQ, n, 1), F32)],
        name=name, compiler_params=_params("parallel", "parallel"))(q, k, v)


def _attn_bwd(q, k, v, dcat, o, lse, *, name, bq=256):
    n = q.shape[1]
    t = k.shape[1]
    bq = min(bq, n)

    def body(q_ref, k_ref, v_ref, dc_ref, o_ref, lse_ref, dq_ref, dk_ref, dv_ref):
        @pl.when(pl.program_id(1) == 0)
        def _():
            dk_ref[...] = jnp.zeros_like(dk_ref)
            dv_ref[...] = jnp.zeros_like(dv_ref)

        q2 = q_ref[...].reshape(2 * bq, HD)
        do_f = jnp.concatenate([dc_ref[:, 0:HD], dc_ref[:, HD:2 * HD]], axis=0)
        o_f = jnp.concatenate([o_ref[:, 0:HD], o_ref[:, HD:2 * HD]], axis=0)
        delta = jnp.sum(do_f * o_f, axis=-1, keepdims=True)
        do2 = do_f.astype(BF16)
        s = lax.dot_general(q2, k_ref[0], _NT, preferred_element_type=F32)
        pv = jnp.exp2(s - lse_ref[...].reshape(2 * bq, 1))
        dp = lax.dot_general(do2, v_ref[0], _NT, preferred_element_type=F32)
        ds = (pv * (dp - delta)).astype(BF16)
        dq_ref[...] = (jnp.dot(ds, k_ref[0], preferred_element_type=F32) * _SCALE).reshape(2, bq, HD)
        dk_ref[0] += lax.dot_general(ds, q2, _TN, preferred_element_type=F32) * _LN2
        dv_ref[0] += lax.dot_general(pv.astype(BF16), do2, _TN, preferred_element_type=F32)

    qspec = pl.BlockSpec((2, bq, HD), lambda h, i: (h, i, 0))
    kspec = pl.BlockSpec((1, t, HD), lambda h, i: (h, 0, 0))
    sspec = pl.BlockSpec((2, bq, 1), lambda h, i: (h, i, 0))
    cspec = pl.BlockSpec((bq, 2 * HD), lambda h, i: (i, h))
    return pl.pallas_call(
        body, grid=(NKV, n // bq), in_specs=[qspec, kspec, kspec, cspec, cspec, sspec], out_specs=[qspec, kspec, kspec],
        out_shape=[jax.ShapeDtypeStruct((NQ, n, HD), F32), jax.ShapeDtypeStruct((NKV, t, HD), F32),
                   jax.ShapeDtypeStruct((NKV, t, HD), F32)],
        name=name, compiler_params=_params("parallel", "arbitrary"))(q, k, v, dcat, o, lse)


def _window_sums(ext, w):
    s, step = ext, 1
    while step < w:
        s = s + _roll_rows(s, step)
        step *= 2
    return s


def _pool_counts(i, tm, n, w, rows, first):
    t = i * tm - HALO + first + lax.broadcasted_iota(jnp.int32, (rows, 1), 0)
    lo = jnp.clip(t - w // 2, 0, n)
    hi = jnp.clip(t + w - w // 2, 0, n)
    return jnp.maximum(hi - lo, 1).astype(F32)


def _norm_mod_ext(xext, gain_ref, sc_ref, sh_ref, i, tm, n):
    rows = xext.shape[0]
    t = i * tm - HALO + lax.broadcasted_iota(jnp.int32, (rows, 1), 0)
    inside = (t >= 0) & (t < n)
    r = lax.rsqrt(jnp.mean(xext * xext, axis=-1, keepdims=True) + EPS)
    xh = xext * r
    a = (xh * gain_ref[...]) * (1.0 + sc_ref[...]) + sh_ref[...]
    return jnp.where(inside, a, 0.0), r, xh


def _pool_fwd(x, gain, sc, sh, pool_w, *, name, tm=256):
    n, d = x.shape
    ni = n // tm

    def body(x_ref, xp_ref, xn_ref, gain_ref, sc_ref, sh_ref, w_ref, o_ref):
        i = pl.program_id(0)
        xext = _ext(xp_ref, x_ref, xn_ref, i, ni)
        aext, _, _ = _norm_mod_ext(xext, gain_ref, sc_ref, sh_ref, i, tm, n)
        for gi, w in enumerate(POOL_WINDOWS):
            ag = aext[:, gi * PG:(gi + 1) * PG]
            mean = _sh(_window_sums(ag, w), -(w // 2), tm) / _pool_counts(i, tm, n, w, tm, HALO)
            pooled = mean - ag[HALO:HALO + tm]
            o_ref[:, gi * PG:(gi + 1) * PG] = jnp.dot(pooled.astype(BF16), w_ref[gi], preferred_element_type=F32)

    row = pl.BlockSpec((tm, d), lambda i: (i, 0))
    prev, nxt = _halo_specs(tm, d, n)
    return pl.pallas_call(
        body, grid=(ni,),
        in_specs=[row, prev, nxt, _vec(d), _vec(d), _vec(d), pl.BlockSpec((4, PG, PG), lambda i: (0, 0, 0))],
        out_specs=row, out_shape=jax.ShapeDtypeStruct((n, d), F32),
        name=name, compiler_params=_params("parallel"))(x, x, x, gain, sc, sh, pool_w)


def _pool_bwd(dxo, mixed, x, g, scale, gain, sc, sh, pool_w, zprev, gprev, *, name, tm=256):
    n, d = x.shape
    ni = n // tm

    def body(dx_ref, dxp_ref, dxn_ref, mx_ref, x_ref, xp_ref, xn_ref, g_ref, s_ref, gain_ref, sc_ref, sh_ref, w_ref,
             zp_ref, gp_ref, dxi_ref, dw_ref, dg_ref, dsl_ref, dsh_ref, dsc_ref, dgn_ref, dzp_ref, dgp_ref):
        i = pl.program_id(0)

        @pl.when(i == 0)
        def _():
            dw_ref[...] = jnp.zeros_like(dw_ref)

        dxo_t = dx_ref[...]
        mixed_t = mx_ref[...]
        dy_t = dxo_t * g_ref[...]
        _acc_out(dg_ref, i, _colsum(dxo_t * (mixed_t * s_ref[...])))
        _acc_out(dsl_ref, i, _colsum(dy_t * mixed_t))
        dmixed = (_ext(dxp_ref, dx_ref, dxn_ref, i, ni) * g_ref[...]) * s_ref[...]
        xext = _ext(xp_ref, x_ref, xn_ref, i, ni)
        aext, rext, xhext = _norm_mod_ext(xext, gain_ref, sc_ref, sh_ref, i, tm, n)
        rows = tm + 2 * HALO
        da_parts = []
        for gi, w in enumerate(POOL_WINDOWS):
            sl = slice(gi * PG, (gi + 1) * PG)
            ag = aext[:, sl]
            mean = _sh(_window_sums(ag, w), -(w // 2), tm) / _pool_counts(i, tm, n, w, tm, HALO)
            pooled = (mean - ag[HALO:HALO + tm]).astype(BF16)
            dmg = dmixed[:, sl].astype(BF16)
            dw_ref[gi] += lax.dot_general(pooled, dmixed[HALO:HALO + tm, sl].astype(BF16), _TN,
                                          preferred_element_type=F32)
            dpl = lax.dot_general(dmg, w_ref[gi], _NT, preferred_element_type=F32)
            e = dpl / _pool_counts(i, tm, n, w, rows, 0)
            da_parts.append(_sh(_window_sums(e, w), 1 - w // 2, tm) - dpl[HALO:HALO + tm])
        da = jnp.concatenate(da_parts, axis=1)
        r = rext[HALO:HALO + tm]
        xh = xhext[HALO:HALO + tm]
        nrm = xh * gain_ref[...]
        dn = da * (1.0 + sc_ref[...])
        dxh = dn * gain_ref[...]
        dxi = dxo_t + r * (dxh - xh * jnp.mean(dxh * xh, axis=-1, keepdims=True))
        dxi_ref[...] = dxi
        _acc_out(dsh_ref, i, _colsum(da))
        _acc_out(dsc_ref, i, _colsum(da * nrm))
        _acc_out(dgn_ref, i, _colsum(dn * xh))
        dzp_ref[...] = (dxi * gp_ref[...]).astype(BF16)
        _acc_out(dgp_ref, i, _colsum(dxi * zp_ref[...]))

    row = pl.BlockSpec((tm, d), lambda i: (i, 0))
    prev, nxt = _halo_specs(tm, d, n)
    wspec = pl.BlockSpec((4, PG, PG), lambda i: (0, 0, 0))
    vshape = jax.ShapeDtypeStruct((1, d), F32)
    return pl.pallas_call(
        body, grid=(ni,),
        in_specs=[row, prev, nxt, row, row, prev, nxt] + [_vec(d)] * 5 + [wspec, row, _vec(d)],
        out_specs=[row, wspec] + [_vec(d)] * 5 + [row, _vec(d)],
        out_shape=[jax.ShapeDtypeStruct((n, d), F32), jax.ShapeDtypeStruct((4, PG, PG), F32)] + [vshape] * 5
        + [jax.ShapeDtypeStruct((n, d), BF16), vshape],
        name=name, compiler_params=_params("arbitrary"))(dxo, dxo, dxo, mixed, x, x, x, g, scale, gain, sc, sh, pool_w,
                                                         zprev, gprev)


def _adamw(gparts_list, w, m, v, *, name, silu_grad_of=None):
    nl = len(gparts_list)
    nparts, r, c = gparts_list[0].shape
    tr = _pick(r, (352, 256, 128, 64, 32, 16, 8))
    has_c = silu_grad_of is not None

    def body(*refs):
        gp_refs = refs[:nl]
        it = iter(refs[nl:])
        w_ref, m_ref, v_ref = next(it), next(it), next(it)
        c_ref = next(it) if has_c else None
        g_ref, d_ref, mo_ref, vo_ref = next(it), next(it), next(it), next(it)
        layer = pl.program_id(0)

        def update(gp_ref):
            g = gp_ref[0].astype(F32)
            for p in range(1, nparts):
                g = g + gp_ref[p].astype(F32)
            if has_c:
                cv = c_ref[0]
                sg = _sigmoid(cv)
                g = g * (sg * (1.0 + cv * (1.0 - sg)))
            g_ref[0] = g
            mn = ADAM_B1 * m_ref[0] + (1.0 - ADAM_B1) * g
            vn = ADAM_B2 * v_ref[0] + (1.0 - ADAM_B2) * (g * g)
            m_hat = mn / (1.0 - ADAM_B1 ** ADAM_STEP)
            v_hat = vn / (1.0 - ADAM_B2 ** ADAM_STEP)
            d_ref[0] = -ADAM_LR * (m_hat / (jnp.sqrt(v_hat) + ADAM_EPS) + ADAM_WD * w_ref[0])
            mo_ref[0] = mn
            vo_ref[0] = vn

        if nl == 1:
            update(gp_refs[0])
        else:
            for li in range(nl):
                pl.when(layer == li)(functools.partial(update, gp_refs[li]))

    row = pl.BlockSpec((1, tr, c), lambda l, i: (l, i, 0))
    in_specs = [pl.BlockSpec((nparts, tr, c), lambda l, i, li=li: (0, jnp.where(l == li, i, 0), 0)) for li in range(nl)]
    in_specs += [row, row, row]
    args = list(gparts_list) + [w, m, v]
    if has_c:
        in_specs.append(row)
        args.append(silu_grad_of)
    return pl.pallas_call(
        body, grid=(nl, r // tr), in_specs=in_specs, out_specs=[row] * 4,
        out_shape=[jax.ShapeDtypeStruct((nl, r, c), F32)] * 4, name=name,
        compiler_params=_params("arbitrary", "arbitrary"))(*args)


def _adamw_nd(gparts, w, m, v, *, name, silu_grad_of=None):
    shape = w.shape
    c = shape[-1]
    if isinstance(gparts, (list, tuple)):
        nl = len(gparts)
        r = math.prod(shape[1:-1])
    else:
        nl = 1
        r = math.prod(shape[:-1]) if len(shape) > 1 else 1
        gparts = [gparts]
    rs = lambda a: a.reshape(nl, r, c)
    res = _adamw([gp.reshape(gp.shape[0], r, c) for gp in gparts], rs(w), rs(m), rs(v), name=name,
                 silu_grad_of=None if silu_grad_of is None else rs(silu_grad_of))
    return [a.reshape(shape) for a in res]


def _place():
    return lax.axis_index("x"), lax.axis_index("y"), lax.axis_index("c")


def _all_gather(arrs, *, name):
    k_arr = len(arrs)

    def body(*refs):
        ins = refs[:k_arr]
        outs = refs[k_arr:2 * k_arr]
        send_sems, recv_sems, local_sems = refs[2 * k_arr:]
        x, y, c = _place()
        me, sibling = (x, y, c), (x, y, 1 - c)
        chips = [(1 - x, y), (x, 1 - y), (1 - x, 1 - y)]

        def slot(a, px, py, pc):
            return outs[a].at[4 * px + 2 * py + pc]

        def copy(a, s, block, to, src=None):
            return pltpu.make_async_remote_copy(
                src_ref=slot(a, *block) if src is None else src, dst_ref=slot(a, *block),
                send_sem=send_sems.at[a, s], recv_sem=recv_sems.at[a, s], device_id=to, device_id_type=MESH)

        mine = [pltpu.make_async_copy(ins[a], slot(a, *me), local_sems.at[a]) for a in range(k_arr)]
        for cp in mine:
            cp.start()
        first = []
        for a in range(k_arr):
            first.append(copy(a, 0, me, sibling, src=ins[a]))
            first += [copy(a, 1 + j, me, (*chip, c), src=ins[a]) for j, chip in enumerate(chips)]
        for cp in first:
            cp.start()
        passed = []
        for j, chip in enumerate(chips):
            for a in range(k_arr):
                copy(a, 1 + j, (*chip, c), me).wait_recv()
                fw = copy(a, 4 + j, (*chip, c), sibling)
                fw.start()
                passed.append(fw)
        for a in range(k_arr):
            copy(a, 0, sibling, me).wait_recv()
            for j, chip in enumerate(chips):
                copy(a, 4 + j, (*chip, 1 - c), me).wait_recv()
        for cp in first + passed:
            cp.wait_send()
        for cp in mine:
            cp.wait()

    any_spec = pl.BlockSpec(memory_space=pl.ANY)
    return pl.pallas_call(
        body, in_specs=[any_spec] * k_arr, out_specs=[any_spec] * k_arr,
        out_shape=[jax.ShapeDtypeStruct((NDEV,) + a.shape, a.dtype) for a in arrs],
        scratch_shapes=[pltpu.SemaphoreType.DMA((k_arr, 7)), pltpu.SemaphoreType.DMA((k_arr, 7)),
                        pltpu.SemaphoreType.DMA((k_arr,))],
        name=name)(*arrs)


_HBM = pl.BlockSpec(memory_space=pltpu.HBM)
_SEM = pl.BlockSpec(memory_space=pltpu.SEMAPHORE)
_EFFECT = pltpu.SideEffectType.DATAFLOW_SIDE_EFFECTING


def _peers(x, y, c):
    return [(x ^ (rel >> 2), y ^ ((rel >> 1) & 1), c ^ (rel & 1)) for rel in range(1, NDEV)]


def _exchange_copies(srcs, lands, send_sems, recv_sems, scatter):
    x, y, c = _place()
    me = 4 * x + 2 * y + c
    copies = []
    for r, (px, py, pc) in enumerate(_peers(x, y, c)):
        peer = 4 * px + 2 * py + pc
        for a in range(len(srcs)):
            copies.append(pltpu.make_async_remote_copy(
                src_ref=srcs[a].at[peer] if scatter else srcs[a], dst_ref=lands[a].at[me],
                send_sem=send_sems.at[7 * a + r], recv_sem=recv_sems.at[7 * a + r], device_id=(px, py, pc),
                device_id_type=MESH))
    return copies


def _exchange_start(arrs, *, scatter, name):
    k_arr = len(arrs)
    land_shapes = [a.shape if scatter else (NDEV,) + a.shape for a in arrs]
    lands = [pltpu.with_memory_space_constraint(lax.empty(s, a.dtype), pltpu.HBM) for s, a in zip(land_shapes, arrs)]
    srcs = [pltpu.with_memory_space_constraint(a, pltpu.HBM) for a in arrs]

    def body(*refs):
        src_refs, land_refs = refs[:k_arr], refs[k_arr:2 * k_arr]
        send_sems, recv_sems = refs[2 * k_arr], refs[2 * k_arr + 1]
        token = refs[-1]
        for cp in _exchange_copies(src_refs, land_refs, send_sems, recv_sems, scatter):
            cp.start()
        token[...] = jnp.zeros_like(token)

    out_shape = ([pltpu.SemaphoreType.DMA((7 * k_arr,)), pltpu.SemaphoreType.DMA((7 * k_arr,))]
                 + [pltpu.HBM(a.shape, a.dtype) for a in arrs] + [pltpu.HBM(s, a.dtype) for s, a in zip(land_shapes, arrs)]
                 + [jax.ShapeDtypeStruct((8, 128), F32)])
    res = pl.pallas_call(
        body, name=name, out_shape=out_shape, in_specs=[_HBM] * (2 * k_arr),
        out_specs=[_SEM, _SEM] + [_HBM] * (2 * k_arr) + [pl.BlockSpec(memory_space=pltpu.VMEM)],
        input_output_aliases={i: 2 + i for i in range(2 * k_arr)},
        compiler_params=pltpu.CompilerParams(has_side_effects=_EFFECT))(*srcs, *lands)
    return dict(send=res[0], recv=res[1], srcs=list(res[2:2 + k_arr]), lands=list(res[2 + k_arr:2 + 2 * k_arr]),
                token=res[-1], scatter=scatter)


def _exchange_wait(handle, after, *, name):
    k_arr = len(handle["srcs"])
    scatter = handle["scatter"]

    def body(*refs):
        src_refs, land_refs = refs[:k_arr], refs[k_arr:2 * k_arr]
        send_sems, recv_sems = refs[2 * k_arr], refs[2 * k_arr + 1]
        x, y, c = _place()
        me = 4 * x + 2 * y + c
        for r, (px, py, pc) in enumerate(_peers(x, y, c)):
            peer = 4 * px + 2 * py + pc
            for a in range(k_arr):
                cp = pltpu.make_async_remote_copy(
                    src_ref=src_refs[a].at[peer] if scatter else src_refs[a], dst_ref=land_refs[a].at[peer],
                    send_sem=send_sems.at[7 * a + r], recv_sem=recv_sems.at[7 * a + r], device_id=(x, y, c),
                    device_id_type=MESH)
                cp.wait_send()
                cp.wait_recv()

    arrs = handle["srcs"] + handle["lands"]
    res = pl.pallas_call(
        body, name=name, out_shape=[pltpu.HBM(a.shape, a.dtype) for a in arrs],
        in_specs=[_HBM] * (2 * k_arr) + [_SEM, _SEM, pl.BlockSpec(memory_space=pl.ANY)],
        out_specs=[_HBM] * (2 * k_arr), input_output_aliases={i: i for i in range(2 * k_arr)},
        compiler_params=pltpu.CompilerParams(has_side_effects=_EFFECT))(*arrs, handle["send"], handle["recv"], after)
    me = 4 * lax.axis_index("x") + 2 * lax.axis_index("y") + lax.axis_index("c")
    out = []
    for src, land in zip(res[:k_arr], res[k_arr:]):
        own = lax.dynamic_index_in_dim(src, me, 0, keepdims=False) if scatter else src
        out.append(lax.dynamic_update_index_in_dim(land, own, me, 0))
    return out


def _ffn_bwd(dxo, dz, xr, f, u_gc, hmid, gain, sc, w_up, cw, w_down, tag, gate_y=None, gate_g=None):
    d_wdown = _mm_tn((hmid, dz), name=f"ffn_down_dw_{tag}")
    dug, duv, dcw, dcb = _ffn_down_glu_bwd(dz, w_down, u_gc[0], u_gc[1], cw, name=f"ffn_down_glu_bwd_{tag}")
    d_wup = _mm_tn((dug, f), blocks=2, block=0, name=f"ffn_up_dwg_{tag}")
    d_wup = _mm_tn((duv, f), blocks=2, block=1, into=d_wup, name=f"ffn_up_dwv_{tag}")
    gated = gate_y is not None
    res = _mm_w_ep([dug, duv], w_up, _ep_norm_bwd(gated), [xr, dxo] + ([gate_y] if gated else []),
                   [gain, sc] + ([gate_g] if gated else []), [F32] + ([BF16] if gated else []),
                   [D] * (4 if gated else 3), name=f"ffn_up_dx_norm_bwd_{tag}")
    n_out = 2 if gated else 1
    return res[:n_out], res[n_out:], (d_wup, d_wdown, dcw, dcb)


def _split6(mod):
    return [mod[j * D:(j + 1) * D][None, :] for j in range(6)]


def _row(v):
    return v.reshape(1, -1)


def kernel(x, c, ctx, c_ctx, ada_w, ada_b, mix_norm, ffn_norm, even_w_in, even_q_gain, even_k_gain, even_conv_w, even_w_out, odd_pool_w, odd_pool_scale, ffn_w_up, ffn_conv_w, ffn_conv_b, ffn_w_down, loss_target, m_c_ctx, m_ada_w, m_ada_b, m_mix_norm, m_ffn_norm, m_even_w_in, m_even_q_gain, m_even_k_gain, m_even_conv_w, m_even_w_out, m_odd_pool_w, m_odd_pool_scale, m_ffn_w_up, m_ffn_conv_w, m_ffn_conv_b, m_ffn_w_down, v_c_ctx, v_ada_w, v_ada_b, v_mix_norm, v_ffn_norm, v_even_w_in, v_even_q_gain, v_even_k_gain, v_even_conv_w, v_even_w_out, v_odd_pool_w, v_odd_pool_scale, v_ffn_w_up, v_ffn_conv_w, v_ffn_conv_b, v_ffn_w_down):
    n = x.shape[1]
    lc = ctx.shape[1]
    me = 4 * lax.axis_index("x") + 2 * lax.axis_index("y") + lax.axis_index("c")
    xs, ctxs, tgt = x[0], ctx[0], loss_target[0]
    acols = ada_w.shape[2]

    small = jnp.concatenate([even_conv_w.reshape(-1), ffn_conv_w.reshape(-1), odd_pool_scale.reshape(-1)])
    nsmall = small.shape[0]
    small = jnp.pad(small, (0, (-nsmall) % 1024)).reshape(-1, 128)
    c_rows = jnp.pad(c, ((0, 7), (0, 0)))
    tr = lambda a: jnp.swapaxes(a, -1, -2)
    g_c, g_win, g_small = _all_gather([c_rows, tr(even_w_in[0]).astype(BF16), small], name="gather_first")
    w_in_t = g_win.reshape(-1, D)
    g_small = g_small.reshape(NDEV, -1)
    ecw = even_conv_w.shape[2]
    fcw = ffn_conv_w.shape[2]
    conv_w = g_small[:, :3 * ecw].reshape(NDEV, 3, ecw).transpose(1, 0, 2).reshape(3, CW)
    o1 = 3 * ecw
    fconv_w = g_small[:, o1:o1 + 6 * fcw].reshape(NDEV, 2, 3, fcw).transpose(1, 2, 0, 3).reshape(2, 3, DFF)
    o2 = o1 + 6 * fcw
    pool_scale = g_small[:, o2:o2 + D // NDEV].reshape(1, D)

    mraw = jnp.concatenate([g_c[:, 0, :], c_ctx[None, :], jnp.zeros((7, D), F32)], axis=0)
    my_bias = lax.dynamic_slice_in_dim(ada_b, me * acols, acols, axis=1)
    modp = jnp.stack([_mm(mraw, ada_w[l], silu_a=True, bias=my_bias[l:l + 1], name=f"ada_proj_{l}", tm=16, tn=256)
                      for l in range(2)])
    (g_mod,) = _all_gather([modp], name="gather_mod")
    mod_rows = g_mod.transpose(1, 2, 0, 3).reshape(2, 16, 6 * D)
    late_shards = [even_w_out[0].astype(BF16), odd_pool_w[0].astype(BF16), tr(ffn_w_up[0]).astype(BF16),
                   tr(ffn_w_up[1]).astype(BF16), ffn_w_down[0].astype(BF16), ffn_w_down[1].astype(BF16)]
    late_shards, mod_rows = lax.optimization_barrier((late_shards, mod_rows))
    h_weights = _exchange_start(late_shards, scatter=False, name="weights_start")
    mod_rows = mod_rows + h_weights["token"][0, 0]
    mod = lax.dynamic_index_in_dim(mod_rows, me, axis=1, keepdims=False)
    sh1, sc1, g1, sh2, sc2, g2 = _split6(mod[0])
    sh1b, sc1b, g1b, sh2b, sc2b, g2b = _split6(mod[1])
    csh1, csc1 = _split6(mod_rows[0, 8])[:2]
    mixn = [_row(mix_norm[l]) for l in range(2)]
    ffnn = [_row(ffn_norm[l]) for l in range(2)]
    qg, kg = _row(even_q_gain[0]), _row(even_k_gain[0])
    fcb = [_row(ffn_conv_b[l]) for l in range(2)]

    cs_t, sn_t = _rope_tables(n)
    a_lat = _norm_mod(xs, mixn[0], sc1, sh1, name="mix0_norm")
    a_ctx = _norm_mod(ctxs, mixn[0], csc1, csh1, name="mix0_norm_ctx")
    p_ctx = _mm(a_ctx, w_in_t[AW:AW + 4 * HD], tb=True, name="in_proj_ctx", tm=256, tn=512, tk=1024)
    kv_ctx = _qkv_prep(p_ctx, qg, kg, None, None, has_q=False, kv_col=0, kv_rows=lc + n, name="qkv_prep_ctx")
    p_lat, q_r, k_all, v_all, conv = _in_proj_qkv(a_lat, w_in_t, qg, kg, cs_t, sn_t, conv_w, kv_ctx, kv_row_off=lc,
                                                  name="in_proj_qkv")
    o_attn, lse = _attn_fwd(q_r, k_all, v_all, name="attn_fwd")
    g_wout, g_pool, g_up0, g_up1, g_down0, g_down1 = _exchange_wait(h_weights, o_attn, name="weights_wait")
    w_out = g_wout.reshape(D, D)
    pool_w = g_pool.transpose(1, 0, 2, 3).reshape(4, PG, PG)
    w_up_t = [g_up0.reshape(2 * DFF, D), g_up1.reshape(2 * DFF, D)]
    w_up = [w.T for w in w_up_t]
    w_down = [g_down0.reshape(DFF, D), g_down1.reshape(DFF, D)]
    y0, x1, f0 = _mm_w_ep([o_attn, conv], w_out, _ep_resid_norm, [xs], [g1, ffnn[0], sc2, sh2], [F32, F32, BF16], [],
                          tm=512, name="out_proj_norm")[:3]
    *u0, h0 = _ffn_up_glu(f0, w_up[0], fconv_w[0], fcb[0], name="ffn_up_glu_l0")
    z0, x2 = _mm_w_ep(h0, w_down[0], _ep_resid, [x1], [g2], [F32, F32], [], tm=512, name="ffn_down_resid_l0")[:2]

    mixed = _pool_fwd(x2, mixn[1], sc1b, sh1b, pool_w, name="pool_fwd")
    x3, f1 = _norm_mod(x2, ffnn[1], sc2b, sh2b, y=mixed, g=g1b, ymul=pool_scale, name="ffn_norm_l1")
    *u1, h1 = _ffn_up_glu(f1, w_up[1], fconv_w[1], fcb[1], name="ffn_up_glu_l1")
    dx4, dz1, loss_part, dg2b = _mm_w_ep(h1, w_down[1], _ep_loss(D), [x3, tgt], [g2b], [F32, BF16], [128, D],
                                         tm=512, name="ffn_down_loss")

    (dx3,), (dsh2b, dsc2b, dffn1), (dup1, ddown1, dfcw1, dfcb1) = _ffn_bwd(
        dx4, dz1, x3, f1, u1, h1, ffnn[1], sc2b, w_up_t[1], fconv_w[1], w_down[1], "l1")
    dx2, dpool_w, dg1b, dpscale, dsh1b, dsc1b, dmix1, dz0, dg2 = _pool_bwd(
        dx3, mixed, x2, g1b, pool_scale, mixn[1], sc1b, sh1b, pool_w, z0, g2, name="pool_bwd")

    s_pool = dpool_w.astype(BF16).reshape(4, NDEV, PG // NDEV, PG).transpose(1, 0, 2, 3)
    h_g1 = _exchange_start([s_pool, dup1.reshape(NDEV, -1, D), ddown1.reshape(NDEV, DFF // NDEV, D)], scatter=True,
                           name="grads1_start")

    (dx1, dy0), (dsh2, dsc2, dffn0, dg1), (dup0, ddown0, dfcw0, dfcb0) = _ffn_bwd(
        dx2, dz0, x1, f0, u0, h0, ffnn[0], sc2, w_up_t[0], fconv_w[0] + h_g1["token"][0, 0], w_down[0], "l0",
        gate_y=y0, gate_g=g1)
    h_g0 = _exchange_start([dup0.reshape(NDEV, -1, D), ddown0.reshape(NDEV, DFF // NDEV, D)], scatter=True,
                           name="grads0_start")
    d_attn, dp_conv, dconv_w = _out_proj_dx_conv_bwd(dy0, w_out, p_lat, conv_w + h_g0["token"][0, 0],
                                                     name="out_proj_dx_conv_bwd")
    d_wout = _mm_tn((o_attn, dy0), blocks=2, block=0, name="out_proj_dw_attn")
    d_wout = _mm_tn((conv, dy0), blocks=2, block=1, into=d_wout, name="out_proj_dw_conv")
    dq_r, dk_all, dv_all = _attn_bwd(q_r, k_all, v_all, d_attn, o_attn, lse, name="attn_bwd")
    dp_qkv, dqg_l, dkg_l = _qkv_bwd(p_lat, dq_r, dk_all, dv_all, qg, kg, cs_t, sn_t, has_q=True, kv_col=1,
                                    kv_row_off=lc, name="qkv_bwd")
    dp_ctx, _zero_qg, dkg_c = _qkv_bwd(p_ctx, None, dk_all, dv_all, qg, kg, None, None, has_q=False, kv_col=0,
                                       kv_row_off=0, name="qkv_bwd_ctx")
    da_ctx = _mm(dp_ctx, w_in_t[:D], name="in_proj_dx_ctx", tm=256, tn=512, tk=1024)
    d_win_qkv = _mm_tn([(dp_qkv, a_lat), (dp_ctx, a_ctx)], name="in_proj_dw_qkv")
    d_win_conv = _mm_tn((dp_conv, a_lat), name="in_proj_dw_conv")
    d_win_t = jnp.concatenate([d_win_qkv, d_win_conv], axis=0)
    grad_x, dsh1, dsc1, dmix0 = _mm_w_ep([dp_qkv, dp_conv], w_in_t, _ep_norm_bwd(False), [xs, dx1], [mixn[0], sc1],
                                         [F32], [D] * 3, tm=512, name="in_proj_dx_norm_bwd")
    _dctx, dcsh1, dcsc1, dmix0c = _norm_mod_bwd(da_ctx, ctxs, mixn[0], csc1, name="mix0_norm_bwd_ctx")

    z1k = jnp.zeros((1, D), F32)
    pack = jnp.concatenate(
        [v.reshape(-1) for v in (dsh1, dsc1, dg1, dsh2, dsc2, dg2, dsh1b, dsc1b, dg1b, dsh2b, dsc2b, dg2b,
                                 dcsh1, dcsc1, z1k, z1k, z1k, z1k,
                                 dmix0, dmix1, dmix0c, z1k, dffn0, dffn1, dqg_l, dkg_l + dkg_c,
                                 dfcb0, dfcb1, dconv_w, dfcw0, dfcw1, dpscale, loss_part[:, 0:1])])
    npack = pack.shape[0]
    pack = jnp.pad(pack, (0, (-npack) % 1024)).reshape(-1, 128)
    (g_pack,) = _all_gather([pack], name="gather_small_grads")
    def split(gp):
        off = [0]

        def take(size):
            seg = gp[:, off[0]:off[0] + size]
            off[0] += size
            return seg

        return (take(12 * D).reshape(NDEV, 2, 6 * D),
                take(6 * D).reshape(NDEV, 1, 6 * D),
                take(4 * D).reshape(NDEV, 2, 2, D),
                take(2 * D).reshape(NDEV, 2, D), take(HD).reshape(NDEV, 1, HD), take(HD).reshape(NDEV, 1, HD),
                take(2 * DFF).reshape(NDEV, 2, DFF), take(3 * CW).reshape(NDEV, 3, CW),
                take(6 * DFF).reshape(NDEV, 2, 3, DFF), take(D).reshape(NDEV, D), take(1))

    gp_all = g_pack.reshape(NDEV, -1)
    dmod_all, dmodc_all = split(gp_all)[:2]

    dmodc_sum = dmodc_all[0]
    for dev in range(1, NDEV):
        dmodc_sum = dmodc_sum + dmodc_all[dev]
    my_cols = lambda a: lax.dynamic_slice_in_dim(a, me * acols, acols, axis=a.ndim - 1)
    rows0 = jnp.concatenate([my_cols(dmod_all[:, 0]), my_cols(dmodc_sum), jnp.zeros((7, acols), F32)], axis=0)
    rows1 = jnp.concatenate([my_cols(dmod_all[:, 1]), jnp.zeros((8, acols), F32)], axis=0)
    dscc_part = _mm(rows0, ada_w[0], tb=True, name="ada_dcctx", tm=16, tn=512, tk=256)
    (g_dscc,) = _all_gather([dscc_part[8:16]], name="gather_dcctx")

    attn_shards = [d_win_t.reshape(NDEV, -1, D), d_wout.reshape(NDEV, D // NDEV, D)]
    attn_shards, g_dscc = lax.optimization_barrier((attn_shards, g_dscc))
    h_ga = _exchange_start(attn_shards, scatter=True, name="grads_attn_start")
    (dmod_all, dmodc_all, dmix_all, dffn_all, dqg_all, dkg_all, dfcb_all, dconvw_all, dfcw_all, dpscale_all,
     loss_all) = split(gp_all + h_ga["token"][0, 0])
    d_ada = jnp.stack([_mm(mraw, rows + h_ga["token"][0, 0], ta=True, silu_a=True, name=f"ada_dw_{l}", tm=512,
                           tn=256, tk=16) for l, rows in enumerate((rows0, rows1))])
    loss = loss_all[0, 0]
    for dev in range(1, NDEV):
        loss = loss + loss_all[dev, 0]

    outs = {}

    def put(nm, res):
        outs["grad_" + nm], outs["delta_" + nm], outs["new_m_" + nm], outs["new_v_" + nm] = res

    dmodc_pad = jnp.concatenate([dmodc_all, jnp.zeros_like(dmodc_all)], axis=1)
    put("ada_b", _adamw_nd(jnp.concatenate([dmod_all, dmodc_pad], axis=0), ada_b, m_ada_b, v_ada_b, name="adam_ada_b"))
    put("mix_norm", _adamw_nd(jnp.concatenate([dmix_all[:, 0], dmix_all[:, 1]], axis=0), mix_norm, m_mix_norm,
                              v_mix_norm, name="adam_mix_norm"))
    put("ffn_norm", _adamw_nd(dffn_all, ffn_norm, m_ffn_norm, v_ffn_norm, name="adam_ffn_norm"))
    put("even_q_gain", _adamw_nd(dqg_all, even_q_gain, m_even_q_gain, v_even_q_gain, name="adam_q_gain"))
    put("even_k_gain", _adamw_nd(dkg_all, even_k_gain, m_even_k_gain, v_even_k_gain, name="adam_k_gain"))
    put("ffn_conv_b", _adamw_nd(dfcb_all, ffn_conv_b, m_ffn_conv_b, v_ffn_conv_b, name="adam_ffn_conv_b"))
    my_convw = lax.dynamic_slice_in_dim(dconvw_all, me * ecw, ecw, axis=2)[:, None]
    put("even_conv_w", _adamw_nd(my_convw, even_conv_w, m_even_conv_w, v_even_conv_w, name="adam_even_conv_w"))
    my_fcw = lax.dynamic_slice_in_dim(dfcw_all, me * fcw, fcw, axis=3)
    put("ffn_conv_w", _adamw_nd(my_fcw, ffn_conv_w, m_ffn_conv_w, v_ffn_conv_w, name="adam_ffn_conv_w"))
    my_ps = lax.dynamic_slice_in_dim(dpscale_all, me * (D // NDEV), D // NDEV, axis=1)[:, None]
    put("odd_pool_scale", _adamw_nd(my_ps, odd_pool_scale, m_odd_pool_scale, v_odd_pool_scale, name="adam_pool_scale"))

    put("ada_w", _adamw_nd(d_ada[None], ada_w, m_ada_w, v_ada_w, name="adam_ada_w"))
    put("c_ctx", _adamw_nd(g_dscc[:, 0:1, :].reshape(NDEV, D), c_ctx, m_c_ctx, v_c_ctx, name="adam_c_ctx",
                           silu_grad_of=c_ctx))

    r_pool, r_up1, r_down1 = _exchange_wait(h_g1, outs["grad_ada_b"], name="grads1_wait")
    r_up0, r_down0 = _exchange_wait(h_g0, outs["grad_mix_norm"], name="grads0_wait")
    put("odd_pool_w", _adamw_nd(r_pool[:, None], odd_pool_w, m_odd_pool_w, v_odd_pool_w, name="adam_pool_w"))
    put("ffn_w_up", [tr(a) for a in _adamw_nd([r_up0, r_up1], tr(ffn_w_up), tr(m_ffn_w_up), tr(v_ffn_w_up),
                                              name="adam_w_up")])
    put("ffn_w_down", _adamw_nd([r_down0, r_down1], ffn_w_down, m_ffn_w_down, v_ffn_w_down, name="adam_w_down"))
    r_win, r_wout = _exchange_wait(h_ga, outs["grad_ffn_w_down"], name="grads_attn_wait")
    put("even_w_in", [tr(a) for a in _adamw_nd(r_win[:, None], tr(even_w_in), tr(m_even_w_in), tr(v_even_w_in),
                                               name="adam_w_in")])
    put("even_w_out", _adamw_nd(r_wout[:, None], even_w_out, m_even_w_out, v_even_w_out, name="adam_w_out"))

    names = ["c_ctx", "ada_w", "ada_b", "mix_norm", "ffn_norm", "even_w_in", "even_q_gain", "even_k_gain",
             "even_conv_w", "even_w_out", "odd_pool_w", "odd_pool_scale", "ffn_w_up", "ffn_conv_w", "ffn_conv_b",
             "ffn_w_down"]
    result = [loss, grad_x[None]]
    for kind in ("grad_", "delta_", "new_m_", "new_v_"):
        result += [outs[kind + nm] for nm in names]
    return tuple(result)
```
